```python
import math
import jax, jax.numpy as jnp
from jax import lax
import numpy as np

D_MODEL = 1024
BATCH = 1
SEQ = 16384
DEPTH = 4

N_A = DEPTH // 2
PLE_DIM = 256
NORM_EPS = 1e-6
L2_EPS = 1e-6
GDN_QK_HEADS = 8
GDN_V_HEADS = 16
GDN_HEAD_DIM = 128
GDN_CONV = 4
GDN_CHUNK = 64
GDN_QK_W = GDN_QK_HEADS * GDN_HEAD_DIM
GDN_V_W = GDN_V_HEADS * GDN_HEAD_DIM
GDN_CONV_W = 2 * GDN_QK_W + GDN_V_W
GDN_IN_W = GDN_CONV_W + GDN_V_W + 2 * GDN_V_HEADS
NSA_HEADS = 16
NSA_GROUPS = 4
NSA_HEAD_DIM = 64
NSA_REP = NSA_HEADS // NSA_GROUPS
CMP_BLOCK = 32
CMP_STRIDE = 16
CMP_HIDDEN = 256
SEL_BLOCK = 64
SEL_TOPK = 16
WINDOW = 512
Q_BLOCK = 128
FORCED_SCORE = 1e4
NSA_Q_W = NSA_HEADS * NSA_HEAD_DIM
NSA_QG_W = NSA_Q_W + 3 * NSA_HEADS
NSA_KV_W = 6 * NSA_GROUPS * NSA_HEAD_DIM
FFN_HIDDEN = ((8 * D_MODEL + 3 * 256 - 1) // (3 * 256)) * 256

kernel_name = "yoco_gdn_nsa_hybrid"


def rms_norm(x, g):
    xf = x.astype(jnp.float32)
    y = xf * lax.rsqrt(jnp.mean(xf * xf, axis=-1, keepdims=True) + NORM_EPS)
    return (y * g.astype(jnp.float32)).astype(x.dtype)


def l2_normalize(x):
    xf = x.astype(jnp.float32)
    return xf * lax.rsqrt(jnp.sum(xf * xf, axis=-1, keepdims=True) + L2_EPS)


def causal_short_conv(x, w):
    k_w, t = w.shape[0], x.shape[1]
    xp = jnp.pad(x, ((0, 0), (k_w - 1, 0), (0, 0)))
    y = xp[:, 0:t] * w[0]
    for j in range(1, k_w):
        y = y + xp[:, j:j + t] * w[j]
    return jax.nn.silu(y)


def chunk_gated_delta_rule(q, k, v, g, beta):
    f32 = jnp.float32
    b, t, h, dk = q.shape
    dv = v.shape[-1]
    c = GDN_CHUNK
    n = t // c

    def chunks(a):
        a = a.astype(f32).reshape((b, n, c, h) + a.shape[3:])
        return jnp.moveaxis(a, 3, 1)

    qc = chunks(q) * (dk ** -0.5)
    kc, vc, gc, bc = chunks(k), chunks(v), chunks(g), chunks(beta)
    gcum = jnp.cumsum(gc, axis=-1)
    causal = jnp.tril(jnp.ones((c, c), dtype=bool))
    strict = jnp.tril(jnp.ones((c, c), dtype=bool), k=-1)
    decay = jnp.exp(jnp.where(causal, gcum[..., :, None] - gcum[..., None, :], -jnp.inf))
    kb = kc * bc[..., None]
    a_mat = jnp.where(strict, jnp.einsum('bhncd,bhnsd->bhncs', kb, kc) * decay, 0.0) + jnp.eye(c, dtype=f32)
    rhs = jnp.concatenate([vc * bc[..., None], kb * jnp.exp(gcum)[..., None]], axis=-1)
    sol = lax.linalg.triangular_solve(a_mat, rhs, left_side=True, lower=True, unit_diagonal=True)
    u, w = sol[..., :dv], sol[..., dv:]
    qk = jnp.einsum('bhncd,bhnsd->bhncs', qc, kc) * decay
    q_dec = qc * jnp.exp(gcum)[..., None]
    g_last = gcum[..., -1]
    k_dec = kc * jnp.exp(g_last[..., None] - gcum)[..., None]

    def step(state, xs):
        u_i, w_i, qk_i, qd_i, kd_i, gl_i = xs
        v_new = u_i - jnp.einsum('bhcd,bhde->bhce', w_i, state)
        o_i = jnp.einsum('bhcd,bhde->bhce', qd_i, state) + jnp.einsum('bhcs,bhse->bhce', qk_i, v_new)
        state = state * jnp.exp(gl_i)[..., None, None] + jnp.einsum('bhcd,bhce->bhde', kd_i, v_new)
        return state, o_i

    xs = tuple(jnp.moveaxis(a, 2, 0) for a in (u, w, qk, q_dec, k_dec, g_last))
    s0 = jnp.zeros((b, h, dk, dv), f32)
    _, o = lax.scan(step, s0, xs)
    return jnp.transpose(o, (1, 0, 3, 2, 4)).reshape(b, t, h, dv)


def gated_deltanet(h, w_in, conv_w, a_log, dt_bias, o_norm, w_out):
    f32 = jnp.float32
    b, t, _ = h.shape
    proj = h @ w_in
    qkv, z, beta_logit, a = jnp.split(
        proj, [GDN_CONV_W, GDN_CONV_W + GDN_V_W, GDN_CONV_W + GDN_V_W + GDN_V_HEADS], axis=-1)
    qkv = causal_short_conv(qkv, conv_w)
    q, k, v = jnp.split(qkv, [GDN_QK_W, 2 * GDN_QK_W], axis=-1)
    rep = GDN_V_HEADS // GDN_QK_HEADS
    q = jnp.repeat(l2_normalize(q.reshape(b, t, GDN_QK_HEADS, GDN_HEAD_DIM)), rep, axis=2)
    k = jnp.repeat(l2_normalize(k.reshape(b, t, GDN_QK_HEADS, GDN_HEAD_DIM)), rep, axis=2)
    v = v.reshape(b, t, GDN_V_HEADS, GDN_HEAD_DIM)
    beta = jax.nn.sigmoid(beta_logit.astype(f32))
    g = -jnp.exp(a_log.astype(f32)) * jax.nn.softplus(a.astype(f32) + dt_bias.astype(f32))
    o = chunk_gated_delta_rule(q, k, v, g, beta)
    o = rms_norm(o, o_norm) * jax.nn.silu(z.reshape(b, t, GDN_V_HEADS, GDN_HEAD_DIM).astype(f32))
    return o.reshape(b, t, GDN_V_W).astype(h.dtype) @ w_out


def compress_blocks(x_raw, pos, w1, w2):
    b, g, t, d = x_raw.shape
    n_c = (t - CMP_BLOCK) // CMP_STRIDE + 1
    idx = CMP_STRIDE * jnp.arange(n_c)[:, None] + jnp.arange(CMP_BLOCK)[None, :]
    blocks = x_raw[:, :, idx] + pos
    flat = blocks.reshape(b, g, n_c, CMP_BLOCK * d)
    return jax.nn.silu(flat @ w1) @ w2


def nsa_shared_kv(h, kv_norm, kv_w, cmp_pos, cmp_w1, cmp_w2):
    b, t, _ = h.shape
    kv = rms_norm(h, kv_norm) @ kv_w
    kv = jnp.transpose(kv.reshape(b, t, 6, NSA_GROUPS, NSA_HEAD_DIM), (2, 0, 3, 1, 4))
    k_c = compress_blocks(kv[0], cmp_pos[0], cmp_w1[0], cmp_w2[0])
    v_c = compress_blocks(kv[1], cmp_pos[1], cmp_w1[1], cmp_w2[1])
    return k_c, v_c, kv[2], kv[3], kv[4], kv[5]


def masked_softmax(s, mask):
    s = jnp.where(mask, s.astype(jnp.float32), -jnp.inf)
    m = jnp.max(s, axis=-1, keepdims=True)
    m = jnp.where(jnp.isfinite(m), m, 0.0)
    e = jnp.where(mask, jnp.exp(s - m), 0.0)
    return e / jnp.maximum(jnp.sum(e, axis=-1, keepdims=True), 1e-30)


def selection_importance(p_cmp, n_sel):
    r = SEL_BLOCK // CMP_STRIDE
    c = CMP_BLOCK // CMP_STRIDE
    n_c = p_cmp.shape[-1]
    pp = jnp.pad(p_cmp, [(0, 0)] * (p_cmp.ndim - 1) + [(c - 1, r * n_sel - n_c)])
    out = jnp.zeros(p_cmp.shape[:-1] + (n_sel,), p_cmp.dtype)
    for m in range(r):
        for n in range(c):
            off = m - n + c - 1
            out = out + pp[..., off:off + r * n_sel:r]
    return out


def nsa_attention(h, w_qg, w_o, k_cmp, v_cmp, k_slc, v_slc, k_win, v_win):
    b, t, _ = h.shape
    G, R, Dh = NSA_GROUPS, NSA_REP, NSA_HEAD_DIM
    qg = h @ w_qg
    q = qg[..., :NSA_Q_W].reshape(b, t, G, R, Dh) * (Dh ** -0.5)
    gates = jax.nn.sigmoid(qg[..., NSA_Q_W:].astype(jnp.float32)).reshape(b, t, G, R, 3)
    n_c = k_cmp.shape[2]
    n_sel = t // SEL_BLOCK
    topk = min(SEL_TOPK, n_sel)
    cmp_end = CMP_STRIDE * jnp.arange(n_c) + CMP_BLOCK - 1
    blk = jnp.arange(n_sel)
    k_sel_blocks = k_slc.reshape(b, G, n_sel, SEL_BLOCK, Dh)
    v_sel_blocks = v_slc.reshape(b, G, n_sel, SEL_BLOCK, Dh)
    k_win_p = jnp.pad(k_win, ((0, 0), (0, 0), (WINDOW, 0), (0, 0)))
    v_win_p = jnp.pad(v_win, ((0, 0), (0, 0), (WINDOW, 0), (0, 0)))
    b_idx = jnp.arange(b)[:, None, None, None]
    g_idx = jnp.arange(G)[None, :, None, None]

    def query_block(i):
        s = i * Q_BLOCK
        qb = lax.dynamic_slice_in_dim(q, s, Q_BLOCK, axis=1)
        gb = lax.dynamic_slice_in_dim(gates, s, Q_BLOCK, axis=1)
        tq = s + jnp.arange(Q_BLOCK)
        sc = jnp.einsum('bqgrd,bgnd->bgrqn', qb, k_cmp)
        p_c = masked_softmax(sc, cmp_end[None, :] <= tq[:, None])
        o_c = jnp.einsum('bgrqn,bgnd->bqgrd', p_c, v_cmp)
        imp = selection_importance(jnp.sum(p_c, axis=2), n_sel)
        cur = tq // SEL_BLOCK
        forced = (blk[None, :] == 0) | (blk[None, :] == cur[:, None]) | (blk[None, :] == cur[:, None] - 1)
        valid = blk[None, :] * SEL_BLOCK <= tq[:, None]
        score = jnp.where(valid, jnp.where(forced, FORCED_SCORE, imp), -jnp.inf)
        _, sel = lax.top_k(score, topk)
        ks = k_sel_blocks[b_idx, g_idx, sel]
        vs = v_sel_blocks[b_idx, g_idx, sel]
        key_pos = sel[..., None] * SEL_BLOCK + jnp.arange(SEL_BLOCK)
        m_s = (key_pos <= tq[None, None, :, None, None]).reshape(b, G, 1, Q_BLOCK, topk * SEL_BLOCK)
        ss = jnp.einsum('bqgrd,bgqkld->bgrqkl', qb, ks).reshape(b, G, R, Q_BLOCK, topk * SEL_BLOCK)
        p_s = masked_softmax(ss, m_s).reshape(b, G, R, Q_BLOCK, topk, SEL_BLOCK)
        o_s = jnp.einsum('bgrqkl,bgqkld->bqgrd', p_s, vs)
        kw = lax.dynamic_slice_in_dim(k_win_p, s, Q_BLOCK + WINDOW, axis=2)
        vw = lax.dynamic_slice_in_dim(v_win_p, s, Q_BLOCK + WINDOW, axis=2)
        wpos = s - WINDOW + jnp.arange(Q_BLOCK + WINDOW)
        m_w = (wpos[None, :] >= 0) & (wpos[None, :] <= tq[:, None]) & (wpos[None, :] > tq[:, None] - WINDOW)
        sw = jnp.einsum('bqgrd,bgkd->bgrqk', qb, kw)
        o_w = jnp.einsum('bgrqk,bgkd->bqgrd', masked_softmax(sw, m_w), vw)
        return gb[..., 0:1] * o_c + gb[..., 1:2] * o_s + gb[..., 2:3] * o_w

    o = lax.map(query_block, jnp.arange(t // Q_BLOCK))
    o = jnp.moveaxis(o, 0, 1).reshape(b, t, NSA_Q_W).astype(h.dtype)
    return o @ w_o


def swiglu(h, w_in, w_out):
    gate, up = jnp.split(h @ w_in, 2, axis=-1)
    return (jax.nn.silu(gate) * up) @ w_out


def setup_inputs(seed: int = 0) -> dict:
    key = jax.random.key(seed)
    ks = jax.random.split(key, 24)
    f32 = jnp.float32
    n_b = DEPTH - N_A

    def nrm(k, shape, fan_in):
        return jax.random.normal(k, shape, f32) * (fan_in ** -0.5)

    def gain(k, shape):
        return 1.0 + 0.05 * jax.random.normal(k, shape, f32)

    dt = jnp.exp(jax.random.uniform(ks[10], (N_A, GDN_V_HEADS), f32, math.log(1e-3), math.log(1e-1)))
    return {
        "x": jax.random.normal(ks[0], (BATCH, SEQ, D_MODEL), f32),
        "p": jax.random.normal(ks[1], (DEPTH, BATCH, SEQ, PLE_DIM), f32),
        "mix_pre_norm": gain(ks[2], (DEPTH, D_MODEL)),
        "mix_post_norm": gain(ks[3], (DEPTH, D_MODEL)),
        "ffn_pre_norm": gain(ks[4], (DEPTH, D_MODEL)),
        "ffn_post_norm": gain(ks[5], (DEPTH, D_MODEL)),
        "gdn_w_in": nrm(ks[6], (N_A, D_MODEL, GDN_IN_W), D_MODEL),
        "gdn_conv_w": nrm(ks[7], (N_A, GDN_CONV, GDN_CONV_W), GDN_CONV),
        "gdn_a_log": jnp.log(jax.random.uniform(ks[8], (N_A, GDN_V_HEADS), f32, 1.0, 16.0)),
        "gdn_dt_bias": dt + jnp.log(-jnp.expm1(-dt)),
        "gdn_o_norm": gain(ks[9], (N_A, GDN_HEAD_DIM)),
        "gdn_w_out": nrm(ks[11], (N_A, GDN_V_W, D_MODEL), GDN_V_W),
        "kv_norm": gain(ks[12], (D_MODEL,)),
        "kv_w": nrm(ks[13], (D_MODEL, NSA_KV_W), D_MODEL),
        "cmp_pos": 0.1 * jax.random.normal(ks[14], (2, CMP_BLOCK, NSA_HEAD_DIM), f32),
        "cmp_w1": nrm(ks[15], (2, CMP_BLOCK * NSA_HEAD_DIM, CMP_HIDDEN), CMP_BLOCK * NSA_HEAD_DIM),
        "cmp_w2": nrm(ks[16], (2, CMP_HIDDEN, NSA_HEAD_DIM), CMP_HIDDEN),
        "nsa_w_qg": nrm(ks[17], (n_b, D_MODEL, NSA_QG_W), D_MODEL),
        "nsa_w_o": nrm(ks[18], (n_b, NSA_Q_W, D_MODEL), NSA_Q_W),
        "ffn_w_in": nrm(ks[19], (DEPTH, D_MODEL, 2 * FFN_HIDDEN), D_MODEL),
        "ffn_w_out": nrm(ks[20], (DEPTH, FFN_HIDDEN, D_MODEL), FFN_HIDDEN),
        "ple_w_in": nrm(ks[21], (DEPTH, PLE_DIM, D_MODEL), PLE_DIM),
        "ple_w_gate": nrm(ks[22], (DEPTH, D_MODEL, D_MODEL), D_MODEL),
    }


def reference(x, p, mix_pre_norm, mix_post_norm, ffn_pre_norm, ffn_post_norm,
              gdn_w_in, gdn_conv_w, gdn_a_log, gdn_dt_bias, gdn_o_norm, gdn_w_out,
              kv_norm, kv_w, cmp_pos, cmp_w1, cmp_w2, nsa_w_qg, nsa_w_o,
              ffn_w_in, ffn_w_out, ple_w_in, ple_w_gate):
    def channel_and_ple(h, i):
        f = swiglu(rms_norm(h, ffn_pre_norm[i]), ffn_w_in[i], ffn_w_out[i])
        h = h + rms_norm(f, ffn_post_norm[i])
        return h + (p[i] @ ple_w_in[i]) * jax.nn.sigmoid(h @ ple_w_gate[i])

    h = x
    for i in range(N_A):
        mix = gated_deltanet(rms_norm(h, mix_pre_norm[i]), gdn_w_in[i], gdn_conv_w[i], gdn_a_log[i],
                             gdn_dt_bias[i], gdn_o_norm[i], gdn_w_out[i])
        h = h + rms_norm(mix, mix_post_norm[i])
        h = channel_and_ple(h, i)
    k_cmp, v_cmp, k_slc, v_slc, k_win, v_win = nsa_shared_kv(h, kv_norm, kv_w, cmp_pos, cmp_w1, cmp_w2)
    for i in range(N_A, DEPTH):
        j = i - N_A
        mix = nsa_attention(rms_norm(h, mix_pre_norm[i]), nsa_w_qg[j], nsa_w_o[j],
                            k_cmp, v_cmp, k_slc, v_slc, k_win, v_win)
        h = h + rms_norm(mix, mix_post_norm[i])
        h = channel_and_ple(h, i)
    return h
```

```python
import functools

import numpy as np
import jax
import jax.numpy as jnp
from jax import lax
from jax.experimental import pallas as pl
from jax.experimental.pallas import tpu as pltpu

F32 = jnp.float32
BF16 = jnp.bfloat16

NORM_EPS = 1e-6
L2_EPS = 1e-6
GDN_QK_HEADS = 8
GDN_V_HEADS = 16
GDN_HEAD_DIM = 128
GDN_CONV = 4
GDN_CHUNK = 64
GDN_GROUP = 256
NSA_GROUPS = 4
NSA_REP = 4
NSA_HEAD_DIM = 64
NSA_GW = NSA_REP * NSA_HEAD_DIM
CMP_BLOCK = 32
CMP_STRIDE = 16
SEL_BLOCK = 64
SEL_TOPK = 16
WINDOW = 512
FORCED_SCORE = 1e4
LANES = 128
NEG_INIT = -1e30

VMEM_LIMIT = 56 * 1024 * 1024


def _cparams(sem):
    return pltpu.CompilerParams(dimension_semantics=sem, vmem_limit_bytes=VMEM_LIMIT)


def _rms(x, gain):
    return x * lax.rsqrt(jnp.mean(x * x, axis=-1, keepdims=True) + NORM_EPS) * gain


def _mm(a, b):
    return jnp.dot(a.astype(BF16), b.astype(BF16), preferred_element_type=F32)


def _sigmoid(x):
    return 1.0 / (1.0 + jnp.exp(-x))


def _div_pow2(x, d):
    shift = d.bit_length() - 1
    assert d == 1 << shift
    return jnp.right_shift(x, shift)


def _split3(x):
    a = x.astype(BF16)
    r = x - a.astype(F32)
    b = r.astype(BF16)
    c = (r - b.astype(F32)).astype(BF16)
    return a, b, c


def _norm_matmul_kernel(x_ref, g_ref, w_ref, o_ref, xn_ref):
    @pl.when(pl.program_id(1) == 0)
    def _():
        xn_ref[...] = _rms(x_ref[...], g_ref[...]).astype(BF16)

    o_ref[...] = jnp.dot(xn_ref[...], w_ref[...], preferred_element_type=F32)


def norm_matmul(h, gain, w, tm=512, tn=1024):
    t, d = h.shape
    n = w.shape[1]
    tn = min(tn, n)
    assert t % tm == 0 and n % tn == 0
    return pl.pallas_call(
        _norm_matmul_kernel,
        grid=(t // tm, n // tn),
        in_specs=[
            pl.BlockSpec((tm, d), lambda i, j: (i, 0)),
            pl.BlockSpec((1, d), lambda i, j: (0, 0)),
            pl.BlockSpec((d, tn), lambda i, j: (0, j)),
        ],
        out_specs=pl.BlockSpec((tm, tn), lambda i, j: (i, j)),
        out_shape=jax.ShapeDtypeStruct((t, n), F32),
        scratch_shapes=[pltpu.VMEM((tm, d), BF16)],
        compiler_params=_cparams(("parallel", "arbitrary")),
        name="norm_matmul",
    )(h, gain.reshape(1, d), w)


def _ffn_ple_kernel(h_ref, gpre_ref, wg_ref, wu_ref, wo_ref, gpost_ref, p_ref, wple_ref, wgt_ref,
                    o_ref, xn_ref, acc_ref, *, nf):
    f = pl.program_id(1)

    @pl.when(f == 0)
    def _():
        xn_ref[...] = _rms(h_ref[...], gpre_ref[...]).astype(BF16)
        acc_ref[...] = jnp.zeros_like(acc_ref)

    xn = xn_ref[...]
    gate = jnp.dot(xn, wg_ref[...], preferred_element_type=F32)
    up = jnp.dot(xn, wu_ref[...], preferred_element_type=F32)
    act = gate * _sigmoid(gate) * up
    acc_ref[...] += jnp.dot(act.astype(BF16), wo_ref[...], preferred_element_type=F32)

    @pl.when(f == nf - 1)
    def _():
        h2 = h_ref[...] + _rms(acc_ref[...], gpost_ref[...])
        emb = jnp.dot(p_ref[...].astype(BF16), wple_ref[...], preferred_element_type=F32)
        gt = _sigmoid(jnp.dot(h2.astype(BF16), wgt_ref[...], preferred_element_type=F32))
        o_ref[...] = h2 + emb * gt


def ffn_ple(h, gpre, w_gate, w_up, w_out, gpost, p, w_ple, w_plegate, tm=512, tf=1408):
    t, d = h.shape
    fh = w_gate.shape[1]
    pd = p.shape[1]
    assert t % tm == 0 and fh % tf == 0
    nf = fh // tf
    return pl.pallas_call(
        functools.partial(_ffn_ple_kernel, nf=nf),
        grid=(t // tm, nf),
        in_specs=[
            pl.BlockSpec((tm, d), lambda i, f: (i, 0)),
            pl.BlockSpec((1, d), lambda i, f: (0, 0)),
            pl.BlockSpec((d, tf), lambda i, f: (0, f)),
            pl.BlockSpec((d, tf), lambda i, f: (0, f)),
            pl.BlockSpec((tf, d), lambda i, f: (f, 0)),
            pl.BlockSpec((1, d), lambda i, f: (0, 0)),
            pl.BlockSpec((tm, pd), lambda i, f: (i, 0)),
            pl.BlockSpec((pd, d), lambda i, f: (0, 0)),
            pl.BlockSpec((d, d), lambda i, f: (0, 0)),
        ],
        out_specs=pl.BlockSpec((tm, d), lambda i, f: (i, 0)),
        out_shape=jax.ShapeDtypeStruct((t, d), F32),
        scratch_shapes=[pltpu.VMEM((tm, d), BF16), pltpu.VMEM((tm, d), F32)],
        compiler_params=_cparams(("parallel", "arbitrary")),
        name="ffn_ple",
    )(h, gpre.reshape(1, d), w_gate, w_up, w_out, gpost.reshape(1, d), p, w_ple, w_plegate)


def _gdn_conv_kernel(x_ref, halo_ref, w_ref, o_ref, *, tm):
    c = pl.program_id(0)
    i = pl.program_id(1)
    x = x_ref[...]
    halo = jnp.where(i > 0, halo_ref[...], 0.0)
    ext = jnp.concatenate([halo, x], axis=0)
    w = w_ref[...]
    y = x * w[GDN_CONV - 1:GDN_CONV, :]
    for k in range(1, GDN_CONV):
        shifted = pltpu.roll(ext, k, axis=0)[8:8 + tm]
        y = y + shifted * w[GDN_CONV - 1 - k:GDN_CONV - k, :]
    y = y * _sigmoid(y)
    normed = y * lax.rsqrt(jnp.sum(y * y, axis=-1, keepdims=True) + L2_EPS)
    q_scale = jnp.where(c < GDN_QK_HEADS, GDN_HEAD_DIM ** -0.5, 1.0)
    o_ref[...] = jnp.where(c < 2 * GDN_QK_HEADS, normed * q_scale, y)


def gdn_conv(proj, conv_w, tm=1024):
    t = proj.shape[0]
    n_tiles = conv_w.shape[1] // LANES
    assert t % tm == 0
    return pl.pallas_call(
        functools.partial(_gdn_conv_kernel, tm=tm),
        grid=(n_tiles, t // tm),
        in_specs=[
            pl.BlockSpec((tm, LANES), lambda c, i: (i, c)),
            pl.BlockSpec((8, LANES), lambda c, i: (jnp.maximum(i * (tm // 8) - 1, 0), c)),
            pl.BlockSpec((GDN_CONV, LANES), lambda c, i: (0, c)),
        ],
        out_specs=pl.BlockSpec((None, tm, LANES), lambda c, i: (c, i, 0)),
        out_shape=jax.ShapeDtypeStruct((n_tiles, t, LANES), F32),
        compiler_params=_cparams(("parallel", "parallel")),
        name="gdn_conv",
    )(proj, proj, conv_w)


def _gdn_gate_kernel(x_ref, alog_ref, dtb_ref, lc_ref, lf_ref, o_ref):
    x = x_ref[...]
    lane = lax.broadcasted_iota(jnp.int32, x.shape, 1)
    beta = _sigmoid(x)
    z = x + dtb_ref[...]
    softplus = jnp.maximum(z, 0.0) + jnp.log(1.0 + jnp.exp(-jnp.abs(z)))
    g = -jnp.exp(alog_ref[...]) * softplus
    gcum = jnp.zeros_like(x)
    gtot = jnp.zeros_like(x)
    for piece in _split3(g):
        gcum = gcum + jnp.dot(lc_ref[...], piece, preferred_element_type=F32)
        gtot = gtot + jnp.dot(lf_ref[...], piece, preferred_element_type=F32)
    o_ref[...] = jnp.where(lane < GDN_V_HEADS, beta, jnp.where(lane < 2 * GDN_V_HEADS, gcum, gtot))


def gdn_gates(ba, alog_vec, dtb_vec):
    t = ba.shape[0]
    tm = GDN_GROUP
    r = np.arange(tm)
    same = (r[:, None] // GDN_CHUNK) == (r[None, :] // GDN_CHUNK)
    lc = jnp.asarray(same & (r[None, :] <= r[:, None]), BF16)
    lf = jnp.asarray(same, BF16)
    return pl.pallas_call(
        _gdn_gate_kernel,
        grid=(t // tm,),
        in_specs=[
            pl.BlockSpec((tm, LANES), lambda i: (i, 0)),
            pl.BlockSpec((1, LANES), lambda i: (0, 0)),
            pl.BlockSpec((1, LANES), lambda i: (0, 0)),
            pl.BlockSpec((tm, tm), lambda i: (0, 0)),
            pl.BlockSpec((tm, tm), lambda i: (0, 0)),
        ],
        out_specs=pl.BlockSpec((tm, LANES), lambda i: (i, 0)),
        out_shape=jax.ShapeDtypeStruct((t, LANES), F32),
        compiler_params=_cparams(("parallel",)),
        name="gdn_gates",
    )(ba, alog_vec, dtb_vec, lc, lf)


def _gdn_scan_kernel(q_ref, k_ref, kt_ref, v0_ref, v1_ref, slab_ref, slabt_ref, o_ref, s_ref):
    j = pl.program_id(0)
    n = pl.program_id(1)
    L = GDN_GROUP
    C = GDN_CHUNK
    D = GDN_HEAD_DIM

    @pl.when(n == 0)
    def _():
        s_ref[...] = jnp.zeros_like(s_ref)

    q = q_ref[...]
    k = k_ref[...]
    kt = kt_ref[...]
    kk = _mm(k, kt)
    qk = _mm(q, kt)
    slab = slab_ref[...]
    slabt = slabt_ref[...]
    row = lax.broadcasted_iota(jnp.int32, (L, L), 0)
    col = lax.broadcasted_iota(jnp.int32, (L, L), 1)
    same = _div_pow2(row, C) == _div_pow2(col, C)
    causal = same & (col <= row)
    strict = same & (col < row)
    eye = (row == col).astype(F32)
    lane = lax.broadcasted_iota(jnp.int32, (L, LANES), 1)
    sub = lax.broadcasted_iota(jnp.int32, (LANES, L), 0)

    def column(idx):
        return jnp.sum(jnp.where(lane == idx, slab, 0.0), axis=1, keepdims=True)

    def rowvec(idx):
        return jnp.sum(jnp.where(sub == idx, slabt, 0.0), axis=0, keepdims=True)

    for e, v_ref in enumerate((v0_ref, v1_ref)):
        hv = 2 * j + e
        beta_c = column(hv)
        gc_c = column(GDN_V_HEADS + hv)
        gt_c = column(2 * GDN_V_HEADS + hv)
        gc_r = rowvec(GDN_V_HEADS + hv)
        gt_r = rowvec(2 * GDN_V_HEADS + hv)
        decay = jnp.where(causal, jnp.exp(jnp.where(causal, gc_c - gc_r, 0.0)), 0.0)
        b = jnp.where(strict, -(kk * beta_c) * decay, 0.0)
        inv = eye + b
        bp = b
        for _ in range(5):
            bp = _mm(bp, bp)
            inv = inv + _mm(inv, bp)
        v = v_ref[...]
        egc = jnp.exp(gc_c)
        rhs = jnp.concatenate([v * beta_c, k * (beta_c * egc)], axis=1)
        sol = _mm(inv, rhs)
        u = sol[:, :D]
        w = sol[:, D:]
        qkm = jnp.where(causal, qk * decay, 0.0)
        q_dec = q * egc
        kdt = kt * jnp.exp(gt_r - gc_r)
        state = s_ref[e]
        colk = lax.broadcasted_iota(jnp.int32, (D, L), 1)
        v_done = []
        for c in range(L // C):
            lo, hi = c * C, (c + 1) * C
            r = _mm(jnp.concatenate([w[lo:hi], q_dec[lo:hi]], axis=0), state)
            v_done.append(u[lo:hi] - r[:C])
            v_all = jnp.concatenate(v_done + [jnp.zeros((L - hi, D), F32)] * (hi < L), axis=0)
            o_c = r[C:] + _mm(qkm[lo:hi, :], v_all)
            kdt_c = jnp.where((colk >= lo) & (colk < hi), kdt, 0.0)
            state = state * jnp.exp(gt_c[lo:lo + 1, :]) + _mm(kdt_c, v_all)
            o_ref[lo:hi, e * D:(e + 1) * D] = o_c
        s_ref[e] = state


def gdn_scan(qkv_hm, kt_hm, slab, slabt):
    t = qkv_hm.shape[1]
    L = GDN_GROUP
    D = GDN_HEAD_DIM
    assert t % L == 0
    hm = lambda off, mul: pl.BlockSpec((None, L, D), lambda j, n: (off + mul * j, n, 0))
    return pl.pallas_call(
        _gdn_scan_kernel,
        grid=(GDN_QK_HEADS, t // L),
        in_specs=[
            hm(0, 1),
            hm(GDN_QK_HEADS, 1),
            pl.BlockSpec((None, D, L), lambda j, n: (j, 0, n)),
            hm(2 * GDN_QK_HEADS, 2),
            hm(2 * GDN_QK_HEADS + 1, 2),
            pl.BlockSpec((L, LANES), lambda j, n: (n, 0)),
            pl.BlockSpec((LANES, L), lambda j, n: (0, n)),
        ],
        out_specs=pl.BlockSpec((L, 2 * D), lambda j, n: (n, j)),
        out_shape=jax.ShapeDtypeStruct((t, GDN_V_HEADS * D), F32),
        scratch_shapes=[pltpu.VMEM((2, D, D), F32)],
        compiler_params=_cparams(("parallel", "arbitrary")),
        name="gdn_scan",
    )(qkv_hm, qkv_hm, kt_hm, qkv_hm, qkv_hm, slab, slabt)


def _gdn_out_kernel(o_ref, z_ref, onorm_ref, w_ref, gpost_ref, h_ref, out_ref):
    o = o_ref[...]
    z = z_ref[...]
    parts = []
    for hd in range(GDN_V_HEADS):
        seg = o[:, hd * GDN_HEAD_DIM:(hd + 1) * GDN_HEAD_DIM]
        parts.append(seg * lax.rsqrt(jnp.mean(seg * seg, axis=-1, keepdims=True) + NORM_EPS))
    gated = jnp.concatenate(parts, axis=1) * onorm_ref[...] * (z * _sigmoid(z))
    mix = jnp.dot(gated.astype(BF16), w_ref[...], preferred_element_type=F32)
    out_ref[...] = h_ref[...] + _rms(mix, gpost_ref[...])


def gdn_out(o, proj, onorm_tiled, w_out, gpost, h, tm=256):
    t, vw = o.shape
    d = h.shape[1]
    z_blk = (proj.shape[1] - vw) // vw
    assert proj.shape[1] % vw == 0 and t % tm == 0
    return pl.pallas_call(
        _gdn_out_kernel,
        grid=(t // tm,),
        in_specs=[
            pl.BlockSpec((tm, vw), lambda i: (i, 0)),
            pl.BlockSpec((tm, vw), lambda i: (i, z_blk)),
            pl.BlockSpec((1, vw), lambda i: (0, 0)),
            pl.BlockSpec((vw, d), lambda i: (0, 0)),
            pl.BlockSpec((1, d), lambda i: (0, 0)),
            pl.BlockSpec((tm, d), lambda i: (i, 0)),
        ],
        out_specs=pl.BlockSpec((tm, d), lambda i: (i, 0)),
        out_shape=jax.ShapeDtypeStruct((t, d), F32),
        compiler_params=_cparams(("parallel",)),
        name="gdn_out",
    )(o, proj, onorm_tiled, w_out, gpost.reshape(1, d), h)


def _compress_kernel(x_ref, pos_ref, w1_ref, w2_ref, o_ref):
    x = x_ref[...]
    pos = pos_ref[...]
    nc, half = x.shape
    w1 = w1_ref[...]
    first = _mm(x + pos[0:1, :], w1[:half])
    second = _mm(x + pos[1:2, :], w1[half:])
    hid = first + pltpu.roll(second, nc - 1, axis=0)
    hid = hid * _sigmoid(hid)
    out = jnp.dot(hid.astype(BF16), w2_ref[...], preferred_element_type=F32)
    rowi = lax.broadcasted_iota(jnp.int32, out.shape, 0)
    o_ref[...] = jnp.where(rowi < nc - 1, out, 0.0)


def compress(x2, pos2, w1, w2):
    _, g, nc, wdt = x2.shape
    hid = w1.shape[2]
    dh = w2.shape[2]
    return pl.pallas_call(
        _compress_kernel,
        grid=(2, g),
        in_specs=[
            pl.BlockSpec((None, None, nc, wdt), lambda b, gi: (b, gi, 0, 0)),
            pl.BlockSpec((None, 2, wdt), lambda b, gi: (b, 0, 0)),
            pl.BlockSpec((None, 2 * wdt, hid), lambda b, gi: (b, 0, 0)),
            pl.BlockSpec((None, hid, dh), lambda b, gi: (b, 0, 0)),
        ],
        out_specs=pl.BlockSpec((None, None, nc, dh), lambda b, gi: (b, gi, 0, 0)),
        out_shape=jax.ShapeDtypeStruct((2, g, nc, dh), F32),
        compiler_params=_cparams(("parallel", "parallel")),
        name="nsa_compress",
    )(x2, pos2, w1, w2)


def _cmp_topk_kernel(q_ref, kbd_ref, vbd_ref, m_ref, oc_ref, sel_ref, *, tq, nc, nsel, topk):
    i = pl.program_id(1)
    q = (q_ref[...] * (NSA_HEAD_DIM ** -0.5)).astype(BF16)
    s = jnp.dot(q, kbd_ref[...], preferred_element_type=F32)
    tpos = i * tq + lax.broadcasted_iota(jnp.int32, (tq, nc), 0)
    cblk = lax.broadcasted_iota(jnp.int32, (tq, nc), 1)
    mask = (CMP_STRIDE * cblk + CMP_BLOCK - 1) <= tpos
    psum = jnp.zeros((tq, nc), F32)
    oc = jnp.zeros((tq, NSA_GW), F32)
    for r in range(NSA_REP):
        s_r = jnp.where(mask, s[:, r * nc:(r + 1) * nc], -jnp.inf)
        m = jnp.max(s_r, axis=1, keepdims=True)
        m = jnp.where(m > -jnp.inf, m, 0.0)
        e = jnp.where(mask, jnp.exp(s_r - m), 0.0)
        p = e / jnp.maximum(jnp.sum(e, axis=1, keepdims=True), 1e-30)
        psum = psum + p
        oc = oc + jnp.dot(p.astype(BF16), vbd_ref[r * nc:(r + 1) * nc, :], preferred_element_type=F32)
    oc_ref[...] = oc
    imp = jnp.zeros((tq, nsel), F32)
    for piece in _split3(psum):
        imp = imp + jnp.dot(piece, m_ref[...], preferred_element_type=F32)
    t1 = i * tq + lax.broadcasted_iota(jnp.int32, (tq, nsel), 0)
    blk = lax.broadcasted_iota(jnp.int32, (tq, nsel), 1)
    cur = _div_pow2(t1, SEL_BLOCK)
    forced = (blk == 0) | (blk == cur) | (blk == cur - 1)
    valid = blk * SEL_BLOCK <= t1
    score = jnp.where(valid, jnp.where(forced, FORCED_SCORE, imp), -jnp.inf)
    blkf = blk.astype(F32)

    def pick(_, carry):
        work, sel = carry
        mx = jnp.max(work, axis=1, keepdims=True)
        first = jnp.min(jnp.where(work == mx, blkf, float(nsel)), axis=1, keepdims=True)
        hit = blkf == first
        return jnp.where(hit, -jnp.inf, work), jnp.where(hit, 1.0, sel)

    _, sel = lax.fori_loop(0, topk, pick, (score, jnp.zeros((tq, nsel), F32)))
    sel_ref[...] = sel.astype(sel_ref.dtype)


def cmp_topk(q, kbd, vbd, imp_mat, tq=128):
    t, qw = q.shape
    g = kbd.shape[0]
    nc = kbd.shape[2] // NSA_REP
    nsel = imp_mat.shape[1]
    topk = min(SEL_TOPK, nsel)
    assert t % tq == 0
    return pl.pallas_call(
        functools.partial(_cmp_topk_kernel, tq=tq, nc=nc, nsel=nsel, topk=topk),
        grid=(g, t // tq),
        in_specs=[
            pl.BlockSpec((tq, NSA_GW), lambda gi, i: (i, gi)),
            pl.BlockSpec((None, NSA_GW, NSA_REP * nc), lambda gi, i: (gi, 0, 0)),
            pl.BlockSpec((None, NSA_REP * nc, NSA_GW), lambda gi, i: (gi, 0, 0)),
            pl.BlockSpec((nc, nsel), lambda gi, i: (0, 0)),
        ],
        out_specs=[
            pl.BlockSpec((tq, NSA_GW), lambda gi, i: (i, gi)),
            pl.BlockSpec((None, tq, nsel), lambda gi, i: (gi, i, 0)),
        ],
        out_shape=[
            jax.ShapeDtypeStruct((t, qw), F32),
            jax.ShapeDtypeStruct((g, t, nsel), BF16),
        ],
        compiler_params=_cparams(("parallel", "parallel")),
        name="nsa_cmp_topk",
    )(q, kbd, vbd, imp_mat)


def _flash_kernel(qi_ref, ki_ref, first_ref, last_ref, *refs, tq, kt, nsel, selected):
    if selected:
        q_ref, k_ref, v_ref, sel_ref, o_ref, m_ref, l_ref, acc_ref = refs
    else:
        q_ref, k_ref, v_ref, o_ref, m_ref, l_ref, acc_ref = refs
    step = pl.program_id(1)
    qi = qi_ref[step]
    ki = ki_ref[step]

    @pl.when(first_ref[step] == 1)
    def _():
        m_ref[...] = jnp.full_like(m_ref, NEG_INIT)
        l_ref[...] = jnp.zeros_like(l_ref)
        acc_ref[...] = jnp.zeros_like(acc_ref)

    q = (q_ref[...] * (NSA_HEAD_DIM ** -0.5)).astype(BF16)
    k4 = k_ref[...]
    rblk = _div_pow2(lax.broadcasted_iota(jnp.int32, k4.shape, 0), NSA_HEAD_DIM)
    kbd = jnp.concatenate([jnp.where(rblk == r, k4, jnp.zeros_like(k4)) for r in range(NSA_REP)], axis=1)
    s = jnp.dot(q, kbd, preferred_element_type=F32)
    tpos = qi * tq + lax.broadcasted_iota(jnp.int32, (tq, kt), 0)
    kpos = ki * kt + lax.broadcasted_iota(jnp.int32, (tq, kt), 1)
    if selected:
        kblk = _div_pow2(ki * kt + lax.broadcasted_iota(jnp.int32, (nsel, kt), 1), SEL_BLOCK)
        expand = (kblk == lax.broadcasted_iota(jnp.int32, (nsel, kt), 0)).astype(BF16)
        chosen = jnp.dot(sel_ref[...], expand, preferred_element_type=F32)
        mask = (chosen > 0.5) & (kpos <= tpos)
    else:
        mask = (kpos <= tpos) & (kpos > tpos - WINDOW)
    v4 = v_ref[...]
    vseg = _div_pow2(lax.broadcasted_iota(jnp.int32, v4.shape, 1), NSA_HEAD_DIM)
    lane = lax.broadcasted_iota(jnp.int32, (tq, LANES), 1)
    pv = jnp.zeros((tq, NSA_GW), F32)
    alphas = []
    for r in range(NSA_REP):
        s_r = jnp.where(mask, s[:, r * kt:(r + 1) * kt], -jnp.inf)
        m_prev = m_ref[r]
        m_new = jnp.maximum(m_prev, jnp.max(s_r, axis=1, keepdims=True))
        alpha = jnp.exp(m_prev - m_new)
        p = jnp.exp(s_r - m_new[:, :1])
        l_ref[r] = alpha * l_ref[r] + jnp.sum(p, axis=1, keepdims=True)
        m_ref[r] = m_new
        vb = jnp.where(vseg == r, v4, jnp.zeros_like(v4))
        pv = pv + jnp.dot(p.astype(BF16), vb, preferred_element_type=F32)
        alphas.append(alpha)

    def per_head(vals):
        return jnp.concatenate([jnp.where(lane < NSA_HEAD_DIM, vals[0], vals[1]),
                                jnp.where(lane < NSA_HEAD_DIM, vals[2], vals[3])], axis=1)

    acc_ref[...] = acc_ref[...] * per_head(alphas) + pv

    @pl.when(last_ref[step] == 1)
    def _():
        o_ref[...] = acc_ref[...] / per_head([l_ref[r] for r in range(NSA_REP)])


def _pair_tables(t, tq, kt, window):
    qi, ki, first, last = [], [], [], []
    for a in range(t // tq):
        lo = 0 if window is None else max(0, (a * tq - window + 1) // kt)
        hi = (a * tq + tq - 1) // kt
        for b in range(lo, hi + 1):
            qi.append(a)
            ki.append(b)
            first.append(int(b == lo))
            last.append(int(b == hi))
    return [jnp.asarray(np.asarray(x, np.int32)) for x in (qi, ki, first, last)]


def flash_branch(q, kt4, v4, sel, tq, kt):
    t, qw = q.shape
    g = kt4.shape[0]
    selected = sel is not None
    nsel = sel.shape[2] if selected else 0
    tables = _pair_tables(t, tq, kt, None if selected else WINDOW)
    in_specs = [
        pl.BlockSpec((tq, NSA_GW), lambda gi, s, qi, ki, fi, la: (qi[s], gi)),
        pl.BlockSpec((None, NSA_GW, kt), lambda gi, s, qi, ki, fi, la: (gi, 0, ki[s])),
        pl.BlockSpec((None, kt, NSA_GW), lambda gi, s, qi, ki, fi, la: (gi, ki[s], 0)),
    ]
    args = [q, kt4, v4]
    if selected:
        in_specs.append(pl.BlockSpec((None, tq, nsel), lambda gi, s, qi, ki, fi, la: (gi, qi[s], 0)))
        args.append(sel)
    return pl.pallas_call(
        functools.partial(_flash_kernel, tq=tq, kt=kt, nsel=nsel, selected=selected),
        grid_spec=pltpu.PrefetchScalarGridSpec(
            num_scalar_prefetch=4,
            grid=(g, int(tables[0].shape[0])),
            in_specs=in_specs,
            out_specs=pl.BlockSpec((tq, NSA_GW), lambda gi, s, qi, ki, fi, la: (qi[s], gi)),
            scratch_shapes=[
                pltpu.VMEM((NSA_REP, tq, LANES), F32),
                pltpu.VMEM((NSA_REP, tq, LANES), F32),
                pltpu.VMEM((tq, NSA_GW), F32),
            ],
        ),
        out_shape=jax.ShapeDtypeStruct((t, qw), F32),
        compiler_params=_cparams(("parallel", "arbitrary")),
        name="nsa_selected" if selected else "nsa_window",
    )(*tables, *args)


def _nsa_out_kernel(oc_ref, os_ref, ow_ref, gl_ref, eg_ref, w_ref, gpost_ref, h_ref, out_ref):
    pieces = _split3(_sigmoid(gl_ref[...]))
    mixed = jnp.zeros(oc_ref.shape, F32)
    for b, br_ref in enumerate((oc_ref, os_ref, ow_ref)):
        gfull = jnp.zeros(oc_ref.shape, F32)
        for piece in pieces:
            gfull = gfull + jnp.dot(piece, eg_ref[b], preferred_element_type=F32)
        mixed = mixed + gfull * br_ref[...]
    mix = jnp.dot(mixed.astype(BF16), w_ref[...], preferred_element_type=F32)
    out_ref[...] = h_ref[...] + _rms(mix, gpost_ref[...])


def nsa_out(oc, osel, ow, gate_logits, expand, w_o, gpost, h, tm=256):
    t, qw = oc.shape
    d = h.shape[1]
    assert t % tm == 0
    row = lambda w: pl.BlockSpec((tm, w), lambda i: (i, 0))
    return pl.pallas_call(
        _nsa_out_kernel,
        grid=(t // tm,),
        in_specs=[
            row(qw), row(qw), row(qw), row(LANES),
            pl.BlockSpec((3, LANES, qw), lambda i: (0, 0, 0)),
            pl.BlockSpec((qw, d), lambda i: (0, 0)),
            pl.BlockSpec((1, d), lambda i: (0, 0)),
            row(d),
        ],
        out_specs=row(d),
        out_shape=jax.ShapeDtypeStruct((t, d), F32),
        compiler_params=_cparams(("parallel",)),
        name="nsa_out",
    )(oc, osel, ow, gate_logits, expand, w_o, gpost.reshape(1, d), h)


def _pad_cols(w, n):
    return jnp.pad(w, ((0, 0), (0, n - w.shape[1])))


def _importance_matrix(nc, nsel):
    r = SEL_BLOCK // CMP_STRIDE
    c = CMP_BLOCK // CMP_STRIDE
    mat = np.zeros((nc, nsel), np.float32)
    for kblk in range(nsel):
        for m in range(r):
            for n in range(c):
                j = r * kblk + m - n
                if 0 <= j < nc - 1:
                    mat[j, kblk] += 1.0
    return jnp.asarray(mat, BF16)


def _gate_expand():
    e = np.zeros((3, LANES, NSA_GROUPS * NSA_GW), np.float32)
    for head in range(NSA_GROUPS * NSA_REP):
        for b in range(3):
            e[b, head * 3 + b, head * NSA_HEAD_DIM:(head + 1) * NSA_HEAD_DIM] = 1.0
    return jnp.asarray(e, BF16)


def _block_diag_kv(k_cmp, v_cmp):
    g, nc, dh = k_cmp.shape
    eye = jnp.eye(NSA_REP, dtype=bool)
    kct = jnp.swapaxes(k_cmp, 1, 2)
    kbd = jnp.where(eye[None, :, None, :, None], kct[:, None, :, None, :], 0.0)
    vbd = jnp.where(eye[None, :, None, :, None], v_cmp[:, None, :, None, :], 0.0)
    return (kbd.reshape(g, NSA_REP * dh, NSA_REP * nc).astype(BF16),
            vbd.reshape(g, NSA_REP * nc, NSA_REP * dh).astype(BF16))


def kernel(x, p, mix_pre_norm, mix_post_norm, ffn_pre_norm, ffn_post_norm, gdn_w_in, gdn_conv_w, gdn_a_log,
           gdn_dt_bias, gdn_o_norm, gdn_w_out, kv_norm, kv_w, cmp_pos, cmp_w1, cmp_w2, nsa_w_qg, nsa_w_o,
           ffn_w_in, ffn_w_out, ple_w_in, ple_w_gate):
    depth = p.shape[0]
    n_a = gdn_w_in.shape[0]
    t = x.shape[1]
    h = x[0]
    fh = ffn_w_out.shape[1]
    conv_w_cols = gdn_conv_w.shape[2]
    vw = GDN_V_HEADS * GDN_HEAD_DIM
    main_w = conv_w_cols + vw

    def channel_and_ple(h, i):
        w_in = ffn_w_in[i].astype(BF16)
        return ffn_ple(h, ffn_pre_norm[i], w_in[:, :fh], w_in[:, fh:], ffn_w_out[i].astype(BF16),
                       ffn_post_norm[i], p[i, 0], ple_w_in[i].astype(BF16), ple_w_gate[i].astype(BF16))

    for i in range(n_a):
        w_in = gdn_w_in[i]
        w_beta = w_in[:, main_w:main_w + GDN_V_HEADS]
        w_a = w_in[:, main_w + GDN_V_HEADS:]
        w_small = _pad_cols(jnp.concatenate([w_beta, w_a, w_a], axis=1), LANES).astype(BF16)
        proj = norm_matmul(h, mix_pre_norm[i], w_in[:, :main_w].astype(BF16))
        ba = norm_matmul(h, mix_pre_norm[i], w_small)
        pad_vec = lambda v: jnp.pad(v, (GDN_V_HEADS, LANES - 2 * GDN_V_HEADS))
        alog_vec = (pad_vec(gdn_a_log[i]) + jnp.pad(gdn_a_log[i], (2 * GDN_V_HEADS, LANES - 3 * GDN_V_HEADS)))
        dtb_vec = (pad_vec(gdn_dt_bias[i]) + jnp.pad(gdn_dt_bias[i], (2 * GDN_V_HEADS, LANES - 3 * GDN_V_HEADS)))
        slab = gdn_gates(ba, alog_vec.reshape(1, LANES), dtb_vec.reshape(1, LANES))
        qkv_hm = gdn_conv(proj, gdn_conv_w[i])
        kt_hm = jnp.swapaxes(qkv_hm[GDN_QK_HEADS:2 * GDN_QK_HEADS], 1, 2)
        o = gdn_scan(qkv_hm, kt_hm, slab, slab.T)
        onorm_tiled = jnp.tile(gdn_o_norm[i], GDN_V_HEADS).reshape(1, vw)
        h = gdn_out(o, proj, onorm_tiled, gdn_w_out[i].astype(BF16), mix_post_norm[i], h)
        h = channel_and_ple(h, i)

    g = NSA_GROUPS
    dh = NSA_HEAD_DIM
    kv = norm_matmul(h, kv_norm, kv_w.astype(BF16), tn=768)
    kv6 = jnp.transpose(kv.reshape(t, 6, g, dh), (1, 2, 0, 3))
    nc = t // CMP_STRIDE
    nsel = t // SEL_BLOCK
    x2 = kv6[0:2].reshape(2, g, nc, CMP_STRIDE * dh)
    pos2 = cmp_pos.reshape(2, 2, CMP_STRIDE * dh)
    cmp_out = compress(x2, pos2, cmp_w1.astype(BF16), cmp_w2.astype(BF16))
    kbd, vbd = _block_diag_kv(cmp_out[0], cmp_out[1])
    imp_mat = _importance_matrix(nc, nsel)
    rep_t = lambda a: jnp.tile(jnp.swapaxes(a, 1, 2), (1, NSA_REP, 1)).astype(BF16)
    rep_v = lambda a: jnp.tile(a, (1, 1, NSA_REP)).astype(BF16)
    k_slc_t, v_slc4 = rep_t(kv6[2]), rep_v(kv6[3])
    k_win_t, v_win4 = rep_t(kv6[4]), rep_v(kv6[5])
    expand = _gate_expand()

    for i in range(n_a, depth):
        j = i - n_a
        qw = g * NSA_GW
        w_qg = nsa_w_qg[j]
        q = norm_matmul(h, mix_pre_norm[i], w_qg[:, :qw].astype(BF16))
        gate_logits = norm_matmul(h, mix_pre_norm[i], _pad_cols(w_qg[:, qw:], LANES).astype(BF16))
        o_c, sel = cmp_topk(q, kbd, vbd, imp_mat)
        o_s = flash_branch(q, k_slc_t, v_slc4, sel, tq=256, kt=256)
        o_w = flash_branch(q, k_win_t, v_win4, None, tq=256, kt=256)
        h = nsa_out(o_c, o_s, o_w, gate_logits, expand, nsa_w_o[j].astype(BF16), mix_post_norm[i], h)
        h = channel_and_ple(h, i)
    return h[None]
```

```python
import functools

import numpy as np
import jax
import jax.numpy as jnp
from jax import lax
from jax.experimental import pallas as pl
from jax.experimental.pallas import tpu as pltpu

F32 = jnp.float32
BF16 = jnp.bfloat16

NORM_EPS = 1e-6
L2_EPS = 1e-6
GDN_QK_HEADS = 8
GDN_V_HEADS = 16
GDN_HEAD_DIM = 128
GDN_CONV = 4
GDN_CHUNK = 64
GDN_GROUP = 256
GDN_SCAN_HEADS = 2
NSA_GROUPS = 4
NSA_REP = 4
NSA_HEAD_DIM = 64
NSA_GW = NSA_REP * NSA_HEAD_DIM
CMP_BLOCK = 32
CMP_STRIDE = 16
SEL_BLOCK = 64
SEL_TOPK = 16
WINDOW = 512
FORCED_SCORE = 1e4
LANES = 128
NEG_INIT = -1e30
LOG2_E = 1.4426950408889634
FLASH_SUB = 256
FLASH_SUM_ROWS = 16
CMP_ROWS = 256

VMEM_LIMIT = 56 * 1024 * 1024


def _cparams(sem):
    return pltpu.CompilerParams(dimension_semantics=sem, vmem_limit_bytes=VMEM_LIMIT)


def _rms(x, gain):
    return x * lax.rsqrt(jnp.mean(x * x, axis=-1, keepdims=True) + NORM_EPS) * gain


def _mm(a, b):
    return jnp.dot(a.astype(BF16), b.astype(BF16), preferred_element_type=F32)


def _sigmoid(x):
    return 1.0 / (1.0 + jnp.exp(-x))


def _div_pow2(x, d):
    shift = d.bit_length() - 1
    assert d == 1 << shift
    return jnp.right_shift(x, shift)


def _split3(x):
    a = x.astype(BF16)
    r = x - a.astype(F32)
    b = r.astype(BF16)
    c = (r - b.astype(F32)).astype(BF16)
    return a, b, c


def _norm_matmul_kernel(x_ref, g_ref, w_ref, o_ref, xn_ref):
    @pl.when(pl.program_id(1) == 0)
    def _():
        xn_ref[...] = _rms(x_ref[...], g_ref[...]).astype(BF16)

    o_ref[...] = jnp.dot(xn_ref[...], w_ref[...], preferred_element_type=F32)


def norm_matmul(h, gain, w, tm=512, tn=1024):
    t, d = h.shape
    n = w.shape[1]
    tn = min(tn, n)
    assert t % tm == 0 and n % tn == 0
    return pl.pallas_call(
        _norm_matmul_kernel,
        grid=(t // tm, n // tn),
        in_specs=[
            pl.BlockSpec((tm, d), lambda i, j: (i, 0)),
            pl.BlockSpec((1, d), lambda i, j: (0, 0)),
            pl.BlockSpec((d, tn), lambda i, j: (0, j)),
        ],
        out_specs=pl.BlockSpec((tm, tn), lambda i, j: (i, j)),
        out_shape=jax.ShapeDtypeStruct((t, n), F32),
        scratch_shapes=[pltpu.VMEM((tm, d), BF16)],
        compiler_params=_cparams(("parallel", "arbitrary")),
        name="norm_matmul",
    )(h, gain.reshape(1, d), w)


def _ffn_ple_kernel(h_ref, gpre_ref, wg_ref, wu_ref, wo_ref, gpost_ref, p_ref, wple_ref, wgt_ref,
                    o_ref, xn_ref, acc_ref, *, nf):
    f = pl.program_id(1)

    @pl.when(f == 0)
    def _():
        xn_ref[...] = _rms(h_ref[...], gpre_ref[...]).astype(BF16)
        acc_ref[...] = jnp.zeros_like(acc_ref)

    xn = xn_ref[...]
    gate = jnp.dot(xn, wg_ref[...], preferred_element_type=F32)
    up = jnp.dot(xn, wu_ref[...], preferred_element_type=F32)
    act = gate * _sigmoid(gate) * up
    acc_ref[...] += jnp.dot(act.astype(BF16), wo_ref[...], preferred_element_type=F32)

    @pl.when(f == nf - 1)
    def _():
        h2 = h_ref[...] + _rms(acc_ref[...], gpost_ref[...])
        emb = jnp.dot(p_ref[...].astype(BF16), wple_ref[...], preferred_element_type=F32)
        gt = _sigmoid(jnp.dot(h2.astype(BF16), wgt_ref[...], preferred_element_type=F32))
        o_ref[...] = h2 + emb * gt


def ffn_ple(h, gpre, w_gate, w_up, w_out, gpost, p, w_ple, w_plegate, tm=512, tf=1408):
    t, d = h.shape
    fh = w_gate.shape[1]
    pd = p.shape[1]
    assert t % tm == 0 and fh % tf == 0
    nf = fh // tf
    return pl.pallas_call(
        functools.partial(_ffn_ple_kernel, nf=nf),
        grid=(t // tm, nf),
        in_specs=[
            pl.BlockSpec((tm, d), lambda i, f: (i, 0)),
            pl.BlockSpec((1, d), lambda i, f: (0, 0)),
            pl.BlockSpec((d, tf), lambda i, f: (0, f)),
            pl.BlockSpec((d, tf), lambda i, f: (0, f)),
            pl.BlockSpec((tf, d), lambda i, f: (f, 0)),
            pl.BlockSpec((1, d), lambda i, f: (0, 0)),
            pl.BlockSpec((tm, pd), lambda i, f: (i, 0)),
            pl.BlockSpec((pd, d), lambda i, f: (0, 0)),
            pl.BlockSpec((d, d), lambda i, f: (0, 0)),
        ],
        out_specs=pl.BlockSpec((tm, d), lambda i, f: (i, 0)),
        out_shape=jax.ShapeDtypeStruct((t, d), F32),
        scratch_shapes=[pltpu.VMEM((tm, d), BF16), pltpu.VMEM((tm, d), F32)],
        compiler_params=_cparams(("parallel", "arbitrary")),
        name="ffn_ple",
    )(h, gpre.reshape(1, d), w_gate, w_up, w_out, gpost.reshape(1, d), p, w_ple, w_plegate)


def _gdn_conv_kernel(x_ref, halo_ref, w_ref, o_ref, *, tm):
    c = pl.program_id(0)
    i = pl.program_id(1)
    x = x_ref[...]
    halo = jnp.where(i > 0, halo_ref[...], 0.0)
    ext = jnp.concatenate([halo, x], axis=0)
    w = w_ref[...]
    y = x * w[GDN_CONV - 1:GDN_CONV, :]
    for k in range(1, GDN_CONV):
        shifted = pltpu.roll(ext, k, axis=0)[8:8 + tm]
        y = y + shifted * w[GDN_CONV - 1 - k:GDN_CONV - k, :]
    y = y * _sigmoid(y)
    normed = y * lax.rsqrt(jnp.sum(y * y, axis=-1, keepdims=True) + L2_EPS)
    q_scale = jnp.where(c < GDN_QK_HEADS, GDN_HEAD_DIM ** -0.5, 1.0)
    o_ref[...] = jnp.where(c < 2 * GDN_QK_HEADS, normed * q_scale, y)


def gdn_conv(proj, conv_w, tm=1024):
    t = proj.shape[0]
    n_tiles = conv_w.shape[1] // LANES
    assert t % tm == 0
    return pl.pallas_call(
        functools.partial(_gdn_conv_kernel, tm=tm),
        grid=(n_tiles, t // tm),
        in_specs=[
            pl.BlockSpec((tm, LANES), lambda c, i: (i, c)),
            pl.BlockSpec((8, LANES), lambda c, i: (jnp.maximum(i * (tm // 8) - 1, 0), c)),
            pl.BlockSpec((GDN_CONV, LANES), lambda c, i: (0, c)),
        ],
        out_specs=pl.BlockSpec((None, tm, LANES), lambda c, i: (c, i, 0)),
        out_shape=jax.ShapeDtypeStruct((n_tiles, t, LANES), F32),
        compiler_params=_cparams(("parallel", "parallel")),
        name="gdn_conv",
    )(proj, proj, conv_w)


def _gdn_gate_kernel(x_ref, alog_ref, dtb_ref, lc_ref, lf_ref, o_ref):
    x = x_ref[...]
    lane = lax.broadcasted_iota(jnp.int32, x.shape, 1)
    beta = _sigmoid(x)
    z = x + dtb_ref[...]
    softplus = jnp.maximum(z, 0.0) + jnp.log(1.0 + jnp.exp(-jnp.abs(z)))
    g = -jnp.exp(alog_ref[...]) * softplus
    gcum = jnp.zeros_like(x)
    gtot = jnp.zeros_like(x)
    for piece in _split3(g):
        gcum = gcum + jnp.dot(lc_ref[...], piece, preferred_element_type=F32)
        gtot = gtot + jnp.dot(lf_ref[...], piece, preferred_element_type=F32)
    o_ref[...] = jnp.where(lane < GDN_V_HEADS, beta, jnp.where(lane < 2 * GDN_V_HEADS, gcum, gtot))


def gdn_gates(ba, alog_vec, dtb_vec):
    t = ba.shape[0]
    tm = GDN_GROUP
    r = np.arange(tm)
    same = (r[:, None] // GDN_CHUNK) == (r[None, :] // GDN_CHUNK)
    lc = jnp.asarray(same & (r[None, :] <= r[:, None]), BF16)
    lf = jnp.asarray(same, BF16)
    return pl.pallas_call(
        _gdn_gate_kernel,
        grid=(t // tm,),
        in_specs=[
            pl.BlockSpec((tm, LANES), lambda i: (i, 0)),
            pl.BlockSpec((1, LANES), lambda i: (0, 0)),
            pl.BlockSpec((1, LANES), lambda i: (0, 0)),
            pl.BlockSpec((tm, tm), lambda i: (0, 0)),
            pl.BlockSpec((tm, tm), lambda i: (0, 0)),
        ],
        out_specs=pl.BlockSpec((tm, LANES), lambda i: (i, 0)),
        out_shape=jax.ShapeDtypeStruct((t, LANES), F32),
        compiler_params=_cparams(("parallel",)),
        name="gdn_gates",
    )(ba, alog_vec, dtb_vec, lc, lf)


def _gdn_scan_kernel(q_ref, k_ref, kt_ref, v_ref, slab_ref, slabt_ref, o_ref, s_ref):
    n = pl.program_id(1)

    @pl.when(n == 0)
    def _():
        s_ref[...] = jnp.zeros_like(s_ref)

    L = GDN_GROUP
    C = GDN_CHUNK
    D = GDN_HEAD_DIM
    slab = slab_ref[...]
    slabt = slabt_ref[...]
    row = lax.broadcasted_iota(jnp.int32, (L, L), 0)
    col = lax.broadcasted_iota(jnp.int32, (L, L), 1)
    same = _div_pow2(row, C) == _div_pow2(col, C)
    causal = same & (col <= row)
    strict = same & (col < row)
    eye = (row == col).astype(F32)
    lane = lax.broadcasted_iota(jnp.int32, (L, LANES), 1)
    sub = lax.broadcasted_iota(jnp.int32, (LANES, L), 0)

    def column(idx):
        return jnp.sum(jnp.where(lane == idx, slab, 0.0), axis=1, keepdims=True)

    def rowvec(idx):
        return jnp.sum(jnp.where(sub == idx, slabt, 0.0), axis=0, keepdims=True)

    colk = lax.broadcasted_iota(jnp.int32, (D, L), 1)
    slots = range(2 * GDN_SCAN_HEADS)

    q = [q_ref[hq] for hq in range(GDN_SCAN_HEADS)]
    k = [k_ref[hq] for hq in range(GDN_SCAN_HEADS)]
    kt = [kt_ref[hq] for hq in range(GDN_SCAN_HEADS)]
    kk = [_mm(k[hq], kt[hq]) for hq in range(GDN_SCAN_HEADS)]
    qk = [_mm(q[hq], kt[hq]) for hq in range(GDN_SCAN_HEADS)]
    beta_c, gc_c, gt_c, decay, bp, inv, kdt = [], [], [], [], [], [], []
    for slot in slots:
        hq = slot // 2
        hv = 2 * (pl.program_id(0) * GDN_SCAN_HEADS + hq) + slot % 2
        beta_c.append(column(hv))
        gc_c.append(column(GDN_V_HEADS + hv))
        gt_c.append(column(2 * GDN_V_HEADS + hv))
        gc_r = rowvec(GDN_V_HEADS + hv)
        gt_r = rowvec(2 * GDN_V_HEADS + hv)
        decay.append(jnp.where(causal, jnp.exp(jnp.where(causal, gc_c[slot] - gc_r, 0.0)), 0.0))
        bp.append(jnp.where(strict, -(kk[hq] * beta_c[slot]) * decay[slot], 0.0))
        inv.append(eye + bp[slot])
        kdt.append(kt[hq] * jnp.exp(gt_r - gc_r))
    for _ in range(5):
        bp = [_mm(bp[slot], bp[slot]) for slot in slots]
        inv = [inv[slot] + _mm(inv[slot], bp[slot]) for slot in slots]
    u, w, qkm, q_dec = [], [], [], []
    for slot in slots:
        hq = slot // 2
        egc = jnp.exp(gc_c[slot])
        rhs = jnp.concatenate([v_ref[slot] * beta_c[slot], k[hq] * (beta_c[slot] * egc)], axis=1)
        sol = _mm(inv[slot], rhs)
        u.append(sol[:, :D])
        w.append(sol[:, D:])
        qkm.append(jnp.where(causal, qk[hq] * decay[slot], 0.0))
        q_dec.append(q[hq] * egc)
    state = [s_ref[slot] for slot in slots]
    v_done = [[] for _ in slots]
    for c in range(L // C):
        lo, hi = c * C, (c + 1) * C
        r = [_mm(jnp.concatenate([w[slot][lo:hi], q_dec[slot][lo:hi]], axis=0), state[slot]) for slot in slots]
        for slot in slots:
            v_done[slot].append(u[slot][lo:hi] - r[slot][:C])
            v_all = jnp.concatenate(v_done[slot] + [jnp.zeros((L - hi, D), F32)] * (hi < L), axis=0)
            o_ref[lo:hi, slot * D:(slot + 1) * D] = r[slot][C:] + _mm(qkm[slot][lo:hi, :], v_all)
            kdt_c = jnp.where((colk >= lo) & (colk < hi), kdt[slot], 0.0)
            state[slot] = state[slot] * jnp.exp(gt_c[slot][lo:lo + 1, :]) + _mm(kdt_c, v_all)
    for slot in slots:
        s_ref[slot] = state[slot]


def gdn_scan(qkv_hm, kt_hm, slab, slabt):
    t = qkv_hm.shape[1]
    L = GDN_GROUP
    D = GDN_HEAD_DIM
    hq = GDN_SCAN_HEADS
    assert t % L == 0 and GDN_QK_HEADS % hq == 0
    q_blocks = GDN_QK_HEADS // hq
    return pl.pallas_call(
        _gdn_scan_kernel,
        grid=(q_blocks, t // L),
        in_specs=[
            pl.BlockSpec((hq, L, D), lambda j, n: (j, n, 0)),
            pl.BlockSpec((hq, L, D), lambda j, n: (q_blocks + j, n, 0)),
            pl.BlockSpec((hq, D, L), lambda j, n: (j, 0, n)),
            pl.BlockSpec((2 * hq, L, D), lambda j, n: (q_blocks + j, n, 0)),
            pl.BlockSpec((L, LANES), lambda j, n: (n, 0)),
            pl.BlockSpec((LANES, L), lambda j, n: (0, n)),
        ],
        out_specs=pl.BlockSpec((L, 2 * hq * D), lambda j, n: (n, j)),
        out_shape=jax.ShapeDtypeStruct((t, GDN_V_HEADS * D), F32),
        scratch_shapes=[pltpu.VMEM((2 * hq, D, D), F32)],
        compiler_params=_cparams(("parallel", "arbitrary")),
        name="gdn_scan",
    )(qkv_hm, qkv_hm, kt_hm, qkv_hm, slab, slabt)


def _gdn_out_kernel(o_ref, z_ref, onorm_ref, w_ref, gpost_ref, h_ref, out_ref):
    o = o_ref[...]
    z = z_ref[...]
    parts = []
    for hd in range(GDN_V_HEADS):
        seg = o[:, hd * GDN_HEAD_DIM:(hd + 1) * GDN_HEAD_DIM]
        parts.append(seg * lax.rsqrt(jnp.mean(seg * seg, axis=-1, keepdims=True) + NORM_EPS))
    gated = jnp.concatenate(parts, axis=1) * onorm_ref[...] * (z * _sigmoid(z))
    mix = jnp.dot(gated.astype(BF16), w_ref[...], preferred_element_type=F32)
    out_ref[...] = h_ref[...] + _rms(mix, gpost_ref[...])


def gdn_out(o, proj, onorm_tiled, w_out, gpost, h, tm=256):
    t, vw = o.shape
    d = h.shape[1]
    z_blk = (proj.shape[1] - vw) // vw
    assert proj.shape[1] % vw == 0 and t % tm == 0
    return pl.pallas_call(
        _gdn_out_kernel,
        grid=(t // tm,),
        in_specs=[
            pl.BlockSpec((tm, vw), lambda i: (i, 0)),
            pl.BlockSpec((tm, vw), lambda i: (i, z_blk)),
            pl.BlockSpec((1, vw), lambda i: (0, 0)),
            pl.BlockSpec((vw, d), lambda i: (0, 0)),
            pl.BlockSpec((1, d), lambda i: (0, 0)),
            pl.BlockSpec((tm, d), lambda i: (i, 0)),
        ],
        out_specs=pl.BlockSpec((tm, d), lambda i: (i, 0)),
        out_shape=jax.ShapeDtypeStruct((t, d), F32),
        compiler_params=_cparams(("parallel",)),
        name="gdn_out",
    )(o, proj, onorm_tiled, w_out, gpost.reshape(1, d), h)


def _compress_kernel(x_ref, pos_ref, w1_ref, w2_ref, o_ref):
    x = x_ref[...]
    pos = pos_ref[...]
    nc, half = x.shape
    w1 = w1_ref[...]
    first = _mm(x + pos[0:1, :], w1[:half])
    second = _mm(x + pos[1:2, :], w1[half:])
    hid = first + pltpu.roll(second, nc - 1, axis=0)
    hid = hid * _sigmoid(hid)
    out = jnp.dot(hid.astype(BF16), w2_ref[...], preferred_element_type=F32)
    rowi = lax.broadcasted_iota(jnp.int32, out.shape, 0)
    o_ref[...] = jnp.where(rowi < nc - 1, out, 0.0)


def compress(x2, pos2, w1, w2):
    _, g, nc, wdt = x2.shape
    hid = w1.shape[2]
    dh = w2.shape[2]
    return pl.pallas_call(
        _compress_kernel,
        grid=(2, g),
        in_specs=[
            pl.BlockSpec((None, None, nc, wdt), lambda b, gi: (b, gi, 0, 0)),
            pl.BlockSpec((None, 2, wdt), lambda b, gi: (b, 0, 0)),
            pl.BlockSpec((None, 2 * wdt, hid), lambda b, gi: (b, 0, 0)),
            pl.BlockSpec((None, hid, dh), lambda b, gi: (b, 0, 0)),
        ],
        out_specs=pl.BlockSpec((None, None, nc, dh), lambda b, gi: (b, gi, 0, 0)),
        out_shape=jax.ShapeDtypeStruct((2, g, nc, dh), F32),
        compiler_params=_cparams(("parallel", "parallel")),
        name="nsa_compress",
    )(x2, pos2, w1, w2)


def _cmp_topk_kernel(q_ref, kbd_ref, vbd_ref, m_ref, oc_ref, sel_ref, s_ref, p_ref, psum_ref, *,
                     tq, nc, nsel, topk):
    i = pl.program_id(1)
    dh = NSA_HEAD_DIM
    rows_per = min(nc, CMP_ROWS)
    q = (q_ref[...] * ((dh ** -0.5) * LOG2_E)).astype(BF16)
    s_ref[...] = jnp.dot(kbd_ref[...], q, preferred_element_type=F32)
    for ch in range(tq // LANES):
        lanes = slice(ch * LANES, (ch + 1) * LANES)
        tpos = i * tq + ch * LANES + lax.broadcasted_iota(jnp.int32, (rows_per, LANES), 1)
        cblk0 = lax.broadcasted_iota(jnp.int32, (rows_per, LANES), 0)
        masks = [(CMP_STRIDE * (cblk0 + c * rows_per) + CMP_BLOCK - 1) <= tpos for c in range(nc // rows_per)]
        for r in range(NSA_REP):
            pieces = [slice(r * nc + c * rows_per, r * nc + (c + 1) * rows_per) for c in range(nc // rows_per)]
            m = jnp.full((1, LANES), -jnp.inf, F32)
            for rows, mask in zip(pieces, masks):
                m = jnp.maximum(m, jnp.max(jnp.where(mask, s_ref[rows, lanes], -jnp.inf), axis=0, keepdims=True))
            m = jnp.where(m > -jnp.inf, m, 0.0)
            total = jnp.zeros((1, LANES), F32)
            for rows, mask in zip(pieces, masks):
                e = jnp.exp2(jnp.where(mask, s_ref[rows, lanes], -jnp.inf) - m)
                s_ref[rows, lanes] = e
                total = total + jnp.sum(e, axis=0, keepdims=True)
            inv = 1.0 / jnp.maximum(total, 1e-30)
            for c, rows in enumerate(pieces):
                p = s_ref[rows, lanes] * inv
                p_ref[rows, lanes] = p.astype(BF16)
                prow = slice(c * rows_per, (c + 1) * rows_per)
                psum_ref[prow, lanes] = p if r == 0 else psum_ref[prow, lanes] + p
    oc_t = jnp.dot(vbd_ref[...], p_ref[...], preferred_element_type=F32)
    imp = jnp.zeros((nsel, tq), F32)
    for piece in _split3(psum_ref[...]):
        imp = imp + jnp.dot(m_ref[...], piece, preferred_element_type=F32)
    for ch in range(tq // LANES):
        lanes = slice(ch * LANES, (ch + 1) * LANES)
        oc_ref[lanes, :] = oc_t[:, lanes].T
        t1 = i * tq + ch * LANES + lax.broadcasted_iota(jnp.int32, (nsel, LANES), 1)
        blk = lax.broadcasted_iota(jnp.int32, (nsel, LANES), 0)
        cur = _div_pow2(t1, SEL_BLOCK)
        forced = (blk == 0) | (blk == cur) | (blk == cur - 1)
        valid = blk * SEL_BLOCK <= t1
        score = jnp.where(valid, jnp.where(forced, FORCED_SCORE, imp[:, lanes]), -jnp.inf)
        blkf = blk.astype(F32)
        work = score
        for _ in range(topk):
            mx = jnp.max(work, axis=0, keepdims=True)
            first = jnp.min(jnp.where(work == mx, blkf, float(nsel)), axis=0, keepdims=True)
            work = jnp.where(blkf == first, -jnp.inf, work)
        sel_ref[:, lanes] = jnp.where((score > -jnp.inf) & (work == -jnp.inf), 1.0, 0.0).astype(sel_ref.dtype)


def cmp_topk(q_t, kbd, vbd_t, imp_mat_t, tq=256):
    qw, t = q_t.shape
    g = kbd.shape[0]
    nc = kbd.shape[1] // NSA_REP
    nsel = imp_mat_t.shape[0]
    topk = min(SEL_TOPK, nsel)
    assert t % tq == 0 and nc % min(nc, CMP_ROWS) == 0
    return pl.pallas_call(
        functools.partial(_cmp_topk_kernel, tq=tq, nc=nc, nsel=nsel, topk=topk),
        grid=(g, t // tq),
        in_specs=[
            pl.BlockSpec((NSA_GW, tq), lambda gi, i: (gi, i)),
            pl.BlockSpec((None, NSA_REP * nc, NSA_GW), lambda gi, i: (gi, 0, 0)),
            pl.BlockSpec((None, NSA_GW, NSA_REP * nc), lambda gi, i: (gi, 0, 0)),
            pl.BlockSpec((nsel, nc), lambda gi, i: (0, 0)),
        ],
        out_specs=[
            pl.BlockSpec((tq, NSA_GW), lambda gi, i: (i, gi)),
            pl.BlockSpec((None, nsel, tq), lambda gi, i: (gi, 0, i)),
        ],
        out_shape=[
            jax.ShapeDtypeStruct((t, qw), F32),
            jax.ShapeDtypeStruct((g, nsel, t), F32),
        ],
        scratch_shapes=[
            pltpu.VMEM((NSA_REP * nc, tq), F32),
            pltpu.VMEM((NSA_REP * nc, tq), BF16),
            pltpu.VMEM((nc, tq), F32),
        ],
        compiler_params=_cparams(("parallel", "parallel")),
        name="nsa_cmp_topk",
    )(q_t, kbd, vbd_t, imp_mat_t)


def _flash_kernel(qi_ref, ki_ref, first_ref, last_ref, *refs, tq, kt, nsel, selected):
    if selected:
        q_ref, k_ref, v_ref, sel_ref, o_ref = refs[:5]
    else:
        q_ref, k_ref, v_ref, o_ref = refs[:4]
    m_ref, l_ref, alpha_ref, acc_ref, s_ref, p_ref, bias_ref, kbd_ref, vbd_ref = refs[-9:]
    step = pl.program_id(1)
    qi = qi_ref[step]
    ki = ki_ref[step]
    dh = NSA_HEAD_DIM

    @pl.when(first_ref[step] == 1)
    def _():
        m_ref[...] = jnp.full_like(m_ref, NEG_INIT)
        l_ref[...] = jnp.zeros_like(l_ref)
        acc_ref[...] = jnp.zeros_like(acc_ref)

    k4 = k_ref[...]
    kseg = _div_pow2(lax.broadcasted_iota(jnp.int32, k4.shape, 1), dh)
    v4 = v_ref[...]
    vblk = _div_pow2(lax.broadcasted_iota(jnp.int32, v4.shape, 0), dh)
    for r in range(NSA_REP):
        kbd_ref[r * kt:(r + 1) * kt, :] = jnp.where(kseg == r, k4, jnp.zeros_like(k4))
        vbd_ref[:NSA_GW, r * kt:(r + 1) * kt] = jnp.where(vblk == r, v4, jnp.zeros_like(v4))
    one_row = lax.broadcasted_iota(jnp.int32, (FLASH_SUM_ROWS, NSA_REP * kt), 0)
    one_head = _div_pow2(lax.broadcasted_iota(jnp.int32, (FLASH_SUM_ROWS, NSA_REP * kt), 1), kt)
    vbd_ref[NSA_GW:, :] = (one_row == one_head).astype(BF16)

    q_scale = (dh ** -0.5) * LOG2_E
    n_sub = tq // FLASH_SUB

    def scores(sub):
        c0 = sub * FLASH_SUB
        cols = slice(c0, c0 + FLASH_SUB)
        q = (q_ref[:, cols] * q_scale).astype(BF16)
        s_ref[:, cols] = jnp.dot(kbd_ref[...], q, preferred_element_type=F32)
        for jb in range(kt // SEL_BLOCK):
            rows = slice(jb * SEL_BLOCK, (jb + 1) * SEL_BLOCK)
            tpos = qi * tq + c0 + lax.broadcasted_iota(jnp.int32, (SEL_BLOCK, FLASH_SUB), 1)
            kpos = ki * kt + jb * SEL_BLOCK + lax.broadcasted_iota(jnp.int32, (SEL_BLOCK, FLASH_SUB), 0)
            if selected:
                picked = sel_ref[pl.ds(ki * (kt // SEL_BLOCK) + jb, 1), cols] > 0.5
                allowed = picked & (kpos <= tpos)
            else:
                allowed = (kpos <= tpos) & (kpos > tpos - WINDOW)
            bias_ref[rows, cols] = jnp.where(allowed, 0.0, -jnp.inf)

    scores(0)
    for sub in range(n_sub):
        c0 = sub * FLASH_SUB
        cols = slice(c0, c0 + FLASH_SUB)
        if sub + 1 < n_sub:
            scores(sub + 1)
        for ch in range(FLASH_SUB // LANES):
            lanes = slice(c0 + ch * LANES, c0 + (ch + 1) * LANES)
            bias = bias_ref[:, lanes]
            for r in range(NSA_REP):
                x = s_ref[r * kt:(r + 1) * kt, lanes] + bias
                m_prev = m_ref[r:r + 1, lanes]
                m_new = jnp.maximum(m_prev, jnp.max(x, axis=0, keepdims=True))
                m_ref[r:r + 1, lanes] = m_new
                alpha_ref[r:r + 1, lanes] = jnp.exp2(m_prev - m_new)
                p_ref[r * kt:(r + 1) * kt, lanes] = jnp.exp2(x - m_new).astype(BF16)
        pv = jnp.dot(vbd_ref[...], p_ref[:, cols], preferred_element_type=F32)
        for r in range(NSA_REP):
            hd = slice(r * dh, (r + 1) * dh)
            alpha = alpha_ref[r:r + 1, cols]
            acc_ref[hd, cols] = acc_ref[hd, cols] * alpha + pv[hd]
            l_ref[r:r + 1, cols] = l_ref[r:r + 1, cols] * alpha + pv[NSA_GW + r:NSA_GW + r + 1]

    @pl.when(last_ref[step] == 1)
    def _():
        for ch in range(tq // LANES):
            lanes = slice(ch * LANES, (ch + 1) * LANES)
            out_t = jnp.concatenate(
                [acc_ref[r * dh:(r + 1) * dh, lanes] / l_ref[r:r + 1, lanes] for r in range(NSA_REP)], axis=0)
            o_ref[lanes, :] = out_t.T


def _pair_tables(t, tq, kt, window):
    qi, ki, first, last = [], [], [], []
    for a in range(t // tq):
        lo = 0 if window is None else max(0, (a * tq - window + 1) // kt)
        hi = (a * tq + tq - 1) // kt
        for b in range(lo, hi + 1):
            qi.append(a)
            ki.append(b)
            first.append(int(b == lo))
            last.append(int(b == hi))
    return [jnp.asarray(np.asarray(x, np.int32)) for x in (qi, ki, first, last)]


def flash_branch(q_t, k4, vt4, sel_t, tq, kt):
    qw, t = q_t.shape
    g = k4.shape[0]
    selected = sel_t is not None
    nsel = sel_t.shape[1] if selected else 0
    tables = _pair_tables(t, tq, kt, None if selected else WINDOW)
    in_specs = [
        pl.BlockSpec((NSA_GW, tq), lambda gi, s, qi, ki, fi, la: (gi, qi[s])),
        pl.BlockSpec((None, kt, NSA_GW), lambda gi, s, qi, ki, fi, la: (gi, ki[s], 0)),
        pl.BlockSpec((None, NSA_GW, kt), lambda gi, s, qi, ki, fi, la: (gi, 0, ki[s])),
    ]
    args = [q_t, k4, vt4]
    if selected:
        in_specs.append(pl.BlockSpec((None, nsel, tq), lambda gi, s, qi, ki, fi, la: (gi, 0, qi[s])))
        args.append(sel_t)
    return pl.pallas_call(
        functools.partial(_flash_kernel, tq=tq, kt=kt, nsel=nsel, selected=selected),
        grid_spec=pltpu.PrefetchScalarGridSpec(
            num_scalar_prefetch=4,
            grid=(g, int(tables[0].shape[0])),
            in_specs=in_specs,
            out_specs=pl.BlockSpec((tq, NSA_GW), lambda gi, s, qi, ki, fi, la: (qi[s], gi)),
            scratch_shapes=[
                pltpu.VMEM((8, tq), F32),
                pltpu.VMEM((8, tq), F32),
                pltpu.VMEM((8, tq), F32),
                pltpu.VMEM((NSA_GW, tq), F32),
                pltpu.VMEM((NSA_REP * kt, tq), F32),
                pltpu.VMEM((NSA_REP * kt, tq), BF16),
                pltpu.VMEM((kt, tq), F32),
                pltpu.VMEM((NSA_REP * kt, NSA_GW), BF16),
                pltpu.VMEM((NSA_GW + FLASH_SUM_ROWS, NSA_REP * kt), BF16),
            ],
        ),
        out_shape=jax.ShapeDtypeStruct((t, qw), F32),
        compiler_params=_cparams(("parallel", "arbitrary")),
        name="nsa_selected" if selected else "nsa_window",
    )(*tables, *args)


def _nsa_out_kernel(oc_ref, os_ref, ow_ref, gl_ref, eg_ref, w_ref, gpost_ref, h_ref, out_ref):
    pieces = _split3(_sigmoid(gl_ref[...]))
    mixed = jnp.zeros(oc_ref.shape, F32)
    for b, br_ref in enumerate((oc_ref, os_ref, ow_ref)):
        gfull = jnp.zeros(oc_ref.shape, F32)
        for piece in pieces:
            gfull = gfull + jnp.dot(piece, eg_ref[b], preferred_element_type=F32)
        mixed = mixed + gfull * br_ref[...]
    mix = jnp.dot(mixed.astype(BF16), w_ref[...], preferred_element_type=F32)
    out_ref[...] = h_ref[...] + _rms(mix, gpost_ref[...])


def nsa_out(oc, osel, ow, gate_logits, expand, w_o, gpost, h, tm=256):
    t, qw = oc.shape
    d = h.shape[1]
    assert t % tm == 0
    row = lambda w: pl.BlockSpec((tm, w), lambda i: (i, 0))
    return pl.pallas_call(
        _nsa_out_kernel,
        grid=(t // tm,),
        in_specs=[
            row(qw), row(qw), row(qw), row(LANES),
            pl.BlockSpec((3, LANES, qw), lambda i: (0, 0, 0)),
            pl.BlockSpec((qw, d), lambda i: (0, 0)),
            pl.BlockSpec((1, d), lambda i: (0, 0)),
            row(d),
        ],
        out_specs=row(d),
        out_shape=jax.ShapeDtypeStruct((t, d), F32),
        compiler_params=_cparams(("parallel",)),
        name="nsa_out",
    )(oc, osel, ow, gate_logits, expand, w_o, gpost.reshape(1, d), h)


def _pad_cols(w, n):
    return jnp.pad(w, ((0, 0), (0, n - w.shape[1])))


def _importance_matrix(nc, nsel):
    r = SEL_BLOCK // CMP_STRIDE
    c = CMP_BLOCK // CMP_STRIDE
    mat = np.zeros((nc, nsel), np.float32)
    for kblk in range(nsel):
        for m in range(r):
            for n in range(c):
                j = r * kblk + m - n
                if 0 <= j < nc - 1:
                    mat[j, kblk] += 1.0
    return jnp.asarray(mat, BF16)


def _gate_expand():
    e = np.zeros((3, LANES, NSA_GROUPS * NSA_GW), np.float32)
    for head in range(NSA_GROUPS * NSA_REP):
        for b in range(3):
            e[b, head * 3 + b, head * NSA_HEAD_DIM:(head + 1) * NSA_HEAD_DIM] = 1.0
    return jnp.asarray(e, BF16)


def _block_diag_kv(k_cmp, v_cmp):
    g, nc, dh = k_cmp.shape
    eye = jnp.eye(NSA_REP, dtype=bool)
    kct = jnp.swapaxes(k_cmp, 1, 2)
    kbd = jnp.where(eye[None, :, None, :, None], kct[:, None, :, None, :], 0.0)
    vbd = jnp.where(eye[None, :, None, :, None], v_cmp[:, None, :, None, :], 0.0)
    return (kbd.reshape(g, NSA_REP * dh, NSA_REP * nc).astype(BF16),
            vbd.reshape(g, NSA_REP * nc, NSA_REP * dh).astype(BF16))


def kernel(x, p, mix_pre_norm, mix_post_norm, ffn_pre_norm, ffn_post_norm, gdn_w_in, gdn_conv_w, gdn_a_log,
           gdn_dt_bias, gdn_o_norm, gdn_w_out, kv_norm, kv_w, cmp_pos, cmp_w1, cmp_w2, nsa_w_qg, nsa_w_o,
           ffn_w_in, ffn_w_out, ple_w_in, ple_w_gate):
    depth = p.shape[0]
    n_a = gdn_w_in.shape[0]
    t = x.shape[1]
    h = x[0]
    fh = ffn_w_out.shape[1]
    conv_w_cols = gdn_conv_w.shape[2]
    vw = GDN_V_HEADS * GDN_HEAD_DIM
    main_w = conv_w_cols + vw

    def channel_and_ple(h, i):
        w_in = ffn_w_in[i].astype(BF16)
        return ffn_ple(h, ffn_pre_norm[i], w_in[:, :fh], w_in[:, fh:], ffn_w_out[i].astype(BF16),
                       ffn_post_norm[i], p[i, 0], ple_w_in[i].astype(BF16), ple_w_gate[i].astype(BF16))

    for i in range(n_a):
        w_in = gdn_w_in[i]
        w_beta = w_in[:, main_w:main_w + GDN_V_HEADS]
        w_a = w_in[:, main_w + GDN_V_HEADS:]
        w_small = _pad_cols(jnp.concatenate([w_beta, w_a, w_a], axis=1), LANES).astype(BF16)
        proj = norm_matmul(h, mix_pre_norm[i], w_in[:, :main_w].astype(BF16))
        ba = norm_matmul(h, mix_pre_norm[i], w_small)
        pad_vec = lambda v: jnp.pad(v, (GDN_V_HEADS, LANES - 2 * GDN_V_HEADS))
        alog_vec = (pad_vec(gdn_a_log[i]) + jnp.pad(gdn_a_log[i], (2 * GDN_V_HEADS, LANES - 3 * GDN_V_HEADS)))
        dtb_vec = (pad_vec(gdn_dt_bias[i]) + jnp.pad(gdn_dt_bias[i], (2 * GDN_V_HEADS, LANES - 3 * GDN_V_HEADS)))
        slab = gdn_gates(ba, alog_vec.reshape(1, LANES), dtb_vec.reshape(1, LANES))
        qkv_hm = gdn_conv(proj, gdn_conv_w[i])
        kt_hm = jnp.swapaxes(qkv_hm[GDN_QK_HEADS:2 * GDN_QK_HEADS], 1, 2)
        o = gdn_scan(qkv_hm, kt_hm, slab, slab.T)
        onorm_tiled = jnp.tile(gdn_o_norm[i], GDN_V_HEADS).reshape(1, vw)
        h = gdn_out(o, proj, onorm_tiled, gdn_w_out[i].astype(BF16), mix_post_norm[i], h)
        h = channel_and_ple(h, i)

    g = NSA_GROUPS
    dh = NSA_HEAD_DIM
    kv = norm_matmul(h, kv_norm, kv_w.astype(BF16), tn=768)
    kv6 = jnp.transpose(kv.reshape(t, 6, g, dh), (1, 2, 0, 3))
    nc = t // CMP_STRIDE
    nsel = t // SEL_BLOCK
    x2 = kv6[0:2].reshape(2, g, nc, CMP_STRIDE * dh)
    pos2 = cmp_pos.reshape(2, 2, CMP_STRIDE * dh)
    cmp_out = compress(x2, pos2, cmp_w1.astype(BF16), cmp_w2.astype(BF16))
    vbd_t, kbd = _block_diag_kv(cmp_out[1], cmp_out[0])
    imp_mat_t = _importance_matrix(nc, nsel).T
    rep_t = lambda a: jnp.tile(jnp.swapaxes(a, 1, 2), (1, NSA_REP, 1)).astype(BF16)
    rep_v = lambda a: jnp.tile(a, (1, 1, NSA_REP)).astype(BF16)
    k_slc4, v_slc_t = rep_v(kv6[2]), rep_t(kv6[3])
    k_win4, v_win_t = rep_v(kv6[4]), rep_t(kv6[5])
    expand = _gate_expand()

    for i in range(n_a, depth):
        j = i - n_a
        qw = g * NSA_GW
        w_qg = nsa_w_qg[j]
        q = norm_matmul(h, mix_pre_norm[i], w_qg[:, :qw].astype(BF16))
        gate_logits = norm_matmul(h, mix_pre_norm[i], _pad_cols(w_qg[:, qw:], LANES).astype(BF16))
        q_t = q.T
        o_c, sel_t = cmp_topk(q_t, kbd, vbd_t, imp_mat_t)
        o_s = flash_branch(q_t, k_slc4, v_slc_t, sel_t, tq=1024, kt=256)
        o_w = flash_branch(q_t, k_win4, v_win_t, None, tq=512, kt=256)
        h = nsa_out(o_c, o_s, o_w, gate_logits, expand, nsa_w_o[j].astype(BF16), mix_post_norm[i], h)
        h = channel_and_ple(h, i)
    return h[None]
```

```python
import functools

import numpy as np
import jax
import jax.numpy as jnp
from jax import lax
from jax.experimental import pallas as pl
from jax.experimental.pallas import tpu as pltpu

F32 = jnp.float32
BF16 = jnp.bfloat16

NORM_EPS = 1e-6
L2_EPS = 1e-6
GDN_QK_HEADS = 8
GDN_V_HEADS = 16
GDN_HEAD_DIM = 128
GDN_CONV = 4
GDN_CHUNK = 64
GDN_GROUP = 256
GDN_SCAN_HEADS = 2
NSA_GROUPS = 4
NSA_REP = 4
NSA_HEAD_DIM = 64
NSA_GW = NSA_REP * NSA_HEAD_DIM
CMP_BLOCK = 32
CMP_STRIDE = 16
SEL_BLOCK = 64
SEL_TOPK = 16
WINDOW = 512
FORCED_SCORE = 1e4
LANES = 128
NEG_INIT = -(2.0 ** 100)
LOG2_E = 1.4426950408889634
FLASH_SUB = 256
FLASH_SUM_ROWS = 16
CMP_ROWS = 256

VMEM_LIMIT = 56 * 1024 * 1024


def _cparams(sem):
    return pltpu.CompilerParams(dimension_semantics=sem, vmem_limit_bytes=VMEM_LIMIT)


def _rms(x, gain):
    return x * lax.rsqrt(jnp.mean(x * x, axis=-1, keepdims=True) + NORM_EPS) * gain


def _mm(a, b):
    return jnp.dot(a.astype(BF16), b.astype(BF16), preferred_element_type=F32)


def _sigmoid(x):
    return 1.0 / (1.0 + jnp.exp(-x))


def _div_pow2(x, d):
    shift = d.bit_length() - 1
    assert d == 1 << shift
    return jnp.right_shift(x, shift)


def _split3(x):
    a = x.astype(BF16)
    r = x - a.astype(F32)
    b = r.astype(BF16)
    c = (r - b.astype(F32)).astype(BF16)
    return a, b, c


def _norm_matmul_kernel(x_ref, g_ref, w_ref, o_ref, xn_ref):
    @pl.when(pl.program_id(1) == 0)
    def _():
        xn_ref[...] = _rms(x_ref[...], g_ref[...]).astype(BF16)

    o_ref[...] = jnp.dot(xn_ref[...], w_ref[...], preferred_element_type=F32)


def norm_matmul(h, gain, w, tm=512, tn=1024):
    t, d = h.shape
    n = w.shape[1]
    tn = min(tn, n)
    assert t % tm == 0 and n % tn == 0
    return pl.pallas_call(
        _norm_matmul_kernel,
        grid=(t // tm, n // tn),
        in_specs=[
            pl.BlockSpec((tm, d), lambda i, j: (i, 0)),
            pl.BlockSpec((1, d), lambda i, j: (0, 0)),
            pl.BlockSpec((d, tn), lambda i, j: (0, j)),
        ],
        out_specs=pl.BlockSpec((tm, tn), lambda i, j: (i, j)),
        out_shape=jax.ShapeDtypeStruct((t, n), F32),
        scratch_shapes=[pltpu.VMEM((tm, d), BF16)],
        compiler_params=_cparams(("parallel", "arbitrary")),
        name="norm_matmul",
    )(h, gain.reshape(1, d), w)


def _ffn_ple_kernel(h_ref, gpre_ref, wg_ref, wu_ref, wo_ref, gpost_ref, p_ref, wple_ref, wgt_ref,
                    o_ref, xn_ref, acc_ref, *, nf):
    f = pl.program_id(1)

    @pl.when(f == 0)
    def _():
        xn_ref[...] = _rms(h_ref[...], gpre_ref[...]).astype(BF16)
        acc_ref[...] = jnp.zeros_like(acc_ref)

    xn = xn_ref[...]
    gate = jnp.dot(xn, wg_ref[...], preferred_element_type=F32)
    up = jnp.dot(xn, wu_ref[...], preferred_element_type=F32)
    act = gate * _sigmoid(gate) * up
    acc_ref[...] += jnp.dot(act.astype(BF16), wo_ref[...], preferred_element_type=F32)

    @pl.when(f == nf - 1)
    def _():
        h2 = h_ref[...] + _rms(acc_ref[...], gpost_ref[...])
        emb = jnp.dot(p_ref[...].astype(BF16), wple_ref[...], preferred_element_type=F32)
        gt = _sigmoid(jnp.dot(h2.astype(BF16), wgt_ref[...], preferred_element_type=F32))
        o_ref[...] = h2 + emb * gt


def ffn_ple(h, gpre, w_gate, w_up, w_out, gpost, p, w_ple, w_plegate, tm=512, tf=1408):
    t, d = h.shape
    fh = w_gate.shape[1]
    pd = p.shape[1]
    assert t % tm == 0 and fh % tf == 0
    nf = fh // tf
    return pl.pallas_call(
        functools.partial(_ffn_ple_kernel, nf=nf),
        grid=(t // tm, nf),
        in_specs=[
            pl.BlockSpec((tm, d), lambda i, f: (i, 0)),
            pl.BlockSpec((1, d), lambda i, f: (0, 0)),
            pl.BlockSpec((d, tf), lambda i, f: (0, f)),
            pl.BlockSpec((d, tf), lambda i, f: (0, f)),
            pl.BlockSpec((tf, d), lambda i, f: (f, 0)),
            pl.BlockSpec((1, d), lambda i, f: (0, 0)),
            pl.BlockSpec((tm, pd), lambda i, f: (i, 0)),
            pl.BlockSpec((pd, d), lambda i, f: (0, 0)),
            pl.BlockSpec((d, d), lambda i, f: (0, 0)),
        ],
        out_specs=pl.BlockSpec((tm, d), lambda i, f: (i, 0)),
        out_shape=jax.ShapeDtypeStruct((t, d), F32),
        scratch_shapes=[pltpu.VMEM((tm, d), BF16), pltpu.VMEM((tm, d), F32)],
        compiler_params=_cparams(("parallel", "arbitrary")),
        name="ffn_ple",
    )(h, gpre.reshape(1, d), w_gate, w_up, w_out, gpost.reshape(1, d), p, w_ple, w_plegate)


def _gdn_conv_kernel(x_ref, halo_ref, w_ref, o_ref, *, tm):
    c = pl.program_id(0)
    i = pl.program_id(1)
    x = x_ref[...]
    halo = jnp.where(i > 0, halo_ref[...], 0.0)
    ext = jnp.concatenate([halo, x], axis=0)
    w = w_ref[...]
    y = x * w[GDN_CONV - 1:GDN_CONV, :]
    for k in range(1, GDN_CONV):
        shifted = pltpu.roll(ext, k, axis=0)[8:8 + tm]
        y = y + shifted * w[GDN_CONV - 1 - k:GDN_CONV - k, :]
    y = y * _sigmoid(y)
    normed = y * lax.rsqrt(jnp.sum(y * y, axis=-1, keepdims=True) + L2_EPS)
    q_scale = jnp.where(c < GDN_QK_HEADS, GDN_HEAD_DIM ** -0.5, 1.0)
    o_ref[...] = jnp.where(c < 2 * GDN_QK_HEADS, normed * q_scale, y)


def gdn_conv(proj, conv_w, tm=1024):
    t = proj.shape[0]
    n_tiles = conv_w.shape[1] // LANES
    assert t % tm == 0
    return pl.pallas_call(
        functools.partial(_gdn_conv_kernel, tm=tm),
        grid=(n_tiles, t // tm),
        in_specs=[
            pl.BlockSpec((tm, LANES), lambda c, i: (i, c)),
            pl.BlockSpec((8, LANES), lambda c, i: (jnp.maximum(i * (tm // 8) - 1, 0), c)),
            pl.BlockSpec((GDN_CONV, LANES), lambda c, i: (0, c)),
        ],
        out_specs=pl.BlockSpec((None, tm, LANES), lambda c, i: (c, i, 0)),
        out_shape=jax.ShapeDtypeStruct((n_tiles, t, LANES), F32),
        compiler_params=_cparams(("parallel", "parallel")),
        name="gdn_conv",
    )(proj, proj, conv_w)


def _gdn_gate_kernel(x_ref, alog_ref, dtb_ref, lc_ref, lf_ref, o_ref):
    x = x_ref[...]
    lane = lax.broadcasted_iota(jnp.int32, x.shape, 1)
    beta = _sigmoid(x)
    z = x + dtb_ref[...]
    softplus = jnp.maximum(z, 0.0) + jnp.log(1.0 + jnp.exp(-jnp.abs(z)))
    g = -jnp.exp(alog_ref[...]) * softplus
    gcum = jnp.zeros_like(x)
    gtot = jnp.zeros_like(x)
    for piece in _split3(g):
        gcum = gcum + jnp.dot(lc_ref[...], piece, preferred_element_type=F32)
        gtot = gtot + jnp.dot(lf_ref[...], piece, preferred_element_type=F32)
    o_ref[...] = jnp.where(lane < GDN_V_HEADS, beta, jnp.where(lane < 2 * GDN_V_HEADS, gcum, gtot))


def gdn_gates(ba, alog_vec, dtb_vec):
    t = ba.shape[0]
    tm = GDN_GROUP
    r = np.arange(tm)
    same = (r[:, None] // GDN_CHUNK) == (r[None, :] // GDN_CHUNK)
    lc = jnp.asarray(same & (r[None, :] <= r[:, None]), BF16)
    lf = jnp.asarray(same, BF16)
    return pl.pallas_call(
        _gdn_gate_kernel,
        grid=(t // tm,),
        in_specs=[
            pl.BlockSpec((tm, LANES), lambda i: (i, 0)),
            pl.BlockSpec((1, LANES), lambda i: (0, 0)),
            pl.BlockSpec((1, LANES), lambda i: (0, 0)),
            pl.BlockSpec((tm, tm), lambda i: (0, 0)),
            pl.BlockSpec((tm, tm), lambda i: (0, 0)),
        ],
        out_specs=pl.BlockSpec((tm, LANES), lambda i: (i, 0)),
        out_shape=jax.ShapeDtypeStruct((t, LANES), F32),
        compiler_params=_cparams(("parallel",)),
        name="gdn_gates",
    )(ba, alog_vec, dtb_vec, lc, lf)


def _gdn_scan_kernel(q_ref, k_ref, kt_ref, v_ref, slab_ref, slabt_ref, o_ref, s_ref):
    n = pl.program_id(1)

    @pl.when(n == 0)
    def _():
        s_ref[...] = jnp.zeros_like(s_ref)

    L = GDN_GROUP
    C = GDN_CHUNK
    D = GDN_HEAD_DIM
    slab = slab_ref[...]
    slabt = slabt_ref[...]
    row = lax.broadcasted_iota(jnp.int32, (L, L), 0)
    col = lax.broadcasted_iota(jnp.int32, (L, L), 1)
    same = _div_pow2(row, C) == _div_pow2(col, C)
    causal = same & (col <= row)
    strict = same & (col < row)
    eye = (row == col).astype(F32)
    lane = lax.broadcasted_iota(jnp.int32, (L, LANES), 1)
    sub = lax.broadcasted_iota(jnp.int32, (LANES, L), 0)

    def column(idx):
        return jnp.sum(jnp.where(lane == idx, slab, 0.0), axis=1, keepdims=True)

    def rowvec(idx):
        return jnp.sum(jnp.where(sub == idx, slabt, 0.0), axis=0, keepdims=True)

    colk = lax.broadcasted_iota(jnp.int32, (D, L), 1)
    slots = range(2 * GDN_SCAN_HEADS)

    q = [q_ref[hq] for hq in range(GDN_SCAN_HEADS)]
    k = [k_ref[hq] for hq in range(GDN_SCAN_HEADS)]
    kt = [kt_ref[hq] for hq in range(GDN_SCAN_HEADS)]
    kk = [_mm(k[hq], kt[hq]) for hq in range(GDN_SCAN_HEADS)]
    qk = [_mm(q[hq], kt[hq]) for hq in range(GDN_SCAN_HEADS)]
    beta_c, gc_c, gt_c, decay, bp, inv, kdt = [], [], [], [], [], [], []
    for slot in slots:
        hq = slot // 2
        hv = 2 * (pl.program_id(0) * GDN_SCAN_HEADS + hq) + slot % 2
        beta_c.append(column(hv))
        gc_c.append(column(GDN_V_HEADS + hv))
        gt_c.append(column(2 * GDN_V_HEADS + hv))
        gc_r = rowvec(GDN_V_HEADS + hv)
        gt_r = rowvec(2 * GDN_V_HEADS + hv)
        decay.append(jnp.where(causal, jnp.exp(jnp.where(causal, gc_c[slot] - gc_r, 0.0)), 0.0))
        bp.append(jnp.where(strict, -(kk[hq] * beta_c[slot]) * decay[slot], 0.0))
        inv.append(eye + bp[slot])
        kdt.append(kt[hq] * jnp.exp(gt_r - gc_r))
    for _ in range(5):
        bp = [_mm(bp[slot], bp[slot]) for slot in slots]
        inv = [inv[slot] + _mm(inv[slot], bp[slot]) for slot in slots]
    u, w, qkm, q_dec = [], [], [], []
    for slot in slots:
        hq = slot // 2
        egc = jnp.exp(gc_c[slot])
        rhs = jnp.concatenate([v_ref[slot] * beta_c[slot], k[hq] * (beta_c[slot] * egc)], axis=1)
        sol = _mm(inv[slot], rhs)
        u.append(sol[:, :D])
        w.append(sol[:, D:])
        qkm.append(jnp.where(causal, qk[hq] * decay[slot], 0.0))
        q_dec.append(q[hq] * egc)
    state = [s_ref[slot] for slot in slots]
    v_done = [[] for _ in slots]
    for c in range(L // C):
        lo, hi = c * C, (c + 1) * C
        r = [_mm(jnp.concatenate([w[slot][lo:hi], q_dec[slot][lo:hi]], axis=0), state[slot]) for slot in slots]
        for slot in slots:
            v_done[slot].append(u[slot][lo:hi] - r[slot][:C])
            v_all = jnp.concatenate(v_done[slot] + [jnp.zeros((L - hi, D), F32)] * (hi < L), axis=0)
            o_ref[lo:hi, slot * D:(slot + 1) * D] = r[slot][C:] + _mm(qkm[slot][lo:hi, :], v_all)
            kdt_c = jnp.where((colk >= lo) & (colk < hi), kdt[slot], 0.0)
            state[slot] = state[slot] * jnp.exp(gt_c[slot][lo:lo + 1, :]) + _mm(kdt_c, v_all)
    for slot in slots:
        s_ref[slot] = state[slot]


def gdn_scan(qkv_hm, kt_hm, slab, slabt):
    t = qkv_hm.shape[1]
    L = GDN_GROUP
    D = GDN_HEAD_DIM
    hq = GDN_SCAN_HEADS
    assert t % L == 0 and GDN_QK_HEADS % hq == 0
    q_blocks = GDN_QK_HEADS // hq
    return pl.pallas_call(
        _gdn_scan_kernel,
        grid=(q_blocks, t // L),
        in_specs=[
            pl.BlockSpec((hq, L, D), lambda j, n: (j, n, 0)),
            pl.BlockSpec((hq, L, D), lambda j, n: (q_blocks + j, n, 0)),
            pl.BlockSpec((hq, D, L), lambda j, n: (j, 0, n)),
            pl.BlockSpec((2 * hq, L, D), lambda j, n: (q_blocks + j, n, 0)),
            pl.BlockSpec((L, LANES), lambda j, n: (n, 0)),
            pl.BlockSpec((LANES, L), lambda j, n: (0, n)),
        ],
        out_specs=pl.BlockSpec((L, 2 * hq * D), lambda j, n: (n, j)),
        out_shape=jax.ShapeDtypeStruct((t, GDN_V_HEADS * D), F32),
        scratch_shapes=[pltpu.VMEM((2 * hq, D, D), F32)],
        compiler_params=_cparams(("parallel", "arbitrary")),
        name="gdn_scan",
    )(qkv_hm, qkv_hm, kt_hm, qkv_hm, slab, slabt)


def _gdn_out_kernel(o_ref, z_ref, onorm_ref, w_ref, gpost_ref, h_ref, out_ref):
    o = o_ref[...]
    z = z_ref[...]
    parts = []
    for hd in range(GDN_V_HEADS):
        seg = o[:, hd * GDN_HEAD_DIM:(hd + 1) * GDN_HEAD_DIM]
        parts.append(seg * lax.rsqrt(jnp.mean(seg * seg, axis=-1, keepdims=True) + NORM_EPS))
    gated = jnp.concatenate(parts, axis=1) * onorm_ref[...] * (z * _sigmoid(z))
    mix = jnp.dot(gated.astype(BF16), w_ref[...], preferred_element_type=F32)
    out_ref[...] = h_ref[...] + _rms(mix, gpost_ref[...])


def gdn_out(o, proj, onorm_tiled, w_out, gpost, h, tm=256):
    t, vw = o.shape
    d = h.shape[1]
    z_blk = (proj.shape[1] - vw) // vw
    assert proj.shape[1] % vw == 0 and t % tm == 0
    return pl.pallas_call(
        _gdn_out_kernel,
        grid=(t // tm,),
        in_specs=[
            pl.BlockSpec((tm, vw), lambda i: (i, 0)),
            pl.BlockSpec((tm, vw), lambda i: (i, z_blk)),
            pl.BlockSpec((1, vw), lambda i: (0, 0)),
            pl.BlockSpec((vw, d), lambda i: (0, 0)),
            pl.BlockSpec((1, d), lambda i: (0, 0)),
            pl.BlockSpec((tm, d), lambda i: (i, 0)),
        ],
        out_specs=pl.BlockSpec((tm, d), lambda i: (i, 0)),
        out_shape=jax.ShapeDtypeStruct((t, d), F32),
        compiler_params=_cparams(("parallel",)),
        name="gdn_out",
    )(o, proj, onorm_tiled, w_out, gpost.reshape(1, d), h)


def _compress_kernel(x_ref, pos_ref, w1_ref, w2_ref, o_ref):
    x = x_ref[...]
    pos = pos_ref[...]
    nc, half = x.shape
    w1 = w1_ref[...]
    first = _mm(x + pos[0:1, :], w1[:half])
    second = _mm(x + pos[1:2, :], w1[half:])
    hid = first + pltpu.roll(second, nc - 1, axis=0)
    hid = hid * _sigmoid(hid)
    out = jnp.dot(hid.astype(BF16), w2_ref[...], preferred_element_type=F32)
    rowi = lax.broadcasted_iota(jnp.int32, out.shape, 0)
    o_ref[...] = jnp.where(rowi < nc - 1, out, 0.0)


def compress(x2, pos2, w1, w2):
    _, g, nc, wdt = x2.shape
    hid = w1.shape[2]
    dh = w2.shape[2]
    return pl.pallas_call(
        _compress_kernel,
        grid=(2, g),
        in_specs=[
            pl.BlockSpec((None, None, nc, wdt), lambda b, gi: (b, gi, 0, 0)),
            pl.BlockSpec((None, 2, wdt), lambda b, gi: (b, 0, 0)),
            pl.BlockSpec((None, 2 * wdt, hid), lambda b, gi: (b, 0, 0)),
            pl.BlockSpec((None, hid, dh), lambda b, gi: (b, 0, 0)),
        ],
        out_specs=pl.BlockSpec((None, None, nc, dh), lambda b, gi: (b, gi, 0, 0)),
        out_shape=jax.ShapeDtypeStruct((2, g, nc, dh), F32),
        compiler_params=_cparams(("parallel", "parallel")),
        name="nsa_compress",
    )(x2, pos2, w1, w2)


def _cmp_topk_kernel(q_ref, kbd_ref, vbd_ref, m_ref, oc_ref, sel_ref, s_ref, p_ref, psum_ref, *,
                     tq, nc, nsel, topk):
    i = pl.program_id(1)
    dh = NSA_HEAD_DIM
    rows_per = min(nc, CMP_ROWS)
    q = (q_ref[...] * ((dh ** -0.5) * LOG2_E)).astype(BF16)
    s_ref[...] = jnp.dot(kbd_ref[...], q, preferred_element_type=F32)
    for ch in range(tq // LANES):
        lanes = slice(ch * LANES, (ch + 1) * LANES)
        tpos = i * tq + ch * LANES + lax.broadcasted_iota(jnp.int32, (rows_per, LANES), 1)
        cblk0 = lax.broadcasted_iota(jnp.int32, (rows_per, LANES), 0)
        masks = [(CMP_STRIDE * (cblk0 + c * rows_per) + CMP_BLOCK - 1) <= tpos for c in range(nc // rows_per)]
        for r in range(NSA_REP):
            pieces = [slice(r * nc + c * rows_per, r * nc + (c + 1) * rows_per) for c in range(nc // rows_per)]
            m = jnp.full((1, LANES), -jnp.inf, F32)
            for rows, mask in zip(pieces, masks):
                m = jnp.maximum(m, jnp.max(jnp.where(mask, s_ref[rows, lanes], -jnp.inf), axis=0, keepdims=True))
            m = jnp.where(m > -jnp.inf, m, 0.0)
            total = jnp.zeros((1, LANES), F32)
            for rows, mask in zip(pieces, masks):
                e = jnp.exp2(jnp.where(mask, s_ref[rows, lanes], -jnp.inf) - m)
                s_ref[rows, lanes] = e
                total = total + jnp.sum(e, axis=0, keepdims=True)
            inv = 1.0 / jnp.maximum(total, 1e-30)
            for c, rows in enumerate(pieces):
                p = s_ref[rows, lanes] * inv
                p_ref[rows, lanes] = p.astype(BF16)
                prow = slice(c * rows_per, (c + 1) * rows_per)
                psum_ref[prow, lanes] = p if r == 0 else psum_ref[prow, lanes] + p
    oc_t = jnp.dot(vbd_ref[...], p_ref[...], preferred_element_type=F32)
    imp = jnp.zeros((nsel, tq), F32)
    for piece in _split3(psum_ref[...]):
        imp = imp + jnp.dot(m_ref[...], piece, preferred_element_type=F32)
    for ch in range(tq // LANES):
        lanes = slice(ch * LANES, (ch + 1) * LANES)
        oc_ref[lanes, :] = oc_t[:, lanes].T
        t1 = i * tq + ch * LANES + lax.broadcasted_iota(jnp.int32, (nsel, LANES), 1)
        blk = lax.broadcasted_iota(jnp.int32, (nsel, LANES), 0)
        cur = _div_pow2(t1, SEL_BLOCK)
        forced = (blk == 0) | (blk == cur) | (blk == cur - 1)
        valid = blk * SEL_BLOCK <= t1
        score = jnp.where(valid, jnp.where(forced, FORCED_SCORE, imp[:, lanes]), -jnp.inf)
        blkf = blk.astype(F32)
        work = score
        for _ in range(topk):
            mx = jnp.max(work, axis=0, keepdims=True)
            first = jnp.min(jnp.where(work == mx, blkf, float(nsel)), axis=0, keepdims=True)
            work = jnp.where(blkf == first, -jnp.inf, work)
        sel_ref[:, lanes] = jnp.where((score > -jnp.inf) & (work == -jnp.inf), 1.0, 0.0).astype(sel_ref.dtype)


def cmp_topk(q_t, kbd, vbd_t, imp_mat_t, tq=256):
    qw, t = q_t.shape
    g = kbd.shape[0]
    nc = kbd.shape[1] // NSA_REP
    nsel = imp_mat_t.shape[0]
    topk = min(SEL_TOPK, nsel)
    assert t % tq == 0 and nc % min(nc, CMP_ROWS) == 0
    return pl.pallas_call(
        functools.partial(_cmp_topk_kernel, tq=tq, nc=nc, nsel=nsel, topk=topk),
        grid=(g, t // tq),
        in_specs=[
            pl.BlockSpec((NSA_GW, tq), lambda gi, i: (gi, i)),
            pl.BlockSpec((None, NSA_REP * nc, NSA_GW), lambda gi, i: (gi, 0, 0)),
            pl.BlockSpec((None, NSA_GW, NSA_REP * nc), lambda gi, i: (gi, 0, 0)),
            pl.BlockSpec((nsel, nc), lambda gi, i: (0, 0)),
        ],
        out_specs=[
            pl.BlockSpec((tq, NSA_GW), lambda gi, i: (i, gi)),
            pl.BlockSpec((None, nsel, tq), lambda gi, i: (gi, 0, i)),
        ],
        out_shape=[
            jax.ShapeDtypeStruct((t, qw), F32),
            jax.ShapeDtypeStruct((g, nsel, t), F32),
        ],
        scratch_shapes=[
            pltpu.VMEM((NSA_REP * nc, tq), F32),
            pltpu.VMEM((NSA_REP * nc, tq), BF16),
            pltpu.VMEM((nc, tq), F32),
        ],
        compiler_params=_cparams(("parallel", "parallel")),
        name="nsa_cmp_topk",
    )(q_t, kbd, vbd_t, imp_mat_t)


def _flash_kernel(qi_ref, ki_ref, first_ref, last_ref, *refs, tq, kt, nsel, selected):
    if selected:
        q_ref, k_ref, v_ref, sel_ref, o_ref = refs[:5]
    else:
        q_ref, k_ref, v_ref, o_ref = refs[:4]
    m_ref, l_ref, alpha_ref, acc_ref, s_ref, p_ref, bias_ref, vt_ref, qs_ref = refs[-9:]
    step = pl.program_id(1)
    qi = qi_ref[step]
    ki = ki_ref[step]
    dh = NSA_HEAD_DIM

    @pl.when(first_ref[step] == 1)
    def _():
        m_ref[...] = jnp.full_like(m_ref, NEG_INIT)
        l_ref[...] = jnp.zeros_like(l_ref)
        acc_ref[...] = jnp.zeros_like(acc_ref)
        qs_ref[...] = (q_ref[...] * ((dh ** -0.5) * LOG2_E)).astype(BF16)

    vt_ref[:dh, :] = v_ref[...]
    vt_ref[dh:, :] = jnp.ones((FLASH_SUM_ROWS, kt), BF16)

    n_sub = tq // FLASH_SUB
    n_blk = kt // SEL_BLOCK

    def body(positional):
        def scores(sub):
            c0 = sub * FLASH_SUB
            cols = slice(c0, c0 + FLASH_SUB)
            for r in range(NSA_REP):
                s_ref[r * kt:(r + 1) * kt, cols] = jnp.dot(
                    k_ref[...], qs_ref[r * dh:(r + 1) * dh, cols], preferred_element_type=F32).astype(BF16)
            for jb in range(n_blk):
                rows = slice(jb * SEL_BLOCK, (jb + 1) * SEL_BLOCK)
                if selected:
                    picked = sel_ref[pl.ds(ki * n_blk + jb, 1), cols] > 0.5
                if positional:
                    tpos = qi * tq + c0 + lax.broadcasted_iota(jnp.int32, (SEL_BLOCK, FLASH_SUB), 1)
                    kpos = ki * kt + jb * SEL_BLOCK + lax.broadcasted_iota(jnp.int32, (SEL_BLOCK, FLASH_SUB), 0)
                    if selected:
                        allowed = picked & (kpos <= tpos)
                    else:
                        allowed = (kpos <= tpos) & (kpos > tpos - WINDOW)
                else:
                    allowed = jnp.broadcast_to(picked, (SEL_BLOCK, FLASH_SUB))
                bias_ref[rows, cols] = jnp.where(allowed, 0.0, -jnp.inf).astype(BF16)

        scores(0)
        for sub in range(n_sub):
            c0 = sub * FLASH_SUB
            cols = slice(c0, c0 + FLASH_SUB)
            if sub + 1 < n_sub:
                scores(sub + 1)
            for ch in range(FLASH_SUB // LANES):
                lanes = slice(c0 + ch * LANES, c0 + (ch + 1) * LANES)
                bias = bias_ref[:, lanes]
                for r in range(NSA_REP):
                    x = s_ref[r * kt:(r + 1) * kt, lanes] + bias
                    m_prev = m_ref[r:r + 1, lanes]
                    m_new = jnp.maximum(m_prev, jnp.max(x, axis=0, keepdims=True).astype(F32))
                    m_ref[r:r + 1, lanes] = m_new
                    alpha_ref[r:r + 1, lanes] = jnp.exp2(m_prev - m_new)
                    p_ref[r * kt:(r + 1) * kt, lanes] = jnp.exp2(x - m_new.astype(BF16))
            for r in range(NSA_REP):
                pv = jnp.dot(vt_ref[...], p_ref[r * kt:(r + 1) * kt, cols],
                             preferred_element_type=F32)
                hd = slice(r * dh, (r + 1) * dh)
                alpha = alpha_ref[r:r + 1, cols]
                acc_ref[hd, cols] = acc_ref[hd, cols] * alpha + pv[:dh]
                l_ref[r:r + 1, cols] = l_ref[r:r + 1, cols] * alpha + pv[dh:dh + 1]

    if selected:
        below_diagonal = (ki + 1) * kt <= qi * tq
        pl.when(below_diagonal)(lambda: body(False))
        pl.when(jnp.logical_not(below_diagonal))(lambda: body(True))
    else:
        body(True)

    @pl.when(last_ref[step] == 1)
    def _():
        for ch in range(tq // LANES):
            lanes = slice(ch * LANES, (ch + 1) * LANES)
            out_t = jnp.concatenate(
                [acc_ref[r * dh:(r + 1) * dh, lanes] / l_ref[r:r + 1, lanes] for r in range(NSA_REP)], axis=0)
            o_ref[lanes, :] = out_t.T


def _pair_tables(t, tq, kt, window):
    qi, ki, first, last = [], [], [], []
    for a in range(t // tq):
        lo = 0 if window is None else max(0, (a * tq - window + 1) // kt)
        hi = (a * tq + tq - 1) // kt
        for b in range(lo, hi + 1):
            qi.append(a)
            ki.append(b)
            first.append(int(b == lo))
            last.append(int(b == hi))
    return [jnp.asarray(np.asarray(x, np.int32)) for x in (qi, ki, first, last)]


def flash_branch(q_t, k, v_t, sel_t, tq, kt):
    qw, t = q_t.shape
    g, _, dh = k.shape
    selected = sel_t is not None
    nsel = sel_t.shape[1] if selected else 0
    tables = _pair_tables(t, tq, kt, None if selected else WINDOW)
    in_specs = [
        pl.BlockSpec((NSA_GW, tq), lambda gi, s, qi, ki, fi, la: (gi, qi[s])),
        pl.BlockSpec((None, kt, dh), lambda gi, s, qi, ki, fi, la: (gi, ki[s], 0)),
        pl.BlockSpec((None, dh, kt), lambda gi, s, qi, ki, fi, la: (gi, 0, ki[s])),
    ]
    args = [q_t, k, v_t]
    if selected:
        in_specs.append(pl.BlockSpec((None, nsel, tq), lambda gi, s, qi, ki, fi, la: (gi, 0, qi[s])))
        args.append(sel_t)
    return pl.pallas_call(
        functools.partial(_flash_kernel, tq=tq, kt=kt, nsel=nsel, selected=selected),
        grid_spec=pltpu.PrefetchScalarGridSpec(
            num_scalar_prefetch=4,
            grid=(g, int(tables[0].shape[0])),
            in_specs=in_specs,
            out_specs=pl.BlockSpec((tq, NSA_GW), lambda gi, s, qi, ki, fi, la: (qi[s], gi)),
            scratch_shapes=[
                pltpu.VMEM((8, tq), F32),
                pltpu.VMEM((8, tq), F32),
                pltpu.VMEM((8, tq), F32),
                pltpu.VMEM((NSA_GW, tq), F32),
                pltpu.VMEM((NSA_REP * kt, tq), BF16),
                pltpu.VMEM((NSA_REP * kt, tq), BF16),
                pltpu.VMEM((kt, tq), BF16),
                pltpu.VMEM((dh + FLASH_SUM_ROWS, kt), BF16),
                pltpu.VMEM((NSA_GW, tq), BF16),
            ],
        ),
        out_shape=jax.ShapeDtypeStruct((t, qw), F32),
        compiler_params=_cparams(("parallel", "arbitrary")),
        name="nsa_selected" if selected else "nsa_window",
    )(*tables, *args)


def _nsa_out_kernel(oc_ref, os_ref, ow_ref, gl_ref, eg_ref, w_ref, gpost_ref, h_ref, out_ref):
    pieces = _split3(_sigmoid(gl_ref[...]))
    mixed = jnp.zeros(oc_ref.shape, F32)
    for b, br_ref in enumerate((oc_ref, os_ref, ow_ref)):
        gfull = jnp.zeros(oc_ref.shape, F32)
        for piece in pieces:
            gfull = gfull + jnp.dot(piece, eg_ref[b], preferred_element_type=F32)
        mixed = mixed + gfull * br_ref[...]
    mix = jnp.dot(mixed.astype(BF16), w_ref[...], preferred_element_type=F32)
    out_ref[...] = h_ref[...] + _rms(mix, gpost_ref[...])


def nsa_out(oc, osel, ow, gate_logits, expand, w_o, gpost, h, tm=256):
    t, qw = oc.shape
    d = h.shape[1]
    assert t % tm == 0
    row = lambda w: pl.BlockSpec((tm, w), lambda i: (i, 0))
    return pl.pallas_call(
        _nsa_out_kernel,
        grid=(t // tm,),
        in_specs=[
            row(qw), row(qw), row(qw), row(LANES),
            pl.BlockSpec((3, LANES, qw), lambda i: (0, 0, 0)),
            pl.BlockSpec((qw, d), lambda i: (0, 0)),
            pl.BlockSpec((1, d), lambda i: (0, 0)),
            row(d),
        ],
        out_specs=row(d),
        out_shape=jax.ShapeDtypeStruct((t, d), F32),
        compiler_params=_cparams(("parallel",)),
        name="nsa_out",
    )(oc, osel, ow, gate_logits, expand, w_o, gpost.reshape(1, d), h)


def _pad_cols(w, n):
    return jnp.pad(w, ((0, 0), (0, n - w.shape[1])))


def _importance_matrix(nc, nsel):
    r = SEL_BLOCK // CMP_STRIDE
    c = CMP_BLOCK // CMP_STRIDE
    mat = np.zeros((nc, nsel), np.float32)
    for kblk in range(nsel):
        for m in range(r):
            for n in range(c):
                j = r * kblk + m - n
                if 0 <= j < nc - 1:
                    mat[j, kblk] += 1.0
    return jnp.asarray(mat, BF16)


def _gate_expand():
    e = np.zeros((3, LANES, NSA_GROUPS * NSA_GW), np.float32)
    for head in range(NSA_GROUPS * NSA_REP):
        for b in range(3):
            e[b, head * 3 + b, head * NSA_HEAD_DIM:(head + 1) * NSA_HEAD_DIM] = 1.0
    return jnp.asarray(e, BF16)


def _block_diag_kv(k_cmp, v_cmp):
    g, nc, dh = k_cmp.shape
    eye = jnp.eye(NSA_REP, dtype=bool)
    kct = jnp.swapaxes(k_cmp, 1, 2)
    kbd = jnp.where(eye[None, :, None, :, None], kct[:, None, :, None, :], 0.0)
    vbd = jnp.where(eye[None, :, None, :, None], v_cmp[:, None, :, None, :], 0.0)
    return (kbd.reshape(g, NSA_REP * dh, NSA_REP * nc).astype(BF16),
            vbd.reshape(g, NSA_REP * nc, NSA_REP * dh).astype(BF16))


def kernel(x, p, mix_pre_norm, mix_post_norm, ffn_pre_norm, ffn_post_norm, gdn_w_in, gdn_conv_w, gdn_a_log,
           gdn_dt_bias, gdn_o_norm, gdn_w_out, kv_norm, kv_w, cmp_pos, cmp_w1, cmp_w2, nsa_w_qg, nsa_w_o,
           ffn_w_in, ffn_w_out, ple_w_in, ple_w_gate):
    depth = p.shape[0]
    n_a = gdn_w_in.shape[0]
    t = x.shape[1]
    h = x[0]
    fh = ffn_w_out.shape[1]
    conv_w_cols = gdn_conv_w.shape[2]
    vw = GDN_V_HEADS * GDN_HEAD_DIM
    main_w = conv_w_cols + vw

    def channel_and_ple(h, i):
        w_in = ffn_w_in[i].astype(BF16)
        return ffn_ple(h, ffn_pre_norm[i], w_in[:, :fh], w_in[:, fh:], ffn_w_out[i].astype(BF16),
                       ffn_post_norm[i], p[i, 0], ple_w_in[i].astype(BF16), ple_w_gate[i].astype(BF16))

    for i in range(n_a):
        w_in = gdn_w_in[i]
        w_beta = w_in[:, main_w:main_w + GDN_V_HEADS]
        w_a = w_in[:, main_w + GDN_V_HEADS:]
        w_small = _pad_cols(jnp.concatenate([w_beta, w_a, w_a], axis=1), LANES).astype(BF16)
        proj = norm_matmul(h, mix_pre_norm[i], w_in[:, :main_w].astype(BF16))
        ba = norm_matmul(h, mix_pre_norm[i], w_small)
        pad_vec = lambda v: jnp.pad(v, (GDN_V_HEADS, LANES - 2 * GDN_V_HEADS))
        alog_vec = (pad_vec(gdn_a_log[i]) + jnp.pad(gdn_a_log[i], (2 * GDN_V_HEADS, LANES - 3 * GDN_V_HEADS)))
        dtb_vec = (pad_vec(gdn_dt_bias[i]) + jnp.pad(gdn_dt_bias[i], (2 * GDN_V_HEADS, LANES - 3 * GDN_V_HEADS)))
        slab = gdn_gates(ba, alog_vec.reshape(1, LANES), dtb_vec.reshape(1, LANES))
        qkv_hm = gdn_conv(proj, gdn_conv_w[i])
        kt_hm = jnp.swapaxes(qkv_hm[GDN_QK_HEADS:2 * GDN_QK_HEADS], 1, 2)
        o = gdn_scan(qkv_hm, kt_hm, slab, slab.T)
        onorm_tiled = jnp.tile(gdn_o_norm[i], GDN_V_HEADS).reshape(1, vw)
        h = gdn_out(o, proj, onorm_tiled, gdn_w_out[i].astype(BF16), mix_post_norm[i], h)
        h = channel_and_ple(h, i)

    g = NSA_GROUPS
    dh = NSA_HEAD_DIM
    kv = norm_matmul(h, kv_norm, kv_w.astype(BF16), tn=768)
    kv6 = jnp.transpose(kv.reshape(t, 6, g, dh), (1, 2, 0, 3))
    nc = t // CMP_STRIDE
    nsel = t // SEL_BLOCK
    x2 = kv6[0:2].reshape(2, g, nc, CMP_STRIDE * dh)
    pos2 = cmp_pos.reshape(2, 2, CMP_STRIDE * dh)
    cmp_out = compress(x2, pos2, cmp_w1.astype(BF16), cmp_w2.astype(BF16))
    vbd_t, kbd = _block_diag_kv(cmp_out[1], cmp_out[0])
    imp_mat_t = _importance_matrix(nc, nsel).T
    k_slc, v_slc_t = kv6[2].astype(BF16), jnp.swapaxes(kv6[3], 1, 2).astype(BF16)
    k_win, v_win_t = kv6[4].astype(BF16), jnp.swapaxes(kv6[5], 1, 2).astype(BF16)
    expand = _gate_expand()

    for i in range(n_a, depth):
        j = i - n_a
        qw = g * NSA_GW
        w_qg = nsa_w_qg[j]
        q = norm_matmul(h, mix_pre_norm[i], w_qg[:, :qw].astype(BF16))
        gate_logits = norm_matmul(h, mix_pre_norm[i], _pad_cols(w_qg[:, qw:], LANES).astype(BF16))
        q_t = q.T
        o_c, sel_t = cmp_topk(q_t, kbd, vbd_t, imp_mat_t)
        o_s = flash_branch(q_t, k_slc, v_slc_t, sel_t, tq=1024, kt=256)
        o_w = flash_branch(q_t, k_win, v_win_t, None, tq=512, kt=256)
        h = nsa_out(o_c, o_s, o_w, gate_logits, expand, nsa_w_o[j].astype(BF16), mix_post_norm[i], h)
        h = channel_and_ple(h, i)
    return h[None]
```

```python
import functools

import numpy as np
import jax
import jax.numpy as jnp
from jax import lax
from jax.experimental import pallas as pl
from jax.experimental.pallas import tpu as pltpu

F32 = jnp.float32
BF16 = jnp.bfloat16

NORM_EPS = 1e-6
L2_EPS = 1e-6
GDN_QK_HEADS = 8
GDN_V_HEADS = 16
GDN_HEAD_DIM = 128
GDN_CONV = 4
GDN_CHUNK = 64
GDN_GROUP = 256
GDN_SCAN_HEADS = 2
GDN_CONV_HEADS = 4
NSA_GROUPS = 4
NSA_REP = 4
NSA_HEAD_DIM = 64
NSA_GW = NSA_REP * NSA_HEAD_DIM
CMP_BLOCK = 32
CMP_STRIDE = 16
SEL_BLOCK = 64
SEL_TOPK = 16
WINDOW = 512
FORCED_SCORE = 1e4
LANES = 128
NEG_INIT = -(2.0 ** 100)
LOG2_E = 1.4426950408889634
FLASH_SUB = 256
FLASH_SUM_ROWS = 16
CMP_ROWS = 256

VMEM_LIMIT = 56 * 1024 * 1024


def _cparams(sem):
    return pltpu.CompilerParams(dimension_semantics=sem, vmem_limit_bytes=VMEM_LIMIT)


def _rms(x, gain):
    return x * lax.rsqrt(jnp.mean(x * x, axis=-1, keepdims=True) + NORM_EPS) * gain


def _mm(a, b):
    return jnp.dot(a.astype(BF16), b.astype(BF16), preferred_element_type=F32)


def _sigmoid(x):
    return 1.0 / (1.0 + jnp.exp(-x))


def _div_pow2(x, d):
    shift = d.bit_length() - 1
    assert d == 1 << shift
    return jnp.right_shift(x, shift)


def _split3(x):
    a = x.astype(BF16)
    r = x - a.astype(F32)
    b = r.astype(BF16)
    c = (r - b.astype(F32)).astype(BF16)
    return a, b, c


def _norm_matmul_kernel(x_ref, g_ref, w_ref, o_ref, xn_ref):
    @pl.when(pl.program_id(1) == 0)
    def _():
        xn_ref[...] = _rms(x_ref[...], g_ref[...]).astype(BF16)

    o_ref[...] = jnp.dot(xn_ref[...], w_ref[...], preferred_element_type=F32)


def norm_matmul(h, gain, w, tm=512, tn=1024):
    t, d = h.shape
    n = w.shape[1]
    tn = min(tn, n)
    assert t % tm == 0 and n % tn == 0
    return pl.pallas_call(
        _norm_matmul_kernel,
        grid=(t // tm, n // tn),
        in_specs=[
            pl.BlockSpec((tm, d), lambda i, j: (i, 0)),
            pl.BlockSpec((1, d), lambda i, j: (0, 0)),
            pl.BlockSpec((d, tn), lambda i, j: (0, j)),
        ],
        out_specs=pl.BlockSpec((tm, tn), lambda i, j: (i, j)),
        out_shape=jax.ShapeDtypeStruct((t, n), F32),
        scratch_shapes=[pltpu.VMEM((tm, d), BF16)],
        compiler_params=_cparams(("parallel", "arbitrary")),
        name="norm_matmul",
    )(h, gain.reshape(1, d), w)


def _ffn_ple_kernel(h_ref, gpre_ref, wg_ref, wu_ref, wo_ref, gpost_ref, p_ref, wple_ref, wgt_ref,
                    o_ref, xn_ref, acc_ref, *, nf):
    f = pl.program_id(1)

    @pl.when(f == 0)
    def _():
        xn_ref[...] = _rms(h_ref[...], gpre_ref[...]).astype(BF16)
        acc_ref[...] = jnp.zeros_like(acc_ref)

    xn = xn_ref[...]
    gate = jnp.dot(xn, wg_ref[...], preferred_element_type=F32)
    up = jnp.dot(xn, wu_ref[...], preferred_element_type=F32)
    act = gate * _sigmoid(gate) * up
    acc_ref[...] += jnp.dot(act.astype(BF16), wo_ref[...], preferred_element_type=F32)

    @pl.when(f == nf - 1)
    def _():
        h2 = h_ref[...] + _rms(acc_ref[...], gpost_ref[...])
        emb = jnp.dot(p_ref[...].astype(BF16), wple_ref[...], preferred_element_type=F32)
        gt = _sigmoid(jnp.dot(h2.astype(BF16), wgt_ref[...], preferred_element_type=F32))
        o_ref[...] = h2 + emb * gt


def ffn_ple(h, gpre, w_gate, w_up, w_out, gpost, p, w_ple, w_plegate, tm=512, tf=1408):
    t, d = h.shape
    fh = w_gate.shape[1]
    pd = p.shape[1]
    assert t % tm == 0 and fh % tf == 0
    nf = fh // tf
    return pl.pallas_call(
        functools.partial(_ffn_ple_kernel, nf=nf),
        grid=(t // tm, nf),
        in_specs=[
            pl.BlockSpec((tm, d), lambda i, f: (i, 0)),
            pl.BlockSpec((1, d), lambda i, f: (0, 0)),
            pl.BlockSpec((d, tf), lambda i, f: (0, f)),
            pl.BlockSpec((d, tf), lambda i, f: (0, f)),
            pl.BlockSpec((tf, d), lambda i, f: (f, 0)),
            pl.BlockSpec((1, d), lambda i, f: (0, 0)),
            pl.BlockSpec((tm, pd), lambda i, f: (i, 0)),
            pl.BlockSpec((pd, d), lambda i, f: (0, 0)),
            pl.BlockSpec((d, d), lambda i, f: (0, 0)),
        ],
        out_specs=pl.BlockSpec((tm, d), lambda i, f: (i, 0)),
        out_shape=jax.ShapeDtypeStruct((t, d), F32),
        scratch_shapes=[pltpu.VMEM((tm, d), BF16), pltpu.VMEM((tm, d), F32)],
        compiler_params=_cparams(("parallel", "arbitrary")),
        name="ffn_ple",
    )(h, gpre.reshape(1, d), w_gate, w_up, w_out, gpost.reshape(1, d), p, w_ple, w_plegate)


def _gdn_conv_kernel(x_ref, halo_ref, w_ref, o_ref, *, tm):
    c = pl.program_id(0)
    i = pl.program_id(1)
    x = x_ref[...]
    halo = jnp.where(i > 0, halo_ref[...], 0.0)
    ext = jnp.concatenate([halo, x], axis=0)
    w = w_ref[...]
    y = x * w[GDN_CONV - 1:GDN_CONV, :]
    for k in range(1, GDN_CONV):
        shifted = pltpu.roll(ext, k, axis=0)[8:8 + tm]
        y = y + shifted * w[GDN_CONV - 1 - k:GDN_CONV - k, :]
    y = y * _sigmoid(y)
    for hd in range(GDN_CONV_HEADS):
        head = c * GDN_CONV_HEADS + hd
        seg = y[:, hd * LANES:(hd + 1) * LANES]
        normed = seg * lax.rsqrt(jnp.sum(seg * seg, axis=-1, keepdims=True) + L2_EPS)
        q_scale = jnp.where(head < GDN_QK_HEADS, GDN_HEAD_DIM ** -0.5, 1.0)
        o_ref[hd] = jnp.where(head < 2 * GDN_QK_HEADS, normed * q_scale, seg)


def gdn_conv(proj, conv_w, tm=1024):
    t = proj.shape[0]
    n_tiles = conv_w.shape[1] // LANES
    cw = GDN_CONV_HEADS
    assert t % tm == 0 and n_tiles % cw == 0
    return pl.pallas_call(
        functools.partial(_gdn_conv_kernel, tm=tm),
        grid=(n_tiles // cw, t // tm),
        in_specs=[
            pl.BlockSpec((tm, cw * LANES), lambda c, i: (i, c)),
            pl.BlockSpec((8, cw * LANES), lambda c, i: (jnp.maximum(i * (tm // 8) - 1, 0), c)),
            pl.BlockSpec((GDN_CONV, cw * LANES), lambda c, i: (0, c)),
        ],
        out_specs=pl.BlockSpec((cw, tm, LANES), lambda c, i: (c, i, 0)),
        out_shape=jax.ShapeDtypeStruct((n_tiles, t, LANES), F32),
        compiler_params=_cparams(("parallel", "parallel")),
        name="gdn_conv",
    )(proj, proj, conv_w)


def _gdn_gate_kernel(x_ref, alog_ref, dtb_ref, lc_ref, lf_ref, o_ref):
    x = x_ref[...]
    lane = lax.broadcasted_iota(jnp.int32, x.shape, 1)
    beta = _sigmoid(x)
    z = x + dtb_ref[...]
    softplus = jnp.maximum(z, 0.0) + jnp.log(1.0 + jnp.exp(-jnp.abs(z)))
    g = -jnp.exp(alog_ref[...]) * softplus
    gcum = jnp.zeros_like(x)
    gtot = jnp.zeros_like(x)
    for piece in _split3(g):
        gcum = gcum + jnp.dot(lc_ref[...], piece, preferred_element_type=F32)
        gtot = gtot + jnp.dot(lf_ref[...], piece, preferred_element_type=F32)
    o_ref[...] = jnp.where(lane < GDN_V_HEADS, beta, jnp.where(lane < 2 * GDN_V_HEADS, gcum, gtot))


def gdn_gates(ba, alog_vec, dtb_vec):
    t = ba.shape[0]
    tm = GDN_GROUP
    r = np.arange(tm)
    same = (r[:, None] // GDN_CHUNK) == (r[None, :] // GDN_CHUNK)
    lc = jnp.asarray(same & (r[None, :] <= r[:, None]), BF16)
    lf = jnp.asarray(same, BF16)
    return pl.pallas_call(
        _gdn_gate_kernel,
        grid=(t // tm,),
        in_specs=[
            pl.BlockSpec((tm, LANES), lambda i: (i, 0)),
            pl.BlockSpec((1, LANES), lambda i: (0, 0)),
            pl.BlockSpec((1, LANES), lambda i: (0, 0)),
            pl.BlockSpec((tm, tm), lambda i: (0, 0)),
            pl.BlockSpec((tm, tm), lambda i: (0, 0)),
        ],
        out_specs=pl.BlockSpec((tm, LANES), lambda i: (i, 0)),
        out_shape=jax.ShapeDtypeStruct((t, LANES), F32),
        compiler_params=_cparams(("parallel",)),
        name="gdn_gates",
    )(ba, alog_vec, dtb_vec, lc, lf)


def _gdn_scan_kernel(q_ref, k_ref, kt_ref, v_ref, slab_ref, slabt_ref, o_ref, s_ref):
    n = pl.program_id(1)

    @pl.when(n == 0)
    def _():
        s_ref[...] = jnp.zeros_like(s_ref)

    L = GDN_GROUP
    C = GDN_CHUNK
    D = GDN_HEAD_DIM
    slab = slab_ref[...]
    slabt = slabt_ref[...]
    row = lax.broadcasted_iota(jnp.int32, (L, L), 0)
    col = lax.broadcasted_iota(jnp.int32, (L, L), 1)
    same = _div_pow2(row, C) == _div_pow2(col, C)
    causal = same & (col <= row)
    strict = same & (col < row)
    eye = (row == col).astype(F32)
    lane = lax.broadcasted_iota(jnp.int32, (L, LANES), 1)
    sub = lax.broadcasted_iota(jnp.int32, (LANES, L), 0)

    def column(idx):
        return jnp.sum(jnp.where(lane == idx, slab, 0.0), axis=1, keepdims=True)

    def rowvec(idx):
        return jnp.sum(jnp.where(sub == idx, slabt, 0.0), axis=0, keepdims=True)

    colk = lax.broadcasted_iota(jnp.int32, (D, L), 1)
    slots = range(2 * GDN_SCAN_HEADS)

    q = [q_ref[hq] for hq in range(GDN_SCAN_HEADS)]
    k = [k_ref[hq] for hq in range(GDN_SCAN_HEADS)]
    kt = [kt_ref[hq] for hq in range(GDN_SCAN_HEADS)]
    kk = [_mm(k[hq], kt[hq]) for hq in range(GDN_SCAN_HEADS)]
    qk = [_mm(q[hq], kt[hq]) for hq in range(GDN_SCAN_HEADS)]
    beta_c, gc_c, gt_c, decay, bp, inv, kdt = [], [], [], [], [], [], []
    for slot in slots:
        hq = slot // 2
        hv = 2 * (pl.program_id(0) * GDN_SCAN_HEADS + hq) + slot % 2
        beta_c.append(column(hv))
        gc_c.append(column(GDN_V_HEADS + hv))
        gt_c.append(column(2 * GDN_V_HEADS + hv))
        gc_r = rowvec(GDN_V_HEADS + hv)
        gt_r = rowvec(2 * GDN_V_HEADS + hv)
        decay.append(jnp.where(causal, jnp.exp(jnp.where(causal, gc_c[slot] - gc_r, 0.0)), 0.0))
        bp.append(jnp.where(strict, -(kk[hq] * beta_c[slot]) * decay[slot], 0.0))
        inv.append(eye + bp[slot])
        kdt.append(kt[hq] * jnp.exp(gt_r - gc_r))
    for _ in range(5):
        bp = [_mm(bp[slot], bp[slot]) for slot in slots]
        inv = [inv[slot] + _mm(inv[slot], bp[slot]) for slot in slots]
    u, w, qkm, q_dec = [], [], [], []
    for slot in slots:
        hq = slot // 2
        egc = jnp.exp(gc_c[slot])
        rhs = jnp.concatenate([v_ref[slot] * beta_c[slot], k[hq] * (beta_c[slot] * egc)], axis=1)
        sol = _mm(inv[slot], rhs)
        u.append(sol[:, :D])
        w.append(sol[:, D:])
        qkm.append(jnp.where(causal, qk[hq] * decay[slot], 0.0))
        q_dec.append(q[hq] * egc)
    state = [s_ref[slot] for slot in slots]
    v_done = [[] for _ in slots]
    for c in range(L // C):
        lo, hi = c * C, (c + 1) * C
        r = [_mm(jnp.concatenate([w[slot][lo:hi], q_dec[slot][lo:hi]], axis=0), state[slot]) for slot in slots]
        for slot in slots:
            v_done[slot].append(u[slot][lo:hi] - r[slot][:C])
            v_all = jnp.concatenate(v_done[slot] + [jnp.zeros((L - hi, D), F32)] * (hi < L), axis=0)
            o_ref[lo:hi, slot * D:(slot + 1) * D] = r[slot][C:] + _mm(qkm[slot][lo:hi, :], v_all)
            kdt_c = jnp.where((colk >= lo) & (colk < hi), kdt[slot], 0.0)
            state[slot] = state[slot] * jnp.exp(gt_c[slot][lo:lo + 1, :]) + _mm(kdt_c, v_all)
    for slot in slots:
        s_ref[slot] = state[slot]


def gdn_scan(qkv_hm, kt_hm, slab, slabt):
    t = qkv_hm.shape[1]
    L = GDN_GROUP
    D = GDN_HEAD_DIM
    hq = GDN_SCAN_HEADS
    assert t % L == 0 and GDN_QK_HEADS % hq == 0
    q_blocks = GDN_QK_HEADS // hq
    return pl.pallas_call(
        _gdn_scan_kernel,
        grid=(q_blocks, t // L),
        in_specs=[
            pl.BlockSpec((hq, L, D), lambda j, n: (j, n, 0)),
            pl.BlockSpec((hq, L, D), lambda j, n: (q_blocks + j, n, 0)),
            pl.BlockSpec((hq, D, L), lambda j, n: (j, 0, n)),
            pl.BlockSpec((2 * hq, L, D), lambda j, n: (q_blocks + j, n, 0)),
            pl.BlockSpec((L, LANES), lambda j, n: (n, 0)),
            pl.BlockSpec((LANES, L), lambda j, n: (0, n)),
        ],
        out_specs=pl.BlockSpec((L, 2 * hq * D), lambda j, n: (n, j)),
        out_shape=jax.ShapeDtypeStruct((t, GDN_V_HEADS * D), F32),
        scratch_shapes=[pltpu.VMEM((2 * hq, D, D), F32)],
        compiler_params=_cparams(("parallel", "arbitrary")),
        name="gdn_scan",
    )(qkv_hm, qkv_hm, kt_hm, qkv_hm, slab, slabt)


def _gdn_out_kernel(o_ref, z_ref, onorm_ref, w_ref, gpost_ref, h_ref, out_ref):
    o = o_ref[...]
    z = z_ref[...]
    parts = []
    for hd in range(GDN_V_HEADS):
        seg = o[:, hd * GDN_HEAD_DIM:(hd + 1) * GDN_HEAD_DIM]
        parts.append(seg * lax.rsqrt(jnp.mean(seg * seg, axis=-1, keepdims=True) + NORM_EPS))
    gated = jnp.concatenate(parts, axis=1) * onorm_ref[...] * (z * _sigmoid(z))
    mix = jnp.dot(gated.astype(BF16), w_ref[...], preferred_element_type=F32)
    out_ref[...] = h_ref[...] + _rms(mix, gpost_ref[...])


def gdn_out(o, proj, onorm_tiled, w_out, gpost, h, tm=256):
    t, vw = o.shape
    d = h.shape[1]
    z_blk = (proj.shape[1] - vw) // vw
    assert proj.shape[1] % vw == 0 and t % tm == 0
    return pl.pallas_call(
        _gdn_out_kernel,
        grid=(t // tm,),
        in_specs=[
            pl.BlockSpec((tm, vw), lambda i: (i, 0)),
            pl.BlockSpec((tm, vw), lambda i: (i, z_blk)),
            pl.BlockSpec((1, vw), lambda i: (0, 0)),
            pl.BlockSpec((vw, d), lambda i: (0, 0)),
            pl.BlockSpec((1, d), lambda i: (0, 0)),
            pl.BlockSpec((tm, d), lambda i: (i, 0)),
        ],
        out_specs=pl.BlockSpec((tm, d), lambda i: (i, 0)),
        out_shape=jax.ShapeDtypeStruct((t, d), F32),
        compiler_params=_cparams(("parallel",)),
        name="gdn_out",
    )(o, proj, onorm_tiled, w_out, gpost.reshape(1, d), h)


def _compress_kernel(x_ref, pos_ref, w1_ref, w2_ref, o_ref):
    x = x_ref[...]
    pos = pos_ref[...]
    nc, half = x.shape
    w1 = w1_ref[...]
    first = _mm(x + pos[0:1, :], w1[:half])
    second = _mm(x + pos[1:2, :], w1[half:])
    hid = first + pltpu.roll(second, nc - 1, axis=0)
    hid = hid * _sigmoid(hid)
    out = jnp.dot(hid.astype(BF16), w2_ref[...], preferred_element_type=F32)
    rowi = lax.broadcasted_iota(jnp.int32, out.shape, 0)
    o_ref[...] = jnp.where(rowi < nc - 1, out, 0.0)


def compress(x2, pos2, w1, w2):
    _, g, nc, wdt = x2.shape
    hid = w1.shape[2]
    dh = w2.shape[2]
    return pl.pallas_call(
        _compress_kernel,
        grid=(2, g),
        in_specs=[
            pl.BlockSpec((None, None, nc, wdt), lambda b, gi: (b, gi, 0, 0)),
            pl.BlockSpec((None, 2, wdt), lambda b, gi: (b, 0, 0)),
            pl.BlockSpec((None, 2 * wdt, hid), lambda b, gi: (b, 0, 0)),
            pl.BlockSpec((None, hid, dh), lambda b, gi: (b, 0, 0)),
        ],
        out_specs=pl.BlockSpec((None, None, nc, dh), lambda b, gi: (b, gi, 0, 0)),
        out_shape=jax.ShapeDtypeStruct((2, g, nc, dh), F32),
        compiler_params=_cparams(("parallel", "parallel")),
        name="nsa_compress",
    )(x2, pos2, w1, w2)


def _cmp_topk_kernel(q_ref, kbd_ref, vt_ref, m_ref, oc_ref, sel_ref, s_ref, p_ref, psum_ref, *,
                     tq, nc, nsel, topk):
    i = pl.program_id(1)
    dh = NSA_HEAD_DIM
    rows_per = min(nc, CMP_ROWS)
    q = (q_ref[...] * ((dh ** -0.5) * LOG2_E)).astype(BF16)
    s_ref[...] = jnp.dot(kbd_ref[...], q, preferred_element_type=F32)
    for ch in range(tq // LANES):
        lanes = slice(ch * LANES, (ch + 1) * LANES)
        tpos = i * tq + ch * LANES + lax.broadcasted_iota(jnp.int32, (rows_per, LANES), 1)
        cblk0 = lax.broadcasted_iota(jnp.int32, (rows_per, LANES), 0)
        masks = [(CMP_STRIDE * (cblk0 + c * rows_per) + CMP_BLOCK - 1) <= tpos for c in range(nc // rows_per)]
        for r in range(NSA_REP):
            pieces = [slice(r * nc + c * rows_per, r * nc + (c + 1) * rows_per) for c in range(nc // rows_per)]
            m = jnp.full((1, LANES), -jnp.inf, F32)
            for rows, mask in zip(pieces, masks):
                m = jnp.maximum(m, jnp.max(jnp.where(mask, s_ref[rows, lanes], -jnp.inf), axis=0, keepdims=True))
            m = jnp.where(m > -jnp.inf, m, 0.0)
            total = jnp.zeros((1, LANES), F32)
            for rows, mask in zip(pieces, masks):
                e = jnp.exp2(jnp.where(mask, s_ref[rows, lanes], -jnp.inf) - m)
                s_ref[rows, lanes] = e
                total = total + jnp.sum(e, axis=0, keepdims=True)
            inv = 1.0 / jnp.maximum(total, 1e-30)
            for c, rows in enumerate(pieces):
                p = s_ref[rows, lanes] * inv
                p_ref[rows, lanes] = p.astype(BF16)
                prow = slice(c * rows_per, (c + 1) * rows_per)
                psum_ref[prow, lanes] = p if r == 0 else psum_ref[prow, lanes] + p
    oc_t = jnp.concatenate(
        [jnp.dot(vt_ref[...], p_ref[r * nc:(r + 1) * nc, :], preferred_element_type=F32) for r in range(NSA_REP)],
        axis=0)
    imp = jnp.zeros((nsel, tq), F32)
    for piece in _split3(psum_ref[...]):
        imp = imp + jnp.dot(m_ref[...], piece, preferred_element_type=F32)
    for ch in range(tq // LANES):
        lanes = slice(ch * LANES, (ch + 1) * LANES)
        oc_ref[lanes, :] = oc_t[:, lanes].T
        t1 = i * tq + ch * LANES + lax.broadcasted_iota(jnp.int32, (nsel, LANES), 1)
        blk = lax.broadcasted_iota(jnp.int32, (nsel, LANES), 0)
        cur = _div_pow2(t1, SEL_BLOCK)
        forced = (blk == 0) | (blk == cur) | (blk == cur - 1)
        valid = blk * SEL_BLOCK <= t1
        score = jnp.where(valid, jnp.where(forced, FORCED_SCORE, imp[:, lanes]), -jnp.inf)
        blkf = blk.astype(F32)
        work = score
        for _ in range(topk):
            mx = jnp.max(work, axis=0, keepdims=True)
            first = jnp.min(jnp.where(work == mx, blkf, float(nsel)), axis=0, keepdims=True)
            work = jnp.where(blkf == first, -jnp.inf, work)
        sel_ref[:, lanes] = jnp.where((score > -jnp.inf) & (work == -jnp.inf), 1.0, 0.0).astype(sel_ref.dtype)


def cmp_topk(q_t, kbd, v_t, imp_mat_t, tq=256):
    qw, t = q_t.shape
    g = kbd.shape[0]
    nc = kbd.shape[1] // NSA_REP
    nsel = imp_mat_t.shape[0]
    topk = min(SEL_TOPK, nsel)
    assert t % tq == 0 and nc % min(nc, CMP_ROWS) == 0
    return pl.pallas_call(
        functools.partial(_cmp_topk_kernel, tq=tq, nc=nc, nsel=nsel, topk=topk),
        grid=(g, t // tq),
        in_specs=[
            pl.BlockSpec((NSA_GW, tq), lambda gi, i: (gi, i)),
            pl.BlockSpec((None, NSA_REP * nc, NSA_GW), lambda gi, i: (gi, 0, 0)),
            pl.BlockSpec((None, NSA_HEAD_DIM, nc), lambda gi, i: (gi, 0, 0)),
            pl.BlockSpec((nsel, nc), lambda gi, i: (0, 0)),
        ],
        out_specs=[
            pl.BlockSpec((tq, NSA_GW), lambda gi, i: (i, gi)),
            pl.BlockSpec((None, nsel, tq), lambda gi, i: (gi, 0, i)),
        ],
        out_shape=[
            jax.ShapeDtypeStruct((t, qw), F32),
            jax.ShapeDtypeStruct((g, nsel, t), F32),
        ],
        scratch_shapes=[
            pltpu.VMEM((NSA_REP * nc, tq), F32),
            pltpu.VMEM((NSA_REP * nc, tq), BF16),
            pltpu.VMEM((nc, tq), F32),
        ],
        compiler_params=_cparams(("parallel", "parallel")),
        name="nsa_cmp_topk",
    )(q_t, kbd, v_t, imp_mat_t)


def _flash_kernel(qi_ref, ki_ref, first_ref, last_ref, *refs, tq, kt, nsel, selected):
    if selected:
        q_ref, k_ref, v_ref, sel_ref, o_ref = refs[:5]
    else:
        q_ref, k_ref, v_ref, o_ref = refs[:4]
    m_ref, l_ref, alpha_ref, acc_ref, s_ref, p_ref, bias_ref, vt_ref, qs_ref = refs[-9:]
    step = pl.program_id(1)
    qi = qi_ref[step]
    ki = ki_ref[step]
    dh = NSA_HEAD_DIM

    @pl.when(first_ref[step] == 1)
    def _():
        m_ref[...] = jnp.full_like(m_ref, NEG_INIT)
        l_ref[...] = jnp.zeros_like(l_ref)
        acc_ref[...] = jnp.zeros_like(acc_ref)
        qs_ref[...] = (q_ref[...] * ((dh ** -0.5) * LOG2_E)).astype(BF16)

    vt_ref[:dh, :] = v_ref[...]
    vt_ref[dh:, :] = jnp.ones((FLASH_SUM_ROWS, kt), BF16)

    n_sub = tq // FLASH_SUB
    n_blk = kt // SEL_BLOCK

    def body(positional):
        def scores(sub):
            c0 = sub * FLASH_SUB
            cols = slice(c0, c0 + FLASH_SUB)
            for r in range(NSA_REP):
                s_ref[r * kt:(r + 1) * kt, cols] = jnp.dot(
                    k_ref[...], qs_ref[r * dh:(r + 1) * dh, cols], preferred_element_type=F32)
            for jb in range(n_blk):
                rows = slice(jb * SEL_BLOCK, (jb + 1) * SEL_BLOCK)
                if selected:
                    picked = sel_ref[pl.ds(ki * n_blk + jb, 1), cols] > 0.5
                if positional:
                    tpos = qi * tq + c0 + lax.broadcasted_iota(jnp.int32, (SEL_BLOCK, FLASH_SUB), 1)
                    kpos = ki * kt + jb * SEL_BLOCK + lax.broadcasted_iota(jnp.int32, (SEL_BLOCK, FLASH_SUB), 0)
                    if selected:
                        allowed = picked & (kpos <= tpos)
                    else:
                        allowed = (kpos <= tpos) & (kpos > tpos - WINDOW)
                else:
                    allowed = jnp.broadcast_to(picked, (SEL_BLOCK, FLASH_SUB))
                bias_ref[rows, cols] = jnp.where(allowed, 0.0, -jnp.inf)

        scores(0)
        for sub in range(n_sub):
            c0 = sub * FLASH_SUB
            cols = slice(c0, c0 + FLASH_SUB)
            if sub + 1 < n_sub:
                scores(sub + 1)
            for ch in range(FLASH_SUB // LANES):
                lanes = slice(c0 + ch * LANES, c0 + (ch + 1) * LANES)
                bias = bias_ref[:, lanes]
                for r in range(NSA_REP):
                    x = s_ref[r * kt:(r + 1) * kt, lanes] + bias
                    m_prev = m_ref[r:r + 1, lanes]
                    m_new = jnp.maximum(m_prev, jnp.max(x, axis=0, keepdims=True))
                    m_ref[r:r + 1, lanes] = m_new
                    alpha_ref[r:r + 1, lanes] = jnp.exp2(m_prev - m_new)
                    p_ref[r * kt:(r + 1) * kt, lanes] = jnp.exp2(x - m_new).astype(BF16)
            for r in range(NSA_REP):
                pv = jnp.dot(vt_ref[...], p_ref[r * kt:(r + 1) * kt, cols],
                             preferred_element_type=F32)
                hd = slice(r * dh, (r + 1) * dh)
                alpha = alpha_ref[r:r + 1, cols]
                acc_ref[hd, cols] = acc_ref[hd, cols] * alpha + pv[:dh]
                l_ref[r:r + 1, cols] = l_ref[r:r + 1, cols] * alpha + pv[dh:dh + 1]

    if selected:
        below_diagonal = (ki + 1) * kt <= qi * tq
        pl.when(below_diagonal)(lambda: body(False))
        pl.when(jnp.logical_not(below_diagonal))(lambda: body(True))
    else:
        body(True)

    @pl.when(last_ref[step] == 1)
    def _():
        for ch in range(tq // LANES):
            lanes = slice(ch * LANES, (ch + 1) * LANES)
            out_t = jnp.concatenate(
                [acc_ref[r * dh:(r + 1) * dh, lanes] / l_ref[r:r + 1, lanes] for r in range(NSA_REP)], axis=0)
            o_ref[lanes, :] = out_t.T


def _pair_tables(t, tq, kt, window):
    qi, ki, first, last = [], [], [], []
    for a in range(t // tq):
        lo = 0 if window is None else max(0, (a * tq - window + 1) // kt)
        hi = (a * tq + tq - 1) // kt
        for b in range(lo, hi + 1):
            qi.append(a)
            ki.append(b)
            first.append(int(b == lo))
            last.append(int(b == hi))
    return [jnp.asarray(np.asarray(x, np.int32)) for x in (qi, ki, first, last)]


def flash_branch(q_t, k, v_t, sel_t, tq, kt):
    qw, t = q_t.shape
    g, _, dh = k.shape
    selected = sel_t is not None
    nsel = sel_t.shape[1] if selected else 0
    tables = _pair_tables(t, tq, kt, None if selected else WINDOW)
    in_specs = [
        pl.BlockSpec((NSA_GW, tq), lambda gi, s, qi, ki, fi, la: (gi, qi[s])),
        pl.BlockSpec((None, kt, dh), lambda gi, s, qi, ki, fi, la: (gi, ki[s], 0)),
        pl.BlockSpec((None, dh, kt), lambda gi, s, qi, ki, fi, la: (gi, 0, ki[s])),
    ]
    args = [q_t, k, v_t]
    if selected:
        in_specs.append(pl.BlockSpec((None, nsel, tq), lambda gi, s, qi, ki, fi, la: (gi, 0, qi[s])))
        args.append(sel_t)
    return pl.pallas_call(
        functools.partial(_flash_kernel, tq=tq, kt=kt, nsel=nsel, selected=selected),
        grid_spec=pltpu.PrefetchScalarGridSpec(
            num_scalar_prefetch=4,
            grid=(g, int(tables[0].shape[0])),
            in_specs=in_specs,
            out_specs=pl.BlockSpec((tq, NSA_GW), lambda gi, s, qi, ki, fi, la: (qi[s], gi)),
            scratch_shapes=[
                pltpu.VMEM((8, tq), F32),
                pltpu.VMEM((8, tq), F32),
                pltpu.VMEM((8, tq), F32),
                pltpu.VMEM((NSA_GW, tq), F32),
                pltpu.VMEM((NSA_REP * kt, tq), F32),
                pltpu.VMEM((NSA_REP * kt, tq), BF16),
                pltpu.VMEM((kt, tq), F32),
                pltpu.VMEM((dh + FLASH_SUM_ROWS, kt), BF16),
                pltpu.VMEM((NSA_GW, tq), BF16),
            ],
        ),
        out_shape=jax.ShapeDtypeStruct((t, qw), F32),
        compiler_params=_cparams(("parallel", "arbitrary")),
        name="nsa_selected" if selected else "nsa_window",
    )(*tables, *args)


def _nsa_out_kernel(oc_ref, os_ref, ow_ref, gl_ref, eg_ref, w_ref, gpost_ref, h_ref, out_ref):
    pieces = _split3(_sigmoid(gl_ref[...]))
    mixed = jnp.zeros(oc_ref.shape, F32)
    for b, br_ref in enumerate((oc_ref, os_ref, ow_ref)):
        gfull = jnp.zeros(oc_ref.shape, F32)
        for piece in pieces:
            gfull = gfull + jnp.dot(piece, eg_ref[b], preferred_element_type=F32)
        mixed = mixed + gfull * br_ref[...]
    mix = jnp.dot(mixed.astype(BF16), w_ref[...], preferred_element_type=F32)
    out_ref[...] = h_ref[...] + _rms(mix, gpost_ref[...])


def nsa_out(oc, osel, ow, gate_logits, expand, w_o, gpost, h, tm=256):
    t, qw = oc.shape
    d = h.shape[1]
    assert t % tm == 0
    row = lambda w: pl.BlockSpec((tm, w), lambda i: (i, 0))
    return pl.pallas_call(
        _nsa_out_kernel,
        grid=(t // tm,),
        in_specs=[
            row(qw), row(qw), row(qw), row(LANES),
            pl.BlockSpec((3, LANES, qw), lambda i: (0, 0, 0)),
            pl.BlockSpec((qw, d), lambda i: (0, 0)),
            pl.BlockSpec((1, d), lambda i: (0, 0)),
            row(d),
        ],
        out_specs=row(d),
        out_shape=jax.ShapeDtypeStruct((t, d), F32),
        compiler_params=_cparams(("parallel",)),
        name="nsa_out",
    )(oc, osel, ow, gate_logits, expand, w_o, gpost.reshape(1, d), h)


def _pad_cols(w, n):
    return jnp.pad(w, ((0, 0), (0, n - w.shape[1])))


def _importance_matrix(nc, nsel):
    r = SEL_BLOCK // CMP_STRIDE
    c = CMP_BLOCK // CMP_STRIDE
    mat = np.zeros((nc, nsel), np.float32)
    for kblk in range(nsel):
        for m in range(r):
            for n in range(c):
                j = r * kblk + m - n
                if 0 <= j < nc - 1:
                    mat[j, kblk] += 1.0
    return jnp.asarray(mat, BF16)


def _gate_expand():
    e = np.zeros((3, LANES, NSA_GROUPS * NSA_GW), np.float32)
    for head in range(NSA_GROUPS * NSA_REP):
        for b in range(3):
            e[b, head * 3 + b, head * NSA_HEAD_DIM:(head + 1) * NSA_HEAD_DIM] = 1.0
    return jnp.asarray(e, BF16)


def _block_diag_kv(k_cmp, v_cmp):
    g, nc, dh = k_cmp.shape
    eye = jnp.eye(NSA_REP, dtype=bool)
    kct = jnp.swapaxes(k_cmp, 1, 2)
    kbd = jnp.where(eye[None, :, None, :, None], kct[:, None, :, None, :], 0.0)
    vbd = jnp.where(eye[None, :, None, :, None], v_cmp[:, None, :, None, :], 0.0)
    return (kbd.reshape(g, NSA_REP * dh, NSA_REP * nc).astype(BF16),
            vbd.reshape(g, NSA_REP * nc, NSA_REP * dh).astype(BF16))


def kernel(x, p, mix_pre_norm, mix_post_norm, ffn_pre_norm, ffn_post_norm, gdn_w_in, gdn_conv_w, gdn_a_log,
           gdn_dt_bias, gdn_o_norm, gdn_w_out, kv_norm, kv_w, cmp_pos, cmp_w1, cmp_w2, nsa_w_qg, nsa_w_o,
           ffn_w_in, ffn_w_out, ple_w_in, ple_w_gate):
    depth = p.shape[0]
    n_a = gdn_w_in.shape[0]
    t = x.shape[1]
    h = x[0]
    fh = ffn_w_out.shape[1]
    conv_w_cols = gdn_conv_w.shape[2]
    vw = GDN_V_HEADS * GDN_HEAD_DIM
    main_w = conv_w_cols + vw

    def channel_and_ple(h, i):
        w_in = ffn_w_in[i].astype(BF16)
        return ffn_ple(h, ffn_pre_norm[i], w_in[:, :fh], w_in[:, fh:], ffn_w_out[i].astype(BF16),
                       ffn_post_norm[i], p[i, 0], ple_w_in[i].astype(BF16), ple_w_gate[i].astype(BF16))

    for i in range(n_a):
        w_in = gdn_w_in[i]
        w_beta = w_in[:, main_w:main_w + GDN_V_HEADS]
        w_a = w_in[:, main_w + GDN_V_HEADS:]
        w_small = _pad_cols(jnp.concatenate([w_beta, w_a, w_a], axis=1), LANES).astype(BF16)
        proj = norm_matmul(h, mix_pre_norm[i], w_in[:, :main_w].astype(BF16))
        ba = norm_matmul(h, mix_pre_norm[i], w_small)
        pad_vec = lambda v: jnp.pad(v, (GDN_V_HEADS, LANES - 2 * GDN_V_HEADS))
        alog_vec = (pad_vec(gdn_a_log[i]) + jnp.pad(gdn_a_log[i], (2 * GDN_V_HEADS, LANES - 3 * GDN_V_HEADS)))
        dtb_vec = (pad_vec(gdn_dt_bias[i]) + jnp.pad(gdn_dt_bias[i], (2 * GDN_V_HEADS, LANES - 3 * GDN_V_HEADS)))
        slab = gdn_gates(ba, alog_vec.reshape(1, LANES), dtb_vec.reshape(1, LANES))
        qkv_hm = gdn_conv(proj, gdn_conv_w[i])
        kt_hm = jnp.swapaxes(qkv_hm[GDN_QK_HEADS:2 * GDN_QK_HEADS], 1, 2)
        o = gdn_scan(qkv_hm, kt_hm, slab, slab.T)
        onorm_tiled = jnp.tile(gdn_o_norm[i], GDN_V_HEADS).reshape(1, vw)
        h = gdn_out(o, proj, onorm_tiled, gdn_w_out[i].astype(BF16), mix_post_norm[i], h)
        h = channel_and_ple(h, i)

    g = NSA_GROUPS
    dh = NSA_HEAD_DIM
    kv = norm_matmul(h, kv_norm, kv_w.astype(BF16), tn=768)
    kv6 = jnp.transpose(kv.reshape(t, 6, g, dh), (1, 2, 0, 3))
    nc = t // CMP_STRIDE
    nsel = t // SEL_BLOCK
    x2 = kv6[0:2].reshape(2, g, nc, CMP_STRIDE * dh)
    pos2 = cmp_pos.reshape(2, 2, CMP_STRIDE * dh)
    cmp_out = compress(x2, pos2, cmp_w1.astype(BF16), cmp_w2.astype(BF16))
    _, kbd = _block_diag_kv(cmp_out[1], cmp_out[0])
    v_cmp_t = jnp.swapaxes(cmp_out[1], 1, 2).astype(BF16)
    imp_mat_t = _importance_matrix(nc, nsel).T
    k_slc, v_slc_t = kv6[2].astype(BF16), jnp.swapaxes(kv6[3], 1, 2).astype(BF16)
    k_win, v_win_t = kv6[4].astype(BF16), jnp.swapaxes(kv6[5], 1, 2).astype(BF16)
    expand = _gate_expand()

    for i in range(n_a, depth):
        j = i - n_a
        qw = g * NSA_GW
        w_qg = nsa_w_qg[j]
        q = norm_matmul(h, mix_pre_norm[i], w_qg[:, :qw].astype(BF16))
        gate_logits = norm_matmul(h, mix_pre_norm[i], _pad_cols(w_qg[:, qw:], LANES).astype(BF16))
        q_t = q.T
        o_c, sel_t = cmp_topk(q_t, kbd, v_cmp_t, imp_mat_t)
        o_s = flash_branch(q_t, k_slc, v_slc_t, sel_t, tq=1024, kt=256)
        o_w = flash_branch(q_t, k_win, v_win_t, None, tq=512, kt=256)
        h = nsa_out(o_c, o_s, o_w, gate_logits, expand, nsa_w_o[j].astype(BF16), mix_post_norm[i], h)
        h = channel_and_ple(h, i)
    return h[None]
```

```python
import functools

import numpy as np
import jax
import jax.numpy as jnp
from jax import lax
from jax.experimental import pallas as pl
from jax.experimental.pallas import tpu as pltpu

F32 = jnp.float32
BF16 = jnp.bfloat16

NORM_EPS = 1e-6
L2_EPS = 1e-6
GDN_QK_HEADS = 8
GDN_V_HEADS = 16
GDN_HEAD_DIM = 128
GDN_CONV = 4
GDN_CHUNK = 64
GDN_GROUP = 256
GDN_SCAN_HEADS = 4
GDN_CONV_HEADS = 4
NSA_GROUPS = 4
NSA_REP = 4
NSA_HEAD_DIM = 64
NSA_GW = NSA_REP * NSA_HEAD_DIM
CMP_BLOCK = 32
CMP_STRIDE = 16
SEL_BLOCK = 64
SEL_TOPK = 16
WINDOW = 512
FORCED_SCORE = 1e4
LANES = 128
NEG_INIT = -(2.0 ** 100)
LOG2_E = 1.4426950408889634
FLASH_SUB = 256
FLASH_SUM_ROWS = 16
CMP_ROWS = 256

VMEM_LIMIT = 56 * 1024 * 1024


def _cparams(sem):
    return pltpu.CompilerParams(dimension_semantics=sem, vmem_limit_bytes=VMEM_LIMIT)


def _rms(x, gain):
    return x * lax.rsqrt(jnp.mean(x * x, axis=-1, keepdims=True) + NORM_EPS) * gain


def _mm(a, b):
    return jnp.dot(a.astype(BF16), b.astype(BF16), preferred_element_type=F32)


def _sigmoid(x):
    return 1.0 / (1.0 + jnp.exp(-x))


def _div_pow2(x, d):
    shift = d.bit_length() - 1
    assert d == 1 << shift
    return jnp.right_shift(x, shift)


def _split3(x):
    a = x.astype(BF16)
    r = x - a.astype(F32)
    b = r.astype(BF16)
    c = (r - b.astype(F32)).astype(BF16)
    return a, b, c


def _norm_matmul_kernel(x_ref, g_ref, w_ref, o_ref, xn_ref):
    @pl.when(pl.program_id(1) == 0)
    def _():
        xn_ref[...] = _rms(x_ref[...], g_ref[...]).astype(BF16)

    o_ref[...] = jnp.dot(xn_ref[...], w_ref[...], preferred_element_type=F32)


def norm_matmul(h, gain, w, tm=512, tn=1024):
    t, d = h.shape
    n = w.shape[1]
    tn = min(tn, n)
    assert t % tm == 0 and n % tn == 0
    return pl.pallas_call(
        _norm_matmul_kernel,
        grid=(t // tm, n // tn),
        in_specs=[
            pl.BlockSpec((tm, d), lambda i, j: (i, 0)),
            pl.BlockSpec((1, d), lambda i, j: (0, 0)),
            pl.BlockSpec((d, tn), lambda i, j: (0, j)),
        ],
        out_specs=pl.BlockSpec((tm, tn), lambda i, j: (i, j)),
        out_shape=jax.ShapeDtypeStruct((t, n), F32),
        scratch_shapes=[pltpu.VMEM((tm, d), BF16)],
        compiler_params=_cparams(("parallel", "arbitrary")),
        name="norm_matmul",
    )(h, gain.reshape(1, d), w)


def _ffn_ple_kernel(h_ref, gpre_ref, wg_ref, wu_ref, wo_ref, gpost_ref, p_ref, wple_ref, wgt_ref,
                    o_ref, xn_ref, acc_ref, *, nf):
    f = pl.program_id(1)

    @pl.when(f == 0)
    def _():
        xn_ref[...] = _rms(h_ref[...], gpre_ref[...]).astype(BF16)
        acc_ref[...] = jnp.zeros_like(acc_ref)

    xn = xn_ref[...]
    gate = jnp.dot(xn, wg_ref[...], preferred_element_type=F32)
    up = jnp.dot(xn, wu_ref[...], preferred_element_type=F32)
    act = gate * _sigmoid(gate) * up
    acc_ref[...] += jnp.dot(act.astype(BF16), wo_ref[...], preferred_element_type=F32)

    @pl.when(f == nf - 1)
    def _():
        h2 = h_ref[...] + _rms(acc_ref[...], gpost_ref[...])
        emb = jnp.dot(p_ref[...].astype(BF16), wple_ref[...], preferred_element_type=F32)
        gt = _sigmoid(jnp.dot(h2.astype(BF16), wgt_ref[...], preferred_element_type=F32))
        o_ref[...] = h2 + emb * gt


def ffn_ple(h, gpre, w_gate, w_up, w_out, gpost, p, w_ple, w_plegate, tm=512, tf=1408):
    t, d = h.shape
    fh = w_gate.shape[1]
    pd = p.shape[1]
    assert t % tm == 0 and fh % tf == 0
    nf = fh // tf
    return pl.pallas_call(
        functools.partial(_ffn_ple_kernel, nf=nf),
        grid=(t // tm, nf),
        in_specs=[
            pl.BlockSpec((tm, d), lambda i, f: (i, 0)),
            pl.BlockSpec((1, d), lambda i, f: (0, 0)),
            pl.BlockSpec((d, tf), lambda i, f: (0, f)),
            pl.BlockSpec((d, tf), lambda i, f: (0, f)),
            pl.BlockSpec((tf, d), lambda i, f: (f, 0)),
            pl.BlockSpec((1, d), lambda i, f: (0, 0)),
            pl.BlockSpec((tm, pd), lambda i, f: (i, 0)),
            pl.BlockSpec((pd, d), lambda i, f: (0, 0)),
            pl.BlockSpec((d, d), lambda i, f: (0, 0)),
        ],
        out_specs=pl.BlockSpec((tm, d), lambda i, f: (i, 0)),
        out_shape=jax.ShapeDtypeStruct((t, d), F32),
        scratch_shapes=[pltpu.VMEM((tm, d), BF16), pltpu.VMEM((tm, d), F32)],
        compiler_params=_cparams(("parallel", "arbitrary")),
        name="ffn_ple",
    )(h, gpre.reshape(1, d), w_gate, w_up, w_out, gpost.reshape(1, d), p, w_ple, w_plegate)


def _gdn_conv_kernel(x_ref, halo_ref, w_ref, o_ref, *, tm):
    c = pl.program_id(0)
    i = pl.program_id(1)
    x = x_ref[...]
    halo = jnp.where(i > 0, halo_ref[...], 0.0)
    ext = jnp.concatenate([halo, x], axis=0)
    w = w_ref[...]
    y = x * w[GDN_CONV - 1:GDN_CONV, :]
    for k in range(1, GDN_CONV):
        shifted = pltpu.roll(ext, k, axis=0)[8:8 + tm]
        y = y + shifted * w[GDN_CONV - 1 - k:GDN_CONV - k, :]
    y = y * _sigmoid(y)
    for hd in range(GDN_CONV_HEADS):
        head = c * GDN_CONV_HEADS + hd
        seg = y[:, hd * LANES:(hd + 1) * LANES]
        normed = seg * lax.rsqrt(jnp.sum(seg * seg, axis=-1, keepdims=True) + L2_EPS)
        q_scale = jnp.where(head < GDN_QK_HEADS, GDN_HEAD_DIM ** -0.5, 1.0)
        o_ref[hd] = jnp.where(head < 2 * GDN_QK_HEADS, normed * q_scale, seg)


def gdn_conv(proj, conv_w, tm=1024):
    t = proj.shape[0]
    n_tiles = conv_w.shape[1] // LANES
    cw = GDN_CONV_HEADS
    assert t % tm == 0 and n_tiles % cw == 0
    return pl.pallas_call(
        functools.partial(_gdn_conv_kernel, tm=tm),
        grid=(n_tiles // cw, t // tm),
        in_specs=[
            pl.BlockSpec((tm, cw * LANES), lambda c, i: (i, c)),
            pl.BlockSpec((8, cw * LANES), lambda c, i: (jnp.maximum(i * (tm // 8) - 1, 0), c)),
            pl.BlockSpec((GDN_CONV, cw * LANES), lambda c, i: (0, c)),
        ],
        out_specs=pl.BlockSpec((cw, tm, LANES), lambda c, i: (c, i, 0)),
        out_shape=jax.ShapeDtypeStruct((n_tiles, t, LANES), F32),
        compiler_params=_cparams(("parallel", "parallel")),
        name="gdn_conv",
    )(proj, proj, conv_w)


def _gdn_gate_kernel(x_ref, alog_ref, dtb_ref, lc_ref, lf_ref, o_ref):
    x = x_ref[...]
    lane = lax.broadcasted_iota(jnp.int32, x.shape, 1)
    beta = _sigmoid(x)
    z = x + dtb_ref[...]
    softplus = jnp.maximum(z, 0.0) + jnp.log(1.0 + jnp.exp(-jnp.abs(z)))
    g = -jnp.exp(alog_ref[...]) * softplus
    gcum = jnp.zeros_like(x)
    gtot = jnp.zeros_like(x)
    for piece in _split3(g):
        gcum = gcum + jnp.dot(lc_ref[...], piece, preferred_element_type=F32)
        gtot = gtot + jnp.dot(lf_ref[...], piece, preferred_element_type=F32)
    o_ref[...] = jnp.where(lane < GDN_V_HEADS, beta, jnp.where(lane < 2 * GDN_V_HEADS, gcum, gtot))


def gdn_gates(ba, alog_vec, dtb_vec):
    t = ba.shape[0]
    tm = GDN_GROUP
    r = np.arange(tm)
    same = (r[:, None] // GDN_CHUNK) == (r[None, :] // GDN_CHUNK)
    lc = jnp.asarray(same & (r[None, :] <= r[:, None]), BF16)
    lf = jnp.asarray(same, BF16)
    return pl.pallas_call(
        _gdn_gate_kernel,
        grid=(t // tm,),
        in_specs=[
            pl.BlockSpec((tm, LANES), lambda i: (i, 0)),
            pl.BlockSpec((1, LANES), lambda i: (0, 0)),
            pl.BlockSpec((1, LANES), lambda i: (0, 0)),
            pl.BlockSpec((tm, tm), lambda i: (0, 0)),
            pl.BlockSpec((tm, tm), lambda i: (0, 0)),
        ],
        out_specs=pl.BlockSpec((tm, LANES), lambda i: (i, 0)),
        out_shape=jax.ShapeDtypeStruct((t, LANES), F32),
        compiler_params=_cparams(("parallel",)),
        name="gdn_gates",
    )(ba, alog_vec, dtb_vec, lc, lf)


def _gdn_scan_kernel(q_ref, k_ref, kt_ref, v_ref, slab_ref, slabt_ref, o_ref, s_ref):
    n = pl.program_id(1)

    @pl.when(n == 0)
    def _():
        s_ref[...] = jnp.zeros_like(s_ref)

    L = GDN_GROUP
    C = GDN_CHUNK
    D = GDN_HEAD_DIM
    slab = slab_ref[...]
    slabt = slabt_ref[...]
    row = lax.broadcasted_iota(jnp.int32, (L, L), 0)
    col = lax.broadcasted_iota(jnp.int32, (L, L), 1)
    same = _div_pow2(row, C) == _div_pow2(col, C)
    causal = same & (col <= row)
    strict = same & (col < row)
    eye = (row == col).astype(F32)
    lane = lax.broadcasted_iota(jnp.int32, (L, LANES), 1)
    sub = lax.broadcasted_iota(jnp.int32, (LANES, L), 0)

    def column(idx):
        return jnp.sum(jnp.where(lane == idx, slab, 0.0), axis=1, keepdims=True)

    def rowvec(idx):
        return jnp.sum(jnp.where(sub == idx, slabt, 0.0), axis=0, keepdims=True)

    colk = lax.broadcasted_iota(jnp.int32, (D, L), 1)
    slots = range(2 * GDN_SCAN_HEADS)

    q = [q_ref[hq] for hq in range(GDN_SCAN_HEADS)]
    k = [k_ref[hq] for hq in range(GDN_SCAN_HEADS)]
    kt = [kt_ref[hq] for hq in range(GDN_SCAN_HEADS)]
    kk = [_mm(k[hq], kt[hq]) for hq in range(GDN_SCAN_HEADS)]
    qk = [_mm(q[hq], kt[hq]) for hq in range(GDN_SCAN_HEADS)]
    beta_c, gc_c, gt_c, decay, bp, inv, kdt = [], [], [], [], [], [], []
    for slot in slots:
        hq = slot // 2
        hv = 2 * (pl.program_id(0) * GDN_SCAN_HEADS + hq) + slot % 2
        beta_c.append(column(hv))
        gc_c.append(column(GDN_V_HEADS + hv))
        gt_c.append(column(2 * GDN_V_HEADS + hv))
        gc_r = rowvec(GDN_V_HEADS + hv)
        gt_r = rowvec(2 * GDN_V_HEADS + hv)
        decay.append(jnp.where(causal, jnp.exp(jnp.where(causal, gc_c[slot] - gc_r, 0.0)), 0.0))
        bp.append(jnp.where(strict, -(kk[hq] * beta_c[slot]) * decay[slot], 0.0))
        inv.append(eye + bp[slot])
        kdt.append(kt[hq] * jnp.exp(gt_r - gc_r))
    for _ in range(5):
        bp = [_mm(bp[slot], bp[slot]) for slot in slots]
        inv = [inv[slot] + _mm(inv[slot], bp[slot]) for slot in slots]
    u, w, qkm, q_dec = [], [], [], []
    for slot in slots:
        hq = slot // 2
        egc = jnp.exp(gc_c[slot])
        rhs = jnp.concatenate([v_ref[slot] * beta_c[slot], k[hq] * (beta_c[slot] * egc)], axis=1)
        sol = _mm(inv[slot], rhs)
        u.append(sol[:, :D])
        w.append(sol[:, D:])
        qkm.append(jnp.where(causal, qk[hq] * decay[slot], 0.0))
        q_dec.append(q[hq] * egc)
    state = [s_ref[slot] for slot in slots]
    v_done = [[] for _ in slots]
    for c in range(L // C):
        lo, hi = c * C, (c + 1) * C
        r = [_mm(jnp.concatenate([w[slot][lo:hi], q_dec[slot][lo:hi]], axis=0), state[slot]) for slot in slots]
        for slot in slots:
            v_done[slot].append(u[slot][lo:hi] - r[slot][:C])
            v_all = jnp.concatenate(v_done[slot] + [jnp.zeros((L - hi, D), F32)] * (hi < L), axis=0)
            o_ref[lo:hi, slot * D:(slot + 1) * D] = r[slot][C:] + _mm(qkm[slot][lo:hi, :], v_all)
            kdt_c = jnp.where((colk >= lo) & (colk < hi), kdt[slot], 0.0)
            state[slot] = state[slot] * jnp.exp(gt_c[slot][lo:lo + 1, :]) + _mm(kdt_c, v_all)
    for slot in slots:
        s_ref[slot] = state[slot]


def gdn_scan(qkv_hm, kt_hm, slab, slabt):
    t = qkv_hm.shape[1]
    L = GDN_GROUP
    D = GDN_HEAD_DIM
    hq = GDN_SCAN_HEADS
    assert t % L == 0 and GDN_QK_HEADS % hq == 0
    q_blocks = GDN_QK_HEADS // hq
    return pl.pallas_call(
        _gdn_scan_kernel,
        grid=(q_blocks, t // L),
        in_specs=[
            pl.BlockSpec((hq, L, D), lambda j, n: (j, n, 0)),
            pl.BlockSpec((hq, L, D), lambda j, n: (q_blocks + j, n, 0)),
            pl.BlockSpec((hq, D, L), lambda j, n: (j, 0, n)),
            pl.BlockSpec((2 * hq, L, D), lambda j, n: (q_blocks + j, n, 0)),
            pl.BlockSpec((L, LANES), lambda j, n: (n, 0)),
            pl.BlockSpec((LANES, L), lambda j, n: (0, n)),
        ],
        out_specs=pl.BlockSpec((L, 2 * hq * D), lambda j, n: (n, j)),
        out_shape=jax.ShapeDtypeStruct((t, GDN_V_HEADS * D), F32),
        scratch_shapes=[pltpu.VMEM((2 * hq, D, D), F32)],
        compiler_params=_cparams(("parallel", "arbitrary")),
        name="gdn_scan",
    )(qkv_hm, qkv_hm, kt_hm, qkv_hm, slab, slabt)


def _gdn_out_kernel(o_ref, z_ref, onorm_ref, w_ref, gpost_ref, h_ref, out_ref):
    o = o_ref[...]
    z = z_ref[...]
    parts = []
    for hd in range(GDN_V_HEADS):
        seg = o[:, hd * GDN_HEAD_DIM:(hd + 1) * GDN_HEAD_DIM]
        parts.append(seg * lax.rsqrt(jnp.mean(seg * seg, axis=-1, keepdims=True) + NORM_EPS))
    gated = jnp.concatenate(parts, axis=1) * onorm_ref[...] * (z * _sigmoid(z))
    mix = jnp.dot(gated.astype(BF16), w_ref[...], preferred_element_type=F32)
    out_ref[...] = h_ref[...] + _rms(mix, gpost_ref[...])


def gdn_out(o, proj, onorm_tiled, w_out, gpost, h, tm=256):
    t, vw = o.shape
    d = h.shape[1]
    z_blk = (proj.shape[1] - vw) // vw
    assert proj.shape[1] % vw == 0 and t % tm == 0
    return pl.pallas_call(
        _gdn_out_kernel,
        grid=(t // tm,),
        in_specs=[
            pl.BlockSpec((tm, vw), lambda i: (i, 0)),
            pl.BlockSpec((tm, vw), lambda i: (i, z_blk)),
            pl.BlockSpec((1, vw), lambda i: (0, 0)),
            pl.BlockSpec((vw, d), lambda i: (0, 0)),
            pl.BlockSpec((1, d), lambda i: (0, 0)),
            pl.BlockSpec((tm, d), lambda i: (i, 0)),
        ],
        out_specs=pl.BlockSpec((tm, d), lambda i: (i, 0)),
        out_shape=jax.ShapeDtypeStruct((t, d), F32),
        compiler_params=_cparams(("parallel",)),
        name="gdn_out",
    )(o, proj, onorm_tiled, w_out, gpost.reshape(1, d), h)


def _compress_kernel(x_ref, pos_ref, w1_ref, w2_ref, o_ref):
    x = x_ref[...]
    pos = pos_ref[...]
    nc, half = x.shape
    w1 = w1_ref[...]
    first = _mm(x + pos[0:1, :], w1[:half])
    second = _mm(x + pos[1:2, :], w1[half:])
    hid = first + pltpu.roll(second, nc - 1, axis=0)
    hid = hid * _sigmoid(hid)
    out = jnp.dot(hid.astype(BF16), w2_ref[...], preferred_element_type=F32)
    rowi = lax.broadcasted_iota(jnp.int32, out.shape, 0)
    o_ref[...] = jnp.where(rowi < nc - 1, out, 0.0)


def compress(x2, pos2, w1, w2):
    _, g, nc, wdt = x2.shape
    hid = w1.shape[2]
    dh = w2.shape[2]
    return pl.pallas_call(
        _compress_kernel,
        grid=(2, g),
        in_specs=[
            pl.BlockSpec((None, None, nc, wdt), lambda b, gi: (b, gi, 0, 0)),
            pl.BlockSpec((None, 2, wdt), lambda b, gi: (b, 0, 0)),
            pl.BlockSpec((None, 2 * wdt, hid), lambda b, gi: (b, 0, 0)),
            pl.BlockSpec((None, hid, dh), lambda b, gi: (b, 0, 0)),
        ],
        out_specs=pl.BlockSpec((None, None, nc, dh), lambda b, gi: (b, gi, 0, 0)),
        out_shape=jax.ShapeDtypeStruct((2, g, nc, dh), F32),
        compiler_params=_cparams(("parallel", "parallel")),
        name="nsa_compress",
    )(x2, pos2, w1, w2)


def _cmp_topk_kernel(q_ref, kbd_ref, vt_ref, m_ref, oc_ref, sel_ref, s_ref, p_ref, psum_ref, *,
                     tq, nc, nsel, topk):
    i = pl.program_id(1)
    dh = NSA_HEAD_DIM
    rows_per = min(nc, CMP_ROWS)
    q = (q_ref[...] * ((dh ** -0.5) * LOG2_E)).astype(BF16)
    s_ref[...] = jnp.dot(kbd_ref[...], q, preferred_element_type=F32)
    for ch in range(tq // LANES):
        lanes = slice(ch * LANES, (ch + 1) * LANES)
        tpos = i * tq + ch * LANES + lax.broadcasted_iota(jnp.int32, (rows_per, LANES), 1)
        cblk0 = lax.broadcasted_iota(jnp.int32, (rows_per, LANES), 0)
        masks = [(CMP_STRIDE * (cblk0 + c * rows_per) + CMP_BLOCK - 1) <= tpos for c in range(nc // rows_per)]
        for r in range(NSA_REP):
            pieces = [slice(r * nc + c * rows_per, r * nc + (c + 1) * rows_per) for c in range(nc // rows_per)]
            m = jnp.full((1, LANES), -jnp.inf, F32)
            for rows, mask in zip(pieces, masks):
                m = jnp.maximum(m, jnp.max(jnp.where(mask, s_ref[rows, lanes], -jnp.inf), axis=0, keepdims=True))
            m = jnp.where(m > -jnp.inf, m, 0.0)
            total = jnp.zeros((1, LANES), F32)
            for rows, mask in zip(pieces, masks):
                e = jnp.exp2(jnp.where(mask, s_ref[rows, lanes], -jnp.inf) - m)
                s_ref[rows, lanes] = e
                total = total + jnp.sum(e, axis=0, keepdims=True)
            inv = 1.0 / jnp.maximum(total, 1e-30)
            for c, rows in enumerate(pieces):
                p = s_ref[rows, lanes] * inv
                p_ref[rows, lanes] = p.astype(BF16)
                prow = slice(c * rows_per, (c + 1) * rows_per)
                psum_ref[prow, lanes] = p if r == 0 else psum_ref[prow, lanes] + p
    oc_t = jnp.dot(vt_ref[...], p_ref[...], preferred_element_type=F32)
    imp = jnp.zeros((nsel, tq), F32)
    for piece in _split3(psum_ref[...]):
        imp = imp + jnp.dot(m_ref[...], piece, preferred_element_type=F32)
    for ch in range(tq // LANES):
        lanes = slice(ch * LANES, (ch + 1) * LANES)
        oc_ref[lanes, :] = oc_t[:, lanes].T
        t1 = i * tq + ch * LANES + lax.broadcasted_iota(jnp.int32, (nsel, LANES), 1)
        blk = lax.broadcasted_iota(jnp.int32, (nsel, LANES), 0)
        cur = _div_pow2(t1, SEL_BLOCK)
        forced = (blk == 0) | (blk == cur) | (blk == cur - 1)
        valid = blk * SEL_BLOCK <= t1
        score = jnp.where(valid, jnp.where(forced, FORCED_SCORE, imp[:, lanes]), -jnp.inf)
        blkf = blk.astype(F32)
        work = score
        for _ in range(topk):
            mx = jnp.max(work, axis=0, keepdims=True)
            first = jnp.min(jnp.where(work == mx, blkf, float(nsel)), axis=0, keepdims=True)
            work = jnp.where(blkf == first, -jnp.inf, work)
        sel_ref[:, lanes] = jnp.where((score > -jnp.inf) & (work == -jnp.inf), 1.0, 0.0).astype(sel_ref.dtype)


def cmp_topk(q_t, kbd, v_t, imp_mat_t, tq=256):
    qw, t = q_t.shape
    g = kbd.shape[0]
    nc = kbd.shape[1] // NSA_REP
    nsel = imp_mat_t.shape[0]
    topk = min(SEL_TOPK, nsel)
    assert t % tq == 0 and nc % min(nc, CMP_ROWS) == 0
    return pl.pallas_call(
        functools.partial(_cmp_topk_kernel, tq=tq, nc=nc, nsel=nsel, topk=topk),
        grid=(g, t // tq),
        in_specs=[
            pl.BlockSpec((NSA_GW, tq), lambda gi, i: (gi, i)),
            pl.BlockSpec((None, NSA_REP * nc, NSA_GW), lambda gi, i: (gi, 0, 0)),
            pl.BlockSpec((None, NSA_GW, NSA_REP * nc), lambda gi, i: (gi, 0, 0)),
            pl.BlockSpec((nsel, nc), lambda gi, i: (0, 0)),
        ],
        out_specs=[
            pl.BlockSpec((tq, NSA_GW), lambda gi, i: (i, gi)),
            pl.BlockSpec((None, nsel, tq), lambda gi, i: (gi, 0, i)),
        ],
        out_shape=[
            jax.ShapeDtypeStruct((t, qw), F32),
            jax.ShapeDtypeStruct((g, nsel, t), F32),
        ],
        scratch_shapes=[
            pltpu.VMEM((NSA_REP * nc, tq), F32),
            pltpu.VMEM((NSA_REP * nc, tq), BF16),
            pltpu.VMEM((nc, tq), F32),
        ],
        compiler_params=_cparams(("parallel", "parallel")),
        name="nsa_cmp_topk",
    )(q_t, kbd, v_t, imp_mat_t)


def _flash_kernel(qi_ref, ki_ref, first_ref, last_ref, *refs, tq, kt, nsel, selected):
    if selected:
        q_ref, k_ref, v_ref, sel_ref, o_ref = refs[:5]
    else:
        q_ref, k_ref, v_ref, o_ref = refs[:4]
    m_ref, l_ref, alpha_ref, acc_ref, s_ref, p_ref, bias_ref, kbd_ref, vbd_ref, qs_ref = refs[-10:]
    step = pl.program_id(1)
    qi = qi_ref[step]
    ki = ki_ref[step]
    dh = NSA_HEAD_DIM

    @pl.when(first_ref[step] == 1)
    def _():
        m_ref[...] = jnp.full_like(m_ref, NEG_INIT)
        l_ref[...] = jnp.zeros_like(l_ref)
        acc_ref[...] = jnp.zeros_like(acc_ref)
        qs_ref[...] = (q_ref[...] * ((dh ** -0.5) * LOG2_E)).astype(BF16)

    k4 = k_ref[...]
    kseg = _div_pow2(lax.broadcasted_iota(jnp.int32, k4.shape, 1), dh)
    v4 = v_ref[...]
    vblk = _div_pow2(lax.broadcasted_iota(jnp.int32, v4.shape, 0), dh)
    for r in range(NSA_REP):
        kbd_ref[r * kt:(r + 1) * kt, :] = jnp.where(kseg == r, k4, jnp.zeros_like(k4))
        vbd_ref[:NSA_GW, r * kt:(r + 1) * kt] = jnp.where(vblk == r, v4, jnp.zeros_like(v4))
    one_row = lax.broadcasted_iota(jnp.int32, (FLASH_SUM_ROWS, NSA_REP * kt), 0)
    one_head = _div_pow2(lax.broadcasted_iota(jnp.int32, (FLASH_SUM_ROWS, NSA_REP * kt), 1), kt)
    vbd_ref[NSA_GW:, :] = (one_row == one_head).astype(BF16)

    n_sub = tq // FLASH_SUB
    n_blk = kt // SEL_BLOCK

    def body(positional):
        def scores(sub):
            c0 = sub * FLASH_SUB
            cols = slice(c0, c0 + FLASH_SUB)
            s_ref[:, cols] = jnp.dot(kbd_ref[...], qs_ref[:, cols],
                                     preferred_element_type=F32).astype(BF16)
            for jb in range(n_blk):
                rows = slice(jb * SEL_BLOCK, (jb + 1) * SEL_BLOCK)
                if selected:
                    picked = sel_ref[pl.ds(ki * n_blk + jb, 1), cols] > 0.5
                if positional:
                    tpos = qi * tq + c0 + lax.broadcasted_iota(jnp.int32, (SEL_BLOCK, FLASH_SUB), 1)
                    kpos = ki * kt + jb * SEL_BLOCK + lax.broadcasted_iota(jnp.int32, (SEL_BLOCK, FLASH_SUB), 0)
                    if selected:
                        allowed = picked & (kpos <= tpos)
                    else:
                        allowed = (kpos <= tpos) & (kpos > tpos - WINDOW)
                else:
                    allowed = jnp.broadcast_to(picked, (SEL_BLOCK, FLASH_SUB))
                bias_ref[rows, cols] = jnp.where(allowed, 0.0, -jnp.inf).astype(BF16)

        scores(0)
        for sub in range(n_sub):
            c0 = sub * FLASH_SUB
            cols = slice(c0, c0 + FLASH_SUB)
            if sub + 1 < n_sub:
                scores(sub + 1)
            for ch in range(FLASH_SUB // LANES):
                lanes = slice(c0 + ch * LANES, c0 + (ch + 1) * LANES)
                bias = bias_ref[:, lanes]
                for r in range(NSA_REP):
                    x = s_ref[r * kt:(r + 1) * kt, lanes] + bias
                    m_prev = m_ref[r:r + 1, lanes]
                    m_new = jnp.maximum(m_prev, jnp.max(x, axis=0, keepdims=True).astype(F32))
                    m_ref[r:r + 1, lanes] = m_new
                    alpha_ref[r:r + 1, lanes] = jnp.exp2(m_prev - m_new)
                    p_ref[r * kt:(r + 1) * kt, lanes] = jnp.exp2(x - m_new.astype(BF16))
            pv = jnp.dot(vbd_ref[...], p_ref[:, cols], preferred_element_type=F32)
            for r in range(NSA_REP):
                hd = slice(r * dh, (r + 1) * dh)
                alpha = alpha_ref[r:r + 1, cols]
                acc_ref[hd, cols] = acc_ref[hd, cols] * alpha + pv[hd]
                l_ref[r:r + 1, cols] = l_ref[r:r + 1, cols] * alpha + pv[NSA_GW + r:NSA_GW + r + 1]

    if selected:
        below_diagonal = (ki + 1) * kt <= qi * tq
        pl.when(below_diagonal)(lambda: body(False))
        pl.when(jnp.logical_not(below_diagonal))(lambda: body(True))
    else:
        body(True)

    @pl.when(last_ref[step] == 1)
    def _():
        for ch in range(tq // LANES):
            lanes = slice(ch * LANES, (ch + 1) * LANES)
            out_t = jnp.concatenate(
                [acc_ref[r * dh:(r + 1) * dh, lanes] / l_ref[r:r + 1, lanes] for r in range(NSA_REP)], axis=0)
            o_ref[lanes, :] = out_t.T


def _pair_tables(t, tq, kt, window):
    qi, ki, first, last = [], [], [], []
    for a in range(t // tq):
        lo = 0 if window is None else max(0, (a * tq - window + 1) // kt)
        hi = (a * tq + tq - 1) // kt
        for b in range(lo, hi + 1):
            qi.append(a)
            ki.append(b)
            first.append(int(b == lo))
            last.append(int(b == hi))
    return [jnp.asarray(np.asarray(x, np.int32)) for x in (qi, ki, first, last)]


def flash_branch(q_t, k4, vt4, sel_t, tq, kt):
    qw, t = q_t.shape
    g = k4.shape[0]
    selected = sel_t is not None
    nsel = sel_t.shape[1] if selected else 0
    tables = _pair_tables(t, tq, kt, None if selected else WINDOW)
    in_specs = [
        pl.BlockSpec((NSA_GW, tq), lambda gi, s, qi, ki, fi, la: (gi, qi[s])),
        pl.BlockSpec((None, kt, NSA_GW), lambda gi, s, qi, ki, fi, la: (gi, ki[s], 0)),
        pl.BlockSpec((None, NSA_GW, kt), lambda gi, s, qi, ki, fi, la: (gi, 0, ki[s])),
    ]
    args = [q_t, k4, vt4]
    if selected:
        in_specs.append(pl.BlockSpec((None, nsel, tq), lambda gi, s, qi, ki, fi, la: (gi, 0, qi[s])))
        args.append(sel_t)
    return pl.pallas_call(
        functools.partial(_flash_kernel, tq=tq, kt=kt, nsel=nsel, selected=selected),
        grid_spec=pltpu.PrefetchScalarGridSpec(
            num_scalar_prefetch=4,
            grid=(g, int(tables[0].shape[0])),
            in_specs=in_specs,
            out_specs=pl.BlockSpec((tq, NSA_GW), lambda gi, s, qi, ki, fi, la: (qi[s], gi)),
            scratch_shapes=[
                pltpu.VMEM((8, tq), F32),
                pltpu.VMEM((8, tq), F32),
                pltpu.VMEM((8, tq), F32),
                pltpu.VMEM((NSA_GW, tq), F32),
                pltpu.VMEM((NSA_REP * kt, tq), BF16),
                pltpu.VMEM((NSA_REP * kt, tq), BF16),
                pltpu.VMEM((kt, tq), BF16),
                pltpu.VMEM((NSA_REP * kt, NSA_GW), BF16),
                pltpu.VMEM((NSA_GW + FLASH_SUM_ROWS, NSA_REP * kt), BF16),
                pltpu.VMEM((NSA_GW, tq), BF16),
            ],
        ),
        out_shape=jax.ShapeDtypeStruct((t, qw), F32),
        compiler_params=_cparams(("parallel", "arbitrary")),
        name="nsa_selected" if selected else "nsa_window",
    )(*tables, *args)


def _nsa_out_kernel(oc_ref, os_ref, ow_ref, gl_ref, eg_ref, w_ref, gpost_ref, h_ref, out_ref):
    pieces = _split3(_sigmoid(gl_ref[...]))
    mixed = jnp.zeros(oc_ref.shape, F32)
    for b, br_ref in enumerate((oc_ref, os_ref, ow_ref)):
        gfull = jnp.zeros(oc_ref.shape, F32)
        for piece in pieces:
            gfull = gfull + jnp.dot(piece, eg_ref[b], preferred_element_type=F32)
        mixed = mixed + gfull * br_ref[...]
    mix = jnp.dot(mixed.astype(BF16), w_ref[...], preferred_element_type=F32)
    out_ref[...] = h_ref[...] + _rms(mix, gpost_ref[...])


def nsa_out(oc, osel, ow, gate_logits, expand, w_o, gpost, h, tm=256):
    t, qw = oc.shape
    d = h.shape[1]
    assert t % tm == 0
    row = lambda w: pl.BlockSpec((tm, w), lambda i: (i, 0))
    return pl.pallas_call(
        _nsa_out_kernel,
        grid=(t // tm,),
        in_specs=[
            row(qw), row(qw), row(qw), row(LANES),
            pl.BlockSpec((3, LANES, qw), lambda i: (0, 0, 0)),
            pl.BlockSpec((qw, d), lambda i: (0, 0)),
            pl.BlockSpec((1, d), lambda i: (0, 0)),
            row(d),
        ],
        out_specs=row(d),
        out_shape=jax.ShapeDtypeStruct((t, d), F32),
        compiler_params=_cparams(("parallel",)),
        name="nsa_out",
    )(oc, osel, ow, gate_logits, expand, w_o, gpost.reshape(1, d), h)


def _pad_cols(w, n):
    return jnp.pad(w, ((0, 0), (0, n - w.shape[1])))


def _importance_matrix(nc, nsel):
    r = SEL_BLOCK // CMP_STRIDE
    c = CMP_BLOCK // CMP_STRIDE
    mat = np.zeros((nc, nsel), np.float32)
    for kblk in range(nsel):
        for m in range(r):
            for n in range(c):
                j = r * kblk + m - n
                if 0 <= j < nc - 1:
                    mat[j, kblk] += 1.0
    return jnp.asarray(mat, BF16)


def _gate_expand():
    e = np.zeros((3, LANES, NSA_GROUPS * NSA_GW), np.float32)
    for head in range(NSA_GROUPS * NSA_REP):
        for b in range(3):
            e[b, head * 3 + b, head * NSA_HEAD_DIM:(head + 1) * NSA_HEAD_DIM] = 1.0
    return jnp.asarray(e, BF16)


def _block_diag_kv(k_cmp, v_cmp):
    g, nc, dh = k_cmp.shape
    eye = jnp.eye(NSA_REP, dtype=bool)
    kct = jnp.swapaxes(k_cmp, 1, 2)
    kbd = jnp.where(eye[None, :, None, :, None], kct[:, None, :, None, :], 0.0)
    vbd = jnp.where(eye[None, :, None, :, None], v_cmp[:, None, :, None, :], 0.0)
    return (kbd.reshape(g, NSA_REP * dh, NSA_REP * nc).astype(BF16),
            vbd.reshape(g, NSA_REP * nc, NSA_REP * dh).astype(BF16))


def kernel(x, p, mix_pre_norm, mix_post_norm, ffn_pre_norm, ffn_post_norm, gdn_w_in, gdn_conv_w, gdn_a_log,
           gdn_dt_bias, gdn_o_norm, gdn_w_out, kv_norm, kv_w, cmp_pos, cmp_w1, cmp_w2, nsa_w_qg, nsa_w_o,
           ffn_w_in, ffn_w_out, ple_w_in, ple_w_gate):
    depth = p.shape[0]
    n_a = gdn_w_in.shape[0]
    t = x.shape[1]
    h = x[0]
    fh = ffn_w_out.shape[1]
    conv_w_cols = gdn_conv_w.shape[2]
    vw = GDN_V_HEADS * GDN_HEAD_DIM
    main_w = conv_w_cols + vw

    def channel_and_ple(h, i):
        w_in = ffn_w_in[i].astype(BF16)
        return ffn_ple(h, ffn_pre_norm[i], w_in[:, :fh], w_in[:, fh:], ffn_w_out[i].astype(BF16),
                       ffn_post_norm[i], p[i, 0], ple_w_in[i].astype(BF16), ple_w_gate[i].astype(BF16))

    for i in range(n_a):
        w_in = gdn_w_in[i]
        w_beta = w_in[:, main_w:main_w + GDN_V_HEADS]
        w_a = w_in[:, main_w + GDN_V_HEADS:]
        w_small = _pad_cols(jnp.concatenate([w_beta, w_a, w_a], axis=1), LANES).astype(BF16)
        proj = norm_matmul(h, mix_pre_norm[i], w_in[:, :main_w].astype(BF16))
        ba = norm_matmul(h, mix_pre_norm[i], w_small)
        pad_vec = lambda v: jnp.pad(v, (GDN_V_HEADS, LANES - 2 * GDN_V_HEADS))
        alog_vec = (pad_vec(gdn_a_log[i]) + jnp.pad(gdn_a_log[i], (2 * GDN_V_HEADS, LANES - 3 * GDN_V_HEADS)))
        dtb_vec = (pad_vec(gdn_dt_bias[i]) + jnp.pad(gdn_dt_bias[i], (2 * GDN_V_HEADS, LANES - 3 * GDN_V_HEADS)))
        slab = gdn_gates(ba, alog_vec.reshape(1, LANES), dtb_vec.reshape(1, LANES))
        qkv_hm = gdn_conv(proj, gdn_conv_w[i])
        kt_hm = jnp.swapaxes(qkv_hm[GDN_QK_HEADS:2 * GDN_QK_HEADS], 1, 2)
        o = gdn_scan(qkv_hm, kt_hm, slab, slab.T)
        onorm_tiled = jnp.tile(gdn_o_norm[i], GDN_V_HEADS).reshape(1, vw)
        h = gdn_out(o, proj, onorm_tiled, gdn_w_out[i].astype(BF16), mix_post_norm[i], h)
        h = channel_and_ple(h, i)

    g = NSA_GROUPS
    dh = NSA_HEAD_DIM
    kv = norm_matmul(h, kv_norm, kv_w.astype(BF16), tn=768)
    kv6 = jnp.transpose(kv.reshape(t, 6, g, dh), (1, 2, 0, 3))
    nc = t // CMP_STRIDE
    nsel = t // SEL_BLOCK
    x2 = kv6[0:2].reshape(2, g, nc, CMP_STRIDE * dh)
    pos2 = cmp_pos.reshape(2, 2, CMP_STRIDE * dh)
    cmp_out = compress(x2, pos2, cmp_w1.astype(BF16), cmp_w2.astype(BF16))
    v_cmp_t, kbd = _block_diag_kv(cmp_out[1], cmp_out[0])
    imp_mat_t = _importance_matrix(nc, nsel).T
    rep_t = lambda a: jnp.tile(jnp.swapaxes(a, 1, 2), (1, NSA_REP, 1)).astype(BF16)
    rep_k = lambda a: jnp.tile(a, (1, 1, NSA_REP)).astype(BF16)
    k_slc, v_slc_t = rep_k(kv6[2]), rep_t(kv6[3])
    k_win, v_win_t = rep_k(kv6[4]), rep_t(kv6[5])
    expand = _gate_expand()

    for i in range(n_a, depth):
        j = i - n_a
        qw = g * NSA_GW
        w_qg = nsa_w_qg[j]
        q = norm_matmul(h, mix_pre_norm[i], w_qg[:, :qw].astype(BF16))
        gate_logits = norm_matmul(h, mix_pre_norm[i], _pad_cols(w_qg[:, qw:], LANES).astype(BF16))
        q_t = q.T
        o_c, sel_t = cmp_topk(q_t, kbd, v_cmp_t, imp_mat_t)
        o_s = flash_branch(q_t, k_slc, v_slc_t, sel_t, tq=1024, kt=256)
        o_w = flash_branch(q_t, k_win, v_win_t, None, tq=512, kt=256)
        h = nsa_out(o_c, o_s, o_w, gate_logits, expand, nsa_w_o[j].astype(BF16), mix_post_norm[i], h)
        h = channel_and_ple(h, i)
    return h[None]
```

```python
import functools

import numpy as np
import jax
import jax.numpy as jnp
from jax import lax
from jax.experimental import pallas as pl
from jax.experimental.pallas import tpu as pltpu

F32 = jnp.float32
BF16 = jnp.bfloat16

NORM_EPS = 1e-6
L2_EPS = 1e-6
GDN_QK_HEADS = 8
GDN_V_HEADS = 16
GDN_HEAD_DIM = 128
GDN_CONV = 4
GDN_CHUNK = 64
GDN_GROUP = 256
GDN_SCAN_HEADS = 4
GDN_CONV_HEADS = 4
NSA_GROUPS = 4
NSA_REP = 4
NSA_HEAD_DIM = 64
NSA_GW = NSA_REP * NSA_HEAD_DIM
CMP_BLOCK = 32
CMP_STRIDE = 16
SEL_BLOCK = 64
SEL_TOPK = 16
WINDOW = 512
FORCED_SCORE = 1e4
LANES = 128
NEG_INIT = -(2.0 ** 100)
LOG2_E = 1.4426950408889634
FLASH_SUB = 256
FLASH_SUM_ROWS = 16
CMP_ROWS = 256

VMEM_LIMIT = 56 * 1024 * 1024


def _cparams(sem):
    return pltpu.CompilerParams(dimension_semantics=sem, vmem_limit_bytes=VMEM_LIMIT)


def _rms(x, gain):
    return x * lax.rsqrt(jnp.mean(x * x, axis=-1, keepdims=True) + NORM_EPS) * gain


def _mm(a, b):
    return jnp.dot(a.astype(BF16), b.astype(BF16), preferred_element_type=F32)


def _sigmoid(x):
    return 1.0 / (1.0 + jnp.exp(-x))


def _div_pow2(x, d):
    shift = d.bit_length() - 1
    assert d == 1 << shift
    return jnp.right_shift(x, shift)


def _split3(x):
    a = x.astype(BF16)
    r = x - a.astype(F32)
    b = r.astype(BF16)
    c = (r - b.astype(F32)).astype(BF16)
    return a, b, c


def _norm_matmul_kernel(x_ref, g_ref, w_ref, o_ref, xn_ref):
    @pl.when(pl.program_id(1) == 0)
    def _():
        xn_ref[...] = _rms(x_ref[...], g_ref[...]).astype(BF16)

    o_ref[...] = jnp.dot(xn_ref[...], w_ref[...], preferred_element_type=F32)


def norm_matmul(h, gain, w, tm=512, tn=1024):
    t, d = h.shape
    n = w.shape[1]
    tn = min(tn, n)
    assert t % tm == 0 and n % tn == 0
    return pl.pallas_call(
        _norm_matmul_kernel,
        grid=(t // tm, n // tn),
        in_specs=[
            pl.BlockSpec((tm, d), lambda i, j: (i, 0)),
            pl.BlockSpec((1, d), lambda i, j: (0, 0)),
            pl.BlockSpec((d, tn), lambda i, j: (0, j)),
        ],
        out_specs=pl.BlockSpec((tm, tn), lambda i, j: (i, j)),
        out_shape=jax.ShapeDtypeStruct((t, n), F32),
        scratch_shapes=[pltpu.VMEM((tm, d), BF16)],
        compiler_params=_cparams(("parallel", "arbitrary")),
        name="norm_matmul",
    )(h, gain.reshape(1, d), w)


def _ffn_ple_kernel(h_ref, gpre_ref, wg_ref, wu_ref, wo_ref, gpost_ref, p_ref, wple_ref, wgt_ref,
                    o_ref, xn_ref, acc_ref, *, nf):
    f = pl.program_id(1)

    @pl.when(f == 0)
    def _():
        xn_ref[...] = _rms(h_ref[...], gpre_ref[...]).astype(BF16)
        acc_ref[...] = jnp.zeros_like(acc_ref)

    xn = xn_ref[...]
    gate = jnp.dot(xn, wg_ref[...], preferred_element_type=F32)
    up = jnp.dot(xn, wu_ref[...], preferred_element_type=F32)
    act = gate * _sigmoid(gate) * up
    acc_ref[...] += jnp.dot(act.astype(BF16), wo_ref[...], preferred_element_type=F32)

    @pl.when(f == nf - 1)
    def _():
        h2 = h_ref[...] + _rms(acc_ref[...], gpost_ref[...])
        emb = jnp.dot(p_ref[...].astype(BF16), wple_ref[...], preferred_element_type=F32)
        gt = _sigmoid(jnp.dot(h2.astype(BF16), wgt_ref[...], preferred_element_type=F32))
        o_ref[...] = h2 + emb * gt


def ffn_ple(h, gpre, w_gate, w_up, w_out, gpost, p, w_ple, w_plegate, tm=512, tf=1408):
    t, d = h.shape
    fh = w_gate.shape[1]
    pd = p.shape[1]
    assert t % tm == 0 and fh % tf == 0
    nf = fh // tf
    return pl.pallas_call(
        functools.partial(_ffn_ple_kernel, nf=nf),
        grid=(t // tm, nf),
        in_specs=[
            pl.BlockSpec((tm, d), lambda i, f: (i, 0)),
            pl.BlockSpec((1, d), lambda i, f: (0, 0)),
            pl.BlockSpec((d, tf), lambda i, f: (0, f)),
            pl.BlockSpec((d, tf), lambda i, f: (0, f)),
            pl.BlockSpec((tf, d), lambda i, f: (f, 0)),
            pl.BlockSpec((1, d), lambda i, f: (0, 0)),
            pl.BlockSpec((tm, pd), lambda i, f: (i, 0)),
            pl.BlockSpec((pd, d), lambda i, f: (0, 0)),
            pl.BlockSpec((d, d), lambda i, f: (0, 0)),
        ],
        out_specs=pl.BlockSpec((tm, d), lambda i, f: (i, 0)),
        out_shape=jax.ShapeDtypeStruct((t, d), F32),
        scratch_shapes=[pltpu.VMEM((tm, d), BF16), pltpu.VMEM((tm, d), F32)],
        compiler_params=_cparams(("parallel", "arbitrary")),
        name="ffn_ple",
    )(h, gpre.reshape(1, d), w_gate, w_up, w_out, gpost.reshape(1, d), p, w_ple, w_plegate)


def _gdn_conv_kernel(x_ref, halo_ref, w_ref, o_ref, *, tm):
    c = pl.program_id(0)
    i = pl.program_id(1)
    x = x_ref[...]
    halo = jnp.where(i > 0, halo_ref[...], 0.0)
    ext = jnp.concatenate([halo, x], axis=0)
    w = w_ref[...]
    y = x * w[GDN_CONV - 1:GDN_CONV, :]
    for k in range(1, GDN_CONV):
        shifted = pltpu.roll(ext, k, axis=0)[8:8 + tm]
        y = y + shifted * w[GDN_CONV - 1 - k:GDN_CONV - k, :]
    y = y * _sigmoid(y)
    for hd in range(GDN_CONV_HEADS):
        head = c * GDN_CONV_HEADS + hd
        seg = y[:, hd * LANES:(hd + 1) * LANES]
        normed = seg * lax.rsqrt(jnp.sum(seg * seg, axis=-1, keepdims=True) + L2_EPS)
        q_scale = jnp.where(head < GDN_QK_HEADS, GDN_HEAD_DIM ** -0.5, 1.0)
        o_ref[hd] = jnp.where(head < 2 * GDN_QK_HEADS, normed * q_scale, seg)


def gdn_conv(proj, conv_w, tm=1024):
    t = proj.shape[0]
    n_tiles = conv_w.shape[1] // LANES
    cw = GDN_CONV_HEADS
    assert t % tm == 0 and n_tiles % cw == 0
    return pl.pallas_call(
        functools.partial(_gdn_conv_kernel, tm=tm),
        grid=(n_tiles // cw, t // tm),
        in_specs=[
            pl.BlockSpec((tm, cw * LANES), lambda c, i: (i, c)),
            pl.BlockSpec((8, cw * LANES), lambda c, i: (jnp.maximum(i * (tm // 8) - 1, 0), c)),
            pl.BlockSpec((GDN_CONV, cw * LANES), lambda c, i: (0, c)),
        ],
        out_specs=pl.BlockSpec((cw, tm, LANES), lambda c, i: (c, i, 0)),
        out_shape=jax.ShapeDtypeStruct((n_tiles, t, LANES), F32),
        compiler_params=_cparams(("parallel", "parallel")),
        name="gdn_conv",
    )(proj, proj, conv_w)


def _gdn_gate_kernel(x_ref, alog_ref, dtb_ref, lc_ref, lf_ref, o_ref):
    x = x_ref[...]
    lane = lax.broadcasted_iota(jnp.int32, x.shape, 1)
    beta = _sigmoid(x)
    z = x + dtb_ref[...]
    softplus = jnp.maximum(z, 0.0) + jnp.log(1.0 + jnp.exp(-jnp.abs(z)))
    g = -jnp.exp(alog_ref[...]) * softplus
    gcum = jnp.zeros_like(x)
    gtot = jnp.zeros_like(x)
    for piece in _split3(g):
        gcum = gcum + jnp.dot(lc_ref[...], piece, preferred_element_type=F32)
        gtot = gtot + jnp.dot(lf_ref[...], piece, preferred_element_type=F32)
    o_ref[...] = jnp.where(lane < GDN_V_HEADS, beta, jnp.where(lane < 2 * GDN_V_HEADS, gcum, gtot))


def gdn_gates(ba, alog_vec, dtb_vec):
    t = ba.shape[0]
    tm = GDN_GROUP
    r = np.arange(tm)
    same = (r[:, None] // GDN_CHUNK) == (r[None, :] // GDN_CHUNK)
    lc = jnp.asarray(same & (r[None, :] <= r[:, None]), BF16)
    lf = jnp.asarray(same, BF16)
    return pl.pallas_call(
        _gdn_gate_kernel,
        grid=(t // tm,),
        in_specs=[
            pl.BlockSpec((tm, LANES), lambda i: (i, 0)),
            pl.BlockSpec((1, LANES), lambda i: (0, 0)),
            pl.BlockSpec((1, LANES), lambda i: (0, 0)),
            pl.BlockSpec((tm, tm), lambda i: (0, 0)),
            pl.BlockSpec((tm, tm), lambda i: (0, 0)),
        ],
        out_specs=pl.BlockSpec((tm, LANES), lambda i: (i, 0)),
        out_shape=jax.ShapeDtypeStruct((t, LANES), F32),
        compiler_params=_cparams(("parallel",)),
        name="gdn_gates",
    )(ba, alog_vec, dtb_vec, lc, lf)


def _gdn_scan_kernel(q_ref, k_ref, kt_ref, v_ref, slab_ref, slabt_ref, o_ref, s_ref):
    n = pl.program_id(1)

    @pl.when(n == 0)
    def _():
        s_ref[...] = jnp.zeros_like(s_ref)

    L = GDN_GROUP
    C = GDN_CHUNK
    D = GDN_HEAD_DIM
    slab = slab_ref[...]
    slabt = slabt_ref[...]
    row = lax.broadcasted_iota(jnp.int32, (L, L), 0)
    col = lax.broadcasted_iota(jnp.int32, (L, L), 1)
    same = _div_pow2(row, C) == _div_pow2(col, C)
    causal = same & (col <= row)
    strict = same & (col < row)
    eye = (row == col).astype(F32)
    lane = lax.broadcasted_iota(jnp.int32, (L, LANES), 1)
    sub = lax.broadcasted_iota(jnp.int32, (LANES, L), 0)

    def column(idx):
        return jnp.sum(jnp.where(lane == idx, slab, 0.0), axis=1, keepdims=True)

    def rowvec(idx):
        return jnp.sum(jnp.where(sub == idx, slabt, 0.0), axis=0, keepdims=True)

    colk = lax.broadcasted_iota(jnp.int32, (D, L), 1)
    slots = range(2 * GDN_SCAN_HEADS)

    q = [q_ref[hq] for hq in range(GDN_SCAN_HEADS)]
    k = [k_ref[hq] for hq in range(GDN_SCAN_HEADS)]
    kt = [kt_ref[hq] for hq in range(GDN_SCAN_HEADS)]
    kk = [_mm(k[hq], kt[hq]) for hq in range(GDN_SCAN_HEADS)]
    qk = [_mm(q[hq], kt[hq]) for hq in range(GDN_SCAN_HEADS)]
    beta_c, gc_c, gt_c, decay, bp, inv, kdt = [], [], [], [], [], [], []
    for slot in slots:
        hq = slot // 2
        hv = 2 * (pl.program_id(0) * GDN_SCAN_HEADS + hq) + slot % 2
        beta_c.append(column(hv))
        gc_c.append(column(GDN_V_HEADS + hv))
        gt_c.append(column(2 * GDN_V_HEADS + hv))
        gc_r = rowvec(GDN_V_HEADS + hv)
        gt_r = rowvec(2 * GDN_V_HEADS + hv)
        decay.append(jnp.where(causal, jnp.exp(jnp.where(causal, gc_c[slot] - gc_r, 0.0)), 0.0))
        bp.append(jnp.where(strict, -(kk[hq] * beta_c[slot]) * decay[slot], 0.0))
        inv.append(eye + bp[slot])
        kdt.append(kt[hq] * jnp.exp(gt_r - gc_r))
    for _ in range(5):
        bp = [_mm(bp[slot], bp[slot]) for slot in slots]
        inv = [inv[slot] + _mm(inv[slot], bp[slot]) for slot in slots]
    u, w, qkm, q_dec = [], [], [], []
    for slot in slots:
        hq = slot // 2
        egc = jnp.exp(gc_c[slot])
        rhs = jnp.concatenate([v_ref[slot] * beta_c[slot], k[hq] * (beta_c[slot] * egc)], axis=1)
        sol = _mm(inv[slot], rhs)
        u.append(sol[:, :D])
        w.append(sol[:, D:])
        qkm.append(jnp.where(causal, qk[hq] * decay[slot], 0.0))
        q_dec.append(q[hq] * egc)
    state = [s_ref[slot] for slot in slots]
    v_done = [[] for _ in slots]
    for c in range(L // C):
        lo, hi = c * C, (c + 1) * C
        r = [_mm(jnp.concatenate([w[slot][lo:hi], q_dec[slot][lo:hi]], axis=0), state[slot]) for slot in slots]
        for slot in slots:
            v_done[slot].append(u[slot][lo:hi] - r[slot][:C])
            v_all = jnp.concatenate(v_done[slot] + [jnp.zeros((L - hi, D), F32)] * (hi < L), axis=0)
            o_ref[lo:hi, slot * D:(slot + 1) * D] = r[slot][C:] + _mm(qkm[slot][lo:hi, :], v_all)
            kdt_c = jnp.where((colk >= lo) & (colk < hi), kdt[slot], 0.0)
            state[slot] = state[slot] * jnp.exp(gt_c[slot][lo:lo + 1, :]) + _mm(kdt_c, v_all)
    for slot in slots:
        s_ref[slot] = state[slot]


def gdn_scan(qkv_hm, kt_hm, slab, slabt):
    t = qkv_hm.shape[1]
    L = GDN_GROUP
    D = GDN_HEAD_DIM
    hq = GDN_SCAN_HEADS
    assert t % L == 0 and GDN_QK_HEADS % hq == 0
    q_blocks = GDN_QK_HEADS // hq
    return pl.pallas_call(
        _gdn_scan_kernel,
        grid=(q_blocks, t // L),
        in_specs=[
            pl.BlockSpec((hq, L, D), lambda j, n: (j, n, 0)),
            pl.BlockSpec((hq, L, D), lambda j, n: (q_blocks + j, n, 0)),
            pl.BlockSpec((hq, D, L), lambda j, n: (j, 0, n)),
            pl.BlockSpec((2 * hq, L, D), lambda j, n: (q_blocks + j, n, 0)),
            pl.BlockSpec((L, LANES), lambda j, n: (n, 0)),
            pl.BlockSpec((LANES, L), lambda j, n: (0, n)),
        ],
        out_specs=pl.BlockSpec((L, 2 * hq * D), lambda j, n: (n, j)),
        out_shape=jax.ShapeDtypeStruct((t, GDN_V_HEADS * D), F32),
        scratch_shapes=[pltpu.VMEM((2 * hq, D, D), F32)],
        compiler_params=_cparams(("parallel", "arbitrary")),
        name="gdn_scan",
    )(qkv_hm, qkv_hm, kt_hm, qkv_hm, slab, slabt)


def _gdn_out_kernel(o_ref, z_ref, onorm_ref, w_ref, gpost_ref, h_ref, out_ref):
    o = o_ref[...]
    z = z_ref[...]
    parts = []
    for hd in range(GDN_V_HEADS):
        seg = o[:, hd * GDN_HEAD_DIM:(hd + 1) * GDN_HEAD_DIM]
        parts.append(seg * lax.rsqrt(jnp.mean(seg * seg, axis=-1, keepdims=True) + NORM_EPS))
    gated = jnp.concatenate(parts, axis=1) * onorm_ref[...] * (z * _sigmoid(z))
    mix = jnp.dot(gated.astype(BF16), w_ref[...], preferred_element_type=F32)
    out_ref[...] = h_ref[...] + _rms(mix, gpost_ref[...])


def gdn_out(o, proj, onorm_tiled, w_out, gpost, h, tm=256):
    t, vw = o.shape
    d = h.shape[1]
    z_blk = (proj.shape[1] - vw) // vw
    assert proj.shape[1] % vw == 0 and t % tm == 0
    return pl.pallas_call(
        _gdn_out_kernel,
        grid=(t // tm,),
        in_specs=[
            pl.BlockSpec((tm, vw), lambda i: (i, 0)),
            pl.BlockSpec((tm, vw), lambda i: (i, z_blk)),
            pl.BlockSpec((1, vw), lambda i: (0, 0)),
            pl.BlockSpec((vw, d), lambda i: (0, 0)),
            pl.BlockSpec((1, d), lambda i: (0, 0)),
            pl.BlockSpec((tm, d), lambda i: (i, 0)),
        ],
        out_specs=pl.BlockSpec((tm, d), lambda i: (i, 0)),
        out_shape=jax.ShapeDtypeStruct((t, d), F32),
        compiler_params=_cparams(("parallel",)),
        name="gdn_out",
    )(o, proj, onorm_tiled, w_out, gpost.reshape(1, d), h)


def _compress_kernel(x_ref, pos_ref, w1_ref, w2_ref, o_ref):
    x = x_ref[...]
    pos = pos_ref[...]
    nc, half = x.shape
    w1 = w1_ref[...]
    first = _mm(x + pos[0:1, :], w1[:half])
    second = _mm(x + pos[1:2, :], w1[half:])
    hid = first + pltpu.roll(second, nc - 1, axis=0)
    hid = hid * _sigmoid(hid)
    out = jnp.dot(hid.astype(BF16), w2_ref[...], preferred_element_type=F32)
    rowi = lax.broadcasted_iota(jnp.int32, out.shape, 0)
    o_ref[...] = jnp.where(rowi < nc - 1, out, 0.0)


def compress(x2, pos2, w1, w2):
    _, g, nc, wdt = x2.shape
    hid = w1.shape[2]
    dh = w2.shape[2]
    return pl.pallas_call(
        _compress_kernel,
        grid=(2, g),
        in_specs=[
            pl.BlockSpec((None, None, nc, wdt), lambda b, gi: (b, gi, 0, 0)),
            pl.BlockSpec((None, 2, wdt), lambda b, gi: (b, 0, 0)),
            pl.BlockSpec((None, 2 * wdt, hid), lambda b, gi: (b, 0, 0)),
            pl.BlockSpec((None, hid, dh), lambda b, gi: (b, 0, 0)),
        ],
        out_specs=pl.BlockSpec((None, None, nc, dh), lambda b, gi: (b, gi, 0, 0)),
        out_shape=jax.ShapeDtypeStruct((2, g, nc, dh), F32),
        compiler_params=_cparams(("parallel", "parallel")),
        name="nsa_compress",
    )(x2, pos2, w1, w2)


def _cmp_topk_kernel(q_ref, kbd_ref, vt_ref, m_ref, oc_ref, sel_ref, s_ref, p_ref, psum_ref, *,
                     tq, nc, nsel, topk):
    i = pl.program_id(1)
    dh = NSA_HEAD_DIM
    rows_per = min(nc, CMP_ROWS)
    q = (q_ref[...] * ((dh ** -0.5) * LOG2_E)).astype(BF16)
    s_ref[...] = jnp.dot(kbd_ref[...], q, preferred_element_type=F32)
    for ch in range(tq // LANES):
        lanes = slice(ch * LANES, (ch + 1) * LANES)
        tpos = i * tq + ch * LANES + lax.broadcasted_iota(jnp.int32, (rows_per, LANES), 1)
        cblk0 = lax.broadcasted_iota(jnp.int32, (rows_per, LANES), 0)
        masks = [(CMP_STRIDE * (cblk0 + c * rows_per) + CMP_BLOCK - 1) <= tpos for c in range(nc // rows_per)]
        for r in range(NSA_REP):
            pieces = [slice(r * nc + c * rows_per, r * nc + (c + 1) * rows_per) for c in range(nc // rows_per)]
            m = jnp.full((1, LANES), -jnp.inf, F32)
            for rows, mask in zip(pieces, masks):
                m = jnp.maximum(m, jnp.max(jnp.where(mask, s_ref[rows, lanes], -jnp.inf), axis=0, keepdims=True))
            m = jnp.where(m > -jnp.inf, m, 0.0)
            total = jnp.zeros((1, LANES), F32)
            for rows, mask in zip(pieces, masks):
                e = jnp.exp2(jnp.where(mask, s_ref[rows, lanes], -jnp.inf) - m)
                s_ref[rows, lanes] = e
                total = total + jnp.sum(e, axis=0, keepdims=True)
            inv = 1.0 / jnp.maximum(total, 1e-30)
            for c, rows in enumerate(pieces):
                p = s_ref[rows, lanes] * inv
                p_ref[rows, lanes] = p.astype(BF16)
                prow = slice(c * rows_per, (c + 1) * rows_per)
                psum_ref[prow, lanes] = p if r == 0 else psum_ref[prow, lanes] + p
    oc_t = jnp.dot(vt_ref[...], p_ref[...], preferred_element_type=F32)
    imp = jnp.zeros((nsel, tq), F32)
    for piece in _split3(psum_ref[...]):
        imp = imp + jnp.dot(m_ref[...], piece, preferred_element_type=F32)
    for ch in range(tq // LANES):
        lanes = slice(ch * LANES, (ch + 1) * LANES)
        oc_ref[lanes, :] = oc_t[:, lanes].T
        t1 = i * tq + ch * LANES + lax.broadcasted_iota(jnp.int32, (nsel, LANES), 1)
        blk = lax.broadcasted_iota(jnp.int32, (nsel, LANES), 0)
        cur = _div_pow2(t1, SEL_BLOCK)
        forced = (blk == 0) | (blk == cur) | (blk == cur - 1)
        valid = blk * SEL_BLOCK <= t1
        score = jnp.where(valid, jnp.where(forced, FORCED_SCORE, imp[:, lanes]), -jnp.inf)
        blkf = blk.astype(F32)
        work = score
        for _ in range(topk):
            mx = jnp.max(work, axis=0, keepdims=True)
            first = jnp.min(jnp.where(work == mx, blkf, float(nsel)), axis=0, keepdims=True)
            work = jnp.where(blkf == first, -jnp.inf, work)
        sel_ref[:, lanes] = jnp.where((score > -jnp.inf) & (work == -jnp.inf), 1.0, 0.0).astype(sel_ref.dtype)


def cmp_topk(q_t, kbd, v_t, imp_mat_t, tq=256):
    qw, t = q_t.shape
    g = kbd.shape[0]
    nc = kbd.shape[1] // NSA_REP
    nsel = imp_mat_t.shape[0]
    topk = min(SEL_TOPK, nsel)
    assert t % tq == 0 and nc % min(nc, CMP_ROWS) == 0
    return pl.pallas_call(
        functools.partial(_cmp_topk_kernel, tq=tq, nc=nc, nsel=nsel, topk=topk),
        grid=(g, t // tq),
        in_specs=[
            pl.BlockSpec((NSA_GW, tq), lambda gi, i: (gi, i)),
            pl.BlockSpec((None, NSA_REP * nc, NSA_GW), lambda gi, i: (gi, 0, 0)),
            pl.BlockSpec((None, NSA_GW, NSA_REP * nc), lambda gi, i: (gi, 0, 0)),
            pl.BlockSpec((nsel, nc), lambda gi, i: (0, 0)),
        ],
        out_specs=[
            pl.BlockSpec((tq, NSA_GW), lambda gi, i: (i, gi)),
            pl.BlockSpec((None, nsel, tq), lambda gi, i: (gi, 0, i)),
        ],
        out_shape=[
            jax.ShapeDtypeStruct((t, qw), F32),
            jax.ShapeDtypeStruct((g, nsel, t), F32),
        ],
        scratch_shapes=[
            pltpu.VMEM((NSA_REP * nc, tq), F32),
            pltpu.VMEM((NSA_REP * nc, tq), BF16),
            pltpu.VMEM((nc, tq), F32),
        ],
        compiler_params=_cparams(("parallel", "parallel")),
        name="nsa_cmp_topk",
    )(q_t, kbd, v_t, imp_mat_t)


def _flash_kernel(qi_ref, ki_ref, first_ref, last_ref, *refs, tq, kt, nsel, selected):
    if selected:
        q_ref, k_ref, v_ref, sel_ref, o_ref = refs[:5]
    else:
        q_ref, k_ref, v_ref, o_ref = refs[:4]
    m_ref, l_ref, alpha_ref, acc_ref, s_ref, p_ref, bias_ref, kbd_ref, vbd_ref, qs_ref = refs[-10:]
    step = pl.program_id(1)
    qi = qi_ref[step]
    ki = ki_ref[step]
    dh = NSA_HEAD_DIM

    @pl.when(first_ref[step] == 1)
    def _():
        m_ref[...] = jnp.full_like(m_ref, NEG_INIT)
        l_ref[...] = jnp.zeros_like(l_ref)
        acc_ref[...] = jnp.zeros_like(acc_ref)
        qs_ref[...] = (q_ref[...] * ((dh ** -0.5) * LOG2_E)).astype(BF16)

    k4 = k_ref[...]
    kseg = _div_pow2(lax.broadcasted_iota(jnp.int32, k4.shape, 1), dh)
    v4 = v_ref[...]
    vblk = _div_pow2(lax.broadcasted_iota(jnp.int32, v4.shape, 0), dh)
    for r in range(NSA_REP):
        kbd_ref[r * kt:(r + 1) * kt, :] = jnp.where(kseg == r, k4, jnp.zeros_like(k4))
        vbd_ref[:NSA_GW, r * kt:(r + 1) * kt] = jnp.where(vblk == r, v4, jnp.zeros_like(v4))
    one_row = lax.broadcasted_iota(jnp.int32, (FLASH_SUM_ROWS, NSA_REP * kt), 0)
    one_head = _div_pow2(lax.broadcasted_iota(jnp.int32, (FLASH_SUM_ROWS, NSA_REP * kt), 1), kt)
    vbd_ref[NSA_GW:, :] = (one_row == one_head).astype(BF16)

    n_sub = tq // FLASH_SUB
    n_blk = kt // SEL_BLOCK

    def body(positional):
        def scores(sub):
            c0 = sub * FLASH_SUB
            cols = slice(c0, c0 + FLASH_SUB)
            s_ref[:, cols] = jnp.dot(kbd_ref[...], qs_ref[:, cols],
                                     preferred_element_type=F32)
            for jb in range(n_blk):
                rows = slice(jb * SEL_BLOCK, (jb + 1) * SEL_BLOCK)
                if selected:
                    picked = sel_ref[pl.ds(ki * n_blk + jb, 1), cols] > 0.5
                if positional:
                    tpos = qi * tq + c0 + lax.broadcasted_iota(jnp.int32, (SEL_BLOCK, FLASH_SUB), 1)
                    kpos = ki * kt + jb * SEL_BLOCK + lax.broadcasted_iota(jnp.int32, (SEL_BLOCK, FLASH_SUB), 0)
                    if selected:
                        allowed = picked & (kpos <= tpos)
                    else:
                        allowed = (kpos <= tpos) & (kpos > tpos - WINDOW)
                else:
                    allowed = jnp.broadcast_to(picked, (SEL_BLOCK, FLASH_SUB))
                bias_ref[rows, cols] = jnp.where(allowed, 0.0, -jnp.inf)

        scores(0)
        for sub in range(n_sub):
            c0 = sub * FLASH_SUB
            cols = slice(c0, c0 + FLASH_SUB)
            if sub + 1 < n_sub:
                scores(sub + 1)
            for ch in range(FLASH_SUB // LANES):
                lanes = slice(c0 + ch * LANES, c0 + (ch + 1) * LANES)
                bias = bias_ref[:, lanes]
                for r in range(NSA_REP):
                    x = s_ref[r * kt:(r + 1) * kt, lanes] + bias
                    m_prev = m_ref[r:r + 1, lanes]
                    m_new = jnp.maximum(m_prev, jnp.max(x, axis=0, keepdims=True))
                    m_ref[r:r + 1, lanes] = m_new
                    alpha_ref[r:r + 1, lanes] = jnp.exp2(m_prev - m_new)
                    p_ref[r * kt:(r + 1) * kt, lanes] = jnp.exp2(x - m_new).astype(BF16)
            pv = jnp.dot(vbd_ref[...], p_ref[:, cols], preferred_element_type=F32)
            for r in range(NSA_REP):
                hd = slice(r * dh, (r + 1) * dh)
                alpha = alpha_ref[r:r + 1, cols]
                acc_ref[hd, cols] = acc_ref[hd, cols] * alpha + pv[hd]
                l_ref[r:r + 1, cols] = l_ref[r:r + 1, cols] * alpha + pv[NSA_GW + r:NSA_GW + r + 1]

    if selected:
        below_diagonal = (ki + 1) * kt <= qi * tq
        pl.when(below_diagonal)(lambda: body(False))
        pl.when(jnp.logical_not(below_diagonal))(lambda: body(True))
    else:
        body(True)

    @pl.when(last_ref[step] == 1)
    def _():
        for ch in range(tq // LANES):
            lanes = slice(ch * LANES, (ch + 1) * LANES)
            out_t = jnp.concatenate(
                [acc_ref[r * dh:(r + 1) * dh, lanes] / l_ref[r:r + 1, lanes] for r in range(NSA_REP)], axis=0)
            o_ref[lanes, :] = out_t.T


def _pair_tables(t, tq, kt, window):
    qi, ki, first, last = [], [], [], []
    for a in range(t // tq):
        lo = 0 if window is None else max(0, (a * tq - window + 1) // kt)
        hi = (a * tq + tq - 1) // kt
        for b in range(lo, hi + 1):
            qi.append(a)
            ki.append(b)
            first.append(int(b == lo))
            last.append(int(b == hi))
    return [jnp.asarray(np.asarray(x, np.int32)) for x in (qi, ki, first, last)]


def flash_branch(q_t, k4, vt4, sel_t, tq, kt):
    qw, t = q_t.shape
    g = k4.shape[0]
    selected = sel_t is not None
    nsel = sel_t.shape[1] if selected else 0
    tables = _pair_tables(t, tq, kt, None if selected else WINDOW)
    in_specs = [
        pl.BlockSpec((NSA_GW, tq), lambda gi, s, qi, ki, fi, la: (gi, qi[s])),
        pl.BlockSpec((None, kt, NSA_GW), lambda gi, s, qi, ki, fi, la: (gi, ki[s], 0)),
        pl.BlockSpec((None, NSA_GW, kt), lambda gi, s, qi, ki, fi, la: (gi, 0, ki[s])),
    ]
    args = [q_t, k4, vt4]
    if selected:
        in_specs.append(pl.BlockSpec((None, nsel, tq), lambda gi, s, qi, ki, fi, la: (gi, 0, qi[s])))
        args.append(sel_t)
    return pl.pallas_call(
        functools.partial(_flash_kernel, tq=tq, kt=kt, nsel=nsel, selected=selected),
        grid_spec=pltpu.PrefetchScalarGridSpec(
            num_scalar_prefetch=4,
            grid=(g, int(tables[0].shape[0])),
            in_specs=in_specs,
            out_specs=pl.BlockSpec((tq, NSA_GW), lambda gi, s, qi, ki, fi, la: (qi[s], gi)),
            scratch_shapes=[
                pltpu.VMEM((8, tq), F32),
                pltpu.VMEM((8, tq), F32),
                pltpu.VMEM((8, tq), F32),
                pltpu.VMEM((NSA_GW, tq), F32),
                pltpu.VMEM((NSA_REP * kt, tq), F32),
                pltpu.VMEM((NSA_REP * kt, tq), BF16),
                pltpu.VMEM((kt, tq), F32),
                pltpu.VMEM((NSA_REP * kt, NSA_GW), BF16),
                pltpu.VMEM((NSA_GW + FLASH_SUM_ROWS, NSA_REP * kt), BF16),
                pltpu.VMEM((NSA_GW, tq), BF16),
            ],
        ),
        out_shape=jax.ShapeDtypeStruct((t, qw), F32),
        compiler_params=_cparams(("parallel", "arbitrary")),
        name="nsa_selected" if selected else "nsa_window",
    )(*tables, *args)


def _nsa_out_kernel(oc_ref, os_ref, ow_ref, gl_ref, eg_ref, w_ref, gpost_ref, h_ref, out_ref):
    pieces = _split3(_sigmoid(gl_ref[...]))
    mixed = jnp.zeros(oc_ref.shape, F32)
    for b, br_ref in enumerate((oc_ref, os_ref, ow_ref)):
        gfull = jnp.zeros(oc_ref.shape, F32)
        for piece in pieces:
            gfull = gfull + jnp.dot(piece, eg_ref[b], preferred_element_type=F32)
        mixed = mixed + gfull * br_ref[...]
    mix = jnp.dot(mixed.astype(BF16), w_ref[...], preferred_element_type=F32)
    out_ref[...] = h_ref[...] + _rms(mix, gpost_ref[...])


def nsa_out(oc, osel, ow, gate_logits, expand, w_o, gpost, h, tm=256):
    t, qw = oc.shape
    d = h.shape[1]
    assert t % tm == 0
    row = lambda w: pl.BlockSpec((tm, w), lambda i: (i, 0))
    return pl.pallas_call(
        _nsa_out_kernel,
        grid=(t // tm,),
        in_specs=[
            row(qw), row(qw), row(qw), row(LANES),
            pl.BlockSpec((3, LANES, qw), lambda i: (0, 0, 0)),
            pl.BlockSpec((qw, d), lambda i: (0, 0)),
            pl.BlockSpec((1, d), lambda i: (0, 0)),
            row(d),
        ],
        out_specs=row(d),
        out_shape=jax.ShapeDtypeStruct((t, d), F32),
        compiler_params=_cparams(("parallel",)),
        name="nsa_out",
    )(oc, osel, ow, gate_logits, expand, w_o, gpost.reshape(1, d), h)


def _pad_cols(w, n):
    return jnp.pad(w, ((0, 0), (0, n - w.shape[1])))


def _importance_matrix(nc, nsel):
    r = SEL_BLOCK // CMP_STRIDE
    c = CMP_BLOCK // CMP_STRIDE
    mat = np.zeros((nc, nsel), np.float32)
    for kblk in range(nsel):
        for m in range(r):
            for n in range(c):
                j = r * kblk + m - n
                if 0 <= j < nc - 1:
                    mat[j, kblk] += 1.0
    return jnp.asarray(mat, BF16)


def _gate_expand():
    e = np.zeros((3, LANES, NSA_GROUPS * NSA_GW), np.float32)
    for head in range(NSA_GROUPS * NSA_REP):
        for b in range(3):
            e[b, head * 3 + b, head * NSA_HEAD_DIM:(head + 1) * NSA_HEAD_DIM] = 1.0
    return jnp.asarray(e, BF16)


def _block_diag_kv(k_cmp, v_cmp):
    g, nc, dh = k_cmp.shape
    eye = jnp.eye(NSA_REP, dtype=bool)
    kct = jnp.swapaxes(k_cmp, 1, 2)
    kbd = jnp.where(eye[None, :, None, :, None], kct[:, None, :, None, :], 0.0)
    vbd = jnp.where(eye[None, :, None, :, None], v_cmp[:, None, :, None, :], 0.0)
    return (kbd.reshape(g, NSA_REP * dh, NSA_REP * nc).astype(BF16),
            vbd.reshape(g, NSA_REP * nc, NSA_REP * dh).astype(BF16))


def kernel(x, p, mix_pre_norm, mix_post_norm, ffn_pre_norm, ffn_post_norm, gdn_w_in, gdn_conv_w, gdn_a_log,
           gdn_dt_bias, gdn_o_norm, gdn_w_out, kv_norm, kv_w, cmp_pos, cmp_w1, cmp_w2, nsa_w_qg, nsa_w_o,
           ffn_w_in, ffn_w_out, ple_w_in, ple_w_gate):
    depth = p.shape[0]
    n_a = gdn_w_in.shape[0]
    t = x.shape[1]
    h = x[0]
    fh = ffn_w_out.shape[1]
    conv_w_cols = gdn_conv_w.shape[2]
    vw = GDN_V_HEADS * GDN_HEAD_DIM
    main_w = conv_w_cols + vw

    def channel_and_ple(h, i):
        w_in = ffn_w_in[i].astype(BF16)
        return ffn_ple(h, ffn_pre_norm[i], w_in[:, :fh], w_in[:, fh:], ffn_w_out[i].astype(BF16),
                       ffn_post_norm[i], p[i, 0], ple_w_in[i].astype(BF16), ple_w_gate[i].astype(BF16))

    for i in range(n_a):
        w_in = gdn_w_in[i]
        w_beta = w_in[:, main_w:main_w + GDN_V_HEADS]
        w_a = w_in[:, main_w + GDN_V_HEADS:]
        w_small = _pad_cols(jnp.concatenate([w_beta, w_a, w_a], axis=1), LANES).astype(BF16)
        proj = norm_matmul(h, mix_pre_norm[i], w_in[:, :main_w].astype(BF16))
        ba = norm_matmul(h, mix_pre_norm[i], w_small)
        pad_vec = lambda v: jnp.pad(v, (GDN_V_HEADS, LANES - 2 * GDN_V_HEADS))
        alog_vec = (pad_vec(gdn_a_log[i]) + jnp.pad(gdn_a_log[i], (2 * GDN_V_HEADS, LANES - 3 * GDN_V_HEADS)))
        dtb_vec = (pad_vec(gdn_dt_bias[i]) + jnp.pad(gdn_dt_bias[i], (2 * GDN_V_HEADS, LANES - 3 * GDN_V_HEADS)))
        slab = gdn_gates(ba, alog_vec.reshape(1, LANES), dtb_vec.reshape(1, LANES))
        qkv_hm = gdn_conv(proj, gdn_conv_w[i])
        kt_hm = jnp.swapaxes(qkv_hm[GDN_QK_HEADS:2 * GDN_QK_HEADS], 1, 2)
        o = gdn_scan(qkv_hm, kt_hm, slab, slab.T)
        onorm_tiled = jnp.tile(gdn_o_norm[i], GDN_V_HEADS).reshape(1, vw)
        h = gdn_out(o, proj, onorm_tiled, gdn_w_out[i].astype(BF16), mix_post_norm[i], h)
        h = channel_and_ple(h, i)

    g = NSA_GROUPS
    dh = NSA_HEAD_DIM
    kv = norm_matmul(h, kv_norm, kv_w.astype(BF16), tn=768)
    kv6 = jnp.transpose(kv.reshape(t, 6, g, dh), (1, 2, 0, 3))
    nc = t // CMP_STRIDE
    nsel = t // SEL_BLOCK
    x2 = kv6[0:2].reshape(2, g, nc, CMP_STRIDE * dh)
    pos2 = cmp_pos.reshape(2, 2, CMP_STRIDE * dh)
    cmp_out = compress(x2, pos2, cmp_w1.astype(BF16), cmp_w2.astype(BF16))
    v_cmp_t, kbd = _block_diag_kv(cmp_out[1], cmp_out[0])
    imp_mat_t = _importance_matrix(nc, nsel).T
    rep_t = lambda a: jnp.tile(jnp.swapaxes(a, 1, 2), (1, NSA_REP, 1)).astype(BF16)
    rep_k = lambda a: jnp.tile(a, (1, 1, NSA_REP)).astype(BF16)
    k_slc, v_slc_t = rep_k(kv6[2]), rep_t(kv6[3])
    k_win, v_win_t = rep_k(kv6[4]), rep_t(kv6[5])
    expand = _gate_expand()

    for i in range(n_a, depth):
        j = i - n_a
        qw = g * NSA_GW
        w_qg = nsa_w_qg[j]
        q = norm_matmul(h, mix_pre_norm[i], w_qg[:, :qw].astype(BF16))
        gate_logits = norm_matmul(h, mix_pre_norm[i], _pad_cols(w_qg[:, qw:], LANES).astype(BF16))
        q_t = q.T
        o_c, sel_t = cmp_topk(q_t, kbd, v_cmp_t, imp_mat_t)
        o_s = flash_branch(q_t, k_slc, v_slc_t, sel_t, tq=1024, kt=256)
        o_w = flash_branch(q_t, k_win, v_win_t, None, tq=512, kt=256)
        h = nsa_out(o_c, o_s, o_w, gate_logits, expand, nsa_w_o[j].astype(BF16), mix_post_norm[i], h)
        h = channel_and_ple(h, i)
    return h[None]
```

```python
import functools

import numpy as np
import jax
import jax.numpy as jnp
from jax import lax
from jax.experimental import pallas as pl
from jax.experimental.pallas import tpu as pltpu

F32 = jnp.float32
BF16 = jnp.bfloat16

NORM_EPS = 1e-6
L2_EPS = 1e-6
GDN_QK_HEADS = 8
GDN_V_HEADS = 16
GDN_HEAD_DIM = 128
GDN_CONV = 4
GDN_CHUNK = 64
GDN_GROUP = 256
GDN_SCAN_HEADS = 4
GDN_CONV_HEADS = 4
NSA_GROUPS = 4
NSA_REP = 4
NSA_HEAD_DIM = 64
NSA_GW = NSA_REP * NSA_HEAD_DIM
CMP_BLOCK = 32
CMP_STRIDE = 16
SEL_BLOCK = 64
SEL_TOPK = 16
WINDOW = 512
FORCED_SCORE = 1e4
LANES = 128
NEG_INIT = -(2.0 ** 100)
LOG2_E = 1.4426950408889634
FLASH_KT = 256
FLASH_SUB = 256
FLASH_SUM_ROWS = 16
CMP_ROWS = 256

VMEM_LIMIT = 56 * 1024 * 1024


def _cparams(sem):
    return pltpu.CompilerParams(dimension_semantics=sem, vmem_limit_bytes=VMEM_LIMIT)


def _rms(x, gain):
    return x * lax.rsqrt(jnp.mean(x * x, axis=-1, keepdims=True) + NORM_EPS) * gain


def _mm(a, b):
    return jnp.dot(a.astype(BF16), b.astype(BF16), preferred_element_type=F32)


def _sigmoid(x):
    return 1.0 / (1.0 + jnp.exp(-x))


def _div_pow2(x, d):
    shift = d.bit_length() - 1
    assert d == 1 << shift
    return jnp.right_shift(x, shift)


def _split3(x):
    a = x.astype(BF16)
    r = x - a.astype(F32)
    b = r.astype(BF16)
    c = (r - b.astype(F32)).astype(BF16)
    return a, b, c


def _norm_matmul_kernel(x_ref, g_ref, w_ref, o_ref, xn_ref):
    @pl.when(pl.program_id(1) == 0)
    def _():
        xn_ref[...] = _rms(x_ref[...], g_ref[...]).astype(BF16)

    o_ref[...] = jnp.dot(xn_ref[...], w_ref[...], preferred_element_type=F32)


def norm_matmul(h, gain, w, tm=512, tn=1024):
    t, d = h.shape
    n = w.shape[1]
    tn = min(tn, n)
    assert t % tm == 0 and n % tn == 0
    return pl.pallas_call(
        _norm_matmul_kernel,
        grid=(t // tm, n // tn),
        in_specs=[
            pl.BlockSpec((tm, d), lambda i, j: (i, 0)),
            pl.BlockSpec((1, d), lambda i, j: (0, 0)),
            pl.BlockSpec((d, tn), lambda i, j: (0, j)),
        ],
        out_specs=pl.BlockSpec((tm, tn), lambda i, j: (i, j)),
        out_shape=jax.ShapeDtypeStruct((t, n), F32),
        scratch_shapes=[pltpu.VMEM((tm, d), BF16)],
        compiler_params=_cparams(("parallel", "arbitrary")),
        name="norm_matmul",
    )(h, gain.reshape(1, d), w)


def _ffn_ple_kernel(h_ref, gpre_ref, wg_ref, wu_ref, wo_ref, gpost_ref, p_ref, wple_ref, wgt_ref,
                    o_ref, xn_ref, acc_ref, *, nf):
    f = pl.program_id(1)

    @pl.when(f == 0)
    def _():
        xn_ref[...] = _rms(h_ref[...], gpre_ref[...]).astype(BF16)
        acc_ref[...] = jnp.zeros_like(acc_ref)

    xn = xn_ref[...]
    gate = jnp.dot(xn, wg_ref[...], preferred_element_type=F32)
    up = jnp.dot(xn, wu_ref[...], preferred_element_type=F32)
    act = gate * _sigmoid(gate) * up
    acc_ref[...] += jnp.dot(act.astype(BF16), wo_ref[...], preferred_element_type=F32)

    @pl.when(f == nf - 1)
    def _():
        h2 = h_ref[...] + _rms(acc_ref[...], gpost_ref[...])
        emb = jnp.dot(p_ref[...].astype(BF16), wple_ref[...], preferred_element_type=F32)
        gt = _sigmoid(jnp.dot(h2.astype(BF16), wgt_ref[...], preferred_element_type=F32))
        o_ref[...] = h2 + emb * gt


def ffn_ple(h, gpre, w_gate, w_up, w_out, gpost, p, w_ple, w_plegate, tm=512, tf=1408):
    t, d = h.shape
    fh = w_gate.shape[1]
    pd = p.shape[1]
    assert t % tm == 0 and fh % tf == 0
    nf = fh // tf
    return pl.pallas_call(
        functools.partial(_ffn_ple_kernel, nf=nf),
        grid=(t // tm, nf),
        in_specs=[
            pl.BlockSpec((tm, d), lambda i, f: (i, 0)),
            pl.BlockSpec((1, d), lambda i, f: (0, 0)),
            pl.BlockSpec((d, tf), lambda i, f: (0, f)),
            pl.BlockSpec((d, tf), lambda i, f: (0, f)),
            pl.BlockSpec((tf, d), lambda i, f: (f, 0)),
            pl.BlockSpec((1, d), lambda i, f: (0, 0)),
            pl.BlockSpec((tm, pd), lambda i, f: (i, 0)),
            pl.BlockSpec((pd, d), lambda i, f: (0, 0)),
            pl.BlockSpec((d, d), lambda i, f: (0, 0)),
        ],
        out_specs=pl.BlockSpec((tm, d), lambda i, f: (i, 0)),
        out_shape=jax.ShapeDtypeStruct((t, d), F32),
        scratch_shapes=[pltpu.VMEM((tm, d), BF16), pltpu.VMEM((tm, d), F32)],
        compiler_params=_cparams(("parallel", "arbitrary")),
        name="ffn_ple",
    )(h, gpre.reshape(1, d), w_gate, w_up, w_out, gpost.reshape(1, d), p, w_ple, w_plegate)


def _gdn_conv_kernel(x_ref, halo_ref, w_ref, o_ref, *, tm):
    c = pl.program_id(0)
    i = pl.program_id(1)
    x = x_ref[...]
    halo = jnp.where(i > 0, halo_ref[...], 0.0)
    ext = jnp.concatenate([halo, x], axis=0)
    w = w_ref[...]
    y = x * w[GDN_CONV - 1:GDN_CONV, :]
    for k in range(1, GDN_CONV):
        shifted = pltpu.roll(ext, k, axis=0)[8:8 + tm]
        y = y + shifted * w[GDN_CONV - 1 - k:GDN_CONV - k, :]
    y = y * _sigmoid(y)
    for hd in range(GDN_CONV_HEADS):
        head = c * GDN_CONV_HEADS + hd
        seg = y[:, hd * LANES:(hd + 1) * LANES]
        normed = seg * lax.rsqrt(jnp.sum(seg * seg, axis=-1, keepdims=True) + L2_EPS)
        q_scale = jnp.where(head < GDN_QK_HEADS, GDN_HEAD_DIM ** -0.5, 1.0)
        o_ref[hd] = jnp.where(head < 2 * GDN_QK_HEADS, normed * q_scale, seg)


def gdn_conv(proj, conv_w, tm=1024):
    t = proj.shape[0]
    n_tiles = conv_w.shape[1] // LANES
    cw = GDN_CONV_HEADS
    assert t % tm == 0 and n_tiles % cw == 0
    return pl.pallas_call(
        functools.partial(_gdn_conv_kernel, tm=tm),
        grid=(n_tiles // cw, t // tm),
        in_specs=[
            pl.BlockSpec((tm, cw * LANES), lambda c, i: (i, c)),
            pl.BlockSpec((8, cw * LANES), lambda c, i: (jnp.maximum(i * (tm // 8) - 1, 0), c)),
            pl.BlockSpec((GDN_CONV, cw * LANES), lambda c, i: (0, c)),
        ],
        out_specs=pl.BlockSpec((cw, tm, LANES), lambda c, i: (c, i, 0)),
        out_shape=jax.ShapeDtypeStruct((n_tiles, t, LANES), F32),
        compiler_params=_cparams(("parallel", "parallel")),
        name="gdn_conv",
    )(proj, proj, conv_w)


def _gdn_gate_kernel(x_ref, alog_ref, dtb_ref, lc_ref, lf_ref, o_ref):
    x = x_ref[...]
    lane = lax.broadcasted_iota(jnp.int32, x.shape, 1)
    beta = _sigmoid(x)
    z = x + dtb_ref[...]
    softplus = jnp.maximum(z, 0.0) + jnp.log(1.0 + jnp.exp(-jnp.abs(z)))
    g = -jnp.exp(alog_ref[...]) * softplus
    gcum = jnp.zeros_like(x)
    gtot = jnp.zeros_like(x)
    for piece in _split3(g):
        gcum = gcum + jnp.dot(lc_ref[...], piece, preferred_element_type=F32)
        gtot = gtot + jnp.dot(lf_ref[...], piece, preferred_element_type=F32)
    o_ref[...] = jnp.where(lane < GDN_V_HEADS, beta, jnp.where(lane < 2 * GDN_V_HEADS, gcum, gtot))


def gdn_gates(ba, alog_vec, dtb_vec):
    t = ba.shape[0]
    tm = GDN_GROUP
    r = np.arange(tm)
    same = (r[:, None] // GDN_CHUNK) == (r[None, :] // GDN_CHUNK)
    lc = jnp.asarray(same & (r[None, :] <= r[:, None]), BF16)
    lf = jnp.asarray(same, BF16)
    return pl.pallas_call(
        _gdn_gate_kernel,
        grid=(t // tm,),
        in_specs=[
            pl.BlockSpec((tm, LANES), lambda i: (i, 0)),
            pl.BlockSpec((1, LANES), lambda i: (0, 0)),
            pl.BlockSpec((1, LANES), lambda i: (0, 0)),
            pl.BlockSpec((tm, tm), lambda i: (0, 0)),
            pl.BlockSpec((tm, tm), lambda i: (0, 0)),
        ],
        out_specs=pl.BlockSpec((tm, LANES), lambda i: (i, 0)),
        out_shape=jax.ShapeDtypeStruct((t, LANES), F32),
        compiler_params=_cparams(("parallel",)),
        name="gdn_gates",
    )(ba, alog_vec, dtb_vec, lc, lf)


def _gdn_scan_kernel(q_ref, k_ref, kt_ref, v_ref, slab_ref, slabt_ref, o_ref, s_ref):
    n = pl.program_id(1)

    @pl.when(n == 0)
    def _():
        s_ref[...] = jnp.zeros_like(s_ref)

    L = GDN_GROUP
    C = GDN_CHUNK
    D = GDN_HEAD_DIM
    slab = slab_ref[...]
    slabt = slabt_ref[...]
    row = lax.broadcasted_iota(jnp.int32, (L, L), 0)
    col = lax.broadcasted_iota(jnp.int32, (L, L), 1)
    same = _div_pow2(row, C) == _div_pow2(col, C)
    causal = same & (col <= row)
    strict = same & (col < row)
    eye = (row == col).astype(F32)
    lane = lax.broadcasted_iota(jnp.int32, (L, LANES), 1)
    sub = lax.broadcasted_iota(jnp.int32, (LANES, L), 0)

    def column(idx):
        return jnp.sum(jnp.where(lane == idx, slab, 0.0), axis=1, keepdims=True)

    def rowvec(idx):
        return jnp.sum(jnp.where(sub == idx, slabt, 0.0), axis=0, keepdims=True)

    colk = lax.broadcasted_iota(jnp.int32, (D, L), 1)
    slots = range(2 * GDN_SCAN_HEADS)

    q = [q_ref[hq] for hq in range(GDN_SCAN_HEADS)]
    k = [k_ref[hq] for hq in range(GDN_SCAN_HEADS)]
    kt = [kt_ref[hq] for hq in range(GDN_SCAN_HEADS)]
    kk = [_mm(k[hq], kt[hq]) for hq in range(GDN_SCAN_HEADS)]
    qk = [_mm(q[hq], kt[hq]) for hq in range(GDN_SCAN_HEADS)]
    beta_c, gc_c, gt_c, decay, bp, inv, kdt = [], [], [], [], [], [], []
    for slot in slots:
        hq = slot // 2
        hv = 2 * (pl.program_id(0) * GDN_SCAN_HEADS + hq) + slot % 2
        beta_c.append(column(hv))
        gc_c.append(column(GDN_V_HEADS + hv))
        gt_c.append(column(2 * GDN_V_HEADS + hv))
        gc_r = rowvec(GDN_V_HEADS + hv)
        gt_r = rowvec(2 * GDN_V_HEADS + hv)
        decay.append(jnp.where(causal, jnp.exp(jnp.where(causal, gc_c[slot] - gc_r, 0.0)), 0.0))
        bp.append(jnp.where(strict, -(kk[hq] * beta_c[slot]) * decay[slot], 0.0))
        inv.append(eye + bp[slot])
        kdt.append(kt[hq] * jnp.exp(gt_r - gc_r))
    for _ in range(5):
        bp = [_mm(bp[slot], bp[slot]) for slot in slots]
        inv = [inv[slot] + _mm(inv[slot], bp[slot]) for slot in slots]
    u, w, qkm, q_dec = [], [], [], []
    for slot in slots:
        hq = slot // 2
        egc = jnp.exp(gc_c[slot])
        rhs = jnp.concatenate([v_ref[slot] * beta_c[slot], k[hq] * (beta_c[slot] * egc)], axis=1)
        sol = _mm(inv[slot], rhs)
        u.append(sol[:, :D])
        w.append(sol[:, D:])
        qkm.append(jnp.where(causal, qk[hq] * decay[slot], 0.0))
        q_dec.append(q[hq] * egc)
    state = [s_ref[slot] for slot in slots]
    v_done = [[] for _ in slots]
    for c in range(L // C):
        lo, hi = c * C, (c + 1) * C
        r = [_mm(jnp.concatenate([w[slot][lo:hi], q_dec[slot][lo:hi]], axis=0), state[slot]) for slot in slots]
        for slot in slots:
            v_done[slot].append(u[slot][lo:hi] - r[slot][:C])
            v_all = jnp.concatenate(v_done[slot] + [jnp.zeros((L - hi, D), F32)] * (hi < L), axis=0)
            o_ref[lo:hi, slot * D:(slot + 1) * D] = r[slot][C:] + _mm(qkm[slot][lo:hi, :], v_all)
            kdt_c = jnp.where((colk >= lo) & (colk < hi), kdt[slot], 0.0)
            state[slot] = state[slot] * jnp.exp(gt_c[slot][lo:lo + 1, :]) + _mm(kdt_c, v_all)
    for slot in slots:
        s_ref[slot] = state[slot]


def gdn_scan(qkv_hm, kt_hm, slab, slabt):
    t = qkv_hm.shape[1]
    L = GDN_GROUP
    D = GDN_HEAD_DIM
    hq = GDN_SCAN_HEADS
    assert t % L == 0 and GDN_QK_HEADS % hq == 0
    q_blocks = GDN_QK_HEADS // hq
    return pl.pallas_call(
        _gdn_scan_kernel,
        grid=(q_blocks, t // L),
        in_specs=[
            pl.BlockSpec((hq, L, D), lambda j, n: (j, n, 0)),
            pl.BlockSpec((hq, L, D), lambda j, n: (q_blocks + j, n, 0)),
            pl.BlockSpec((hq, D, L), lambda j, n: (j, 0, n)),
            pl.BlockSpec((2 * hq, L, D), lambda j, n: (q_blocks + j, n, 0)),
            pl.BlockSpec((L, LANES), lambda j, n: (n, 0)),
            pl.BlockSpec((LANES, L), lambda j, n: (0, n)),
        ],
        out_specs=pl.BlockSpec((L, 2 * hq * D), lambda j, n: (n, j)),
        out_shape=jax.ShapeDtypeStruct((t, GDN_V_HEADS * D), F32),
        scratch_shapes=[pltpu.VMEM((2 * hq, D, D), F32)],
        compiler_params=_cparams(("parallel", "arbitrary")),
        name="gdn_scan",
    )(qkv_hm, qkv_hm, kt_hm, qkv_hm, slab, slabt)


def _gdn_out_kernel(o_ref, z_ref, onorm_ref, w_ref, gpost_ref, h_ref, out_ref):
    o = o_ref[...]
    z = z_ref[...]
    parts = []
    for hd in range(GDN_V_HEADS):
        seg = o[:, hd * GDN_HEAD_DIM:(hd + 1) * GDN_HEAD_DIM]
        parts.append(seg * lax.rsqrt(jnp.mean(seg * seg, axis=-1, keepdims=True) + NORM_EPS))
    gated = jnp.concatenate(parts, axis=1) * onorm_ref[...] * (z * _sigmoid(z))
    mix = jnp.dot(gated.astype(BF16), w_ref[...], preferred_element_type=F32)
    out_ref[...] = h_ref[...] + _rms(mix, gpost_ref[...])


def gdn_out(o, proj, onorm_tiled, w_out, gpost, h, tm=256):
    t, vw = o.shape
    d = h.shape[1]
    z_blk = (proj.shape[1] - vw) // vw
    assert proj.shape[1] % vw == 0 and t % tm == 0
    return pl.pallas_call(
        _gdn_out_kernel,
        grid=(t // tm,),
        in_specs=[
            pl.BlockSpec((tm, vw), lambda i: (i, 0)),
            pl.BlockSpec((tm, vw), lambda i: (i, z_blk)),
            pl.BlockSpec((1, vw), lambda i: (0, 0)),
            pl.BlockSpec((vw, d), lambda i: (0, 0)),
            pl.BlockSpec((1, d), lambda i: (0, 0)),
            pl.BlockSpec((tm, d), lambda i: (i, 0)),
        ],
        out_specs=pl.BlockSpec((tm, d), lambda i: (i, 0)),
        out_shape=jax.ShapeDtypeStruct((t, d), F32),
        compiler_params=_cparams(("parallel",)),
        name="gdn_out",
    )(o, proj, onorm_tiled, w_out, gpost.reshape(1, d), h)


def _compress_kernel(x_ref, pos_ref, w1_ref, w2_ref, o_ref):
    x = x_ref[...]
    pos = pos_ref[...]
    nc, half = x.shape
    w1 = w1_ref[...]
    first = _mm(x + pos[0:1, :], w1[:half])
    second = _mm(x + pos[1:2, :], w1[half:])
    hid = first + pltpu.roll(second, nc - 1, axis=0)
    hid = hid * _sigmoid(hid)
    out = jnp.dot(hid.astype(BF16), w2_ref[...], preferred_element_type=F32)
    rowi = lax.broadcasted_iota(jnp.int32, out.shape, 0)
    o_ref[...] = jnp.where(rowi < nc - 1, out, 0.0)


def compress(x2, pos2, w1, w2):
    _, g, nc, wdt = x2.shape
    hid = w1.shape[2]
    dh = w2.shape[2]
    return pl.pallas_call(
        _compress_kernel,
        grid=(2, g),
        in_specs=[
            pl.BlockSpec((None, None, nc, wdt), lambda b, gi: (b, gi, 0, 0)),
            pl.BlockSpec((None, 2, wdt), lambda b, gi: (b, 0, 0)),
            pl.BlockSpec((None, 2 * wdt, hid), lambda b, gi: (b, 0, 0)),
            pl.BlockSpec((None, hid, dh), lambda b, gi: (b, 0, 0)),
        ],
        out_specs=pl.BlockSpec((None, None, nc, dh), lambda b, gi: (b, gi, 0, 0)),
        out_shape=jax.ShapeDtypeStruct((2, g, nc, dh), F32),
        compiler_params=_cparams(("parallel", "parallel")),
        name="nsa_compress",
    )(x2, pos2, w1, w2)


def _cmp_topk_kernel(q_ref, kbd_ref, vt_ref, m_ref, oc_ref, sel_ref, s_ref, p_ref, psum_ref, *,
                     tq, nc, nsel, topk):
    i = pl.program_id(1)
    dh = NSA_HEAD_DIM
    rows_per = min(nc, CMP_ROWS)
    q = (q_ref[...] * ((dh ** -0.5) * LOG2_E)).astype(BF16)
    s_ref[...] = jnp.dot(kbd_ref[...], q, preferred_element_type=F32)
    for ch in range(tq // LANES):
        lanes = slice(ch * LANES, (ch + 1) * LANES)
        tpos = i * tq + ch * LANES + lax.broadcasted_iota(jnp.int32, (rows_per, LANES), 1)
        cblk0 = lax.broadcasted_iota(jnp.int32, (rows_per, LANES), 0)
        masks = [(CMP_STRIDE * (cblk0 + c * rows_per) + CMP_BLOCK - 1) <= tpos for c in range(nc // rows_per)]
        for r in range(NSA_REP):
            pieces = [slice(r * nc + c * rows_per, r * nc + (c + 1) * rows_per) for c in range(nc // rows_per)]
            m = jnp.full((1, LANES), -jnp.inf, F32)
            for rows, mask in zip(pieces, masks):
                m = jnp.maximum(m, jnp.max(jnp.where(mask, s_ref[rows, lanes], -jnp.inf), axis=0, keepdims=True))
            m = jnp.where(m > -jnp.inf, m, 0.0)
            total = jnp.zeros((1, LANES), F32)
            for rows, mask in zip(pieces, masks):
                e = jnp.exp2(jnp.where(mask, s_ref[rows, lanes], -jnp.inf) - m)
                s_ref[rows, lanes] = e
                total = total + jnp.sum(e, axis=0, keepdims=True)
            inv = 1.0 / jnp.maximum(total, 1e-30)
            for c, rows in enumerate(pieces):
                p = s_ref[rows, lanes] * inv
                p_ref[rows, lanes] = p.astype(BF16)
                prow = slice(c * rows_per, (c + 1) * rows_per)
                psum_ref[prow, lanes] = p if r == 0 else psum_ref[prow, lanes] + p
    oc_t = jnp.dot(vt_ref[...], p_ref[...], preferred_element_type=F32)
    imp = jnp.zeros((nsel, tq), F32)
    for piece in _split3(psum_ref[...]):
        imp = imp + jnp.dot(m_ref[...], piece, preferred_element_type=F32)
    for ch in range(tq // LANES):
        lanes = slice(ch * LANES, (ch + 1) * LANES)
        oc_ref[lanes, :] = oc_t[:, lanes].T
        t1 = i * tq + ch * LANES + lax.broadcasted_iota(jnp.int32, (nsel, LANES), 1)
        blk = lax.broadcasted_iota(jnp.int32, (nsel, LANES), 0)
        cur = _div_pow2(t1, SEL_BLOCK)
        forced = (blk == 0) | (blk == cur) | (blk == cur - 1)
        valid = blk * SEL_BLOCK <= t1
        score = jnp.where(valid, jnp.where(forced, FORCED_SCORE, imp[:, lanes]), -jnp.inf)
        blkf = blk.astype(F32)
        work = score
        for _ in range(topk):
            mx = jnp.max(work, axis=0, keepdims=True)
            first = jnp.min(jnp.where(work == mx, blkf, float(nsel)), axis=0, keepdims=True)
            work = jnp.where(blkf == first, -jnp.inf, work)
        sel_ref[:, lanes] = jnp.where((score > -jnp.inf) & (work == -jnp.inf), 1.0, 0.0).astype(sel_ref.dtype)


def cmp_topk(q_t, kbd, v_t, imp_mat_t, tq=256):
    qw, t = q_t.shape
    g = kbd.shape[0]
    nc = kbd.shape[1] // NSA_REP
    nsel = imp_mat_t.shape[0]
    topk = min(SEL_TOPK, nsel)
    assert t % tq == 0 and nc % min(nc, CMP_ROWS) == 0
    return pl.pallas_call(
        functools.partial(_cmp_topk_kernel, tq=tq, nc=nc, nsel=nsel, topk=topk),
        grid=(g, t // tq),
        in_specs=[
            pl.BlockSpec((NSA_GW, tq), lambda gi, i: (gi, i)),
            pl.BlockSpec((None, NSA_REP * nc, NSA_GW), lambda gi, i: (gi, 0, 0)),
            pl.BlockSpec((None, NSA_GW, NSA_REP * nc), lambda gi, i: (gi, 0, 0)),
            pl.BlockSpec((nsel, nc), lambda gi, i: (0, 0)),
        ],
        out_specs=[
            pl.BlockSpec((tq, NSA_GW), lambda gi, i: (i, gi)),
            pl.BlockSpec((None, nsel, tq), lambda gi, i: (gi, 0, i)),
        ],
        out_shape=[
            jax.ShapeDtypeStruct((t, qw), F32),
            jax.ShapeDtypeStruct((g, nsel, t), F32),
        ],
        scratch_shapes=[
            pltpu.VMEM((NSA_REP * nc, tq), F32),
            pltpu.VMEM((NSA_REP * nc, tq), BF16),
            pltpu.VMEM((nc, tq), F32),
        ],
        compiler_params=_cparams(("parallel", "parallel")),
        name="nsa_cmp_topk",
    )(q_t, kbd, v_t, imp_mat_t)


def _flash_kernel(*refs, tq, kt, selected):
    if selected:
        q_ref, k_ref, v_ref, sel_ref, o_ref = refs[:5]
    else:
        q_ref, k_ref, v_ref, o_ref = refs[:4]
    m_ref, l_ref, alpha_ref, acc_ref, s_ref, p_ref, bias_ref, kbd_ref, vbd_ref, qs_ref = refs[-10:]
    qi = pl.program_id(1)
    dh = NSA_HEAD_DIM
    n_sub = tq // FLASH_SUB
    n_blk = kt // SEL_BLOCK

    m_ref[...] = jnp.full_like(m_ref, NEG_INIT)
    l_ref[...] = jnp.zeros_like(l_ref)
    acc_ref[...] = jnp.zeros_like(acc_ref)
    qs_ref[...] = (q_ref[...] * ((dh ** -0.5) * LOG2_E)).astype(BF16)
    one_row = lax.broadcasted_iota(jnp.int32, (FLASH_SUM_ROWS, NSA_REP * kt), 0)
    one_head = _div_pow2(lax.broadcasted_iota(jnp.int32, (FLASH_SUM_ROWS, NSA_REP * kt), 1), kt)
    vbd_ref[NSA_GW:, :] = (one_row == one_head).astype(BF16)

    def key_tile(ki, positional):
        k4 = k_ref[ki]
        kseg = _div_pow2(lax.broadcasted_iota(jnp.int32, k4.shape, 1), dh)
        v4 = v_ref[ki]
        vblk = _div_pow2(lax.broadcasted_iota(jnp.int32, v4.shape, 0), dh)
        for r in range(NSA_REP):
            kbd_ref[r * kt:(r + 1) * kt, :] = jnp.where(kseg == r, k4, jnp.zeros_like(k4))
            vbd_ref[:NSA_GW, r * kt:(r + 1) * kt] = jnp.where(vblk == r, v4, jnp.zeros_like(v4))

        def scores(sub):
            c0 = sub * FLASH_SUB
            cols = slice(c0, c0 + FLASH_SUB)
            s_ref[:, cols] = jnp.dot(kbd_ref[...], qs_ref[:, cols],
                                     preferred_element_type=F32)
            for jb in range(n_blk):
                rows = slice(jb * SEL_BLOCK, (jb + 1) * SEL_BLOCK)
                if selected:
                    picked = sel_ref[pl.ds(ki * n_blk + jb, 1), cols] > 0.5
                if positional:
                    tpos = qi * tq + c0 + lax.broadcasted_iota(jnp.int32, (SEL_BLOCK, FLASH_SUB), 1)
                    kpos = ki * kt + jb * SEL_BLOCK + lax.broadcasted_iota(jnp.int32, (SEL_BLOCK, FLASH_SUB), 0)
                    if selected:
                        allowed = picked & (kpos <= tpos)
                    else:
                        allowed = (kpos <= tpos) & (kpos > tpos - WINDOW)
                else:
                    allowed = jnp.broadcast_to(picked, (SEL_BLOCK, FLASH_SUB))
                bias_ref[rows, cols] = jnp.where(allowed, 0.0, -jnp.inf)

        scores(0)
        for sub in range(n_sub):
            c0 = sub * FLASH_SUB
            cols = slice(c0, c0 + FLASH_SUB)
            if sub + 1 < n_sub:
                scores(sub + 1)
            for ch in range(FLASH_SUB // LANES):
                lanes = slice(c0 + ch * LANES, c0 + (ch + 1) * LANES)
                bias = bias_ref[:, lanes]
                for r in range(NSA_REP):
                    x = s_ref[r * kt:(r + 1) * kt, lanes] + bias
                    m_prev = m_ref[r:r + 1, lanes]
                    m_new = jnp.maximum(m_prev, jnp.max(x, axis=0, keepdims=True))
                    m_ref[r:r + 1, lanes] = m_new
                    alpha_ref[r:r + 1, lanes] = jnp.exp2(m_prev - m_new)
                    p_ref[r * kt:(r + 1) * kt, lanes] = jnp.exp2(x - m_new).astype(BF16)
            pv = jnp.dot(vbd_ref[...], p_ref[:, cols], preferred_element_type=F32)
            for r in range(NSA_REP):
                hd = slice(r * dh, (r + 1) * dh)
                alpha = alpha_ref[r:r + 1, cols]
                acc_ref[hd, cols] = acc_ref[hd, cols] * alpha + pv[hd]
                l_ref[r:r + 1, cols] = l_ref[r:r + 1, cols] * alpha + pv[NSA_GW + r:NSA_GW + r + 1]

    def walk(lo, hi, positional):
        def step(ki, carry):
            key_tile(ki, positional)
            return carry
        lax.fori_loop(lo, hi, step, 0)

    diag = qi * (tq // kt)
    if selected:
        walk(0, diag, False)
        walk(diag, diag + tq // kt, True)
    else:
        first = jnp.maximum(qi * tq - (WINDOW - 1), 0) // kt
        walk(first, diag + tq // kt, True)

    for ch in range(tq // LANES):
        lanes = slice(ch * LANES, (ch + 1) * LANES)
        out_t = jnp.concatenate(
            [acc_ref[r * dh:(r + 1) * dh, lanes] / l_ref[r:r + 1, lanes] for r in range(NSA_REP)], axis=0)
        o_ref[lanes, :] = out_t.T


def flash_branch(q_t, k4, vt4, sel_t, tq, kt):
    qw, t = q_t.shape
    g, nk = k4.shape[:2]
    selected = sel_t is not None
    assert t % tq == 0 and tq % kt == 0 and kt % SEL_BLOCK == 0 and tq % FLASH_SUB == 0
    in_specs = [
        pl.BlockSpec((NSA_GW, tq), lambda gi, i: (gi, i)),
        pl.BlockSpec((None, nk, kt, NSA_GW), lambda gi, i: (gi, 0, 0, 0)),
        pl.BlockSpec((None, nk, NSA_GW, kt), lambda gi, i: (gi, 0, 0, 0)),
    ]
    args = [q_t, k4, vt4]
    if selected:
        nsel = sel_t.shape[1]
        in_specs.append(pl.BlockSpec((None, nsel, tq), lambda gi, i: (gi, 0, i)))
        args.append(sel_t)
    return pl.pallas_call(
        functools.partial(_flash_kernel, tq=tq, kt=kt, selected=selected),
        grid=(g, t // tq),
        in_specs=in_specs,
        out_specs=pl.BlockSpec((tq, NSA_GW), lambda gi, i: (i, gi)),
        scratch_shapes=[
            pltpu.VMEM((8, tq), F32),
            pltpu.VMEM((8, tq), F32),
            pltpu.VMEM((8, tq), F32),
            pltpu.VMEM((NSA_GW, tq), F32),
            pltpu.VMEM((NSA_REP * kt, tq), F32),
            pltpu.VMEM((NSA_REP * kt, tq), BF16),
            pltpu.VMEM((kt, tq), F32),
            pltpu.VMEM((NSA_REP * kt, NSA_GW), BF16),
            pltpu.VMEM((NSA_GW + FLASH_SUM_ROWS, NSA_REP * kt), BF16),
            pltpu.VMEM((NSA_GW, tq), BF16),
        ],
        out_shape=jax.ShapeDtypeStruct((t, qw), F32),
        compiler_params=_cparams(("parallel", "arbitrary")),
        name="nsa_selected" if selected else "nsa_window",
    )(*args)


def _nsa_out_kernel(oc_ref, os_ref, ow_ref, gl_ref, eg_ref, w_ref, gpost_ref, h_ref, out_ref):
    pieces = _split3(_sigmoid(gl_ref[...]))
    mixed = jnp.zeros(oc_ref.shape, F32)
    for b, br_ref in enumerate((oc_ref, os_ref, ow_ref)):
        gfull = jnp.zeros(oc_ref.shape, F32)
        for piece in pieces:
            gfull = gfull + jnp.dot(piece, eg_ref[b], preferred_element_type=F32)
        mixed = mixed + gfull * br_ref[...]
    mix = jnp.dot(mixed.astype(BF16), w_ref[...], preferred_element_type=F32)
    out_ref[...] = h_ref[...] + _rms(mix, gpost_ref[...])


def nsa_out(oc, osel, ow, gate_logits, expand, w_o, gpost, h, tm=256):
    t, qw = oc.shape
    d = h.shape[1]
    assert t % tm == 0
    row = lambda w: pl.BlockSpec((tm, w), lambda i: (i, 0))
    return pl.pallas_call(
        _nsa_out_kernel,
        grid=(t // tm,),
        in_specs=[
            row(qw), row(qw), row(qw), row(LANES),
            pl.BlockSpec((3, LANES, qw), lambda i: (0, 0, 0)),
            pl.BlockSpec((qw, d), lambda i: (0, 0)),
            pl.BlockSpec((1, d), lambda i: (0, 0)),
            row(d),
        ],
        out_specs=row(d),
        out_shape=jax.ShapeDtypeStruct((t, d), F32),
        compiler_params=_cparams(("parallel",)),
        name="nsa_out",
    )(oc, osel, ow, gate_logits, expand, w_o, gpost.reshape(1, d), h)


def _pad_cols(w, n):
    return jnp.pad(w, ((0, 0), (0, n - w.shape[1])))


def _importance_matrix(nc, nsel):
    r = SEL_BLOCK // CMP_STRIDE
    c = CMP_BLOCK // CMP_STRIDE
    mat = np.zeros((nc, nsel), np.float32)
    for kblk in range(nsel):
        for m in range(r):
            for n in range(c):
                j = r * kblk + m - n
                if 0 <= j < nc - 1:
                    mat[j, kblk] += 1.0
    return jnp.asarray(mat, BF16)


def _gate_expand():
    e = np.zeros((3, LANES, NSA_GROUPS * NSA_GW), np.float32)
    for head in range(NSA_GROUPS * NSA_REP):
        for b in range(3):
            e[b, head * 3 + b, head * NSA_HEAD_DIM:(head + 1) * NSA_HEAD_DIM] = 1.0
    return jnp.asarray(e, BF16)


def _block_diag_kv(k_cmp, v_cmp):
    g, nc, dh = k_cmp.shape
    eye = jnp.eye(NSA_REP, dtype=bool)
    kct = jnp.swapaxes(k_cmp, 1, 2)
    kbd = jnp.where(eye[None, :, None, :, None], kct[:, None, :, None, :], 0.0)
    vbd = jnp.where(eye[None, :, None, :, None], v_cmp[:, None, :, None, :], 0.0)
    return (kbd.reshape(g, NSA_REP * dh, NSA_REP * nc).astype(BF16),
            vbd.reshape(g, NSA_REP * nc, NSA_REP * dh).astype(BF16))


def kernel(x, p, mix_pre_norm, mix_post_norm, ffn_pre_norm, ffn_post_norm, gdn_w_in, gdn_conv_w, gdn_a_log,
           gdn_dt_bias, gdn_o_norm, gdn_w_out, kv_norm, kv_w, cmp_pos, cmp_w1, cmp_w2, nsa_w_qg, nsa_w_o,
           ffn_w_in, ffn_w_out, ple_w_in, ple_w_gate):
    depth = p.shape[0]
    n_a = gdn_w_in.shape[0]
    t = x.shape[1]
    h = x[0]
    fh = ffn_w_out.shape[1]
    conv_w_cols = gdn_conv_w.shape[2]
    vw = GDN_V_HEADS * GDN_HEAD_DIM
    main_w = conv_w_cols + vw

    def channel_and_ple(h, i):
        w_in = ffn_w_in[i].astype(BF16)
        return ffn_ple(h, ffn_pre_norm[i], w_in[:, :fh], w_in[:, fh:], ffn_w_out[i].astype(BF16),
                       ffn_post_norm[i], p[i, 0], ple_w_in[i].astype(BF16), ple_w_gate[i].astype(BF16))

    for i in range(n_a):
        w_in = gdn_w_in[i]
        w_beta = w_in[:, main_w:main_w + GDN_V_HEADS]
        w_a = w_in[:, main_w + GDN_V_HEADS:]
        w_small = _pad_cols(jnp.concatenate([w_beta, w_a, w_a], axis=1), LANES).astype(BF16)
        proj = norm_matmul(h, mix_pre_norm[i], w_in[:, :main_w].astype(BF16))
        ba = norm_matmul(h, mix_pre_norm[i], w_small)
        pad_vec = lambda v: jnp.pad(v, (GDN_V_HEADS, LANES - 2 * GDN_V_HEADS))
        alog_vec = (pad_vec(gdn_a_log[i]) + jnp.pad(gdn_a_log[i], (2 * GDN_V_HEADS, LANES - 3 * GDN_V_HEADS)))
        dtb_vec = (pad_vec(gdn_dt_bias[i]) + jnp.pad(gdn_dt_bias[i], (2 * GDN_V_HEADS, LANES - 3 * GDN_V_HEADS)))
        slab = gdn_gates(ba, alog_vec.reshape(1, LANES), dtb_vec.reshape(1, LANES))
        qkv_hm = gdn_conv(proj, gdn_conv_w[i])
        kt_hm = jnp.swapaxes(qkv_hm[GDN_QK_HEADS:2 * GDN_QK_HEADS], 1, 2)
        o = gdn_scan(qkv_hm, kt_hm, slab, slab.T)
        onorm_tiled = jnp.tile(gdn_o_norm[i], GDN_V_HEADS).reshape(1, vw)
        h = gdn_out(o, proj, onorm_tiled, gdn_w_out[i].astype(BF16), mix_post_norm[i], h)
        h = channel_and_ple(h, i)

    g = NSA_GROUPS
    dh = NSA_HEAD_DIM
    kv = norm_matmul(h, kv_norm, kv_w.astype(BF16), tn=768)
    kv6 = jnp.transpose(kv.reshape(t, 6, g, dh), (1, 2, 0, 3))
    nc = t // CMP_STRIDE
    nsel = t // SEL_BLOCK
    x2 = kv6[0:2].reshape(2, g, nc, CMP_STRIDE * dh)
    pos2 = cmp_pos.reshape(2, 2, CMP_STRIDE * dh)
    cmp_out = compress(x2, pos2, cmp_w1.astype(BF16), cmp_w2.astype(BF16))
    v_cmp_t, kbd = _block_diag_kv(cmp_out[1], cmp_out[0])
    imp_mat_t = _importance_matrix(nc, nsel).T
    kt = FLASH_KT
    tiles = lambda a: a.reshape(g, t // kt, kt, dh)
    rep_k = lambda a: jnp.tile(tiles(a), (1, 1, 1, NSA_REP)).astype(BF16)
    rep_t = lambda a: jnp.tile(jnp.swapaxes(tiles(a), 2, 3), (1, 1, NSA_REP, 1)).astype(BF16)
    k_slc, v_slc_t = rep_k(kv6[2]), rep_t(kv6[3])
    k_win, v_win_t = rep_k(kv6[4]), rep_t(kv6[5])
    expand = _gate_expand()

    for i in range(n_a, depth):
        j = i - n_a
        qw = g * NSA_GW
        w_qg = nsa_w_qg[j]
        q = norm_matmul(h, mix_pre_norm[i], w_qg[:, :qw].astype(BF16))
        gate_logits = norm_matmul(h, mix_pre_norm[i], _pad_cols(w_qg[:, qw:], LANES).astype(BF16))
        q_t = q.T
        o_c, sel_t = cmp_topk(q_t, kbd, v_cmp_t, imp_mat_t)
        o_s = flash_branch(q_t, k_slc, v_slc_t, sel_t, tq=1024, kt=kt)
        o_w = flash_branch(q_t, k_win, v_win_t, None, tq=512, kt=kt)
        h = nsa_out(o_c, o_s, o_w, gate_logits, expand, nsa_w_o[j].astype(BF16), mix_post_norm[i], h)
        h = channel_and_ple(h, i)
    return h[None]
```

```python
import functools

import numpy as np
import jax
import jax.numpy as jnp
from jax import lax
from jax.experimental import pallas as pl
from jax.experimental.pallas import tpu as pltpu

F32 = jnp.float32
BF16 = jnp.bfloat16

NORM_EPS = 1e-6
L2_EPS = 1e-6
GDN_QK_HEADS = 8
GDN_V_HEADS = 16
GDN_HEAD_DIM = 128
GDN_CONV = 4
GDN_CHUNK = 64
GDN_GROUP = 256
GDN_SCAN_HEADS = 4
GDN_CONV_HEADS = 4
NSA_GROUPS = 4
NSA_REP = 4
NSA_HEAD_DIM = 64
NSA_GW = NSA_REP * NSA_HEAD_DIM
CMP_BLOCK = 32
CMP_STRIDE = 16
SEL_BLOCK = 64
SEL_TOPK = 16
WINDOW = 512
FORCED_SCORE = 1e4
LANES = 128
NEG_INIT = -(2.0 ** 100)
LOG2_E = 1.4426950408889634
FLASH_KT = 256
FLASH_SUB = 256
FLASH_SUM_ROWS = 16
CMP_ROWS = 256
CMP_TQ = 256
CMP_RANGES = 4

VMEM_LIMIT = 56 * 1024 * 1024


def _cparams(sem):
    return pltpu.CompilerParams(dimension_semantics=sem, vmem_limit_bytes=VMEM_LIMIT)


def _rms(x, gain):
    return x * lax.rsqrt(jnp.mean(x * x, axis=-1, keepdims=True) + NORM_EPS) * gain


def _mm(a, b):
    return jnp.dot(a.astype(BF16), b.astype(BF16), preferred_element_type=F32)


def _sigmoid(x):
    return 1.0 / (1.0 + jnp.exp(-x))


def _div_pow2(x, d):
    shift = d.bit_length() - 1
    assert d == 1 << shift
    return jnp.right_shift(x, shift)


def _split3(x):
    a = x.astype(BF16)
    r = x - a.astype(F32)
    b = r.astype(BF16)
    c = (r - b.astype(F32)).astype(BF16)
    return a, b, c


def _norm_matmul_kernel(x_ref, g_ref, w_ref, o_ref, xn_ref):
    @pl.when(pl.program_id(1) == 0)
    def _():
        xn_ref[...] = _rms(x_ref[...], g_ref[...]).astype(BF16)

    o_ref[...] = jnp.dot(xn_ref[...], w_ref[...], preferred_element_type=F32)


def norm_matmul(h, gain, w, tm=512, tn=1024):
    t, d = h.shape
    n = w.shape[1]
    tn = min(tn, n)
    assert t % tm == 0 and n % tn == 0
    return pl.pallas_call(
        _norm_matmul_kernel,
        grid=(t // tm, n // tn),
        in_specs=[
            pl.BlockSpec((tm, d), lambda i, j: (i, 0)),
            pl.BlockSpec((1, d), lambda i, j: (0, 0)),
            pl.BlockSpec((d, tn), lambda i, j: (0, j)),
        ],
        out_specs=pl.BlockSpec((tm, tn), lambda i, j: (i, j)),
        out_shape=jax.ShapeDtypeStruct((t, n), F32),
        scratch_shapes=[pltpu.VMEM((tm, d), BF16)],
        compiler_params=_cparams(("parallel", "arbitrary")),
        name="norm_matmul",
    )(h, gain.reshape(1, d), w)


def _ffn_ple_kernel(h_ref, gpre_ref, wg_ref, wu_ref, wo_ref, gpost_ref, p_ref, wple_ref, wgt_ref,
                    o_ref, xn_ref, acc_ref, *, nf):
    f = pl.program_id(1)

    @pl.when(f == 0)
    def _():
        xn_ref[...] = _rms(h_ref[...], gpre_ref[...]).astype(BF16)
        acc_ref[...] = jnp.zeros_like(acc_ref)

    xn = xn_ref[...]
    gate = jnp.dot(xn, wg_ref[...], preferred_element_type=F32)
    up = jnp.dot(xn, wu_ref[...], preferred_element_type=F32)
    act = gate * _sigmoid(gate) * up
    acc_ref[...] += jnp.dot(act.astype(BF16), wo_ref[...], preferred_element_type=F32)

    @pl.when(f == nf - 1)
    def _():
        h2 = h_ref[...] + _rms(acc_ref[...], gpost_ref[...])
        emb = jnp.dot(p_ref[...].astype(BF16), wple_ref[...], preferred_element_type=F32)
        gt = _sigmoid(jnp.dot(h2.astype(BF16), wgt_ref[...], preferred_element_type=F32))
        o_ref[...] = h2 + emb * gt


def ffn_ple(h, gpre, w_gate, w_up, w_out, gpost, p, w_ple, w_plegate, tm=512, tf=1408):
    t, d = h.shape
    fh = w_gate.shape[1]
    pd = p.shape[1]
    assert t % tm == 0 and fh % tf == 0
    nf = fh // tf
    return pl.pallas_call(
        functools.partial(_ffn_ple_kernel, nf=nf),
        grid=(t // tm, nf),
        in_specs=[
            pl.BlockSpec((tm, d), lambda i, f: (i, 0)),
            pl.BlockSpec((1, d), lambda i, f: (0, 0)),
            pl.BlockSpec((d, tf), lambda i, f: (0, f)),
            pl.BlockSpec((d, tf), lambda i, f: (0, f)),
            pl.BlockSpec((tf, d), lambda i, f: (f, 0)),
            pl.BlockSpec((1, d), lambda i, f: (0, 0)),
            pl.BlockSpec((tm, pd), lambda i, f: (i, 0)),
            pl.BlockSpec((pd, d), lambda i, f: (0, 0)),
            pl.BlockSpec((d, d), lambda i, f: (0, 0)),
        ],
        out_specs=pl.BlockSpec((tm, d), lambda i, f: (i, 0)),
        out_shape=jax.ShapeDtypeStruct((t, d), F32),
        scratch_shapes=[pltpu.VMEM((tm, d), BF16), pltpu.VMEM((tm, d), F32)],
        compiler_params=_cparams(("parallel", "arbitrary")),
        name="ffn_ple",
    )(h, gpre.reshape(1, d), w_gate, w_up, w_out, gpost.reshape(1, d), p, w_ple, w_plegate)


def _gdn_conv_kernel(x_ref, halo_ref, w_ref, o_ref, *, tm):
    c = pl.program_id(0)
    i = pl.program_id(1)
    x = x_ref[...]
    halo = jnp.where(i > 0, halo_ref[...], 0.0)
    ext = jnp.concatenate([halo, x], axis=0)
    w = w_ref[...]
    y = x * w[GDN_CONV - 1:GDN_CONV, :]
    for k in range(1, GDN_CONV):
        shifted = pltpu.roll(ext, k, axis=0)[8:8 + tm]
        y = y + shifted * w[GDN_CONV - 1 - k:GDN_CONV - k, :]
    y = y * _sigmoid(y)
    for hd in range(GDN_CONV_HEADS):
        head = c * GDN_CONV_HEADS + hd
        seg = y[:, hd * LANES:(hd + 1) * LANES]
        normed = seg * lax.rsqrt(jnp.sum(seg * seg, axis=-1, keepdims=True) + L2_EPS)
        q_scale = jnp.where(head < GDN_QK_HEADS, GDN_HEAD_DIM ** -0.5, 1.0)
        o_ref[hd] = jnp.where(head < 2 * GDN_QK_HEADS, normed * q_scale, seg)


def gdn_conv(proj, conv_w, tm=1024):
    t = proj.shape[0]
    n_tiles = conv_w.shape[1] // LANES
    cw = GDN_CONV_HEADS
    assert t % tm == 0 and n_tiles % cw == 0
    return pl.pallas_call(
        functools.partial(_gdn_conv_kernel, tm=tm),
        grid=(n_tiles // cw, t // tm),
        in_specs=[
            pl.BlockSpec((tm, cw * LANES), lambda c, i: (i, c)),
            pl.BlockSpec((8, cw * LANES), lambda c, i: (jnp.maximum(i * (tm // 8) - 1, 0), c)),
            pl.BlockSpec((GDN_CONV, cw * LANES), lambda c, i: (0, c)),
        ],
        out_specs=pl.BlockSpec((cw, tm, LANES), lambda c, i: (c, i, 0)),
        out_shape=jax.ShapeDtypeStruct((n_tiles, t, LANES), F32),
        compiler_params=_cparams(("parallel", "parallel")),
        name="gdn_conv",
    )(proj, proj, conv_w)


def _gdn_gate_kernel(x_ref, alog_ref, dtb_ref, lc_ref, lf_ref, o_ref):
    x = x_ref[...]
    lane = lax.broadcasted_iota(jnp.int32, x.shape, 1)
    beta = _sigmoid(x)
    z = x + dtb_ref[...]
    softplus = jnp.maximum(z, 0.0) + jnp.log(1.0 + jnp.exp(-jnp.abs(z)))
    g = -jnp.exp(alog_ref[...]) * softplus
    gcum = jnp.zeros_like(x)
    gtot = jnp.zeros_like(x)
    for piece in _split3(g):
        gcum = gcum + jnp.dot(lc_ref[...], piece, preferred_element_type=F32)
        gtot = gtot + jnp.dot(lf_ref[...], piece, preferred_element_type=F32)
    o_ref[...] = jnp.where(lane < GDN_V_HEADS, beta, jnp.where(lane < 2 * GDN_V_HEADS, gcum, gtot))


def gdn_gates(ba, alog_vec, dtb_vec):
    t = ba.shape[0]
    tm = GDN_GROUP
    r = np.arange(tm)
    same = (r[:, None] // GDN_CHUNK) == (r[None, :] // GDN_CHUNK)
    lc = jnp.asarray(same & (r[None, :] <= r[:, None]), BF16)
    lf = jnp.asarray(same, BF16)
    return pl.pallas_call(
        _gdn_gate_kernel,
        grid=(t // tm,),
        in_specs=[
            pl.BlockSpec((tm, LANES), lambda i: (i, 0)),
            pl.BlockSpec((1, LANES), lambda i: (0, 0)),
            pl.BlockSpec((1, LANES), lambda i: (0, 0)),
            pl.BlockSpec((tm, tm), lambda i: (0, 0)),
            pl.BlockSpec((tm, tm), lambda i: (0, 0)),
        ],
        out_specs=pl.BlockSpec((tm, LANES), lambda i: (i, 0)),
        out_shape=jax.ShapeDtypeStruct((t, LANES), F32),
        compiler_params=_cparams(("parallel",)),
        name="gdn_gates",
    )(ba, alog_vec, dtb_vec, lc, lf)


def _gdn_scan_kernel(q_ref, k_ref, kt_ref, v_ref, slab_ref, slabt_ref, o_ref, s_ref):
    n = pl.program_id(1)

    @pl.when(n == 0)
    def _():
        s_ref[...] = jnp.zeros_like(s_ref)

    L = GDN_GROUP
    C = GDN_CHUNK
    D = GDN_HEAD_DIM
    slab = slab_ref[...]
    slabt = slabt_ref[...]
    row = lax.broadcasted_iota(jnp.int32, (L, L), 0)
    col = lax.broadcasted_iota(jnp.int32, (L, L), 1)
    same = _div_pow2(row, C) == _div_pow2(col, C)
    causal = same & (col <= row)
    strict = same & (col < row)
    eye = (row == col).astype(F32)
    lane = lax.broadcasted_iota(jnp.int32, (L, LANES), 1)
    sub = lax.broadcasted_iota(jnp.int32, (LANES, L), 0)

    def column(idx):
        return jnp.sum(jnp.where(lane == idx, slab, 0.0), axis=1, keepdims=True)

    def rowvec(idx):
        return jnp.sum(jnp.where(sub == idx, slabt, 0.0), axis=0, keepdims=True)

    colk = lax.broadcasted_iota(jnp.int32, (D, L), 1)
    slots = range(2 * GDN_SCAN_HEADS)

    q = [q_ref[hq] for hq in range(GDN_SCAN_HEADS)]
    k = [k_ref[hq] for hq in range(GDN_SCAN_HEADS)]
    kt = [kt_ref[hq] for hq in range(GDN_SCAN_HEADS)]
    kk = [_mm(k[hq], kt[hq]) for hq in range(GDN_SCAN_HEADS)]
    qk = [_mm(q[hq], kt[hq]) for hq in range(GDN_SCAN_HEADS)]
    beta_c, gc_c, gt_c, decay, bp, inv, kdt = [], [], [], [], [], [], []
    for slot in slots:
        hq = slot // 2
        hv = 2 * (pl.program_id(0) * GDN_SCAN_HEADS + hq) + slot % 2
        beta_c.append(column(hv))
        gc_c.append(column(GDN_V_HEADS + hv))
        gt_c.append(column(2 * GDN_V_HEADS + hv))
        gc_r = rowvec(GDN_V_HEADS + hv)
        gt_r = rowvec(2 * GDN_V_HEADS + hv)
        decay.append(jnp.where(causal, jnp.exp(jnp.where(causal, gc_c[slot] - gc_r, 0.0)), 0.0))
        bp.append(jnp.where(strict, -(kk[hq] * beta_c[slot]) * decay[slot], 0.0))
        inv.append(eye + bp[slot])
        kdt.append(kt[hq] * jnp.exp(gt_r - gc_r))
    for _ in range(5):
        bp = [_mm(bp[slot], bp[slot]) for slot in slots]
        inv = [inv[slot] + _mm(inv[slot], bp[slot]) for slot in slots]
    u, w, qkm, q_dec = [], [], [], []
    for slot in slots:
        hq = slot // 2
        egc = jnp.exp(gc_c[slot])
        rhs = jnp.concatenate([v_ref[slot] * beta_c[slot], k[hq] * (beta_c[slot] * egc)], axis=1)
        sol = _mm(inv[slot], rhs)
        u.append(sol[:, :D])
        w.append(sol[:, D:])
        qkm.append(jnp.where(causal, qk[hq] * decay[slot], 0.0))
        q_dec.append(q[hq] * egc)
    state = [s_ref[slot] for slot in slots]
    v_done = [[] for _ in slots]
    for c in range(L // C):
        lo, hi = c * C, (c + 1) * C
        r = [_mm(jnp.concatenate([w[slot][lo:hi], q_dec[slot][lo:hi]], axis=0), state[slot]) for slot in slots]
        for slot in slots:
            v_done[slot].append(u[slot][lo:hi] - r[slot][:C])
            v_all = jnp.concatenate(v_done[slot] + [jnp.zeros((L - hi, D), F32)] * (hi < L), axis=0)
            o_ref[lo:hi, slot * D:(slot + 1) * D] = r[slot][C:] + _mm(qkm[slot][lo:hi, :], v_all)
            kdt_c = jnp.where((colk >= lo) & (colk < hi), kdt[slot], 0.0)
            state[slot] = state[slot] * jnp.exp(gt_c[slot][lo:lo + 1, :]) + _mm(kdt_c, v_all)
    for slot in slots:
        s_ref[slot] = state[slot]


def gdn_scan(qkv_hm, kt_hm, slab, slabt):
    t = qkv_hm.shape[1]
    L = GDN_GROUP
    D = GDN_HEAD_DIM
    hq = GDN_SCAN_HEADS
    assert t % L == 0 and GDN_QK_HEADS % hq == 0
    q_blocks = GDN_QK_HEADS // hq
    return pl.pallas_call(
        _gdn_scan_kernel,
        grid=(q_blocks, t // L),
        in_specs=[
            pl.BlockSpec((hq, L, D), lambda j, n: (j, n, 0)),
            pl.BlockSpec((hq, L, D), lambda j, n: (q_blocks + j, n, 0)),
            pl.BlockSpec((hq, D, L), lambda j, n: (j, 0, n)),
            pl.BlockSpec((2 * hq, L, D), lambda j, n: (q_blocks + j, n, 0)),
            pl.BlockSpec((L, LANES), lambda j, n: (n, 0)),
            pl.BlockSpec((LANES, L), lambda j, n: (0, n)),
        ],
        out_specs=pl.BlockSpec((L, 2 * hq * D), lambda j, n: (n, j)),
        out_shape=jax.ShapeDtypeStruct((t, GDN_V_HEADS * D), F32),
        scratch_shapes=[pltpu.VMEM((2 * hq, D, D), F32)],
        compiler_params=_cparams(("parallel", "arbitrary")),
        name="gdn_scan",
    )(qkv_hm, qkv_hm, kt_hm, qkv_hm, slab, slabt)


def _gdn_out_kernel(o_ref, z_ref, onorm_ref, w_ref, gpost_ref, h_ref, out_ref):
    o = o_ref[...]
    z = z_ref[...]
    parts = []
    for hd in range(GDN_V_HEADS):
        seg = o[:, hd * GDN_HEAD_DIM:(hd + 1) * GDN_HEAD_DIM]
        parts.append(seg * lax.rsqrt(jnp.mean(seg * seg, axis=-1, keepdims=True) + NORM_EPS))
    gated = jnp.concatenate(parts, axis=1) * onorm_ref[...] * (z * _sigmoid(z))
    mix = jnp.dot(gated.astype(BF16), w_ref[...], preferred_element_type=F32)
    out_ref[...] = h_ref[...] + _rms(mix, gpost_ref[...])


def gdn_out(o, proj, onorm_tiled, w_out, gpost, h, tm=256):
    t, vw = o.shape
    d = h.shape[1]
    z_blk = (proj.shape[1] - vw) // vw
    assert proj.shape[1] % vw == 0 and t % tm == 0
    return pl.pallas_call(
        _gdn_out_kernel,
        grid=(t // tm,),
        in_specs=[
            pl.BlockSpec((tm, vw), lambda i: (i, 0)),
            pl.BlockSpec((tm, vw), lambda i: (i, z_blk)),
            pl.BlockSpec((1, vw), lambda i: (0, 0)),
            pl.BlockSpec((vw, d), lambda i: (0, 0)),
            pl.BlockSpec((1, d), lambda i: (0, 0)),
            pl.BlockSpec((tm, d), lambda i: (i, 0)),
        ],
        out_specs=pl.BlockSpec((tm, d), lambda i: (i, 0)),
        out_shape=jax.ShapeDtypeStruct((t, d), F32),
        compiler_params=_cparams(("parallel",)),
        name="gdn_out",
    )(o, proj, onorm_tiled, w_out, gpost.reshape(1, d), h)


def _compress_kernel(x_ref, pos_ref, w1_ref, w2_ref, o_ref):
    x = x_ref[...]
    pos = pos_ref[...]
    nc, half = x.shape
    w1 = w1_ref[...]
    first = _mm(x + pos[0:1, :], w1[:half])
    second = _mm(x + pos[1:2, :], w1[half:])
    hid = first + pltpu.roll(second, nc - 1, axis=0)
    hid = hid * _sigmoid(hid)
    out = jnp.dot(hid.astype(BF16), w2_ref[...], preferred_element_type=F32)
    rowi = lax.broadcasted_iota(jnp.int32, out.shape, 0)
    o_ref[...] = jnp.where(rowi < nc - 1, out, 0.0)


def compress(x2, pos2, w1, w2):
    _, g, nc, wdt = x2.shape
    hid = w1.shape[2]
    dh = w2.shape[2]
    return pl.pallas_call(
        _compress_kernel,
        grid=(2, g),
        in_specs=[
            pl.BlockSpec((None, None, nc, wdt), lambda b, gi: (b, gi, 0, 0)),
            pl.BlockSpec((None, 2, wdt), lambda b, gi: (b, 0, 0)),
            pl.BlockSpec((None, 2 * wdt, hid), lambda b, gi: (b, 0, 0)),
            pl.BlockSpec((None, hid, dh), lambda b, gi: (b, 0, 0)),
        ],
        out_specs=pl.BlockSpec((None, None, nc, dh), lambda b, gi: (b, gi, 0, 0)),
        out_shape=jax.ShapeDtypeStruct((2, g, nc, dh), F32),
        compiler_params=_cparams(("parallel", "parallel")),
        name="nsa_compress",
    )(x2, pos2, w1, w2)


def _cmp_topk_kernel(q_ref, kbd_ref, vt_ref, m_ref, oc_ref, sel_ref, s_ref, p_ref, psum_ref, *,
                     tq, nc, nsel, topk, q0):
    i = q0 + pl.program_id(1)
    dh = NSA_HEAD_DIM
    rows_per = min(nc, CMP_ROWS)
    q = (q_ref[...] * ((dh ** -0.5) * LOG2_E)).astype(BF16)
    s_ref[...] = jnp.dot(kbd_ref[...], q, preferred_element_type=F32)
    for ch in range(tq // LANES):
        lanes = slice(ch * LANES, (ch + 1) * LANES)
        tpos = i * tq + ch * LANES + lax.broadcasted_iota(jnp.int32, (rows_per, LANES), 1)
        cblk0 = lax.broadcasted_iota(jnp.int32, (rows_per, LANES), 0)
        masks = [(CMP_STRIDE * (cblk0 + c * rows_per) + CMP_BLOCK - 1) <= tpos for c in range(nc // rows_per)]
        for r in range(NSA_REP):
            pieces = [slice(r * nc + c * rows_per, r * nc + (c + 1) * rows_per) for c in range(nc // rows_per)]
            m = jnp.full((1, LANES), -jnp.inf, F32)
            for rows, mask in zip(pieces, masks):
                m = jnp.maximum(m, jnp.max(jnp.where(mask, s_ref[rows, lanes], -jnp.inf), axis=0, keepdims=True))
            m = jnp.where(m > -jnp.inf, m, 0.0)
            total = jnp.zeros((1, LANES), F32)
            for rows, mask in zip(pieces, masks):
                e = jnp.exp2(jnp.where(mask, s_ref[rows, lanes], -jnp.inf) - m)
                s_ref[rows, lanes] = e
                total = total + jnp.sum(e, axis=0, keepdims=True)
            inv = 1.0 / jnp.maximum(total, 1e-30)
            for c, rows in enumerate(pieces):
                p = s_ref[rows, lanes] * inv
                p_ref[rows, lanes] = p.astype(BF16)
                prow = slice(c * rows_per, (c + 1) * rows_per)
                psum_ref[prow, lanes] = p if r == 0 else psum_ref[prow, lanes] + p
    oc_t = jnp.dot(vt_ref[...], p_ref[...], preferred_element_type=F32)
    imp = jnp.zeros((nsel, tq), F32)
    for piece in _split3(psum_ref[...]):
        imp = imp + jnp.dot(m_ref[...], piece, preferred_element_type=F32)
    for ch in range(tq // LANES):
        lanes = slice(ch * LANES, (ch + 1) * LANES)
        oc_ref[lanes, :] = oc_t[:, lanes].T
        t1 = i * tq + ch * LANES + lax.broadcasted_iota(jnp.int32, (nsel, LANES), 1)
        blk = lax.broadcasted_iota(jnp.int32, (nsel, LANES), 0)
        cur = _div_pow2(t1, SEL_BLOCK)
        forced = (blk == 0) | (blk == cur) | (blk == cur - 1)
        valid = blk * SEL_BLOCK <= t1
        score = jnp.where(valid, jnp.where(forced, FORCED_SCORE, imp[:, lanes]), -jnp.inf)
        blkf = blk.astype(F32)
        work = score
        for _ in range(topk):
            mx = jnp.max(work, axis=0, keepdims=True)
            first = jnp.min(jnp.where(work == mx, blkf, float(nsel)), axis=0, keepdims=True)
            work = jnp.where(blkf == first, -jnp.inf, work)
        sel_ref[:nsel, lanes] = jnp.where((score > -jnp.inf) & (work == -jnp.inf), 1.0, 0.0).astype(sel_ref.dtype)
        if sel_ref.shape[0] > nsel:
            sel_ref[nsel:, lanes] = jnp.zeros((sel_ref.shape[0] - nsel, LANES), sel_ref.dtype)


def cmp_topk(q_t, kbd, v_t, imp_mat_t, nsel_all, topk, q0, nq, tq):
    qw = q_t.shape[0]
    g = kbd.shape[0]
    nc = kbd.shape[1] // NSA_REP
    nsel = imp_mat_t.shape[0]
    assert nc % min(nc, CMP_ROWS) == 0
    return pl.pallas_call(
        functools.partial(_cmp_topk_kernel, tq=tq, nc=nc, nsel=nsel, topk=topk, q0=q0),
        grid=(g, nq),
        in_specs=[
            pl.BlockSpec((NSA_GW, tq), lambda gi, i: (gi, q0 + i)),
            pl.BlockSpec((None, NSA_REP * nc, NSA_GW), lambda gi, i: (gi, 0, 0)),
            pl.BlockSpec((None, NSA_GW, NSA_REP * nc), lambda gi, i: (gi, 0, 0)),
            pl.BlockSpec((nsel, nc), lambda gi, i: (0, 0)),
        ],
        out_specs=[
            pl.BlockSpec((tq, NSA_GW), lambda gi, i: (i, gi)),
            pl.BlockSpec((None, nsel_all, tq), lambda gi, i: (gi, 0, i)),
        ],
        out_shape=[
            jax.ShapeDtypeStruct((nq * tq, qw), F32),
            jax.ShapeDtypeStruct((g, nsel_all, nq * tq), F32),
        ],
        scratch_shapes=[
            pltpu.VMEM((NSA_REP * nc, tq), F32),
            pltpu.VMEM((NSA_REP * nc, tq), BF16),
            pltpu.VMEM((nc, tq), F32),
        ],
        compiler_params=_cparams(("parallel", "parallel")),
        name="nsa_cmp_topk",
    )(q_t, kbd, v_t, imp_mat_t)


def _flash_kernel(*refs, tq, kt, selected):
    if selected:
        q_ref, k_ref, v_ref, sel_ref, o_ref = refs[:5]
    else:
        q_ref, k_ref, v_ref, o_ref = refs[:4]
    m_ref, l_ref, alpha_ref, acc_ref, s_ref, p_ref, bias_ref, kbd_ref, vbd_ref, qs_ref = refs[-10:]
    qi = pl.program_id(1)
    dh = NSA_HEAD_DIM
    n_sub = tq // FLASH_SUB
    n_blk = kt // SEL_BLOCK

    m_ref[...] = jnp.full_like(m_ref, NEG_INIT)
    l_ref[...] = jnp.zeros_like(l_ref)
    acc_ref[...] = jnp.zeros_like(acc_ref)
    qs_ref[...] = (q_ref[...] * ((dh ** -0.5) * LOG2_E)).astype(BF16)
    one_row = lax.broadcasted_iota(jnp.int32, (FLASH_SUM_ROWS, NSA_REP * kt), 0)
    one_head = _div_pow2(lax.broadcasted_iota(jnp.int32, (FLASH_SUM_ROWS, NSA_REP * kt), 1), kt)
    vbd_ref[NSA_GW:, :] = (one_row == one_head).astype(BF16)

    def key_tile(ki, positional):
        k4 = k_ref[ki]
        kseg = _div_pow2(lax.broadcasted_iota(jnp.int32, k4.shape, 1), dh)
        v4 = v_ref[ki]
        vblk = _div_pow2(lax.broadcasted_iota(jnp.int32, v4.shape, 0), dh)
        for r in range(NSA_REP):
            kbd_ref[r * kt:(r + 1) * kt, :] = jnp.where(kseg == r, k4, jnp.zeros_like(k4))
            vbd_ref[:NSA_GW, r * kt:(r + 1) * kt] = jnp.where(vblk == r, v4, jnp.zeros_like(v4))

        def scores(sub):
            c0 = sub * FLASH_SUB
            cols = slice(c0, c0 + FLASH_SUB)
            s_ref[:, cols] = jnp.dot(kbd_ref[...], qs_ref[:, cols],
                                     preferred_element_type=F32)
            for jb in range(n_blk):
                rows = slice(jb * SEL_BLOCK, (jb + 1) * SEL_BLOCK)
                if selected:
                    picked = sel_ref[pl.ds(ki * n_blk + jb, 1), cols] > 0.5
                if positional:
                    tpos = qi * tq + c0 + lax.broadcasted_iota(jnp.int32, (SEL_BLOCK, FLASH_SUB), 1)
                    kpos = ki * kt + jb * SEL_BLOCK + lax.broadcasted_iota(jnp.int32, (SEL_BLOCK, FLASH_SUB), 0)
                    if selected:
                        allowed = picked & (kpos <= tpos)
                    else:
                        allowed = (kpos <= tpos) & (kpos > tpos - WINDOW)
                else:
                    allowed = jnp.broadcast_to(picked, (SEL_BLOCK, FLASH_SUB))
                bias_ref[rows, cols] = jnp.where(allowed, 0.0, -jnp.inf)

        scores(0)
        for sub in range(n_sub):
            c0 = sub * FLASH_SUB
            cols = slice(c0, c0 + FLASH_SUB)
            if sub + 1 < n_sub:
                scores(sub + 1)
            for ch in range(FLASH_SUB // LANES):
                lanes = slice(c0 + ch * LANES, c0 + (ch + 1) * LANES)
                bias = bias_ref[:, lanes]
                for r in range(NSA_REP):
                    x = s_ref[r * kt:(r + 1) * kt, lanes] + bias
                    m_prev = m_ref[r:r + 1, lanes]
                    m_new = jnp.maximum(m_prev, jnp.max(x, axis=0, keepdims=True))
                    m_ref[r:r + 1, lanes] = m_new
                    alpha_ref[r:r + 1, lanes] = jnp.exp2(m_prev - m_new)
                    p_ref[r * kt:(r + 1) * kt, lanes] = jnp.exp2(x - m_new).astype(BF16)
            pv = jnp.dot(vbd_ref[...], p_ref[:, cols], preferred_element_type=F32)
            for r in range(NSA_REP):
                hd = slice(r * dh, (r + 1) * dh)
                alpha = alpha_ref[r:r + 1, cols]
                acc_ref[hd, cols] = acc_ref[hd, cols] * alpha + pv[hd]
                l_ref[r:r + 1, cols] = l_ref[r:r + 1, cols] * alpha + pv[NSA_GW + r:NSA_GW + r + 1]

    def walk(lo, hi, positional):
        def step(ki, carry):
            key_tile(ki, positional)
            return carry
        lax.fori_loop(lo, hi, step, 0)

    diag = qi * (tq // kt)
    if selected:
        walk(0, diag, False)
        walk(diag, diag + tq // kt, True)
    else:
        first = jnp.maximum(qi * tq - (WINDOW - 1), 0) // kt
        walk(first, diag + tq // kt, True)

    for ch in range(tq // LANES):
        lanes = slice(ch * LANES, (ch + 1) * LANES)
        out_t = jnp.concatenate(
            [acc_ref[r * dh:(r + 1) * dh, lanes] / l_ref[r:r + 1, lanes] for r in range(NSA_REP)], axis=0)
        o_ref[lanes, :] = out_t.T


def flash_branch(q_t, k4, vt4, sel_t, tq, kt):
    qw, t = q_t.shape
    g, nk = k4.shape[:2]
    selected = sel_t is not None
    assert t % tq == 0 and tq % kt == 0 and kt % SEL_BLOCK == 0 and tq % FLASH_SUB == 0
    in_specs = [
        pl.BlockSpec((NSA_GW, tq), lambda gi, i: (gi, i)),
        pl.BlockSpec((None, nk, kt, NSA_GW), lambda gi, i: (gi, 0, 0, 0)),
        pl.BlockSpec((None, nk, NSA_GW, kt), lambda gi, i: (gi, 0, 0, 0)),
    ]
    args = [q_t, k4, vt4]
    if selected:
        nsel = sel_t.shape[1]
        in_specs.append(pl.BlockSpec((None, nsel, tq), lambda gi, i: (gi, 0, i)))
        args.append(sel_t)
    return pl.pallas_call(
        functools.partial(_flash_kernel, tq=tq, kt=kt, selected=selected),
        grid=(g, t // tq),
        in_specs=in_specs,
        out_specs=pl.BlockSpec((tq, NSA_GW), lambda gi, i: (i, gi)),
        scratch_shapes=[
            pltpu.VMEM((8, tq), F32),
            pltpu.VMEM((8, tq), F32),
            pltpu.VMEM((8, tq), F32),
            pltpu.VMEM((NSA_GW, tq), F32),
            pltpu.VMEM((NSA_REP * kt, tq), F32),
            pltpu.VMEM((NSA_REP * kt, tq), BF16),
            pltpu.VMEM((kt, tq), F32),
            pltpu.VMEM((NSA_REP * kt, NSA_GW), BF16),
            pltpu.VMEM((NSA_GW + FLASH_SUM_ROWS, NSA_REP * kt), BF16),
            pltpu.VMEM((NSA_GW, tq), BF16),
        ],
        out_shape=jax.ShapeDtypeStruct((t, qw), F32),
        compiler_params=_cparams(("parallel", "arbitrary")),
        name="nsa_selected" if selected else "nsa_window",
    )(*args)


def _nsa_out_kernel(oc_ref, os_ref, ow_ref, gl_ref, eg_ref, w_ref, gpost_ref, h_ref, out_ref):
    pieces = _split3(_sigmoid(gl_ref[...]))
    mixed = jnp.zeros(oc_ref.shape, F32)
    for b, br_ref in enumerate((oc_ref, os_ref, ow_ref)):
        gfull = jnp.zeros(oc_ref.shape, F32)
        for piece in pieces:
            gfull = gfull + jnp.dot(piece, eg_ref[b], preferred_element_type=F32)
        mixed = mixed + gfull * br_ref[...]
    mix = jnp.dot(mixed.astype(BF16), w_ref[...], preferred_element_type=F32)
    out_ref[...] = h_ref[...] + _rms(mix, gpost_ref[...])


def nsa_out(oc, osel, ow, gate_logits, expand, w_o, gpost, h, tm=256):
    t, qw = oc.shape
    d = h.shape[1]
    assert t % tm == 0
    row = lambda w: pl.BlockSpec((tm, w), lambda i: (i, 0))
    return pl.pallas_call(
        _nsa_out_kernel,
        grid=(t // tm,),
        in_specs=[
            row(qw), row(qw), row(qw), row(LANES),
            pl.BlockSpec((3, LANES, qw), lambda i: (0, 0, 0)),
            pl.BlockSpec((qw, d), lambda i: (0, 0)),
            pl.BlockSpec((1, d), lambda i: (0, 0)),
            row(d),
        ],
        out_specs=row(d),
        out_shape=jax.ShapeDtypeStruct((t, d), F32),
        compiler_params=_cparams(("parallel",)),
        name="nsa_out",
    )(oc, osel, ow, gate_logits, expand, w_o, gpost.reshape(1, d), h)


def _pad_cols(w, n):
    return jnp.pad(w, ((0, 0), (0, n - w.shape[1])))


def _importance_matrix(nc, nsel):
    r = SEL_BLOCK // CMP_STRIDE
    c = CMP_BLOCK // CMP_STRIDE
    mat = np.zeros((nc, nsel), np.float32)
    for kblk in range(nsel):
        for m in range(r):
            for n in range(c):
                j = r * kblk + m - n
                if 0 <= j < nc - 1:
                    mat[j, kblk] += 1.0
    return jnp.asarray(mat, BF16)


def _gate_expand():
    e = np.zeros((3, LANES, NSA_GROUPS * NSA_GW), np.float32)
    for head in range(NSA_GROUPS * NSA_REP):
        for b in range(3):
            e[b, head * 3 + b, head * NSA_HEAD_DIM:(head + 1) * NSA_HEAD_DIM] = 1.0
    return jnp.asarray(e, BF16)


def _block_diag_kv(k_cmp, v_cmp):
    g, nc, dh = k_cmp.shape
    eye = jnp.eye(NSA_REP, dtype=bool)
    kct = jnp.swapaxes(k_cmp, 1, 2)
    kbd = jnp.where(eye[None, :, None, :, None], kct[:, None, :, None, :], 0.0)
    vbd = jnp.where(eye[None, :, None, :, None], v_cmp[:, None, :, None, :], 0.0)
    return (kbd.reshape(g, NSA_REP * dh, NSA_REP * nc).astype(BF16),
            vbd.reshape(g, NSA_REP * nc, NSA_REP * dh).astype(BF16))


def kernel(x, p, mix_pre_norm, mix_post_norm, ffn_pre_norm, ffn_post_norm, gdn_w_in, gdn_conv_w, gdn_a_log,
           gdn_dt_bias, gdn_o_norm, gdn_w_out, kv_norm, kv_w, cmp_pos, cmp_w1, cmp_w2, nsa_w_qg, nsa_w_o,
           ffn_w_in, ffn_w_out, ple_w_in, ple_w_gate):
    depth = p.shape[0]
    n_a = gdn_w_in.shape[0]
    t = x.shape[1]
    h = x[0]
    fh = ffn_w_out.shape[1]
    conv_w_cols = gdn_conv_w.shape[2]
    vw = GDN_V_HEADS * GDN_HEAD_DIM
    main_w = conv_w_cols + vw

    def channel_and_ple(h, i):
        w_in = ffn_w_in[i].astype(BF16)
        return ffn_ple(h, ffn_pre_norm[i], w_in[:, :fh], w_in[:, fh:], ffn_w_out[i].astype(BF16),
                       ffn_post_norm[i], p[i, 0], ple_w_in[i].astype(BF16), ple_w_gate[i].astype(BF16))

    for i in range(n_a):
        w_in = gdn_w_in[i]
        w_beta = w_in[:, main_w:main_w + GDN_V_HEADS]
        w_a = w_in[:, main_w + GDN_V_HEADS:]
        w_small = _pad_cols(jnp.concatenate([w_beta, w_a, w_a], axis=1), LANES).astype(BF16)
        proj = norm_matmul(h, mix_pre_norm[i], w_in[:, :main_w].astype(BF16))
        ba = norm_matmul(h, mix_pre_norm[i], w_small)
        pad_vec = lambda v: jnp.pad(v, (GDN_V_HEADS, LANES - 2 * GDN_V_HEADS))
        alog_vec = (pad_vec(gdn_a_log[i]) + jnp.pad(gdn_a_log[i], (2 * GDN_V_HEADS, LANES - 3 * GDN_V_HEADS)))
        dtb_vec = (pad_vec(gdn_dt_bias[i]) + jnp.pad(gdn_dt_bias[i], (2 * GDN_V_HEADS, LANES - 3 * GDN_V_HEADS)))
        slab = gdn_gates(ba, alog_vec.reshape(1, LANES), dtb_vec.reshape(1, LANES))
        qkv_hm = gdn_conv(proj, gdn_conv_w[i])
        kt_hm = jnp.swapaxes(qkv_hm[GDN_QK_HEADS:2 * GDN_QK_HEADS], 1, 2)
        o = gdn_scan(qkv_hm, kt_hm, slab, slab.T)
        onorm_tiled = jnp.tile(gdn_o_norm[i], GDN_V_HEADS).reshape(1, vw)
        h = gdn_out(o, proj, onorm_tiled, gdn_w_out[i].astype(BF16), mix_post_norm[i], h)
        h = channel_and_ple(h, i)

    g = NSA_GROUPS
    dh = NSA_HEAD_DIM
    kv = norm_matmul(h, kv_norm, kv_w.astype(BF16), tn=768)
    kv6 = jnp.transpose(kv.reshape(t, 6, g, dh), (1, 2, 0, 3))
    nc = t // CMP_STRIDE
    nsel = t // SEL_BLOCK
    x2 = kv6[0:2].reshape(2, g, nc, CMP_STRIDE * dh)
    pos2 = cmp_pos.reshape(2, 2, CMP_STRIDE * dh)
    cmp_out = compress(x2, pos2, cmp_w1.astype(BF16), cmp_w2.astype(BF16))
    imp_mat_t = _importance_matrix(nc, nsel).T
    nq_r = t // (CMP_TQ * CMP_RANGES)
    assert t % (CMP_TQ * CMP_RANGES) == 0 and nc % CMP_RANGES == 0 and nsel % (8 * CMP_RANGES) == 0
    cmp_ranges = []
    for rg in range(CMP_RANGES):
        nc_r, nsel_r = (rg + 1) * nc // CMP_RANGES, (rg + 1) * nsel // CMP_RANGES
        vbd_t, kbd = _block_diag_kv(cmp_out[1][:, :nc_r], cmp_out[0][:, :nc_r])
        cmp_ranges.append((kbd, vbd_t, imp_mat_t[:nsel_r, :nc_r]))
    kt = FLASH_KT
    tiles = lambda a: a.reshape(g, t // kt, kt, dh)
    rep_k = lambda a: jnp.tile(tiles(a), (1, 1, 1, NSA_REP)).astype(BF16)
    rep_t = lambda a: jnp.tile(jnp.swapaxes(tiles(a), 2, 3), (1, 1, NSA_REP, 1)).astype(BF16)
    k_slc, v_slc_t = rep_k(kv6[2]), rep_t(kv6[3])
    k_win, v_win_t = rep_k(kv6[4]), rep_t(kv6[5])
    expand = _gate_expand()

    for i in range(n_a, depth):
        j = i - n_a
        qw = g * NSA_GW
        w_qg = nsa_w_qg[j]
        q = norm_matmul(h, mix_pre_norm[i], w_qg[:, :qw].astype(BF16))
        gate_logits = norm_matmul(h, mix_pre_norm[i], _pad_cols(w_qg[:, qw:], LANES).astype(BF16))
        q_t = q.T
        parts = [cmp_topk(q_t, kbd, vbd_t, imp_r, nsel, min(SEL_TOPK, nsel), rg * nq_r, nq_r, CMP_TQ)
                 for rg, (kbd, vbd_t, imp_r) in enumerate(cmp_ranges)]
        o_c = jnp.concatenate([o for o, _ in parts], axis=0)
        sel_t = jnp.concatenate([sl for _, sl in parts], axis=2)
        o_s = flash_branch(q_t, k_slc, v_slc_t, sel_t, tq=1024, kt=kt)
        o_w = flash_branch(q_t, k_win, v_win_t, None, tq=512, kt=kt)
        h = nsa_out(o_c, o_s, o_w, gate_logits, expand, nsa_w_o[j].astype(BF16), mix_post_norm[i], h)
        h = channel_and_ple(h, i)
    return h[None]
```

```python
import functools

import numpy as np
import jax
import jax.numpy as jnp
from jax import lax
from jax.experimental import pallas as pl
from jax.experimental.pallas import tpu as pltpu

F32 = jnp.float32
BF16 = jnp.bfloat16

NORM_EPS = 1e-6
L2_EPS = 1e-6
GDN_QK_HEADS = 8
GDN_V_HEADS = 16
GDN_HEAD_DIM = 128
GDN_CONV = 4
GDN_CHUNK = 64
GDN_GROUP = 256
GDN_SCAN_HEADS = 4
GDN_CONV_HEADS = 4
NSA_GROUPS = 4
NSA_REP = 4
NSA_HEAD_DIM = 64
NSA_GW = NSA_REP * NSA_HEAD_DIM
CMP_BLOCK = 32
CMP_STRIDE = 16
SEL_BLOCK = 64
SEL_TOPK = 16
WINDOW = 512
FORCED_SCORE = 1e4
LANES = 128
NEG_INIT = -(2.0 ** 100)
LOG2_E = 1.4426950408889634
FLASH_KT = 256
FLASH_SUB = 256
FLASH_SUM_ROWS = 16
CMP_ROWS = 256
CMP_TQ = 256
CMP_RANGES = 4

VMEM_LIMIT = 56 * 1024 * 1024


def _cparams(sem):
    return pltpu.CompilerParams(dimension_semantics=sem, vmem_limit_bytes=VMEM_LIMIT)


def _rms(x, gain):
    return x * lax.rsqrt(jnp.mean(x * x, axis=-1, keepdims=True) + NORM_EPS) * gain


def _mm(a, b):
    return jnp.dot(a.astype(BF16), b.astype(BF16), preferred_element_type=F32)


def _sigmoid(x):
    return 1.0 / (1.0 + jnp.exp(-x))


def _div_pow2(x, d):
    shift = d.bit_length() - 1
    assert d == 1 << shift
    return jnp.right_shift(x, shift)


def _split3(x):
    a = x.astype(BF16)
    r = x - a.astype(F32)
    b = r.astype(BF16)
    c = (r - b.astype(F32)).astype(BF16)
    return a, b, c


def _norm_matmul_kernel(x_ref, g_ref, w_ref, o_ref, xn_ref):
    @pl.when(pl.program_id(1) == 0)
    def _():
        xn_ref[...] = _rms(x_ref[...], g_ref[...]).astype(BF16)

    o_ref[...] = jnp.dot(xn_ref[...], w_ref[...], preferred_element_type=F32)


def _norm_matmul_t_kernel(x_ref, g_ref, wt_ref, o_ref, xn_ref):
    @pl.when(pl.program_id(1) == 0)
    def _():
        xn_ref[...] = _rms(x_ref[...], g_ref[...]).astype(BF16)

    o_ref[...] = lax.dot_general(wt_ref[...], xn_ref[...], (((1,), (1,)), ((), ())), preferred_element_type=F32)


def norm_matmul_t(h, gain, w_t, tm=1024, tn=1024):
    t, d = h.shape
    n = w_t.shape[0]
    tn = min(tn, n)
    assert t % tm == 0 and n % tn == 0
    return pl.pallas_call(
        _norm_matmul_t_kernel,
        grid=(t // tm, n // tn),
        in_specs=[
            pl.BlockSpec((tm, d), lambda i, j: (i, 0)),
            pl.BlockSpec((1, d), lambda i, j: (0, 0)),
            pl.BlockSpec((tn, d), lambda i, j: (j, 0)),
        ],
        out_specs=pl.BlockSpec((tn, tm), lambda i, j: (j, i)),
        out_shape=jax.ShapeDtypeStruct((n, t), F32),
        scratch_shapes=[pltpu.VMEM((tm, d), BF16)],
        compiler_params=_cparams(("parallel", "arbitrary")),
        name="norm_matmul_t",
    )(h, gain.reshape(1, d), w_t)


def norm_matmul(h, gain, w, tm=1024, tn=1024):
    t, d = h.shape
    n = w.shape[1]
    tn = min(tn, n)
    assert t % tm == 0 and n % tn == 0
    return pl.pallas_call(
        _norm_matmul_kernel,
        grid=(t // tm, n // tn),
        in_specs=[
            pl.BlockSpec((tm, d), lambda i, j: (i, 0)),
            pl.BlockSpec((1, d), lambda i, j: (0, 0)),
            pl.BlockSpec((d, tn), lambda i, j: (0, j)),
        ],
        out_specs=pl.BlockSpec((tm, tn), lambda i, j: (i, j)),
        out_shape=jax.ShapeDtypeStruct((t, n), F32),
        scratch_shapes=[pltpu.VMEM((tm, d), BF16)],
        compiler_params=_cparams(("parallel", "arbitrary")),
        name="norm_matmul",
    )(h, gain.reshape(1, d), w)


def _ffn_ple_kernel(h_ref, gpre_ref, wg_ref, wu_ref, wo_ref, gpost_ref, p_ref, wple_ref, wgt_ref,
                    o_ref, xn_ref, acc_ref, *, nf):
    f = pl.program_id(1)

    @pl.when(f == 0)
    def _():
        xn_ref[...] = _rms(h_ref[...], gpre_ref[...]).astype(BF16)
        acc_ref[...] = jnp.zeros_like(acc_ref)

    xn = xn_ref[...]
    gate = jnp.dot(xn, wg_ref[...], preferred_element_type=F32)
    up = jnp.dot(xn, wu_ref[...], preferred_element_type=F32)
    act = gate * _sigmoid(gate) * up
    acc_ref[...] += jnp.dot(act.astype(BF16), wo_ref[...], preferred_element_type=F32)

    @pl.when(f == nf - 1)
    def _():
        h2 = h_ref[...] + _rms(acc_ref[...], gpost_ref[...])
        emb = jnp.dot(p_ref[...].astype(BF16), wple_ref[...], preferred_element_type=F32)
        gt = _sigmoid(jnp.dot(h2.astype(BF16), wgt_ref[...], preferred_element_type=F32))
        o_ref[...] = h2 + emb * gt


def ffn_ple(h, gpre, w_gate, w_up, w_out, gpost, p, w_ple, w_plegate, tm=512, tf=1408):
    t, d = h.shape
    fh = w_gate.shape[1]
    pd = p.shape[1]
    assert t % tm == 0 and fh % tf == 0
    nf = fh // tf
    return pl.pallas_call(
        functools.partial(_ffn_ple_kernel, nf=nf),
        grid=(t // tm, nf),
        in_specs=[
            pl.BlockSpec((tm, d), lambda i, f: (i, 0)),
            pl.BlockSpec((1, d), lambda i, f: (0, 0)),
            pl.BlockSpec((d, tf), lambda i, f: (0, f)),
            pl.BlockSpec((d, tf), lambda i, f: (0, f)),
            pl.BlockSpec((tf, d), lambda i, f: (f, 0)),
            pl.BlockSpec((1, d), lambda i, f: (0, 0)),
            pl.BlockSpec((tm, pd), lambda i, f: (i, 0)),
            pl.BlockSpec((pd, d), lambda i, f: (0, 0)),
            pl.BlockSpec((d, d), lambda i, f: (0, 0)),
        ],
        out_specs=pl.BlockSpec((tm, d), lambda i, f: (i, 0)),
        out_shape=jax.ShapeDtypeStruct((t, d), F32),
        scratch_shapes=[pltpu.VMEM((tm, d), BF16), pltpu.VMEM((tm, d), F32)],
        compiler_params=_cparams(("parallel", "arbitrary")),
        name="ffn_ple",
    )(h, gpre.reshape(1, d), w_gate, w_up, w_out, gpost.reshape(1, d), p, w_ple, w_plegate)


def _gdn_conv_kernel(x_ref, halo_ref, w_ref, o_ref, *, tm):
    c = pl.program_id(0)
    i = pl.program_id(1)
    x = x_ref[...]
    halo = jnp.where(i > 0, halo_ref[...], 0.0)
    ext = jnp.concatenate([halo, x], axis=0)
    w = w_ref[...]
    y = x * w[GDN_CONV - 1:GDN_CONV, :]
    for k in range(1, GDN_CONV):
        shifted = pltpu.roll(ext, k, axis=0)[8:8 + tm]
        y = y + shifted * w[GDN_CONV - 1 - k:GDN_CONV - k, :]
    y = y * _sigmoid(y)
    for hd in range(GDN_CONV_HEADS):
        head = c * GDN_CONV_HEADS + hd
        seg = y[:, hd * LANES:(hd + 1) * LANES]
        normed = seg * lax.rsqrt(jnp.sum(seg * seg, axis=-1, keepdims=True) + L2_EPS)
        q_scale = jnp.where(head < GDN_QK_HEADS, GDN_HEAD_DIM ** -0.5, 1.0)
        o_ref[hd] = jnp.where(head < 2 * GDN_QK_HEADS, normed * q_scale, seg)


def gdn_conv(proj, conv_w, tm=1024):
    t = proj.shape[0]
    n_tiles = conv_w.shape[1] // LANES
    cw = GDN_CONV_HEADS
    assert t % tm == 0 and n_tiles % cw == 0
    return pl.pallas_call(
        functools.partial(_gdn_conv_kernel, tm=tm),
        grid=(n_tiles // cw, t // tm),
        in_specs=[
            pl.BlockSpec((tm, cw * LANES), lambda c, i: (i, c)),
            pl.BlockSpec((8, cw * LANES), lambda c, i: (jnp.maximum(i * (tm // 8) - 1, 0), c)),
            pl.BlockSpec((GDN_CONV, cw * LANES), lambda c, i: (0, c)),
        ],
        out_specs=pl.BlockSpec((cw, tm, LANES), lambda c, i: (c, i, 0)),
        out_shape=jax.ShapeDtypeStruct((n_tiles, t, LANES), F32),
        compiler_params=_cparams(("parallel", "parallel")),
        name="gdn_conv",
    )(proj, proj, conv_w)


def _gdn_gate_kernel(x_ref, alog_ref, dtb_ref, lc_ref, lf_ref, o_ref):
    x = x_ref[...]
    lane = lax.broadcasted_iota(jnp.int32, x.shape, 1)
    beta = _sigmoid(x)
    z = x + dtb_ref[...]
    softplus = jnp.maximum(z, 0.0) + jnp.log(1.0 + jnp.exp(-jnp.abs(z)))
    g = -jnp.exp(alog_ref[...]) * softplus
    gcum = jnp.zeros_like(x)
    gtot = jnp.zeros_like(x)
    for piece in _split3(g):
        gcum = gcum + jnp.dot(lc_ref[...], piece, preferred_element_type=F32)
        gtot = gtot + jnp.dot(lf_ref[...], piece, preferred_element_type=F32)
    o_ref[...] = jnp.where(lane < GDN_V_HEADS, beta, jnp.where(lane < 2 * GDN_V_HEADS, gcum, gtot))


def gdn_gates(ba, alog_vec, dtb_vec):
    t = ba.shape[0]
    tm = GDN_GROUP
    r = np.arange(tm)
    same = (r[:, None] // GDN_CHUNK) == (r[None, :] // GDN_CHUNK)
    lc = jnp.asarray(same & (r[None, :] <= r[:, None]), BF16)
    lf = jnp.asarray(same, BF16)
    return pl.pallas_call(
        _gdn_gate_kernel,
        grid=(t // tm,),
        in_specs=[
            pl.BlockSpec((tm, LANES), lambda i: (i, 0)),
            pl.BlockSpec((1, LANES), lambda i: (0, 0)),
            pl.BlockSpec((1, LANES), lambda i: (0, 0)),
            pl.BlockSpec((tm, tm), lambda i: (0, 0)),
            pl.BlockSpec((tm, tm), lambda i: (0, 0)),
        ],
        out_specs=pl.BlockSpec((tm, LANES), lambda i: (i, 0)),
        out_shape=jax.ShapeDtypeStruct((t, LANES), F32),
        compiler_params=_cparams(("parallel",)),
        name="gdn_gates",
    )(ba, alog_vec, dtb_vec, lc, lf)


def _gdn_scan_kernel(q_ref, k_ref, kt_ref, v_ref, slab_ref, slabt_ref, o_ref, s_ref):
    n = pl.program_id(1)

    @pl.when(n == 0)
    def _():
        s_ref[...] = jnp.zeros_like(s_ref)

    L = GDN_GROUP
    C = GDN_CHUNK
    D = GDN_HEAD_DIM
    slab = slab_ref[...]
    slabt = slabt_ref[...]
    row = lax.broadcasted_iota(jnp.int32, (L, L), 0)
    col = lax.broadcasted_iota(jnp.int32, (L, L), 1)
    same = _div_pow2(row, C) == _div_pow2(col, C)
    causal = same & (col <= row)
    strict = same & (col < row)
    eye = (row == col).astype(F32)
    lane = lax.broadcasted_iota(jnp.int32, (L, LANES), 1)
    sub = lax.broadcasted_iota(jnp.int32, (LANES, L), 0)

    def column(idx):
        return jnp.sum(jnp.where(lane == idx, slab, 0.0), axis=1, keepdims=True)

    def rowvec(idx):
        return jnp.sum(jnp.where(sub == idx, slabt, 0.0), axis=0, keepdims=True)

    colk = lax.broadcasted_iota(jnp.int32, (D, L), 1)
    slots = range(2 * GDN_SCAN_HEADS)

    q = [q_ref[hq] for hq in range(GDN_SCAN_HEADS)]
    k = [k_ref[hq] for hq in range(GDN_SCAN_HEADS)]
    kt = [kt_ref[hq] for hq in range(GDN_SCAN_HEADS)]
    kk = [_mm(k[hq], kt[hq]) for hq in range(GDN_SCAN_HEADS)]
    qk = [_mm(q[hq], kt[hq]) for hq in range(GDN_SCAN_HEADS)]
    beta_c, gc_c, gt_c, decay, bp, inv, kdt = [], [], [], [], [], [], []
    for slot in slots:
        hq = slot // 2
        hv = 2 * (pl.program_id(0) * GDN_SCAN_HEADS + hq) + slot % 2
        beta_c.append(column(hv))
        gc_c.append(column(GDN_V_HEADS + hv))
        gt_c.append(column(2 * GDN_V_HEADS + hv))
        gc_r = rowvec(GDN_V_HEADS + hv)
        gt_r = rowvec(2 * GDN_V_HEADS + hv)
        decay.append(jnp.where(causal, jnp.exp(jnp.where(causal, gc_c[slot] - gc_r, 0.0)), 0.0))
        bp.append(jnp.where(strict, -(kk[hq] * beta_c[slot]) * decay[slot], 0.0))
        inv.append(eye + bp[slot])
        kdt.append(kt[hq] * jnp.exp(gt_r - gc_r))
    for _ in range(5):
        bp = [_mm(bp[slot], bp[slot]) for slot in slots]
        inv = [inv[slot] + _mm(inv[slot], bp[slot]) for slot in slots]
    u, w, qkm, q_dec = [], [], [], []
    for slot in slots:
        hq = slot // 2
        egc = jnp.exp(gc_c[slot])
        rhs = jnp.concatenate([v_ref[slot] * beta_c[slot], k[hq] * (beta_c[slot] * egc)], axis=1)
        sol = _mm(inv[slot], rhs)
        u.append(sol[:, :D])
        w.append(sol[:, D:])
        qkm.append(jnp.where(causal, qk[hq] * decay[slot], 0.0))
        q_dec.append(q[hq] * egc)
    state = [s_ref[slot] for slot in slots]
    v_done = [[] for _ in slots]
    for c in range(L // C):
        lo, hi = c * C, (c + 1) * C
        r = [_mm(jnp.concatenate([w[slot][lo:hi], q_dec[slot][lo:hi]], axis=0), state[slot]) for slot in slots]
        for slot in slots:
            v_done[slot].append(u[slot][lo:hi] - r[slot][:C])
            v_all = jnp.concatenate(v_done[slot] + [jnp.zeros((L - hi, D), F32)] * (hi < L), axis=0)
            o_ref[lo:hi, slot * D:(slot + 1) * D] = r[slot][C:] + _mm(qkm[slot][lo:hi, :], v_all)
            kdt_c = jnp.where((colk >= lo) & (colk < hi), kdt[slot], 0.0)
            state[slot] = state[slot] * jnp.exp(gt_c[slot][lo:lo + 1, :]) + _mm(kdt_c, v_all)
    for slot in slots:
        s_ref[slot] = state[slot]


def gdn_scan(qkv_hm, kt_hm, slab, slabt):
    t = qkv_hm.shape[1]
    L = GDN_GROUP
    D = GDN_HEAD_DIM
    hq = GDN_SCAN_HEADS
    assert t % L == 0 and GDN_QK_HEADS % hq == 0
    q_blocks = GDN_QK_HEADS // hq
    return pl.pallas_call(
        _gdn_scan_kernel,
        grid=(q_blocks, t // L),
        in_specs=[
            pl.BlockSpec((hq, L, D), lambda j, n: (j, n, 0)),
            pl.BlockSpec((hq, L, D), lambda j, n: (q_blocks + j, n, 0)),
            pl.BlockSpec((hq, D, L), lambda j, n: (j, 0, n)),
            pl.BlockSpec((2 * hq, L, D), lambda j, n: (q_blocks + j, n, 0)),
            pl.BlockSpec((L, LANES), lambda j, n: (n, 0)),
            pl.BlockSpec((LANES, L), lambda j, n: (0, n)),
        ],
        out_specs=pl.BlockSpec((L, 2 * hq * D), lambda j, n: (n, j)),
        out_shape=jax.ShapeDtypeStruct((t, GDN_V_HEADS * D), F32),
        scratch_shapes=[pltpu.VMEM((2 * hq, D, D), F32)],
        compiler_params=_cparams(("parallel", "arbitrary")),
        name="gdn_scan",
    )(qkv_hm, qkv_hm, kt_hm, qkv_hm, slab, slabt)


def _gdn_out_kernel(o_ref, z_ref, onorm_ref, w_ref, gpost_ref, h_ref, out_ref):
    o = o_ref[...]
    z = z_ref[...]
    parts = []
    for hd in range(GDN_V_HEADS):
        seg = o[:, hd * GDN_HEAD_DIM:(hd + 1) * GDN_HEAD_DIM]
        parts.append(seg * lax.rsqrt(jnp.mean(seg * seg, axis=-1, keepdims=True) + NORM_EPS))
    gated = jnp.concatenate(parts, axis=1) * onorm_ref[...] * (z * _sigmoid(z))
    mix = jnp.dot(gated.astype(BF16), w_ref[...], preferred_element_type=F32)
    out_ref[...] = h_ref[...] + _rms(mix, gpost_ref[...])


def gdn_out(o, proj, onorm_tiled, w_out, gpost, h, tm=512):
    t, vw = o.shape
    d = h.shape[1]
    z_blk = (proj.shape[1] - vw) // vw
    assert proj.shape[1] % vw == 0 and t % tm == 0
    return pl.pallas_call(
        _gdn_out_kernel,
        grid=(t // tm,),
        in_specs=[
            pl.BlockSpec((tm, vw), lambda i: (i, 0)),
            pl.BlockSpec((tm, vw), lambda i: (i, z_blk)),
            pl.BlockSpec((1, vw), lambda i: (0, 0)),
            pl.BlockSpec((vw, d), lambda i: (0, 0)),
            pl.BlockSpec((1, d), lambda i: (0, 0)),
            pl.BlockSpec((tm, d), lambda i: (i, 0)),
        ],
        out_specs=pl.BlockSpec((tm, d), lambda i: (i, 0)),
        out_shape=jax.ShapeDtypeStruct((t, d), F32),
        compiler_params=_cparams(("parallel",)),
        name="gdn_out",
    )(o, proj, onorm_tiled, w_out, gpost.reshape(1, d), h)


def _compress_kernel(x_ref, pos_ref, w1_ref, w2_ref, o_ref):
    x = x_ref[...]
    pos = pos_ref[...]
    nc, half = x.shape
    w1 = w1_ref[...]
    first = _mm(x + pos[0:1, :], w1[:half])
    second = _mm(x + pos[1:2, :], w1[half:])
    hid = first + pltpu.roll(second, nc - 1, axis=0)
    hid = hid * _sigmoid(hid)
    out = jnp.dot(hid.astype(BF16), w2_ref[...], preferred_element_type=F32)
    rowi = lax.broadcasted_iota(jnp.int32, out.shape, 0)
    o_ref[...] = jnp.where(rowi < nc - 1, out, 0.0)


def compress(x2, pos2, w1, w2):
    _, g, nc, wdt = x2.shape
    hid = w1.shape[2]
    dh = w2.shape[2]
    return pl.pallas_call(
        _compress_kernel,
        grid=(2, g),
        in_specs=[
            pl.BlockSpec((None, None, nc, wdt), lambda b, gi: (b, gi, 0, 0)),
            pl.BlockSpec((None, 2, wdt), lambda b, gi: (b, 0, 0)),
            pl.BlockSpec((None, 2 * wdt, hid), lambda b, gi: (b, 0, 0)),
            pl.BlockSpec((None, hid, dh), lambda b, gi: (b, 0, 0)),
        ],
        out_specs=pl.BlockSpec((None, None, nc, dh), lambda b, gi: (b, gi, 0, 0)),
        out_shape=jax.ShapeDtypeStruct((2, g, nc, dh), F32),
        compiler_params=_cparams(("parallel", "parallel")),
        name="nsa_compress",
    )(x2, pos2, w1, w2)


def _cmp_topk_kernel(q_ref, kbd_ref, vt_ref, m_ref, oc_ref, sel_ref, s_ref, p_ref, psum_ref, *,
                     tq, nc, nsel, topk, q0):
    i = q0 + pl.program_id(1)
    dh = NSA_HEAD_DIM
    rows_per = min(nc, CMP_ROWS)
    q = (q_ref[...] * ((dh ** -0.5) * LOG2_E)).astype(BF16)
    s_ref[...] = jnp.dot(kbd_ref[...], q, preferred_element_type=F32)
    for ch in range(tq // LANES):
        lanes = slice(ch * LANES, (ch + 1) * LANES)
        tpos = i * tq + ch * LANES + lax.broadcasted_iota(jnp.int32, (rows_per, LANES), 1)
        cblk0 = lax.broadcasted_iota(jnp.int32, (rows_per, LANES), 0)
        masks = [(CMP_STRIDE * (cblk0 + c * rows_per) + CMP_BLOCK - 1) <= tpos for c in range(nc // rows_per)]
        for r in range(NSA_REP):
            pieces = [slice(r * nc + c * rows_per, r * nc + (c + 1) * rows_per) for c in range(nc // rows_per)]
            m = jnp.full((1, LANES), -jnp.inf, F32)
            for rows, mask in zip(pieces, masks):
                m = jnp.maximum(m, jnp.max(jnp.where(mask, s_ref[rows, lanes], -jnp.inf), axis=0, keepdims=True))
            m = jnp.where(m > -jnp.inf, m, 0.0)
            total = jnp.zeros((1, LANES), F32)
            for rows, mask in zip(pieces, masks):
                e = jnp.exp2(jnp.where(mask, s_ref[rows, lanes], -jnp.inf) - m)
                s_ref[rows, lanes] = e
                total = total + jnp.sum(e, axis=0, keepdims=True)
            inv = 1.0 / jnp.maximum(total, 1e-30)
            for c, rows in enumerate(pieces):
                p = s_ref[rows, lanes] * inv
                p_ref[rows, lanes] = p.astype(BF16)
                prow = slice(c * rows_per, (c + 1) * rows_per)
                psum_ref[prow, lanes] = p if r == 0 else psum_ref[prow, lanes] + p
    oc_t = jnp.dot(vt_ref[...], p_ref[...], preferred_element_type=F32)
    imp = jnp.zeros((nsel, tq), F32)
    for piece in _split3(psum_ref[...]):
        imp = imp + jnp.dot(m_ref[...], piece, preferred_element_type=F32)
    for ch in range(tq // LANES):
        lanes = slice(ch * LANES, (ch + 1) * LANES)
        oc_ref[lanes, :] = oc_t[:, lanes].T
        t1 = i * tq + ch * LANES + lax.broadcasted_iota(jnp.int32, (nsel, LANES), 1)
        blk = lax.broadcasted_iota(jnp.int32, (nsel, LANES), 0)
        cur = _div_pow2(t1, SEL_BLOCK)
        forced = (blk == 0) | (blk == cur) | (blk == cur - 1)
        valid = blk * SEL_BLOCK <= t1
        score = jnp.where(valid, jnp.where(forced, FORCED_SCORE, imp[:, lanes]), -jnp.inf)
        blkf = blk.astype(F32)
        work = score
        for _ in range(topk):
            mx = jnp.max(work, axis=0, keepdims=True)
            first = jnp.min(jnp.where(work == mx, blkf, float(nsel)), axis=0, keepdims=True)
            work = jnp.where(blkf == first, -jnp.inf, work)
        sel_ref[:nsel, lanes] = jnp.where((score > -jnp.inf) & (work == -jnp.inf), 1.0, 0.0).astype(sel_ref.dtype)
        if sel_ref.shape[0] > nsel:
            sel_ref[nsel:, lanes] = jnp.zeros((sel_ref.shape[0] - nsel, LANES), sel_ref.dtype)


def cmp_topk(q_t, kbd, v_t, imp_mat_t, nsel_all, topk, q0, nq, tq):
    qw = q_t.shape[0]
    g = kbd.shape[0]
    nc = kbd.shape[1] // NSA_REP
    nsel = imp_mat_t.shape[0]
    assert nc % min(nc, CMP_ROWS) == 0
    return pl.pallas_call(
        functools.partial(_cmp_topk_kernel, tq=tq, nc=nc, nsel=nsel, topk=topk, q0=q0),
        grid=(g, nq),
        in_specs=[
            pl.BlockSpec((NSA_GW, tq), lambda gi, i: (gi, q0 + i)),
            pl.BlockSpec((None, NSA_REP * nc, NSA_GW), lambda gi, i: (gi, 0, 0)),
            pl.BlockSpec((None, NSA_GW, NSA_REP * nc), lambda gi, i: (gi, 0, 0)),
            pl.BlockSpec((nsel, nc), lambda gi, i: (0, 0)),
        ],
        out_specs=[
            pl.BlockSpec((tq, NSA_GW), lambda gi, i: (i, gi)),
            pl.BlockSpec((None, nsel_all, tq), lambda gi, i: (gi, 0, i)),
        ],
        out_shape=[
            jax.ShapeDtypeStruct((nq * tq, qw), F32),
            jax.ShapeDtypeStruct((g, nsel_all, nq * tq), F32),
        ],
        scratch_shapes=[
            pltpu.VMEM((NSA_REP * nc, tq), F32),
            pltpu.VMEM((NSA_REP * nc, tq), BF16),
            pltpu.VMEM((nc, tq), F32),
        ],
        compiler_params=_cparams(("parallel", "parallel")),
        name="nsa_cmp_topk",
    )(q_t, kbd, v_t, imp_mat_t)


def _flash_kernel(*refs, tq, kt, selected):
    if selected:
        q_ref, k_ref, v_ref, sel_ref, o_ref = refs[:5]
    else:
        q_ref, k_ref, v_ref, o_ref = refs[:4]
    m_ref, l_ref, alpha_ref, acc_ref, s_ref, p_ref, bias_ref, kbd_ref, vbd_ref, qs_ref = refs[-10:]
    qi = pl.program_id(1)
    dh = NSA_HEAD_DIM
    n_sub = tq // FLASH_SUB
    n_blk = kt // SEL_BLOCK

    m_ref[...] = jnp.full_like(m_ref, NEG_INIT)
    l_ref[...] = jnp.zeros_like(l_ref)
    acc_ref[...] = jnp.zeros_like(acc_ref)
    qs_ref[...] = (q_ref[...] * ((dh ** -0.5) * LOG2_E)).astype(BF16)
    one_row = lax.broadcasted_iota(jnp.int32, (FLASH_SUM_ROWS, NSA_REP * kt), 0)
    one_head = _div_pow2(lax.broadcasted_iota(jnp.int32, (FLASH_SUM_ROWS, NSA_REP * kt), 1), kt)
    vbd_ref[NSA_GW:, :] = (one_row == one_head).astype(BF16)

    def key_tile(ki, positional):
        k4 = k_ref[ki]
        kseg = _div_pow2(lax.broadcasted_iota(jnp.int32, k4.shape, 1), dh)
        v4 = v_ref[ki]
        vblk = _div_pow2(lax.broadcasted_iota(jnp.int32, v4.shape, 0), dh)
        for r in range(NSA_REP):
            kbd_ref[r * kt:(r + 1) * kt, :] = jnp.where(kseg == r, k4, jnp.zeros_like(k4))
            vbd_ref[:NSA_GW, r * kt:(r + 1) * kt] = jnp.where(vblk == r, v4, jnp.zeros_like(v4))

        def scores(sub):
            c0 = sub * FLASH_SUB
            cols = slice(c0, c0 + FLASH_SUB)
            s_ref[:, cols] = jnp.dot(kbd_ref[...], qs_ref[:, cols],
                                     preferred_element_type=F32)
            for jb in range(n_blk):
                rows = slice(jb * SEL_BLOCK, (jb + 1) * SEL_BLOCK)
                if selected:
                    picked = sel_ref[pl.ds(ki * n_blk + jb, 1), cols] > 0.5
                if positional:
                    tpos = qi * tq + c0 + lax.broadcasted_iota(jnp.int32, (SEL_BLOCK, FLASH_SUB), 1)
                    kpos = ki * kt + jb * SEL_BLOCK + lax.broadcasted_iota(jnp.int32, (SEL_BLOCK, FLASH_SUB), 0)
                    if selected:
                        allowed = picked & (kpos <= tpos)
                    else:
                        allowed = (kpos <= tpos) & (kpos > tpos - WINDOW)
                else:
                    allowed = jnp.broadcast_to(picked, (SEL_BLOCK, FLASH_SUB))
                bias_ref[rows, cols] = jnp.where(allowed, 0.0, -jnp.inf)

        scores(0)
        for sub in range(n_sub):
            c0 = sub * FLASH_SUB
            cols = slice(c0, c0 + FLASH_SUB)
            if sub + 1 < n_sub:
                scores(sub + 1)
            for ch in range(FLASH_SUB // LANES):
                lanes = slice(c0 + ch * LANES, c0 + (ch + 1) * LANES)
                bias = bias_ref[:, lanes]
                for r in range(NSA_REP):
                    x = s_ref[r * kt:(r + 1) * kt, lanes] + bias
                    m_prev = m_ref[r:r + 1, lanes]
                    m_new = jnp.maximum(m_prev, jnp.max(x, axis=0, keepdims=True))
                    m_ref[r:r + 1, lanes] = m_new
                    alpha_ref[r:r + 1, lanes] = jnp.exp2(m_prev - m_new)
                    p_ref[r * kt:(r + 1) * kt, lanes] = jnp.exp2(x - m_new).astype(BF16)
            pv = jnp.dot(vbd_ref[...], p_ref[:, cols], preferred_element_type=F32)
            for r in range(NSA_REP):
                hd = slice(r * dh, (r + 1) * dh)
                alpha = alpha_ref[r:r + 1, cols]
                acc_ref[hd, cols] = acc_ref[hd, cols] * alpha + pv[hd]
                l_ref[r:r + 1, cols] = l_ref[r:r + 1, cols] * alpha + pv[NSA_GW + r:NSA_GW + r + 1]

    def walk(lo, hi, positional):
        def step(ki, carry):
            key_tile(ki, positional)
            return carry
        lax.fori_loop(lo, hi, step, 0)

    diag = qi * (tq // kt)
    if selected:
        walk(0, diag, False)
        walk(diag, diag + tq // kt, True)
    else:
        first = jnp.maximum(qi * tq - (WINDOW - 1), 0) // kt
        walk(first, diag + tq // kt, True)

    for ch in range(tq // LANES):
        lanes = slice(ch * LANES, (ch + 1) * LANES)
        out_t = jnp.concatenate(
            [acc_ref[r * dh:(r + 1) * dh, lanes] / l_ref[r:r + 1, lanes] for r in range(NSA_REP)], axis=0)
        o_ref[lanes, :] = out_t.T


def flash_branch(q_t, k4, vt4, sel_t, tq, kt):
    qw, t = q_t.shape
    g, nk = k4.shape[:2]
    selected = sel_t is not None
    assert t % tq == 0 and tq % kt == 0 and kt % SEL_BLOCK == 0 and tq % FLASH_SUB == 0
    in_specs = [
        pl.BlockSpec((NSA_GW, tq), lambda gi, i: (gi, i)),
        pl.BlockSpec((None, nk, kt, NSA_GW), lambda gi, i: (gi, 0, 0, 0)),
        pl.BlockSpec((None, nk, NSA_GW, kt), lambda gi, i: (gi, 0, 0, 0)),
    ]
    args = [q_t, k4, vt4]
    if selected:
        nsel = sel_t.shape[1]
        in_specs.append(pl.BlockSpec((None, nsel, tq), lambda gi, i: (gi, 0, i)))
        args.append(sel_t)
    return pl.pallas_call(
        functools.partial(_flash_kernel, tq=tq, kt=kt, selected=selected),
        grid=(g, t // tq),
        in_specs=in_specs,
        out_specs=pl.BlockSpec((tq, NSA_GW), lambda gi, i: (i, gi)),
        scratch_shapes=[
            pltpu.VMEM((8, tq), F32),
            pltpu.VMEM((8, tq), F32),
            pltpu.VMEM((8, tq), F32),
            pltpu.VMEM((NSA_GW, tq), F32),
            pltpu.VMEM((NSA_REP * kt, tq), F32),
            pltpu.VMEM((NSA_REP * kt, tq), BF16),
            pltpu.VMEM((kt, tq), F32),
            pltpu.VMEM((NSA_REP * kt, NSA_GW), BF16),
            pltpu.VMEM((NSA_GW + FLASH_SUM_ROWS, NSA_REP * kt), BF16),
            pltpu.VMEM((NSA_GW, tq), BF16),
        ],
        out_shape=jax.ShapeDtypeStruct((t, qw), F32),
        compiler_params=_cparams(("parallel", "arbitrary")),
        name="nsa_selected" if selected else "nsa_window",
    )(*args)


def _window_kernel(q_ref, k_ref, v_ref, o_ref, s_ref, p_ref, bias_ref, kbd_ref, vbd_ref, *, tq, kt):
    i = pl.program_id(1)
    dh = NSA_HEAD_DIM
    nw = WINDOW // kt + 1
    span = nw * kt
    qs = (q_ref[...] * ((dh ** -0.5) * LOG2_E)).astype(BF16)
    for w in range(nw):
        tile = i - (nw - 1) + w
        k4 = k_ref[jnp.maximum(tile, 0)]
        kseg = _div_pow2(lax.broadcasted_iota(jnp.int32, k4.shape, 1), dh)
        v4 = v_ref[jnp.maximum(tile, 0)]
        vblk = _div_pow2(lax.broadcasted_iota(jnp.int32, v4.shape, 0), dh)
        for r in range(NSA_REP):
            at = r * span + w * kt
            kbd_ref[at:at + kt, :] = jnp.where(kseg == r, k4, jnp.zeros_like(k4))
            vbd_ref[:NSA_GW, at:at + kt] = jnp.where(vblk == r, v4, jnp.zeros_like(v4))
        tpos = i * tq + lax.broadcasted_iota(jnp.int32, (kt, tq), 1)
        kpos = tile * kt + lax.broadcasted_iota(jnp.int32, (kt, tq), 0)
        allowed = (kpos >= 0) & (kpos <= tpos) & (kpos > tpos - WINDOW)
        bias_ref[w * kt:(w + 1) * kt, :] = jnp.where(allowed, 0.0, -jnp.inf)
    one_row = lax.broadcasted_iota(jnp.int32, (FLASH_SUM_ROWS, NSA_REP * span), 0)
    one_col = lax.broadcasted_iota(jnp.int32, (FLASH_SUM_ROWS, NSA_REP * span), 1)
    vbd_ref[NSA_GW:, :] = ((one_col >= one_row * span) & (one_col < (one_row + 1) * span)).astype(BF16)

    s_ref[...] = jnp.dot(kbd_ref[...], qs, preferred_element_type=F32)
    for ch in range(tq // LANES):
        lanes = slice(ch * LANES, (ch + 1) * LANES)
        for r in range(NSA_REP):
            m = jnp.full((1, LANES), -jnp.inf, F32)
            for w in range(nw):
                rows = slice(r * span + w * kt, r * span + (w + 1) * kt)
                m = jnp.maximum(m, jnp.max(s_ref[rows, lanes] + bias_ref[w * kt:(w + 1) * kt, lanes],
                                           axis=0, keepdims=True))
            m = jnp.where(m > -jnp.inf, m, 0.0)
            for w in range(nw):
                rows = slice(r * span + w * kt, r * span + (w + 1) * kt)
                x = s_ref[rows, lanes] + bias_ref[w * kt:(w + 1) * kt, lanes]
                p_ref[rows, lanes] = jnp.exp2(x - m).astype(BF16)
    pv = jnp.dot(vbd_ref[...], p_ref[...], preferred_element_type=F32)
    for ch in range(tq // LANES):
        lanes = slice(ch * LANES, (ch + 1) * LANES)
        out_t = jnp.concatenate(
            [pv[r * dh:(r + 1) * dh, lanes] / jnp.maximum(pv[NSA_GW + r:NSA_GW + r + 1, lanes], 1e-30)
             for r in range(NSA_REP)], axis=0)
        o_ref[lanes, :] = out_t.T


def window_branch(q_t, k4, vt4, tq):
    qw, t = q_t.shape
    g, nk, kt, _ = k4.shape
    assert tq == kt and WINDOW % kt == 0 and t % tq == 0
    span = (WINDOW // kt + 1) * kt
    return pl.pallas_call(
        functools.partial(_window_kernel, tq=tq, kt=kt),
        grid=(g, t // tq),
        in_specs=[
            pl.BlockSpec((NSA_GW, tq), lambda gi, i: (gi, i)),
            pl.BlockSpec((None, nk, kt, NSA_GW), lambda gi, i: (gi, 0, 0, 0)),
            pl.BlockSpec((None, nk, NSA_GW, kt), lambda gi, i: (gi, 0, 0, 0)),
        ],
        out_specs=pl.BlockSpec((tq, NSA_GW), lambda gi, i: (i, gi)),
        scratch_shapes=[
            pltpu.VMEM((NSA_REP * span, tq), F32),
            pltpu.VMEM((NSA_REP * span, tq), BF16),
            pltpu.VMEM((span, tq), F32),
            pltpu.VMEM((NSA_REP * span, NSA_GW), BF16),
            pltpu.VMEM((NSA_GW + FLASH_SUM_ROWS, NSA_REP * span), BF16),
        ],
        out_shape=jax.ShapeDtypeStruct((t, qw), F32),
        compiler_params=_cparams(("parallel", "parallel")),
        name="nsa_window",
    )(q_t, k4, vt4)


def _nsa_out_kernel(oc_ref, os_ref, ow_ref, gl_ref, eg_ref, w_ref, gpost_ref, h_ref, out_ref):
    pieces = _split3(_sigmoid(gl_ref[...]))
    mixed = jnp.zeros(oc_ref.shape, F32)
    for b, br_ref in enumerate((oc_ref, os_ref, ow_ref)):
        gfull = jnp.zeros(oc_ref.shape, F32)
        for piece in pieces:
            gfull = gfull + jnp.dot(piece, eg_ref[b], preferred_element_type=F32)
        mixed = mixed + gfull * br_ref[...]
    mix = jnp.dot(mixed.astype(BF16), w_ref[...], preferred_element_type=F32)
    out_ref[...] = h_ref[...] + _rms(mix, gpost_ref[...])


def nsa_out(oc, osel, ow, gate_logits, expand, w_o, gpost, h, tm=512):
    t, qw = oc.shape
    d = h.shape[1]
    assert t % tm == 0
    row = lambda w: pl.BlockSpec((tm, w), lambda i: (i, 0))
    return pl.pallas_call(
        _nsa_out_kernel,
        grid=(t // tm,),
        in_specs=[
            row(qw), row(qw), row(qw), row(LANES),
            pl.BlockSpec((3, LANES, qw), lambda i: (0, 0, 0)),
            pl.BlockSpec((qw, d), lambda i: (0, 0)),
            pl.BlockSpec((1, d), lambda i: (0, 0)),
            row(d),
        ],
        out_specs=row(d),
        out_shape=jax.ShapeDtypeStruct((t, d), F32),
        compiler_params=_cparams(("parallel",)),
        name="nsa_out",
    )(oc, osel, ow, gate_logits, expand, w_o, gpost.reshape(1, d), h)


def _pad_cols(w, n):
    return jnp.pad(w, ((0, 0), (0, n - w.shape[1])))


def _importance_matrix(nc, nsel):
    r = SEL_BLOCK // CMP_STRIDE
    c = CMP_BLOCK // CMP_STRIDE
    mat = np.zeros((nc, nsel), np.float32)
    for kblk in range(nsel):
        for m in range(r):
            for n in range(c):
                j = r * kblk + m - n
                if 0 <= j < nc - 1:
                    mat[j, kblk] += 1.0
    return jnp.asarray(mat, BF16)


def _gate_expand():
    e = np.zeros((3, LANES, NSA_GROUPS * NSA_GW), np.float32)
    for head in range(NSA_GROUPS * NSA_REP):
        for b in range(3):
            e[b, head * 3 + b, head * NSA_HEAD_DIM:(head + 1) * NSA_HEAD_DIM] = 1.0
    return jnp.asarray(e, BF16)


def _block_diag_kv(k_cmp, v_cmp):
    g, nc, dh = k_cmp.shape
    eye = jnp.eye(NSA_REP, dtype=bool)
    kct = jnp.swapaxes(k_cmp, 1, 2)
    kbd = jnp.where(eye[None, :, None, :, None], kct[:, None, :, None, :], 0.0)
    vbd = jnp.where(eye[None, :, None, :, None], v_cmp[:, None, :, None, :], 0.0)
    return (kbd.reshape(g, NSA_REP * dh, NSA_REP * nc).astype(BF16),
            vbd.reshape(g, NSA_REP * nc, NSA_REP * dh).astype(BF16))


def kernel(x, p, mix_pre_norm, mix_post_norm, ffn_pre_norm, ffn_post_norm, gdn_w_in, gdn_conv_w, gdn_a_log,
           gdn_dt_bias, gdn_o_norm, gdn_w_out, kv_norm, kv_w, cmp_pos, cmp_w1, cmp_w2, nsa_w_qg, nsa_w_o,
           ffn_w_in, ffn_w_out, ple_w_in, ple_w_gate):
    depth = p.shape[0]
    n_a = gdn_w_in.shape[0]
    t = x.shape[1]
    h = x[0]
    fh = ffn_w_out.shape[1]
    conv_w_cols = gdn_conv_w.shape[2]
    vw = GDN_V_HEADS * GDN_HEAD_DIM
    main_w = conv_w_cols + vw

    def channel_and_ple(h, i):
        w_in = ffn_w_in[i].astype(BF16)
        return ffn_ple(h, ffn_pre_norm[i], w_in[:, :fh], w_in[:, fh:], ffn_w_out[i].astype(BF16),
                       ffn_post_norm[i], p[i, 0], ple_w_in[i].astype(BF16), ple_w_gate[i].astype(BF16))

    for i in range(n_a):
        w_in = gdn_w_in[i]
        w_beta = w_in[:, main_w:main_w + GDN_V_HEADS]
        w_a = w_in[:, main_w + GDN_V_HEADS:]
        w_small = _pad_cols(jnp.concatenate([w_beta, w_a, w_a], axis=1), LANES).astype(BF16)
        proj = norm_matmul(h, mix_pre_norm[i], w_in[:, :main_w].astype(BF16))
        ba = norm_matmul(h, mix_pre_norm[i], w_small)
        pad_vec = lambda v: jnp.pad(v, (GDN_V_HEADS, LANES - 2 * GDN_V_HEADS))
        alog_vec = (pad_vec(gdn_a_log[i]) + jnp.pad(gdn_a_log[i], (2 * GDN_V_HEADS, LANES - 3 * GDN_V_HEADS)))
        dtb_vec = (pad_vec(gdn_dt_bias[i]) + jnp.pad(gdn_dt_bias[i], (2 * GDN_V_HEADS, LANES - 3 * GDN_V_HEADS)))
        slab = gdn_gates(ba, alog_vec.reshape(1, LANES), dtb_vec.reshape(1, LANES))
        qkv_hm = gdn_conv(proj, gdn_conv_w[i])
        kt_hm = jnp.swapaxes(qkv_hm[GDN_QK_HEADS:2 * GDN_QK_HEADS], 1, 2)
        o = gdn_scan(qkv_hm, kt_hm, slab, slab.T)
        onorm_tiled = jnp.tile(gdn_o_norm[i], GDN_V_HEADS).reshape(1, vw)
        h = gdn_out(o, proj, onorm_tiled, gdn_w_out[i].astype(BF16), mix_post_norm[i], h)
        h = channel_and_ple(h, i)

    g = NSA_GROUPS
    dh = NSA_HEAD_DIM
    kv = norm_matmul(h, kv_norm, kv_w.astype(BF16), tn=768)
    kv6 = jnp.transpose(kv.reshape(t, 6, g, dh), (1, 2, 0, 3))
    nc = t // CMP_STRIDE
    nsel = t // SEL_BLOCK
    x2 = kv6[0:2].reshape(2, g, nc, CMP_STRIDE * dh)
    pos2 = cmp_pos.reshape(2, 2, CMP_STRIDE * dh)
    cmp_out = compress(x2, pos2, cmp_w1.astype(BF16), cmp_w2.astype(BF16))
    imp_mat_t = _importance_matrix(nc, nsel).T
    nq_r = t // (CMP_TQ * CMP_RANGES)
    assert t % (CMP_TQ * CMP_RANGES) == 0 and nc % CMP_RANGES == 0 and nsel % (8 * CMP_RANGES) == 0
    cmp_ranges = []
    for rg in range(CMP_RANGES):
        nc_r, nsel_r = (rg + 1) * nc // CMP_RANGES, (rg + 1) * nsel // CMP_RANGES
        vbd_t, kbd = _block_diag_kv(cmp_out[1][:, :nc_r], cmp_out[0][:, :nc_r])
        cmp_ranges.append((kbd, vbd_t, imp_mat_t[:nsel_r, :nc_r]))
    kt = FLASH_KT
    tiles = lambda a: a.reshape(g, t // kt, kt, dh)
    rep_k = lambda a: jnp.tile(tiles(a), (1, 1, 1, NSA_REP)).astype(BF16)
    rep_t = lambda a: jnp.tile(jnp.swapaxes(tiles(a), 2, 3), (1, 1, NSA_REP, 1)).astype(BF16)
    k_slc, v_slc_t = rep_k(kv6[2]), rep_t(kv6[3])
    k_win, v_win_t = rep_k(kv6[4]), rep_t(kv6[5])
    expand = _gate_expand()

    for i in range(n_a, depth):
        j = i - n_a
        qw = g * NSA_GW
        w_qg = nsa_w_qg[j]
        q_t = norm_matmul_t(h, mix_pre_norm[i], w_qg[:, :qw].T.astype(BF16))
        gate_logits = norm_matmul(h, mix_pre_norm[i], _pad_cols(w_qg[:, qw:], LANES).astype(BF16))
        parts = [cmp_topk(q_t, kbd, vbd_t, imp_r, nsel, min(SEL_TOPK, nsel), rg * nq_r, nq_r, CMP_TQ)
                 for rg, (kbd, vbd_t, imp_r) in enumerate(cmp_ranges)]
        o_c = jnp.concatenate([o for o, _ in parts], axis=0)
        sel_t = jnp.concatenate([sl for _, sl in parts], axis=2)
        o_s = flash_branch(q_t, k_slc, v_slc_t, sel_t, tq=1024, kt=kt)
        o_w = window_branch(q_t, k_win, v_win_t, tq=kt)
        h = nsa_out(o_c, o_s, o_w, gate_logits, expand, nsa_w_o[j].astype(BF16), mix_post_norm[i], h)
        h = channel_and_ple(h, i)
    return h[None]
```

```python
import functools

import numpy as np
import jax
import jax.numpy as jnp
from jax import lax
from jax.experimental import pallas as pl
from jax.experimental.pallas import tpu as pltpu

F32 = jnp.float32
BF16 = jnp.bfloat16

NORM_EPS = 1e-6
L2_EPS = 1e-6
GDN_QK_HEADS = 8
GDN_V_HEADS = 16
GDN_HEAD_DIM = 128
GDN_CONV = 4
GDN_CHUNK = 64
GDN_GROUP = 256
GDN_SCAN_HEADS = 4
GDN_CONV_HEADS = 4
NSA_GROUPS = 4
NSA_REP = 4
NSA_HEAD_DIM = 64
NSA_GW = NSA_REP * NSA_HEAD_DIM
CMP_BLOCK = 32
CMP_STRIDE = 16
SEL_BLOCK = 64
SEL_TOPK = 16
WINDOW = 512
FORCED_SCORE = 1e4
LANES = 128
NEG_INIT = -(2.0 ** 100)
LOG2_E = 1.4426950408889634
FLASH_KT = 256
FLASH_SUB = 256
FLASH_SUM_ROWS = 16
CMP_ROWS = 256
CMP_TQ = 256
CMP_RANGES = 4

VMEM_LIMIT = 56 * 1024 * 1024


def _cparams(sem):
    return pltpu.CompilerParams(dimension_semantics=sem, vmem_limit_bytes=VMEM_LIMIT)


def _rms(x, gain):
    return x * lax.rsqrt(jnp.mean(x * x, axis=-1, keepdims=True) + NORM_EPS) * gain


def _mm(a, b):
    return jnp.dot(a.astype(BF16), b.astype(BF16), preferred_element_type=F32)


def _sigmoid(x):
    return 1.0 / (1.0 + jnp.exp(-x))


def _div_pow2(x, d):
    shift = d.bit_length() - 1
    assert d == 1 << shift
    return jnp.right_shift(x, shift)


def _split3(x):
    a = x.astype(BF16)
    r = x - a.astype(F32)
    b = r.astype(BF16)
    c = (r - b.astype(F32)).astype(BF16)
    return a, b, c


def _norm_matmul_kernel(x_ref, g_ref, w_ref, o_ref, xn_ref):
    @pl.when(pl.program_id(1) == 0)
    def _():
        xn_ref[...] = _rms(x_ref[...], g_ref[...]).astype(BF16)

    o_ref[...] = jnp.dot(xn_ref[...], w_ref[...], preferred_element_type=F32)


def _norm_matmul_t_kernel(x_ref, g_ref, wt_ref, o_ref, xn_ref):
    @pl.when(pl.program_id(1) == 0)
    def _():
        xn_ref[...] = _rms(x_ref[...], g_ref[...]).astype(BF16)

    o_ref[...] = lax.dot_general(wt_ref[...], xn_ref[...], (((1,), (1,)), ((), ())), preferred_element_type=F32)


def norm_matmul_t(h, gain, w_t, tm=1024, tn=1024):
    t, d = h.shape
    n = w_t.shape[0]
    tn = min(tn, n)
    assert t % tm == 0 and n % tn == 0
    return pl.pallas_call(
        _norm_matmul_t_kernel,
        grid=(t // tm, n // tn),
        in_specs=[
            pl.BlockSpec((tm, d), lambda i, j: (i, 0)),
            pl.BlockSpec((1, d), lambda i, j: (0, 0)),
            pl.BlockSpec((tn, d), lambda i, j: (j, 0)),
        ],
        out_specs=pl.BlockSpec((tn, tm), lambda i, j: (j, i)),
        out_shape=jax.ShapeDtypeStruct((n, t), F32),
        scratch_shapes=[pltpu.VMEM((tm, d), BF16)],
        compiler_params=_cparams(("parallel", "arbitrary")),
        name="norm_matmul_t",
    )(h, gain.reshape(1, d), w_t)


def norm_matmul(h, gain, w, tm=1024, tn=1024):
    t, d = h.shape
    n = w.shape[1]
    tn = min(tn, n)
    assert t % tm == 0 and n % tn == 0
    return pl.pallas_call(
        _norm_matmul_kernel,
        grid=(t // tm, n // tn),
        in_specs=[
            pl.BlockSpec((tm, d), lambda i, j: (i, 0)),
            pl.BlockSpec((1, d), lambda i, j: (0, 0)),
            pl.BlockSpec((d, tn), lambda i, j: (0, j)),
        ],
        out_specs=pl.BlockSpec((tm, tn), lambda i, j: (i, j)),
        out_shape=jax.ShapeDtypeStruct((t, n), F32),
        scratch_shapes=[pltpu.VMEM((tm, d), BF16)],
        compiler_params=_cparams(("parallel", "arbitrary")),
        name="norm_matmul",
    )(h, gain.reshape(1, d), w)


def _ffn_ple_kernel(h_ref, gpre_ref, wg_ref, wu_ref, wo_ref, gpost_ref, p_ref, wple_ref, wgt_ref,
                    o_ref, xn_ref, acc_ref, *, nf):
    f = pl.program_id(1)

    @pl.when(f == 0)
    def _():
        xn_ref[...] = _rms(h_ref[...], gpre_ref[...]).astype(BF16)
        acc_ref[...] = jnp.zeros_like(acc_ref)

    xn = xn_ref[...]
    gate = jnp.dot(xn, wg_ref[...], preferred_element_type=F32)
    up = jnp.dot(xn, wu_ref[...], preferred_element_type=F32)
    act = gate * _sigmoid(gate) * up
    acc_ref[...] += jnp.dot(act.astype(BF16), wo_ref[...], preferred_element_type=F32)

    @pl.when(f == nf - 1)
    def _():
        h2 = h_ref[...] + _rms(acc_ref[...], gpost_ref[...])
        emb = jnp.dot(p_ref[...].astype(BF16), wple_ref[...], preferred_element_type=F32)
        gt = _sigmoid(jnp.dot(h2.astype(BF16), wgt_ref[...], preferred_element_type=F32))
        o_ref[...] = h2 + emb * gt


def ffn_ple(h, gpre, w_gate, w_up, w_out, gpost, p, w_ple, w_plegate, tm=512, tf=1408):
    t, d = h.shape
    fh = w_gate.shape[1]
    pd = p.shape[1]
    assert t % tm == 0 and fh % tf == 0
    nf = fh // tf
    return pl.pallas_call(
        functools.partial(_ffn_ple_kernel, nf=nf),
        grid=(t // tm, nf),
        in_specs=[
            pl.BlockSpec((tm, d), lambda i, f: (i, 0)),
            pl.BlockSpec((1, d), lambda i, f: (0, 0)),
            pl.BlockSpec((d, tf), lambda i, f: (0, f)),
            pl.BlockSpec((d, tf), lambda i, f: (0, f)),
            pl.BlockSpec((tf, d), lambda i, f: (f, 0)),
            pl.BlockSpec((1, d), lambda i, f: (0, 0)),
            pl.BlockSpec((tm, pd), lambda i, f: (i, 0)),
            pl.BlockSpec((pd, d), lambda i, f: (0, 0)),
            pl.BlockSpec((d, d), lambda i, f: (0, 0)),
        ],
        out_specs=pl.BlockSpec((tm, d), lambda i, f: (i, 0)),
        out_shape=jax.ShapeDtypeStruct((t, d), F32),
        scratch_shapes=[pltpu.VMEM((tm, d), BF16), pltpu.VMEM((tm, d), F32)],
        compiler_params=_cparams(("parallel", "arbitrary")),
        name="ffn_ple",
    )(h, gpre.reshape(1, d), w_gate, w_up, w_out, gpost.reshape(1, d), p, w_ple, w_plegate)


def _gdn_conv_kernel(x_ref, halo_ref, w_ref, o_ref, *, tm):
    c = pl.program_id(0)
    i = pl.program_id(1)
    x = x_ref[...]
    halo = jnp.where(i > 0, halo_ref[...], 0.0)
    ext = jnp.concatenate([halo, x], axis=0)
    w = w_ref[...]
    y = x * w[GDN_CONV - 1:GDN_CONV, :]
    for k in range(1, GDN_CONV):
        shifted = pltpu.roll(ext, k, axis=0)[8:8 + tm]
        y = y + shifted * w[GDN_CONV - 1 - k:GDN_CONV - k, :]
    y = y * _sigmoid(y)
    for hd in range(GDN_CONV_HEADS):
        head = c * GDN_CONV_HEADS + hd
        seg = y[:, hd * LANES:(hd + 1) * LANES]
        normed = seg * lax.rsqrt(jnp.sum(seg * seg, axis=-1, keepdims=True) + L2_EPS)
        q_scale = jnp.where(head < GDN_QK_HEADS, GDN_HEAD_DIM ** -0.5, 1.0)
        o_ref[hd] = jnp.where(head < 2 * GDN_QK_HEADS, normed * q_scale, seg)


def gdn_conv(proj, conv_w, tm=1024):
    t = proj.shape[0]
    n_tiles = conv_w.shape[1] // LANES
    cw = GDN_CONV_HEADS
    assert t % tm == 0 and n_tiles % cw == 0
    return pl.pallas_call(
        functools.partial(_gdn_conv_kernel, tm=tm),
        grid=(n_tiles // cw, t // tm),
        in_specs=[
            pl.BlockSpec((tm, cw * LANES), lambda c, i: (i, c)),
            pl.BlockSpec((8, cw * LANES), lambda c, i: (jnp.maximum(i * (tm // 8) - 1, 0), c)),
            pl.BlockSpec((GDN_CONV, cw * LANES), lambda c, i: (0, c)),
        ],
        out_specs=pl.BlockSpec((cw, tm, LANES), lambda c, i: (c, i, 0)),
        out_shape=jax.ShapeDtypeStruct((n_tiles, t, LANES), F32),
        compiler_params=_cparams(("parallel", "parallel")),
        name="gdn_conv",
    )(proj, proj, conv_w)


def _gdn_gate_kernel(x_ref, alog_ref, dtb_ref, lc_ref, lf_ref, o_ref):
    x = x_ref[...]
    lane = lax.broadcasted_iota(jnp.int32, x.shape, 1)
    beta = _sigmoid(x)
    z = x + dtb_ref[...]
    softplus = jnp.maximum(z, 0.0) + jnp.log(1.0 + jnp.exp(-jnp.abs(z)))
    g = -jnp.exp(alog_ref[...]) * softplus
    gcum = jnp.zeros_like(x)
    gtot = jnp.zeros_like(x)
    for piece in _split3(g):
        gcum = gcum + jnp.dot(lc_ref[...], piece, preferred_element_type=F32)
        gtot = gtot + jnp.dot(lf_ref[...], piece, preferred_element_type=F32)
    o_ref[...] = jnp.where(lane < GDN_V_HEADS, beta, jnp.where(lane < 2 * GDN_V_HEADS, gcum, gtot))


def gdn_gates(ba, alog_vec, dtb_vec):
    t = ba.shape[0]
    tm = GDN_GROUP
    r = np.arange(tm)
    same = (r[:, None] // GDN_CHUNK) == (r[None, :] // GDN_CHUNK)
    lc = jnp.asarray(same & (r[None, :] <= r[:, None]), BF16)
    lf = jnp.asarray(same, BF16)
    return pl.pallas_call(
        _gdn_gate_kernel,
        grid=(t // tm,),
        in_specs=[
            pl.BlockSpec((tm, LANES), lambda i: (i, 0)),
            pl.BlockSpec((1, LANES), lambda i: (0, 0)),
            pl.BlockSpec((1, LANES), lambda i: (0, 0)),
            pl.BlockSpec((tm, tm), lambda i: (0, 0)),
            pl.BlockSpec((tm, tm), lambda i: (0, 0)),
        ],
        out_specs=pl.BlockSpec((tm, LANES), lambda i: (i, 0)),
        out_shape=jax.ShapeDtypeStruct((t, LANES), F32),
        compiler_params=_cparams(("parallel",)),
        name="gdn_gates",
    )(ba, alog_vec, dtb_vec, lc, lf)


def _gdn_scan_kernel(q_ref, k_ref, kt_ref, v_ref, slab_ref, slabt_ref, o_ref, s_ref):
    n = pl.program_id(1)

    @pl.when(n == 0)
    def _():
        s_ref[...] = jnp.zeros_like(s_ref)

    L = GDN_GROUP
    C = GDN_CHUNK
    D = GDN_HEAD_DIM
    slab = slab_ref[...]
    slabt = slabt_ref[...]
    row = lax.broadcasted_iota(jnp.int32, (L, L), 0)
    col = lax.broadcasted_iota(jnp.int32, (L, L), 1)
    same = _div_pow2(row, C) == _div_pow2(col, C)
    causal = same & (col <= row)
    strict = same & (col < row)
    eye = (row == col).astype(F32)
    lane = lax.broadcasted_iota(jnp.int32, (L, LANES), 1)
    sub = lax.broadcasted_iota(jnp.int32, (LANES, L), 0)

    def column(idx):
        return jnp.sum(jnp.where(lane == idx, slab, 0.0), axis=1, keepdims=True)

    def rowvec(idx):
        return jnp.sum(jnp.where(sub == idx, slabt, 0.0), axis=0, keepdims=True)

    colk = lax.broadcasted_iota(jnp.int32, (D, L), 1)
    slots = range(2 * GDN_SCAN_HEADS)

    q = [q_ref[hq] for hq in range(GDN_SCAN_HEADS)]
    k = [k_ref[hq] for hq in range(GDN_SCAN_HEADS)]
    kt = [kt_ref[hq] for hq in range(GDN_SCAN_HEADS)]
    kk = [_mm(k[hq], kt[hq]) for hq in range(GDN_SCAN_HEADS)]
    qk = [_mm(q[hq], kt[hq]) for hq in range(GDN_SCAN_HEADS)]
    beta_c, gc_c, gt_c, decay, bp, inv, kdt = [], [], [], [], [], [], []
    for slot in slots:
        hq = slot // 2
        hv = 2 * (pl.program_id(0) * GDN_SCAN_HEADS + hq) + slot % 2
        beta_c.append(column(hv))
        gc_c.append(column(GDN_V_HEADS + hv))
        gt_c.append(column(2 * GDN_V_HEADS + hv))
        gc_r = rowvec(GDN_V_HEADS + hv)
        gt_r = rowvec(2 * GDN_V_HEADS + hv)
        decay.append(jnp.where(causal, jnp.exp(jnp.where(causal, gc_c[slot] - gc_r, 0.0)), 0.0))
        bp.append(jnp.where(strict, -(kk[hq] * beta_c[slot]) * decay[slot], 0.0))
        inv.append(eye + bp[slot])
        kdt.append(kt[hq] * jnp.exp(gt_r - gc_r))
    for _ in range(5):
        bp = [_mm(bp[slot], bp[slot]) for slot in slots]
        inv = [inv[slot] + _mm(inv[slot], bp[slot]) for slot in slots]
    u, w, qkm, q_dec = [], [], [], []
    for slot in slots:
        hq = slot // 2
        egc = jnp.exp(gc_c[slot])
        rhs = jnp.concatenate([v_ref[slot] * beta_c[slot], k[hq] * (beta_c[slot] * egc)], axis=1)
        sol = _mm(inv[slot], rhs)
        u.append(sol[:, :D])
        w.append(sol[:, D:])
        qkm.append(jnp.where(causal, qk[hq] * decay[slot], 0.0))
        q_dec.append(q[hq] * egc)
    state = [s_ref[slot] for slot in slots]
    v_done = [[] for _ in slots]
    for c in range(L // C):
        lo, hi = c * C, (c + 1) * C
        r = [_mm(jnp.concatenate([w[slot][lo:hi], q_dec[slot][lo:hi]], axis=0), state[slot]) for slot in slots]
        for slot in slots:
            v_done[slot].append(u[slot][lo:hi] - r[slot][:C])
            v_all = jnp.concatenate(v_done[slot] + [jnp.zeros((L - hi, D), F32)] * (hi < L), axis=0)
            o_ref[lo:hi, slot * D:(slot + 1) * D] = r[slot][C:] + _mm(qkm[slot][lo:hi, :], v_all)
            kdt_c = jnp.where((colk >= lo) & (colk < hi), kdt[slot], 0.0)
            state[slot] = state[slot] * jnp.exp(gt_c[slot][lo:lo + 1, :]) + _mm(kdt_c, v_all)
    for slot in slots:
        s_ref[slot] = state[slot]


def gdn_scan(qkv_hm, kt_hm, slab, slabt):
    t = qkv_hm.shape[1]
    L = GDN_GROUP
    D = GDN_HEAD_DIM
    hq = GDN_SCAN_HEADS
    assert t % L == 0 and GDN_QK_HEADS % hq == 0
    q_blocks = GDN_QK_HEADS // hq
    return pl.pallas_call(
        _gdn_scan_kernel,
        grid=(q_blocks, t // L),
        in_specs=[
            pl.BlockSpec((hq, L, D), lambda j, n: (j, n, 0)),
            pl.BlockSpec((hq, L, D), lambda j, n: (q_blocks + j, n, 0)),
            pl.BlockSpec((hq, D, L), lambda j, n: (j, 0, n)),
            pl.BlockSpec((2 * hq, L, D), lambda j, n: (q_blocks + j, n, 0)),
            pl.BlockSpec((L, LANES), lambda j, n: (n, 0)),
            pl.BlockSpec((LANES, L), lambda j, n: (0, n)),
        ],
        out_specs=pl.BlockSpec((L, 2 * hq * D), lambda j, n: (n, j)),
        out_shape=jax.ShapeDtypeStruct((t, GDN_V_HEADS * D), F32),
        scratch_shapes=[pltpu.VMEM((2 * hq, D, D), F32)],
        compiler_params=_cparams(("parallel", "arbitrary")),
        name="gdn_scan",
    )(qkv_hm, qkv_hm, kt_hm, qkv_hm, slab, slabt)


def _gdn_out_kernel(o_ref, z_ref, onorm_ref, w_ref, gpost_ref, h_ref, out_ref):
    o = o_ref[...]
    z = z_ref[...]
    parts = []
    for hd in range(GDN_V_HEADS):
        seg = o[:, hd * GDN_HEAD_DIM:(hd + 1) * GDN_HEAD_DIM]
        parts.append(seg * lax.rsqrt(jnp.mean(seg * seg, axis=-1, keepdims=True) + NORM_EPS))
    gated = jnp.concatenate(parts, axis=1) * onorm_ref[...] * (z * _sigmoid(z))
    mix = jnp.dot(gated.astype(BF16), w_ref[...], preferred_element_type=F32)
    out_ref[...] = h_ref[...] + _rms(mix, gpost_ref[...])


def gdn_out(o, proj, onorm_tiled, w_out, gpost, h, tm=512):
    t, vw = o.shape
    d = h.shape[1]
    z_blk = (proj.shape[1] - vw) // vw
    assert proj.shape[1] % vw == 0 and t % tm == 0
    return pl.pallas_call(
        _gdn_out_kernel,
        grid=(t // tm,),
        in_specs=[
            pl.BlockSpec((tm, vw), lambda i: (i, 0)),
            pl.BlockSpec((tm, vw), lambda i: (i, z_blk)),
            pl.BlockSpec((1, vw), lambda i: (0, 0)),
            pl.BlockSpec((vw, d), lambda i: (0, 0)),
            pl.BlockSpec((1, d), lambda i: (0, 0)),
            pl.BlockSpec((tm, d), lambda i: (i, 0)),
        ],
        out_specs=pl.BlockSpec((tm, d), lambda i: (i, 0)),
        out_shape=jax.ShapeDtypeStruct((t, d), F32),
        compiler_params=_cparams(("parallel",)),
        name="gdn_out",
    )(o, proj, onorm_tiled, w_out, gpost.reshape(1, d), h)


def _compress_kernel(x_ref, pos_ref, w1_ref, w2_ref, o_ref):
    x = x_ref[...]
    pos = pos_ref[...]
    nc, half = x.shape
    w1 = w1_ref[...]
    first = _mm(x + pos[0:1, :], w1[:half])
    second = _mm(x + pos[1:2, :], w1[half:])
    hid = first + pltpu.roll(second, nc - 1, axis=0)
    hid = hid * _sigmoid(hid)
    out = jnp.dot(hid.astype(BF16), w2_ref[...], preferred_element_type=F32)
    rowi = lax.broadcasted_iota(jnp.int32, out.shape, 0)
    o_ref[...] = jnp.where(rowi < nc - 1, out, 0.0)


def compress(x2, pos2, w1, w2):
    _, g, nc, wdt = x2.shape
    hid = w1.shape[2]
    dh = w2.shape[2]
    return pl.pallas_call(
        _compress_kernel,
        grid=(2, g),
        in_specs=[
            pl.BlockSpec((None, None, nc, wdt), lambda b, gi: (b, gi, 0, 0)),
            pl.BlockSpec((None, 2, wdt), lambda b, gi: (b, 0, 0)),
            pl.BlockSpec((None, 2 * wdt, hid), lambda b, gi: (b, 0, 0)),
            pl.BlockSpec((None, hid, dh), lambda b, gi: (b, 0, 0)),
        ],
        out_specs=pl.BlockSpec((None, None, nc, dh), lambda b, gi: (b, gi, 0, 0)),
        out_shape=jax.ShapeDtypeStruct((2, g, nc, dh), F32),
        compiler_params=_cparams(("parallel", "parallel")),
        name="nsa_compress",
    )(x2, pos2, w1, w2)


def _cmp_topk_kernel(q_ref, kbd_ref, vt_ref, m_ref, oc_ref, sel_ref, s_ref, p_ref, psum_ref, *,
                     tq, nc, nsel, topk, q0):
    i = q0 + pl.program_id(1)
    dh = NSA_HEAD_DIM
    rows_per = min(nc, CMP_ROWS)
    q = (q_ref[...] * ((dh ** -0.5) * LOG2_E)).astype(BF16)
    s_ref[...] = jnp.dot(kbd_ref[...], q, preferred_element_type=F32)
    for ch in range(tq // LANES):
        lanes = slice(ch * LANES, (ch + 1) * LANES)
        tpos = i * tq + ch * LANES + lax.broadcasted_iota(jnp.int32, (rows_per, LANES), 1)
        cblk0 = lax.broadcasted_iota(jnp.int32, (rows_per, LANES), 0)
        masks = [(CMP_STRIDE * (cblk0 + c * rows_per) + CMP_BLOCK - 1) <= tpos for c in range(nc // rows_per)]
        for r in range(NSA_REP):
            pieces = [slice(r * nc + c * rows_per, r * nc + (c + 1) * rows_per) for c in range(nc // rows_per)]
            m = jnp.full((1, LANES), -jnp.inf, F32)
            for rows, mask in zip(pieces, masks):
                m = jnp.maximum(m, jnp.max(jnp.where(mask, s_ref[rows, lanes], -jnp.inf), axis=0, keepdims=True))
            m = jnp.where(m > -jnp.inf, m, 0.0)
            total = jnp.zeros((1, LANES), F32)
            for rows, mask in zip(pieces, masks):
                e = jnp.exp2(jnp.where(mask, s_ref[rows, lanes], -jnp.inf) - m)
                s_ref[rows, lanes] = e
                total = total + jnp.sum(e, axis=0, keepdims=True)
            inv = 1.0 / jnp.maximum(total, 1e-30)
            for c, rows in enumerate(pieces):
                p = s_ref[rows, lanes] * inv
                p_ref[rows, lanes] = p.astype(BF16)
                prow = slice(c * rows_per, (c + 1) * rows_per)
                psum_ref[prow, lanes] = p if r == 0 else psum_ref[prow, lanes] + p
    oc_t = jnp.dot(vt_ref[...], p_ref[...], preferred_element_type=F32)
    imp = jnp.zeros((nsel, tq), F32)
    for piece in _split3(psum_ref[...]):
        imp = imp + jnp.dot(m_ref[...], piece, preferred_element_type=F32)
    for ch in range(tq // LANES):
        lanes = slice(ch * LANES, (ch + 1) * LANES)
        oc_ref[lanes, :] = oc_t[:, lanes].T
        t1 = i * tq + ch * LANES + lax.broadcasted_iota(jnp.int32, (nsel, LANES), 1)
        blk = lax.broadcasted_iota(jnp.int32, (nsel, LANES), 0)
        cur = _div_pow2(t1, SEL_BLOCK)
        forced = (blk == 0) | (blk == cur) | (blk == cur - 1)
        valid = blk * SEL_BLOCK <= t1
        score = jnp.where(valid, jnp.where(forced, FORCED_SCORE, imp[:, lanes]), -jnp.inf)
        blkf = blk.astype(F32)
        work = score
        for _ in range(topk):
            mx = jnp.max(work, axis=0, keepdims=True)
            first = jnp.min(jnp.where(work == mx, blkf, float(nsel)), axis=0, keepdims=True)
            work = jnp.where(blkf == first, -jnp.inf, work)
        sel_ref[:nsel, lanes] = jnp.where((score > -jnp.inf) & (work == -jnp.inf), 1.0, 0.0).astype(sel_ref.dtype)
        if sel_ref.shape[0] > nsel:
            sel_ref[nsel:, lanes] = jnp.zeros((sel_ref.shape[0] - nsel, LANES), sel_ref.dtype)


def cmp_topk(q_t, kbd, v_t, imp_mat_t, nsel_all, topk, q0, nq, tq):
    qw = q_t.shape[0]
    g = kbd.shape[0]
    nc = kbd.shape[1] // NSA_REP
    nsel = imp_mat_t.shape[0]
    assert nc % min(nc, CMP_ROWS) == 0
    return pl.pallas_call(
        functools.partial(_cmp_topk_kernel, tq=tq, nc=nc, nsel=nsel, topk=topk, q0=q0),
        grid=(g, nq),
        in_specs=[
            pl.BlockSpec((NSA_GW, tq), lambda gi, i: (gi, q0 + i)),
            pl.BlockSpec((None, NSA_REP * nc, NSA_GW), lambda gi, i: (gi, 0, 0)),
            pl.BlockSpec((None, NSA_GW, NSA_REP * nc), lambda gi, i: (gi, 0, 0)),
            pl.BlockSpec((nsel, nc), lambda gi, i: (0, 0)),
        ],
        out_specs=[
            pl.BlockSpec((tq, NSA_GW), lambda gi, i: (i, gi)),
            pl.BlockSpec((None, nsel_all, tq), lambda gi, i: (gi, 0, i)),
        ],
        out_shape=[
            jax.ShapeDtypeStruct((nq * tq, qw), F32),
            jax.ShapeDtypeStruct((g, nsel_all, nq * tq), F32),
        ],
        scratch_shapes=[
            pltpu.VMEM((NSA_REP * nc, tq), F32),
            pltpu.VMEM((NSA_REP * nc, tq), BF16),
            pltpu.VMEM((nc, tq), F32),
        ],
        compiler_params=_cparams(("parallel", "parallel")),
        name="nsa_cmp_topk",
    )(q_t, kbd, v_t, imp_mat_t)


def _flash_kernel(q_ref, k_ref, v_ref, sel_ref, o_ref,
                  m_ref, l_ref, alpha_ref, acc_ref, s_ref, p_ref, bias_ref, kbd_ref, vbd_ref, qs_ref, *, tq, kt):
    qi = pl.program_id(1)
    dh = NSA_HEAD_DIM
    n_sub = tq // FLASH_SUB
    n_blk = kt // SEL_BLOCK

    m_ref[...] = jnp.full_like(m_ref, NEG_INIT)
    l_ref[...] = jnp.zeros_like(l_ref)
    acc_ref[...] = jnp.zeros_like(acc_ref)
    qs_ref[...] = (q_ref[...] * ((dh ** -0.5) * LOG2_E)).astype(BF16)
    one_row = lax.broadcasted_iota(jnp.int32, (FLASH_SUM_ROWS, NSA_REP * kt), 0)
    one_head = _div_pow2(lax.broadcasted_iota(jnp.int32, (FLASH_SUM_ROWS, NSA_REP * kt), 1), kt)
    vbd_ref[NSA_GW:, :] = (one_row == one_head).astype(BF16)

    def key_tile(ki, positional, first_sub=0):
        k4 = k_ref[ki]
        kseg = _div_pow2(lax.broadcasted_iota(jnp.int32, k4.shape, 1), dh)
        v4 = v_ref[ki]
        vblk = _div_pow2(lax.broadcasted_iota(jnp.int32, v4.shape, 0), dh)
        for r in range(NSA_REP):
            kbd_ref[r * kt:(r + 1) * kt, :] = jnp.where(kseg == r, k4, jnp.zeros_like(k4))
            vbd_ref[:NSA_GW, r * kt:(r + 1) * kt] = jnp.where(vblk == r, v4, jnp.zeros_like(v4))

        def scores(sub):
            c0 = sub * FLASH_SUB
            cols = slice(c0, c0 + FLASH_SUB)
            s_ref[:, cols] = jnp.dot(kbd_ref[...], qs_ref[:, cols],
                                     preferred_element_type=F32)
            for jb in range(n_blk):
                rows = slice(jb * SEL_BLOCK, (jb + 1) * SEL_BLOCK)
                picked = sel_ref[pl.ds(ki * n_blk + jb, 1), cols] > 0.5
                if positional:
                    tpos = qi * tq + c0 + lax.broadcasted_iota(jnp.int32, (SEL_BLOCK, FLASH_SUB), 1)
                    kpos = ki * kt + jb * SEL_BLOCK + lax.broadcasted_iota(jnp.int32, (SEL_BLOCK, FLASH_SUB), 0)
                    allowed = picked & (kpos <= tpos)
                else:
                    allowed = jnp.broadcast_to(picked, (SEL_BLOCK, FLASH_SUB))
                bias_ref[rows, cols] = jnp.where(allowed, 0.0, -jnp.inf)

        scores(first_sub)
        for sub in range(first_sub, n_sub):
            c0 = sub * FLASH_SUB
            cols = slice(c0, c0 + FLASH_SUB)
            if sub + 1 < n_sub:
                scores(sub + 1)
            for ch in range(FLASH_SUB // LANES):
                lanes = slice(c0 + ch * LANES, c0 + (ch + 1) * LANES)
                bias = bias_ref[:, lanes]
                for r in range(NSA_REP):
                    x = s_ref[r * kt:(r + 1) * kt, lanes] + bias
                    m_prev = m_ref[r:r + 1, lanes]
                    m_new = jnp.maximum(m_prev, jnp.max(x, axis=0, keepdims=True))
                    m_ref[r:r + 1, lanes] = m_new
                    alpha_ref[r:r + 1, lanes] = jnp.exp2(m_prev - m_new)
                    p_ref[r * kt:(r + 1) * kt, lanes] = jnp.exp2(x - m_new).astype(BF16)
            pv = jnp.dot(vbd_ref[...], p_ref[:, cols], preferred_element_type=F32)
            for r in range(NSA_REP):
                hd = slice(r * dh, (r + 1) * dh)
                alpha = alpha_ref[r:r + 1, cols]
                acc_ref[hd, cols] = acc_ref[hd, cols] * alpha + pv[hd]
                l_ref[r:r + 1, cols] = l_ref[r:r + 1, cols] * alpha + pv[NSA_GW + r:NSA_GW + r + 1]

    def before_diagonal(ki, carry):
        key_tile(ki, False)
        return carry

    diag = qi * (tq // kt)
    lax.fori_loop(0, diag, before_diagonal, 0)
    for d in range(tq // kt):
        key_tile(diag + d, True, first_sub=d * kt // FLASH_SUB)

    for ch in range(tq // LANES):
        lanes = slice(ch * LANES, (ch + 1) * LANES)
        out_t = jnp.concatenate(
            [acc_ref[r * dh:(r + 1) * dh, lanes] / l_ref[r:r + 1, lanes] for r in range(NSA_REP)], axis=0)
        o_ref[lanes, :] = out_t.T


def flash_branch(q_t, k4, vt4, sel_t, tq, kt):
    qw, t = q_t.shape
    g, nk = k4.shape[:2]
    nsel = sel_t.shape[1]
    assert t % tq == 0 and tq % kt == 0 and kt % SEL_BLOCK == 0 and tq % FLASH_SUB == 0 and kt % FLASH_SUB == 0
    return pl.pallas_call(
        functools.partial(_flash_kernel, tq=tq, kt=kt),
        grid=(g, t // tq),
        in_specs=[
            pl.BlockSpec((NSA_GW, tq), lambda gi, i: (gi, i)),
            pl.BlockSpec((None, nk, kt, NSA_GW), lambda gi, i: (gi, 0, 0, 0)),
            pl.BlockSpec((None, nk, NSA_GW, kt), lambda gi, i: (gi, 0, 0, 0)),
            pl.BlockSpec((None, nsel, tq), lambda gi, i: (gi, 0, i)),
        ],
        out_specs=pl.BlockSpec((tq, NSA_GW), lambda gi, i: (i, gi)),
        scratch_shapes=[
            pltpu.VMEM((8, tq), F32),
            pltpu.VMEM((8, tq), F32),
            pltpu.VMEM((8, tq), F32),
            pltpu.VMEM((NSA_GW, tq), F32),
            pltpu.VMEM((NSA_REP * kt, tq), F32),
            pltpu.VMEM((NSA_REP * kt, tq), BF16),
            pltpu.VMEM((kt, tq), F32),
            pltpu.VMEM((NSA_REP * kt, NSA_GW), BF16),
            pltpu.VMEM((NSA_GW + FLASH_SUM_ROWS, NSA_REP * kt), BF16),
            pltpu.VMEM((NSA_GW, tq), BF16),
        ],
        out_shape=jax.ShapeDtypeStruct((t, qw), F32),
        compiler_params=_cparams(("parallel", "arbitrary")),
        name="nsa_selected",
    )(q_t, k4, vt4, sel_t)


def _window_kernel(q_ref, k_ref, v_ref, o_ref, s_ref, p_ref, bias_ref, kbd_ref, vbd_ref, *, tq, kt):
    i = pl.program_id(1)
    dh = NSA_HEAD_DIM
    nw = WINDOW // kt + 1
    span = nw * kt
    qs = (q_ref[...] * ((dh ** -0.5) * LOG2_E)).astype(BF16)
    for w in range(nw):
        tile = i - (nw - 1) + w
        k4 = k_ref[jnp.maximum(tile, 0)]
        kseg = _div_pow2(lax.broadcasted_iota(jnp.int32, k4.shape, 1), dh)
        v4 = v_ref[jnp.maximum(tile, 0)]
        vblk = _div_pow2(lax.broadcasted_iota(jnp.int32, v4.shape, 0), dh)
        for r in range(NSA_REP):
            at = r * span + w * kt
            kbd_ref[at:at + kt, :] = jnp.where(kseg == r, k4, jnp.zeros_like(k4))
            vbd_ref[:NSA_GW, at:at + kt] = jnp.where(vblk == r, v4, jnp.zeros_like(v4))
        tpos = i * tq + lax.broadcasted_iota(jnp.int32, (kt, tq), 1)
        kpos = tile * kt + lax.broadcasted_iota(jnp.int32, (kt, tq), 0)
        allowed = (kpos >= 0) & (kpos <= tpos) & (kpos > tpos - WINDOW)
        bias_ref[w * kt:(w + 1) * kt, :] = jnp.where(allowed, 0.0, -jnp.inf)
    one_row = lax.broadcasted_iota(jnp.int32, (FLASH_SUM_ROWS, NSA_REP * span), 0)
    one_col = lax.broadcasted_iota(jnp.int32, (FLASH_SUM_ROWS, NSA_REP * span), 1)
    vbd_ref[NSA_GW:, :] = ((one_col >= one_row * span) & (one_col < (one_row + 1) * span)).astype(BF16)

    s_ref[...] = jnp.dot(kbd_ref[...], qs, preferred_element_type=F32)
    for ch in range(tq // LANES):
        lanes = slice(ch * LANES, (ch + 1) * LANES)
        for r in range(NSA_REP):
            m = jnp.full((1, LANES), -jnp.inf, F32)
            for w in range(nw):
                rows = slice(r * span + w * kt, r * span + (w + 1) * kt)
                m = jnp.maximum(m, jnp.max(s_ref[rows, lanes] + bias_ref[w * kt:(w + 1) * kt, lanes],
                                           axis=0, keepdims=True))
            m = jnp.where(m > -jnp.inf, m, 0.0)
            for w in range(nw):
                rows = slice(r * span + w * kt, r * span + (w + 1) * kt)
                x = s_ref[rows, lanes] + bias_ref[w * kt:(w + 1) * kt, lanes]
                p_ref[rows, lanes] = jnp.exp2(x - m).astype(BF16)
    pv = jnp.dot(vbd_ref[...], p_ref[...], preferred_element_type=F32)
    for ch in range(tq // LANES):
        lanes = slice(ch * LANES, (ch + 1) * LANES)
        out_t = jnp.concatenate(
            [pv[r * dh:(r + 1) * dh, lanes] / jnp.maximum(pv[NSA_GW + r:NSA_GW + r + 1, lanes], 1e-30)
             for r in range(NSA_REP)], axis=0)
        o_ref[lanes, :] = out_t.T


def window_branch(q_t, k4, vt4, tq):
    qw, t = q_t.shape
    g, nk, kt, _ = k4.shape
    assert tq == kt and WINDOW % kt == 0 and t % tq == 0
    span = (WINDOW // kt + 1) * kt
    return pl.pallas_call(
        functools.partial(_window_kernel, tq=tq, kt=kt),
        grid=(g, t // tq),
        in_specs=[
            pl.BlockSpec((NSA_GW, tq), lambda gi, i: (gi, i)),
            pl.BlockSpec((None, nk, kt, NSA_GW), lambda gi, i: (gi, 0, 0, 0)),
            pl.BlockSpec((None, nk, NSA_GW, kt), lambda gi, i: (gi, 0, 0, 0)),
        ],
        out_specs=pl.BlockSpec((tq, NSA_GW), lambda gi, i: (i, gi)),
        scratch_shapes=[
            pltpu.VMEM((NSA_REP * span, tq), F32),
            pltpu.VMEM((NSA_REP * span, tq), BF16),
            pltpu.VMEM((span, tq), F32),
            pltpu.VMEM((NSA_REP * span, NSA_GW), BF16),
            pltpu.VMEM((NSA_GW + FLASH_SUM_ROWS, NSA_REP * span), BF16),
        ],
        out_shape=jax.ShapeDtypeStruct((t, qw), F32),
        compiler_params=_cparams(("parallel", "parallel")),
        name="nsa_window",
    )(q_t, k4, vt4)


def _nsa_out_kernel(oc_ref, os_ref, ow_ref, gl_ref, eg_ref, w_ref, gpost_ref, h_ref, out_ref):
    pieces = _split3(_sigmoid(gl_ref[...]))
    mixed = jnp.zeros(oc_ref.shape, F32)
    for b, br_ref in enumerate((oc_ref, os_ref, ow_ref)):
        gfull = jnp.zeros(oc_ref.shape, F32)
        for piece in pieces:
            gfull = gfull + jnp.dot(piece, eg_ref[b], preferred_element_type=F32)
        mixed = mixed + gfull * br_ref[...]
    mix = jnp.dot(mixed.astype(BF16), w_ref[...], preferred_element_type=F32)
    out_ref[...] = h_ref[...] + _rms(mix, gpost_ref[...])


def nsa_out(oc, osel, ow, gate_logits, expand, w_o, gpost, h, tm=512):
    t, qw = oc.shape
    d = h.shape[1]
    assert t % tm == 0
    row = lambda w: pl.BlockSpec((tm, w), lambda i: (i, 0))
    return pl.pallas_call(
        _nsa_out_kernel,
        grid=(t // tm,),
        in_specs=[
            row(qw), row(qw), row(qw), row(LANES),
            pl.BlockSpec((3, LANES, qw), lambda i: (0, 0, 0)),
            pl.BlockSpec((qw, d), lambda i: (0, 0)),
            pl.BlockSpec((1, d), lambda i: (0, 0)),
            row(d),
        ],
        out_specs=row(d),
        out_shape=jax.ShapeDtypeStruct((t, d), F32),
        compiler_params=_cparams(("parallel",)),
        name="nsa_out",
    )(oc, osel, ow, gate_logits, expand, w_o, gpost.reshape(1, d), h)


def _pad_cols(w, n):
    return jnp.pad(w, ((0, 0), (0, n - w.shape[1])))


def _importance_matrix(nc, nsel):
    r = SEL_BLOCK // CMP_STRIDE
    c = CMP_BLOCK // CMP_STRIDE
    mat = np.zeros((nc, nsel), np.float32)
    for kblk in range(nsel):
        for m in range(r):
            for n in range(c):
                j = r * kblk + m - n
                if 0 <= j < nc - 1:
                    mat[j, kblk] += 1.0
    return jnp.asarray(mat, BF16)


def _gate_expand():
    e = np.zeros((3, LANES, NSA_GROUPS * NSA_GW), np.float32)
    for head in range(NSA_GROUPS * NSA_REP):
        for b in range(3):
            e[b, head * 3 + b, head * NSA_HEAD_DIM:(head + 1) * NSA_HEAD_DIM] = 1.0
    return jnp.asarray(e, BF16)


def _block_diag_kv(k_cmp, v_cmp):
    g, nc, dh = k_cmp.shape
    eye = jnp.eye(NSA_REP, dtype=bool)
    kct = jnp.swapaxes(k_cmp, 1, 2)
    kbd = jnp.where(eye[None, :, None, :, None], kct[:, None, :, None, :], 0.0)
    vbd = jnp.where(eye[None, :, None, :, None], v_cmp[:, None, :, None, :], 0.0)
    return (kbd.reshape(g, NSA_REP * dh, NSA_REP * nc).astype(BF16),
            vbd.reshape(g, NSA_REP * nc, NSA_REP * dh).astype(BF16))


def kernel(x, p, mix_pre_norm, mix_post_norm, ffn_pre_norm, ffn_post_norm, gdn_w_in, gdn_conv_w, gdn_a_log,
           gdn_dt_bias, gdn_o_norm, gdn_w_out, kv_norm, kv_w, cmp_pos, cmp_w1, cmp_w2, nsa_w_qg, nsa_w_o,
           ffn_w_in, ffn_w_out, ple_w_in, ple_w_gate):
    depth = p.shape[0]
    n_a = gdn_w_in.shape[0]
    t = x.shape[1]
    h = x[0]
    fh = ffn_w_out.shape[1]
    conv_w_cols = gdn_conv_w.shape[2]
    vw = GDN_V_HEADS * GDN_HEAD_DIM
    main_w = conv_w_cols + vw

    def channel_and_ple(h, i):
        w_in = ffn_w_in[i].astype(BF16)
        return ffn_ple(h, ffn_pre_norm[i], w_in[:, :fh], w_in[:, fh:], ffn_w_out[i].astype(BF16),
                       ffn_post_norm[i], p[i, 0], ple_w_in[i].astype(BF16), ple_w_gate[i].astype(BF16))

    for i in range(n_a):
        w_in = gdn_w_in[i]
        w_beta = w_in[:, main_w:main_w + GDN_V_HEADS]
        w_a = w_in[:, main_w + GDN_V_HEADS:]
        w_small = _pad_cols(jnp.concatenate([w_beta, w_a, w_a], axis=1), LANES).astype(BF16)
        proj = norm_matmul(h, mix_pre_norm[i], w_in[:, :main_w].astype(BF16))
        ba = norm_matmul(h, mix_pre_norm[i], w_small)
        pad_vec = lambda v: jnp.pad(v, (GDN_V_HEADS, LANES - 2 * GDN_V_HEADS))
        alog_vec = (pad_vec(gdn_a_log[i]) + jnp.pad(gdn_a_log[i], (2 * GDN_V_HEADS, LANES - 3 * GDN_V_HEADS)))
        dtb_vec = (pad_vec(gdn_dt_bias[i]) + jnp.pad(gdn_dt_bias[i], (2 * GDN_V_HEADS, LANES - 3 * GDN_V_HEADS)))
        slab = gdn_gates(ba, alog_vec.reshape(1, LANES), dtb_vec.reshape(1, LANES))
        qkv_hm = gdn_conv(proj, gdn_conv_w[i])
        kt_hm = jnp.swapaxes(qkv_hm[GDN_QK_HEADS:2 * GDN_QK_HEADS], 1, 2)
        o = gdn_scan(qkv_hm, kt_hm, slab, slab.T)
        onorm_tiled = jnp.tile(gdn_o_norm[i], GDN_V_HEADS).reshape(1, vw)
        h = gdn_out(o, proj, onorm_tiled, gdn_w_out[i].astype(BF16), mix_post_norm[i], h)
        h = channel_and_ple(h, i)

    g = NSA_GROUPS
    dh = NSA_HEAD_DIM
    kv = norm_matmul(h, kv_norm, kv_w.astype(BF16), tn=768)
    kv6 = jnp.transpose(kv.reshape(t, 6, g, dh), (1, 2, 0, 3))
    nc = t // CMP_STRIDE
    nsel = t // SEL_BLOCK
    x2 = kv6[0:2].reshape(2, g, nc, CMP_STRIDE * dh)
    pos2 = cmp_pos.reshape(2, 2, CMP_STRIDE * dh)
    cmp_out = compress(x2, pos2, cmp_w1.astype(BF16), cmp_w2.astype(BF16))
    imp_mat_t = _importance_matrix(nc, nsel).T
    nq_r = t // (CMP_TQ * CMP_RANGES)
    assert t % (CMP_TQ * CMP_RANGES) == 0 and nc % CMP_RANGES == 0 and nsel % (8 * CMP_RANGES) == 0
    cmp_ranges = []
    for rg in range(CMP_RANGES):
        nc_r, nsel_r = (rg + 1) * nc // CMP_RANGES, (rg + 1) * nsel // CMP_RANGES
        vbd_t, kbd = _block_diag_kv(cmp_out[1][:, :nc_r], cmp_out[0][:, :nc_r])
        cmp_ranges.append((kbd, vbd_t, imp_mat_t[:nsel_r, :nc_r]))
    kt = FLASH_KT
    tiles = lambda a: a.reshape(g, t // kt, kt, dh)
    rep_k = lambda a: jnp.tile(tiles(a), (1, 1, 1, NSA_REP)).astype(BF16)
    rep_t = lambda a: jnp.tile(jnp.swapaxes(tiles(a), 2, 3), (1, 1, NSA_REP, 1)).astype(BF16)
    k_slc, v_slc_t = rep_k(kv6[2]), rep_t(kv6[3])
    k_win, v_win_t = rep_k(kv6[4]), rep_t(kv6[5])
    expand = _gate_expand()

    for i in range(n_a, depth):
        j = i - n_a
        qw = g * NSA_GW
        w_qg = nsa_w_qg[j]
        q_t = norm_matmul_t(h, mix_pre_norm[i], w_qg[:, :qw].T.astype(BF16))
        gate_logits = norm_matmul(h, mix_pre_norm[i], _pad_cols(w_qg[:, qw:], LANES).astype(BF16))
        parts = [cmp_topk(q_t, kbd, vbd_t, imp_r, nsel, min(SEL_TOPK, nsel), rg * nq_r, nq_r, CMP_TQ)
                 for rg, (kbd, vbd_t, imp_r) in enumerate(cmp_ranges)]
        o_c = jnp.concatenate([o for o, _ in parts], axis=0)
        sel_t = jnp.concatenate([sl for _, sl in parts], axis=2)
        o_s = flash_branch(q_t, k_slc, v_slc_t, sel_t, tq=1024, kt=kt)
        o_w = window_branch(q_t, k_win, v_win_t, tq=kt)
        h = nsa_out(o_c, o_s, o_w, gate_logits, expand, nsa_w_o[j].astype(BF16), mix_post_norm[i], h)
        h = channel_and_ple(h, i)
    return h[None]
```

```python
import functools

import numpy as np
import jax
import jax.numpy as jnp
from jax import lax
from jax.experimental import pallas as pl
from jax.experimental.pallas import tpu as pltpu

F32 = jnp.float32
BF16 = jnp.bfloat16

NORM_EPS = 1e-6
L2_EPS = 1e-6
GDN_QK_HEADS = 8
GDN_V_HEADS = 16
GDN_HEAD_DIM = 128
GDN_CONV = 4
GDN_CHUNK = 64
GDN_GROUP = 256
GDN_SCAN_HEADS = 4
GDN_CONV_HEADS = 4
NSA_GROUPS = 4
NSA_REP = 4
NSA_HEAD_DIM = 64
NSA_GW = NSA_REP * NSA_HEAD_DIM
CMP_BLOCK = 32
CMP_STRIDE = 16
SEL_BLOCK = 64
SEL_TOPK = 16
WINDOW = 512
FORCED_SCORE = 1e4
LANES = 128
NEG_INIT = -(2.0 ** 100)
LOG2_E = 1.4426950408889634
FLASH_TQ = 2048
FLASH_KT = 256
FLASH_SUB = 256
FLASH_SUM_ROWS = 16
CMP_ROWS = 256
CMP_TQ = 256
CMP_RANGES = 4

VMEM_LIMIT = 56 * 1024 * 1024
FFN_VMEM_LIMIT = 60 * 1024 * 1024


def _cparams(sem, vmem_limit=VMEM_LIMIT):
    return pltpu.CompilerParams(dimension_semantics=sem, vmem_limit_bytes=vmem_limit)


def _rms(x, gain):
    return x * lax.rsqrt(jnp.mean(x * x, axis=-1, keepdims=True) + NORM_EPS) * gain


def _mm(a, b):
    return jnp.dot(a.astype(BF16), b.astype(BF16), preferred_element_type=F32)


def _sigmoid(x):
    return 1.0 / (1.0 + jnp.exp(-x))


def _div_pow2(x, d):
    shift = d.bit_length() - 1
    assert d == 1 << shift
    return jnp.right_shift(x, shift)


def _split3(x):
    a = x.astype(BF16)
    r = x - a.astype(F32)
    b = r.astype(BF16)
    c = (r - b.astype(F32)).astype(BF16)
    return a, b, c


def _norm_matmul_kernel(x_ref, g_ref, w_ref, o_ref, xn_ref):
    @pl.when(pl.program_id(1) == 0)
    def _():
        xn_ref[...] = _rms(x_ref[...], g_ref[...]).astype(BF16)

    o_ref[...] = jnp.dot(xn_ref[...], w_ref[...], preferred_element_type=F32)


def _norm_matmul_t_kernel(x_ref, g_ref, wt_ref, o_ref, xn_ref):
    @pl.when(pl.program_id(1) == 0)
    def _():
        xn_ref[...] = _rms(x_ref[...], g_ref[...]).astype(BF16)

    o_ref[...] = lax.dot_general(wt_ref[...], xn_ref[...], (((1,), (1,)), ((), ())), preferred_element_type=F32)


def norm_matmul_t(h, gain, w_t, tm=1024, tn=1024):
    t, d = h.shape
    n = w_t.shape[0]
    tn = min(tn, n)
    assert t % tm == 0 and n % tn == 0
    return pl.pallas_call(
        _norm_matmul_t_kernel,
        grid=(t // tm, n // tn),
        in_specs=[
            pl.BlockSpec((tm, d), lambda i, j: (i, 0)),
            pl.BlockSpec((1, d), lambda i, j: (0, 0)),
            pl.BlockSpec((tn, d), lambda i, j: (j, 0)),
        ],
        out_specs=pl.BlockSpec((tn, tm), lambda i, j: (j, i)),
        out_shape=jax.ShapeDtypeStruct((n, t), F32),
        scratch_shapes=[pltpu.VMEM((tm, d), BF16)],
        compiler_params=_cparams(("parallel", "arbitrary")),
        name="norm_matmul_t",
    )(h, gain.reshape(1, d), w_t)


def norm_matmul(h, gain, w, tm=1024, tn=1024):
    t, d = h.shape
    n = w.shape[1]
    tn = min(tn, n)
    assert t % tm == 0 and n % tn == 0
    return pl.pallas_call(
        _norm_matmul_kernel,
        grid=(t // tm, n // tn),
        in_specs=[
            pl.BlockSpec((tm, d), lambda i, j: (i, 0)),
            pl.BlockSpec((1, d), lambda i, j: (0, 0)),
            pl.BlockSpec((d, tn), lambda i, j: (0, j)),
        ],
        out_specs=pl.BlockSpec((tm, tn), lambda i, j: (i, j)),
        out_shape=jax.ShapeDtypeStruct((t, n), F32),
        scratch_shapes=[pltpu.VMEM((tm, d), BF16)],
        compiler_params=_cparams(("parallel", "arbitrary")),
        name="norm_matmul",
    )(h, gain.reshape(1, d), w)


def _ffn_ple_kernel(h_ref, gpre_ref, wg_ref, wu_ref, wo_ref, gpost_ref, p_ref, wple_ref, wgt_ref,
                    o_ref, xn_ref, acc_ref, *, nf):
    f = pl.program_id(1)

    @pl.when(f == 0)
    def _():
        xn_ref[...] = _rms(h_ref[...], gpre_ref[...]).astype(BF16)
        acc_ref[...] = jnp.zeros_like(acc_ref)

    xn = xn_ref[...]
    tf = wg_ref.shape[1]
    half = (tf // LANES // 2) * LANES
    for a, b in ((0, half), (half, tf)):
        gate = jnp.dot(xn, wg_ref[:, a:b], preferred_element_type=F32)
        up = jnp.dot(xn, wu_ref[:, a:b], preferred_element_type=F32)
        act = gate * _sigmoid(gate) * up
        acc_ref[...] += jnp.dot(act.astype(BF16), wo_ref[a:b, :], preferred_element_type=F32)

    @pl.when(f == nf - 1)
    def _():
        h2 = h_ref[...] + _rms(acc_ref[...], gpost_ref[...])
        emb = jnp.dot(p_ref[...].astype(BF16), wple_ref[...], preferred_element_type=F32)
        gt = _sigmoid(jnp.dot(h2.astype(BF16), wgt_ref[...], preferred_element_type=F32))
        o_ref[...] = h2 + emb * gt


def ffn_ple(h, gpre, w_gate, w_up, w_out, gpost, p, w_ple, w_plegate, tm=1024, tf=1408):
    t, d = h.shape
    fh = w_gate.shape[1]
    pd = p.shape[1]
    assert t % tm == 0 and fh % tf == 0
    nf = fh // tf
    return pl.pallas_call(
        functools.partial(_ffn_ple_kernel, nf=nf),
        grid=(t // tm, nf),
        in_specs=[
            pl.BlockSpec((tm, d), lambda i, f: (i, 0)),
            pl.BlockSpec((1, d), lambda i, f: (0, 0)),
            pl.BlockSpec((d, tf), lambda i, f: (0, f)),
            pl.BlockSpec((d, tf), lambda i, f: (0, f)),
            pl.BlockSpec((tf, d), lambda i, f: (f, 0)),
            pl.BlockSpec((1, d), lambda i, f: (0, 0)),
            pl.BlockSpec((tm, pd), lambda i, f: (i, 0)),
            pl.BlockSpec((pd, d), lambda i, f: (0, 0), pipeline_mode=pl.Buffered(1)),
            pl.BlockSpec((d, d), lambda i, f: (0, 0), pipeline_mode=pl.Buffered(1)),
        ],
        out_specs=pl.BlockSpec((tm, d), lambda i, f: (i, 0)),
        out_shape=jax.ShapeDtypeStruct((t, d), F32),
        scratch_shapes=[pltpu.VMEM((tm, d), BF16), pltpu.VMEM((tm, d), F32)],
        compiler_params=_cparams(("parallel", "arbitrary"), FFN_VMEM_LIMIT),
        name="ffn_ple",
    )(h, gpre.reshape(1, d), w_gate, w_up, w_out, gpost.reshape(1, d), p, w_ple, w_plegate)


def _gdn_conv_kernel(x_ref, halo_ref, w_ref, o_ref, *, tm):
    c = pl.program_id(0)
    i = pl.program_id(1)
    x = x_ref[...]
    halo = jnp.where(i > 0, halo_ref[...], 0.0)
    ext = jnp.concatenate([halo, x], axis=0)
    w = w_ref[...]
    y = x * w[GDN_CONV - 1:GDN_CONV, :]
    for k in range(1, GDN_CONV):
        shifted = pltpu.roll(ext, k, axis=0)[8:8 + tm]
        y = y + shifted * w[GDN_CONV - 1 - k:GDN_CONV - k, :]
    y = y * _sigmoid(y)
    for hd in range(GDN_CONV_HEADS):
        head = c * GDN_CONV_HEADS + hd
        seg = y[:, hd * LANES:(hd + 1) * LANES]
        normed = seg * lax.rsqrt(jnp.sum(seg * seg, axis=-1, keepdims=True) + L2_EPS)
        q_scale = jnp.where(head < GDN_QK_HEADS, GDN_HEAD_DIM ** -0.5, 1.0)
        o_ref[hd] = jnp.where(head < 2 * GDN_QK_HEADS, normed * q_scale, seg)


def gdn_conv(proj, conv_w, tm=1024):
    t = proj.shape[0]
    n_tiles = conv_w.shape[1] // LANES
    cw = GDN_CONV_HEADS
    assert t % tm == 0 and n_tiles % cw == 0
    return pl.pallas_call(
        functools.partial(_gdn_conv_kernel, tm=tm),
        grid=(n_tiles // cw, t // tm),
        in_specs=[
            pl.BlockSpec((tm, cw * LANES), lambda c, i: (i, c)),
            pl.BlockSpec((8, cw * LANES), lambda c, i: (jnp.maximum(i * (tm // 8) - 1, 0), c)),
            pl.BlockSpec((GDN_CONV, cw * LANES), lambda c, i: (0, c)),
        ],
        out_specs=pl.BlockSpec((cw, tm, LANES), lambda c, i: (c, i, 0)),
        out_shape=jax.ShapeDtypeStruct((n_tiles, t, LANES), F32),
        compiler_params=_cparams(("parallel", "parallel")),
        name="gdn_conv",
    )(proj, proj, conv_w)


def _gdn_gate_kernel(x_ref, alog_ref, dtb_ref, lc_ref, lf_ref, o_ref):
    x = x_ref[...]
    lane = lax.broadcasted_iota(jnp.int32, x.shape, 1)
    beta = _sigmoid(x)
    z = x + dtb_ref[...]
    softplus = jnp.maximum(z, 0.0) + jnp.log(1.0 + jnp.exp(-jnp.abs(z)))
    g = -jnp.exp(alog_ref[...]) * softplus
    gcum = jnp.zeros_like(x)
    gtot = jnp.zeros_like(x)
    for piece in _split3(g):
        gcum = gcum + jnp.dot(lc_ref[...], piece, preferred_element_type=F32)
        gtot = gtot + jnp.dot(lf_ref[...], piece, preferred_element_type=F32)
    o_ref[...] = jnp.where(lane < GDN_V_HEADS, beta, jnp.where(lane < 2 * GDN_V_HEADS, gcum, gtot))


def gdn_gates(ba, alog_vec, dtb_vec):
    t = ba.shape[0]
    tm = GDN_GROUP
    r = np.arange(tm)
    same = (r[:, None] // GDN_CHUNK) == (r[None, :] // GDN_CHUNK)
    lc = jnp.asarray(same & (r[None, :] <= r[:, None]), BF16)
    lf = jnp.asarray(same, BF16)
    return pl.pallas_call(
        _gdn_gate_kernel,
        grid=(t // tm,),
        in_specs=[
            pl.BlockSpec((tm, LANES), lambda i: (i, 0)),
            pl.BlockSpec((1, LANES), lambda i: (0, 0)),
            pl.BlockSpec((1, LANES), lambda i: (0, 0)),
            pl.BlockSpec((tm, tm), lambda i: (0, 0)),
            pl.BlockSpec((tm, tm), lambda i: (0, 0)),
        ],
        out_specs=pl.BlockSpec((tm, LANES), lambda i: (i, 0)),
        out_shape=jax.ShapeDtypeStruct((t, LANES), F32),
        compiler_params=_cparams(("parallel",)),
        name="gdn_gates",
    )(ba, alog_vec, dtb_vec, lc, lf)


def _gdn_scan_kernel(q_ref, k_ref, kt_ref, v_ref, slab_ref, slabt_ref, o_ref, s_ref):
    n = pl.program_id(1)

    @pl.when(n == 0)
    def _():
        s_ref[...] = jnp.zeros_like(s_ref)

    L = GDN_GROUP
    C = GDN_CHUNK
    D = GDN_HEAD_DIM
    slab = slab_ref[...]
    slabt = slabt_ref[...]
    row = lax.broadcasted_iota(jnp.int32, (L, L), 0)
    col = lax.broadcasted_iota(jnp.int32, (L, L), 1)
    same = _div_pow2(row, C) == _div_pow2(col, C)
    causal = same & (col <= row)
    strict = same & (col < row)
    eye = (row == col).astype(F32)
    lane = lax.broadcasted_iota(jnp.int32, (L, LANES), 1)
    sub = lax.broadcasted_iota(jnp.int32, (LANES, L), 0)

    def column(idx):
        return jnp.sum(jnp.where(lane == idx, slab, 0.0), axis=1, keepdims=True)

    def rowvec(idx):
        return jnp.sum(jnp.where(sub == idx, slabt, 0.0), axis=0, keepdims=True)

    colk = lax.broadcasted_iota(jnp.int32, (D, L), 1)
    slots = range(2 * GDN_SCAN_HEADS)

    q = [q_ref[hq] for hq in range(GDN_SCAN_HEADS)]
    k = [k_ref[hq] for hq in range(GDN_SCAN_HEADS)]
    kt = [kt_ref[hq] for hq in range(GDN_SCAN_HEADS)]
    kk = [_mm(k[hq], kt[hq]) for hq in range(GDN_SCAN_HEADS)]
    qk = [_mm(q[hq], kt[hq]) for hq in range(GDN_SCAN_HEADS)]
    beta_c, gc_c, gt_c, decay, bp, inv, kdt = [], [], [], [], [], [], []
    for slot in slots:
        hq = slot // 2
        hv = 2 * (pl.program_id(0) * GDN_SCAN_HEADS + hq) + slot % 2
        beta_c.append(column(hv))
        gc_c.append(column(GDN_V_HEADS + hv))
        gt_c.append(column(2 * GDN_V_HEADS + hv))
        gc_r = rowvec(GDN_V_HEADS + hv)
        gt_r = rowvec(2 * GDN_V_HEADS + hv)
        decay.append(jnp.where(causal, jnp.exp(jnp.where(causal, gc_c[slot] - gc_r, 0.0)), 0.0))
        bp.append(jnp.where(strict, -(kk[hq] * beta_c[slot]) * decay[slot], 0.0))
        inv.append(eye + bp[slot])
        kdt.append(kt[hq] * jnp.exp(gt_r - gc_r))
    for _ in range(5):
        bp = [_mm(bp[slot], bp[slot]) for slot in slots]
        inv = [inv[slot] + _mm(inv[slot], bp[slot]) for slot in slots]
    u, w, qkm, q_dec = [], [], [], []
    for slot in slots:
        hq = slot // 2
        egc = jnp.exp(gc_c[slot])
        rhs = jnp.concatenate([v_ref[slot] * beta_c[slot], k[hq] * (beta_c[slot] * egc)], axis=1)
        sol = _mm(inv[slot], rhs)
        u.append(sol[:, :D])
        w.append(sol[:, D:])
        qkm.append(jnp.where(causal, qk[hq] * decay[slot], 0.0))
        q_dec.append(q[hq] * egc)
    state = [s_ref[slot] for slot in slots]
    v_done = [[] for _ in slots]
    for c in range(L // C):
        lo, hi = c * C, (c + 1) * C
        r = [_mm(jnp.concatenate([w[slot][lo:hi], q_dec[slot][lo:hi]], axis=0), state[slot]) for slot in slots]
        for slot in slots:
            v_done[slot].append(u[slot][lo:hi] - r[slot][:C])
            v_all = jnp.concatenate(v_done[slot] + [jnp.zeros((L - hi, D), F32)] * (hi < L), axis=0)
            o_ref[lo:hi, slot * D:(slot + 1) * D] = r[slot][C:] + _mm(qkm[slot][lo:hi, :], v_all)
            kdt_c = jnp.where((colk >= lo) & (colk < hi), kdt[slot], 0.0)
            state[slot] = state[slot] * jnp.exp(gt_c[slot][lo:lo + 1, :]) + _mm(kdt_c, v_all)
    for slot in slots:
        s_ref[slot] = state[slot]


def gdn_scan(qkv_hm, kt_hm, slab, slabt):
    t = qkv_hm.shape[1]
    L = GDN_GROUP
    D = GDN_HEAD_DIM
    hq = GDN_SCAN_HEADS
    assert t % L == 0 and GDN_QK_HEADS % hq == 0
    q_blocks = GDN_QK_HEADS // hq
    return pl.pallas_call(
        _gdn_scan_kernel,
        grid=(q_blocks, t // L),
        in_specs=[
            pl.BlockSpec((hq, L, D), lambda j, n: (j, n, 0)),
            pl.BlockSpec((hq, L, D), lambda j, n: (q_blocks + j, n, 0)),
            pl.BlockSpec((hq, D, L), lambda j, n: (j, 0, n)),
            pl.BlockSpec((2 * hq, L, D), lambda j, n: (q_blocks + j, n, 0)),
            pl.BlockSpec((L, LANES), lambda j, n: (n, 0)),
            pl.BlockSpec((LANES, L), lambda j, n: (0, n)),
        ],
        out_specs=pl.BlockSpec((L, 2 * hq * D), lambda j, n: (n, j)),
        out_shape=jax.ShapeDtypeStruct((t, GDN_V_HEADS * D), F32),
        scratch_shapes=[pltpu.VMEM((2 * hq, D, D), F32)],
        compiler_params=_cparams(("parallel", "arbitrary")),
        name="gdn_scan",
    )(qkv_hm, qkv_hm, kt_hm, qkv_hm, slab, slabt)


def _gdn_out_kernel(o_ref, z_ref, onorm_ref, w_ref, gpost_ref, h_ref, out_ref):
    o = o_ref[...]
    z = z_ref[...]
    parts = []
    for hd in range(GDN_V_HEADS):
        seg = o[:, hd * GDN_HEAD_DIM:(hd + 1) * GDN_HEAD_DIM]
        parts.append(seg * lax.rsqrt(jnp.mean(seg * seg, axis=-1, keepdims=True) + NORM_EPS))
    gated = jnp.concatenate(parts, axis=1) * onorm_ref[...] * (z * _sigmoid(z))
    mix = jnp.dot(gated.astype(BF16), w_ref[...], preferred_element_type=F32)
    out_ref[...] = h_ref[...] + _rms(mix, gpost_ref[...])


def gdn_out(o, proj, onorm_tiled, w_out, gpost, h, tm=512):
    t, vw = o.shape
    d = h.shape[1]
    z_blk = (proj.shape[1] - vw) // vw
    assert proj.shape[1] % vw == 0 and t % tm == 0
    return pl.pallas_call(
        _gdn_out_kernel,
        grid=(t // tm,),
        in_specs=[
            pl.BlockSpec((tm, vw), lambda i: (i, 0)),
            pl.BlockSpec((tm, vw), lambda i: (i, z_blk)),
            pl.BlockSpec((1, vw), lambda i: (0, 0)),
            pl.BlockSpec((vw, d), lambda i: (0, 0)),
            pl.BlockSpec((1, d), lambda i: (0, 0)),
            pl.BlockSpec((tm, d), lambda i: (i, 0)),
        ],
        out_specs=pl.BlockSpec((tm, d), lambda i: (i, 0)),
        out_shape=jax.ShapeDtypeStruct((t, d), F32),
        compiler_params=_cparams(("parallel",)),
        name="gdn_out",
    )(o, proj, onorm_tiled, w_out, gpost.reshape(1, d), h)


def _compress_kernel(x_ref, pos_ref, w1_ref, w2_ref, o_ref):
    x = x_ref[...]
    pos = pos_ref[...]
    nc, half = x.shape
    w1 = w1_ref[...]
    first = _mm(x + pos[0:1, :], w1[:half])
    second = _mm(x + pos[1:2, :], w1[half:])
    hid = first + pltpu.roll(second, nc - 1, axis=0)
    hid = hid * _sigmoid(hid)
    out = jnp.dot(hid.astype(BF16), w2_ref[...], preferred_element_type=F32)
    rowi = lax.broadcasted_iota(jnp.int32, out.shape, 0)
    o_ref[...] = jnp.where(rowi < nc - 1, out, 0.0)


def compress(x2, pos2, w1, w2):
    _, g, nc, wdt = x2.shape
    hid = w1.shape[2]
    dh = w2.shape[2]
    return pl.pallas_call(
        _compress_kernel,
        grid=(2, g),
        in_specs=[
            pl.BlockSpec((None, None, nc, wdt), lambda b, gi: (b, gi, 0, 0)),
            pl.BlockSpec((None, 2, wdt), lambda b, gi: (b, 0, 0)),
            pl.BlockSpec((None, 2 * wdt, hid), lambda b, gi: (b, 0, 0)),
            pl.BlockSpec((None, hid, dh), lambda b, gi: (b, 0, 0)),
        ],
        out_specs=pl.BlockSpec((None, None, nc, dh), lambda b, gi: (b, gi, 0, 0)),
        out_shape=jax.ShapeDtypeStruct((2, g, nc, dh), F32),
        compiler_params=_cparams(("parallel", "parallel")),
        name="nsa_compress",
    )(x2, pos2, w1, w2)


def _cmp_topk_kernel(q_ref, kbd_ref, vt_ref, m_ref, oc_ref, sel_ref, s_ref, p_ref, psum_ref, *,
                     tq, nc, nsel, topk, q0):
    i = q0 + pl.program_id(1)
    dh = NSA_HEAD_DIM
    rows_per = min(nc, CMP_ROWS)
    q = (q_ref[...] * ((dh ** -0.5) * LOG2_E)).astype(BF16)
    s_ref[...] = jnp.dot(kbd_ref[...], q, preferred_element_type=F32)
    for ch in range(tq // LANES):
        lanes = slice(ch * LANES, (ch + 1) * LANES)
        tpos = i * tq + ch * LANES + lax.broadcasted_iota(jnp.int32, (rows_per, LANES), 1)
        cblk0 = lax.broadcasted_iota(jnp.int32, (rows_per, LANES), 0)
        masks = [(CMP_STRIDE * (cblk0 + c * rows_per) + CMP_BLOCK - 1) <= tpos for c in range(nc // rows_per)]
        for r in range(NSA_REP):
            pieces = [slice(r * nc + c * rows_per, r * nc + (c + 1) * rows_per) for c in range(nc // rows_per)]
            m = jnp.full((1, LANES), -jnp.inf, F32)
            for rows, mask in zip(pieces, masks):
                m = jnp.maximum(m, jnp.max(jnp.where(mask, s_ref[rows, lanes], -jnp.inf), axis=0, keepdims=True))
            m = jnp.where(m > -jnp.inf, m, 0.0)
            total = jnp.zeros((1, LANES), F32)
            for rows, mask in zip(pieces, masks):
                e = jnp.exp2(jnp.where(mask, s_ref[rows, lanes], -jnp.inf) - m)
                s_ref[rows, lanes] = e
                total = total + jnp.sum(e, axis=0, keepdims=True)
            inv = 1.0 / jnp.maximum(total, 1e-30)
            for c, rows in enumerate(pieces):
                p = s_ref[rows, lanes] * inv
                p_ref[rows, lanes] = p.astype(BF16)
                prow = slice(c * rows_per, (c + 1) * rows_per)
                psum_ref[prow, lanes] = p if r == 0 else psum_ref[prow, lanes] + p
    oc_t = jnp.dot(vt_ref[...], p_ref[...], preferred_element_type=F32)
    imp = jnp.zeros((nsel, tq), F32)
    for piece in _split3(psum_ref[...]):
        imp = imp + jnp.dot(m_ref[...], piece, preferred_element_type=F32)
    for ch in range(tq // LANES):
        lanes = slice(ch * LANES, (ch + 1) * LANES)
        oc_ref[lanes, :] = oc_t[:, lanes].T
        t1 = i * tq + ch * LANES + lax.broadcasted_iota(jnp.int32, (nsel, LANES), 1)
        blk = lax.broadcasted_iota(jnp.int32, (nsel, LANES), 0)
        cur = _div_pow2(t1, SEL_BLOCK)
        forced = (blk == 0) | (blk == cur) | (blk == cur - 1)
        valid = blk * SEL_BLOCK <= t1
        score = jnp.where(valid, jnp.where(forced, FORCED_SCORE, imp[:, lanes]), -jnp.inf)
        blkf = blk.astype(F32)
        work = score
        for _ in range(topk):
            mx = jnp.max(work, axis=0, keepdims=True)
            first = jnp.min(jnp.where(work == mx, blkf, float(nsel)), axis=0, keepdims=True)
            work = jnp.where(blkf == first, -jnp.inf, work)
        sel_ref[:nsel, lanes] = jnp.where((score > -jnp.inf) & (work == -jnp.inf), 1.0, 0.0).astype(sel_ref.dtype)
        if sel_ref.shape[0] > nsel:
            sel_ref[nsel:, lanes] = jnp.zeros((sel_ref.shape[0] - nsel, LANES), sel_ref.dtype)


def cmp_topk(q_t, kbd, v_t, imp_mat_t, nsel_all, topk, q0, nq, tq):
    qw = q_t.shape[0]
    g = kbd.shape[0]
    nc = kbd.shape[1] // NSA_REP
    nsel = imp_mat_t.shape[0]
    assert nc % min(nc, CMP_ROWS) == 0
    return pl.pallas_call(
        functools.partial(_cmp_topk_kernel, tq=tq, nc=nc, nsel=nsel, topk=topk, q0=q0),
        grid=(g, nq),
        in_specs=[
            pl.BlockSpec((NSA_GW, tq), lambda gi, i: (gi, q0 + i)),
            pl.BlockSpec((None, NSA_REP * nc, NSA_GW), lambda gi, i: (gi, 0, 0)),
            pl.BlockSpec((None, NSA_GW, NSA_REP * nc), lambda gi, i: (gi, 0, 0)),
            pl.BlockSpec((nsel, nc), lambda gi, i: (0, 0)),
        ],
        out_specs=[
            pl.BlockSpec((tq, NSA_GW), lambda gi, i: (i, gi)),
            pl.BlockSpec((None, nsel_all, tq), lambda gi, i: (gi, 0, i)),
        ],
        out_shape=[
            jax.ShapeDtypeStruct((nq * tq, qw), F32),
            jax.ShapeDtypeStruct((g, nsel_all, nq * tq), F32),
        ],
        scratch_shapes=[
            pltpu.VMEM((NSA_REP * nc, tq), F32),
            pltpu.VMEM((NSA_REP * nc, tq), BF16),
            pltpu.VMEM((nc, tq), F32),
        ],
        compiler_params=_cparams(("parallel", "parallel")),
        name="nsa_cmp_topk",
    )(q_t, kbd, v_t, imp_mat_t)


def _flash_kernel(q_ref, k_ref, v_ref, sel_ref, o_ref,
                  m_ref, l_ref, alpha_ref, acc_ref, s_ref, p_ref, bias_ref, kbd_ref, vbd_ref, qs_ref, *, tq, kt):
    qi = pl.program_id(1)
    dh = NSA_HEAD_DIM
    n_sub = tq // FLASH_SUB
    n_blk = kt // SEL_BLOCK

    m_ref[...] = jnp.full_like(m_ref, NEG_INIT)
    l_ref[...] = jnp.zeros_like(l_ref)
    acc_ref[...] = jnp.zeros_like(acc_ref)
    qs_ref[...] = (q_ref[...] * ((dh ** -0.5) * LOG2_E)).astype(BF16)
    one_row = lax.broadcasted_iota(jnp.int32, (FLASH_SUM_ROWS, NSA_REP * kt), 0)
    one_head = _div_pow2(lax.broadcasted_iota(jnp.int32, (FLASH_SUM_ROWS, NSA_REP * kt), 1), kt)
    vbd_ref[NSA_GW:, :] = (one_row == one_head).astype(BF16)

    def key_tile(ki, positional, first_sub=0):
        k4 = k_ref[ki]
        kseg = _div_pow2(lax.broadcasted_iota(jnp.int32, k4.shape, 1), dh)
        v4 = v_ref[ki]
        vblk = _div_pow2(lax.broadcasted_iota(jnp.int32, v4.shape, 0), dh)
        for r in range(NSA_REP):
            kbd_ref[r * kt:(r + 1) * kt, :] = jnp.where(kseg == r, k4, jnp.zeros_like(k4))
            vbd_ref[:NSA_GW, r * kt:(r + 1) * kt] = jnp.where(vblk == r, v4, jnp.zeros_like(v4))

        def scores(sub):
            c0 = sub * FLASH_SUB
            cols = slice(c0, c0 + FLASH_SUB)
            s_ref[:, cols] = jnp.dot(kbd_ref[...], qs_ref[:, cols],
                                     preferred_element_type=F32)
            for jb in range(n_blk):
                rows = slice(jb * SEL_BLOCK, (jb + 1) * SEL_BLOCK)
                picked = sel_ref[pl.ds(ki * n_blk + jb, 1), cols] > 0.5
                if positional:
                    tpos = qi * tq + c0 + lax.broadcasted_iota(jnp.int32, (SEL_BLOCK, FLASH_SUB), 1)
                    kpos = ki * kt + jb * SEL_BLOCK + lax.broadcasted_iota(jnp.int32, (SEL_BLOCK, FLASH_SUB), 0)
                    allowed = picked & (kpos <= tpos)
                else:
                    allowed = jnp.broadcast_to(picked, (SEL_BLOCK, FLASH_SUB))
                bias_ref[rows, cols] = jnp.where(allowed, 0.0, -jnp.inf)

        scores(first_sub)
        for sub in range(first_sub, n_sub):
            c0 = sub * FLASH_SUB
            cols = slice(c0, c0 + FLASH_SUB)
            if sub + 1 < n_sub:
                scores(sub + 1)
            for ch in range(FLASH_SUB // LANES):
                lanes = slice(c0 + ch * LANES, c0 + (ch + 1) * LANES)
                bias = bias_ref[:, lanes]
                for r in range(NSA_REP):
                    x = s_ref[r * kt:(r + 1) * kt, lanes] + bias
                    m_prev = m_ref[r:r + 1, lanes]
                    m_new = jnp.maximum(m_prev, jnp.max(x, axis=0, keepdims=True))
                    m_ref[r:r + 1, lanes] = m_new
                    alpha_ref[r:r + 1, lanes] = jnp.exp2(m_prev - m_new)
                    p_ref[r * kt:(r + 1) * kt, lanes] = jnp.exp2(x - m_new).astype(BF16)
            pv = jnp.dot(vbd_ref[...], p_ref[:, cols], preferred_element_type=F32)
            for r in range(NSA_REP):
                hd = slice(r * dh, (r + 1) * dh)
                alpha = alpha_ref[r:r + 1, cols]
                acc_ref[hd, cols] = acc_ref[hd, cols] * alpha + pv[hd]
                l_ref[r:r + 1, cols] = l_ref[r:r + 1, cols] * alpha + pv[NSA_GW + r:NSA_GW + r + 1]

    def before_diagonal(ki, carry):
        key_tile(ki, False)
        return carry

    diag = qi * (tq // kt)
    lax.fori_loop(0, diag, before_diagonal, 0)
    for d in range(tq // kt):
        key_tile(diag + d, True, first_sub=d * kt // FLASH_SUB)

    for ch in range(tq // LANES):
        lanes = slice(ch * LANES, (ch + 1) * LANES)
        out_t = jnp.concatenate(
            [acc_ref[r * dh:(r + 1) * dh, lanes] / l_ref[r:r + 1, lanes] for r in range(NSA_REP)], axis=0)
        o_ref[lanes, :] = out_t.T


def flash_branch(q_t, k4, vt4, sel_t, tq, kt):
    qw, t = q_t.shape
    g, nk = k4.shape[:2]
    nsel = sel_t.shape[1]
    assert t % tq == 0 and tq % kt == 0 and kt % SEL_BLOCK == 0 and tq % FLASH_SUB == 0 and kt % FLASH_SUB == 0
    return pl.pallas_call(
        functools.partial(_flash_kernel, tq=tq, kt=kt),
        grid=(g, t // tq),
        in_specs=[
            pl.BlockSpec((NSA_GW, tq), lambda gi, i: (gi, i)),
            pl.BlockSpec((None, nk, kt, NSA_GW), lambda gi, i: (gi, 0, 0, 0), pipeline_mode=pl.Buffered(1)),
            pl.BlockSpec((None, nk, NSA_GW, kt), lambda gi, i: (gi, 0, 0, 0), pipeline_mode=pl.Buffered(1)),
            pl.BlockSpec((None, nsel, tq), lambda gi, i: (gi, 0, i)),
        ],
        out_specs=pl.BlockSpec((tq, NSA_GW), lambda gi, i: (i, gi)),
        scratch_shapes=[
            pltpu.VMEM((8, tq), F32),
            pltpu.VMEM((8, tq), F32),
            pltpu.VMEM((8, tq), F32),
            pltpu.VMEM((NSA_GW, tq), F32),
            pltpu.VMEM((NSA_REP * kt, tq), F32),
            pltpu.VMEM((NSA_REP * kt, tq), BF16),
            pltpu.VMEM((kt, tq), F32),
            pltpu.VMEM((NSA_REP * kt, NSA_GW), BF16),
            pltpu.VMEM((NSA_GW + FLASH_SUM_ROWS, NSA_REP * kt), BF16),
            pltpu.VMEM((NSA_GW, tq), BF16),
        ],
        out_shape=jax.ShapeDtypeStruct((t, qw), F32),
        compiler_params=_cparams(("parallel", "arbitrary")),
        name="nsa_selected",
    )(q_t, k4, vt4, sel_t)


def _window_kernel(q_ref, k_ref, v_ref, o_ref, s_ref, p_ref, bias_ref, kbd_ref, vbd_ref, *, tq, kt):
    i = pl.program_id(1)
    dh = NSA_HEAD_DIM
    nw = WINDOW // kt + 1
    span = nw * kt
    qs = (q_ref[...] * ((dh ** -0.5) * LOG2_E)).astype(BF16)
    for w in range(nw):
        tile = i - (nw - 1) + w
        k4 = k_ref[jnp.maximum(tile, 0)]
        kseg = _div_pow2(lax.broadcasted_iota(jnp.int32, k4.shape, 1), dh)
        v4 = v_ref[jnp.maximum(tile, 0)]
        vblk = _div_pow2(lax.broadcasted_iota(jnp.int32, v4.shape, 0), dh)
        for r in range(NSA_REP):
            at = r * span + w * kt
            kbd_ref[at:at + kt, :] = jnp.where(kseg == r, k4, jnp.zeros_like(k4))
            vbd_ref[:NSA_GW, at:at + kt] = jnp.where(vblk == r, v4, jnp.zeros_like(v4))
        tpos = i * tq + lax.broadcasted_iota(jnp.int32, (kt, tq), 1)
        kpos = tile * kt + lax.broadcasted_iota(jnp.int32, (kt, tq), 0)
        allowed = (kpos >= 0) & (kpos <= tpos) & (kpos > tpos - WINDOW)
        bias_ref[w * kt:(w + 1) * kt, :] = jnp.where(allowed, 0.0, -jnp.inf)
    one_row = lax.broadcasted_iota(jnp.int32, (FLASH_SUM_ROWS, NSA_REP * span), 0)
    one_col = lax.broadcasted_iota(jnp.int32, (FLASH_SUM_ROWS, NSA_REP * span), 1)
    vbd_ref[NSA_GW:, :] = ((one_col >= one_row * span) & (one_col < (one_row + 1) * span)).astype(BF16)

    s_ref[...] = jnp.dot(kbd_ref[...], qs, preferred_element_type=F32)
    for ch in range(tq // LANES):
        lanes = slice(ch * LANES, (ch + 1) * LANES)
        for r in range(NSA_REP):
            m = jnp.full((1, LANES), -jnp.inf, F32)
            for w in range(nw):
                rows = slice(r * span + w * kt, r * span + (w + 1) * kt)
                m = jnp.maximum(m, jnp.max(s_ref[rows, lanes] + bias_ref[w * kt:(w + 1) * kt, lanes],
                                           axis=0, keepdims=True))
            m = jnp.where(m > -jnp.inf, m, 0.0)
            for w in range(nw):
                rows = slice(r * span + w * kt, r * span + (w + 1) * kt)
                x = s_ref[rows, lanes] + bias_ref[w * kt:(w + 1) * kt, lanes]
                p_ref[rows, lanes] = jnp.exp2(x - m).astype(BF16)
    pv = jnp.dot(vbd_ref[...], p_ref[...], preferred_element_type=F32)
    for ch in range(tq // LANES):
        lanes = slice(ch * LANES, (ch + 1) * LANES)
        out_t = jnp.concatenate(
            [pv[r * dh:(r + 1) * dh, lanes] / jnp.maximum(pv[NSA_GW + r:NSA_GW + r + 1, lanes], 1e-30)
             for r in range(NSA_REP)], axis=0)
        o_ref[lanes, :] = out_t.T


def window_branch(q_t, k4, vt4, tq):
    qw, t = q_t.shape
    g, nk, kt, _ = k4.shape
    assert tq == kt and WINDOW % kt == 0 and t % tq == 0
    span = (WINDOW // kt + 1) * kt
    return pl.pallas_call(
        functools.partial(_window_kernel, tq=tq, kt=kt),
        grid=(g, t // tq),
        in_specs=[
            pl.BlockSpec((NSA_GW, tq), lambda gi, i: (gi, i)),
            pl.BlockSpec((None, nk, kt, NSA_GW), lambda gi, i: (gi, 0, 0, 0)),
            pl.BlockSpec((None, nk, NSA_GW, kt), lambda gi, i: (gi, 0, 0, 0)),
        ],
        out_specs=pl.BlockSpec((tq, NSA_GW), lambda gi, i: (i, gi)),
        scratch_shapes=[
            pltpu.VMEM((NSA_REP * span, tq), F32),
            pltpu.VMEM((NSA_REP * span, tq), BF16),
            pltpu.VMEM((span, tq), F32),
            pltpu.VMEM((NSA_REP * span, NSA_GW), BF16),
            pltpu.VMEM((NSA_GW + FLASH_SUM_ROWS, NSA_REP * span), BF16),
        ],
        out_shape=jax.ShapeDtypeStruct((t, qw), F32),
        compiler_params=_cparams(("parallel", "parallel")),
        name="nsa_window",
    )(q_t, k4, vt4)


def _nsa_out_kernel(oc_ref, os_ref, ow_ref, gl_ref, eg_ref, w_ref, gpost_ref, h_ref, out_ref):
    pieces = _split3(_sigmoid(gl_ref[...]))
    mixed = jnp.zeros(oc_ref.shape, F32)
    for b, br_ref in enumerate((oc_ref, os_ref, ow_ref)):
        gfull = jnp.zeros(oc_ref.shape, F32)
        for piece in pieces:
            gfull = gfull + jnp.dot(piece, eg_ref[b], preferred_element_type=F32)
        mixed = mixed + gfull * br_ref[...]
    mix = jnp.dot(mixed.astype(BF16), w_ref[...], preferred_element_type=F32)
    out_ref[...] = h_ref[...] + _rms(mix, gpost_ref[...])


def nsa_out(oc, osel, ow, gate_logits, expand, w_o, gpost, h, tm=512):
    t, qw = oc.shape
    d = h.shape[1]
    assert t % tm == 0
    row = lambda w: pl.BlockSpec((tm, w), lambda i: (i, 0))
    return pl.pallas_call(
        _nsa_out_kernel,
        grid=(t // tm,),
        in_specs=[
            row(qw), row(qw), row(qw), row(LANES),
            pl.BlockSpec((3, LANES, qw), lambda i: (0, 0, 0)),
            pl.BlockSpec((qw, d), lambda i: (0, 0)),
            pl.BlockSpec((1, d), lambda i: (0, 0)),
            row(d),
        ],
        out_specs=row(d),
        out_shape=jax.ShapeDtypeStruct((t, d), F32),
        compiler_params=_cparams(("parallel",)),
        name="nsa_out",
    )(oc, osel, ow, gate_logits, expand, w_o, gpost.reshape(1, d), h)


def _pad_cols(w, n):
    return jnp.pad(w, ((0, 0), (0, n - w.shape[1])))


def _importance_matrix(nc, nsel):
    r = SEL_BLOCK // CMP_STRIDE
    c = CMP_BLOCK // CMP_STRIDE
    mat = np.zeros((nc, nsel), np.float32)
    for kblk in range(nsel):
        for m in range(r):
            for n in range(c):
                j = r * kblk + m - n
                if 0 <= j < nc - 1:
                    mat[j, kblk] += 1.0
    return jnp.asarray(mat, BF16)


def _gate_expand():
    e = np.zeros((3, LANES, NSA_GROUPS * NSA_GW), np.float32)
    for head in range(NSA_GROUPS * NSA_REP):
        for b in range(3):
            e[b, head * 3 + b, head * NSA_HEAD_DIM:(head + 1) * NSA_HEAD_DIM] = 1.0
    return jnp.asarray(e, BF16)


def _block_diag_kv(k_cmp, v_cmp):
    g, nc, dh = k_cmp.shape
    eye = jnp.eye(NSA_REP, dtype=bool)
    kct = jnp.swapaxes(k_cmp, 1, 2)
    kbd = jnp.where(eye[None, :, None, :, None], kct[:, None, :, None, :], 0.0)
    vbd = jnp.where(eye[None, :, None, :, None], v_cmp[:, None, :, None, :], 0.0)
    return (kbd.reshape(g, NSA_REP * dh, NSA_REP * nc).astype(BF16),
            vbd.reshape(g, NSA_REP * nc, NSA_REP * dh).astype(BF16))


def kernel(x, p, mix_pre_norm, mix_post_norm, ffn_pre_norm, ffn_post_norm, gdn_w_in, gdn_conv_w, gdn_a_log,
           gdn_dt_bias, gdn_o_norm, gdn_w_out, kv_norm, kv_w, cmp_pos, cmp_w1, cmp_w2, nsa_w_qg, nsa_w_o,
           ffn_w_in, ffn_w_out, ple_w_in, ple_w_gate):
    depth = p.shape[0]
    n_a = gdn_w_in.shape[0]
    t = x.shape[1]
    h = x[0]
    fh = ffn_w_out.shape[1]
    conv_w_cols = gdn_conv_w.shape[2]
    vw = GDN_V_HEADS * GDN_HEAD_DIM
    main_w = conv_w_cols + vw

    def channel_and_ple(h, i):
        w_in = ffn_w_in[i].astype(BF16)
        return ffn_ple(h, ffn_pre_norm[i], w_in[:, :fh], w_in[:, fh:], ffn_w_out[i].astype(BF16),
                       ffn_post_norm[i], p[i, 0], ple_w_in[i].astype(BF16), ple_w_gate[i].astype(BF16))

    for i in range(n_a):
        w_in = gdn_w_in[i]
        w_beta = w_in[:, main_w:main_w + GDN_V_HEADS]
        w_a = w_in[:, main_w + GDN_V_HEADS:]
        w_small = _pad_cols(jnp.concatenate([w_beta, w_a, w_a], axis=1), LANES).astype(BF16)
        proj = norm_matmul(h, mix_pre_norm[i], w_in[:, :main_w].astype(BF16))
        ba = norm_matmul(h, mix_pre_norm[i], w_small)
        pad_vec = lambda v: jnp.pad(v, (GDN_V_HEADS, LANES - 2 * GDN_V_HEADS))
        alog_vec = (pad_vec(gdn_a_log[i]) + jnp.pad(gdn_a_log[i], (2 * GDN_V_HEADS, LANES - 3 * GDN_V_HEADS)))
        dtb_vec = (pad_vec(gdn_dt_bias[i]) + jnp.pad(gdn_dt_bias[i], (2 * GDN_V_HEADS, LANES - 3 * GDN_V_HEADS)))
        slab = gdn_gates(ba, alog_vec.reshape(1, LANES), dtb_vec.reshape(1, LANES))
        qkv_hm = gdn_conv(proj, gdn_conv_w[i])
        kt_hm = jnp.swapaxes(qkv_hm[GDN_QK_HEADS:2 * GDN_QK_HEADS], 1, 2)
        o = gdn_scan(qkv_hm, kt_hm, slab, slab.T)
        onorm_tiled = jnp.tile(gdn_o_norm[i], GDN_V_HEADS).reshape(1, vw)
        h = gdn_out(o, proj, onorm_tiled, gdn_w_out[i].astype(BF16), mix_post_norm[i], h)
        h = channel_and_ple(h, i)

    g = NSA_GROUPS
    dh = NSA_HEAD_DIM
    kv = norm_matmul(h, kv_norm, kv_w.astype(BF16), tn=768)
    kv6 = jnp.transpose(kv.reshape(t, 6, g, dh), (1, 2, 0, 3))
    nc = t // CMP_STRIDE
    nsel = t // SEL_BLOCK
    x2 = kv6[0:2].reshape(2, g, nc, CMP_STRIDE * dh)
    pos2 = cmp_pos.reshape(2, 2, CMP_STRIDE * dh)
    cmp_out = compress(x2, pos2, cmp_w1.astype(BF16), cmp_w2.astype(BF16))
    imp_mat_t = _importance_matrix(nc, nsel).T
    nq_r = t // (CMP_TQ * CMP_RANGES)
    assert t % (CMP_TQ * CMP_RANGES) == 0 and nc % CMP_RANGES == 0 and nsel % (8 * CMP_RANGES) == 0
    cmp_ranges = []
    for rg in range(CMP_RANGES):
        nc_r, nsel_r = (rg + 1) * nc // CMP_RANGES, (rg + 1) * nsel // CMP_RANGES
        vbd_t, kbd = _block_diag_kv(cmp_out[1][:, :nc_r], cmp_out[0][:, :nc_r])
        cmp_ranges.append((kbd, vbd_t, imp_mat_t[:nsel_r, :nc_r]))
    kt = FLASH_KT
    tiles = lambda a: a.reshape(g, t // kt, kt, dh)
    rep_k = lambda a: jnp.tile(tiles(a), (1, 1, 1, NSA_REP)).astype(BF16)
    rep_t = lambda a: jnp.tile(jnp.swapaxes(tiles(a), 2, 3), (1, 1, NSA_REP, 1)).astype(BF16)
    k_slc, v_slc_t = rep_k(kv6[2]), rep_t(kv6[3])
    k_win, v_win_t = rep_k(kv6[4]), rep_t(kv6[5])
    expand = _gate_expand()

    for i in range(n_a, depth):
        j = i - n_a
        qw = g * NSA_GW
        w_qg = nsa_w_qg[j]
        q_t = norm_matmul_t(h, mix_pre_norm[i], w_qg[:, :qw].T.astype(BF16))
        gate_logits = norm_matmul(h, mix_pre_norm[i], _pad_cols(w_qg[:, qw:], LANES).astype(BF16))
        parts = [cmp_topk(q_t, kbd, vbd_t, imp_r, nsel, min(SEL_TOPK, nsel), rg * nq_r, nq_r, CMP_TQ)
                 for rg, (kbd, vbd_t, imp_r) in enumerate(cmp_ranges)]
        o_c = jnp.concatenate([o for o, _ in parts], axis=0)
        sel_t = jnp.concatenate([sl for _, sl in parts], axis=2)
        o_s = flash_branch(q_t, k_slc, v_slc_t, sel_t, tq=min(FLASH_TQ, t), kt=kt)
        o_w = window_branch(q_t, k_win, v_win_t, tq=kt)
        h = nsa_out(o_c, o_s, o_w, gate_logits, expand, nsa_w_o[j].astype(BF16), mix_post_norm[i], h)
        h = channel_and_ple(h, i)
    return h[None]
```

```python
import functools

import numpy as np
import jax
import jax.numpy as jnp
from jax import lax
from jax.experimental import pallas as pl
from jax.experimental.pallas import tpu as pltpu

F32 = jnp.float32
BF16 = jnp.bfloat16

NORM_EPS = 1e-6
L2_EPS = 1e-6
GDN_QK_HEADS = 8
GDN_V_HEADS = 16
GDN_HEAD_DIM = 128
GDN_CONV = 4
GDN_CHUNK = 64
GDN_GROUP = 256
GDN_SCAN_HEADS = 8
GDN_CONV_HEADS = 4
NSA_GROUPS = 4
NSA_REP = 4
NSA_HEAD_DIM = 64
NSA_GW = NSA_REP * NSA_HEAD_DIM
CMP_BLOCK = 32
CMP_STRIDE = 16
SEL_BLOCK = 64
SEL_TOPK = 16
WINDOW = 512
FORCED_SCORE = 1e4
LANES = 128
NEG_INIT = -(2.0 ** 100)
LOG2_E = 1.4426950408889634
FLASH_TQ = 2048
FLASH_KT = 256
FLASH_SUB = 256
FLASH_SUM_ROWS = 16
CMP_ROWS = 256
CMP_TQ = 256
CMP_RANGES = 4

VMEM_LIMIT = 56 * 1024 * 1024
FFN_VMEM_LIMIT = 60 * 1024 * 1024


def _cparams(sem, vmem_limit=VMEM_LIMIT):
    return pltpu.CompilerParams(dimension_semantics=sem, vmem_limit_bytes=vmem_limit)


def _rms(x, gain):
    return x * lax.rsqrt(jnp.mean(x * x, axis=-1, keepdims=True) + NORM_EPS) * gain


def _mm(a, b):
    return jnp.dot(a.astype(BF16), b.astype(BF16), preferred_element_type=F32)


def _sigmoid(x):
    return 1.0 / (1.0 + jnp.exp(-x))


def _div_pow2(x, d):
    shift = d.bit_length() - 1
    assert d == 1 << shift
    return jnp.right_shift(x, shift)


def _split3(x):
    a = x.astype(BF16)
    r = x - a.astype(F32)
    b = r.astype(BF16)
    c = (r - b.astype(F32)).astype(BF16)
    return a, b, c


def _norm_matmul_kernel(x_ref, g_ref, w_ref, o_ref, xn_ref):
    @pl.when(pl.program_id(1) == 0)
    def _():
        xn_ref[...] = _rms(x_ref[...], g_ref[...]).astype(BF16)

    o_ref[...] = jnp.dot(xn_ref[...], w_ref[...], preferred_element_type=F32)


def _norm_matmul_t_kernel(x_ref, g_ref, wt_ref, o_ref, xn_ref):
    @pl.when(pl.program_id(1) == 0)
    def _():
        xn_ref[...] = _rms(x_ref[...], g_ref[...]).astype(BF16)

    o_ref[...] = lax.dot_general(wt_ref[...], xn_ref[...], (((1,), (1,)), ((), ())), preferred_element_type=F32)


def norm_matmul_t(h, gain, w_t, tm=1024, tn=1024):
    t, d = h.shape
    n = w_t.shape[0]
    tn = min(tn, n)
    assert t % tm == 0 and n % tn == 0
    return pl.pallas_call(
        _norm_matmul_t_kernel,
        grid=(t // tm, n // tn),
        in_specs=[
            pl.BlockSpec((tm, d), lambda i, j: (i, 0)),
            pl.BlockSpec((1, d), lambda i, j: (0, 0)),
            pl.BlockSpec((tn, d), lambda i, j: (j, 0)),
        ],
        out_specs=pl.BlockSpec((tn, tm), lambda i, j: (j, i)),
        out_shape=jax.ShapeDtypeStruct((n, t), F32),
        scratch_shapes=[pltpu.VMEM((tm, d), BF16)],
        compiler_params=_cparams(("parallel", "arbitrary")),
        name="norm_matmul_t",
    )(h, gain.reshape(1, d), w_t)


def norm_matmul(h, gain, w, tm=1024, tn=1024, n=None):
    t, d = h.shape
    n = w.shape[1] if n is None else n
    tn = min(tn, n)
    assert t % tm == 0 and n % tn == 0
    return pl.pallas_call(
        _norm_matmul_kernel,
        grid=(t // tm, n // tn),
        in_specs=[
            pl.BlockSpec((tm, d), lambda i, j: (i, 0)),
            pl.BlockSpec((1, d), lambda i, j: (0, 0)),
            pl.BlockSpec((d, tn), lambda i, j: (0, j)),
        ],
        out_specs=pl.BlockSpec((tm, tn), lambda i, j: (i, j)),
        out_shape=jax.ShapeDtypeStruct((t, n), F32),
        scratch_shapes=[pltpu.VMEM((tm, d), BF16)],
        compiler_params=_cparams(("parallel", "arbitrary")),
        name="norm_matmul",
    )(h, gain.reshape(1, d), w)


def _ffn_ple_kernel(h_ref, gpre_ref, wg_ref, wu_ref, wo_ref, gpost_ref, p_ref, wple_ref, wgt_ref,
                    o_ref, xn_ref, acc_ref, *, nf):
    f = pl.program_id(1)

    @pl.when(f == 0)
    def _():
        xn_ref[...] = _rms(h_ref[...], gpre_ref[...]).astype(BF16)
        acc_ref[...] = jnp.zeros_like(acc_ref)

    xn = xn_ref[...]
    tf = wg_ref.shape[1]
    half = (tf // LANES // 2) * LANES
    for a, b in ((0, half), (half, tf)):
        gate = jnp.dot(xn, wg_ref[:, a:b], preferred_element_type=F32)
        up = jnp.dot(xn, wu_ref[:, a:b], preferred_element_type=F32)
        act = gate * _sigmoid(gate) * up
        acc_ref[...] += jnp.dot(act.astype(BF16), wo_ref[a:b, :], preferred_element_type=F32)

    @pl.when(f == nf - 1)
    def _():
        h2 = h_ref[...] + _rms(acc_ref[...], gpost_ref[...])
        emb = jnp.dot(p_ref[...].astype(BF16), wple_ref[...], preferred_element_type=F32)
        gt = _sigmoid(jnp.dot(h2.astype(BF16), wgt_ref[...], preferred_element_type=F32))
        o_ref[...] = h2 + emb * gt


def ffn_ple(h, gpre, w_in, w_out, gpost, p, w_ple, w_plegate, tm=1024, tf=1408):
    t, d = h.shape
    fh = w_out.shape[0]
    pd = p.shape[1]
    assert t % tm == 0 and fh % tf == 0 and w_in.shape[1] == 2 * fh
    nf = fh // tf
    return pl.pallas_call(
        functools.partial(_ffn_ple_kernel, nf=nf),
        grid=(t // tm, nf),
        in_specs=[
            pl.BlockSpec((tm, d), lambda i, f: (i, 0)),
            pl.BlockSpec((1, d), lambda i, f: (0, 0)),
            pl.BlockSpec((d, tf), lambda i, f: (0, f)),
            pl.BlockSpec((d, tf), lambda i, f: (0, nf + f)),
            pl.BlockSpec((tf, d), lambda i, f: (f, 0)),
            pl.BlockSpec((1, d), lambda i, f: (0, 0)),
            pl.BlockSpec((tm, pd), lambda i, f: (i, 0)),
            pl.BlockSpec((pd, d), lambda i, f: (0, 0), pipeline_mode=pl.Buffered(1)),
            pl.BlockSpec((d, d), lambda i, f: (0, 0), pipeline_mode=pl.Buffered(1)),
        ],
        out_specs=pl.BlockSpec((tm, d), lambda i, f: (i, 0)),
        out_shape=jax.ShapeDtypeStruct((t, d), F32),
        scratch_shapes=[pltpu.VMEM((tm, d), BF16), pltpu.VMEM((tm, d), F32)],
        compiler_params=_cparams(("parallel", "arbitrary"), FFN_VMEM_LIMIT),
        name="ffn_ple",
    )(h, gpre.reshape(1, d), w_in, w_in, w_out, gpost.reshape(1, d), p, w_ple, w_plegate)


def _gdn_conv_kernel(x_ref, halo_ref, w_ref, o_ref, *, tm):
    c = pl.program_id(0)
    i = pl.program_id(1)
    x = x_ref[...]
    halo = jnp.where(i > 0, halo_ref[...], 0.0)
    ext = jnp.concatenate([halo, x], axis=0)
    w = w_ref[...]
    y = x * w[GDN_CONV - 1:GDN_CONV, :]
    for k in range(1, GDN_CONV):
        shifted = pltpu.roll(ext, k, axis=0)[8:8 + tm]
        y = y + shifted * w[GDN_CONV - 1 - k:GDN_CONV - k, :]
    y = y * _sigmoid(y)
    for hd in range(GDN_CONV_HEADS):
        head = c * GDN_CONV_HEADS + hd
        seg = y[:, hd * LANES:(hd + 1) * LANES]
        normed = seg * lax.rsqrt(jnp.sum(seg * seg, axis=-1, keepdims=True) + L2_EPS)
        q_scale = jnp.where(head < GDN_QK_HEADS, GDN_HEAD_DIM ** -0.5, 1.0)
        o_ref[hd] = jnp.where(head < 2 * GDN_QK_HEADS, normed * q_scale, seg)


def gdn_conv(proj, conv_w, tm=1024):
    t = proj.shape[0]
    n_tiles = conv_w.shape[1] // LANES
    cw = GDN_CONV_HEADS
    assert t % tm == 0 and n_tiles % cw == 0
    return pl.pallas_call(
        functools.partial(_gdn_conv_kernel, tm=tm),
        grid=(n_tiles // cw, t // tm),
        in_specs=[
            pl.BlockSpec((tm, cw * LANES), lambda c, i: (i, c)),
            pl.BlockSpec((8, cw * LANES), lambda c, i: (jnp.maximum(i * (tm // 8) - 1, 0), c)),
            pl.BlockSpec((GDN_CONV, cw * LANES), lambda c, i: (0, c)),
        ],
        out_specs=pl.BlockSpec((cw, tm, LANES), lambda c, i: (c, i, 0)),
        out_shape=jax.ShapeDtypeStruct((n_tiles, t, LANES), F32),
        compiler_params=_cparams(("parallel", "parallel")),
        name="gdn_conv",
    )(proj, proj, conv_w)


def _gdn_gate_kernel(x_ref, alog_ref, dtb_ref, lc_ref, lf_ref, o_ref):
    x = x_ref[...]
    lane = lax.broadcasted_iota(jnp.int32, x.shape, 1)
    beta = _sigmoid(x)
    z = x + dtb_ref[...]
    softplus = jnp.maximum(z, 0.0) + jnp.log(1.0 + jnp.exp(-jnp.abs(z)))
    g = -jnp.exp(alog_ref[...]) * softplus
    gcum = jnp.zeros_like(x)
    gtot = jnp.zeros_like(x)
    for piece in _split3(g):
        gcum = gcum + jnp.dot(lc_ref[...], piece, preferred_element_type=F32)
        gtot = gtot + jnp.dot(lf_ref[...], piece, preferred_element_type=F32)
    o_ref[...] = jnp.where(lane < GDN_V_HEADS, beta, jnp.where(lane < 2 * GDN_V_HEADS, gcum, gtot))


def gdn_gates(ba, alog_vec, dtb_vec):
    t = ba.shape[0]
    tm = GDN_GROUP
    r = np.arange(tm)
    same = (r[:, None] // GDN_CHUNK) == (r[None, :] // GDN_CHUNK)
    lc = jnp.asarray(same & (r[None, :] <= r[:, None]), BF16)
    lf = jnp.asarray(same, BF16)
    return pl.pallas_call(
        _gdn_gate_kernel,
        grid=(t // tm,),
        in_specs=[
            pl.BlockSpec((tm, LANES), lambda i: (i, 0)),
            pl.BlockSpec((1, LANES), lambda i: (0, 0)),
            pl.BlockSpec((1, LANES), lambda i: (0, 0)),
            pl.BlockSpec((tm, tm), lambda i: (0, 0)),
            pl.BlockSpec((tm, tm), lambda i: (0, 0)),
        ],
        out_specs=pl.BlockSpec((tm, LANES), lambda i: (i, 0)),
        out_shape=jax.ShapeDtypeStruct((t, LANES), F32),
        compiler_params=_cparams(("parallel",)),
        name="gdn_gates",
    )(ba, alog_vec, dtb_vec, lc, lf)


def _gdn_scan_kernel(q_ref, k_ref, kt_ref, v_ref, slab_ref, slabt_ref, o_ref, s_ref):
    n = pl.program_id(1)

    @pl.when(n == 0)
    def _():
        s_ref[...] = jnp.zeros_like(s_ref)

    L = GDN_GROUP
    C = GDN_CHUNK
    D = GDN_HEAD_DIM
    slab = slab_ref[...]
    slabt = slabt_ref[...]
    row = lax.broadcasted_iota(jnp.int32, (L, L), 0)
    col = lax.broadcasted_iota(jnp.int32, (L, L), 1)
    same = _div_pow2(row, C) == _div_pow2(col, C)
    causal = same & (col <= row)
    strict = same & (col < row)
    eye = (row == col).astype(F32)
    lane = lax.broadcasted_iota(jnp.int32, (L, LANES), 1)
    sub = lax.broadcasted_iota(jnp.int32, (LANES, L), 0)

    def column(idx):
        return jnp.sum(jnp.where(lane == idx, slab, 0.0), axis=1, keepdims=True)

    def rowvec(idx):
        return jnp.sum(jnp.where(sub == idx, slabt, 0.0), axis=0, keepdims=True)

    colk = lax.broadcasted_iota(jnp.int32, (D, L), 1)
    slots = range(2 * GDN_SCAN_HEADS)

    q = [q_ref[hq] for hq in range(GDN_SCAN_HEADS)]
    k = [k_ref[hq] for hq in range(GDN_SCAN_HEADS)]
    kt = [kt_ref[hq] for hq in range(GDN_SCAN_HEADS)]
    kk = [_mm(k[hq], kt[hq]) for hq in range(GDN_SCAN_HEADS)]
    qk = [_mm(q[hq], kt[hq]) for hq in range(GDN_SCAN_HEADS)]
    beta_c, gc_c, gt_c, decay, bp, inv, kdt = [], [], [], [], [], [], []
    for slot in slots:
        hq = slot // 2
        hv = 2 * (pl.program_id(0) * GDN_SCAN_HEADS + hq) + slot % 2
        beta_c.append(column(hv))
        gc_c.append(column(GDN_V_HEADS + hv))
        gt_c.append(column(2 * GDN_V_HEADS + hv))
        gc_r = rowvec(GDN_V_HEADS + hv)
        gt_r = rowvec(2 * GDN_V_HEADS + hv)
        decay.append(jnp.where(causal, jnp.exp(jnp.where(causal, gc_c[slot] - gc_r, 0.0)), 0.0))
        bp.append(jnp.where(strict, -(kk[hq] * beta_c[slot]) * decay[slot], 0.0))
        inv.append(eye + bp[slot])
        kdt.append(kt[hq] * jnp.exp(gt_r - gc_r))
    for _ in range(5):
        bp = [_mm(bp[slot], bp[slot]) for slot in slots]
        inv = [inv[slot] + _mm(inv[slot], bp[slot]) for slot in slots]
    u, w, qkm, q_dec = [], [], [], []
    for slot in slots:
        hq = slot // 2
        egc = jnp.exp(gc_c[slot])
        rhs = jnp.concatenate([v_ref[slot] * beta_c[slot], k[hq] * (beta_c[slot] * egc)], axis=1)
        sol = _mm(inv[slot], rhs)
        u.append(sol[:, :D])
        w.append(sol[:, D:])
        qkm.append(jnp.where(causal, qk[hq] * decay[slot], 0.0))
        q_dec.append(q[hq] * egc)
    state = [s_ref[slot] for slot in slots]
    v_done = [[] for _ in slots]
    for c in range(L // C):
        lo, hi = c * C, (c + 1) * C
        r = [_mm(jnp.concatenate([w[slot][lo:hi], q_dec[slot][lo:hi]], axis=0), state[slot]) for slot in slots]
        for slot in slots:
            v_done[slot].append(u[slot][lo:hi] - r[slot][:C])
            v_all = jnp.concatenate(v_done[slot] + [jnp.zeros((L - hi, D), F32)] * (hi < L), axis=0)
            o_ref[lo:hi, slot * D:(slot + 1) * D] = r[slot][C:] + _mm(qkm[slot][lo:hi, :], v_all)
            kdt_c = jnp.where((colk >= lo) & (colk < hi), kdt[slot], 0.0)
            state[slot] = state[slot] * jnp.exp(gt_c[slot][lo:lo + 1, :]) + _mm(kdt_c, v_all)
    for slot in slots:
        s_ref[slot] = state[slot]


def gdn_scan(qkv_hm, kt_hm, slab, slabt):
    t = qkv_hm.shape[1]
    L = GDN_GROUP
    D = GDN_HEAD_DIM
    hq = GDN_SCAN_HEADS
    assert t % L == 0 and GDN_QK_HEADS % hq == 0
    q_blocks = GDN_QK_HEADS // hq
    return pl.pallas_call(
        _gdn_scan_kernel,
        grid=(q_blocks, t // L),
        in_specs=[
            pl.BlockSpec((hq, L, D), lambda j, n: (j, n, 0)),
            pl.BlockSpec((hq, L, D), lambda j, n: (q_blocks + j, n, 0)),
            pl.BlockSpec((hq, D, L), lambda j, n: (j, 0, n)),
            pl.BlockSpec((2 * hq, L, D), lambda j, n: (q_blocks + j, n, 0)),
            pl.BlockSpec((L, LANES), lambda j, n: (n, 0)),
            pl.BlockSpec((LANES, L), lambda j, n: (0, n)),
        ],
        out_specs=pl.BlockSpec((L, 2 * hq * D), lambda j, n: (n, j)),
        out_shape=jax.ShapeDtypeStruct((t, GDN_V_HEADS * D), F32),
        scratch_shapes=[pltpu.VMEM((2 * hq, D, D), F32)],
        compiler_params=_cparams(("parallel", "arbitrary")),
        name="gdn_scan",
    )(qkv_hm, qkv_hm, kt_hm, qkv_hm, slab, slabt)


def _gdn_out_kernel(o_ref, z_ref, onorm_ref, w_ref, gpost_ref, h_ref, out_ref):
    o = o_ref[...]
    z = z_ref[...]
    parts = []
    for hd in range(GDN_V_HEADS):
        seg = o[:, hd * GDN_HEAD_DIM:(hd + 1) * GDN_HEAD_DIM]
        parts.append(seg * lax.rsqrt(jnp.mean(seg * seg, axis=-1, keepdims=True) + NORM_EPS))
    gated = jnp.concatenate(parts, axis=1) * onorm_ref[...] * (z * _sigmoid(z))
    mix = jnp.dot(gated.astype(BF16), w_ref[...], preferred_element_type=F32)
    out_ref[...] = h_ref[...] + _rms(mix, gpost_ref[...])


def gdn_out(o, proj, onorm_tiled, w_out, gpost, h, tm=512):
    t, vw = o.shape
    d = h.shape[1]
    z_blk = (proj.shape[1] - vw) // vw
    assert proj.shape[1] % vw == 0 and t % tm == 0
    return pl.pallas_call(
        _gdn_out_kernel,
        grid=(t // tm,),
        in_specs=[
            pl.BlockSpec((tm, vw), lambda i: (i, 0)),
            pl.BlockSpec((tm, vw), lambda i: (i, z_blk)),
            pl.BlockSpec((1, vw), lambda i: (0, 0)),
            pl.BlockSpec((vw, d), lambda i: (0, 0)),
            pl.BlockSpec((1, d), lambda i: (0, 0)),
            pl.BlockSpec((tm, d), lambda i: (i, 0)),
        ],
        out_specs=pl.BlockSpec((tm, d), lambda i: (i, 0)),
        out_shape=jax.ShapeDtypeStruct((t, d), F32),
        compiler_params=_cparams(("parallel",)),
        name="gdn_out",
    )(o, proj, onorm_tiled, w_out, gpost.reshape(1, d), h)


def _compress_kernel(x_ref, pos_ref, w1_ref, w2_ref, o_ref):
    x = x_ref[...]
    pos = pos_ref[...]
    nc, half = x.shape
    w1 = w1_ref[...]
    first = _mm(x + pos[0:1, :], w1[:half])
    second = _mm(x + pos[1:2, :], w1[half:])
    hid = first + pltpu.roll(second, nc - 1, axis=0)
    hid = hid * _sigmoid(hid)
    out = jnp.dot(hid.astype(BF16), w2_ref[...], preferred_element_type=F32)
    rowi = lax.broadcasted_iota(jnp.int32, out.shape, 0)
    o_ref[...] = jnp.where(rowi < nc - 1, out, 0.0)


def compress(x2, pos2, w1, w2):
    _, g, nc, wdt = x2.shape
    hid = w1.shape[2]
    dh = w2.shape[2]
    return pl.pallas_call(
        _compress_kernel,
        grid=(2, g),
        in_specs=[
            pl.BlockSpec((None, None, nc, wdt), lambda b, gi: (b, gi, 0, 0)),
            pl.BlockSpec((None, 2, wdt), lambda b, gi: (b, 0, 0)),
            pl.BlockSpec((None, 2 * wdt, hid), lambda b, gi: (b, 0, 0)),
            pl.BlockSpec((None, hid, dh), lambda b, gi: (b, 0, 0)),
        ],
        out_specs=pl.BlockSpec((None, None, nc, dh), lambda b, gi: (b, gi, 0, 0)),
        out_shape=jax.ShapeDtypeStruct((2, g, nc, dh), F32),
        compiler_params=_cparams(("parallel", "parallel")),
        name="nsa_compress",
    )(x2, pos2, w1, w2)


def _cmp_topk_kernel(q_ref, kbd_ref, vt_ref, m_ref, *refs, tq, nc, nsel, topk, q0):
    oc_ref, sel_ref, s_ref, p_ref, psum_ref = refs[-5:]
    i = q0 + pl.program_id(1)
    dh = NSA_HEAD_DIM
    rows_per = min(nc, CMP_ROWS)
    q = (q_ref[...] * ((dh ** -0.5) * LOG2_E)).astype(BF16)
    s_ref[...] = jnp.dot(kbd_ref[...], q, preferred_element_type=F32)
    for ch in range(tq // LANES):
        lanes = slice(ch * LANES, (ch + 1) * LANES)
        tpos = i * tq + ch * LANES + lax.broadcasted_iota(jnp.int32, (rows_per, LANES), 1)
        cblk0 = lax.broadcasted_iota(jnp.int32, (rows_per, LANES), 0)
        masks = [(CMP_STRIDE * (cblk0 + c * rows_per) + CMP_BLOCK - 1) <= tpos for c in range(nc // rows_per)]
        for r in range(NSA_REP):
            pieces = [slice(r * nc + c * rows_per, r * nc + (c + 1) * rows_per) for c in range(nc // rows_per)]
            m = jnp.full((1, LANES), -jnp.inf, F32)
            for rows, mask in zip(pieces, masks):
                m = jnp.maximum(m, jnp.max(jnp.where(mask, s_ref[rows, lanes], -jnp.inf), axis=0, keepdims=True))
            m = jnp.where(m > -jnp.inf, m, 0.0)
            total = jnp.zeros((1, LANES), F32)
            for rows, mask in zip(pieces, masks):
                e = jnp.exp2(jnp.where(mask, s_ref[rows, lanes], -jnp.inf) - m)
                s_ref[rows, lanes] = e
                total = total + jnp.sum(e, axis=0, keepdims=True)
            inv = 1.0 / jnp.maximum(total, 1e-30)
            for c, rows in enumerate(pieces):
                p = s_ref[rows, lanes] * inv
                p_ref[rows, lanes] = p.astype(BF16)
                prow = slice(c * rows_per, (c + 1) * rows_per)
                psum_ref[prow, lanes] = p if r == 0 else psum_ref[prow, lanes] + p
    oc_t = jnp.dot(vt_ref[...], p_ref[...], preferred_element_type=F32)
    imp = jnp.zeros((nsel, tq), F32)
    for piece in _split3(psum_ref[...]):
        imp = imp + jnp.dot(m_ref[...], piece, preferred_element_type=F32)
    for ch in range(tq // LANES):
        lanes = slice(ch * LANES, (ch + 1) * LANES)
        oc_ref[lanes, :] = oc_t[:, lanes].T
        t1 = i * tq + ch * LANES + lax.broadcasted_iota(jnp.int32, (nsel, LANES), 1)
        blk = lax.broadcasted_iota(jnp.int32, (nsel, LANES), 0)
        cur = _div_pow2(t1, SEL_BLOCK)
        forced = (blk == 0) | (blk == cur) | (blk == cur - 1)
        valid = blk * SEL_BLOCK <= t1
        score = jnp.where(valid, jnp.where(forced, FORCED_SCORE, imp[:, lanes]), -jnp.inf)
        blkf = blk.astype(F32)
        work = score
        for _ in range(topk):
            mx = jnp.max(work, axis=0, keepdims=True)
            first = jnp.min(jnp.where(work == mx, blkf, float(nsel)), axis=0, keepdims=True)
            work = jnp.where(blkf == first, -jnp.inf, work)
        sel_ref[:nsel, lanes] = jnp.where((score > -jnp.inf) & (work == -jnp.inf), 1.0, 0.0).astype(sel_ref.dtype)
        if sel_ref.shape[0] > nsel:
            sel_ref[nsel:, lanes] = jnp.zeros((sel_ref.shape[0] - nsel, LANES), sel_ref.dtype)


def cmp_topk(q_t, kbd, v_t, imp_mat_t, nsel_all, topk, q0, nq, tq, prev):
    qw, t = q_t.shape
    g = kbd.shape[0]
    nc = kbd.shape[1] // NSA_REP
    nsel = imp_mat_t.shape[0]
    assert nc % min(nc, CMP_ROWS) == 0
    in_specs = [
        pl.BlockSpec((NSA_GW, tq), lambda gi, i: (gi, q0 + i)),
        pl.BlockSpec((None, NSA_REP * nc, NSA_GW), lambda gi, i: (gi, 0, 0)),
        pl.BlockSpec((None, NSA_GW, NSA_REP * nc), lambda gi, i: (gi, 0, 0)),
        pl.BlockSpec((nsel, nc), lambda gi, i: (0, 0)),
    ]
    in_specs += [pl.BlockSpec(memory_space=pl.ANY)] * 2
    args = [q_t, kbd, v_t, imp_mat_t, *prev]
    aliases = {len(args) - 2: 0, len(args) - 1: 1}
    return pl.pallas_call(
        functools.partial(_cmp_topk_kernel, tq=tq, nc=nc, nsel=nsel, topk=topk, q0=q0),
        grid=(g, nq),
        in_specs=in_specs,
        out_specs=[
            pl.BlockSpec((tq, NSA_GW), lambda gi, i: (q0 + i, gi)),
            pl.BlockSpec((None, nsel_all, tq), lambda gi, i: (gi, 0, q0 + i)),
        ],
        out_shape=[
            jax.ShapeDtypeStruct((t, qw), F32),
            jax.ShapeDtypeStruct((g, nsel_all, t), F32),
        ],
        input_output_aliases=aliases,
        scratch_shapes=[
            pltpu.VMEM((NSA_REP * nc, tq), F32),
            pltpu.VMEM((NSA_REP * nc, tq), BF16),
            pltpu.VMEM((nc, tq), F32),
        ],
        compiler_params=_cparams(("parallel", "parallel")),
        name="nsa_cmp_topk",
    )(*args)


def _flash_kernel(q_ref, k_ref, v_ref, sel_ref, o_ref,
                  m_ref, l_ref, alpha_ref, acc_ref, s_ref, p_ref, bias_ref, kbd_ref, vbd_ref, qs_ref, *, tq, kt):
    qi = pl.program_id(1)
    dh = NSA_HEAD_DIM
    n_sub = tq // FLASH_SUB
    n_blk = kt // SEL_BLOCK

    m_ref[...] = jnp.full_like(m_ref, NEG_INIT)
    l_ref[...] = jnp.zeros_like(l_ref)
    acc_ref[...] = jnp.zeros_like(acc_ref)
    qs_ref[...] = (q_ref[...] * ((dh ** -0.5) * LOG2_E)).astype(BF16)
    one_row = lax.broadcasted_iota(jnp.int32, (FLASH_SUM_ROWS, NSA_REP * kt), 0)
    one_head = _div_pow2(lax.broadcasted_iota(jnp.int32, (FLASH_SUM_ROWS, NSA_REP * kt), 1), kt)
    vbd_ref[NSA_GW:, :] = (one_row == one_head).astype(BF16)

    def key_tile(ki, positional, first_sub=0):
        k4 = k_ref[ki]
        kseg = _div_pow2(lax.broadcasted_iota(jnp.int32, k4.shape, 1), dh)
        v4 = v_ref[ki]
        vblk = _div_pow2(lax.broadcasted_iota(jnp.int32, v4.shape, 0), dh)
        for r in range(NSA_REP):
            kbd_ref[r * kt:(r + 1) * kt, :] = jnp.where(kseg == r, k4, jnp.zeros_like(k4))
            vbd_ref[:NSA_GW, r * kt:(r + 1) * kt] = jnp.where(vblk == r, v4, jnp.zeros_like(v4))

        def scores(sub):
            c0 = sub * FLASH_SUB
            cols = slice(c0, c0 + FLASH_SUB)
            s_ref[:, cols] = jnp.dot(kbd_ref[...], qs_ref[:, cols],
                                     preferred_element_type=F32)
            for jb in range(n_blk):
                rows = slice(jb * SEL_BLOCK, (jb + 1) * SEL_BLOCK)
                picked = sel_ref[pl.ds(ki * n_blk + jb, 1), cols] > 0.5
                if positional:
                    tpos = qi * tq + c0 + lax.broadcasted_iota(jnp.int32, (SEL_BLOCK, FLASH_SUB), 1)
                    kpos = ki * kt + jb * SEL_BLOCK + lax.broadcasted_iota(jnp.int32, (SEL_BLOCK, FLASH_SUB), 0)
                    allowed = picked & (kpos <= tpos)
                else:
                    allowed = jnp.broadcast_to(picked, (SEL_BLOCK, FLASH_SUB))
                bias_ref[rows, cols] = jnp.where(allowed, 0.0, -jnp.inf)

        scores(first_sub)
        for sub in range(first_sub, n_sub):
            c0 = sub * FLASH_SUB
            cols = slice(c0, c0 + FLASH_SUB)
            if sub + 1 < n_sub:
                scores(sub + 1)
            for ch in range(FLASH_SUB // LANES):
                lanes = slice(c0 + ch * LANES, c0 + (ch + 1) * LANES)
                bias = bias_ref[:, lanes]
                for r in range(NSA_REP):
                    x = s_ref[r * kt:(r + 1) * kt, lanes] + bias
                    m_prev = m_ref[r:r + 1, lanes]
                    m_new = jnp.maximum(m_prev, jnp.max(x, axis=0, keepdims=True))
                    m_ref[r:r + 1, lanes] = m_new
                    alpha_ref[r:r + 1, lanes] = jnp.exp2(m_prev - m_new)
                    p_ref[r * kt:(r + 1) * kt, lanes] = jnp.exp2(x - m_new).astype(BF16)
            pv = jnp.dot(vbd_ref[...], p_ref[:, cols], preferred_element_type=F32)
            for r in range(NSA_REP):
                hd = slice(r * dh, (r + 1) * dh)
                alpha = alpha_ref[r:r + 1, cols]
                acc_ref[hd, cols] = acc_ref[hd, cols] * alpha + pv[hd]
                l_ref[r:r + 1, cols] = l_ref[r:r + 1, cols] * alpha + pv[NSA_GW + r:NSA_GW + r + 1]

    def before_diagonal(ki, carry):
        key_tile(ki, False)
        return carry

    diag = qi * (tq // kt)
    lax.fori_loop(0, diag, before_diagonal, 0)
    for d in range(tq // kt):
        key_tile(diag + d, True, first_sub=d * kt // FLASH_SUB)

    for ch in range(tq // LANES):
        lanes = slice(ch * LANES, (ch + 1) * LANES)
        out_t = jnp.concatenate(
            [acc_ref[r * dh:(r + 1) * dh, lanes] / l_ref[r:r + 1, lanes] for r in range(NSA_REP)], axis=0)
        o_ref[lanes, :] = out_t.T


def flash_branch(q_t, k4, vt4, sel_t, tq, kt):
    qw, t = q_t.shape
    g, nk = k4.shape[:2]
    nsel = sel_t.shape[1]
    assert t % tq == 0 and tq % kt == 0 and kt % SEL_BLOCK == 0 and tq % FLASH_SUB == 0 and kt % FLASH_SUB == 0
    return pl.pallas_call(
        functools.partial(_flash_kernel, tq=tq, kt=kt),
        grid=(g, t // tq),
        in_specs=[
            pl.BlockSpec((NSA_GW, tq), lambda gi, i: (gi, i)),
            pl.BlockSpec((None, nk, kt, NSA_GW), lambda gi, i: (gi, 0, 0, 0), pipeline_mode=pl.Buffered(1)),
            pl.BlockSpec((None, nk, NSA_GW, kt), lambda gi, i: (gi, 0, 0, 0), pipeline_mode=pl.Buffered(1)),
            pl.BlockSpec((None, nsel, tq), lambda gi, i: (gi, 0, i)),
        ],
        out_specs=pl.BlockSpec((tq, NSA_GW), lambda gi, i: (i, gi)),
        scratch_shapes=[
            pltpu.VMEM((8, tq), F32),
            pltpu.VMEM((8, tq), F32),
            pltpu.VMEM((8, tq), F32),
            pltpu.VMEM((NSA_GW, tq), F32),
            pltpu.VMEM((NSA_REP * kt, tq), F32),
            pltpu.VMEM((NSA_REP * kt, tq), BF16),
            pltpu.VMEM((kt, tq), F32),
            pltpu.VMEM((NSA_REP * kt, NSA_GW), BF16),
            pltpu.VMEM((NSA_GW + FLASH_SUM_ROWS, NSA_REP * kt), BF16),
            pltpu.VMEM((NSA_GW, tq), BF16),
        ],
        out_shape=jax.ShapeDtypeStruct((t, qw), F32),
        compiler_params=_cparams(("parallel", "arbitrary")),
        name="nsa_selected",
    )(q_t, k4, vt4, sel_t)


def _window_kernel(q_ref, k_ref, v_ref, o_ref, s_ref, p_ref, bias_ref, kbd_ref, vbd_ref, *, tq, kt):
    i = pl.program_id(1)
    dh = NSA_HEAD_DIM
    nw = WINDOW // kt + 1
    span = nw * kt
    qs = (q_ref[...] * ((dh ** -0.5) * LOG2_E)).astype(BF16)
    for w in range(nw):
        tile = i - (nw - 1) + w
        k4 = k_ref[jnp.maximum(tile, 0)]
        kseg = _div_pow2(lax.broadcasted_iota(jnp.int32, k4.shape, 1), dh)
        v4 = v_ref[jnp.maximum(tile, 0)]
        vblk = _div_pow2(lax.broadcasted_iota(jnp.int32, v4.shape, 0), dh)
        for r in range(NSA_REP):
            at = r * span + w * kt
            kbd_ref[at:at + kt, :] = jnp.where(kseg == r, k4, jnp.zeros_like(k4))
            vbd_ref[:NSA_GW, at:at + kt] = jnp.where(vblk == r, v4, jnp.zeros_like(v4))
        tpos = i * tq + lax.broadcasted_iota(jnp.int32, (kt, tq), 1)
        kpos = tile * kt + lax.broadcasted_iota(jnp.int32, (kt, tq), 0)
        allowed = (kpos >= 0) & (kpos <= tpos) & (kpos > tpos - WINDOW)
        bias_ref[w * kt:(w + 1) * kt, :] = jnp.where(allowed, 0.0, -jnp.inf)
    one_row = lax.broadcasted_iota(jnp.int32, (FLASH_SUM_ROWS, NSA_REP * span), 0)
    one_col = lax.broadcasted_iota(jnp.int32, (FLASH_SUM_ROWS, NSA_REP * span), 1)
    vbd_ref[NSA_GW:, :] = ((one_col >= one_row * span) & (one_col < (one_row + 1) * span)).astype(BF16)

    s_ref[...] = jnp.dot(kbd_ref[...], qs, preferred_element_type=F32)
    for ch in range(tq // LANES):
        lanes = slice(ch * LANES, (ch + 1) * LANES)
        for r in range(NSA_REP):
            m = jnp.full((1, LANES), -jnp.inf, F32)
            for w in range(nw):
                rows = slice(r * span + w * kt, r * span + (w + 1) * kt)
                m = jnp.maximum(m, jnp.max(s_ref[rows, lanes] + bias_ref[w * kt:(w + 1) * kt, lanes],
                                           axis=0, keepdims=True))
            m = jnp.where(m > -jnp.inf, m, 0.0)
            for w in range(nw):
                rows = slice(r * span + w * kt, r * span + (w + 1) * kt)
                x = s_ref[rows, lanes] + bias_ref[w * kt:(w + 1) * kt, lanes]
                p_ref[rows, lanes] = jnp.exp2(x - m).astype(BF16)
    pv = jnp.dot(vbd_ref[...], p_ref[...], preferred_element_type=F32)
    for ch in range(tq // LANES):
        lanes = slice(ch * LANES, (ch + 1) * LANES)
        out_t = jnp.concatenate(
            [pv[r * dh:(r + 1) * dh, lanes] / jnp.maximum(pv[NSA_GW + r:NSA_GW + r + 1, lanes], 1e-30)
             for r in range(NSA_REP)], axis=0)
        o_ref[lanes, :] = out_t.T


def window_branch(q_t, k4, vt4, tq):
    qw, t = q_t.shape
    g, nk, kt, _ = k4.shape
    assert tq == kt and WINDOW % kt == 0 and t % tq == 0
    span = (WINDOW // kt + 1) * kt
    return pl.pallas_call(
        functools.partial(_window_kernel, tq=tq, kt=kt),
        grid=(g, t // tq),
        in_specs=[
            pl.BlockSpec((NSA_GW, tq), lambda gi, i: (gi, i)),
            pl.BlockSpec((None, nk, kt, NSA_GW), lambda gi, i: (gi, 0, 0, 0)),
            pl.BlockSpec((None, nk, NSA_GW, kt), lambda gi, i: (gi, 0, 0, 0)),
        ],
        out_specs=pl.BlockSpec((tq, NSA_GW), lambda gi, i: (i, gi)),
        scratch_shapes=[
            pltpu.VMEM((NSA_REP * span, tq), F32),
            pltpu.VMEM((NSA_REP * span, tq), BF16),
            pltpu.VMEM((span, tq), F32),
            pltpu.VMEM((NSA_REP * span, NSA_GW), BF16),
            pltpu.VMEM((NSA_GW + FLASH_SUM_ROWS, NSA_REP * span), BF16),
        ],
        out_shape=jax.ShapeDtypeStruct((t, qw), F32),
        compiler_params=_cparams(("parallel", "parallel")),
        name="nsa_window",
    )(q_t, k4, vt4)


def _nsa_out_kernel(oc_ref, os_ref, ow_ref, gl_ref, eg_ref, w_ref, gpost_ref, h_ref, out_ref):
    pieces = _split3(_sigmoid(gl_ref[...]))
    mixed = jnp.zeros(oc_ref.shape, F32)
    for b, br_ref in enumerate((oc_ref, os_ref, ow_ref)):
        gfull = jnp.zeros(oc_ref.shape, F32)
        for piece in pieces:
            gfull = gfull + jnp.dot(piece, eg_ref[b], preferred_element_type=F32)
        mixed = mixed + gfull * br_ref[...]
    mix = jnp.dot(mixed.astype(BF16), w_ref[...], preferred_element_type=F32)
    out_ref[...] = h_ref[...] + _rms(mix, gpost_ref[...])


def nsa_out(oc, osel, ow, gate_logits, expand, w_o, gpost, h, tm=512):
    t, qw = oc.shape
    d = h.shape[1]
    assert t % tm == 0
    row = lambda w: pl.BlockSpec((tm, w), lambda i: (i, 0))
    return pl.pallas_call(
        _nsa_out_kernel,
        grid=(t // tm,),
        in_specs=[
            row(qw), row(qw), row(qw), row(LANES),
            pl.BlockSpec((3, LANES, qw), lambda i: (0, 0, 0)),
            pl.BlockSpec((qw, d), lambda i: (0, 0)),
            pl.BlockSpec((1, d), lambda i: (0, 0)),
            row(d),
        ],
        out_specs=row(d),
        out_shape=jax.ShapeDtypeStruct((t, d), F32),
        compiler_params=_cparams(("parallel",)),
        name="nsa_out",
    )(oc, osel, ow, gate_logits, expand, w_o, gpost.reshape(1, d), h)


def _pad_cols(w, n):
    return jnp.pad(w, ((0, 0), (0, n - w.shape[1])))


def _importance_matrix(nc, nsel):
    r = SEL_BLOCK // CMP_STRIDE
    c = CMP_BLOCK // CMP_STRIDE
    mat = np.zeros((nc, nsel), np.float32)
    for kblk in range(nsel):
        for m in range(r):
            for n in range(c):
                j = r * kblk + m - n
                if 0 <= j < nc - 1:
                    mat[j, kblk] += 1.0
    return jnp.asarray(mat, BF16)


def _gate_expand():
    e = np.zeros((3, LANES, NSA_GROUPS * NSA_GW), np.float32)
    for head in range(NSA_GROUPS * NSA_REP):
        for b in range(3):
            e[b, head * 3 + b, head * NSA_HEAD_DIM:(head + 1) * NSA_HEAD_DIM] = 1.0
    return jnp.asarray(e, BF16)


def _block_diag_kv(k_cmp, v_cmp):
    g, nc, dh = k_cmp.shape
    eye = jnp.eye(NSA_REP, dtype=bool)
    kct = jnp.swapaxes(k_cmp, 1, 2)
    kbd = jnp.where(eye[None, :, None, :, None], kct[:, None, :, None, :], 0.0)
    vbd = jnp.where(eye[None, :, None, :, None], v_cmp[:, None, :, None, :], 0.0)
    return (kbd.reshape(g, NSA_REP * dh, NSA_REP * nc).astype(BF16),
            vbd.reshape(g, NSA_REP * nc, NSA_REP * dh).astype(BF16))


def kernel(x, p, mix_pre_norm, mix_post_norm, ffn_pre_norm, ffn_post_norm, gdn_w_in, gdn_conv_w, gdn_a_log,
           gdn_dt_bias, gdn_o_norm, gdn_w_out, kv_norm, kv_w, cmp_pos, cmp_w1, cmp_w2, nsa_w_qg, nsa_w_o,
           ffn_w_in, ffn_w_out, ple_w_in, ple_w_gate):
    depth = p.shape[0]
    n_a = gdn_w_in.shape[0]
    t = x.shape[1]
    h = x[0]
    fh = ffn_w_out.shape[1]
    conv_w_cols = gdn_conv_w.shape[2]
    vw = GDN_V_HEADS * GDN_HEAD_DIM
    main_w = conv_w_cols + vw

    def channel_and_ple(h, i):
        return ffn_ple(h, ffn_pre_norm[i], ffn_w_in[i].astype(BF16), ffn_w_out[i].astype(BF16),
                       ffn_post_norm[i], p[i, 0], ple_w_in[i].astype(BF16), ple_w_gate[i].astype(BF16))

    for i in range(n_a):
        w_in = gdn_w_in[i]
        w_beta = w_in[:, main_w:main_w + GDN_V_HEADS]
        w_a = w_in[:, main_w + GDN_V_HEADS:]
        w_small = _pad_cols(jnp.concatenate([w_beta, w_a, w_a], axis=1), LANES).astype(BF16)
        proj = norm_matmul(h, mix_pre_norm[i], w_in.astype(BF16), n=main_w)
        ba = norm_matmul(h, mix_pre_norm[i], w_small)
        pad_vec = lambda v: jnp.pad(v, (GDN_V_HEADS, LANES - 2 * GDN_V_HEADS))
        alog_vec = (pad_vec(gdn_a_log[i]) + jnp.pad(gdn_a_log[i], (2 * GDN_V_HEADS, LANES - 3 * GDN_V_HEADS)))
        dtb_vec = (pad_vec(gdn_dt_bias[i]) + jnp.pad(gdn_dt_bias[i], (2 * GDN_V_HEADS, LANES - 3 * GDN_V_HEADS)))
        slab = gdn_gates(ba, alog_vec.reshape(1, LANES), dtb_vec.reshape(1, LANES))
        qkv_hm = gdn_conv(proj, gdn_conv_w[i])
        kt_hm = jnp.swapaxes(qkv_hm[GDN_QK_HEADS:2 * GDN_QK_HEADS], 1, 2)
        o = gdn_scan(qkv_hm, kt_hm, slab, slab.T)
        onorm_tiled = jnp.tile(gdn_o_norm[i], GDN_V_HEADS).reshape(1, vw)
        h = gdn_out(o, proj, onorm_tiled, gdn_w_out[i].astype(BF16), mix_post_norm[i], h)
        h = channel_and_ple(h, i)

    g = NSA_GROUPS
    dh = NSA_HEAD_DIM
    kv = norm_matmul(h, kv_norm, kv_w.astype(BF16), tn=768)
    kv6 = jnp.transpose(kv.reshape(t, 6, g, dh), (1, 2, 0, 3))
    nc = t // CMP_STRIDE
    nsel = t // SEL_BLOCK
    x2 = kv6[0:2].reshape(2, g, nc, CMP_STRIDE * dh)
    pos2 = cmp_pos.reshape(2, 2, CMP_STRIDE * dh)
    cmp_out = compress(x2, pos2, cmp_w1.astype(BF16), cmp_w2.astype(BF16))
    imp_mat_t = _importance_matrix(nc, nsel).T
    nq_r = t // (CMP_TQ * CMP_RANGES)
    assert t % (CMP_TQ * CMP_RANGES) == 0 and nc % CMP_RANGES == 0 and nsel % (8 * CMP_RANGES) == 0
    cmp_ranges = []
    for rg in range(CMP_RANGES):
        nc_r, nsel_r = (rg + 1) * nc // CMP_RANGES, (rg + 1) * nsel // CMP_RANGES
        vbd_t, kbd = _block_diag_kv(cmp_out[1][:, :nc_r], cmp_out[0][:, :nc_r])
        cmp_ranges.append((kbd, vbd_t, imp_mat_t[:nsel_r, :nc_r]))
    kt = FLASH_KT
    tiles = lambda a: a.reshape(g, t // kt, kt, dh)
    rep_k = lambda a: jnp.tile(tiles(a), (1, 1, 1, NSA_REP)).astype(BF16)
    rep_t = lambda a: jnp.tile(jnp.swapaxes(tiles(a), 2, 3), (1, 1, NSA_REP, 1)).astype(BF16)
    k_slc, v_slc_t = rep_k(kv6[2]), rep_t(kv6[3])
    k_win, v_win_t = rep_k(kv6[4]), rep_t(kv6[5])
    expand = _gate_expand()

    for i in range(n_a, depth):
        j = i - n_a
        qw = g * NSA_GW
        w_qg = nsa_w_qg[j]
        q_t = norm_matmul_t(h, mix_pre_norm[i], w_qg[:, :qw].T.astype(BF16))
        gate_logits = norm_matmul(h, mix_pre_norm[i], _pad_cols(w_qg[:, qw:], LANES).astype(BF16))
        cmp_res = (jnp.zeros((t, qw), F32), jnp.zeros((g, nsel, t), F32))
        for rg, (kbd, vbd_t, imp_r) in enumerate(cmp_ranges):
            cmp_res = cmp_topk(q_t, kbd, vbd_t, imp_r, nsel, min(SEL_TOPK, nsel), rg * nq_r, nq_r, CMP_TQ,
                               prev=cmp_res)
        o_c, sel_t = cmp_res
        o_s = flash_branch(q_t, k_slc, v_slc_t, sel_t, tq=min(FLASH_TQ, t), kt=kt)
        o_w = window_branch(q_t, k_win, v_win_t, tq=kt)
        h = nsa_out(o_c, o_s, o_w, gate_logits, expand, nsa_w_o[j].astype(BF16), mix_post_norm[i], h)
        h = channel_and_ple(h, i)
    return h[None]
```

```python
import functools

import numpy as np
import jax
import jax.numpy as jnp
from jax import lax
from jax.experimental import pallas as pl
from jax.experimental.pallas import tpu as pltpu

F32 = jnp.float32
BF16 = jnp.bfloat16

NORM_EPS = 1e-6
L2_EPS = 1e-6
GDN_QK_HEADS = 8
GDN_V_HEADS = 16
GDN_HEAD_DIM = 128
GDN_CONV = 4
GDN_CHUNK = 64
GDN_GROUP = 256
GDN_SCAN_HEADS = 8
GDN_CONV_HEADS = 4
NSA_GROUPS = 4
NSA_REP = 4
NSA_HEAD_DIM = 64
NSA_GW = NSA_REP * NSA_HEAD_DIM
CMP_BLOCK = 32
CMP_STRIDE = 16
SEL_BLOCK = 64
SEL_TOPK = 16
WINDOW = 512
FORCED_SCORE = 1e4
LANES = 128
NEG_INIT = -(2.0 ** 100)
LOG2_E = 1.4426950408889634
FLASH_TQ = 2048
FLASH_KT = 256
FLASH_SUB = 256
FLASH_SUM_ROWS = 16
CMP_ROWS = 256
CMP_TQ = 256
CMP_RANGES = 4

VMEM_LIMIT = 56 * 1024 * 1024
FFN_VMEM_LIMIT = 60 * 1024 * 1024


def _cparams(sem, vmem_limit=VMEM_LIMIT):
    return pltpu.CompilerParams(dimension_semantics=sem, vmem_limit_bytes=vmem_limit)


def _rms(x, gain):
    return x * lax.rsqrt(jnp.mean(x * x, axis=-1, keepdims=True) + NORM_EPS) * gain


def _mm(a, b):
    return jnp.dot(a.astype(BF16), b.astype(BF16), preferred_element_type=F32)


def _sigmoid(x):
    return 1.0 / (1.0 + jnp.exp(-x))


def _div_pow2(x, d):
    shift = d.bit_length() - 1
    assert d == 1 << shift
    return jnp.right_shift(x, shift)


def _split3(x):
    a = x.astype(BF16)
    r = x - a.astype(F32)
    b = r.astype(BF16)
    c = (r - b.astype(F32)).astype(BF16)
    return a, b, c


def _norm_matmul_kernel(x_ref, g_ref, w_ref, o_ref, xn_ref):
    @pl.when(pl.program_id(1) == 0)
    def _():
        xn_ref[...] = _rms(x_ref[...], g_ref[...]).astype(BF16)

    o_ref[...] = jnp.dot(xn_ref[...], w_ref[...], preferred_element_type=F32)


def _norm_matmul_t_kernel(x_ref, g_ref, wt_ref, o_ref, xn_ref):
    @pl.when(pl.program_id(1) == 0)
    def _():
        xn_ref[...] = _rms(x_ref[...], g_ref[...]).astype(BF16)

    o_ref[...] = lax.dot_general(wt_ref[...], xn_ref[...], (((1,), (1,)), ((), ())), preferred_element_type=F32)


def norm_matmul_t(h, gain, w_t, tm=1024, tn=1024):
    t, d = h.shape
    n = w_t.shape[0]
    tn = min(tn, n)
    assert t % tm == 0 and n % tn == 0
    return pl.pallas_call(
        _norm_matmul_t_kernel,
        grid=(t // tm, n // tn),
        in_specs=[
            pl.BlockSpec((tm, d), lambda i, j: (i, 0)),
            pl.BlockSpec((1, d), lambda i, j: (0, 0)),
            pl.BlockSpec((tn, d), lambda i, j: (j, 0)),
        ],
        out_specs=pl.BlockSpec((tn, tm), lambda i, j: (j, i)),
        out_shape=jax.ShapeDtypeStruct((n, t), F32),
        scratch_shapes=[pltpu.VMEM((tm, d), BF16)],
        compiler_params=_cparams(("parallel", "arbitrary")),
        name="norm_matmul_t",
    )(h, gain.reshape(1, d), w_t)


def norm_matmul(h, gain, w, tm=1024, tn=1024, n=None):
    t, d = h.shape
    n = w.shape[1] if n is None else n
    tn = min(tn, n)
    assert t % tm == 0 and n % tn == 0
    return pl.pallas_call(
        _norm_matmul_kernel,
        grid=(t // tm, n // tn),
        in_specs=[
            pl.BlockSpec((tm, d), lambda i, j: (i, 0)),
            pl.BlockSpec((1, d), lambda i, j: (0, 0)),
            pl.BlockSpec((d, tn), lambda i, j: (0, j)),
        ],
        out_specs=pl.BlockSpec((tm, tn), lambda i, j: (i, j)),
        out_shape=jax.ShapeDtypeStruct((t, n), F32),
        scratch_shapes=[pltpu.VMEM((tm, d), BF16)],
        compiler_params=_cparams(("parallel", "arbitrary")),
        name="norm_matmul",
    )(h, gain.reshape(1, d), w)


def _ffn_ple_kernel(h_ref, gpre_ref, wg_ref, wu_ref, wo_ref, gpost_ref, p_ref, wple_ref, wgt_ref,
                    o_ref, xn_ref, acc_ref, *, nf):
    f = pl.program_id(1)

    @pl.when(f == 0)
    def _():
        xn_ref[...] = _rms(h_ref[...], gpre_ref[...]).astype(BF16)
        acc_ref[...] = jnp.zeros_like(acc_ref)

    xn = xn_ref[...]
    tf = wg_ref.shape[1]
    half = (tf // LANES // 2) * LANES
    for a, b in ((0, half), (half, tf)):
        gate = jnp.dot(xn, wg_ref[:, a:b], preferred_element_type=F32)
        up = jnp.dot(xn, wu_ref[:, a:b], preferred_element_type=F32)
        act = gate * _sigmoid(gate) * up
        acc_ref[...] += jnp.dot(act.astype(BF16), wo_ref[a:b, :], preferred_element_type=F32)

    @pl.when(f == nf - 1)
    def _():
        h2 = h_ref[...] + _rms(acc_ref[...], gpost_ref[...])
        emb = jnp.dot(p_ref[...].astype(BF16), wple_ref[...], preferred_element_type=F32)
        gt = _sigmoid(jnp.dot(h2.astype(BF16), wgt_ref[...], preferred_element_type=F32))
        o_ref[...] = h2 + emb * gt


def ffn_ple(h, gpre, w_in, w_out, gpost, p, w_ple, w_plegate, tm=1024, tf=1408):
    t, d = h.shape
    fh = w_out.shape[0]
    pd = p.shape[1]
    assert t % tm == 0 and fh % tf == 0 and w_in.shape[1] == 2 * fh
    nf = fh // tf
    return pl.pallas_call(
        functools.partial(_ffn_ple_kernel, nf=nf),
        grid=(t // tm, nf),
        in_specs=[
            pl.BlockSpec((tm, d), lambda i, f: (i, 0)),
            pl.BlockSpec((1, d), lambda i, f: (0, 0)),
            pl.BlockSpec((d, tf), lambda i, f: (0, f)),
            pl.BlockSpec((d, tf), lambda i, f: (0, nf + f)),
            pl.BlockSpec((tf, d), lambda i, f: (f, 0)),
            pl.BlockSpec((1, d), lambda i, f: (0, 0)),
            pl.BlockSpec((tm, pd), lambda i, f: (i, 0)),
            pl.BlockSpec((pd, d), lambda i, f: (0, 0), pipeline_mode=pl.Buffered(1)),
            pl.BlockSpec((d, d), lambda i, f: (0, 0), pipeline_mode=pl.Buffered(1)),
        ],
        out_specs=pl.BlockSpec((tm, d), lambda i, f: (i, 0)),
        out_shape=jax.ShapeDtypeStruct((t, d), F32),
        scratch_shapes=[pltpu.VMEM((tm, d), BF16), pltpu.VMEM((tm, d), F32)],
        compiler_params=_cparams(("parallel", "arbitrary"), FFN_VMEM_LIMIT),
        name="ffn_ple",
    )(h, gpre.reshape(1, d), w_in, w_in, w_out, gpost.reshape(1, d), p, w_ple, w_plegate)


def _gdn_conv_kernel(x_ref, halo_ref, w_ref, o_ref, *, tm):
    c = pl.program_id(0)
    i = pl.program_id(1)
    x = x_ref[...]
    halo = jnp.where(i > 0, halo_ref[...], 0.0)
    ext = jnp.concatenate([halo, x], axis=0)
    w = w_ref[...]
    y = x * w[GDN_CONV - 1:GDN_CONV, :]
    for k in range(1, GDN_CONV):
        shifted = pltpu.roll(ext, k, axis=0)[8:8 + tm]
        y = y + shifted * w[GDN_CONV - 1 - k:GDN_CONV - k, :]
    y = y * _sigmoid(y)
    for hd in range(GDN_CONV_HEADS):
        head = c * GDN_CONV_HEADS + hd
        seg = y[:, hd * LANES:(hd + 1) * LANES]
        normed = seg * lax.rsqrt(jnp.sum(seg * seg, axis=-1, keepdims=True) + L2_EPS)
        q_scale = jnp.where(head < GDN_QK_HEADS, GDN_HEAD_DIM ** -0.5, 1.0)
        o_ref[hd] = jnp.where(head < 2 * GDN_QK_HEADS, normed * q_scale, seg)


def gdn_conv(proj, conv_w, tm=1024):
    t = proj.shape[0]
    n_tiles = conv_w.shape[1] // LANES
    cw = GDN_CONV_HEADS
    assert t % tm == 0 and n_tiles % cw == 0
    return pl.pallas_call(
        functools.partial(_gdn_conv_kernel, tm=tm),
        grid=(n_tiles // cw, t // tm),
        in_specs=[
            pl.BlockSpec((tm, cw * LANES), lambda c, i: (i, c)),
            pl.BlockSpec((8, cw * LANES), lambda c, i: (jnp.maximum(i * (tm // 8) - 1, 0), c)),
            pl.BlockSpec((GDN_CONV, cw * LANES), lambda c, i: (0, c)),
        ],
        out_specs=pl.BlockSpec((cw, tm, LANES), lambda c, i: (c, i, 0)),
        out_shape=jax.ShapeDtypeStruct((n_tiles, t, LANES), F32),
        compiler_params=_cparams(("parallel", "parallel")),
        name="gdn_conv",
    )(proj, proj, conv_w)


def _gdn_gate_kernel(x_ref, alog_ref, dtb_ref, lc_ref, lf_ref, o_ref):
    x = x_ref[...]
    lane = lax.broadcasted_iota(jnp.int32, x.shape, 1)
    beta = _sigmoid(x)
    z = x + dtb_ref[...]
    softplus = jnp.maximum(z, 0.0) + jnp.log(1.0 + jnp.exp(-jnp.abs(z)))
    g = -jnp.exp(alog_ref[...]) * softplus
    gcum = jnp.zeros_like(x)
    gtot = jnp.zeros_like(x)
    for piece in _split3(g):
        gcum = gcum + jnp.dot(lc_ref[...], piece, preferred_element_type=F32)
        gtot = gtot + jnp.dot(lf_ref[...], piece, preferred_element_type=F32)
    o_ref[...] = jnp.where(lane < GDN_V_HEADS, beta, jnp.where(lane < 2 * GDN_V_HEADS, gcum, gtot))


def gdn_gates(ba, alog_vec, dtb_vec):
    t = ba.shape[0]
    tm = GDN_GROUP
    r = np.arange(tm)
    same = (r[:, None] // GDN_CHUNK) == (r[None, :] // GDN_CHUNK)
    lc = jnp.asarray(same & (r[None, :] <= r[:, None]), BF16)
    lf = jnp.asarray(same, BF16)
    return pl.pallas_call(
        _gdn_gate_kernel,
        grid=(t // tm,),
        in_specs=[
            pl.BlockSpec((tm, LANES), lambda i: (i, 0)),
            pl.BlockSpec((1, LANES), lambda i: (0, 0)),
            pl.BlockSpec((1, LANES), lambda i: (0, 0)),
            pl.BlockSpec((tm, tm), lambda i: (0, 0)),
            pl.BlockSpec((tm, tm), lambda i: (0, 0)),
        ],
        out_specs=pl.BlockSpec((tm, LANES), lambda i: (i, 0)),
        out_shape=jax.ShapeDtypeStruct((t, LANES), F32),
        compiler_params=_cparams(("parallel",)),
        name="gdn_gates",
    )(ba, alog_vec, dtb_vec, lc, lf)


def _gdn_scan_kernel(q_ref, k_ref, kt_ref, v_ref, slab_ref, slabt_ref, o_ref, s_ref):
    n = pl.program_id(1)

    @pl.when(n == 0)
    def _():
        s_ref[...] = jnp.zeros_like(s_ref)

    L = GDN_GROUP
    C = GDN_CHUNK
    D = GDN_HEAD_DIM
    slab = slab_ref[...]
    slabt = slabt_ref[...]
    row = lax.broadcasted_iota(jnp.int32, (L, L), 0)
    col = lax.broadcasted_iota(jnp.int32, (L, L), 1)
    same = _div_pow2(row, C) == _div_pow2(col, C)
    causal = same & (col <= row)
    strict = same & (col < row)
    eye = (row == col).astype(F32)
    lane = lax.broadcasted_iota(jnp.int32, (L, LANES), 1)
    sub = lax.broadcasted_iota(jnp.int32, (LANES, L), 0)

    def column(idx):
        return jnp.sum(jnp.where(lane == idx, slab, 0.0), axis=1, keepdims=True)

    def rowvec(idx):
        return jnp.sum(jnp.where(sub == idx, slabt, 0.0), axis=0, keepdims=True)

    colk = lax.broadcasted_iota(jnp.int32, (D, L), 1)
    slots = range(2 * GDN_SCAN_HEADS)

    q = [q_ref[hq] for hq in range(GDN_SCAN_HEADS)]
    k = [k_ref[hq] for hq in range(GDN_SCAN_HEADS)]
    kt = [kt_ref[hq] for hq in range(GDN_SCAN_HEADS)]
    kk = [_mm(k[hq], kt[hq]) for hq in range(GDN_SCAN_HEADS)]
    qk = [_mm(q[hq], kt[hq]) for hq in range(GDN_SCAN_HEADS)]
    beta_c, gc_c, gt_c, decay, bp, inv, kdt = [], [], [], [], [], [], []
    for slot in slots:
        hq = slot // 2
        hv = 2 * (pl.program_id(0) * GDN_SCAN_HEADS + hq) + slot % 2
        beta_c.append(column(hv))
        gc_c.append(column(GDN_V_HEADS + hv))
        gt_c.append(column(2 * GDN_V_HEADS + hv))
        gc_r = rowvec(GDN_V_HEADS + hv)
        gt_r = rowvec(2 * GDN_V_HEADS + hv)
        decay.append(jnp.where(causal, jnp.exp(jnp.where(causal, gc_c[slot] - gc_r, 0.0)), 0.0))
        bp.append(jnp.where(strict, -(kk[hq] * beta_c[slot]) * decay[slot], 0.0))
        inv.append(eye + bp[slot])
        kdt.append(kt[hq] * jnp.exp(gt_r - gc_r))
    for _ in range(5):
        bp = [_mm(bp[slot], bp[slot]) for slot in slots]
        inv = [inv[slot] + _mm(inv[slot], bp[slot]) for slot in slots]
    u, w, qkm, q_dec = [], [], [], []
    for slot in slots:
        hq = slot // 2
        egc = jnp.exp(gc_c[slot])
        rhs = jnp.concatenate([v_ref[slot] * beta_c[slot], k[hq] * (beta_c[slot] * egc)], axis=1)
        sol = _mm(inv[slot], rhs)
        u.append(sol[:, :D])
        w.append(sol[:, D:])
        qkm.append(jnp.where(causal, qk[hq] * decay[slot], 0.0))
        q_dec.append(q[hq] * egc)
    state = [s_ref[slot] for slot in slots]
    v_done = [[] for _ in slots]
    for c in range(L // C):
        lo, hi = c * C, (c + 1) * C
        r = [_mm(jnp.concatenate([w[slot][lo:hi], q_dec[slot][lo:hi]], axis=0), state[slot]) for slot in slots]
        for slot in slots:
            v_done[slot].append(u[slot][lo:hi] - r[slot][:C])
            v_all = jnp.concatenate(v_done[slot] + [jnp.zeros((L - hi, D), F32)] * (hi < L), axis=0)
            o_ref[lo:hi, slot * D:(slot + 1) * D] = r[slot][C:] + _mm(qkm[slot][lo:hi, :], v_all)
            kdt_c = jnp.where((colk >= lo) & (colk < hi), kdt[slot], 0.0)
            state[slot] = state[slot] * jnp.exp(gt_c[slot][lo:lo + 1, :]) + _mm(kdt_c, v_all)
    for slot in slots:
        s_ref[slot] = state[slot]


def gdn_scan(qkv_hm, kt_hm, slab, slabt):
    t = qkv_hm.shape[1]
    L = GDN_GROUP
    D = GDN_HEAD_DIM
    hq = GDN_SCAN_HEADS
    assert t % L == 0 and GDN_QK_HEADS % hq == 0
    q_blocks = GDN_QK_HEADS // hq
    return pl.pallas_call(
        _gdn_scan_kernel,
        grid=(q_blocks, t // L),
        in_specs=[
            pl.BlockSpec((hq, L, D), lambda j, n: (j, n, 0)),
            pl.BlockSpec((hq, L, D), lambda j, n: (q_blocks + j, n, 0)),
            pl.BlockSpec((hq, D, L), lambda j, n: (j, 0, n)),
            pl.BlockSpec((2 * hq, L, D), lambda j, n: (q_blocks + j, n, 0)),
            pl.BlockSpec((L, LANES), lambda j, n: (n, 0)),
            pl.BlockSpec((LANES, L), lambda j, n: (0, n)),
        ],
        out_specs=pl.BlockSpec((L, 2 * hq * D), lambda j, n: (n, j)),
        out_shape=jax.ShapeDtypeStruct((t, GDN_V_HEADS * D), F32),
        scratch_shapes=[pltpu.VMEM((2 * hq, D, D), F32)],
        compiler_params=_cparams(("parallel", "arbitrary")),
        name="gdn_scan",
    )(qkv_hm, qkv_hm, kt_hm, qkv_hm, slab, slabt)


def _gdn_out_kernel(o_ref, z_ref, onorm_ref, w_ref, gpost_ref, h_ref, out_ref):
    o = o_ref[...]
    z = z_ref[...]
    parts = []
    for hd in range(GDN_V_HEADS):
        seg = o[:, hd * GDN_HEAD_DIM:(hd + 1) * GDN_HEAD_DIM]
        parts.append(seg * lax.rsqrt(jnp.mean(seg * seg, axis=-1, keepdims=True) + NORM_EPS))
    gated = jnp.concatenate(parts, axis=1) * onorm_ref[...] * (z * _sigmoid(z))
    mix = jnp.dot(gated.astype(BF16), w_ref[...], preferred_element_type=F32)
    out_ref[...] = h_ref[...] + _rms(mix, gpost_ref[...])


def gdn_out(o, proj, onorm_tiled, w_out, gpost, h, tm=512):
    t, vw = o.shape
    d = h.shape[1]
    z_blk = (proj.shape[1] - vw) // vw
    assert proj.shape[1] % vw == 0 and t % tm == 0
    return pl.pallas_call(
        _gdn_out_kernel,
        grid=(t // tm,),
        in_specs=[
            pl.BlockSpec((tm, vw), lambda i: (i, 0)),
            pl.BlockSpec((tm, vw), lambda i: (i, z_blk)),
            pl.BlockSpec((1, vw), lambda i: (0, 0)),
            pl.BlockSpec((vw, d), lambda i: (0, 0)),
            pl.BlockSpec((1, d), lambda i: (0, 0)),
            pl.BlockSpec((tm, d), lambda i: (i, 0)),
        ],
        out_specs=pl.BlockSpec((tm, d), lambda i: (i, 0)),
        out_shape=jax.ShapeDtypeStruct((t, d), F32),
        compiler_params=_cparams(("parallel",)),
        name="gdn_out",
    )(o, proj, onorm_tiled, w_out, gpost.reshape(1, d), h)


def _compress_kernel(x_ref, pos_ref, w1_ref, w2_ref, o_ref):
    x = x_ref[...]
    pos = pos_ref[...]
    nc, half = x.shape
    w1 = w1_ref[...]
    first = _mm(x + pos[0:1, :], w1[:half])
    second = _mm(x + pos[1:2, :], w1[half:])
    hid = first + pltpu.roll(second, nc - 1, axis=0)
    hid = hid * _sigmoid(hid)
    out = jnp.dot(hid.astype(BF16), w2_ref[...], preferred_element_type=F32)
    rowi = lax.broadcasted_iota(jnp.int32, out.shape, 0)
    o_ref[...] = jnp.where(rowi < nc - 1, out, 0.0)


def compress(x2, pos2, w1, w2):
    _, g, nc, wdt = x2.shape
    hid = w1.shape[2]
    dh = w2.shape[2]
    return pl.pallas_call(
        _compress_kernel,
        grid=(2, g),
        in_specs=[
            pl.BlockSpec((None, None, nc, wdt), lambda b, gi: (b, gi, 0, 0)),
            pl.BlockSpec((None, 2, wdt), lambda b, gi: (b, 0, 0)),
            pl.BlockSpec((None, 2 * wdt, hid), lambda b, gi: (b, 0, 0)),
            pl.BlockSpec((None, hid, dh), lambda b, gi: (b, 0, 0)),
        ],
        out_specs=pl.BlockSpec((None, None, nc, dh), lambda b, gi: (b, gi, 0, 0)),
        out_shape=jax.ShapeDtypeStruct((2, g, nc, dh), F32),
        compiler_params=_cparams(("parallel", "parallel")),
        name="nsa_compress",
    )(x2, pos2, w1, w2)


def _cmp_topk_kernel(q_ref, kbd_ref, vt_ref, m_ref, *refs, tq, nc, nsel, topk, q0):
    oc_ref, sel_ref, s_ref, p_ref, psum_ref = refs[-5:]
    i = q0 + pl.program_id(1)
    dh = NSA_HEAD_DIM
    rows_per = min(nc, CMP_ROWS)
    q = (q_ref[...] * ((dh ** -0.5) * LOG2_E)).astype(BF16)
    s_ref[...] = jnp.dot(kbd_ref[...], q, preferred_element_type=F32)
    for ch in range(tq // LANES):
        lanes = slice(ch * LANES, (ch + 1) * LANES)
        tpos = i * tq + ch * LANES + lax.broadcasted_iota(jnp.int32, (rows_per, LANES), 1)
        cblk0 = lax.broadcasted_iota(jnp.int32, (rows_per, LANES), 0)
        masks = [(CMP_STRIDE * (cblk0 + c * rows_per) + CMP_BLOCK - 1) <= tpos for c in range(nc // rows_per)]
        for r in range(NSA_REP):
            pieces = [slice(r * nc + c * rows_per, r * nc + (c + 1) * rows_per) for c in range(nc // rows_per)]
            m = jnp.full((1, LANES), -jnp.inf, F32)
            for rows, mask in zip(pieces, masks):
                m = jnp.maximum(m, jnp.max(jnp.where(mask, s_ref[rows, lanes], -jnp.inf), axis=0, keepdims=True))
            m = jnp.where(m > -jnp.inf, m, 0.0)
            total = jnp.zeros((1, LANES), F32)
            for rows, mask in zip(pieces, masks):
                e = jnp.exp2(jnp.where(mask, s_ref[rows, lanes], -jnp.inf) - m)
                s_ref[rows, lanes] = e
                total = total + jnp.sum(e, axis=0, keepdims=True)
            inv = 1.0 / jnp.maximum(total, 1e-30)
            for c, rows in enumerate(pieces):
                p = s_ref[rows, lanes] * inv
                p_ref[rows, lanes] = p.astype(BF16)
                prow = slice(c * rows_per, (c + 1) * rows_per)
                psum_ref[prow, lanes] = p if r == 0 else psum_ref[prow, lanes] + p
    oc_t = jnp.dot(vt_ref[...], p_ref[...], preferred_element_type=F32)
    imp = jnp.zeros((nsel, tq), F32)
    for piece in _split3(psum_ref[...]):
        imp = imp + jnp.dot(m_ref[...], piece, preferred_element_type=F32)
    for ch in range(tq // LANES):
        lanes = slice(ch * LANES, (ch + 1) * LANES)
        oc_ref[lanes, :] = oc_t[:, lanes].T
        t1 = i * tq + ch * LANES + lax.broadcasted_iota(jnp.int32, (nsel, LANES), 1)
        blk = lax.broadcasted_iota(jnp.int32, (nsel, LANES), 0)
        cur = _div_pow2(t1, SEL_BLOCK)
        forced = (blk == 0) | (blk == cur) | (blk == cur - 1)
        valid = blk * SEL_BLOCK <= t1
        score = jnp.where(valid, jnp.where(forced, FORCED_SCORE, imp[:, lanes]), -jnp.inf)
        blkf = blk.astype(F32)
        work = score
        for _ in range(topk):
            mx = jnp.max(work, axis=0, keepdims=True)
            first = jnp.min(jnp.where(work == mx, blkf, float(nsel)), axis=0, keepdims=True)
            work = jnp.where(blkf == first, -jnp.inf, work)
        sel_ref[:nsel, lanes] = jnp.where((score > -jnp.inf) & (work == -jnp.inf), 1.0, 0.0).astype(sel_ref.dtype)
        if sel_ref.shape[0] > nsel:
            sel_ref[nsel:, lanes] = jnp.zeros((sel_ref.shape[0] - nsel, LANES), sel_ref.dtype)


def cmp_topk(q_t, kbd, v_t, imp_mat_t, nsel_all, topk, q0, nq, tq, prev):
    qw, t = q_t.shape
    g = kbd.shape[0]
    nc = kbd.shape[1] // NSA_REP
    nsel = imp_mat_t.shape[0]
    assert nc % min(nc, CMP_ROWS) == 0
    in_specs = [
        pl.BlockSpec((NSA_GW, tq), lambda gi, i: (gi, q0 + i)),
        pl.BlockSpec((None, NSA_REP * nc, NSA_GW), lambda gi, i: (gi, 0, 0)),
        pl.BlockSpec((None, NSA_GW, NSA_REP * nc), lambda gi, i: (gi, 0, 0)),
        pl.BlockSpec((nsel, nc), lambda gi, i: (0, 0)),
    ]
    in_specs += [pl.BlockSpec(memory_space=pl.ANY)] * 2
    args = [q_t, kbd, v_t, imp_mat_t, *prev]
    aliases = {len(args) - 2: 0, len(args) - 1: 1}
    return pl.pallas_call(
        functools.partial(_cmp_topk_kernel, tq=tq, nc=nc, nsel=nsel, topk=topk, q0=q0),
        grid=(g, nq),
        in_specs=in_specs,
        out_specs=[
            pl.BlockSpec((tq, NSA_GW), lambda gi, i: (q0 + i, gi)),
            pl.BlockSpec((None, nsel_all, tq), lambda gi, i: (gi, 0, q0 + i)),
        ],
        out_shape=[
            jax.ShapeDtypeStruct((t, qw), F32),
            jax.ShapeDtypeStruct((g, nsel_all, t), F32),
        ],
        input_output_aliases=aliases,
        scratch_shapes=[
            pltpu.VMEM((NSA_REP * nc, tq), F32),
            pltpu.VMEM((NSA_REP * nc, tq), BF16),
            pltpu.VMEM((nc, tq), F32),
        ],
        compiler_params=_cparams(("parallel", "parallel")),
        name="nsa_cmp_topk",
    )(*args)


def _flash_kernel(q_ref, k_ref, v_ref, sel_ref, o_ref,
                  m_ref, l_ref, alpha_ref, acc_ref, s_ref, p_ref, bias_ref, kbd_ref, vbd_ref, qs_ref, *, tq, kt):
    qi = pl.program_id(1)
    dh = NSA_HEAD_DIM
    n_sub = tq // FLASH_SUB
    n_blk = kt // SEL_BLOCK

    m_ref[...] = jnp.full_like(m_ref, NEG_INIT)
    l_ref[...] = jnp.zeros_like(l_ref)
    acc_ref[...] = jnp.zeros_like(acc_ref)
    qs_ref[...] = (q_ref[...] * ((dh ** -0.5) * LOG2_E)).astype(BF16)
    one_row = lax.broadcasted_iota(jnp.int32, (FLASH_SUM_ROWS, NSA_REP * kt), 0)
    one_head = _div_pow2(lax.broadcasted_iota(jnp.int32, (FLASH_SUM_ROWS, NSA_REP * kt), 1), kt)
    vbd_ref[NSA_GW:, :] = (one_row == one_head).astype(BF16)

    def key_tile(ki, positional, first_sub=0):
        k4 = k_ref[ki]
        kseg = _div_pow2(lax.broadcasted_iota(jnp.int32, k4.shape, 1), dh)
        v4 = v_ref[ki]
        vblk = _div_pow2(lax.broadcasted_iota(jnp.int32, v4.shape, 0), dh)
        for r in range(NSA_REP):
            kbd_ref[r * kt:(r + 1) * kt, :] = jnp.where(kseg == r, k4, jnp.zeros_like(k4))
            vbd_ref[:NSA_GW, r * kt:(r + 1) * kt] = jnp.where(vblk == r, v4, jnp.zeros_like(v4))

        def scores(sub):
            c0 = sub * FLASH_SUB
            cols = slice(c0, c0 + FLASH_SUB)
            s_ref[:, cols] = jnp.dot(kbd_ref[...], qs_ref[:, cols],
                                     preferred_element_type=F32)
            for jb in range(n_blk):
                rows = slice(jb * SEL_BLOCK, (jb + 1) * SEL_BLOCK)
                picked = sel_ref[pl.ds(ki * n_blk + jb, 1), cols] > 0.5
                if positional:
                    tpos = qi * tq + c0 + lax.broadcasted_iota(jnp.int32, (SEL_BLOCK, FLASH_SUB), 1)
                    kpos = ki * kt + jb * SEL_BLOCK + lax.broadcasted_iota(jnp.int32, (SEL_BLOCK, FLASH_SUB), 0)
                    allowed = picked & (kpos <= tpos)
                else:
                    allowed = jnp.broadcast_to(picked, (SEL_BLOCK, FLASH_SUB))
                bias_ref[rows, cols] = jnp.where(allowed, 0.0, -jnp.inf)

        scores(first_sub)
        for sub in range(first_sub, n_sub):
            c0 = sub * FLASH_SUB
            cols = slice(c0, c0 + FLASH_SUB)
            if sub + 1 < n_sub:
                scores(sub + 1)
            for ch in range(FLASH_SUB // LANES):
                lanes = slice(c0 + ch * LANES, c0 + (ch + 1) * LANES)
                bias = bias_ref[:, lanes]
                for r in range(NSA_REP):
                    x = s_ref[r * kt:(r + 1) * kt, lanes] + bias
                    m_prev = m_ref[r:r + 1, lanes]
                    m_new = jnp.maximum(m_prev, jnp.max(x, axis=0, keepdims=True))
                    m_ref[r:r + 1, lanes] = m_new
                    alpha_ref[r:r + 1, lanes] = jnp.exp2(m_prev - m_new)
                    p_ref[r * kt:(r + 1) * kt, lanes] = jnp.exp2(x - m_new).astype(BF16)
            pv = jnp.dot(vbd_ref[...], p_ref[:, cols], preferred_element_type=F32)
            for r in range(NSA_REP):
                hd = slice(r * dh, (r + 1) * dh)
                alpha = alpha_ref[r:r + 1, cols]
                acc_ref[hd, cols] = acc_ref[hd, cols] * alpha + pv[hd]
                l_ref[r:r + 1, cols] = l_ref[r:r + 1, cols] * alpha + pv[NSA_GW + r:NSA_GW + r + 1]

    def before_diagonal(ki, carry):
        key_tile(ki, False)
        return carry

    diag = qi * (tq // kt)
    lax.fori_loop(0, diag, before_diagonal, 0)
    for d in range(tq // kt):
        def on_diagonal(ki, carry, first_sub=d * kt // FLASH_SUB):
            key_tile(ki, True, first_sub=first_sub)
            return carry
        lax.fori_loop(diag + d, diag + d + 1, on_diagonal, 0)

    for ch in range(tq // LANES):
        lanes = slice(ch * LANES, (ch + 1) * LANES)
        out_t = jnp.concatenate(
            [acc_ref[r * dh:(r + 1) * dh, lanes] / l_ref[r:r + 1, lanes] for r in range(NSA_REP)], axis=0)
        o_ref[lanes, :] = out_t.T


def flash_branch(q_t, k4, vt4, sel_t, tq, kt):
    qw, t = q_t.shape
    g, nk = k4.shape[:2]
    nsel = sel_t.shape[1]
    assert t % tq == 0 and tq % kt == 0 and kt % SEL_BLOCK == 0 and tq % FLASH_SUB == 0 and kt % FLASH_SUB == 0
    return pl.pallas_call(
        functools.partial(_flash_kernel, tq=tq, kt=kt),
        grid=(g, t // tq),
        in_specs=[
            pl.BlockSpec((NSA_GW, tq), lambda gi, i: (gi, i)),
            pl.BlockSpec((None, nk, kt, NSA_GW), lambda gi, i: (gi, 0, 0, 0), pipeline_mode=pl.Buffered(1)),
            pl.BlockSpec((None, nk, NSA_GW, kt), lambda gi, i: (gi, 0, 0, 0), pipeline_mode=pl.Buffered(1)),
            pl.BlockSpec((None, nsel, tq), lambda gi, i: (gi, 0, i)),
        ],
        out_specs=pl.BlockSpec((tq, NSA_GW), lambda gi, i: (i, gi)),
        scratch_shapes=[
            pltpu.VMEM((8, tq), F32),
            pltpu.VMEM((8, tq), F32),
            pltpu.VMEM((8, tq), F32),
            pltpu.VMEM((NSA_GW, tq), F32),
            pltpu.VMEM((NSA_REP * kt, tq), F32),
            pltpu.VMEM((NSA_REP * kt, tq), BF16),
            pltpu.VMEM((kt, tq), F32),
            pltpu.VMEM((NSA_REP * kt, NSA_GW), BF16),
            pltpu.VMEM((NSA_GW + FLASH_SUM_ROWS, NSA_REP * kt), BF16),
            pltpu.VMEM((NSA_GW, tq), BF16),
        ],
        out_shape=jax.ShapeDtypeStruct((t, qw), F32),
        compiler_params=_cparams(("parallel", "arbitrary")),
        name="nsa_selected",
    )(q_t, k4, vt4, sel_t)


def _window_kernel(q_ref, k_ref, v_ref, o_ref, s_ref, p_ref, bias_ref, kbd_ref, vbd_ref, *, tq, kt):
    i = pl.program_id(1)
    dh = NSA_HEAD_DIM
    nw = WINDOW // kt + 1
    span = nw * kt
    qs = (q_ref[...] * ((dh ** -0.5) * LOG2_E)).astype(BF16)
    for w in range(nw):
        tile = i - (nw - 1) + w
        k4 = k_ref[jnp.maximum(tile, 0)]
        kseg = _div_pow2(lax.broadcasted_iota(jnp.int32, k4.shape, 1), dh)
        v4 = v_ref[jnp.maximum(tile, 0)]
        vblk = _div_pow2(lax.broadcasted_iota(jnp.int32, v4.shape, 0), dh)
        for r in range(NSA_REP):
            at = r * span + w * kt
            kbd_ref[at:at + kt, :] = jnp.where(kseg == r, k4, jnp.zeros_like(k4))
            vbd_ref[:NSA_GW, at:at + kt] = jnp.where(vblk == r, v4, jnp.zeros_like(v4))
        tpos = i * tq + lax.broadcasted_iota(jnp.int32, (kt, tq), 1)
        kpos = tile * kt + lax.broadcasted_iota(jnp.int32, (kt, tq), 0)
        allowed = (kpos >= 0) & (kpos <= tpos) & (kpos > tpos - WINDOW)
        bias_ref[w * kt:(w + 1) * kt, :] = jnp.where(allowed, 0.0, -jnp.inf)
    one_row = lax.broadcasted_iota(jnp.int32, (FLASH_SUM_ROWS, NSA_REP * span), 0)
    one_col = lax.broadcasted_iota(jnp.int32, (FLASH_SUM_ROWS, NSA_REP * span), 1)
    vbd_ref[NSA_GW:, :] = ((one_col >= one_row * span) & (one_col < (one_row + 1) * span)).astype(BF16)

    s_ref[...] = jnp.dot(kbd_ref[...], qs, preferred_element_type=F32)
    for ch in range(tq // LANES):
        lanes = slice(ch * LANES, (ch + 1) * LANES)
        for r in range(NSA_REP):
            m = jnp.full((1, LANES), -jnp.inf, F32)
            for w in range(nw):
                rows = slice(r * span + w * kt, r * span + (w + 1) * kt)
                m = jnp.maximum(m, jnp.max(s_ref[rows, lanes] + bias_ref[w * kt:(w + 1) * kt, lanes],
                                           axis=0, keepdims=True))
            m = jnp.where(m > -jnp.inf, m, 0.0)
            for w in range(nw):
                rows = slice(r * span + w * kt, r * span + (w + 1) * kt)
                x = s_ref[rows, lanes] + bias_ref[w * kt:(w + 1) * kt, lanes]
                p_ref[rows, lanes] = jnp.exp2(x - m).astype(BF16)
    pv = jnp.dot(vbd_ref[...], p_ref[...], preferred_element_type=F32)
    for ch in range(tq // LANES):
        lanes = slice(ch * LANES, (ch + 1) * LANES)
        out_t = jnp.concatenate(
            [pv[r * dh:(r + 1) * dh, lanes] / jnp.maximum(pv[NSA_GW + r:NSA_GW + r + 1, lanes], 1e-30)
             for r in range(NSA_REP)], axis=0)
        o_ref[lanes, :] = out_t.T


def window_branch(q_t, k4, vt4, tq):
    qw, t = q_t.shape
    g, nk, kt, _ = k4.shape
    assert tq == kt and WINDOW % kt == 0 and t % tq == 0
    span = (WINDOW // kt + 1) * kt
    return pl.pallas_call(
        functools.partial(_window_kernel, tq=tq, kt=kt),
        grid=(g, t // tq),
        in_specs=[
            pl.BlockSpec((NSA_GW, tq), lambda gi, i: (gi, i)),
            pl.BlockSpec((None, nk, kt, NSA_GW), lambda gi, i: (gi, 0, 0, 0)),
            pl.BlockSpec((None, nk, NSA_GW, kt), lambda gi, i: (gi, 0, 0, 0)),
        ],
        out_specs=pl.BlockSpec((tq, NSA_GW), lambda gi, i: (i, gi)),
        scratch_shapes=[
            pltpu.VMEM((NSA_REP * span, tq), F32),
            pltpu.VMEM((NSA_REP * span, tq), BF16),
            pltpu.VMEM((span, tq), F32),
            pltpu.VMEM((NSA_REP * span, NSA_GW), BF16),
            pltpu.VMEM((NSA_GW + FLASH_SUM_ROWS, NSA_REP * span), BF16),
        ],
        out_shape=jax.ShapeDtypeStruct((t, qw), F32),
        compiler_params=_cparams(("parallel", "parallel")),
        name="nsa_window",
    )(q_t, k4, vt4)


def _nsa_out_kernel(oc_ref, os_ref, ow_ref, gl_ref, eg_ref, w_ref, gpost_ref, h_ref, out_ref):
    pieces = _split3(_sigmoid(gl_ref[...]))
    mixed = jnp.zeros(oc_ref.shape, F32)
    for b, br_ref in enumerate((oc_ref, os_ref, ow_ref)):
        gfull = jnp.zeros(oc_ref.shape, F32)
        for piece in pieces:
            gfull = gfull + jnp.dot(piece, eg_ref[b], preferred_element_type=F32)
        mixed = mixed + gfull * br_ref[...]
    mix = jnp.dot(mixed.astype(BF16), w_ref[...], preferred_element_type=F32)
    out_ref[...] = h_ref[...] + _rms(mix, gpost_ref[...])


def nsa_out(oc, osel, ow, gate_logits, expand, w_o, gpost, h, tm=512):
    t, qw = oc.shape
    d = h.shape[1]
    assert t % tm == 0
    row = lambda w: pl.BlockSpec((tm, w), lambda i: (i, 0))
    return pl.pallas_call(
        _nsa_out_kernel,
        grid=(t // tm,),
        in_specs=[
            row(qw), row(qw), row(qw), row(LANES),
            pl.BlockSpec((3, LANES, qw), lambda i: (0, 0, 0)),
            pl.BlockSpec((qw, d), lambda i: (0, 0)),
            pl.BlockSpec((1, d), lambda i: (0, 0)),
            row(d),
        ],
        out_specs=row(d),
        out_shape=jax.ShapeDtypeStruct((t, d), F32),
        compiler_params=_cparams(("parallel",)),
        name="nsa_out",
    )(oc, osel, ow, gate_logits, expand, w_o, gpost.reshape(1, d), h)


def _pad_cols(w, n):
    return jnp.pad(w, ((0, 0), (0, n - w.shape[1])))


def _importance_matrix(nc, nsel):
    r = SEL_BLOCK // CMP_STRIDE
    c = CMP_BLOCK // CMP_STRIDE
    mat = np.zeros((nc, nsel), np.float32)
    for kblk in range(nsel):
        for m in range(r):
            for n in range(c):
                j = r * kblk + m - n
                if 0 <= j < nc - 1:
                    mat[j, kblk] += 1.0
    return jnp.asarray(mat, BF16)


def _gate_expand():
    e = np.zeros((3, LANES, NSA_GROUPS * NSA_GW), np.float32)
    for head in range(NSA_GROUPS * NSA_REP):
        for b in range(3):
            e[b, head * 3 + b, head * NSA_HEAD_DIM:(head + 1) * NSA_HEAD_DIM] = 1.0
    return jnp.asarray(e, BF16)


def _block_diag_kv(k_cmp, v_cmp):
    g, nc, dh = k_cmp.shape
    eye = jnp.eye(NSA_REP, dtype=bool)
    kct = jnp.swapaxes(k_cmp, 1, 2)
    kbd = jnp.where(eye[None, :, None, :, None], kct[:, None, :, None, :], 0.0)
    vbd = jnp.where(eye[None, :, None, :, None], v_cmp[:, None, :, None, :], 0.0)
    return (kbd.reshape(g, NSA_REP * dh, NSA_REP * nc).astype(BF16),
            vbd.reshape(g, NSA_REP * nc, NSA_REP * dh).astype(BF16))


def kernel(x, p, mix_pre_norm, mix_post_norm, ffn_pre_norm, ffn_post_norm, gdn_w_in, gdn_conv_w, gdn_a_log,
           gdn_dt_bias, gdn_o_norm, gdn_w_out, kv_norm, kv_w, cmp_pos, cmp_w1, cmp_w2, nsa_w_qg, nsa_w_o,
           ffn_w_in, ffn_w_out, ple_w_in, ple_w_gate):
    depth = p.shape[0]
    n_a = gdn_w_in.shape[0]
    t = x.shape[1]
    h = x[0]
    fh = ffn_w_out.shape[1]
    conv_w_cols = gdn_conv_w.shape[2]
    vw = GDN_V_HEADS * GDN_HEAD_DIM
    main_w = conv_w_cols + vw

    def channel_and_ple(h, i):
        return ffn_ple(h, ffn_pre_norm[i], ffn_w_in[i].astype(BF16), ffn_w_out[i].astype(BF16),
                       ffn_post_norm[i], p[i, 0], ple_w_in[i].astype(BF16), ple_w_gate[i].astype(BF16))

    for i in range(n_a):
        w_in = gdn_w_in[i]
        w_beta = w_in[:, main_w:main_w + GDN_V_HEADS]
        w_a = w_in[:, main_w + GDN_V_HEADS:]
        w_small = _pad_cols(jnp.concatenate([w_beta, w_a, w_a], axis=1), LANES).astype(BF16)
        proj = norm_matmul(h, mix_pre_norm[i], w_in.astype(BF16), n=main_w)
        ba = norm_matmul(h, mix_pre_norm[i], w_small)
        pad_vec = lambda v: jnp.pad(v, (GDN_V_HEADS, LANES - 2 * GDN_V_HEADS))
        alog_vec = (pad_vec(gdn_a_log[i]) + jnp.pad(gdn_a_log[i], (2 * GDN_V_HEADS, LANES - 3 * GDN_V_HEADS)))
        dtb_vec = (pad_vec(gdn_dt_bias[i]) + jnp.pad(gdn_dt_bias[i], (2 * GDN_V_HEADS, LANES - 3 * GDN_V_HEADS)))
        slab = gdn_gates(ba, alog_vec.reshape(1, LANES), dtb_vec.reshape(1, LANES))
        qkv_hm = gdn_conv(proj, gdn_conv_w[i])
        kt_hm = jnp.swapaxes(qkv_hm[GDN_QK_HEADS:2 * GDN_QK_HEADS], 1, 2)
        o = gdn_scan(qkv_hm, kt_hm, slab, slab.T)
        onorm_tiled = jnp.tile(gdn_o_norm[i], GDN_V_HEADS).reshape(1, vw)
        h = gdn_out(o, proj, onorm_tiled, gdn_w_out[i].astype(BF16), mix_post_norm[i], h)
        h = channel_and_ple(h, i)

    g = NSA_GROUPS
    dh = NSA_HEAD_DIM
    kv = norm_matmul(h, kv_norm, kv_w.astype(BF16), tn=768)
    kv6 = jnp.transpose(kv.reshape(t, 6, g, dh), (1, 2, 0, 3))
    nc = t // CMP_STRIDE
    nsel = t // SEL_BLOCK
    x2 = kv6[0:2].reshape(2, g, nc, CMP_STRIDE * dh)
    pos2 = cmp_pos.reshape(2, 2, CMP_STRIDE * dh)
    cmp_out = compress(x2, pos2, cmp_w1.astype(BF16), cmp_w2.astype(BF16))
    imp_mat_t = _importance_matrix(nc, nsel).T
    nq_r = t // (CMP_TQ * CMP_RANGES)
    assert t % (CMP_TQ * CMP_RANGES) == 0 and nc % CMP_RANGES == 0 and nsel % (8 * CMP_RANGES) == 0
    cmp_ranges = []
    for rg in range(CMP_RANGES):
        nc_r, nsel_r = (rg + 1) * nc // CMP_RANGES, (rg + 1) * nsel // CMP_RANGES
        vbd_t, kbd = _block_diag_kv(cmp_out[1][:, :nc_r], cmp_out[0][:, :nc_r])
        cmp_ranges.append((kbd, vbd_t, imp_mat_t[:nsel_r, :nc_r]))
    kt = FLASH_KT
    tiles = lambda a: a.reshape(g, t // kt, kt, dh)
    rep_k = lambda a: jnp.tile(tiles(a), (1, 1, 1, NSA_REP)).astype(BF16)
    rep_t = lambda a: jnp.tile(jnp.swapaxes(tiles(a), 2, 3), (1, 1, NSA_REP, 1)).astype(BF16)
    k_slc, v_slc_t = rep_k(kv6[2]), rep_t(kv6[3])
    k_win, v_win_t = rep_k(kv6[4]), rep_t(kv6[5])
    expand = _gate_expand()

    for i in range(n_a, depth):
        j = i - n_a
        qw = g * NSA_GW
        w_qg = nsa_w_qg[j]
        q_t = norm_matmul_t(h, mix_pre_norm[i], w_qg[:, :qw].T.astype(BF16))
        gate_logits = norm_matmul(h, mix_pre_norm[i], _pad_cols(w_qg[:, qw:], LANES).astype(BF16))
        cmp_res = (jnp.zeros((t, qw), F32), jnp.zeros((g, nsel, t), F32))
        for rg, (kbd, vbd_t, imp_r) in enumerate(cmp_ranges):
            cmp_res = cmp_topk(q_t, kbd, vbd_t, imp_r, nsel, min(SEL_TOPK, nsel), rg * nq_r, nq_r, CMP_TQ,
                               prev=cmp_res)
        o_c, sel_t = cmp_res
        o_s = flash_branch(q_t, k_slc, v_slc_t, sel_t, tq=min(FLASH_TQ, t), kt=kt)
        o_w = window_branch(q_t, k_win, v_win_t, tq=kt)
        h = nsa_out(o_c, o_s, o_w, gate_logits, expand, nsa_w_o[j].astype(BF16), mix_post_norm[i], h)
        h = channel_and_ple(h, i)
    return h[None]
```

```python
import functools

import numpy as np
import jax
import jax.numpy as jnp
from jax import lax
from jax.experimental import pallas as pl
from jax.experimental.pallas import tpu as pltpu

F32 = jnp.float32
BF16 = jnp.bfloat16

NORM_EPS = 1e-6
L2_EPS = 1e-6
GDN_QK_HEADS = 8
GDN_V_HEADS = 16
GDN_HEAD_DIM = 128
GDN_CONV = 4
GDN_CHUNK = 64
GDN_GROUP = 256
GDN_SCAN_HEADS = 8
GDN_CONV_HEADS = 4
NSA_GROUPS = 4
NSA_REP = 4
NSA_HEAD_DIM = 64
NSA_GW = NSA_REP * NSA_HEAD_DIM
CMP_BLOCK = 32
CMP_STRIDE = 16
SEL_BLOCK = 64
SEL_TOPK = 16
WINDOW = 512
FORCED_SCORE = 1e4
LANES = 128
NEG_INIT = -(2.0 ** 100)
LOG2_E = 1.4426950408889634
FLASH_TQ = 2048
FLASH_KT = 256
FLASH_SUB = 256
FLASH_SUM_ROWS = 16
CMP_ROWS = 256
CMP_TQ = 256
CMP_RANGES = 4

VMEM_LIMIT = 56 * 1024 * 1024
FFN_VMEM_LIMIT = 60 * 1024 * 1024


def _cparams(sem, vmem_limit=VMEM_LIMIT):
    return pltpu.CompilerParams(dimension_semantics=sem, vmem_limit_bytes=vmem_limit)


def _rms(x, gain):
    return x * lax.rsqrt(jnp.mean(x * x, axis=-1, keepdims=True) + NORM_EPS) * gain


def _mm(a, b):
    return jnp.dot(a.astype(BF16), b.astype(BF16), preferred_element_type=F32)


def _sigmoid(x):
    return 1.0 / (1.0 + jnp.exp(-x))


def _div_pow2(x, d):
    shift = d.bit_length() - 1
    assert d == 1 << shift
    return jnp.right_shift(x, shift)


def _split3(x):
    a = x.astype(BF16)
    r = x - a.astype(F32)
    b = r.astype(BF16)
    c = (r - b.astype(F32)).astype(BF16)
    return a, b, c


def _norm_matmul_kernel(x_ref, g_ref, w_ref, o_ref, xn_ref):
    @pl.when(pl.program_id(1) == 0)
    def _():
        xn_ref[...] = _rms(x_ref[...], g_ref[...]).astype(BF16)

    o_ref[...] = jnp.dot(xn_ref[...], w_ref[...], preferred_element_type=F32)


def _norm_matmul_t_kernel(x_ref, g_ref, wt_ref, o_ref, xn_ref):
    @pl.when(pl.program_id(1) == 0)
    def _():
        xn_ref[...] = _rms(x_ref[...], g_ref[...]).astype(BF16)

    o_ref[...] = lax.dot_general(wt_ref[...], xn_ref[...], (((1,), (1,)), ((), ())), preferred_element_type=F32)


def norm_matmul_t(h, gain, w_t, tm=1024, tn=1024):
    t, d = h.shape
    n = w_t.shape[0]
    tn = min(tn, n)
    assert t % tm == 0 and n % tn == 0
    return pl.pallas_call(
        _norm_matmul_t_kernel,
        grid=(t // tm, n // tn),
        in_specs=[
            pl.BlockSpec((tm, d), lambda i, j: (i, 0)),
            pl.BlockSpec((1, d), lambda i, j: (0, 0)),
            pl.BlockSpec((tn, d), lambda i, j: (j, 0)),
        ],
        out_specs=pl.BlockSpec((tn, tm), lambda i, j: (j, i)),
        out_shape=jax.ShapeDtypeStruct((n, t), F32),
        scratch_shapes=[pltpu.VMEM((tm, d), BF16)],
        compiler_params=_cparams(("parallel", "arbitrary")),
        name="norm_matmul_t",
    )(h, gain.reshape(1, d), w_t)


def norm_matmul(h, gain, w, tm=1024, tn=1024, n=None):
    t, d = h.shape
    n = w.shape[1] if n is None else n
    tn = min(tn, n)
    assert t % tm == 0 and n % tn == 0
    return pl.pallas_call(
        _norm_matmul_kernel,
        grid=(t // tm, n // tn),
        in_specs=[
            pl.BlockSpec((tm, d), lambda i, j: (i, 0)),
            pl.BlockSpec((1, d), lambda i, j: (0, 0)),
            pl.BlockSpec((d, tn), lambda i, j: (0, j)),
        ],
        out_specs=pl.BlockSpec((tm, tn), lambda i, j: (i, j)),
        out_shape=jax.ShapeDtypeStruct((t, n), F32),
        scratch_shapes=[pltpu.VMEM((tm, d), BF16)],
        compiler_params=_cparams(("parallel", "arbitrary")),
        name="norm_matmul",
    )(h, gain.reshape(1, d), w)


def _ffn_ple_kernel(h_ref, gpre_ref, wg_ref, wu_ref, wo_ref, gpost_ref, p_ref, wple_ref, wgt_ref,
                    o_ref, xn_ref, acc_ref, *, nf):
    f = pl.program_id(1)

    @pl.when(f == 0)
    def _():
        xn_ref[...] = _rms(h_ref[...], gpre_ref[...]).astype(BF16)
        acc_ref[...] = jnp.zeros_like(acc_ref)

    xn = xn_ref[...]
    tf = wg_ref.shape[1]
    half = (tf // LANES // 2) * LANES
    for a, b in ((0, half), (half, tf)):
        gate = jnp.dot(xn, wg_ref[:, a:b], preferred_element_type=F32)
        up = jnp.dot(xn, wu_ref[:, a:b], preferred_element_type=F32)
        act = gate * _sigmoid(gate) * up
        acc_ref[...] += jnp.dot(act.astype(BF16), wo_ref[a:b, :], preferred_element_type=F32)

    @pl.when(f == nf - 1)
    def _():
        h2 = h_ref[...] + _rms(acc_ref[...], gpost_ref[...])
        emb = jnp.dot(p_ref[...].astype(BF16), wple_ref[...], preferred_element_type=F32)
        gt = _sigmoid(jnp.dot(h2.astype(BF16), wgt_ref[...], preferred_element_type=F32))
        o_ref[...] = h2 + emb * gt


def ffn_ple(h, gpre, w_in, w_out, gpost, p, w_ple, w_plegate, tm=1024, tf=1408):
    t, d = h.shape
    fh = w_out.shape[0]
    pd = p.shape[1]
    assert t % tm == 0 and fh % tf == 0 and w_in.shape[1] == 2 * fh
    nf = fh // tf
    return pl.pallas_call(
        functools.partial(_ffn_ple_kernel, nf=nf),
        grid=(t // tm, nf),
        in_specs=[
            pl.BlockSpec((tm, d), lambda i, f: (i, 0)),
            pl.BlockSpec((1, d), lambda i, f: (0, 0)),
            pl.BlockSpec((d, tf), lambda i, f: (0, f)),
            pl.BlockSpec((d, tf), lambda i, f: (0, nf + f)),
            pl.BlockSpec((tf, d), lambda i, f: (f, 0)),
            pl.BlockSpec((1, d), lambda i, f: (0, 0)),
            pl.BlockSpec((tm, pd), lambda i, f: (i, 0)),
            pl.BlockSpec((pd, d), lambda i, f: (0, 0), pipeline_mode=pl.Buffered(1)),
            pl.BlockSpec((d, d), lambda i, f: (0, 0), pipeline_mode=pl.Buffered(1)),
        ],
        out_specs=pl.BlockSpec((tm, d), lambda i, f: (i, 0)),
        out_shape=jax.ShapeDtypeStruct((t, d), F32),
        scratch_shapes=[pltpu.VMEM((tm, d), BF16), pltpu.VMEM((tm, d), F32)],
        compiler_params=_cparams(("parallel", "arbitrary"), FFN_VMEM_LIMIT),
        name="ffn_ple",
    )(h, gpre.reshape(1, d), w_in, w_in, w_out, gpost.reshape(1, d), p, w_ple, w_plegate)


def _gdn_conv_kernel(x_ref, halo_ref, w_ref, o_ref, *, tm):
    c = pl.program_id(0)
    i = pl.program_id(1)
    x = x_ref[...]
    halo = jnp.where(i > 0, halo_ref[...], 0.0)
    ext = jnp.concatenate([halo, x], axis=0)
    w = w_ref[...]
    y = x * w[GDN_CONV - 1:GDN_CONV, :]
    for k in range(1, GDN_CONV):
        shifted = pltpu.roll(ext, k, axis=0)[8:8 + tm]
        y = y + shifted * w[GDN_CONV - 1 - k:GDN_CONV - k, :]
    y = y * _sigmoid(y)
    for hd in range(GDN_CONV_HEADS):
        head = c * GDN_CONV_HEADS + hd
        seg = y[:, hd * LANES:(hd + 1) * LANES]
        normed = seg * lax.rsqrt(jnp.sum(seg * seg, axis=-1, keepdims=True) + L2_EPS)
        q_scale = jnp.where(head < GDN_QK_HEADS, GDN_HEAD_DIM ** -0.5, 1.0)
        o_ref[hd] = jnp.where(head < 2 * GDN_QK_HEADS, normed * q_scale, seg)


def gdn_conv(proj, conv_w, tm=1024):
    t = proj.shape[0]
    n_tiles = conv_w.shape[1] // LANES
    cw = GDN_CONV_HEADS
    assert t % tm == 0 and n_tiles % cw == 0
    return pl.pallas_call(
        functools.partial(_gdn_conv_kernel, tm=tm),
        grid=(n_tiles // cw, t // tm),
        in_specs=[
            pl.BlockSpec((tm, cw * LANES), lambda c, i: (i, c)),
            pl.BlockSpec((8, cw * LANES), lambda c, i: (jnp.maximum(i * (tm // 8) - 1, 0), c)),
            pl.BlockSpec((GDN_CONV, cw * LANES), lambda c, i: (0, c)),
        ],
        out_specs=pl.BlockSpec((cw, tm, LANES), lambda c, i: (c, i, 0)),
        out_shape=jax.ShapeDtypeStruct((n_tiles, t, LANES), F32),
        compiler_params=_cparams(("parallel", "parallel")),
        name="gdn_conv",
    )(proj, proj, conv_w)


def _gdn_gate_kernel(x_ref, alog_ref, dtb_ref, lc_ref, lf_ref, o_ref):
    x = x_ref[...]
    lane = lax.broadcasted_iota(jnp.int32, x.shape, 1)
    beta = _sigmoid(x)
    z = x + dtb_ref[...]
    softplus = jnp.maximum(z, 0.0) + jnp.log(1.0 + jnp.exp(-jnp.abs(z)))
    g = -jnp.exp(alog_ref[...]) * softplus
    gcum = jnp.zeros_like(x)
    gtot = jnp.zeros_like(x)
    for piece in _split3(g):
        gcum = gcum + jnp.dot(lc_ref[...], piece, preferred_element_type=F32)
        gtot = gtot + jnp.dot(lf_ref[...], piece, preferred_element_type=F32)
    o_ref[...] = jnp.where(lane < GDN_V_HEADS, beta, jnp.where(lane < 2 * GDN_V_HEADS, gcum, gtot))


def gdn_gates(ba, alog_vec, dtb_vec):
    t = ba.shape[0]
    tm = GDN_GROUP
    r = np.arange(tm)
    same = (r[:, None] // GDN_CHUNK) == (r[None, :] // GDN_CHUNK)
    lc = jnp.asarray(same & (r[None, :] <= r[:, None]), BF16)
    lf = jnp.asarray(same, BF16)
    return pl.pallas_call(
        _gdn_gate_kernel,
        grid=(t // tm,),
        in_specs=[
            pl.BlockSpec((tm, LANES), lambda i: (i, 0)),
            pl.BlockSpec((1, LANES), lambda i: (0, 0)),
            pl.BlockSpec((1, LANES), lambda i: (0, 0)),
            pl.BlockSpec((tm, tm), lambda i: (0, 0)),
            pl.BlockSpec((tm, tm), lambda i: (0, 0)),
        ],
        out_specs=pl.BlockSpec((tm, LANES), lambda i: (i, 0)),
        out_shape=jax.ShapeDtypeStruct((t, LANES), F32),
        compiler_params=_cparams(("parallel",)),
        name="gdn_gates",
    )(ba, alog_vec, dtb_vec, lc, lf)


def _gdn_scan_kernel(q_ref, k_ref, kt_ref, v_ref, slab_ref, slabt_ref, o_ref, s_ref):
    n = pl.program_id(1)

    @pl.when(n == 0)
    def _():
        s_ref[...] = jnp.zeros_like(s_ref)

    L = GDN_GROUP
    C = GDN_CHUNK
    D = GDN_HEAD_DIM
    slab = slab_ref[...]
    slabt = slabt_ref[...]
    row = lax.broadcasted_iota(jnp.int32, (L, L), 0)
    col = lax.broadcasted_iota(jnp.int32, (L, L), 1)
    same = _div_pow2(row, C) == _div_pow2(col, C)
    causal = same & (col <= row)
    strict = same & (col < row)
    eye = (row == col).astype(F32)
    lane = lax.broadcasted_iota(jnp.int32, (L, LANES), 1)
    sub = lax.broadcasted_iota(jnp.int32, (LANES, L), 0)

    def column(idx):
        return jnp.sum(jnp.where(lane == idx, slab, 0.0), axis=1, keepdims=True)

    def rowvec(idx):
        return jnp.sum(jnp.where(sub == idx, slabt, 0.0), axis=0, keepdims=True)

    colk = lax.broadcasted_iota(jnp.int32, (D, L), 1)
    slots = range(2 * GDN_SCAN_HEADS)

    q = [q_ref[hq] for hq in range(GDN_SCAN_HEADS)]
    k = [k_ref[hq] for hq in range(GDN_SCAN_HEADS)]
    kt = [kt_ref[hq] for hq in range(GDN_SCAN_HEADS)]
    kk = [_mm(k[hq], kt[hq]) for hq in range(GDN_SCAN_HEADS)]
    qk = [_mm(q[hq], kt[hq]) for hq in range(GDN_SCAN_HEADS)]
    beta_c, gc_c, gt_c, decay, bp, inv, kdt = [], [], [], [], [], [], []
    for slot in slots:
        hq = slot // 2
        hv = 2 * (pl.program_id(0) * GDN_SCAN_HEADS + hq) + slot % 2
        beta_c.append(column(hv))
        gc_c.append(column(GDN_V_HEADS + hv))
        gt_c.append(column(2 * GDN_V_HEADS + hv))
        gc_r = rowvec(GDN_V_HEADS + hv)
        gt_r = rowvec(2 * GDN_V_HEADS + hv)
        decay.append(jnp.where(causal, jnp.exp(jnp.where(causal, gc_c[slot] - gc_r, 0.0)), 0.0))
        bp.append(jnp.where(strict, -(kk[hq] * beta_c[slot]) * decay[slot], 0.0))
        inv.append(eye + bp[slot])
        kdt.append(kt[hq] * jnp.exp(gt_r - gc_r))
    for _ in range(5):
        bp = [_mm(bp[slot], bp[slot]) for slot in slots]
        inv = [inv[slot] + _mm(inv[slot], bp[slot]) for slot in slots]
    u, w, qkm, q_dec = [], [], [], []
    for slot in slots:
        hq = slot // 2
        egc = jnp.exp(gc_c[slot])
        rhs = jnp.concatenate([v_ref[slot] * beta_c[slot], k[hq] * (beta_c[slot] * egc)], axis=1)
        sol = _mm(inv[slot], rhs)
        u.append(sol[:, :D])
        w.append(sol[:, D:])
        qkm.append(jnp.where(causal, qk[hq] * decay[slot], 0.0))
        q_dec.append(q[hq] * egc)
    state = [s_ref[slot] for slot in slots]
    v_done = [[] for _ in slots]
    for c in range(L // C):
        lo, hi = c * C, (c + 1) * C
        r = [_mm(jnp.concatenate([w[slot][lo:hi], q_dec[slot][lo:hi]], axis=0), state[slot]) for slot in slots]
        for slot in slots:
            v_done[slot].append(u[slot][lo:hi] - r[slot][:C])
            v_all = jnp.concatenate(v_done[slot] + [jnp.zeros((L - hi, D), F32)] * (hi < L), axis=0)
            o_ref[lo:hi, slot * D:(slot + 1) * D] = r[slot][C:] + _mm(qkm[slot][lo:hi, :], v_all)
            kdt_c = jnp.where((colk >= lo) & (colk < hi), kdt[slot], 0.0)
            state[slot] = state[slot] * jnp.exp(gt_c[slot][lo:lo + 1, :]) + _mm(kdt_c, v_all)
    for slot in slots:
        s_ref[slot] = state[slot]


def gdn_scan(qkv_hm, kt_hm, slab, slabt):
    t = qkv_hm.shape[1]
    L = GDN_GROUP
    D = GDN_HEAD_DIM
    hq = GDN_SCAN_HEADS
    assert t % L == 0 and GDN_QK_HEADS % hq == 0
    q_blocks = GDN_QK_HEADS // hq
    return pl.pallas_call(
        _gdn_scan_kernel,
        grid=(q_blocks, t // L),
        in_specs=[
            pl.BlockSpec((hq, L, D), lambda j, n: (j, n, 0)),
            pl.BlockSpec((hq, L, D), lambda j, n: (q_blocks + j, n, 0)),
            pl.BlockSpec((hq, D, L), lambda j, n: (j, 0, n)),
            pl.BlockSpec((2 * hq, L, D), lambda j, n: (q_blocks + j, n, 0)),
            pl.BlockSpec((L, LANES), lambda j, n: (n, 0)),
            pl.BlockSpec((LANES, L), lambda j, n: (0, n)),
        ],
        out_specs=pl.BlockSpec((L, 2 * hq * D), lambda j, n: (n, j)),
        out_shape=jax.ShapeDtypeStruct((t, GDN_V_HEADS * D), F32),
        scratch_shapes=[pltpu.VMEM((2 * hq, D, D), F32)],
        compiler_params=_cparams(("parallel", "arbitrary")),
        name="gdn_scan",
    )(qkv_hm, qkv_hm, kt_hm, qkv_hm, slab, slabt)


def _gdn_out_kernel(o_ref, z_ref, onorm_ref, w_ref, gpost_ref, h_ref, out_ref):
    o = o_ref[...]
    z = z_ref[...]
    parts = []
    for hd in range(GDN_V_HEADS):
        seg = o[:, hd * GDN_HEAD_DIM:(hd + 1) * GDN_HEAD_DIM]
        parts.append(seg * lax.rsqrt(jnp.mean(seg * seg, axis=-1, keepdims=True) + NORM_EPS))
    gated = jnp.concatenate(parts, axis=1) * onorm_ref[...] * (z * _sigmoid(z))
    mix = jnp.dot(gated.astype(BF16), w_ref[...], preferred_element_type=F32)
    out_ref[...] = h_ref[...] + _rms(mix, gpost_ref[...])


def gdn_out(o, proj, onorm_tiled, w_out, gpost, h, tm=512):
    t, vw = o.shape
    d = h.shape[1]
    z_blk = (proj.shape[1] - vw) // vw
    assert proj.shape[1] % vw == 0 and t % tm == 0
    return pl.pallas_call(
        _gdn_out_kernel,
        grid=(t // tm,),
        in_specs=[
            pl.BlockSpec((tm, vw), lambda i: (i, 0)),
            pl.BlockSpec((tm, vw), lambda i: (i, z_blk)),
            pl.BlockSpec((1, vw), lambda i: (0, 0)),
            pl.BlockSpec((vw, d), lambda i: (0, 0)),
            pl.BlockSpec((1, d), lambda i: (0, 0)),
            pl.BlockSpec((tm, d), lambda i: (i, 0)),
        ],
        out_specs=pl.BlockSpec((tm, d), lambda i: (i, 0)),
        out_shape=jax.ShapeDtypeStruct((t, d), F32),
        compiler_params=_cparams(("parallel",)),
        name="gdn_out",
    )(o, proj, onorm_tiled, w_out, gpost.reshape(1, d), h)


def _compress_kernel(x_ref, pos_ref, w1_ref, w2_ref, o_ref):
    x = x_ref[...]
    pos = pos_ref[...]
    nc, half = x.shape
    w1 = w1_ref[...]
    first = _mm(x + pos[0:1, :], w1[:half])
    second = _mm(x + pos[1:2, :], w1[half:])
    hid = first + pltpu.roll(second, nc - 1, axis=0)
    hid = hid * _sigmoid(hid)
    out = jnp.dot(hid.astype(BF16), w2_ref[...], preferred_element_type=F32)
    rowi = lax.broadcasted_iota(jnp.int32, out.shape, 0)
    o_ref[...] = jnp.where(rowi < nc - 1, out, 0.0)


def compress(x2, pos2, w1, w2):
    _, g, nc, wdt = x2.shape
    hid = w1.shape[2]
    dh = w2.shape[2]
    return pl.pallas_call(
        _compress_kernel,
        grid=(2, g),
        in_specs=[
            pl.BlockSpec((None, None, nc, wdt), lambda b, gi: (b, gi, 0, 0)),
            pl.BlockSpec((None, 2, wdt), lambda b, gi: (b, 0, 0)),
            pl.BlockSpec((None, 2 * wdt, hid), lambda b, gi: (b, 0, 0)),
            pl.BlockSpec((None, hid, dh), lambda b, gi: (b, 0, 0)),
        ],
        out_specs=pl.BlockSpec((None, None, nc, dh), lambda b, gi: (b, gi, 0, 0)),
        out_shape=jax.ShapeDtypeStruct((2, g, nc, dh), F32),
        compiler_params=_cparams(("parallel", "parallel")),
        name="nsa_compress",
    )(x2, pos2, w1, w2)


def _cmp_topk_kernel(q_ref, kbd_ref, vt_ref, m_ref, *refs, tq, nc, nsel, topk, q0):
    oc_ref, sel_ref, s_ref, p_ref, psum_ref = refs[-5:]
    i = q0 + pl.program_id(1)
    dh = NSA_HEAD_DIM
    rows_per = min(nc, CMP_ROWS)
    q = (q_ref[...] * ((dh ** -0.5) * LOG2_E)).astype(BF16)
    s_ref[...] = jnp.dot(kbd_ref[...], q, preferred_element_type=F32)
    for ch in range(tq // LANES):
        lanes = slice(ch * LANES, (ch + 1) * LANES)
        tpos = i * tq + ch * LANES + lax.broadcasted_iota(jnp.int32, (rows_per, LANES), 1)
        cblk0 = lax.broadcasted_iota(jnp.int32, (rows_per, LANES), 0)
        masks = [(CMP_STRIDE * (cblk0 + c * rows_per) + CMP_BLOCK - 1) <= tpos for c in range(nc // rows_per)]
        for r in range(NSA_REP):
            pieces = [slice(r * nc + c * rows_per, r * nc + (c + 1) * rows_per) for c in range(nc // rows_per)]
            m = jnp.full((1, LANES), -jnp.inf, F32)
            for rows, mask in zip(pieces, masks):
                m = jnp.maximum(m, jnp.max(jnp.where(mask, s_ref[rows, lanes], -jnp.inf), axis=0, keepdims=True))
            m = jnp.where(m > -jnp.inf, m, 0.0)
            total = jnp.zeros((1, LANES), F32)
            for rows, mask in zip(pieces, masks):
                e = jnp.exp2(jnp.where(mask, s_ref[rows, lanes], -jnp.inf) - m)
                s_ref[rows, lanes] = e
                total = total + jnp.sum(e, axis=0, keepdims=True)
            inv = 1.0 / jnp.maximum(total, 1e-30)
            for c, rows in enumerate(pieces):
                p = s_ref[rows, lanes] * inv
                p_ref[rows, lanes] = p.astype(BF16)
                prow = slice(c * rows_per, (c + 1) * rows_per)
                psum_ref[prow, lanes] = p if r == 0 else psum_ref[prow, lanes] + p
    oc_t = jnp.dot(vt_ref[...], p_ref[...], preferred_element_type=F32)
    imp = jnp.zeros((nsel, tq), F32)
    for piece in _split3(psum_ref[...]):
        imp = imp + jnp.dot(m_ref[...], piece, preferred_element_type=F32)
    for ch in range(tq // LANES):
        lanes = slice(ch * LANES, (ch + 1) * LANES)
        oc_ref[lanes, :] = oc_t[:, lanes].T
        t1 = i * tq + ch * LANES + lax.broadcasted_iota(jnp.int32, (nsel, LANES), 1)
        blk = lax.broadcasted_iota(jnp.int32, (nsel, LANES), 0)
        cur = _div_pow2(t1, SEL_BLOCK)
        forced = (blk == 0) | (blk == cur) | (blk == cur - 1)
        valid = blk * SEL_BLOCK <= t1
        score = jnp.where(valid, jnp.where(forced, FORCED_SCORE, imp[:, lanes]), -jnp.inf)
        blkf = blk.astype(F32)
        work = score
        for _ in range(topk):
            mx = jnp.max(work, axis=0, keepdims=True)
            first = jnp.min(jnp.where(work == mx, blkf, float(nsel)), axis=0, keepdims=True)
            work = jnp.where(blkf == first, -jnp.inf, work)
        sel_ref[:nsel, lanes] = jnp.where((score > -jnp.inf) & (work == -jnp.inf), 1.0, 0.0).astype(sel_ref.dtype)
        if sel_ref.shape[0] > nsel:
            sel_ref[nsel:, lanes] = jnp.zeros((sel_ref.shape[0] - nsel, LANES), sel_ref.dtype)


def cmp_topk(q_t, kbd, v_t, imp_mat_t, nsel_all, topk, q0, nq, tq, prev):
    qw, t = q_t.shape
    g = kbd.shape[0]
    nc = kbd.shape[1] // NSA_REP
    nsel = imp_mat_t.shape[0]
    assert nc % min(nc, CMP_ROWS) == 0
    in_specs = [
        pl.BlockSpec((NSA_GW, tq), lambda gi, i: (gi, q0 + i)),
        pl.BlockSpec((None, NSA_REP * nc, NSA_GW), lambda gi, i: (gi, 0, 0)),
        pl.BlockSpec((None, NSA_GW, NSA_REP * nc), lambda gi, i: (gi, 0, 0)),
        pl.BlockSpec((nsel, nc), lambda gi, i: (0, 0)),
    ]
    in_specs += [pl.BlockSpec(memory_space=pl.ANY)] * 2
    args = [q_t, kbd, v_t, imp_mat_t, *prev]
    aliases = {len(args) - 2: 0, len(args) - 1: 1}
    return pl.pallas_call(
        functools.partial(_cmp_topk_kernel, tq=tq, nc=nc, nsel=nsel, topk=topk, q0=q0),
        grid=(g, nq),
        in_specs=in_specs,
        out_specs=[
            pl.BlockSpec((tq, NSA_GW), lambda gi, i: (q0 + i, gi)),
            pl.BlockSpec((None, nsel_all, tq), lambda gi, i: (gi, 0, q0 + i)),
        ],
        out_shape=[
            jax.ShapeDtypeStruct((t, qw), F32),
            jax.ShapeDtypeStruct((g, nsel_all, t), F32),
        ],
        input_output_aliases=aliases,
        scratch_shapes=[
            pltpu.VMEM((NSA_REP * nc, tq), F32),
            pltpu.VMEM((NSA_REP * nc, tq), BF16),
            pltpu.VMEM((nc, tq), F32),
        ],
        compiler_params=_cparams(("parallel", "parallel")),
        name="nsa_cmp_topk",
    )(*args)


def _flash_kernel(q_ref, k_ref, v_ref, sel_ref, o_ref,
                  m_ref, l_ref, alpha_ref, acc_ref, s_ref, p_ref, bias_ref, kbd_ref, vt_ref, qs_ref, *, tq, kt):
    qi = pl.program_id(1)
    dh = NSA_HEAD_DIM
    n_sub = tq // FLASH_SUB
    n_blk = kt // SEL_BLOCK

    m_ref[...] = jnp.full_like(m_ref, NEG_INIT)
    l_ref[...] = jnp.zeros_like(l_ref)
    acc_ref[...] = jnp.zeros_like(acc_ref)
    qs_ref[...] = (q_ref[...] * ((dh ** -0.5) * LOG2_E)).astype(BF16)
    vt_ref[dh:, :] = jnp.ones((FLASH_SUM_ROWS, kt), BF16)

    def key_tile(ki, positional, first_sub=0):
        k4 = k_ref[ki]
        kseg = _div_pow2(lax.broadcasted_iota(jnp.int32, k4.shape, 1), dh)
        for r in range(NSA_REP):
            kbd_ref[r * kt:(r + 1) * kt, :] = jnp.where(kseg == r, k4, jnp.zeros_like(k4))
        vt_ref[:dh, :] = v_ref[ki]

        def scores(sub):
            c0 = sub * FLASH_SUB
            cols = slice(c0, c0 + FLASH_SUB)
            s_ref[:, cols] = jnp.dot(kbd_ref[...], qs_ref[:, cols],
                                     preferred_element_type=F32)
            for jb in range(n_blk):
                rows = slice(jb * SEL_BLOCK, (jb + 1) * SEL_BLOCK)
                picked = sel_ref[pl.ds(ki * n_blk + jb, 1), cols] > 0.5
                if positional:
                    tpos = qi * tq + c0 + lax.broadcasted_iota(jnp.int32, (SEL_BLOCK, FLASH_SUB), 1)
                    kpos = ki * kt + jb * SEL_BLOCK + lax.broadcasted_iota(jnp.int32, (SEL_BLOCK, FLASH_SUB), 0)
                    allowed = picked & (kpos <= tpos)
                else:
                    allowed = jnp.broadcast_to(picked, (SEL_BLOCK, FLASH_SUB))
                bias_ref[rows, cols] = jnp.where(allowed, 0.0, -jnp.inf)

        scores(first_sub)
        for sub in range(first_sub, n_sub):
            c0 = sub * FLASH_SUB
            cols = slice(c0, c0 + FLASH_SUB)
            if sub + 1 < n_sub:
                scores(sub + 1)
            for ch in range(FLASH_SUB // LANES):
                lanes = slice(c0 + ch * LANES, c0 + (ch + 1) * LANES)
                bias = bias_ref[:, lanes]
                for r in range(NSA_REP):
                    x = s_ref[r * kt:(r + 1) * kt, lanes] + bias
                    m_prev = m_ref[r:r + 1, lanes]
                    m_new = jnp.maximum(m_prev, jnp.max(x, axis=0, keepdims=True))
                    m_ref[r:r + 1, lanes] = m_new
                    alpha_ref[r:r + 1, lanes] = jnp.exp2(m_prev - m_new)
                    p_ref[r * kt:(r + 1) * kt, lanes] = jnp.exp2(x - m_new).astype(BF16)
            for r in range(NSA_REP):
                pv = jnp.dot(vt_ref[...], p_ref[r * kt:(r + 1) * kt, cols],
                             preferred_element_type=F32)
                hd = slice(r * dh, (r + 1) * dh)
                alpha = alpha_ref[r:r + 1, cols]
                acc_ref[hd, cols] = acc_ref[hd, cols] * alpha + pv[:dh]
                l_ref[r:r + 1, cols] = l_ref[r:r + 1, cols] * alpha + pv[dh:dh + 1]

    def before_diagonal(ki, carry):
        key_tile(ki, False)
        return carry

    diag = qi * (tq // kt)
    lax.fori_loop(0, diag, before_diagonal, 0)
    for d in range(tq // kt):
        def on_diagonal(ki, carry, first_sub=d * kt // FLASH_SUB):
            key_tile(ki, True, first_sub=first_sub)
            return carry
        lax.fori_loop(diag + d, diag + d + 1, on_diagonal, 0)

    for ch in range(tq // LANES):
        lanes = slice(ch * LANES, (ch + 1) * LANES)
        out_t = jnp.concatenate(
            [acc_ref[r * dh:(r + 1) * dh, lanes] / l_ref[r:r + 1, lanes] for r in range(NSA_REP)], axis=0)
        o_ref[lanes, :] = out_t.T


def flash_branch(q_t, k4, v_t, sel_t, tq, kt):
    qw, t = q_t.shape
    g, nk = k4.shape[:2]
    dh = v_t.shape[2]
    nsel = sel_t.shape[1]
    assert t % tq == 0 and tq % kt == 0 and kt % SEL_BLOCK == 0 and tq % FLASH_SUB == 0 and kt % FLASH_SUB == 0
    return pl.pallas_call(
        functools.partial(_flash_kernel, tq=tq, kt=kt),
        grid=(g, t // tq),
        in_specs=[
            pl.BlockSpec((NSA_GW, tq), lambda gi, i: (gi, i)),
            pl.BlockSpec((None, nk, kt, NSA_GW), lambda gi, i: (gi, 0, 0, 0), pipeline_mode=pl.Buffered(1)),
            pl.BlockSpec((None, nk, dh, kt), lambda gi, i: (gi, 0, 0, 0), pipeline_mode=pl.Buffered(1)),
            pl.BlockSpec((None, nsel, tq), lambda gi, i: (gi, 0, i)),
        ],
        out_specs=pl.BlockSpec((tq, NSA_GW), lambda gi, i: (i, gi)),
        scratch_shapes=[
            pltpu.VMEM((8, tq), F32),
            pltpu.VMEM((8, tq), F32),
            pltpu.VMEM((8, tq), F32),
            pltpu.VMEM((NSA_GW, tq), F32),
            pltpu.VMEM((NSA_REP * kt, tq), F32),
            pltpu.VMEM((NSA_REP * kt, tq), BF16),
            pltpu.VMEM((kt, tq), F32),
            pltpu.VMEM((NSA_REP * kt, NSA_GW), BF16),
            pltpu.VMEM((dh + FLASH_SUM_ROWS, kt), BF16),
            pltpu.VMEM((NSA_GW, tq), BF16),
        ],
        out_shape=jax.ShapeDtypeStruct((t, qw), F32),
        compiler_params=_cparams(("parallel", "arbitrary")),
        name="nsa_selected",
    )(q_t, k4, v_t, sel_t)


def _window_kernel(q_ref, k_ref, v_ref, o_ref, s_ref, p_ref, bias_ref, kbd_ref, vt_ref, *, tq, kt):
    i = pl.program_id(1)
    dh = NSA_HEAD_DIM
    nw = WINDOW // kt + 1
    span = nw * kt
    qs = (q_ref[...] * ((dh ** -0.5) * LOG2_E)).astype(BF16)
    for w in range(nw):
        tile = i - (nw - 1) + w
        k4 = k_ref[jnp.maximum(tile, 0)]
        kseg = _div_pow2(lax.broadcasted_iota(jnp.int32, k4.shape, 1), dh)
        for r in range(NSA_REP):
            at = r * span + w * kt
            kbd_ref[at:at + kt, :] = jnp.where(kseg == r, k4, jnp.zeros_like(k4))
        vt_ref[:dh, w * kt:(w + 1) * kt] = v_ref[jnp.maximum(tile, 0)]
        tpos = i * tq + lax.broadcasted_iota(jnp.int32, (kt, tq), 1)
        kpos = tile * kt + lax.broadcasted_iota(jnp.int32, (kt, tq), 0)
        allowed = (kpos >= 0) & (kpos <= tpos) & (kpos > tpos - WINDOW)
        bias_ref[w * kt:(w + 1) * kt, :] = jnp.where(allowed, 0.0, -jnp.inf)
    vt_ref[dh:, :] = jnp.ones((FLASH_SUM_ROWS, span), BF16)

    s_ref[...] = jnp.dot(kbd_ref[...], qs, preferred_element_type=F32)
    for ch in range(tq // LANES):
        lanes = slice(ch * LANES, (ch + 1) * LANES)
        for r in range(NSA_REP):
            m = jnp.full((1, LANES), -jnp.inf, F32)
            for w in range(nw):
                rows = slice(r * span + w * kt, r * span + (w + 1) * kt)
                m = jnp.maximum(m, jnp.max(s_ref[rows, lanes] + bias_ref[w * kt:(w + 1) * kt, lanes],
                                           axis=0, keepdims=True))
            m = jnp.where(m > -jnp.inf, m, 0.0)
            for w in range(nw):
                rows = slice(r * span + w * kt, r * span + (w + 1) * kt)
                x = s_ref[rows, lanes] + bias_ref[w * kt:(w + 1) * kt, lanes]
                p_ref[rows, lanes] = jnp.exp2(x - m).astype(BF16)
    pv = [jnp.dot(vt_ref[...], p_ref[r * span:(r + 1) * span, :], preferred_element_type=F32)
          for r in range(NSA_REP)]
    for ch in range(tq // LANES):
        lanes = slice(ch * LANES, (ch + 1) * LANES)
        out_t = jnp.concatenate(
            [pv[r][:dh, lanes] / jnp.maximum(pv[r][dh:dh + 1, lanes], 1e-30) for r in range(NSA_REP)], axis=0)
        o_ref[lanes, :] = out_t.T


def window_branch(q_t, k4, v_t, tq):
    qw, t = q_t.shape
    g, nk, kt, _ = k4.shape
    dh = v_t.shape[2]
    assert tq == kt and WINDOW % kt == 0 and t % tq == 0
    span = (WINDOW // kt + 1) * kt
    return pl.pallas_call(
        functools.partial(_window_kernel, tq=tq, kt=kt),
        grid=(g, t // tq),
        in_specs=[
            pl.BlockSpec((NSA_GW, tq), lambda gi, i: (gi, i)),
            pl.BlockSpec((None, nk, kt, NSA_GW), lambda gi, i: (gi, 0, 0, 0)),
            pl.BlockSpec((None, nk, dh, kt), lambda gi, i: (gi, 0, 0, 0)),
        ],
        out_specs=pl.BlockSpec((tq, NSA_GW), lambda gi, i: (i, gi)),
        scratch_shapes=[
            pltpu.VMEM((NSA_REP * span, tq), F32),
            pltpu.VMEM((NSA_REP * span, tq), BF16),
            pltpu.VMEM((span, tq), F32),
            pltpu.VMEM((NSA_REP * span, NSA_GW), BF16),
            pltpu.VMEM((dh + FLASH_SUM_ROWS, span), BF16),
        ],
        out_shape=jax.ShapeDtypeStruct((t, qw), F32),
        compiler_params=_cparams(("parallel", "parallel")),
        name="nsa_window",
    )(q_t, k4, v_t)


def _nsa_out_kernel(oc_ref, os_ref, ow_ref, gl_ref, eg_ref, w_ref, gpost_ref, h_ref, out_ref):
    pieces = _split3(_sigmoid(gl_ref[...]))
    mixed = jnp.zeros(oc_ref.shape, F32)
    for b, br_ref in enumerate((oc_ref, os_ref, ow_ref)):
        gfull = jnp.zeros(oc_ref.shape, F32)
        for piece in pieces:
            gfull = gfull + jnp.dot(piece, eg_ref[b], preferred_element_type=F32)
        mixed = mixed + gfull * br_ref[...]
    mix = jnp.dot(mixed.astype(BF16), w_ref[...], preferred_element_type=F32)
    out_ref[...] = h_ref[...] + _rms(mix, gpost_ref[...])


def nsa_out(oc, osel, ow, gate_logits, expand, w_o, gpost, h, tm=512):
    t, qw = oc.shape
    d = h.shape[1]
    assert t % tm == 0
    row = lambda w: pl.BlockSpec((tm, w), lambda i: (i, 0))
    return pl.pallas_call(
        _nsa_out_kernel,
        grid=(t // tm,),
        in_specs=[
            row(qw), row(qw), row(qw), row(LANES),
            pl.BlockSpec((3, LANES, qw), lambda i: (0, 0, 0)),
            pl.BlockSpec((qw, d), lambda i: (0, 0)),
            pl.BlockSpec((1, d), lambda i: (0, 0)),
            row(d),
        ],
        out_specs=row(d),
        out_shape=jax.ShapeDtypeStruct((t, d), F32),
        compiler_params=_cparams(("parallel",)),
        name="nsa_out",
    )(oc, osel, ow, gate_logits, expand, w_o, gpost.reshape(1, d), h)


def _pad_cols(w, n):
    return jnp.pad(w, ((0, 0), (0, n - w.shape[1])))


def _importance_matrix(nc, nsel):
    r = SEL_BLOCK // CMP_STRIDE
    c = CMP_BLOCK // CMP_STRIDE
    mat = np.zeros((nc, nsel), np.float32)
    for kblk in range(nsel):
        for m in range(r):
            for n in range(c):
                j = r * kblk + m - n
                if 0 <= j < nc - 1:
                    mat[j, kblk] += 1.0
    return jnp.asarray(mat, BF16)


def _gate_expand():
    e = np.zeros((3, LANES, NSA_GROUPS * NSA_GW), np.float32)
    for head in range(NSA_GROUPS * NSA_REP):
        for b in range(3):
            e[b, head * 3 + b, head * NSA_HEAD_DIM:(head + 1) * NSA_HEAD_DIM] = 1.0
    return jnp.asarray(e, BF16)


def _block_diag_kv(k_cmp, v_cmp):
    g, nc, dh = k_cmp.shape
    eye = jnp.eye(NSA_REP, dtype=bool)
    kct = jnp.swapaxes(k_cmp, 1, 2)
    kbd = jnp.where(eye[None, :, None, :, None], kct[:, None, :, None, :], 0.0)
    vbd = jnp.where(eye[None, :, None, :, None], v_cmp[:, None, :, None, :], 0.0)
    return (kbd.reshape(g, NSA_REP * dh, NSA_REP * nc).astype(BF16),
            vbd.reshape(g, NSA_REP * nc, NSA_REP * dh).astype(BF16))


def kernel(x, p, mix_pre_norm, mix_post_norm, ffn_pre_norm, ffn_post_norm, gdn_w_in, gdn_conv_w, gdn_a_log,
           gdn_dt_bias, gdn_o_norm, gdn_w_out, kv_norm, kv_w, cmp_pos, cmp_w1, cmp_w2, nsa_w_qg, nsa_w_o,
           ffn_w_in, ffn_w_out, ple_w_in, ple_w_gate):
    depth = p.shape[0]
    n_a = gdn_w_in.shape[0]
    t = x.shape[1]
    h = x[0]
    fh = ffn_w_out.shape[1]
    conv_w_cols = gdn_conv_w.shape[2]
    vw = GDN_V_HEADS * GDN_HEAD_DIM
    main_w = conv_w_cols + vw

    def channel_and_ple(h, i):
        return ffn_ple(h, ffn_pre_norm[i], ffn_w_in[i].astype(BF16), ffn_w_out[i].astype(BF16),
                       ffn_post_norm[i], p[i, 0], ple_w_in[i].astype(BF16), ple_w_gate[i].astype(BF16))

    for i in range(n_a):
        w_in = gdn_w_in[i]
        w_beta = w_in[:, main_w:main_w + GDN_V_HEADS]
        w_a = w_in[:, main_w + GDN_V_HEADS:]
        w_small = _pad_cols(jnp.concatenate([w_beta, w_a, w_a], axis=1), LANES).astype(BF16)
        proj = norm_matmul(h, mix_pre_norm[i], w_in.astype(BF16), n=main_w)
        ba = norm_matmul(h, mix_pre_norm[i], w_small)
        pad_vec = lambda v: jnp.pad(v, (GDN_V_HEADS, LANES - 2 * GDN_V_HEADS))
        alog_vec = (pad_vec(gdn_a_log[i]) + jnp.pad(gdn_a_log[i], (2 * GDN_V_HEADS, LANES - 3 * GDN_V_HEADS)))
        dtb_vec = (pad_vec(gdn_dt_bias[i]) + jnp.pad(gdn_dt_bias[i], (2 * GDN_V_HEADS, LANES - 3 * GDN_V_HEADS)))
        slab = gdn_gates(ba, alog_vec.reshape(1, LANES), dtb_vec.reshape(1, LANES))
        qkv_hm = gdn_conv(proj, gdn_conv_w[i])
        kt_hm = jnp.swapaxes(qkv_hm[GDN_QK_HEADS:2 * GDN_QK_HEADS], 1, 2)
        o = gdn_scan(qkv_hm, kt_hm, slab, slab.T)
        onorm_tiled = jnp.tile(gdn_o_norm[i], GDN_V_HEADS).reshape(1, vw)
        h = gdn_out(o, proj, onorm_tiled, gdn_w_out[i].astype(BF16), mix_post_norm[i], h)
        h = channel_and_ple(h, i)

    g = NSA_GROUPS
    dh = NSA_HEAD_DIM
    kv = norm_matmul(h, kv_norm, kv_w.astype(BF16), tn=768)
    kv6 = jnp.transpose(kv.reshape(t, 6, g, dh), (1, 2, 0, 3))
    nc = t // CMP_STRIDE
    nsel = t // SEL_BLOCK
    x2 = kv6[0:2].reshape(2, g, nc, CMP_STRIDE * dh)
    pos2 = cmp_pos.reshape(2, 2, CMP_STRIDE * dh)
    cmp_out = compress(x2, pos2, cmp_w1.astype(BF16), cmp_w2.astype(BF16))
    imp_mat_t = _importance_matrix(nc, nsel).T
    nq_r = t // (CMP_TQ * CMP_RANGES)
    assert t % (CMP_TQ * CMP_RANGES) == 0 and nc % CMP_RANGES == 0 and nsel % (8 * CMP_RANGES) == 0
    cmp_ranges = []
    for rg in range(CMP_RANGES):
        nc_r, nsel_r = (rg + 1) * nc // CMP_RANGES, (rg + 1) * nsel // CMP_RANGES
        vbd_t, kbd = _block_diag_kv(cmp_out[1][:, :nc_r], cmp_out[0][:, :nc_r])
        cmp_ranges.append((kbd, vbd_t, imp_mat_t[:nsel_r, :nc_r]))
    kt = FLASH_KT
    tiles = lambda a: a.reshape(g, t // kt, kt, dh)
    rep_k = lambda a: jnp.tile(tiles(a), (1, 1, 1, NSA_REP)).astype(BF16)
    rep_t = lambda a: jnp.swapaxes(tiles(a), 2, 3).astype(BF16)
    k_slc, v_slc_t = rep_k(kv6[2]), rep_t(kv6[3])
    k_win, v_win_t = rep_k(kv6[4]), rep_t(kv6[5])
    expand = _gate_expand()

    for i in range(n_a, depth):
        j = i - n_a
        qw = g * NSA_GW
        w_qg = nsa_w_qg[j]
        q_t = norm_matmul_t(h, mix_pre_norm[i], w_qg[:, :qw].T.astype(BF16))
        gate_logits = norm_matmul(h, mix_pre_norm[i], _pad_cols(w_qg[:, qw:], LANES).astype(BF16))
        cmp_res = (jnp.zeros((t, qw), F32), jnp.zeros((g, nsel, t), F32))
        for rg, (kbd, vbd_t, imp_r) in enumerate(cmp_ranges):
            cmp_res = cmp_topk(q_t, kbd, vbd_t, imp_r, nsel, min(SEL_TOPK, nsel), rg * nq_r, nq_r, CMP_TQ,
                               prev=cmp_res)
        o_c, sel_t = cmp_res
        o_s = flash_branch(q_t, k_slc, v_slc_t, sel_t, tq=min(FLASH_TQ, t), kt=kt)
        o_w = window_branch(q_t, k_win, v_win_t, tq=kt)
        h = nsa_out(o_c, o_s, o_w, gate_logits, expand, nsa_w_o[j].astype(BF16), mix_post_norm[i], h)
        h = channel_and_ple(h, i)
    return h[None]
```

```python
import functools

import numpy as np
import jax
import jax.numpy as jnp
from jax import lax
from jax.experimental import pallas as pl
from jax.experimental.pallas import tpu as pltpu

F32 = jnp.float32
BF16 = jnp.bfloat16

NORM_EPS = 1e-6
L2_EPS = 1e-6
GDN_QK_HEADS = 8
GDN_V_HEADS = 16
GDN_HEAD_DIM = 128
GDN_CONV = 4
GDN_CHUNK = 64
GDN_GROUP = 256
GDN_SCAN_HEADS = 8
GDN_CONV_HEADS = 4
NSA_GROUPS = 4
NSA_REP = 4
NSA_HEAD_DIM = 64
NSA_GW = NSA_REP * NSA_HEAD_DIM
CMP_BLOCK = 32
CMP_STRIDE = 16
SEL_BLOCK = 64
SEL_TOPK = 16
WINDOW = 512
FORCED_SCORE = 1e4
LANES = 128
NEG_INIT = -(2.0 ** 100)
LOG2_E = 1.4426950408889634
FLASH_TQ = 2048
FLASH_KT = 256
FLASH_SUB = 256
FLASH_SUM_ROWS = 16
CMP_ROWS = 256
CMP_TQ = 256
CMP_RANGES = 4

VMEM_LIMIT = 56 * 1024 * 1024
FFN_VMEM_LIMIT = 60 * 1024 * 1024


def _cparams(sem, vmem_limit=VMEM_LIMIT):
    return pltpu.CompilerParams(dimension_semantics=sem, vmem_limit_bytes=vmem_limit)


def _rms(x, gain):
    return x * lax.rsqrt(jnp.mean(x * x, axis=-1, keepdims=True) + NORM_EPS) * gain


def _mm(a, b):
    return jnp.dot(a.astype(BF16), b.astype(BF16), preferred_element_type=F32)


def _sigmoid(x):
    return 1.0 / (1.0 + jnp.exp(-x))


def _div_pow2(x, d):
    shift = d.bit_length() - 1
    assert d == 1 << shift
    return jnp.right_shift(x, shift)


def _split3(x):
    a = x.astype(BF16)
    r = x - a.astype(F32)
    b = r.astype(BF16)
    c = (r - b.astype(F32)).astype(BF16)
    return a, b, c


def _norm_matmul_kernel(x_ref, g_ref, w_ref, o_ref, xn_ref):
    @pl.when(pl.program_id(1) == 0)
    def _():
        xn_ref[...] = _rms(x_ref[...], g_ref[...]).astype(BF16)

    o_ref[...] = jnp.dot(xn_ref[...], w_ref[...], preferred_element_type=F32)


def _norm_matmul_t_kernel(x_ref, g_ref, wt_ref, o_ref, xn_ref):
    @pl.when(pl.program_id(1) == 0)
    def _():
        xn_ref[...] = _rms(x_ref[...], g_ref[...]).astype(BF16)

    o_ref[...] = lax.dot_general(wt_ref[...], xn_ref[...], (((1,), (1,)), ((), ())), preferred_element_type=F32)


def norm_matmul_t(h, gain, w_t, tm=1024, tn=1024):
    t, d = h.shape
    n = w_t.shape[0]
    tn = min(tn, n)
    assert t % tm == 0 and n % tn == 0
    return pl.pallas_call(
        _norm_matmul_t_kernel,
        grid=(t // tm, n // tn),
        in_specs=[
            pl.BlockSpec((tm, d), lambda i, j: (i, 0)),
            pl.BlockSpec((1, d), lambda i, j: (0, 0)),
            pl.BlockSpec((tn, d), lambda i, j: (j, 0)),
        ],
        out_specs=pl.BlockSpec((tn, tm), lambda i, j: (j, i)),
        out_shape=jax.ShapeDtypeStruct((n, t), F32),
        scratch_shapes=[pltpu.VMEM((tm, d), BF16)],
        compiler_params=_cparams(("parallel", "arbitrary")),
        name="norm_matmul_t",
    )(h, gain.reshape(1, d), w_t)


def norm_matmul(h, gain, w, tm=1024, tn=1024, n=None):
    t, d = h.shape
    n = w.shape[1] if n is None else n
    tn = min(tn, n)
    assert t % tm == 0 and n % tn == 0
    return pl.pallas_call(
        _norm_matmul_kernel,
        grid=(t // tm, n // tn),
        in_specs=[
            pl.BlockSpec((tm, d), lambda i, j: (i, 0)),
            pl.BlockSpec((1, d), lambda i, j: (0, 0)),
            pl.BlockSpec((d, tn), lambda i, j: (0, j)),
        ],
        out_specs=pl.BlockSpec((tm, tn), lambda i, j: (i, j)),
        out_shape=jax.ShapeDtypeStruct((t, n), F32),
        scratch_shapes=[pltpu.VMEM((tm, d), BF16)],
        compiler_params=_cparams(("parallel", "arbitrary")),
        name="norm_matmul",
    )(h, gain.reshape(1, d), w)


def _ffn_ple_kernel(h_ref, gpre_ref, wg_ref, wu_ref, wo_ref, gpost_ref, p_ref, wple_ref, wgt_ref,
                    o_ref, xn_ref, acc_ref, *, nf):
    f = pl.program_id(1)

    @pl.when(f == 0)
    def _():
        xn_ref[...] = _rms(h_ref[...], gpre_ref[...]).astype(BF16)
        acc_ref[...] = jnp.zeros_like(acc_ref)

    xn = xn_ref[...]
    tf = wg_ref.shape[1]
    half = (tf // LANES // 2) * LANES
    for a, b in ((0, half), (half, tf)):
        gate = jnp.dot(xn, wg_ref[:, a:b], preferred_element_type=F32)
        up = jnp.dot(xn, wu_ref[:, a:b], preferred_element_type=F32)
        act = gate * _sigmoid(gate) * up
        acc_ref[...] += jnp.dot(act.astype(BF16), wo_ref[a:b, :], preferred_element_type=F32)

    @pl.when(f == nf - 1)
    def _():
        h2 = h_ref[...] + _rms(acc_ref[...], gpost_ref[...])
        emb = jnp.dot(p_ref[...].astype(BF16), wple_ref[...], preferred_element_type=F32)
        gt = _sigmoid(jnp.dot(h2.astype(BF16), wgt_ref[...], preferred_element_type=F32))
        o_ref[...] = h2 + emb * gt


def ffn_ple(h, gpre, w_in, w_out, gpost, p, w_ple, w_plegate, tm=1024, tf=1408):
    t, d = h.shape
    fh = w_out.shape[0]
    pd = p.shape[1]
    assert t % tm == 0 and fh % tf == 0 and w_in.shape[1] == 2 * fh
    nf = fh // tf
    return pl.pallas_call(
        functools.partial(_ffn_ple_kernel, nf=nf),
        grid=(t // tm, nf),
        in_specs=[
            pl.BlockSpec((tm, d), lambda i, f: (i, 0)),
            pl.BlockSpec((1, d), lambda i, f: (0, 0)),
            pl.BlockSpec((d, tf), lambda i, f: (0, f)),
            pl.BlockSpec((d, tf), lambda i, f: (0, nf + f)),
            pl.BlockSpec((tf, d), lambda i, f: (f, 0)),
            pl.BlockSpec((1, d), lambda i, f: (0, 0)),
            pl.BlockSpec((tm, pd), lambda i, f: (i, 0)),
            pl.BlockSpec((pd, d), lambda i, f: (0, 0), pipeline_mode=pl.Buffered(1)),
            pl.BlockSpec((d, d), lambda i, f: (0, 0), pipeline_mode=pl.Buffered(1)),
        ],
        out_specs=pl.BlockSpec((tm, d), lambda i, f: (i, 0)),
        out_shape=jax.ShapeDtypeStruct((t, d), F32),
        scratch_shapes=[pltpu.VMEM((tm, d), BF16), pltpu.VMEM((tm, d), F32)],
        compiler_params=_cparams(("parallel", "arbitrary"), FFN_VMEM_LIMIT),
        name="ffn_ple",
    )(h, gpre.reshape(1, d), w_in, w_in, w_out, gpost.reshape(1, d), p, w_ple, w_plegate)


def _gdn_conv_kernel(x_ref, halo_ref, w_ref, o_ref, *, tm):
    c = pl.program_id(0)
    i = pl.program_id(1)
    x = x_ref[...]
    halo = jnp.where(i > 0, halo_ref[...], 0.0)
    ext = jnp.concatenate([halo, x], axis=0)
    w = w_ref[...]
    y = x * w[GDN_CONV - 1:GDN_CONV, :]
    for k in range(1, GDN_CONV):
        shifted = pltpu.roll(ext, k, axis=0)[8:8 + tm]
        y = y + shifted * w[GDN_CONV - 1 - k:GDN_CONV - k, :]
    y = y * _sigmoid(y)
    for hd in range(GDN_CONV_HEADS):
        head = c * GDN_CONV_HEADS + hd
        seg = y[:, hd * LANES:(hd + 1) * LANES]
        normed = seg * lax.rsqrt(jnp.sum(seg * seg, axis=-1, keepdims=True) + L2_EPS)
        q_scale = jnp.where(head < GDN_QK_HEADS, GDN_HEAD_DIM ** -0.5, 1.0)
        o_ref[hd] = jnp.where(head < 2 * GDN_QK_HEADS, normed * q_scale, seg)


def gdn_conv(proj, conv_w, tm=1024):
    t = proj.shape[0]
    n_tiles = conv_w.shape[1] // LANES
    cw = GDN_CONV_HEADS
    assert t % tm == 0 and n_tiles % cw == 0
    return pl.pallas_call(
        functools.partial(_gdn_conv_kernel, tm=tm),
        grid=(n_tiles // cw, t // tm),
        in_specs=[
            pl.BlockSpec((tm, cw * LANES), lambda c, i: (i, c)),
            pl.BlockSpec((8, cw * LANES), lambda c, i: (jnp.maximum(i * (tm // 8) - 1, 0), c)),
            pl.BlockSpec((GDN_CONV, cw * LANES), lambda c, i: (0, c)),
        ],
        out_specs=pl.BlockSpec((cw, tm, LANES), lambda c, i: (c, i, 0)),
        out_shape=jax.ShapeDtypeStruct((n_tiles, t, LANES), F32),
        compiler_params=_cparams(("parallel", "parallel")),
        name="gdn_conv",
    )(proj, proj, conv_w)


def _gdn_gate_kernel(x_ref, alog_ref, dtb_ref, lc_ref, lf_ref, o_ref):
    x = x_ref[...]
    lane = lax.broadcasted_iota(jnp.int32, x.shape, 1)
    beta = _sigmoid(x)
    z = x + dtb_ref[...]
    softplus = jnp.maximum(z, 0.0) + jnp.log(1.0 + jnp.exp(-jnp.abs(z)))
    g = -jnp.exp(alog_ref[...]) * softplus
    gcum = jnp.zeros_like(x)
    gtot = jnp.zeros_like(x)
    for piece in _split3(g):
        gcum = gcum + jnp.dot(lc_ref[...], piece, preferred_element_type=F32)
        gtot = gtot + jnp.dot(lf_ref[...], piece, preferred_element_type=F32)
    o_ref[...] = jnp.where(lane < GDN_V_HEADS, beta, jnp.where(lane < 2 * GDN_V_HEADS, gcum, gtot))


def gdn_gates(ba, alog_vec, dtb_vec):
    t = ba.shape[0]
    tm = GDN_GROUP
    r = np.arange(tm)
    same = (r[:, None] // GDN_CHUNK) == (r[None, :] // GDN_CHUNK)
    lc = jnp.asarray(same & (r[None, :] <= r[:, None]), BF16)
    lf = jnp.asarray(same, BF16)
    return pl.pallas_call(
        _gdn_gate_kernel,
        grid=(t // tm,),
        in_specs=[
            pl.BlockSpec((tm, LANES), lambda i: (i, 0)),
            pl.BlockSpec((1, LANES), lambda i: (0, 0)),
            pl.BlockSpec((1, LANES), lambda i: (0, 0)),
            pl.BlockSpec((tm, tm), lambda i: (0, 0)),
            pl.BlockSpec((tm, tm), lambda i: (0, 0)),
        ],
        out_specs=pl.BlockSpec((tm, LANES), lambda i: (i, 0)),
        out_shape=jax.ShapeDtypeStruct((t, LANES), F32),
        compiler_params=_cparams(("parallel",)),
        name="gdn_gates",
    )(ba, alog_vec, dtb_vec, lc, lf)


def _gdn_scan_kernel(q_ref, k_ref, kt_ref, v_ref, slab_ref, slabt_ref, o_ref, s_ref):
    n = pl.program_id(1)

    @pl.when(n == 0)
    def _():
        s_ref[...] = jnp.zeros_like(s_ref)

    L = GDN_GROUP
    C = GDN_CHUNK
    D = GDN_HEAD_DIM
    slab = slab_ref[...]
    slabt = slabt_ref[...]
    row = lax.broadcasted_iota(jnp.int32, (L, L), 0)
    col = lax.broadcasted_iota(jnp.int32, (L, L), 1)
    same = _div_pow2(row, C) == _div_pow2(col, C)
    causal = same & (col <= row)
    strict = same & (col < row)
    eye = (row == col).astype(F32)
    lane = lax.broadcasted_iota(jnp.int32, (L, LANES), 1)
    sub = lax.broadcasted_iota(jnp.int32, (LANES, L), 0)

    def column(idx):
        return jnp.sum(jnp.where(lane == idx, slab, 0.0), axis=1, keepdims=True)

    def rowvec(idx):
        return jnp.sum(jnp.where(sub == idx, slabt, 0.0), axis=0, keepdims=True)

    colk = lax.broadcasted_iota(jnp.int32, (D, L), 1)
    slots = range(2 * GDN_SCAN_HEADS)

    q = [q_ref[hq] for hq in range(GDN_SCAN_HEADS)]
    k = [k_ref[hq] for hq in range(GDN_SCAN_HEADS)]
    kt = [kt_ref[hq] for hq in range(GDN_SCAN_HEADS)]
    kk = [_mm(k[hq], kt[hq]) for hq in range(GDN_SCAN_HEADS)]
    qk = [_mm(q[hq], kt[hq]) for hq in range(GDN_SCAN_HEADS)]
    beta_c, gc_c, gt_c, decay, bp, inv, kdt = [], [], [], [], [], [], []
    for slot in slots:
        hq = slot // 2
        hv = 2 * (pl.program_id(0) * GDN_SCAN_HEADS + hq) + slot % 2
        beta_c.append(column(hv))
        gc_c.append(column(GDN_V_HEADS + hv))
        gt_c.append(column(2 * GDN_V_HEADS + hv))
        gc_r = rowvec(GDN_V_HEADS + hv)
        gt_r = rowvec(2 * GDN_V_HEADS + hv)
        decay.append(jnp.where(causal, jnp.exp(jnp.where(causal, gc_c[slot] - gc_r, 0.0)), 0.0))
        bp.append(jnp.where(strict, -(kk[hq] * beta_c[slot]) * decay[slot], 0.0))
        inv.append(eye + bp[slot])
        kdt.append(kt[hq] * jnp.exp(gt_r - gc_r))
    for _ in range(5):
        bp = [_mm(bp[slot], bp[slot]) for slot in slots]
        inv = [inv[slot] + _mm(inv[slot], bp[slot]) for slot in slots]
    u, w, qkm, q_dec = [], [], [], []
    for slot in slots:
        hq = slot // 2
        egc = jnp.exp(gc_c[slot])
        rhs = jnp.concatenate([v_ref[slot] * beta_c[slot], k[hq] * (beta_c[slot] * egc)], axis=1)
        sol = _mm(inv[slot], rhs)
        u.append(sol[:, :D])
        w.append(sol[:, D:])
        qkm.append(jnp.where(causal, qk[hq] * decay[slot], 0.0))
        q_dec.append(q[hq] * egc)
    state = [s_ref[slot] for slot in slots]
    v_done = [[] for _ in slots]
    for c in range(L // C):
        lo, hi = c * C, (c + 1) * C
        r = [_mm(jnp.concatenate([w[slot][lo:hi], q_dec[slot][lo:hi]], axis=0), state[slot]) for slot in slots]
        for slot in slots:
            v_done[slot].append(u[slot][lo:hi] - r[slot][:C])
            v_all = jnp.concatenate(v_done[slot] + [jnp.zeros((L - hi, D), F32)] * (hi < L), axis=0)
            kdt_c = jnp.where((colk >= lo) & (colk < hi), kdt[slot], 0.0)
            both = _mm(jnp.concatenate([qkm[slot][lo:hi, :], kdt_c], axis=0), v_all)
            o_ref[lo:hi, slot * D:(slot + 1) * D] = r[slot][C:] + both[:C]
            state[slot] = state[slot] * jnp.exp(gt_c[slot][lo:lo + 1, :]) + both[C:]
    for slot in slots:
        s_ref[slot] = state[slot]


def gdn_scan(qkv_hm, kt_hm, slab, slabt):
    t = qkv_hm.shape[1]
    L = GDN_GROUP
    D = GDN_HEAD_DIM
    hq = GDN_SCAN_HEADS
    assert t % L == 0 and GDN_QK_HEADS % hq == 0
    q_blocks = GDN_QK_HEADS // hq
    return pl.pallas_call(
        _gdn_scan_kernel,
        grid=(q_blocks, t // L),
        in_specs=[
            pl.BlockSpec((hq, L, D), lambda j, n: (j, n, 0)),
            pl.BlockSpec((hq, L, D), lambda j, n: (q_blocks + j, n, 0)),
            pl.BlockSpec((hq, D, L), lambda j, n: (j, 0, n)),
            pl.BlockSpec((2 * hq, L, D), lambda j, n: (q_blocks + j, n, 0)),
            pl.BlockSpec((L, LANES), lambda j, n: (n, 0)),
            pl.BlockSpec((LANES, L), lambda j, n: (0, n)),
        ],
        out_specs=pl.BlockSpec((L, 2 * hq * D), lambda j, n: (n, j)),
        out_shape=jax.ShapeDtypeStruct((t, GDN_V_HEADS * D), F32),
        scratch_shapes=[pltpu.VMEM((2 * hq, D, D), F32)],
        compiler_params=_cparams(("parallel", "arbitrary")),
        name="gdn_scan",
    )(qkv_hm, qkv_hm, kt_hm, qkv_hm, slab, slabt)


def _gdn_out_kernel(o_ref, z_ref, onorm_ref, w_ref, gpost_ref, h_ref, out_ref):
    o = o_ref[...]
    z = z_ref[...]
    parts = []
    for hd in range(GDN_V_HEADS):
        seg = o[:, hd * GDN_HEAD_DIM:(hd + 1) * GDN_HEAD_DIM]
        parts.append(seg * lax.rsqrt(jnp.mean(seg * seg, axis=-1, keepdims=True) + NORM_EPS))
    gated = jnp.concatenate(parts, axis=1) * onorm_ref[...] * (z * _sigmoid(z))
    mix = jnp.dot(gated.astype(BF16), w_ref[...], preferred_element_type=F32)
    out_ref[...] = h_ref[...] + _rms(mix, gpost_ref[...])


def gdn_out(o, proj, onorm_tiled, w_out, gpost, h, tm=512):
    t, vw = o.shape
    d = h.shape[1]
    z_blk = (proj.shape[1] - vw) // vw
    assert proj.shape[1] % vw == 0 and t % tm == 0
    return pl.pallas_call(
        _gdn_out_kernel,
        grid=(t // tm,),
        in_specs=[
            pl.BlockSpec((tm, vw), lambda i: (i, 0)),
            pl.BlockSpec((tm, vw), lambda i: (i, z_blk)),
            pl.BlockSpec((1, vw), lambda i: (0, 0)),
            pl.BlockSpec((vw, d), lambda i: (0, 0)),
            pl.BlockSpec((1, d), lambda i: (0, 0)),
            pl.BlockSpec((tm, d), lambda i: (i, 0)),
        ],
        out_specs=pl.BlockSpec((tm, d), lambda i: (i, 0)),
        out_shape=jax.ShapeDtypeStruct((t, d), F32),
        compiler_params=_cparams(("parallel",)),
        name="gdn_out",
    )(o, proj, onorm_tiled, w_out, gpost.reshape(1, d), h)


def _compress_kernel(x_ref, pos_ref, w1_ref, w2_ref, o_ref):
    x = x_ref[...]
    pos = pos_ref[...]
    nc, half = x.shape
    w1 = w1_ref[...]
    first = _mm(x + pos[0:1, :], w1[:half])
    second = _mm(x + pos[1:2, :], w1[half:])
    hid = first + pltpu.roll(second, nc - 1, axis=0)
    hid = hid * _sigmoid(hid)
    out = jnp.dot(hid.astype(BF16), w2_ref[...], preferred_element_type=F32)
    rowi = lax.broadcasted_iota(jnp.int32, out.shape, 0)
    o_ref[...] = jnp.where(rowi < nc - 1, out, 0.0)


def compress(x2, pos2, w1, w2):
    _, g, nc, wdt = x2.shape
    hid = w1.shape[2]
    dh = w2.shape[2]
    return pl.pallas_call(
        _compress_kernel,
        grid=(2, g),
        in_specs=[
            pl.BlockSpec((None, None, nc, wdt), lambda b, gi: (b, gi, 0, 0)),
            pl.BlockSpec((None, 2, wdt), lambda b, gi: (b, 0, 0)),
            pl.BlockSpec((None, 2 * wdt, hid), lambda b, gi: (b, 0, 0)),
            pl.BlockSpec((None, hid, dh), lambda b, gi: (b, 0, 0)),
        ],
        out_specs=pl.BlockSpec((None, None, nc, dh), lambda b, gi: (b, gi, 0, 0)),
        out_shape=jax.ShapeDtypeStruct((2, g, nc, dh), F32),
        compiler_params=_cparams(("parallel", "parallel")),
        name="nsa_compress",
    )(x2, pos2, w1, w2)


def _cmp_topk_kernel(q_ref, kbd_ref, vt_ref, m_ref, *refs, tq, nc, nsel, topk, q0):
    oc_ref, sel_ref, s_ref, p_ref, psum_ref = refs[-5:]
    i = q0 + pl.program_id(1)
    dh = NSA_HEAD_DIM
    rows_per = min(nc, CMP_ROWS)
    q = (q_ref[...] * ((dh ** -0.5) * LOG2_E)).astype(BF16)
    s_ref[...] = jnp.dot(kbd_ref[...], q, preferred_element_type=F32)
    for ch in range(tq // LANES):
        lanes = slice(ch * LANES, (ch + 1) * LANES)
        tpos = i * tq + ch * LANES + lax.broadcasted_iota(jnp.int32, (rows_per, LANES), 1)
        cblk0 = lax.broadcasted_iota(jnp.int32, (rows_per, LANES), 0)
        masks = [(CMP_STRIDE * (cblk0 + c * rows_per) + CMP_BLOCK - 1) <= tpos for c in range(nc // rows_per)]
        for r in range(NSA_REP):
            pieces = [slice(r * nc + c * rows_per, r * nc + (c + 1) * rows_per) for c in range(nc // rows_per)]
            m = jnp.full((1, LANES), -jnp.inf, F32)
            for rows, mask in zip(pieces, masks):
                m = jnp.maximum(m, jnp.max(jnp.where(mask, s_ref[rows, lanes], -jnp.inf), axis=0, keepdims=True))
            m = jnp.where(m > -jnp.inf, m, 0.0)
            total = jnp.zeros((1, LANES), F32)
            for rows, mask in zip(pieces, masks):
                e = jnp.exp2(jnp.where(mask, s_ref[rows, lanes], -jnp.inf) - m)
                s_ref[rows, lanes] = e
                total = total + jnp.sum(e, axis=0, keepdims=True)
            inv = 1.0 / jnp.maximum(total, 1e-30)
            for c, rows in enumerate(pieces):
                p = s_ref[rows, lanes] * inv
                p_ref[rows, lanes] = p.astype(BF16)
                prow = slice(c * rows_per, (c + 1) * rows_per)
                psum_ref[prow, lanes] = p if r == 0 else psum_ref[prow, lanes] + p
    oc_t = jnp.concatenate(
        [jnp.dot(vt_ref[...], p_ref[r * nc:(r + 1) * nc, :], preferred_element_type=F32) for r in range(NSA_REP)],
        axis=0)
    imp = jnp.zeros((nsel, tq), F32)
    for piece in _split3(psum_ref[...]):
        imp = imp + jnp.dot(m_ref[...], piece, preferred_element_type=F32)
    for ch in range(tq // LANES):
        lanes = slice(ch * LANES, (ch + 1) * LANES)
        oc_ref[lanes, :] = oc_t[:, lanes].T
        t1 = i * tq + ch * LANES + lax.broadcasted_iota(jnp.int32, (nsel, LANES), 1)
        blk = lax.broadcasted_iota(jnp.int32, (nsel, LANES), 0)
        cur = _div_pow2(t1, SEL_BLOCK)
        forced = (blk == 0) | (blk == cur) | (blk == cur - 1)
        valid = blk * SEL_BLOCK <= t1
        score = jnp.where(valid, jnp.where(forced, FORCED_SCORE, imp[:, lanes]), -jnp.inf)
        blkf = blk.astype(F32)
        work = score
        for _ in range(topk):
            mx = jnp.max(work, axis=0, keepdims=True)
            first = jnp.min(jnp.where(work == mx, blkf, float(nsel)), axis=0, keepdims=True)
            work = jnp.where(blkf == first, -jnp.inf, work)
        sel_ref[:nsel, lanes] = jnp.where((score > -jnp.inf) & (work == -jnp.inf), 1.0, 0.0).astype(sel_ref.dtype)
        if sel_ref.shape[0] > nsel:
            sel_ref[nsel:, lanes] = jnp.zeros((sel_ref.shape[0] - nsel, LANES), sel_ref.dtype)


def cmp_topk(q_t, kbd, v_t, imp_mat_t, nsel_all, topk, q0, nq, tq, prev):
    qw, t = q_t.shape
    g = kbd.shape[0]
    nc = kbd.shape[1] // NSA_REP
    nsel = imp_mat_t.shape[0]
    assert nc % min(nc, CMP_ROWS) == 0
    in_specs = [
        pl.BlockSpec((NSA_GW, tq), lambda gi, i: (gi, q0 + i)),
        pl.BlockSpec((None, NSA_REP * nc, NSA_GW), lambda gi, i: (gi, 0, 0)),
        pl.BlockSpec((None, NSA_HEAD_DIM, nc), lambda gi, i: (gi, 0, 0)),
        pl.BlockSpec((nsel, nc), lambda gi, i: (0, 0)),
    ]
    in_specs += [pl.BlockSpec(memory_space=pl.ANY)] * 2
    args = [q_t, kbd, v_t, imp_mat_t, *prev]
    aliases = {len(args) - 2: 0, len(args) - 1: 1}
    return pl.pallas_call(
        functools.partial(_cmp_topk_kernel, tq=tq, nc=nc, nsel=nsel, topk=topk, q0=q0),
        grid=(g, nq),
        in_specs=in_specs,
        out_specs=[
            pl.BlockSpec((tq, NSA_GW), lambda gi, i: (q0 + i, gi)),
            pl.BlockSpec((None, nsel_all, tq), lambda gi, i: (gi, 0, q0 + i)),
        ],
        out_shape=[
            jax.ShapeDtypeStruct((t, qw), F32),
            jax.ShapeDtypeStruct((g, nsel_all, t), F32),
        ],
        input_output_aliases=aliases,
        scratch_shapes=[
            pltpu.VMEM((NSA_REP * nc, tq), F32),
            pltpu.VMEM((NSA_REP * nc, tq), BF16),
            pltpu.VMEM((nc, tq), F32),
        ],
        compiler_params=_cparams(("parallel", "parallel")),
        name="nsa_cmp_topk",
    )(*args)


def _flash_kernel(q_ref, k_ref, v_ref, sel_ref, o_ref,
                  m_ref, l_ref, alpha_ref, acc_ref, s_ref, p_ref, bias_ref, kbd_ref, vt_ref, qs_ref, *, tq, kt):
    qi = pl.program_id(1)
    dh = NSA_HEAD_DIM
    n_sub = tq // FLASH_SUB
    n_blk = kt // SEL_BLOCK

    m_ref[...] = jnp.full_like(m_ref, NEG_INIT)
    l_ref[...] = jnp.zeros_like(l_ref)
    acc_ref[...] = jnp.zeros_like(acc_ref)
    qs_ref[...] = (q_ref[...] * ((dh ** -0.5) * LOG2_E)).astype(BF16)
    vt_ref[dh:, :] = jnp.ones((FLASH_SUM_ROWS, kt), BF16)

    def key_tile(ki, positional, first_sub=0):
        k4 = k_ref[ki]
        kseg = _div_pow2(lax.broadcasted_iota(jnp.int32, k4.shape, 1), dh)
        for r in range(NSA_REP):
            kbd_ref[r * kt:(r + 1) * kt, :] = jnp.where(kseg == r, k4, jnp.zeros_like(k4))
        vt_ref[:dh, :] = v_ref[ki]

        def scores(sub):
            c0 = sub * FLASH_SUB
            cols = slice(c0, c0 + FLASH_SUB)
            s_ref[:, cols] = jnp.dot(kbd_ref[...], qs_ref[:, cols],
                                     preferred_element_type=F32)
            if not positional:
                return
            for jb in range(n_blk):
                rows = slice(jb * SEL_BLOCK, (jb + 1) * SEL_BLOCK)
                picked = sel_ref[pl.ds(ki * n_blk + jb, 1), cols] > 0.5
                tpos = qi * tq + c0 + lax.broadcasted_iota(jnp.int32, (SEL_BLOCK, FLASH_SUB), 1)
                kpos = ki * kt + jb * SEL_BLOCK + lax.broadcasted_iota(jnp.int32, (SEL_BLOCK, FLASH_SUB), 0)
                bias_ref[rows, cols] = jnp.where(picked & (kpos <= tpos), 0.0, -jnp.inf)

        scores(first_sub)
        for sub in range(first_sub, n_sub):
            c0 = sub * FLASH_SUB
            cols = slice(c0, c0 + FLASH_SUB)
            if sub + 1 < n_sub:
                scores(sub + 1)
            if not positional:
                picked_sub = [sel_ref[pl.ds(ki * n_blk + jb, 1), cols] > 0.5 for jb in range(n_blk)]
            for ch in range(FLASH_SUB // LANES):
                lanes = slice(c0 + ch * LANES, c0 + (ch + 1) * LANES)
                if positional:
                    bias = bias_ref[:, lanes]
                else:
                    picked = [pk[:, ch * LANES:(ch + 1) * LANES] for pk in picked_sub]
                for r in range(NSA_REP):
                    if positional:
                        x = s_ref[r * kt:(r + 1) * kt, lanes] + bias
                    else:
                        x = jnp.concatenate(
                            [jnp.where(picked[jb], s_ref[r * kt + jb * SEL_BLOCK:r * kt + (jb + 1) * SEL_BLOCK, lanes],
                                       -jnp.inf) for jb in range(n_blk)], axis=0)
                    m_prev = m_ref[r:r + 1, lanes]
                    m_new = jnp.maximum(m_prev, jnp.max(x, axis=0, keepdims=True))
                    m_ref[r:r + 1, lanes] = m_new
                    alpha_ref[r:r + 1, lanes] = jnp.exp2(m_prev - m_new)
                    p_ref[r * kt:(r + 1) * kt, lanes] = jnp.exp2(x - m_new).astype(BF16)
            for r in range(NSA_REP):
                pv = jnp.dot(vt_ref[...], p_ref[r * kt:(r + 1) * kt, cols],
                             preferred_element_type=F32)
                hd = slice(r * dh, (r + 1) * dh)
                alpha = alpha_ref[r:r + 1, cols]
                acc_ref[hd, cols] = acc_ref[hd, cols] * alpha + pv[:dh]
                l_ref[r:r + 1, cols] = l_ref[r:r + 1, cols] * alpha + pv[dh:dh + 1]

    def before_diagonal(ki, carry):
        key_tile(ki, False)
        return carry

    diag = qi * (tq // kt)
    lax.fori_loop(0, diag, before_diagonal, 0)
    for d in range(tq // kt):
        def on_diagonal(ki, carry, first_sub=d * kt // FLASH_SUB):
            key_tile(ki, True, first_sub=first_sub)
            return carry
        lax.fori_loop(diag + d, diag + d + 1, on_diagonal, 0)

    for ch in range(tq // LANES):
        lanes = slice(ch * LANES, (ch + 1) * LANES)
        out_t = jnp.concatenate(
            [acc_ref[r * dh:(r + 1) * dh, lanes] / l_ref[r:r + 1, lanes] for r in range(NSA_REP)], axis=0)
        o_ref[lanes, :] = out_t.T


def flash_branch(q_t, k4, v_t, sel_t, tq, kt):
    qw, t = q_t.shape
    g, nk = k4.shape[:2]
    dh = v_t.shape[2]
    nsel = sel_t.shape[1]
    assert t % tq == 0 and tq % kt == 0 and kt % SEL_BLOCK == 0 and tq % FLASH_SUB == 0 and kt % FLASH_SUB == 0
    return pl.pallas_call(
        functools.partial(_flash_kernel, tq=tq, kt=kt),
        grid=(g, t // tq),
        in_specs=[
            pl.BlockSpec((NSA_GW, tq), lambda gi, i: (gi, i)),
            pl.BlockSpec((None, nk, kt, NSA_GW), lambda gi, i: (gi, 0, 0, 0), pipeline_mode=pl.Buffered(1)),
            pl.BlockSpec((None, nk, dh, kt), lambda gi, i: (gi, 0, 0, 0), pipeline_mode=pl.Buffered(1)),
            pl.BlockSpec((None, nsel, tq), lambda gi, i: (gi, 0, i)),
        ],
        out_specs=pl.BlockSpec((tq, NSA_GW), lambda gi, i: (i, gi)),
        scratch_shapes=[
            pltpu.VMEM((8, tq), F32),
            pltpu.VMEM((8, tq), F32),
            pltpu.VMEM((8, tq), F32),
            pltpu.VMEM((NSA_GW, tq), F32),
            pltpu.VMEM((NSA_REP * kt, tq), F32),
            pltpu.VMEM((NSA_REP * kt, tq), BF16),
            pltpu.VMEM((kt, tq), F32),
            pltpu.VMEM((NSA_REP * kt, NSA_GW), BF16),
            pltpu.VMEM((dh + FLASH_SUM_ROWS, kt), BF16),
            pltpu.VMEM((NSA_GW, tq), BF16),
        ],
        out_shape=jax.ShapeDtypeStruct((t, qw), F32),
        compiler_params=_cparams(("parallel", "arbitrary")),
        name="nsa_selected",
    )(q_t, k4, v_t, sel_t)


def _window_kernel(q_ref, k_ref, v_ref, o_ref, s_ref, p_ref, bias_ref, kbd_ref, vt_ref, *, tq, kt):
    i = pl.program_id(1)
    dh = NSA_HEAD_DIM
    nw = WINDOW // kt + 1
    span = nw * kt
    qs = (q_ref[...] * ((dh ** -0.5) * LOG2_E)).astype(BF16)
    for w in range(nw):
        tile = i - (nw - 1) + w
        k4 = k_ref[jnp.maximum(tile, 0)]
        kseg = _div_pow2(lax.broadcasted_iota(jnp.int32, k4.shape, 1), dh)
        for r in range(NSA_REP):
            at = r * span + w * kt
            kbd_ref[at:at + kt, :] = jnp.where(kseg == r, k4, jnp.zeros_like(k4))
        vt_ref[:dh, w * kt:(w + 1) * kt] = v_ref[jnp.maximum(tile, 0)]
        tpos = i * tq + lax.broadcasted_iota(jnp.int32, (kt, tq), 1)
        kpos = tile * kt + lax.broadcasted_iota(jnp.int32, (kt, tq), 0)
        allowed = (kpos >= 0) & (kpos <= tpos) & (kpos > tpos - WINDOW)
        bias_ref[w * kt:(w + 1) * kt, :] = jnp.where(allowed, 0.0, -jnp.inf)
    vt_ref[dh:, :] = jnp.ones((FLASH_SUM_ROWS, span), BF16)

    s_ref[...] = jnp.dot(kbd_ref[...], qs, preferred_element_type=F32)
    for ch in range(tq // LANES):
        lanes = slice(ch * LANES, (ch + 1) * LANES)
        for r in range(NSA_REP):
            m = jnp.full((1, LANES), -jnp.inf, F32)
            for w in range(nw):
                rows = slice(r * span + w * kt, r * span + (w + 1) * kt)
                m = jnp.maximum(m, jnp.max(s_ref[rows, lanes] + bias_ref[w * kt:(w + 1) * kt, lanes],
                                           axis=0, keepdims=True))
            m = jnp.where(m > -jnp.inf, m, 0.0)
            for w in range(nw):
                rows = slice(r * span + w * kt, r * span + (w + 1) * kt)
                x = s_ref[rows, lanes] + bias_ref[w * kt:(w + 1) * kt, lanes]
                p_ref[rows, lanes] = jnp.exp2(x - m).astype(BF16)
    pv = [jnp.dot(vt_ref[...], p_ref[r * span:(r + 1) * span, :], preferred_element_type=F32)
          for r in range(NSA_REP)]
    for ch in range(tq // LANES):
        lanes = slice(ch * LANES, (ch + 1) * LANES)
        out_t = jnp.concatenate(
            [pv[r][:dh, lanes] / jnp.maximum(pv[r][dh:dh + 1, lanes], 1e-30) for r in range(NSA_REP)], axis=0)
        o_ref[lanes, :] = out_t.T


def window_branch(q_t, k4, v_t, tq):
    qw, t = q_t.shape
    g, nk, kt, _ = k4.shape
    dh = v_t.shape[2]
    assert tq == kt and WINDOW % kt == 0 and t % tq == 0
    span = (WINDOW // kt + 1) * kt
    return pl.pallas_call(
        functools.partial(_window_kernel, tq=tq, kt=kt),
        grid=(g, t // tq),
        in_specs=[
            pl.BlockSpec((NSA_GW, tq), lambda gi, i: (gi, i)),
            pl.BlockSpec((None, nk, kt, NSA_GW), lambda gi, i: (gi, 0, 0, 0)),
            pl.BlockSpec((None, nk, dh, kt), lambda gi, i: (gi, 0, 0, 0)),
        ],
        out_specs=pl.BlockSpec((tq, NSA_GW), lambda gi, i: (i, gi)),
        scratch_shapes=[
            pltpu.VMEM((NSA_REP * span, tq), F32),
            pltpu.VMEM((NSA_REP * span, tq), BF16),
            pltpu.VMEM((span, tq), F32),
            pltpu.VMEM((NSA_REP * span, NSA_GW), BF16),
            pltpu.VMEM((dh + FLASH_SUM_ROWS, span), BF16),
        ],
        out_shape=jax.ShapeDtypeStruct((t, qw), F32),
        compiler_params=_cparams(("parallel", "parallel")),
        name="nsa_window",
    )(q_t, k4, v_t)


def _nsa_out_kernel(oc_ref, os_ref, ow_ref, gl_ref, eg_ref, w_ref, gpost_ref, h_ref, out_ref):
    pieces = _split3(_sigmoid(gl_ref[...]))
    mixed = jnp.zeros(oc_ref.shape, F32)
    for b, br_ref in enumerate((oc_ref, os_ref, ow_ref)):
        gfull = jnp.zeros(oc_ref.shape, F32)
        for piece in pieces:
            gfull = gfull + jnp.dot(piece, eg_ref[b], preferred_element_type=F32)
        mixed = mixed + gfull * br_ref[...]
    mix = jnp.dot(mixed.astype(BF16), w_ref[...], preferred_element_type=F32)
    out_ref[...] = h_ref[...] + _rms(mix, gpost_ref[...])


def nsa_out(oc, osel, ow, gate_logits, expand, w_o, gpost, h, tm=512):
    t, qw = oc.shape
    d = h.shape[1]
    assert t % tm == 0
    row = lambda w: pl.BlockSpec((tm, w), lambda i: (i, 0))
    return pl.pallas_call(
        _nsa_out_kernel,
        grid=(t // tm,),
        in_specs=[
            row(qw), row(qw), row(qw), row(LANES),
            pl.BlockSpec((3, LANES, qw), lambda i: (0, 0, 0)),
            pl.BlockSpec((qw, d), lambda i: (0, 0)),
            pl.BlockSpec((1, d), lambda i: (0, 0)),
            row(d),
        ],
        out_specs=row(d),
        out_shape=jax.ShapeDtypeStruct((t, d), F32),
        compiler_params=_cparams(("parallel",)),
        name="nsa_out",
    )(oc, osel, ow, gate_logits, expand, w_o, gpost.reshape(1, d), h)


def _pad_cols(w, n):
    return jnp.pad(w, ((0, 0), (0, n - w.shape[1])))


def _importance_matrix(nc, nsel):
    r = SEL_BLOCK // CMP_STRIDE
    c = CMP_BLOCK // CMP_STRIDE
    mat = np.zeros((nc, nsel), np.float32)
    for kblk in range(nsel):
        for m in range(r):
            for n in range(c):
                j = r * kblk + m - n
                if 0 <= j < nc - 1:
                    mat[j, kblk] += 1.0
    return jnp.asarray(mat, BF16)


def _gate_expand():
    e = np.zeros((3, LANES, NSA_GROUPS * NSA_GW), np.float32)
    for head in range(NSA_GROUPS * NSA_REP):
        for b in range(3):
            e[b, head * 3 + b, head * NSA_HEAD_DIM:(head + 1) * NSA_HEAD_DIM] = 1.0
    return jnp.asarray(e, BF16)


def _block_diag_kv(k_cmp, v_cmp):
    g, nc, dh = k_cmp.shape
    eye = jnp.eye(NSA_REP, dtype=bool)
    kct = jnp.swapaxes(k_cmp, 1, 2)
    kbd = jnp.where(eye[None, :, None, :, None], kct[:, None, :, None, :], 0.0)
    vbd = jnp.where(eye[None, :, None, :, None], v_cmp[:, None, :, None, :], 0.0)
    return (kbd.reshape(g, NSA_REP * dh, NSA_REP * nc).astype(BF16),
            vbd.reshape(g, NSA_REP * nc, NSA_REP * dh).astype(BF16))


def kernel(x, p, mix_pre_norm, mix_post_norm, ffn_pre_norm, ffn_post_norm, gdn_w_in, gdn_conv_w, gdn_a_log,
           gdn_dt_bias, gdn_o_norm, gdn_w_out, kv_norm, kv_w, cmp_pos, cmp_w1, cmp_w2, nsa_w_qg, nsa_w_o,
           ffn_w_in, ffn_w_out, ple_w_in, ple_w_gate):
    depth = p.shape[0]
    n_a = gdn_w_in.shape[0]
    t = x.shape[1]
    h = x[0]
    fh = ffn_w_out.shape[1]
    conv_w_cols = gdn_conv_w.shape[2]
    vw = GDN_V_HEADS * GDN_HEAD_DIM
    main_w = conv_w_cols + vw

    def channel_and_ple(h, i):
        return ffn_ple(h, ffn_pre_norm[i], ffn_w_in[i].astype(BF16), ffn_w_out[i].astype(BF16),
                       ffn_post_norm[i], p[i, 0], ple_w_in[i].astype(BF16), ple_w_gate[i].astype(BF16))

    for i in range(n_a):
        w_in = gdn_w_in[i]
        w_beta = w_in[:, main_w:main_w + GDN_V_HEADS]
        w_a = w_in[:, main_w + GDN_V_HEADS:]
        w_small = _pad_cols(jnp.concatenate([w_beta, w_a, w_a], axis=1), LANES).astype(BF16)
        proj = norm_matmul(h, mix_pre_norm[i], w_in.astype(BF16), n=main_w)
        ba = norm_matmul(h, mix_pre_norm[i], w_small)
        pad_vec = lambda v: jnp.pad(v, (GDN_V_HEADS, LANES - 2 * GDN_V_HEADS))
        alog_vec = (pad_vec(gdn_a_log[i]) + jnp.pad(gdn_a_log[i], (2 * GDN_V_HEADS, LANES - 3 * GDN_V_HEADS)))
        dtb_vec = (pad_vec(gdn_dt_bias[i]) + jnp.pad(gdn_dt_bias[i], (2 * GDN_V_HEADS, LANES - 3 * GDN_V_HEADS)))
        slab = gdn_gates(ba, alog_vec.reshape(1, LANES), dtb_vec.reshape(1, LANES))
        qkv_hm = gdn_conv(proj, gdn_conv_w[i])
        kt_hm = jnp.swapaxes(qkv_hm[GDN_QK_HEADS:2 * GDN_QK_HEADS], 1, 2)
        o = gdn_scan(qkv_hm, kt_hm, slab, slab.T)
        onorm_tiled = jnp.tile(gdn_o_norm[i], GDN_V_HEADS).reshape(1, vw)
        h = gdn_out(o, proj, onorm_tiled, gdn_w_out[i].astype(BF16), mix_post_norm[i], h)
        h = channel_and_ple(h, i)

    g = NSA_GROUPS
    dh = NSA_HEAD_DIM
    kv = norm_matmul(h, kv_norm, kv_w.astype(BF16), tn=768)
    kv6 = jnp.transpose(kv.reshape(t, 6, g, dh), (1, 2, 0, 3))
    nc = t // CMP_STRIDE
    nsel = t // SEL_BLOCK
    x2 = kv6[0:2].reshape(2, g, nc, CMP_STRIDE * dh)
    pos2 = cmp_pos.reshape(2, 2, CMP_STRIDE * dh)
    cmp_out = compress(x2, pos2, cmp_w1.astype(BF16), cmp_w2.astype(BF16))
    imp_mat_t = _importance_matrix(nc, nsel).T
    nq_r = t // (CMP_TQ * CMP_RANGES)
    assert t % (CMP_TQ * CMP_RANGES) == 0 and nc % CMP_RANGES == 0 and nsel % (8 * CMP_RANGES) == 0
    cmp_ranges = []
    for rg in range(CMP_RANGES):
        nc_r, nsel_r = (rg + 1) * nc // CMP_RANGES, (rg + 1) * nsel // CMP_RANGES
        _, kbd = _block_diag_kv(cmp_out[1][:, :nc_r], cmp_out[0][:, :nc_r])
        v_cmp_t = jnp.swapaxes(cmp_out[1][:, :nc_r], 1, 2).astype(BF16)
        cmp_ranges.append((kbd, v_cmp_t, imp_mat_t[:nsel_r, :nc_r]))
    kt = FLASH_KT
    tiles = lambda a: a.reshape(g, t // kt, kt, dh)
    rep_k = lambda a: jnp.tile(tiles(a), (1, 1, 1, NSA_REP)).astype(BF16)
    rep_t = lambda a: jnp.swapaxes(tiles(a), 2, 3).astype(BF16)
    k_slc, v_slc_t = rep_k(kv6[2]), rep_t(kv6[3])
    k_win, v_win_t = rep_k(kv6[4]), rep_t(kv6[5])
    expand = _gate_expand()

    for i in range(n_a, depth):
        j = i - n_a
        qw = g * NSA_GW
        w_qg = nsa_w_qg[j]
        q_t = norm_matmul_t(h, mix_pre_norm[i], w_qg[:, :qw].T.astype(BF16))
        gate_logits = norm_matmul(h, mix_pre_norm[i], _pad_cols(w_qg[:, qw:], LANES).astype(BF16))
        cmp_res = (jnp.zeros((t, qw), F32), jnp.zeros((g, nsel, t), F32))
        for rg, (kbd, vbd_t, imp_r) in enumerate(cmp_ranges):
            cmp_res = cmp_topk(q_t, kbd, vbd_t, imp_r, nsel, min(SEL_TOPK, nsel), rg * nq_r, nq_r, CMP_TQ,
                               prev=cmp_res)
        o_c, sel_t = cmp_res
        o_s = flash_branch(q_t, k_slc, v_slc_t, sel_t, tq=min(FLASH_TQ, t), kt=kt)
        o_w = window_branch(q_t, k_win, v_win_t, tq=kt)
        h = nsa_out(o_c, o_s, o_w, gate_logits, expand, nsa_w_o[j].astype(BF16), mix_post_norm[i], h)
        h = channel_and_ple(h, i)
    return h[None]
```

```python
import functools

import numpy as np
import jax
import jax.numpy as jnp
from jax import lax
from jax.experimental import pallas as pl
from jax.experimental.pallas import tpu as pltpu

F32 = jnp.float32
BF16 = jnp.bfloat16

NORM_EPS = 1e-6
L2_EPS = 1e-6
GDN_QK_HEADS = 8
GDN_V_HEADS = 16
GDN_HEAD_DIM = 128
GDN_CONV = 4
GDN_CHUNK = 64
GDN_GROUP = 256
GDN_SCAN_HEADS = 8
GDN_CONV_HEADS = 4
NSA_GROUPS = 4
NSA_REP = 4
NSA_HEAD_DIM = 64
NSA_GW = NSA_REP * NSA_HEAD_DIM
CMP_BLOCK = 32
CMP_STRIDE = 16
SEL_BLOCK = 64
SEL_TOPK = 16
WINDOW = 512
FORCED_SCORE = 1e4
LANES = 128
NEG_INIT = -(2.0 ** 100)
LOG2_E = 1.4426950408889634
FLASH_TQ = 2048
FLASH_KT = 256
WINDOW_TQ = 512
FLASH_SUB = 256
FLASH_SUM_ROWS = 16
CMP_ROWS = 256
CMP_TQ = 512
CMP_SUB = 256
CMP_RANGES = 4

VMEM_LIMIT = 56 * 1024 * 1024
FFN_VMEM_LIMIT = 60 * 1024 * 1024


def _cparams(sem, vmem_limit=VMEM_LIMIT):
    return pltpu.CompilerParams(dimension_semantics=sem, vmem_limit_bytes=vmem_limit)


def _rms(x, gain):
    return x * lax.rsqrt(jnp.mean(x * x, axis=-1, keepdims=True) + NORM_EPS) * gain


def _mm(a, b):
    return jnp.dot(a.astype(BF16), b.astype(BF16), preferred_element_type=F32)


def _sigmoid(x):
    return 1.0 / (1.0 + jnp.exp(-x))


def _div_pow2(x, d):
    shift = d.bit_length() - 1
    assert d == 1 << shift
    return jnp.right_shift(x, shift)


def _split3(x):
    a = x.astype(BF16)
    r = x - a.astype(F32)
    b = r.astype(BF16)
    c = (r - b.astype(F32)).astype(BF16)
    return a, b, c


def _norm_matmul_kernel(x_ref, g_ref, w_ref, o_ref, xn_ref):
    @pl.when(pl.program_id(1) == 0)
    def _():
        xn_ref[...] = _rms(x_ref[...], g_ref[...]).astype(BF16)

    o_ref[...] = jnp.dot(xn_ref[...], w_ref[...], preferred_element_type=F32)


def _norm_matmul_t_kernel(x_ref, g_ref, wt_ref, o_ref, xn_ref):
    @pl.when(pl.program_id(1) == 0)
    def _():
        xn_ref[...] = _rms(x_ref[...], g_ref[...]).astype(BF16)

    o_ref[...] = lax.dot_general(wt_ref[...], xn_ref[...], (((1,), (1,)), ((), ())), preferred_element_type=F32)


def norm_matmul_t(h, gain, w_t, tm=1024, tn=1024):
    t, d = h.shape
    n = w_t.shape[0]
    tn = min(tn, n)
    assert t % tm == 0 and n % tn == 0
    return pl.pallas_call(
        _norm_matmul_t_kernel,
        grid=(t // tm, n // tn),
        in_specs=[
            pl.BlockSpec((tm, d), lambda i, j: (i, 0)),
            pl.BlockSpec((1, d), lambda i, j: (0, 0)),
            pl.BlockSpec((tn, d), lambda i, j: (j, 0)),
        ],
        out_specs=pl.BlockSpec((tn, tm), lambda i, j: (j, i)),
        out_shape=jax.ShapeDtypeStruct((n, t), F32),
        scratch_shapes=[pltpu.VMEM((tm, d), BF16)],
        compiler_params=_cparams(("parallel", "arbitrary")),
        name="norm_matmul_t",
    )(h, gain.reshape(1, d), w_t)


def norm_matmul(h, gain, w, tm=1024, tn=1024, n=None):
    t, d = h.shape
    n = w.shape[1] if n is None else n
    tn = min(tn, n)
    assert t % tm == 0 and n % tn == 0
    return pl.pallas_call(
        _norm_matmul_kernel,
        grid=(t // tm, n // tn),
        in_specs=[
            pl.BlockSpec((tm, d), lambda i, j: (i, 0)),
            pl.BlockSpec((1, d), lambda i, j: (0, 0)),
            pl.BlockSpec((d, tn), lambda i, j: (0, j)),
        ],
        out_specs=pl.BlockSpec((tm, tn), lambda i, j: (i, j)),
        out_shape=jax.ShapeDtypeStruct((t, n), F32),
        scratch_shapes=[pltpu.VMEM((tm, d), BF16)],
        compiler_params=_cparams(("parallel", "arbitrary")),
        name="norm_matmul",
    )(h, gain.reshape(1, d), w)


def _ffn_ple_kernel(h_ref, gpre_ref, wg_ref, wu_ref, wo_ref, gpost_ref, p_ref, wple_ref, wgt_ref,
                    o_ref, xn_ref, acc_ref, *, nf):
    f = pl.program_id(1)

    @pl.when(f == 0)
    def _():
        xn_ref[...] = _rms(h_ref[...], gpre_ref[...]).astype(BF16)
        acc_ref[...] = jnp.zeros_like(acc_ref)

    xn = xn_ref[...]
    tf = wg_ref.shape[1]
    half = (tf // LANES // 2) * LANES
    for a, b in ((0, half), (half, tf)):
        gate = jnp.dot(xn, wg_ref[:, a:b], preferred_element_type=F32)
        up = jnp.dot(xn, wu_ref[:, a:b], preferred_element_type=F32)
        act = gate * _sigmoid(gate) * up
        acc_ref[...] += jnp.dot(act.astype(BF16), wo_ref[a:b, :], preferred_element_type=F32)

    @pl.when(f == nf - 1)
    def _():
        h2 = h_ref[...] + _rms(acc_ref[...], gpost_ref[...])
        emb = jnp.dot(p_ref[...].astype(BF16), wple_ref[...], preferred_element_type=F32)
        gt = _sigmoid(jnp.dot(h2.astype(BF16), wgt_ref[...], preferred_element_type=F32))
        o_ref[...] = h2 + emb * gt


def ffn_ple(h, gpre, w_in, w_out, gpost, p, w_ple, w_plegate, tm=1024, tf=1408):
    t, d = h.shape
    fh = w_out.shape[0]
    pd = p.shape[1]
    assert t % tm == 0 and fh % tf == 0 and w_in.shape[1] == 2 * fh
    nf = fh // tf
    return pl.pallas_call(
        functools.partial(_ffn_ple_kernel, nf=nf),
        grid=(t // tm, nf),
        in_specs=[
            pl.BlockSpec((tm, d), lambda i, f: (i, 0)),
            pl.BlockSpec((1, d), lambda i, f: (0, 0)),
            pl.BlockSpec((d, tf), lambda i, f: (0, f)),
            pl.BlockSpec((d, tf), lambda i, f: (0, nf + f)),
            pl.BlockSpec((tf, d), lambda i, f: (f, 0)),
            pl.BlockSpec((1, d), lambda i, f: (0, 0)),
            pl.BlockSpec((tm, pd), lambda i, f: (i, 0)),
            pl.BlockSpec((pd, d), lambda i, f: (0, 0), pipeline_mode=pl.Buffered(1)),
            pl.BlockSpec((d, d), lambda i, f: (0, 0), pipeline_mode=pl.Buffered(1)),
        ],
        out_specs=pl.BlockSpec((tm, d), lambda i, f: (i, 0)),
        out_shape=jax.ShapeDtypeStruct((t, d), F32),
        scratch_shapes=[pltpu.VMEM((tm, d), BF16), pltpu.VMEM((tm, d), F32)],
        compiler_params=_cparams(("parallel", "arbitrary"), FFN_VMEM_LIMIT),
        name="ffn_ple",
    )(h, gpre.reshape(1, d), w_in, w_in, w_out, gpost.reshape(1, d), p, w_ple, w_plegate)


def _gdn_conv_kernel(x_ref, halo_ref, w_ref, o_ref, *, tm):
    c = pl.program_id(0)
    i = pl.program_id(1)
    x = x_ref[...]
    halo = jnp.where(i > 0, halo_ref[...], 0.0)
    ext = jnp.concatenate([halo, x], axis=0)
    w = w_ref[...]
    y = x * w[GDN_CONV - 1:GDN_CONV, :]
    for k in range(1, GDN_CONV):
        shifted = pltpu.roll(ext, k, axis=0)[8:8 + tm]
        y = y + shifted * w[GDN_CONV - 1 - k:GDN_CONV - k, :]
    y = y * _sigmoid(y)
    for hd in range(GDN_CONV_HEADS):
        head = c * GDN_CONV_HEADS + hd
        seg = y[:, hd * LANES:(hd + 1) * LANES]
        normed = seg * lax.rsqrt(jnp.sum(seg * seg, axis=-1, keepdims=True) + L2_EPS)
        q_scale = jnp.where(head < GDN_QK_HEADS, GDN_HEAD_DIM ** -0.5, 1.0)
        o_ref[hd] = jnp.where(head < 2 * GDN_QK_HEADS, normed * q_scale, seg)


def gdn_conv(proj, conv_w, tm=1024):
    t = proj.shape[0]
    n_tiles = conv_w.shape[1] // LANES
    cw = GDN_CONV_HEADS
    assert t % tm == 0 and n_tiles % cw == 0
    return pl.pallas_call(
        functools.partial(_gdn_conv_kernel, tm=tm),
        grid=(n_tiles // cw, t // tm),
        in_specs=[
            pl.BlockSpec((tm, cw * LANES), lambda c, i: (i, c)),
            pl.BlockSpec((8, cw * LANES), lambda c, i: (jnp.maximum(i * (tm // 8) - 1, 0), c)),
            pl.BlockSpec((GDN_CONV, cw * LANES), lambda c, i: (0, c)),
        ],
        out_specs=pl.BlockSpec((cw, tm, LANES), lambda c, i: (c, i, 0)),
        out_shape=jax.ShapeDtypeStruct((n_tiles, t, LANES), F32),
        compiler_params=_cparams(("parallel", "parallel")),
        name="gdn_conv",
    )(proj, proj, conv_w)


def _gdn_gate_kernel(x_ref, alog_ref, dtb_ref, lc_ref, lf_ref, o_ref):
    x = x_ref[...]
    lane = lax.broadcasted_iota(jnp.int32, x.shape, 1)
    beta = _sigmoid(x)
    z = x + dtb_ref[...]
    softplus = jnp.maximum(z, 0.0) + jnp.log(1.0 + jnp.exp(-jnp.abs(z)))
    g = -jnp.exp(alog_ref[...]) * softplus
    gcum = jnp.zeros_like(x)
    gtot = jnp.zeros_like(x)
    for piece in _split3(g):
        gcum = gcum + jnp.dot(lc_ref[...], piece, preferred_element_type=F32)
        gtot = gtot + jnp.dot(lf_ref[...], piece, preferred_element_type=F32)
    o_ref[...] = jnp.where(lane < GDN_V_HEADS, beta, jnp.where(lane < 2 * GDN_V_HEADS, gcum, gtot))


def gdn_gates(ba, alog_vec, dtb_vec):
    t = ba.shape[0]
    tm = GDN_GROUP
    r = np.arange(tm)
    same = (r[:, None] // GDN_CHUNK) == (r[None, :] // GDN_CHUNK)
    lc = jnp.asarray(same & (r[None, :] <= r[:, None]), BF16)
    lf = jnp.asarray(same, BF16)
    return pl.pallas_call(
        _gdn_gate_kernel,
        grid=(t // tm,),
        in_specs=[
            pl.BlockSpec((tm, LANES), lambda i: (i, 0)),
            pl.BlockSpec((1, LANES), lambda i: (0, 0)),
            pl.BlockSpec((1, LANES), lambda i: (0, 0)),
            pl.BlockSpec((tm, tm), lambda i: (0, 0)),
            pl.BlockSpec((tm, tm), lambda i: (0, 0)),
        ],
        out_specs=pl.BlockSpec((tm, LANES), lambda i: (i, 0)),
        out_shape=jax.ShapeDtypeStruct((t, LANES), F32),
        compiler_params=_cparams(("parallel",)),
        name="gdn_gates",
    )(ba, alog_vec, dtb_vec, lc, lf)


def _gdn_scan_kernel(q_ref, k_ref, kt_ref, v_ref, slab_ref, slabt_ref, o_ref, s_ref):
    n = pl.program_id(1)

    @pl.when(n == 0)
    def _():
        s_ref[...] = jnp.zeros_like(s_ref)

    L = GDN_GROUP
    C = GDN_CHUNK
    D = GDN_HEAD_DIM
    slab = slab_ref[...]
    slabt = slabt_ref[...]
    row = lax.broadcasted_iota(jnp.int32, (L, L), 0)
    col = lax.broadcasted_iota(jnp.int32, (L, L), 1)
    same = _div_pow2(row, C) == _div_pow2(col, C)
    causal = same & (col <= row)
    strict = same & (col < row)
    eye = (row == col).astype(F32)
    lane = lax.broadcasted_iota(jnp.int32, (L, LANES), 1)
    sub = lax.broadcasted_iota(jnp.int32, (LANES, L), 0)

    def column(idx):
        return jnp.sum(jnp.where(lane == idx, slab, 0.0), axis=1, keepdims=True)

    def rowvec(idx):
        return jnp.sum(jnp.where(sub == idx, slabt, 0.0), axis=0, keepdims=True)

    colk = lax.broadcasted_iota(jnp.int32, (D, L), 1)
    slots = range(2 * GDN_SCAN_HEADS)

    q = [q_ref[hq] for hq in range(GDN_SCAN_HEADS)]
    k = [k_ref[hq] for hq in range(GDN_SCAN_HEADS)]
    kt = [kt_ref[hq] for hq in range(GDN_SCAN_HEADS)]
    kk = [_mm(k[hq], kt[hq]) for hq in range(GDN_SCAN_HEADS)]
    qk = [_mm(q[hq], kt[hq]) for hq in range(GDN_SCAN_HEADS)]
    beta_c, gc_c, gt_c, decay, bp, inv, kdt = [], [], [], [], [], [], []
    for slot in slots:
        hq = slot // 2
        hv = 2 * (pl.program_id(0) * GDN_SCAN_HEADS + hq) + slot % 2
        beta_c.append(column(hv))
        gc_c.append(column(GDN_V_HEADS + hv))
        gt_c.append(column(2 * GDN_V_HEADS + hv))
        gc_r = rowvec(GDN_V_HEADS + hv)
        gt_r = rowvec(2 * GDN_V_HEADS + hv)
        decay.append(jnp.where(causal, jnp.exp(jnp.where(causal, gc_c[slot] - gc_r, 0.0)), 0.0))
        bp.append(jnp.where(strict, -(kk[hq] * beta_c[slot]) * decay[slot], 0.0))
        inv.append(eye + bp[slot])
        kdt.append(kt[hq] * jnp.exp(gt_r - gc_r))
    for _ in range(5):
        bp = [_mm(bp[slot], bp[slot]) for slot in slots]
        inv = [inv[slot] + _mm(inv[slot], bp[slot]) for slot in slots]
    u, w, qkm, q_dec = [], [], [], []
    for slot in slots:
        hq = slot // 2
        egc = jnp.exp(gc_c[slot])
        rhs = jnp.concatenate([v_ref[slot] * beta_c[slot], k[hq] * (beta_c[slot] * egc)], axis=1)
        sol = _mm(inv[slot], rhs)
        u.append(sol[:, :D])
        w.append(sol[:, D:])
        qkm.append(jnp.where(causal, qk[hq] * decay[slot], 0.0))
        q_dec.append(q[hq] * egc)
    state = [s_ref[slot] for slot in slots]
    v_done = [[] for _ in slots]
    for c in range(L // C):
        lo, hi = c * C, (c + 1) * C
        r = [_mm(jnp.concatenate([w[slot][lo:hi], q_dec[slot][lo:hi]], axis=0), state[slot]) for slot in slots]
        for slot in slots:
            v_done[slot].append(u[slot][lo:hi] - r[slot][:C])
            v_all = jnp.concatenate(v_done[slot] + [jnp.zeros((L - hi, D), F32)] * (hi < L), axis=0)
            kdt_c = jnp.where((colk >= lo) & (colk < hi), kdt[slot], 0.0)
            both = _mm(jnp.concatenate([qkm[slot][lo:hi, :], kdt_c], axis=0), v_all)
            o_ref[lo:hi, slot * D:(slot + 1) * D] = r[slot][C:] + both[:C]
            state[slot] = state[slot] * jnp.exp(gt_c[slot][lo:lo + 1, :]) + both[C:]
    for slot in slots:
        s_ref[slot] = state[slot]


def gdn_scan(qkv_hm, kt_hm, slab, slabt):
    t = qkv_hm.shape[1]
    L = GDN_GROUP
    D = GDN_HEAD_DIM
    hq = GDN_SCAN_HEADS
    assert t % L == 0 and GDN_QK_HEADS % hq == 0
    q_blocks = GDN_QK_HEADS // hq
    return pl.pallas_call(
        _gdn_scan_kernel,
        grid=(q_blocks, t // L),
        in_specs=[
            pl.BlockSpec((hq, L, D), lambda j, n: (j, n, 0)),
            pl.BlockSpec((hq, L, D), lambda j, n: (q_blocks + j, n, 0)),
            pl.BlockSpec((hq, D, L), lambda j, n: (j, 0, n)),
            pl.BlockSpec((2 * hq, L, D), lambda j, n: (q_blocks + j, n, 0)),
            pl.BlockSpec((L, LANES), lambda j, n: (n, 0)),
            pl.BlockSpec((LANES, L), lambda j, n: (0, n)),
        ],
        out_specs=pl.BlockSpec((L, 2 * hq * D), lambda j, n: (n, j)),
        out_shape=jax.ShapeDtypeStruct((t, GDN_V_HEADS * D), F32),
        scratch_shapes=[pltpu.VMEM((2 * hq, D, D), F32)],
        compiler_params=_cparams(("parallel", "arbitrary")),
        name="gdn_scan",
    )(qkv_hm, qkv_hm, kt_hm, qkv_hm, slab, slabt)


def _gdn_out_kernel(o_ref, z_ref, onorm_ref, w_ref, gpost_ref, h_ref, out_ref):
    o = o_ref[...]
    z = z_ref[...]
    parts = []
    for hd in range(GDN_V_HEADS):
        seg = o[:, hd * GDN_HEAD_DIM:(hd + 1) * GDN_HEAD_DIM]
        parts.append(seg * lax.rsqrt(jnp.mean(seg * seg, axis=-1, keepdims=True) + NORM_EPS))
    gated = jnp.concatenate(parts, axis=1) * onorm_ref[...] * (z * _sigmoid(z))
    mix = jnp.dot(gated.astype(BF16), w_ref[...], preferred_element_type=F32)
    out_ref[...] = h_ref[...] + _rms(mix, gpost_ref[...])


def gdn_out(o, proj, onorm_tiled, w_out, gpost, h, tm=512):
    t, vw = o.shape
    d = h.shape[1]
    z_blk = (proj.shape[1] - vw) // vw
    assert proj.shape[1] % vw == 0 and t % tm == 0
    return pl.pallas_call(
        _gdn_out_kernel,
        grid=(t // tm,),
        in_specs=[
            pl.BlockSpec((tm, vw), lambda i: (i, 0)),
            pl.BlockSpec((tm, vw), lambda i: (i, z_blk)),
            pl.BlockSpec((1, vw), lambda i: (0, 0)),
            pl.BlockSpec((vw, d), lambda i: (0, 0)),
            pl.BlockSpec((1, d), lambda i: (0, 0)),
            pl.BlockSpec((tm, d), lambda i: (i, 0)),
        ],
        out_specs=pl.BlockSpec((tm, d), lambda i: (i, 0)),
        out_shape=jax.ShapeDtypeStruct((t, d), F32),
        compiler_params=_cparams(("parallel",)),
        name="gdn_out",
    )(o, proj, onorm_tiled, w_out, gpost.reshape(1, d), h)


def _compress_kernel(x_ref, pos_ref, w1_ref, w2_ref, o_ref):
    x = x_ref[...]
    pos = pos_ref[...]
    nc, half = x.shape
    w1 = w1_ref[...]
    first = _mm(x + pos[0:1, :], w1[:half])
    second = _mm(x + pos[1:2, :], w1[half:])
    hid = first + pltpu.roll(second, nc - 1, axis=0)
    hid = hid * _sigmoid(hid)
    out = jnp.dot(hid.astype(BF16), w2_ref[...], preferred_element_type=F32)
    rowi = lax.broadcasted_iota(jnp.int32, out.shape, 0)
    o_ref[...] = jnp.where(rowi < nc - 1, out, 0.0)


def compress(x2, pos2, w1, w2):
    _, g, nc, wdt = x2.shape
    hid = w1.shape[2]
    dh = w2.shape[2]
    return pl.pallas_call(
        _compress_kernel,
        grid=(2, g),
        in_specs=[
            pl.BlockSpec((None, None, nc, wdt), lambda b, gi: (b, gi, 0, 0)),
            pl.BlockSpec((None, 2, wdt), lambda b, gi: (b, 0, 0)),
            pl.BlockSpec((None, 2 * wdt, hid), lambda b, gi: (b, 0, 0)),
            pl.BlockSpec((None, hid, dh), lambda b, gi: (b, 0, 0)),
        ],
        out_specs=pl.BlockSpec((None, None, nc, dh), lambda b, gi: (b, gi, 0, 0)),
        out_shape=jax.ShapeDtypeStruct((2, g, nc, dh), F32),
        compiler_params=_cparams(("parallel", "parallel")),
        name="nsa_compress",
    )(x2, pos2, w1, w2)


def _cmp_topk_kernel(q_ref, kbd_ref, vt_ref, m_ref, *refs, tq, nc, nsel, topk, q0):
    oc_ref, sel_ref, s_ref, p_ref, psum_ref = refs[-5:]
    i = q0 + pl.program_id(1)
    dh = NSA_HEAD_DIM
    rows_per = min(nc, CMP_ROWS)
    q = (q_ref[...] * ((dh ** -0.5) * LOG2_E)).astype(BF16)
    subs = [slice(c0, c0 + CMP_SUB) for c0 in range(0, tq, CMP_SUB)]

    def scores(cols):
        s_ref[:, cols] = jnp.dot(kbd_ref[...], q[:, cols], preferred_element_type=F32)

    def softmax(cols):
        for c0 in range(cols.start, cols.stop, LANES):
            lanes = slice(c0, c0 + LANES)
            tpos = i * tq + c0 + lax.broadcasted_iota(jnp.int32, (rows_per, LANES), 1)
            cblk0 = lax.broadcasted_iota(jnp.int32, (rows_per, LANES), 0)
            masks = [(CMP_STRIDE * (cblk0 + c * rows_per) + CMP_BLOCK - 1) <= tpos for c in range(nc // rows_per)]
            for r in range(NSA_REP):
                pieces = [slice(r * nc + c * rows_per, r * nc + (c + 1) * rows_per) for c in range(nc // rows_per)]
                m = jnp.full((1, LANES), -jnp.inf, F32)
                for rows, mask in zip(pieces, masks):
                    m = jnp.maximum(m, jnp.max(jnp.where(mask, s_ref[rows, lanes], -jnp.inf), axis=0, keepdims=True))
                m = jnp.where(m > -jnp.inf, m, 0.0)
                total = jnp.zeros((1, LANES), F32)
                for rows, mask in zip(pieces, masks):
                    e = jnp.exp2(jnp.where(mask, s_ref[rows, lanes], -jnp.inf) - m)
                    s_ref[rows, lanes] = e
                    total = total + jnp.sum(e, axis=0, keepdims=True)
                inv = 1.0 / jnp.maximum(total, 1e-30)
                for c, rows in enumerate(pieces):
                    p = s_ref[rows, lanes] * inv
                    p_ref[rows, lanes] = p.astype(BF16)
                    prow = slice(c * rows_per, (c + 1) * rows_per)
                    psum_ref[prow, lanes] = p if r == 0 else psum_ref[prow, lanes] + p

    def outputs(cols):
        oc_t = jnp.dot(vt_ref[...], p_ref[:, cols], preferred_element_type=F32)
        imp = jnp.zeros((nsel, CMP_SUB), F32)
        for piece in _split3(psum_ref[:, cols]):
            imp = imp + jnp.dot(m_ref[...], piece, preferred_element_type=F32)
        return oc_t, imp

    def select(cols, oc_t, imp):
        for c0 in range(0, CMP_SUB, LANES):
            local = slice(c0, c0 + LANES)
            lanes = slice(cols.start + c0, cols.start + c0 + LANES)
            oc_ref[lanes, :] = oc_t[:, local].T
            t1 = i * tq + cols.start + c0 + lax.broadcasted_iota(jnp.int32, (nsel, LANES), 1)
            blk = lax.broadcasted_iota(jnp.int32, (nsel, LANES), 0)
            cur = _div_pow2(t1, SEL_BLOCK)
            forced = (blk == 0) | (blk == cur) | (blk == cur - 1)
            valid = blk * SEL_BLOCK <= t1
            score = jnp.where(valid, jnp.where(forced, FORCED_SCORE, imp[:, local]), -jnp.inf)
            blkf = blk.astype(F32)
            work = score
            for _ in range(topk):
                mx = jnp.max(work, axis=0, keepdims=True)
                first = jnp.min(jnp.where(work == mx, blkf, float(nsel)), axis=0, keepdims=True)
                work = jnp.where(blkf == first, -jnp.inf, work)
            sel_ref[:nsel, lanes] = jnp.where((score > -jnp.inf) & (work == -jnp.inf), 1.0, 0.0).astype(sel_ref.dtype)
            if sel_ref.shape[0] > nsel:
                sel_ref[nsel:, lanes] = jnp.zeros((sel_ref.shape[0] - nsel, LANES), sel_ref.dtype)

    for cols in subs:
        scores(cols)
    results = []
    for cols in subs:
        softmax(cols)
        results.append(outputs(cols))
    for cols, (oc_t, imp) in zip(subs, results):
        select(cols, oc_t, imp)


def cmp_topk(q_t, kbd, v_t, imp_mat_t, nsel_all, topk, q0, nq, tq, prev):
    qw, t = q_t.shape
    g = kbd.shape[0]
    nc = kbd.shape[1] // NSA_REP
    nsel = imp_mat_t.shape[0]
    assert nc % min(nc, CMP_ROWS) == 0
    in_specs = [
        pl.BlockSpec((NSA_GW, tq), lambda gi, i: (gi, q0 + i)),
        pl.BlockSpec((None, NSA_REP * nc, NSA_GW), lambda gi, i: (gi, 0, 0)),
        pl.BlockSpec((None, NSA_GW, NSA_REP * nc), lambda gi, i: (gi, 0, 0)),
        pl.BlockSpec((nsel, nc), lambda gi, i: (0, 0)),
    ]
    in_specs += [pl.BlockSpec(memory_space=pl.ANY)] * 2
    args = [q_t, kbd, v_t, imp_mat_t, *prev]
    aliases = {len(args) - 2: 0, len(args) - 1: 1}
    return pl.pallas_call(
        functools.partial(_cmp_topk_kernel, tq=tq, nc=nc, nsel=nsel, topk=topk, q0=q0),
        grid=(g, nq),
        in_specs=in_specs,
        out_specs=[
            pl.BlockSpec((tq, NSA_GW), lambda gi, i: (q0 + i, gi)),
            pl.BlockSpec((None, nsel_all, tq), lambda gi, i: (gi, 0, q0 + i)),
        ],
        out_shape=[
            jax.ShapeDtypeStruct((t, qw), F32),
            jax.ShapeDtypeStruct((g, nsel_all, t), F32),
        ],
        input_output_aliases=aliases,
        scratch_shapes=[
            pltpu.VMEM((NSA_REP * nc, tq), F32),
            pltpu.VMEM((NSA_REP * nc, tq), BF16),
            pltpu.VMEM((nc, tq), F32),
        ],
        compiler_params=_cparams(("parallel", "parallel")),
        name="nsa_cmp_topk",
    )(*args)


def _flash_kernel(q_ref, k_ref, v_ref, sel_ref, o_ref,
                  m_ref, l_ref, alpha_ref, acc_ref, s_ref, p_ref, bias_ref, kbd_ref, vt_ref, qs_ref, *, tq, kt):
    qi = pl.program_id(1)
    dh = NSA_HEAD_DIM
    n_sub = tq // FLASH_SUB
    n_blk = kt // SEL_BLOCK

    m_ref[...] = jnp.full_like(m_ref, NEG_INIT)
    l_ref[...] = jnp.zeros_like(l_ref)
    acc_ref[...] = jnp.zeros_like(acc_ref)
    qs_ref[...] = (q_ref[...] * ((dh ** -0.5) * LOG2_E)).astype(BF16)
    vt_ref[dh:, :] = jnp.ones((FLASH_SUM_ROWS, kt), BF16)

    def key_tile(ki, positional, first_sub=0):
        k4 = k_ref[ki]
        kseg = _div_pow2(lax.broadcasted_iota(jnp.int32, k4.shape, 1), dh)
        for r in range(NSA_REP):
            kbd_ref[r * kt:(r + 1) * kt, :] = jnp.where(kseg == r, k4, jnp.zeros_like(k4))
        vt_ref[:dh, :] = v_ref[ki]

        def scores(sub):
            c0 = sub * FLASH_SUB
            cols = slice(c0, c0 + FLASH_SUB)
            s_ref[:, cols] = jnp.dot(kbd_ref[...], qs_ref[:, cols],
                                     preferred_element_type=F32)
            if not positional:
                return
            for jb in range(n_blk):
                rows = slice(jb * SEL_BLOCK, (jb + 1) * SEL_BLOCK)
                picked = sel_ref[pl.ds(ki * n_blk + jb, 1), cols] > 0.5
                tpos = qi * tq + c0 + lax.broadcasted_iota(jnp.int32, (SEL_BLOCK, FLASH_SUB), 1)
                kpos = ki * kt + jb * SEL_BLOCK + lax.broadcasted_iota(jnp.int32, (SEL_BLOCK, FLASH_SUB), 0)
                bias_ref[rows, cols] = jnp.where(picked & (kpos <= tpos), 0.0, -jnp.inf)

        scores(first_sub)
        for sub in range(first_sub, n_sub):
            c0 = sub * FLASH_SUB
            cols = slice(c0, c0 + FLASH_SUB)
            if sub + 1 < n_sub:
                scores(sub + 1)
            if not positional:
                picked_sub = [sel_ref[pl.ds(ki * n_blk + jb, 1), cols] > 0.5 for jb in range(n_blk)]
            for ch in range(FLASH_SUB // LANES):
                lanes = slice(c0 + ch * LANES, c0 + (ch + 1) * LANES)
                if positional:
                    bias = bias_ref[:, lanes]
                else:
                    picked = [pk[:, ch * LANES:(ch + 1) * LANES] for pk in picked_sub]
                for r in range(NSA_REP):
                    if positional:
                        x = s_ref[r * kt:(r + 1) * kt, lanes] + bias
                    else:
                        x = jnp.concatenate(
                            [jnp.where(picked[jb], s_ref[r * kt + jb * SEL_BLOCK:r * kt + (jb + 1) * SEL_BLOCK, lanes],
                                       -jnp.inf) for jb in range(n_blk)], axis=0)
                    m_prev = m_ref[r:r + 1, lanes]
                    m_new = jnp.maximum(m_prev, jnp.max(x, axis=0, keepdims=True))
                    m_ref[r:r + 1, lanes] = m_new
                    alpha_ref[r:r + 1, lanes] = jnp.exp2(m_prev - m_new)
                    p_ref[r * kt:(r + 1) * kt, lanes] = jnp.exp2(x - m_new).astype(BF16)
            for r in range(NSA_REP):
                pv = jnp.dot(vt_ref[...], p_ref[r * kt:(r + 1) * kt, cols],
                             preferred_element_type=F32)
                hd = slice(r * dh, (r + 1) * dh)
                alpha = alpha_ref[r:r + 1, cols]
                acc_ref[hd, cols] = acc_ref[hd, cols] * alpha + pv[:dh]
                l_ref[r:r + 1, cols] = l_ref[r:r + 1, cols] * alpha + pv[dh:dh + 1]

    def before_diagonal(ki, carry):
        key_tile(ki, False)
        return carry

    diag = qi * (tq // kt)
    lax.fori_loop(0, diag, before_diagonal, 0)
    for d in range(tq // kt):
        def on_diagonal(ki, carry, first_sub=d * kt // FLASH_SUB):
            key_tile(ki, True, first_sub=first_sub)
            return carry
        lax.fori_loop(diag + d, diag + d + 1, on_diagonal, 0)

    for ch in range(tq // LANES):
        lanes = slice(ch * LANES, (ch + 1) * LANES)
        out_t = jnp.concatenate(
            [acc_ref[r * dh:(r + 1) * dh, lanes] / l_ref[r:r + 1, lanes] for r in range(NSA_REP)], axis=0)
        o_ref[lanes, :] = out_t.T


def flash_branch(q_t, k4, v_t, sel_t, tq, kt):
    qw, t = q_t.shape
    g, nk = k4.shape[:2]
    dh = v_t.shape[2]
    nsel = sel_t.shape[1]
    assert t % tq == 0 and tq % kt == 0 and kt % SEL_BLOCK == 0 and tq % FLASH_SUB == 0 and kt % FLASH_SUB == 0
    return pl.pallas_call(
        functools.partial(_flash_kernel, tq=tq, kt=kt),
        grid=(g, t // tq),
        in_specs=[
            pl.BlockSpec((NSA_GW, tq), lambda gi, i: (gi, i)),
            pl.BlockSpec((None, nk, kt, NSA_GW), lambda gi, i: (gi, 0, 0, 0), pipeline_mode=pl.Buffered(1)),
            pl.BlockSpec((None, nk, dh, kt), lambda gi, i: (gi, 0, 0, 0), pipeline_mode=pl.Buffered(1)),
            pl.BlockSpec((None, nsel, tq), lambda gi, i: (gi, 0, i)),
        ],
        out_specs=pl.BlockSpec((tq, NSA_GW), lambda gi, i: (i, gi)),
        scratch_shapes=[
            pltpu.VMEM((8, tq), F32),
            pltpu.VMEM((8, tq), F32),
            pltpu.VMEM((8, tq), F32),
            pltpu.VMEM((NSA_GW, tq), F32),
            pltpu.VMEM((NSA_REP * kt, tq), F32),
            pltpu.VMEM((NSA_REP * kt, tq), BF16),
            pltpu.VMEM((kt, tq), F32),
            pltpu.VMEM((NSA_REP * kt, NSA_GW), BF16),
            pltpu.VMEM((dh + FLASH_SUM_ROWS, kt), BF16),
            pltpu.VMEM((NSA_GW, tq), BF16),
        ],
        out_shape=jax.ShapeDtypeStruct((t, qw), F32),
        compiler_params=_cparams(("parallel", "arbitrary")),
        name="nsa_selected",
    )(q_t, k4, v_t, sel_t)


def _window_kernel(q_ref, k_ref, v_ref, o_ref, s_ref, p_ref, bias_ref, kbd_ref, vt_ref, *, tq, kt):
    dh = NSA_HEAD_DIM
    nw = WINDOW // kt + 1
    span = nw * kt
    n_tiles = tq // kt
    qs = (q_ref[...] * ((dh ** -0.5) * LOG2_E)).astype(BF16)

    def stage(h):
        i = pl.program_id(1) * n_tiles + h
        for w in range(nw):
            tile = i - (nw - 1) + w
            k4 = k_ref[jnp.maximum(tile, 0)]
            kseg = _div_pow2(lax.broadcasted_iota(jnp.int32, k4.shape, 1), dh)
            for r in range(NSA_REP):
                at = r * span + w * kt
                kbd_ref[h, at:at + kt, :] = jnp.where(kseg == r, k4, jnp.zeros_like(k4))
            vt_ref[h, :dh, w * kt:(w + 1) * kt] = v_ref[jnp.maximum(tile, 0)]
            tpos = i * kt + lax.broadcasted_iota(jnp.int32, (kt, kt), 1)
            kpos = tile * kt + lax.broadcasted_iota(jnp.int32, (kt, kt), 0)
            allowed = (kpos >= 0) & (kpos <= tpos) & (kpos > tpos - WINDOW)
            bias_ref[h, w * kt:(w + 1) * kt, :] = jnp.where(allowed, 0.0, -jnp.inf)
        vt_ref[h, dh:, :] = jnp.ones((FLASH_SUM_ROWS, span), BF16)

    def scores(h):
        s_ref[h] = jnp.dot(kbd_ref[h], qs[:, h * kt:(h + 1) * kt], preferred_element_type=F32)

    def softmax(h):
        for ch in range(kt // LANES):
            lanes = slice(ch * LANES, (ch + 1) * LANES)
            for r in range(NSA_REP):
                m = jnp.full((1, LANES), -jnp.inf, F32)
                for w in range(nw):
                    rows = slice(r * span + w * kt, r * span + (w + 1) * kt)
                    m = jnp.maximum(m, jnp.max(s_ref[h, rows, lanes] + bias_ref[h, w * kt:(w + 1) * kt, lanes],
                                               axis=0, keepdims=True))
                m = jnp.where(m > -jnp.inf, m, 0.0)
                for w in range(nw):
                    rows = slice(r * span + w * kt, r * span + (w + 1) * kt)
                    x = s_ref[h, rows, lanes] + bias_ref[h, w * kt:(w + 1) * kt, lanes]
                    p_ref[h, rows, lanes] = jnp.exp2(x - m).astype(BF16)

    def weighted(h):
        return [jnp.dot(vt_ref[h], p_ref[h, r * span:(r + 1) * span, :], preferred_element_type=F32)
                for r in range(NSA_REP)]

    def store(h, pv):
        for ch in range(kt // LANES):
            lanes = slice(ch * LANES, (ch + 1) * LANES)
            out_t = jnp.concatenate(
                [pv[r][:dh, lanes] / jnp.maximum(pv[r][dh:dh + 1, lanes], 1e-30) for r in range(NSA_REP)], axis=0)
            o_ref[h * kt + ch * LANES:h * kt + (ch + 1) * LANES, :] = out_t.T

    for h in range(n_tiles):
        stage(h)
        scores(h)
    pvs = []
    for h in range(n_tiles):
        softmax(h)
        pvs.append(weighted(h))
    for h in range(n_tiles):
        store(h, pvs[h])


def window_branch(q_t, k4, v_t, tq):
    qw, t = q_t.shape
    g, nk, kt, _ = k4.shape
    dh = v_t.shape[2]
    assert tq % kt == 0 and WINDOW % kt == 0 and t % tq == 0
    n_tiles = tq // kt
    span = (WINDOW // kt + 1) * kt
    return pl.pallas_call(
        functools.partial(_window_kernel, tq=tq, kt=kt),
        grid=(g, t // tq),
        in_specs=[
            pl.BlockSpec((NSA_GW, tq), lambda gi, i: (gi, i)),
            pl.BlockSpec((None, nk, kt, NSA_GW), lambda gi, i: (gi, 0, 0, 0)),
            pl.BlockSpec((None, nk, dh, kt), lambda gi, i: (gi, 0, 0, 0)),
        ],
        out_specs=pl.BlockSpec((tq, NSA_GW), lambda gi, i: (i, gi)),
        scratch_shapes=[
            pltpu.VMEM((n_tiles, NSA_REP * span, kt), F32),
            pltpu.VMEM((n_tiles, NSA_REP * span, kt), BF16),
            pltpu.VMEM((n_tiles, span, kt), F32),
            pltpu.VMEM((n_tiles, NSA_REP * span, NSA_GW), BF16),
            pltpu.VMEM((n_tiles, dh + FLASH_SUM_ROWS, span), BF16),
        ],
        out_shape=jax.ShapeDtypeStruct((t, qw), F32),
        compiler_params=_cparams(("parallel", "parallel")),
        name="nsa_window",
    )(q_t, k4, v_t)


def _nsa_out_kernel(oc_ref, os_ref, ow_ref, gl_ref, eg_ref, w_ref, gpost_ref, h_ref, out_ref):
    pieces = _split3(_sigmoid(gl_ref[...]))
    mixed = jnp.zeros(oc_ref.shape, F32)
    for b, br_ref in enumerate((oc_ref, os_ref, ow_ref)):
        gfull = jnp.zeros(oc_ref.shape, F32)
        for piece in pieces:
            gfull = gfull + jnp.dot(piece, eg_ref[b], preferred_element_type=F32)
        mixed = mixed + gfull * br_ref[...]
    mix = jnp.dot(mixed.astype(BF16), w_ref[...], preferred_element_type=F32)
    out_ref[...] = h_ref[...] + _rms(mix, gpost_ref[...])


def nsa_out(oc, osel, ow, gate_logits, expand, w_o, gpost, h, tm=512):
    t, qw = oc.shape
    d = h.shape[1]
    assert t % tm == 0
    row = lambda w: pl.BlockSpec((tm, w), lambda i: (i, 0))
    return pl.pallas_call(
        _nsa_out_kernel,
        grid=(t // tm,),
        in_specs=[
            row(qw), row(qw), row(qw), row(LANES),
            pl.BlockSpec((3, LANES, qw), lambda i: (0, 0, 0)),
            pl.BlockSpec((qw, d), lambda i: (0, 0)),
            pl.BlockSpec((1, d), lambda i: (0, 0)),
            row(d),
        ],
        out_specs=row(d),
        out_shape=jax.ShapeDtypeStruct((t, d), F32),
        compiler_params=_cparams(("parallel",)),
        name="nsa_out",
    )(oc, osel, ow, gate_logits, expand, w_o, gpost.reshape(1, d), h)


def _pad_cols(w, n):
    return jnp.pad(w, ((0, 0), (0, n - w.shape[1])))


def _importance_matrix(nc, nsel):
    r = SEL_BLOCK // CMP_STRIDE
    c = CMP_BLOCK // CMP_STRIDE
    mat = np.zeros((nc, nsel), np.float32)
    for kblk in range(nsel):
        for m in range(r):
            for n in range(c):
                j = r * kblk + m - n
                if 0 <= j < nc - 1:
                    mat[j, kblk] += 1.0
    return jnp.asarray(mat, BF16)


def _gate_expand():
    e = np.zeros((3, LANES, NSA_GROUPS * NSA_GW), np.float32)
    for head in range(NSA_GROUPS * NSA_REP):
        for b in range(3):
            e[b, head * 3 + b, head * NSA_HEAD_DIM:(head + 1) * NSA_HEAD_DIM] = 1.0
    return jnp.asarray(e, BF16)


def _block_diag_kv(k_cmp, v_cmp):
    g, nc, dh = k_cmp.shape
    eye = jnp.eye(NSA_REP, dtype=bool)
    kct = jnp.swapaxes(k_cmp, 1, 2)
    kbd = jnp.where(eye[None, :, None, :, None], kct[:, None, :, None, :], 0.0)
    vbd = jnp.where(eye[None, :, None, :, None], v_cmp[:, None, :, None, :], 0.0)
    return (kbd.reshape(g, NSA_REP * dh, NSA_REP * nc).astype(BF16),
            vbd.reshape(g, NSA_REP * nc, NSA_REP * dh).astype(BF16))


def kernel(x, p, mix_pre_norm, mix_post_norm, ffn_pre_norm, ffn_post_norm, gdn_w_in, gdn_conv_w, gdn_a_log,
           gdn_dt_bias, gdn_o_norm, gdn_w_out, kv_norm, kv_w, cmp_pos, cmp_w1, cmp_w2, nsa_w_qg, nsa_w_o,
           ffn_w_in, ffn_w_out, ple_w_in, ple_w_gate):
    depth = p.shape[0]
    n_a = gdn_w_in.shape[0]
    t = x.shape[1]
    h = x[0]
    fh = ffn_w_out.shape[1]
    conv_w_cols = gdn_conv_w.shape[2]
    vw = GDN_V_HEADS * GDN_HEAD_DIM
    main_w = conv_w_cols + vw

    def channel_and_ple(h, i):
        return ffn_ple(h, ffn_pre_norm[i], ffn_w_in[i].astype(BF16), ffn_w_out[i].astype(BF16),
                       ffn_post_norm[i], p[i, 0], ple_w_in[i].astype(BF16), ple_w_gate[i].astype(BF16))

    for i in range(n_a):
        w_in = gdn_w_in[i]
        w_beta = w_in[:, main_w:main_w + GDN_V_HEADS]
        w_a = w_in[:, main_w + GDN_V_HEADS:]
        w_small = _pad_cols(jnp.concatenate([w_beta, w_a, w_a], axis=1), LANES).astype(BF16)
        proj = norm_matmul(h, mix_pre_norm[i], w_in.astype(BF16), n=main_w)
        ba = norm_matmul(h, mix_pre_norm[i], w_small)
        pad_vec = lambda v: jnp.pad(v, (GDN_V_HEADS, LANES - 2 * GDN_V_HEADS))
        alog_vec = (pad_vec(gdn_a_log[i]) + jnp.pad(gdn_a_log[i], (2 * GDN_V_HEADS, LANES - 3 * GDN_V_HEADS)))
        dtb_vec = (pad_vec(gdn_dt_bias[i]) + jnp.pad(gdn_dt_bias[i], (2 * GDN_V_HEADS, LANES - 3 * GDN_V_HEADS)))
        slab = gdn_gates(ba, alog_vec.reshape(1, LANES), dtb_vec.reshape(1, LANES))
        qkv_hm = gdn_conv(proj, gdn_conv_w[i])
        kt_hm = jnp.swapaxes(qkv_hm[GDN_QK_HEADS:2 * GDN_QK_HEADS], 1, 2)
        o = gdn_scan(qkv_hm, kt_hm, slab, slab.T)
        onorm_tiled = jnp.tile(gdn_o_norm[i], GDN_V_HEADS).reshape(1, vw)
        h = gdn_out(o, proj, onorm_tiled, gdn_w_out[i].astype(BF16), mix_post_norm[i], h)
        h = channel_and_ple(h, i)

    g = NSA_GROUPS
    dh = NSA_HEAD_DIM
    kv = norm_matmul(h, kv_norm, kv_w.astype(BF16), tn=768)
    kv6 = jnp.transpose(kv.reshape(t, 6, g, dh), (1, 2, 0, 3))
    nc = t // CMP_STRIDE
    nsel = t // SEL_BLOCK
    x2 = kv6[0:2].reshape(2, g, nc, CMP_STRIDE * dh)
    pos2 = cmp_pos.reshape(2, 2, CMP_STRIDE * dh)
    cmp_out = compress(x2, pos2, cmp_w1.astype(BF16), cmp_w2.astype(BF16))
    imp_mat_t = _importance_matrix(nc, nsel).T
    nq_r = t // (CMP_TQ * CMP_RANGES)
    assert t % (CMP_TQ * CMP_RANGES) == 0 and nc % CMP_RANGES == 0 and nsel % (8 * CMP_RANGES) == 0
    cmp_ranges = []
    for rg in range(CMP_RANGES):
        nc_r, nsel_r = (rg + 1) * nc // CMP_RANGES, (rg + 1) * nsel // CMP_RANGES
        vbd_t, kbd = _block_diag_kv(cmp_out[1][:, :nc_r], cmp_out[0][:, :nc_r])
        cmp_ranges.append((kbd, vbd_t, imp_mat_t[:nsel_r, :nc_r]))
    kt = FLASH_KT
    tiles = lambda a: a.reshape(g, t // kt, kt, dh)
    rep_k = lambda a: jnp.tile(tiles(a), (1, 1, 1, NSA_REP)).astype(BF16)
    rep_t = lambda a: jnp.swapaxes(tiles(a), 2, 3).astype(BF16)
    k_slc, v_slc_t = rep_k(kv6[2]), rep_t(kv6[3])
    k_win, v_win_t = rep_k(kv6[4]), rep_t(kv6[5])
    expand = _gate_expand()

    for i in range(n_a, depth):
        j = i - n_a
        qw = g * NSA_GW
        w_qg = nsa_w_qg[j]
        q_t = norm_matmul_t(h, mix_pre_norm[i], w_qg[:, :qw].T.astype(BF16))
        gate_logits = norm_matmul(h, mix_pre_norm[i], _pad_cols(w_qg[:, qw:], LANES).astype(BF16))
        cmp_res = (jnp.zeros((t, qw), F32), jnp.zeros((g, nsel, t), F32))
        for rg, (kbd, vbd_t, imp_r) in enumerate(cmp_ranges):
            cmp_res = cmp_topk(q_t, kbd, vbd_t, imp_r, nsel, min(SEL_TOPK, nsel), rg * nq_r, nq_r, CMP_TQ,
                               prev=cmp_res)
        o_c, sel_t = cmp_res
        o_s = flash_branch(q_t, k_slc, v_slc_t, sel_t, tq=min(FLASH_TQ, t), kt=kt)
        o_w = window_branch(q_t, k_win, v_win_t, tq=WINDOW_TQ)
        h = nsa_out(o_c, o_s, o_w, gate_logits, expand, nsa_w_o[j].astype(BF16), mix_post_norm[i], h)
        h = channel_and_ple(h, i)
    return h[None]
```

```python
import functools

import numpy as np
import jax
import jax.numpy as jnp
from jax import lax
from jax.experimental import pallas as pl
from jax.experimental.pallas import tpu as pltpu

F32 = jnp.float32
BF16 = jnp.bfloat16

NORM_EPS = 1e-6
L2_EPS = 1e-6
GDN_QK_HEADS = 8
GDN_V_HEADS = 16
GDN_HEAD_DIM = 128
GDN_CONV = 4
GDN_CHUNK = 64
GDN_GROUP = 256
GDN_SCAN_HEADS = 8
GDN_CONV_HEADS = 4
NSA_GROUPS = 4
NSA_REP = 4
NSA_HEAD_DIM = 64
NSA_GW = NSA_REP * NSA_HEAD_DIM
CMP_BLOCK = 32
CMP_STRIDE = 16
SEL_BLOCK = 64
SEL_TOPK = 16
WINDOW = 512
FORCED_SCORE = 1e4
LANES = 128
NEG_INIT = -(2.0 ** 100)
LOG2_E = 1.4426950408889634
FLASH_TQ = 2048
FLASH_KT = 256
WINDOW_TQ = 512
FLASH_SUB = 256
FLASH_SUM_ROWS = 16
CMP_ROWS = 256
CMP_TQ = 512
CMP_SUB = 256
CMP_RANGES = 4

VMEM_LIMIT = 56 * 1024 * 1024
FFN_VMEM_LIMIT = 60 * 1024 * 1024


def _cparams(sem, vmem_limit=VMEM_LIMIT):
    return pltpu.CompilerParams(dimension_semantics=sem, vmem_limit_bytes=vmem_limit)


def _rms(x, gain):
    return x * lax.rsqrt(jnp.mean(x * x, axis=-1, keepdims=True) + NORM_EPS) * gain


def _mm(a, b):
    return jnp.dot(a.astype(BF16), b.astype(BF16), preferred_element_type=F32)


def _sigmoid(x):
    return 1.0 / (1.0 + jnp.exp(-x))


def _div_pow2(x, d):
    shift = d.bit_length() - 1
    assert d == 1 << shift
    return jnp.right_shift(x, shift)


def _split3(x):
    a = x.astype(BF16)
    r = x - a.astype(F32)
    b = r.astype(BF16)
    c = (r - b.astype(F32)).astype(BF16)
    return a, b, c


def _norm_matmul_kernel(x_ref, g_ref, w_ref, o_ref, xn_ref):
    @pl.when(pl.program_id(1) == 0)
    def _():
        xn_ref[...] = _rms(x_ref[...], g_ref[...]).astype(BF16)

    o_ref[...] = jnp.dot(xn_ref[...], w_ref[...], preferred_element_type=F32)


def _norm_matmul_t_kernel(x_ref, g_ref, wt_ref, o_ref, xn_ref):
    @pl.when(pl.program_id(1) == 0)
    def _():
        xn_ref[...] = _rms(x_ref[...], g_ref[...]).astype(BF16)

    o_ref[...] = lax.dot_general(wt_ref[...], xn_ref[...], (((1,), (1,)), ((), ())), preferred_element_type=F32)


def norm_matmul_t(h, gain, w_t, tm=2048, tn=1024):
    t, d = h.shape
    n = w_t.shape[0]
    tn = min(tn, n)
    assert t % tm == 0 and n % tn == 0
    return pl.pallas_call(
        _norm_matmul_t_kernel,
        grid=(t // tm, n // tn),
        in_specs=[
            pl.BlockSpec((tm, d), lambda i, j: (i, 0)),
            pl.BlockSpec((1, d), lambda i, j: (0, 0)),
            pl.BlockSpec((tn, d), lambda i, j: (j, 0)),
        ],
        out_specs=pl.BlockSpec((tn, tm), lambda i, j: (j, i)),
        out_shape=jax.ShapeDtypeStruct((n, t), F32),
        scratch_shapes=[pltpu.VMEM((tm, d), BF16)],
        compiler_params=_cparams(("parallel", "arbitrary")),
        name="norm_matmul_t",
    )(h, gain.reshape(1, d), w_t)


def norm_matmul(h, gain, w, tm=2048, tn=1024, n=None):
    t, d = h.shape
    n = w.shape[1] if n is None else n
    tn = min(tn, n)
    assert t % tm == 0 and n % tn == 0
    return pl.pallas_call(
        _norm_matmul_kernel,
        grid=(t // tm, n // tn),
        in_specs=[
            pl.BlockSpec((tm, d), lambda i, j: (i, 0)),
            pl.BlockSpec((1, d), lambda i, j: (0, 0)),
            pl.BlockSpec((d, tn), lambda i, j: (0, j)),
        ],
        out_specs=pl.BlockSpec((tm, tn), lambda i, j: (i, j)),
        out_shape=jax.ShapeDtypeStruct((t, n), F32),
        scratch_shapes=[pltpu.VMEM((tm, d), BF16)],
        compiler_params=_cparams(("parallel", "arbitrary")),
        name="norm_matmul",
    )(h, gain.reshape(1, d), w)


def _ffn_ple_kernel(h_ref, gpre_ref, wg_ref, wu_ref, wo_ref, gpost_ref, p_ref, wple_ref, wgt_ref,
                    o_ref, xn_ref, acc_ref, *, nf):
    f = pl.program_id(1)

    @pl.when(f == 0)
    def _():
        xn_ref[...] = _rms(h_ref[...], gpre_ref[...]).astype(BF16)
        acc_ref[...] = jnp.zeros_like(acc_ref)

    xn = xn_ref[...]
    tf = wg_ref.shape[1]
    half = (tf // LANES // 2) * LANES
    for a, b in ((0, half), (half, tf)):
        gate = jnp.dot(xn, wg_ref[:, a:b], preferred_element_type=F32)
        up = jnp.dot(xn, wu_ref[:, a:b], preferred_element_type=F32)
        act = gate * _sigmoid(gate) * up
        acc_ref[...] += jnp.dot(act.astype(BF16), wo_ref[a:b, :], preferred_element_type=F32)

    @pl.when(f == nf - 1)
    def _():
        h2 = h_ref[...] + _rms(acc_ref[...], gpost_ref[...])
        emb = jnp.dot(p_ref[...].astype(BF16), wple_ref[...], preferred_element_type=F32)
        gt = _sigmoid(jnp.dot(h2.astype(BF16), wgt_ref[...], preferred_element_type=F32))
        o_ref[...] = h2 + emb * gt


def ffn_ple(h, gpre, w_in, w_out, gpost, p, w_ple, w_plegate, tm=1024, tf=1408):
    t, d = h.shape
    fh = w_out.shape[0]
    pd = p.shape[1]
    assert t % tm == 0 and fh % tf == 0 and w_in.shape[1] == 2 * fh
    nf = fh // tf
    return pl.pallas_call(
        functools.partial(_ffn_ple_kernel, nf=nf),
        grid=(t // tm, nf),
        in_specs=[
            pl.BlockSpec((tm, d), lambda i, f: (i, 0)),
            pl.BlockSpec((1, d), lambda i, f: (0, 0)),
            pl.BlockSpec((d, tf), lambda i, f: (0, f)),
            pl.BlockSpec((d, tf), lambda i, f: (0, nf + f)),
            pl.BlockSpec((tf, d), lambda i, f: (f, 0)),
            pl.BlockSpec((1, d), lambda i, f: (0, 0)),
            pl.BlockSpec((tm, pd), lambda i, f: (i, 0)),
            pl.BlockSpec((pd, d), lambda i, f: (0, 0), pipeline_mode=pl.Buffered(1)),
            pl.BlockSpec((d, d), lambda i, f: (0, 0), pipeline_mode=pl.Buffered(1)),
        ],
        out_specs=pl.BlockSpec((tm, d), lambda i, f: (i, 0)),
        out_shape=jax.ShapeDtypeStruct((t, d), F32),
        scratch_shapes=[pltpu.VMEM((tm, d), BF16), pltpu.VMEM((tm, d), F32)],
        compiler_params=_cparams(("parallel", "arbitrary"), FFN_VMEM_LIMIT),
        name="ffn_ple",
    )(h, gpre.reshape(1, d), w_in, w_in, w_out, gpost.reshape(1, d), p, w_ple, w_plegate)


def _gdn_conv_kernel(x_ref, halo_ref, w_ref, o_ref, *, tm):
    c = pl.program_id(0)
    i = pl.program_id(1)
    x = x_ref[...]
    halo = jnp.where(i > 0, halo_ref[...], 0.0)
    ext = jnp.concatenate([halo, x], axis=0)
    w = w_ref[...]
    y = x * w[GDN_CONV - 1:GDN_CONV, :]
    for k in range(1, GDN_CONV):
        shifted = pltpu.roll(ext, k, axis=0)[8:8 + tm]
        y = y + shifted * w[GDN_CONV - 1 - k:GDN_CONV - k, :]
    y = y * _sigmoid(y)
    for hd in range(GDN_CONV_HEADS):
        head = c * GDN_CONV_HEADS + hd
        seg = y[:, hd * LANES:(hd + 1) * LANES]
        normed = seg * lax.rsqrt(jnp.sum(seg * seg, axis=-1, keepdims=True) + L2_EPS)
        q_scale = jnp.where(head < GDN_QK_HEADS, GDN_HEAD_DIM ** -0.5, 1.0)
        o_ref[hd] = jnp.where(head < 2 * GDN_QK_HEADS, normed * q_scale, seg)


def gdn_conv(proj, conv_w, tm=1024):
    t = proj.shape[0]
    n_tiles = conv_w.shape[1] // LANES
    cw = GDN_CONV_HEADS
    assert t % tm == 0 and n_tiles % cw == 0
    return pl.pallas_call(
        functools.partial(_gdn_conv_kernel, tm=tm),
        grid=(n_tiles // cw, t // tm),
        in_specs=[
            pl.BlockSpec((tm, cw * LANES), lambda c, i: (i, c)),
            pl.BlockSpec((8, cw * LANES), lambda c, i: (jnp.maximum(i * (tm // 8) - 1, 0), c)),
            pl.BlockSpec((GDN_CONV, cw * LANES), lambda c, i: (0, c)),
        ],
        out_specs=pl.BlockSpec((cw, tm, LANES), lambda c, i: (c, i, 0)),
        out_shape=jax.ShapeDtypeStruct((n_tiles, t, LANES), F32),
        compiler_params=_cparams(("parallel", "parallel")),
        name="gdn_conv",
    )(proj, proj, conv_w)


def _gdn_gate_kernel(x_ref, alog_ref, dtb_ref, lc_ref, lf_ref, o_ref):
    x = x_ref[...]
    lane = lax.broadcasted_iota(jnp.int32, x.shape, 1)
    beta = _sigmoid(x)
    z = x + dtb_ref[...]
    softplus = jnp.maximum(z, 0.0) + jnp.log(1.0 + jnp.exp(-jnp.abs(z)))
    g = -jnp.exp(alog_ref[...]) * softplus
    gcum = jnp.zeros_like(x)
    gtot = jnp.zeros_like(x)
    for piece in _split3(g):
        gcum = gcum + jnp.dot(lc_ref[...], piece, preferred_element_type=F32)
        gtot = gtot + jnp.dot(lf_ref[...], piece, preferred_element_type=F32)
    o_ref[...] = jnp.where(lane < GDN_V_HEADS, beta, jnp.where(lane < 2 * GDN_V_HEADS, gcum, gtot))


def gdn_gates(ba, alog_vec, dtb_vec):
    t = ba.shape[0]
    tm = GDN_GROUP
    r = np.arange(tm)
    same = (r[:, None] // GDN_CHUNK) == (r[None, :] // GDN_CHUNK)
    lc = jnp.asarray(same & (r[None, :] <= r[:, None]), BF16)
    lf = jnp.asarray(same, BF16)
    return pl.pallas_call(
        _gdn_gate_kernel,
        grid=(t // tm,),
        in_specs=[
            pl.BlockSpec((tm, LANES), lambda i: (i, 0)),
            pl.BlockSpec((1, LANES), lambda i: (0, 0)),
            pl.BlockSpec((1, LANES), lambda i: (0, 0)),
            pl.BlockSpec((tm, tm), lambda i: (0, 0)),
            pl.BlockSpec((tm, tm), lambda i: (0, 0)),
        ],
        out_specs=pl.BlockSpec((tm, LANES), lambda i: (i, 0)),
        out_shape=jax.ShapeDtypeStruct((t, LANES), F32),
        compiler_params=_cparams(("parallel",)),
        name="gdn_gates",
    )(ba, alog_vec, dtb_vec, lc, lf)


def _gdn_scan_kernel(q_ref, k_ref, kt_ref, v_ref, slab_ref, slabt_ref, o_ref, s_ref):
    n = pl.program_id(1)

    @pl.when(n == 0)
    def _():
        s_ref[...] = jnp.zeros_like(s_ref)

    L = GDN_GROUP
    C = GDN_CHUNK
    D = GDN_HEAD_DIM
    slab = slab_ref[...]
    slabt = slabt_ref[...]
    row = lax.broadcasted_iota(jnp.int32, (L, L), 0)
    col = lax.broadcasted_iota(jnp.int32, (L, L), 1)
    same = _div_pow2(row, C) == _div_pow2(col, C)
    causal = same & (col <= row)
    strict = same & (col < row)
    eye = (row == col).astype(F32)
    lane = lax.broadcasted_iota(jnp.int32, (L, LANES), 1)
    sub = lax.broadcasted_iota(jnp.int32, (LANES, L), 0)

    def column(idx):
        return jnp.sum(jnp.where(lane == idx, slab, 0.0), axis=1, keepdims=True)

    def rowvec(idx):
        return jnp.sum(jnp.where(sub == idx, slabt, 0.0), axis=0, keepdims=True)

    colk = lax.broadcasted_iota(jnp.int32, (D, L), 1)
    slots = range(2 * GDN_SCAN_HEADS)

    q = [q_ref[hq] for hq in range(GDN_SCAN_HEADS)]
    k = [k_ref[hq] for hq in range(GDN_SCAN_HEADS)]
    kt = [kt_ref[hq] for hq in range(GDN_SCAN_HEADS)]
    kk = [_mm(k[hq], kt[hq]) for hq in range(GDN_SCAN_HEADS)]
    qk = [_mm(q[hq], kt[hq]) for hq in range(GDN_SCAN_HEADS)]
    beta_c, gc_c, gt_c, decay, bp, inv, kdt = [], [], [], [], [], [], []
    for slot in slots:
        hq = slot // 2
        hv = 2 * (pl.program_id(0) * GDN_SCAN_HEADS + hq) + slot % 2
        beta_c.append(column(hv))
        gc_c.append(column(GDN_V_HEADS + hv))
        gt_c.append(column(2 * GDN_V_HEADS + hv))
        gc_r = rowvec(GDN_V_HEADS + hv)
        gt_r = rowvec(2 * GDN_V_HEADS + hv)
        decay.append(jnp.where(causal, jnp.exp(jnp.where(causal, gc_c[slot] - gc_r, 0.0)), 0.0))
        bp.append(jnp.where(strict, -(kk[hq] * beta_c[slot]) * decay[slot], 0.0))
        inv.append(eye + bp[slot])
        kdt.append(kt[hq] * jnp.exp(gt_r - gc_r))
    for _ in range(5):
        bp = [_mm(bp[slot], bp[slot]) for slot in slots]
        inv = [inv[slot] + _mm(inv[slot], bp[slot]) for slot in slots]
    u, w, qkm, q_dec = [], [], [], []
    for slot in slots:
        hq = slot // 2
        egc = jnp.exp(gc_c[slot])
        rhs = jnp.concatenate([v_ref[slot] * beta_c[slot], k[hq] * (beta_c[slot] * egc)], axis=1)
        sol = _mm(inv[slot], rhs)
        u.append(sol[:, :D])
        w.append(sol[:, D:])
        qkm.append(jnp.where(causal, qk[hq] * decay[slot], 0.0))
        q_dec.append(q[hq] * egc)
    state = [s_ref[slot] for slot in slots]
    v_done = [[] for _ in slots]
    for c in range(L // C):
        lo, hi = c * C, (c + 1) * C
        r = [_mm(jnp.concatenate([w[slot][lo:hi], q_dec[slot][lo:hi]], axis=0), state[slot]) for slot in slots]
        for slot in slots:
            v_done[slot].append(u[slot][lo:hi] - r[slot][:C])
            v_all = jnp.concatenate(v_done[slot] + [jnp.zeros((L - hi, D), F32)] * (hi < L), axis=0)
            kdt_c = jnp.where((colk >= lo) & (colk < hi), kdt[slot], 0.0)
            both = _mm(jnp.concatenate([qkm[slot][lo:hi, :], kdt_c], axis=0), v_all)
            o_ref[lo:hi, slot * D:(slot + 1) * D] = r[slot][C:] + both[:C]
            state[slot] = state[slot] * jnp.exp(gt_c[slot][lo:lo + 1, :]) + both[C:]
    for slot in slots:
        s_ref[slot] = state[slot]


def gdn_scan(qkv_hm, kt_hm, slab, slabt):
    t = qkv_hm.shape[1]
    L = GDN_GROUP
    D = GDN_HEAD_DIM
    hq = GDN_SCAN_HEADS
    assert t % L == 0 and GDN_QK_HEADS % hq == 0
    q_blocks = GDN_QK_HEADS // hq
    return pl.pallas_call(
        _gdn_scan_kernel,
        grid=(q_blocks, t // L),
        in_specs=[
            pl.BlockSpec((hq, L, D), lambda j, n: (j, n, 0)),
            pl.BlockSpec((hq, L, D), lambda j, n: (q_blocks + j, n, 0)),
            pl.BlockSpec((hq, D, L), lambda j, n: (j, 0, n)),
            pl.BlockSpec((2 * hq, L, D), lambda j, n: (q_blocks + j, n, 0)),
            pl.BlockSpec((L, LANES), lambda j, n: (n, 0)),
            pl.BlockSpec((LANES, L), lambda j, n: (0, n)),
        ],
        out_specs=pl.BlockSpec((L, 2 * hq * D), lambda j, n: (n, j)),
        out_shape=jax.ShapeDtypeStruct((t, GDN_V_HEADS * D), F32),
        scratch_shapes=[pltpu.VMEM((2 * hq, D, D), F32)],
        compiler_params=_cparams(("parallel", "arbitrary")),
        name="gdn_scan",
    )(qkv_hm, qkv_hm, kt_hm, qkv_hm, slab, slabt)


def _gdn_out_kernel(o_ref, z_ref, onorm_ref, w_ref, gpost_ref, h_ref, out_ref):
    o = o_ref[...]
    z = z_ref[...]
    parts = []
    for hd in range(GDN_V_HEADS):
        seg = o[:, hd * GDN_HEAD_DIM:(hd + 1) * GDN_HEAD_DIM]
        parts.append(seg * lax.rsqrt(jnp.mean(seg * seg, axis=-1, keepdims=True) + NORM_EPS))
    gated = jnp.concatenate(parts, axis=1) * onorm_ref[...] * (z * _sigmoid(z))
    mix = jnp.dot(gated.astype(BF16), w_ref[...], preferred_element_type=F32)
    out_ref[...] = h_ref[...] + _rms(mix, gpost_ref[...])


def gdn_out(o, proj, onorm_tiled, w_out, gpost, h, tm=512):
    t, vw = o.shape
    d = h.shape[1]
    z_blk = (proj.shape[1] - vw) // vw
    assert proj.shape[1] % vw == 0 and t % tm == 0
    return pl.pallas_call(
        _gdn_out_kernel,
        grid=(t // tm,),
        in_specs=[
            pl.BlockSpec((tm, vw), lambda i: (i, 0)),
            pl.BlockSpec((tm, vw), lambda i: (i, z_blk)),
            pl.BlockSpec((1, vw), lambda i: (0, 0)),
            pl.BlockSpec((vw, d), lambda i: (0, 0)),
            pl.BlockSpec((1, d), lambda i: (0, 0)),
            pl.BlockSpec((tm, d), lambda i: (i, 0)),
        ],
        out_specs=pl.BlockSpec((tm, d), lambda i: (i, 0)),
        out_shape=jax.ShapeDtypeStruct((t, d), F32),
        compiler_params=_cparams(("parallel",)),
        name="gdn_out",
    )(o, proj, onorm_tiled, w_out, gpost.reshape(1, d), h)


def _compress_kernel(x_ref, pos_ref, w1_ref, w2_ref, o_ref):
    x = x_ref[...]
    pos = pos_ref[...]
    nc, half = x.shape
    w1 = w1_ref[...]
    first = _mm(x + pos[0:1, :], w1[:half])
    second = _mm(x + pos[1:2, :], w1[half:])
    hid = first + pltpu.roll(second, nc - 1, axis=0)
    hid = hid * _sigmoid(hid)
    out = jnp.dot(hid.astype(BF16), w2_ref[...], preferred_element_type=F32)
    rowi = lax.broadcasted_iota(jnp.int32, out.shape, 0)
    o_ref[...] = jnp.where(rowi < nc - 1, out, 0.0)


def compress(x2, pos2, w1, w2):
    _, g, nc, wdt = x2.shape
    hid = w1.shape[2]
    dh = w2.shape[2]
    return pl.pallas_call(
        _compress_kernel,
        grid=(2, g),
        in_specs=[
            pl.BlockSpec((None, None, nc, wdt), lambda b, gi: (b, gi, 0, 0)),
            pl.BlockSpec((None, 2, wdt), lambda b, gi: (b, 0, 0)),
            pl.BlockSpec((None, 2 * wdt, hid), lambda b, gi: (b, 0, 0)),
            pl.BlockSpec((None, hid, dh), lambda b, gi: (b, 0, 0)),
        ],
        out_specs=pl.BlockSpec((None, None, nc, dh), lambda b, gi: (b, gi, 0, 0)),
        out_shape=jax.ShapeDtypeStruct((2, g, nc, dh), F32),
        compiler_params=_cparams(("parallel", "parallel")),
        name="nsa_compress",
    )(x2, pos2, w1, w2)


def _cmp_topk_kernel(q_ref, kbd_ref, vt_ref, m_ref, *refs, tq, nc, nsel, topk, q0):
    oc_ref, sel_ref, s_ref, p_ref, psum_ref = refs[-5:]
    i = q0 + pl.program_id(1)
    dh = NSA_HEAD_DIM
    rows_per = min(nc, CMP_ROWS)
    q = (q_ref[...] * ((dh ** -0.5) * LOG2_E)).astype(BF16)
    subs = [slice(c0, c0 + CMP_SUB) for c0 in range(0, tq, CMP_SUB)]

    def scores(cols):
        s_ref[:, cols] = jnp.dot(kbd_ref[...], q[:, cols], preferred_element_type=F32)

    def softmax(cols):
        for c0 in range(cols.start, cols.stop, LANES):
            lanes = slice(c0, c0 + LANES)
            tpos = i * tq + c0 + lax.broadcasted_iota(jnp.int32, (rows_per, LANES), 1)
            cblk0 = lax.broadcasted_iota(jnp.int32, (rows_per, LANES), 0)
            masks = [None if CMP_STRIDE * ((c + 1) * rows_per - 1) + CMP_BLOCK - 1 <= q0 * tq
                     else (CMP_STRIDE * (cblk0 + c * rows_per) + CMP_BLOCK - 1) <= tpos
                     for c in range(nc // rows_per)]
            visible = lambda rows, mask: s_ref[rows, lanes] if mask is None else jnp.where(mask, s_ref[rows, lanes],
                                                                                            -jnp.inf)
            for r in range(NSA_REP):
                pieces = [slice(r * nc + c * rows_per, r * nc + (c + 1) * rows_per) for c in range(nc // rows_per)]
                m = jnp.full((1, LANES), -jnp.inf, F32)
                for rows, mask in zip(pieces, masks):
                    m = jnp.maximum(m, jnp.max(visible(rows, mask), axis=0, keepdims=True))
                m = jnp.where(m > -jnp.inf, m, 0.0)
                total = jnp.zeros((1, LANES), F32)
                for rows, mask in zip(pieces, masks):
                    e = jnp.exp2(visible(rows, mask) - m)
                    s_ref[rows, lanes] = e
                    total = total + jnp.sum(e, axis=0, keepdims=True)
                inv = 1.0 / jnp.maximum(total, 1e-30)
                for c, rows in enumerate(pieces):
                    p = s_ref[rows, lanes] * inv
                    p_ref[rows, lanes] = p.astype(BF16)
                    prow = slice(c * rows_per, (c + 1) * rows_per)
                    psum_ref[prow, lanes] = p if r == 0 else psum_ref[prow, lanes] + p

    def outputs(cols):
        oc_t = jnp.dot(vt_ref[...], p_ref[:, cols], preferred_element_type=F32)
        imp = jnp.zeros((nsel, CMP_SUB), F32)
        for piece in _split3(psum_ref[:, cols]):
            imp = imp + jnp.dot(m_ref[...], piece, preferred_element_type=F32)
        return oc_t, imp

    def select(cols, oc_t, imp):
        for c0 in range(0, CMP_SUB, LANES):
            local = slice(c0, c0 + LANES)
            lanes = slice(cols.start + c0, cols.start + c0 + LANES)
            oc_ref[lanes, :] = oc_t[:, local].T
            t1 = i * tq + cols.start + c0 + lax.broadcasted_iota(jnp.int32, (nsel, LANES), 1)
            blk = lax.broadcasted_iota(jnp.int32, (nsel, LANES), 0)
            cur = _div_pow2(t1, SEL_BLOCK)
            forced = (blk == 0) | (blk == cur) | (blk == cur - 1)
            valid = blk * SEL_BLOCK <= t1
            score = jnp.where(valid, jnp.where(forced, FORCED_SCORE, imp[:, local]), -jnp.inf)
            blkf = blk.astype(F32)
            work = score
            for _ in range(topk):
                mx = jnp.max(work, axis=0, keepdims=True)
                first = jnp.min(jnp.where(work == mx, blkf, float(nsel)), axis=0, keepdims=True)
                work = jnp.where(blkf == first, -jnp.inf, work)
            sel_ref[:nsel, lanes] = jnp.where((score > -jnp.inf) & (work == -jnp.inf), 1.0, 0.0).astype(sel_ref.dtype)
            if sel_ref.shape[0] > nsel:
                sel_ref[nsel:, lanes] = jnp.zeros((sel_ref.shape[0] - nsel, LANES), sel_ref.dtype)

    for cols in subs:
        scores(cols)
    results = []
    for cols in subs:
        softmax(cols)
        results.append(outputs(cols))
    for cols, (oc_t, imp) in zip(subs, results):
        select(cols, oc_t, imp)


def cmp_topk(q_t, kbd, v_t, imp_mat_t, nsel_all, topk, q0, nq, tq, prev):
    qw, t = q_t.shape
    g = kbd.shape[0]
    nc = kbd.shape[1] // NSA_REP
    nsel = imp_mat_t.shape[0]
    assert nc % min(nc, CMP_ROWS) == 0
    in_specs = [
        pl.BlockSpec((NSA_GW, tq), lambda gi, i: (gi, q0 + i)),
        pl.BlockSpec((None, NSA_REP * nc, NSA_GW), lambda gi, i: (gi, 0, 0)),
        pl.BlockSpec((None, NSA_GW, NSA_REP * nc), lambda gi, i: (gi, 0, 0)),
        pl.BlockSpec((nsel, nc), lambda gi, i: (0, 0)),
    ]
    in_specs += [pl.BlockSpec(memory_space=pl.ANY)] * 2
    args = [q_t, kbd, v_t, imp_mat_t, *prev]
    aliases = {len(args) - 2: 0, len(args) - 1: 1}
    return pl.pallas_call(
        functools.partial(_cmp_topk_kernel, tq=tq, nc=nc, nsel=nsel, topk=topk, q0=q0),
        grid=(g, nq),
        in_specs=in_specs,
        out_specs=[
            pl.BlockSpec((tq, NSA_GW), lambda gi, i: (q0 + i, gi)),
            pl.BlockSpec((None, nsel_all, tq), lambda gi, i: (gi, 0, q0 + i)),
        ],
        out_shape=[
            jax.ShapeDtypeStruct((t, qw), F32),
            jax.ShapeDtypeStruct((g, nsel_all, t), F32),
        ],
        input_output_aliases=aliases,
        scratch_shapes=[
            pltpu.VMEM((NSA_REP * nc, tq), F32),
            pltpu.VMEM((NSA_REP * nc, tq), BF16),
            pltpu.VMEM((nc, tq), F32),
        ],
        compiler_params=_cparams(("parallel", "parallel")),
        name="nsa_cmp_topk",
    )(*args)


def _flash_kernel(q_ref, k_ref, v_ref, sel_ref, o_ref,
                  m_ref, l_ref, alpha_ref, acc_ref, s_ref, p_ref, bias_ref, kbd_ref, vt_ref, qs_ref, *, tq, kt):
    qi = pl.program_id(1)
    dh = NSA_HEAD_DIM
    n_sub = tq // FLASH_SUB
    n_blk = kt // SEL_BLOCK

    m_ref[...] = jnp.full_like(m_ref, NEG_INIT)
    l_ref[...] = jnp.zeros_like(l_ref)
    acc_ref[...] = jnp.zeros_like(acc_ref)
    qs_ref[...] = (q_ref[...] * ((dh ** -0.5) * LOG2_E)).astype(BF16)
    vt_ref[dh:, :] = jnp.ones((FLASH_SUM_ROWS, kt), BF16)

    def key_tile(ki, positional, first_sub=0):
        k4 = k_ref[ki]
        kseg = _div_pow2(lax.broadcasted_iota(jnp.int32, k4.shape, 1), dh)
        for r in range(NSA_REP):
            kbd_ref[r * kt:(r + 1) * kt, :] = jnp.where(kseg == r, k4, jnp.zeros_like(k4))
        vt_ref[:dh, :] = v_ref[ki]

        def scores(sub):
            c0 = sub * FLASH_SUB
            cols = slice(c0, c0 + FLASH_SUB)
            s_ref[:, cols] = jnp.dot(kbd_ref[...], qs_ref[:, cols],
                                     preferred_element_type=F32)
            if not positional:
                return
            for jb in range(n_blk):
                rows = slice(jb * SEL_BLOCK, (jb + 1) * SEL_BLOCK)
                picked = sel_ref[pl.ds(ki * n_blk + jb, 1), cols] > 0.5
                tpos = qi * tq + c0 + lax.broadcasted_iota(jnp.int32, (SEL_BLOCK, FLASH_SUB), 1)
                kpos = ki * kt + jb * SEL_BLOCK + lax.broadcasted_iota(jnp.int32, (SEL_BLOCK, FLASH_SUB), 0)
                bias_ref[rows, cols] = jnp.where(picked & (kpos <= tpos), 0.0, -jnp.inf)

        scores(first_sub)
        for sub in range(first_sub, n_sub):
            c0 = sub * FLASH_SUB
            cols = slice(c0, c0 + FLASH_SUB)
            if sub + 1 < n_sub:
                scores(sub + 1)
            if not positional:
                picked_sub = [sel_ref[pl.ds(ki * n_blk + jb, 1), cols] > 0.5 for jb in range(n_blk)]
            for ch in range(FLASH_SUB // LANES):
                lanes = slice(c0 + ch * LANES, c0 + (ch + 1) * LANES)
                if positional:
                    bias = bias_ref[:, lanes]
                else:
                    picked = [pk[:, ch * LANES:(ch + 1) * LANES] for pk in picked_sub]
                for r in range(NSA_REP):
                    if positional:
                        x = s_ref[r * kt:(r + 1) * kt, lanes] + bias
                    else:
                        x = jnp.concatenate(
                            [jnp.where(picked[jb], s_ref[r * kt + jb * SEL_BLOCK:r * kt + (jb + 1) * SEL_BLOCK, lanes],
                                       -jnp.inf) for jb in range(n_blk)], axis=0)
                    m_prev = m_ref[r:r + 1, lanes]
                    m_new = jnp.maximum(m_prev, jnp.max(x, axis=0, keepdims=True))
                    m_ref[r:r + 1, lanes] = m_new
                    alpha_ref[r:r + 1, lanes] = jnp.exp2(m_prev - m_new)
                    p_ref[r * kt:(r + 1) * kt, lanes] = jnp.exp2(x - m_new).astype(BF16)
            for r in range(NSA_REP):
                pv = jnp.dot(vt_ref[...], p_ref[r * kt:(r + 1) * kt, cols],
                             preferred_element_type=F32)
                hd = slice(r * dh, (r + 1) * dh)
                alpha = alpha_ref[r:r + 1, cols]
                acc_ref[hd, cols] = acc_ref[hd, cols] * alpha + pv[:dh]
                l_ref[r:r + 1, cols] = l_ref[r:r + 1, cols] * alpha + pv[dh:dh + 1]

    def before_diagonal(ki, carry):
        key_tile(ki, False)
        return carry

    diag = qi * (tq // kt)
    lax.fori_loop(0, diag, before_diagonal, 0)
    for d in range(tq // kt):
        def on_diagonal(ki, carry, first_sub=d * kt // FLASH_SUB):
            key_tile(ki, True, first_sub=first_sub)
            return carry
        lax.fori_loop(diag + d, diag + d + 1, on_diagonal, 0)

    for ch in range(tq // LANES):
        lanes = slice(ch * LANES, (ch + 1) * LANES)
        out_t = jnp.concatenate(
            [acc_ref[r * dh:(r + 1) * dh, lanes] / l_ref[r:r + 1, lanes] for r in range(NSA_REP)], axis=0)
        o_ref[lanes, :] = out_t.T


def flash_branch(q_t, k4, v_t, sel_t, tq, kt):
    qw, t = q_t.shape
    g, nk = k4.shape[:2]
    dh = v_t.shape[2]
    nsel = sel_t.shape[1]
    assert t % tq == 0 and tq % kt == 0 and kt % SEL_BLOCK == 0 and tq % FLASH_SUB == 0 and kt % FLASH_SUB == 0
    return pl.pallas_call(
        functools.partial(_flash_kernel, tq=tq, kt=kt),
        grid=(g, t // tq),
        in_specs=[
            pl.BlockSpec((NSA_GW, tq), lambda gi, i: (gi, i)),
            pl.BlockSpec((None, nk, kt, NSA_GW), lambda gi, i: (gi, 0, 0, 0), pipeline_mode=pl.Buffered(1)),
            pl.BlockSpec((None, nk, dh, kt), lambda gi, i: (gi, 0, 0, 0), pipeline_mode=pl.Buffered(1)),
            pl.BlockSpec((None, nsel, tq), lambda gi, i: (gi, 0, i)),
        ],
        out_specs=pl.BlockSpec((tq, NSA_GW), lambda gi, i: (i, gi)),
        scratch_shapes=[
            pltpu.VMEM((8, tq), F32),
            pltpu.VMEM((8, tq), F32),
            pltpu.VMEM((8, tq), F32),
            pltpu.VMEM((NSA_GW, tq), F32),
            pltpu.VMEM((NSA_REP * kt, tq), F32),
            pltpu.VMEM((NSA_REP * kt, tq), BF16),
            pltpu.VMEM((kt, tq), F32),
            pltpu.VMEM((NSA_REP * kt, NSA_GW), BF16),
            pltpu.VMEM((dh + FLASH_SUM_ROWS, kt), BF16),
            pltpu.VMEM((NSA_GW, tq), BF16),
        ],
        out_shape=jax.ShapeDtypeStruct((t, qw), F32),
        compiler_params=_cparams(("parallel", "arbitrary")),
        name="nsa_selected",
    )(q_t, k4, v_t, sel_t)


def _window_kernel(q_ref, k_ref, v_ref, o_ref, s_ref, p_ref, bias_ref, kbd_ref, vt_ref, *, tq, kt):
    dh = NSA_HEAD_DIM
    nw = WINDOW // kt + 1
    span = nw * kt
    n_tiles = tq // kt
    qs = (q_ref[...] * ((dh ** -0.5) * LOG2_E)).astype(BF16)

    def stage(h):
        i = pl.program_id(1) * n_tiles + h
        for w in range(nw):
            tile = i - (nw - 1) + w
            k4 = k_ref[jnp.maximum(tile, 0)]
            kseg = _div_pow2(lax.broadcasted_iota(jnp.int32, k4.shape, 1), dh)
            for r in range(NSA_REP):
                at = r * span + w * kt
                kbd_ref[h, at:at + kt, :] = jnp.where(kseg == r, k4, jnp.zeros_like(k4))
            vt_ref[h, :dh, w * kt:(w + 1) * kt] = v_ref[jnp.maximum(tile, 0)]
            tpos = i * kt + lax.broadcasted_iota(jnp.int32, (kt, kt), 1)
            kpos = tile * kt + lax.broadcasted_iota(jnp.int32, (kt, kt), 0)
            allowed = (kpos >= 0) & (kpos <= tpos) & (kpos > tpos - WINDOW)
            bias_ref[h, w * kt:(w + 1) * kt, :] = jnp.where(allowed, 0.0, -jnp.inf)
        vt_ref[h, dh:, :] = jnp.ones((FLASH_SUM_ROWS, span), BF16)

    def scores(h):
        s_ref[h] = jnp.dot(kbd_ref[h], qs[:, h * kt:(h + 1) * kt], preferred_element_type=F32)

    def softmax(h):
        for ch in range(kt // LANES):
            lanes = slice(ch * LANES, (ch + 1) * LANES)
            for r in range(NSA_REP):
                m = jnp.full((1, LANES), -jnp.inf, F32)
                for w in range(nw):
                    rows = slice(r * span + w * kt, r * span + (w + 1) * kt)
                    m = jnp.maximum(m, jnp.max(s_ref[h, rows, lanes] + bias_ref[h, w * kt:(w + 1) * kt, lanes],
                                               axis=0, keepdims=True))
                m = jnp.where(m > -jnp.inf, m, 0.0)
                for w in range(nw):
                    rows = slice(r * span + w * kt, r * span + (w + 1) * kt)
                    x = s_ref[h, rows, lanes] + bias_ref[h, w * kt:(w + 1) * kt, lanes]
                    p_ref[h, rows, lanes] = jnp.exp2(x - m).astype(BF16)

    def weighted(h):
        return [jnp.dot(vt_ref[h], p_ref[h, r * span:(r + 1) * span, :], preferred_element_type=F32)
                for r in range(NSA_REP)]

    def store(h, pv):
        for ch in range(kt // LANES):
            lanes = slice(ch * LANES, (ch + 1) * LANES)
            out_t = jnp.concatenate(
                [pv[r][:dh, lanes] / jnp.maximum(pv[r][dh:dh + 1, lanes], 1e-30) for r in range(NSA_REP)], axis=0)
            o_ref[h * kt + ch * LANES:h * kt + (ch + 1) * LANES, :] = out_t.T

    for h in range(n_tiles):
        stage(h)
        scores(h)
    pvs = []
    for h in range(n_tiles):
        softmax(h)
        pvs.append(weighted(h))
    for h in range(n_tiles):
        store(h, pvs[h])


def window_branch(q_t, k4, v_t, tq):
    qw, t = q_t.shape
    g, nk, kt, _ = k4.shape
    dh = v_t.shape[2]
    assert tq % kt == 0 and WINDOW % kt == 0 and t % tq == 0
    n_tiles = tq // kt
    span = (WINDOW // kt + 1) * kt
    return pl.pallas_call(
        functools.partial(_window_kernel, tq=tq, kt=kt),
        grid=(g, t // tq),
        in_specs=[
            pl.BlockSpec((NSA_GW, tq), lambda gi, i: (gi, i)),
            pl.BlockSpec((None, nk, kt, NSA_GW), lambda gi, i: (gi, 0, 0, 0)),
            pl.BlockSpec((None, nk, dh, kt), lambda gi, i: (gi, 0, 0, 0)),
        ],
        out_specs=pl.BlockSpec((tq, NSA_GW), lambda gi, i: (i, gi)),
        scratch_shapes=[
            pltpu.VMEM((n_tiles, NSA_REP * span, kt), F32),
            pltpu.VMEM((n_tiles, NSA_REP * span, kt), BF16),
            pltpu.VMEM((n_tiles, span, kt), F32),
            pltpu.VMEM((n_tiles, NSA_REP * span, NSA_GW), BF16),
            pltpu.VMEM((n_tiles, dh + FLASH_SUM_ROWS, span), BF16),
        ],
        out_shape=jax.ShapeDtypeStruct((t, qw), F32),
        compiler_params=_cparams(("parallel", "parallel")),
        name="nsa_window",
    )(q_t, k4, v_t)


def _nsa_out_kernel(oc_ref, os_ref, ow_ref, gl_ref, eg_ref, w_ref, gpost_ref, h_ref, out_ref):
    pieces = _split3(_sigmoid(gl_ref[...]))
    mixed = jnp.zeros(oc_ref.shape, F32)
    for b, br_ref in enumerate((oc_ref, os_ref, ow_ref)):
        gfull = jnp.zeros(oc_ref.shape, F32)
        for piece in pieces:
            gfull = gfull + jnp.dot(piece, eg_ref[b], preferred_element_type=F32)
        mixed = mixed + gfull * br_ref[...]
    mix = jnp.dot(mixed.astype(BF16), w_ref[...], preferred_element_type=F32)
    out_ref[...] = h_ref[...] + _rms(mix, gpost_ref[...])


def nsa_out(oc, osel, ow, gate_logits, expand, w_o, gpost, h, tm=512):
    t, qw = oc.shape
    d = h.shape[1]
    assert t % tm == 0
    row = lambda w: pl.BlockSpec((tm, w), lambda i: (i, 0))
    return pl.pallas_call(
        _nsa_out_kernel,
        grid=(t // tm,),
        in_specs=[
            row(qw), row(qw), row(qw), row(LANES),
            pl.BlockSpec((3, LANES, qw), lambda i: (0, 0, 0)),
            pl.BlockSpec((qw, d), lambda i: (0, 0)),
            pl.BlockSpec((1, d), lambda i: (0, 0)),
            row(d),
        ],
        out_specs=row(d),
        out_shape=jax.ShapeDtypeStruct((t, d), F32),
        compiler_params=_cparams(("parallel",)),
        name="nsa_out",
    )(oc, osel, ow, gate_logits, expand, w_o, gpost.reshape(1, d), h)


def _pad_cols(w, n):
    return jnp.pad(w, ((0, 0), (0, n - w.shape[1])))


def _importance_matrix(nc, nsel):
    r = SEL_BLOCK // CMP_STRIDE
    c = CMP_BLOCK // CMP_STRIDE
    mat = np.zeros((nc, nsel), np.float32)
    for kblk in range(nsel):
        for m in range(r):
            for n in range(c):
                j = r * kblk + m - n
                if 0 <= j < nc - 1:
                    mat[j, kblk] += 1.0
    return jnp.asarray(mat, BF16)


def _gate_expand():
    e = np.zeros((3, LANES, NSA_GROUPS * NSA_GW), np.float32)
    for head in range(NSA_GROUPS * NSA_REP):
        for b in range(3):
            e[b, head * 3 + b, head * NSA_HEAD_DIM:(head + 1) * NSA_HEAD_DIM] = 1.0
    return jnp.asarray(e, BF16)


def _block_diag_kv(k_cmp, v_cmp):
    g, nc, dh = k_cmp.shape
    eye = jnp.eye(NSA_REP, dtype=bool)
    kct = jnp.swapaxes(k_cmp, 1, 2)
    kbd = jnp.where(eye[None, :, None, :, None], kct[:, None, :, None, :], 0.0)
    vbd = jnp.where(eye[None, :, None, :, None], v_cmp[:, None, :, None, :], 0.0)
    return (kbd.reshape(g, NSA_REP * dh, NSA_REP * nc).astype(BF16),
            vbd.reshape(g, NSA_REP * nc, NSA_REP * dh).astype(BF16))


def kernel(x, p, mix_pre_norm, mix_post_norm, ffn_pre_norm, ffn_post_norm, gdn_w_in, gdn_conv_w, gdn_a_log,
           gdn_dt_bias, gdn_o_norm, gdn_w_out, kv_norm, kv_w, cmp_pos, cmp_w1, cmp_w2, nsa_w_qg, nsa_w_o,
           ffn_w_in, ffn_w_out, ple_w_in, ple_w_gate):
    depth = p.shape[0]
    n_a = gdn_w_in.shape[0]
    t = x.shape[1]
    h = x[0]
    fh = ffn_w_out.shape[1]
    conv_w_cols = gdn_conv_w.shape[2]
    vw = GDN_V_HEADS * GDN_HEAD_DIM
    main_w = conv_w_cols + vw

    def channel_and_ple(h, i):
        return ffn_ple(h, ffn_pre_norm[i], ffn_w_in[i].astype(BF16), ffn_w_out[i].astype(BF16),
                       ffn_post_norm[i], p[i, 0], ple_w_in[i].astype(BF16), ple_w_gate[i].astype(BF16))

    for i in range(n_a):
        w_in = gdn_w_in[i]
        w_beta = w_in[:, main_w:main_w + GDN_V_HEADS]
        w_a = w_in[:, main_w + GDN_V_HEADS:]
        w_small = _pad_cols(jnp.concatenate([w_beta, w_a, w_a], axis=1), LANES).astype(BF16)
        proj = norm_matmul(h, mix_pre_norm[i], w_in.astype(BF16), n=main_w)
        ba = norm_matmul(h, mix_pre_norm[i], w_small)
        pad_vec = lambda v: jnp.pad(v, (GDN_V_HEADS, LANES - 2 * GDN_V_HEADS))
        alog_vec = (pad_vec(gdn_a_log[i]) + jnp.pad(gdn_a_log[i], (2 * GDN_V_HEADS, LANES - 3 * GDN_V_HEADS)))
        dtb_vec = (pad_vec(gdn_dt_bias[i]) + jnp.pad(gdn_dt_bias[i], (2 * GDN_V_HEADS, LANES - 3 * GDN_V_HEADS)))
        slab = gdn_gates(ba, alog_vec.reshape(1, LANES), dtb_vec.reshape(1, LANES))
        qkv_hm = gdn_conv(proj, gdn_conv_w[i])
        kt_hm = jnp.swapaxes(qkv_hm[GDN_QK_HEADS:2 * GDN_QK_HEADS], 1, 2)
        o = gdn_scan(qkv_hm, kt_hm, slab, slab.T)
        onorm_tiled = jnp.tile(gdn_o_norm[i], GDN_V_HEADS).reshape(1, vw)
        h = gdn_out(o, proj, onorm_tiled, gdn_w_out[i].astype(BF16), mix_post_norm[i], h)
        h = channel_and_ple(h, i)

    g = NSA_GROUPS
    dh = NSA_HEAD_DIM
    kv = norm_matmul(h, kv_norm, kv_w.astype(BF16), tn=768)
    kv6 = jnp.transpose(kv.reshape(t, 6, g, dh), (1, 2, 0, 3))
    nc = t // CMP_STRIDE
    nsel = t // SEL_BLOCK
    x2 = kv6[0:2].reshape(2, g, nc, CMP_STRIDE * dh)
    pos2 = cmp_pos.reshape(2, 2, CMP_STRIDE * dh)
    cmp_out = compress(x2, pos2, cmp_w1.astype(BF16), cmp_w2.astype(BF16))
    imp_mat_t = _importance_matrix(nc, nsel).T
    nq_r = t // (CMP_TQ * CMP_RANGES)
    assert t % (CMP_TQ * CMP_RANGES) == 0 and nc % CMP_RANGES == 0 and nsel % (8 * CMP_RANGES) == 0
    cmp_ranges = []
    for rg in range(CMP_RANGES):
        nc_r, nsel_r = (rg + 1) * nc // CMP_RANGES, (rg + 1) * nsel // CMP_RANGES
        vbd_t, kbd = _block_diag_kv(cmp_out[1][:, :nc_r], cmp_out[0][:, :nc_r])
        cmp_ranges.append((kbd, vbd_t, imp_mat_t[:nsel_r, :nc_r]))
    kt = FLASH_KT
    tiles = lambda a: a.reshape(g, t // kt, kt, dh)
    rep_k = lambda a: jnp.tile(tiles(a), (1, 1, 1, NSA_REP)).astype(BF16)
    rep_t = lambda a: jnp.swapaxes(tiles(a), 2, 3).astype(BF16)
    k_slc, v_slc_t = rep_k(kv6[2]), rep_t(kv6[3])
    k_win, v_win_t = rep_k(kv6[4]), rep_t(kv6[5])
    expand = _gate_expand()

    for i in range(n_a, depth):
        j = i - n_a
        qw = g * NSA_GW
        w_qg = nsa_w_qg[j]
        q_t = norm_matmul_t(h, mix_pre_norm[i], w_qg[:, :qw].T.astype(BF16))
        gate_logits = norm_matmul(h, mix_pre_norm[i], _pad_cols(w_qg[:, qw:], LANES).astype(BF16))
        cmp_res = (jnp.zeros((t, qw), F32), jnp.zeros((g, nsel, t), F32))
        for rg, (kbd, vbd_t, imp_r) in enumerate(cmp_ranges):
            cmp_res = cmp_topk(q_t, kbd, vbd_t, imp_r, nsel, min(SEL_TOPK, nsel), rg * nq_r, nq_r, CMP_TQ,
                               prev=cmp_res)
        o_c, sel_t = cmp_res
        o_s = flash_branch(q_t, k_slc, v_slc_t, sel_t, tq=min(FLASH_TQ, t), kt=kt)
        o_w = window_branch(q_t, k_win, v_win_t, tq=WINDOW_TQ)
        h = nsa_out(o_c, o_s, o_w, gate_logits, expand, nsa_w_o[j].astype(BF16), mix_post_norm[i], h)
        h = channel_and_ple(h, i)
    return h[None]
```

```python
import functools

import numpy as np
import jax
import jax.numpy as jnp
from jax import lax
from jax.experimental import pallas as pl
from jax.experimental.pallas import tpu as pltpu

F32 = jnp.float32
BF16 = jnp.bfloat16

NORM_EPS = 1e-6
L2_EPS = 1e-6
GDN_QK_HEADS = 8
GDN_V_HEADS = 16
GDN_HEAD_DIM = 128
GDN_CONV = 4
GDN_CHUNK = 64
GDN_GROUP = 256
GDN_SCAN_HEADS = 8
GDN_CONV_HEADS = 4
NSA_GROUPS = 4
NSA_REP = 4
NSA_HEAD_DIM = 64
NSA_GW = NSA_REP * NSA_HEAD_DIM
CMP_BLOCK = 32
CMP_STRIDE = 16
SEL_BLOCK = 64
SEL_TOPK = 16
WINDOW = 512
FORCED_SCORE = 1e4
LANES = 128
NEG_INIT = -(2.0 ** 100)
LOG2_E = 1.4426950408889634
FLASH_TQ = 2048
FLASH_KT = 256
WINDOW_TQ = 512
FLASH_SUB = 256
FLASH_SUM_ROWS = 16
CMP_ROWS = 256
CMP_TQ = 512
CMP_SUB = 256
CMP_RANGES = 4

V7X_VMEM_BYTES = 64 * 1024 * 1024
VMEM_LIMIT = V7X_VMEM_BYTES * 7 // 8
FFN_VMEM_LIMIT = V7X_VMEM_BYTES * 15 // 16


def _cparams(sem, vmem_limit=VMEM_LIMIT):
    return pltpu.CompilerParams(dimension_semantics=sem, vmem_limit_bytes=vmem_limit)


def _rms(x, gain):
    return x * lax.rsqrt(jnp.mean(x * x, axis=-1, keepdims=True) + NORM_EPS) * gain


def _mm(a, b):
    return jnp.dot(a.astype(BF16), b.astype(BF16), preferred_element_type=F32)


def _sigmoid(x):
    return 1.0 / (1.0 + jnp.exp(-x))


def _div_pow2(x, d):
    shift = d.bit_length() - 1
    assert d == 1 << shift
    return jnp.right_shift(x, shift)


def _split3(x):
    a = x.astype(BF16)
    r = x - a.astype(F32)
    b = r.astype(BF16)
    c = (r - b.astype(F32)).astype(BF16)
    return a, b, c


def _norm_matmul_kernel(x_ref, g_ref, w_ref, o_ref, xn_ref):
    @pl.when(pl.program_id(1) == 0)
    def _():
        xn_ref[...] = _rms(x_ref[...], g_ref[...]).astype(BF16)

    o_ref[...] = jnp.dot(xn_ref[...], w_ref[...], preferred_element_type=F32)


def _norm_matmul_t_kernel(x_ref, g_ref, wt_ref, o_ref, xn_ref):
    @pl.when(pl.program_id(1) == 0)
    def _():
        xn_ref[...] = _rms(x_ref[...], g_ref[...]).astype(BF16)

    o_ref[...] = lax.dot_general(wt_ref[...], xn_ref[...], (((1,), (1,)), ((), ())), preferred_element_type=F32)


def norm_matmul_t(h, gain, w_t, tm=2048, tn=1024):
    t, d = h.shape
    n = w_t.shape[0]
    tn = min(tn, n)
    assert t % tm == 0 and n % tn == 0
    return pl.pallas_call(
        _norm_matmul_t_kernel,
        grid=(t // tm, n // tn),
        in_specs=[
            pl.BlockSpec((tm, d), lambda i, j: (i, 0)),
            pl.BlockSpec((1, d), lambda i, j: (0, 0)),
            pl.BlockSpec((tn, d), lambda i, j: (j, 0)),
        ],
        out_specs=pl.BlockSpec((tn, tm), lambda i, j: (j, i)),
        out_shape=jax.ShapeDtypeStruct((n, t), F32),
        scratch_shapes=[pltpu.VMEM((tm, d), BF16)],
        compiler_params=_cparams(("parallel", "arbitrary")),
        name="norm_matmul_t",
    )(h, gain.reshape(1, d), w_t)


def norm_matmul(h, gain, w, tm=2048, tn=1024, n=None):
    t, d = h.shape
    n = w.shape[1] if n is None else n
    tn = min(tn, n)
    assert t % tm == 0 and n % tn == 0
    return pl.pallas_call(
        _norm_matmul_kernel,
        grid=(t // tm, n // tn),
        in_specs=[
            pl.BlockSpec((tm, d), lambda i, j: (i, 0)),
            pl.BlockSpec((1, d), lambda i, j: (0, 0)),
            pl.BlockSpec((d, tn), lambda i, j: (0, j)),
        ],
        out_specs=pl.BlockSpec((tm, tn), lambda i, j: (i, j)),
        out_shape=jax.ShapeDtypeStruct((t, n), F32),
        scratch_shapes=[pltpu.VMEM((tm, d), BF16)],
        compiler_params=_cparams(("parallel", "arbitrary")),
        name="norm_matmul",
    )(h, gain.reshape(1, d), w)


def _ffn_ple_kernel(h_ref, gpre_ref, wg_ref, wu_ref, wo_ref, gpost_ref, p_ref, wple_ref, wgt_ref,
                    o_ref, xn_ref, acc_ref, *, nf):
    f = pl.program_id(1)

    @pl.when(f == 0)
    def _():
        xn_ref[...] = _rms(h_ref[...], gpre_ref[...]).astype(BF16)
        acc_ref[...] = jnp.zeros_like(acc_ref)

    xn = xn_ref[...]
    tf = wg_ref.shape[1]
    half = (tf // LANES // 2) * LANES
    for a, b in ((0, half), (half, tf)):
        gate = jnp.dot(xn, wg_ref[:, a:b], preferred_element_type=F32)
        up = jnp.dot(xn, wu_ref[:, a:b], preferred_element_type=F32)
        act = gate * _sigmoid(gate) * up
        acc_ref[...] += jnp.dot(act.astype(BF16), wo_ref[a:b, :], preferred_element_type=F32)

    @pl.when(f == nf - 1)
    def _():
        h2 = h_ref[...] + _rms(acc_ref[...], gpost_ref[...])
        emb = jnp.dot(p_ref[...].astype(BF16), wple_ref[...], preferred_element_type=F32)
        gt = _sigmoid(jnp.dot(h2.astype(BF16), wgt_ref[...], preferred_element_type=F32))
        o_ref[...] = h2 + emb * gt


def ffn_ple(h, gpre, w_in, w_out, gpost, p, w_ple, w_plegate, tm=1024, tf=1408):
    t, d = h.shape
    fh = w_out.shape[0]
    pd = p.shape[1]
    assert t % tm == 0 and fh % tf == 0 and w_in.shape[1] == 2 * fh
    nf = fh // tf
    return pl.pallas_call(
        functools.partial(_ffn_ple_kernel, nf=nf),
        grid=(t // tm, nf),
        in_specs=[
            pl.BlockSpec((tm, d), lambda i, f: (i, 0)),
            pl.BlockSpec((1, d), lambda i, f: (0, 0)),
            pl.BlockSpec((d, tf), lambda i, f: (0, f)),
            pl.BlockSpec((d, tf), lambda i, f: (0, nf + f)),
            pl.BlockSpec((tf, d), lambda i, f: (f, 0)),
            pl.BlockSpec((1, d), lambda i, f: (0, 0)),
            pl.BlockSpec((tm, pd), lambda i, f: (i, 0)),
            pl.BlockSpec((pd, d), lambda i, f: (0, 0), pipeline_mode=pl.Buffered(1)),
            pl.BlockSpec((d, d), lambda i, f: (0, 0), pipeline_mode=pl.Buffered(1)),
        ],
        out_specs=pl.BlockSpec((tm, d), lambda i, f: (i, 0)),
        out_shape=jax.ShapeDtypeStruct((t, d), F32),
        scratch_shapes=[pltpu.VMEM((tm, d), BF16), pltpu.VMEM((tm, d), F32)],
        compiler_params=_cparams(("parallel", "arbitrary"), FFN_VMEM_LIMIT),
        name="ffn_ple",
    )(h, gpre.reshape(1, d), w_in, w_in, w_out, gpost.reshape(1, d), p, w_ple, w_plegate)


def _gdn_conv_kernel(x_ref, halo_ref, w_ref, o_ref, *, tm):
    c = pl.program_id(0)
    i = pl.program_id(1)
    x = x_ref[...]
    halo = jnp.where(i > 0, halo_ref[...], 0.0)
    ext = jnp.concatenate([halo, x], axis=0)
    w = w_ref[...]
    y = x * w[GDN_CONV - 1:GDN_CONV, :]
    for k in range(1, GDN_CONV):
        shifted = pltpu.roll(ext, k, axis=0)[8:8 + tm]
        y = y + shifted * w[GDN_CONV - 1 - k:GDN_CONV - k, :]
    y = y * _sigmoid(y)
    for hd in range(GDN_CONV_HEADS):
        head = c * GDN_CONV_HEADS + hd
        seg = y[:, hd * LANES:(hd + 1) * LANES]
        normed = seg * lax.rsqrt(jnp.sum(seg * seg, axis=-1, keepdims=True) + L2_EPS)
        q_scale = jnp.where(head < GDN_QK_HEADS, GDN_HEAD_DIM ** -0.5, 1.0)
        o_ref[hd] = jnp.where(head < 2 * GDN_QK_HEADS, normed * q_scale, seg)


def gdn_conv(proj, conv_w, tm=1024):
    t = proj.shape[0]
    n_tiles = conv_w.shape[1] // LANES
    cw = GDN_CONV_HEADS
    assert t % tm == 0 and n_tiles % cw == 0
    return pl.pallas_call(
        functools.partial(_gdn_conv_kernel, tm=tm),
        grid=(n_tiles // cw, t // tm),
        in_specs=[
            pl.BlockSpec((tm, cw * LANES), lambda c, i: (i, c)),
            pl.BlockSpec((8, cw * LANES), lambda c, i: (jnp.maximum(i * (tm // 8) - 1, 0), c)),
            pl.BlockSpec((GDN_CONV, cw * LANES), lambda c, i: (0, c)),
        ],
        out_specs=pl.BlockSpec((cw, tm, LANES), lambda c, i: (c, i, 0)),
        out_shape=jax.ShapeDtypeStruct((n_tiles, t, LANES), F32),
        compiler_params=_cparams(("parallel", "parallel")),
        name="gdn_conv",
    )(proj, proj, conv_w)


def _gdn_gate_kernel(x_ref, alog_ref, dtb_ref, lc_ref, lf_ref, o_ref):
    x = x_ref[...]
    lane = lax.broadcasted_iota(jnp.int32, x.shape, 1)
    beta = _sigmoid(x)
    z = x + dtb_ref[...]
    softplus = jnp.maximum(z, 0.0) + jnp.log(1.0 + jnp.exp(-jnp.abs(z)))
    g = -jnp.exp(alog_ref[...]) * softplus
    gcum = jnp.zeros_like(x)
    gtot = jnp.zeros_like(x)
    for piece in _split3(g):
        gcum = gcum + jnp.dot(lc_ref[...], piece, preferred_element_type=F32)
        gtot = gtot + jnp.dot(lf_ref[...], piece, preferred_element_type=F32)
    o_ref[...] = jnp.where(lane < GDN_V_HEADS, beta, jnp.where(lane < 2 * GDN_V_HEADS, gcum, gtot))


def gdn_gates(ba, alog_vec, dtb_vec):
    t = ba.shape[0]
    tm = GDN_GROUP
    r = np.arange(tm)
    same = (r[:, None] // GDN_CHUNK) == (r[None, :] // GDN_CHUNK)
    lc = jnp.asarray(same & (r[None, :] <= r[:, None]), BF16)
    lf = jnp.asarray(same, BF16)
    return pl.pallas_call(
        _gdn_gate_kernel,
        grid=(t // tm,),
        in_specs=[
            pl.BlockSpec((tm, LANES), lambda i: (i, 0)),
            pl.BlockSpec((1, LANES), lambda i: (0, 0)),
            pl.BlockSpec((1, LANES), lambda i: (0, 0)),
            pl.BlockSpec((tm, tm), lambda i: (0, 0)),
            pl.BlockSpec((tm, tm), lambda i: (0, 0)),
        ],
        out_specs=pl.BlockSpec((tm, LANES), lambda i: (i, 0)),
        out_shape=jax.ShapeDtypeStruct((t, LANES), F32),
        compiler_params=_cparams(("parallel",)),
        name="gdn_gates",
    )(ba, alog_vec, dtb_vec, lc, lf)


def _gdn_scan_kernel(q_ref, k_ref, kt_ref, v_ref, slab_ref, slabt_ref, o_ref, s_ref):
    n = pl.program_id(1)

    @pl.when(n == 0)
    def _():
        s_ref[...] = jnp.zeros_like(s_ref)

    L = GDN_GROUP
    C = GDN_CHUNK
    D = GDN_HEAD_DIM
    slab = slab_ref[...]
    slabt = slabt_ref[...]
    row = lax.broadcasted_iota(jnp.int32, (L, L), 0)
    col = lax.broadcasted_iota(jnp.int32, (L, L), 1)
    same = _div_pow2(row, C) == _div_pow2(col, C)
    causal = same & (col <= row)
    strict = same & (col < row)
    eye = (row == col).astype(F32)
    lane = lax.broadcasted_iota(jnp.int32, (L, LANES), 1)
    sub = lax.broadcasted_iota(jnp.int32, (LANES, L), 0)

    def column(idx):
        return jnp.sum(jnp.where(lane == idx, slab, 0.0), axis=1, keepdims=True)

    def rowvec(idx):
        return jnp.sum(jnp.where(sub == idx, slabt, 0.0), axis=0, keepdims=True)

    colk = lax.broadcasted_iota(jnp.int32, (D, L), 1)
    slots = range(2 * GDN_SCAN_HEADS)

    q = [q_ref[hq] for hq in range(GDN_SCAN_HEADS)]
    k = [k_ref[hq] for hq in range(GDN_SCAN_HEADS)]
    kt = [kt_ref[hq] for hq in range(GDN_SCAN_HEADS)]
    kk = [_mm(k[hq], kt[hq]) for hq in range(GDN_SCAN_HEADS)]
    qk = [_mm(q[hq], kt[hq]) for hq in range(GDN_SCAN_HEADS)]
    beta_c, gc_c, gt_c, decay, bp, inv, kdt = [], [], [], [], [], [], []
    for slot in slots:
        hq = slot // 2
        hv = 2 * (pl.program_id(0) * GDN_SCAN_HEADS + hq) + slot % 2
        beta_c.append(column(hv))
        gc_c.append(column(GDN_V_HEADS + hv))
        gt_c.append(column(2 * GDN_V_HEADS + hv))
        gc_r = rowvec(GDN_V_HEADS + hv)
        gt_r = rowvec(2 * GDN_V_HEADS + hv)
        decay.append(jnp.where(causal, jnp.exp(jnp.where(causal, gc_c[slot] - gc_r, 0.0)), 0.0))
        bp.append(jnp.where(strict, -(kk[hq] * beta_c[slot]) * decay[slot], 0.0))
        inv.append(eye + bp[slot])
        kdt.append(kt[hq] * jnp.exp(gt_r - gc_r))
    for _ in range(5):
        bp = [_mm(bp[slot], bp[slot]) for slot in slots]
        inv = [inv[slot] + _mm(inv[slot], bp[slot]) for slot in slots]
    u, w, qkm, q_dec = [], [], [], []
    for slot in slots:
        hq = slot // 2
        egc = jnp.exp(gc_c[slot])
        rhs = jnp.concatenate([v_ref[slot] * beta_c[slot], k[hq] * (beta_c[slot] * egc)], axis=1)
        sol = _mm(inv[slot], rhs)
        u.append(sol[:, :D])
        w.append(sol[:, D:])
        qkm.append(jnp.where(causal, qk[hq] * decay[slot], 0.0))
        q_dec.append(q[hq] * egc)
    state = [s_ref[slot] for slot in slots]
    v_done = [[] for _ in slots]
    for c in range(L // C):
        lo, hi = c * C, (c + 1) * C
        r = [_mm(jnp.concatenate([w[slot][lo:hi], q_dec[slot][lo:hi]], axis=0), state[slot]) for slot in slots]
        for slot in slots:
            v_done[slot].append(u[slot][lo:hi] - r[slot][:C])
            v_all = jnp.concatenate(v_done[slot] + [jnp.zeros((L - hi, D), F32)] * (hi < L), axis=0)
            kdt_c = jnp.where((colk >= lo) & (colk < hi), kdt[slot], 0.0)
            both = _mm(jnp.concatenate([qkm[slot][lo:hi, :], kdt_c], axis=0), v_all)
            o_ref[lo:hi, slot * D:(slot + 1) * D] = r[slot][C:] + both[:C]
            state[slot] = state[slot] * jnp.exp(gt_c[slot][lo:lo + 1, :]) + both[C:]
    for slot in slots:
        s_ref[slot] = state[slot]


def gdn_scan(qkv_hm, kt_hm, slab, slabt):
    t = qkv_hm.shape[1]
    L = GDN_GROUP
    D = GDN_HEAD_DIM
    hq = GDN_SCAN_HEADS
    assert t % L == 0 and GDN_QK_HEADS % hq == 0
    q_blocks = GDN_QK_HEADS // hq
    return pl.pallas_call(
        _gdn_scan_kernel,
        grid=(q_blocks, t // L),
        in_specs=[
            pl.BlockSpec((hq, L, D), lambda j, n: (j, n, 0)),
            pl.BlockSpec((hq, L, D), lambda j, n: (q_blocks + j, n, 0)),
            pl.BlockSpec((hq, D, L), lambda j, n: (j, 0, n)),
            pl.BlockSpec((2 * hq, L, D), lambda j, n: (q_blocks + j, n, 0)),
            pl.BlockSpec((L, LANES), lambda j, n: (n, 0)),
            pl.BlockSpec((LANES, L), lambda j, n: (0, n)),
        ],
        out_specs=pl.BlockSpec((L, 2 * hq * D), lambda j, n: (n, j)),
        out_shape=jax.ShapeDtypeStruct((t, GDN_V_HEADS * D), F32),
        scratch_shapes=[pltpu.VMEM((2 * hq, D, D), F32)],
        compiler_params=_cparams(("parallel", "arbitrary")),
        name="gdn_scan",
    )(qkv_hm, qkv_hm, kt_hm, qkv_hm, slab, slabt)


def _gdn_out_kernel(o_ref, z_ref, onorm_ref, w_ref, gpost_ref, h_ref, out_ref):
    o = o_ref[...]
    z = z_ref[...]
    parts = []
    for hd in range(GDN_V_HEADS):
        seg = o[:, hd * GDN_HEAD_DIM:(hd + 1) * GDN_HEAD_DIM]
        parts.append(seg * lax.rsqrt(jnp.mean(seg * seg, axis=-1, keepdims=True) + NORM_EPS))
    gated = jnp.concatenate(parts, axis=1) * onorm_ref[...] * (z * _sigmoid(z))
    mix = jnp.dot(gated.astype(BF16), w_ref[...], preferred_element_type=F32)
    out_ref[...] = h_ref[...] + _rms(mix, gpost_ref[...])


def gdn_out(o, proj, onorm_tiled, w_out, gpost, h, tm=512):
    t, vw = o.shape
    d = h.shape[1]
    z_blk = (proj.shape[1] - vw) // vw
    assert proj.shape[1] % vw == 0 and t % tm == 0
    return pl.pallas_call(
        _gdn_out_kernel,
        grid=(t // tm,),
        in_specs=[
            pl.BlockSpec((tm, vw), lambda i: (i, 0)),
            pl.BlockSpec((tm, vw), lambda i: (i, z_blk)),
            pl.BlockSpec((1, vw), lambda i: (0, 0)),
            pl.BlockSpec((vw, d), lambda i: (0, 0)),
            pl.BlockSpec((1, d), lambda i: (0, 0)),
            pl.BlockSpec((tm, d), lambda i: (i, 0)),
        ],
        out_specs=pl.BlockSpec((tm, d), lambda i: (i, 0)),
        out_shape=jax.ShapeDtypeStruct((t, d), F32),
        compiler_params=_cparams(("parallel",)),
        name="gdn_out",
    )(o, proj, onorm_tiled, w_out, gpost.reshape(1, d), h)


def _compress_kernel(x_ref, pos_ref, w1_ref, w2_ref, o_ref):
    x = x_ref[...]
    pos = pos_ref[...]
    nc, half = x.shape
    w1 = w1_ref[...]
    first = _mm(x + pos[0:1, :], w1[:half])
    second = _mm(x + pos[1:2, :], w1[half:])
    hid = first + pltpu.roll(second, nc - 1, axis=0)
    hid = hid * _sigmoid(hid)
    out = jnp.dot(hid.astype(BF16), w2_ref[...], preferred_element_type=F32)
    rowi = lax.broadcasted_iota(jnp.int32, out.shape, 0)
    o_ref[...] = jnp.where(rowi < nc - 1, out, 0.0)


def compress(x2, pos2, w1, w2):
    _, g, nc, wdt = x2.shape
    hid = w1.shape[2]
    dh = w2.shape[2]
    return pl.pallas_call(
        _compress_kernel,
        grid=(2, g),
        in_specs=[
            pl.BlockSpec((None, None, nc, wdt), lambda b, gi: (b, gi, 0, 0)),
            pl.BlockSpec((None, 2, wdt), lambda b, gi: (b, 0, 0)),
            pl.BlockSpec((None, 2 * wdt, hid), lambda b, gi: (b, 0, 0)),
            pl.BlockSpec((None, hid, dh), lambda b, gi: (b, 0, 0)),
        ],
        out_specs=pl.BlockSpec((None, None, nc, dh), lambda b, gi: (b, gi, 0, 0)),
        out_shape=jax.ShapeDtypeStruct((2, g, nc, dh), F32),
        compiler_params=_cparams(("parallel", "parallel")),
        name="nsa_compress",
    )(x2, pos2, w1, w2)


def _cmp_topk_kernel(q_ref, kbd_ref, vt_ref, m_ref, *refs, tq, nc, nsel, topk, q0):
    oc_ref, sel_ref, s_ref, p_ref, psum_ref = refs[-5:]
    i = q0 + pl.program_id(1)
    dh = NSA_HEAD_DIM
    rows_per = min(nc, CMP_ROWS)
    q = (q_ref[...] * ((dh ** -0.5) * LOG2_E)).astype(BF16)
    subs = [slice(c0, c0 + CMP_SUB) for c0 in range(0, tq, CMP_SUB)]

    def scores(cols):
        s_ref[:, cols] = jnp.dot(kbd_ref[...], q[:, cols], preferred_element_type=F32)

    def softmax(cols):
        for c0 in range(cols.start, cols.stop, LANES):
            lanes = slice(c0, c0 + LANES)
            tpos = i * tq + c0 + lax.broadcasted_iota(jnp.int32, (rows_per, LANES), 1)
            cblk0 = lax.broadcasted_iota(jnp.int32, (rows_per, LANES), 0)
            masks = [None if CMP_STRIDE * ((c + 1) * rows_per - 1) + CMP_BLOCK - 1 <= q0 * tq
                     else (CMP_STRIDE * (cblk0 + c * rows_per) + CMP_BLOCK - 1) <= tpos
                     for c in range(nc // rows_per)]
            visible = lambda rows, mask: s_ref[rows, lanes] if mask is None else jnp.where(mask, s_ref[rows, lanes],
                                                                                            -jnp.inf)
            for r in range(NSA_REP):
                pieces = [slice(r * nc + c * rows_per, r * nc + (c + 1) * rows_per) for c in range(nc // rows_per)]
                m = jnp.full((1, LANES), -jnp.inf, F32)
                for rows, mask in zip(pieces, masks):
                    m = jnp.maximum(m, jnp.max(visible(rows, mask), axis=0, keepdims=True))
                m = jnp.where(m > -jnp.inf, m, 0.0)
                total = jnp.zeros((1, LANES), F32)
                for rows, mask in zip(pieces, masks):
                    e = jnp.exp2(visible(rows, mask) - m)
                    s_ref[rows, lanes] = e
                    total = total + jnp.sum(e, axis=0, keepdims=True)
                inv = 1.0 / jnp.maximum(total, 1e-30)
                for c, rows in enumerate(pieces):
                    p = s_ref[rows, lanes] * inv
                    p_ref[rows, lanes] = p.astype(BF16)
                    prow = slice(c * rows_per, (c + 1) * rows_per)
                    psum_ref[prow, lanes] = p if r == 0 else psum_ref[prow, lanes] + p

    def outputs(cols):
        oc_t = jnp.dot(vt_ref[...], p_ref[:, cols], preferred_element_type=F32)
        imp = jnp.zeros((nsel, CMP_SUB), F32)
        for piece in _split3(psum_ref[:, cols]):
            imp = imp + jnp.dot(m_ref[...], piece, preferred_element_type=F32)
        return oc_t, imp

    def select(cols, oc_t, imp):
        for c0 in range(0, CMP_SUB, LANES):
            local = slice(c0, c0 + LANES)
            lanes = slice(cols.start + c0, cols.start + c0 + LANES)
            oc_ref[lanes, :] = oc_t[:, local].T
            t1 = i * tq + cols.start + c0 + lax.broadcasted_iota(jnp.int32, (nsel, LANES), 1)
            blk = lax.broadcasted_iota(jnp.int32, (nsel, LANES), 0)
            cur = _div_pow2(t1, SEL_BLOCK)
            forced = (blk == 0) | (blk == cur) | (blk == cur - 1)
            valid = blk * SEL_BLOCK <= t1
            score = jnp.where(valid, jnp.where(forced, FORCED_SCORE, imp[:, local]), -jnp.inf)
            blkf = blk.astype(F32)
            work = score
            for _ in range(topk):
                mx = jnp.max(work, axis=0, keepdims=True)
                first = jnp.min(jnp.where(work == mx, blkf, float(nsel)), axis=0, keepdims=True)
                work = jnp.where(blkf == first, -jnp.inf, work)
            sel_ref[:nsel, lanes] = jnp.where((score > -jnp.inf) & (work == -jnp.inf), 1.0, 0.0).astype(sel_ref.dtype)
            if sel_ref.shape[0] > nsel:
                sel_ref[nsel:, lanes] = jnp.zeros((sel_ref.shape[0] - nsel, LANES), sel_ref.dtype)

    for cols in subs:
        scores(cols)
    results = []
    for cols in subs:
        softmax(cols)
        results.append(outputs(cols))
    for cols, (oc_t, imp) in zip(subs, results):
        select(cols, oc_t, imp)


def cmp_topk(q_t, kbd, v_t, imp_mat_t, nsel_all, topk, q0, nq, tq, prev):
    qw, t = q_t.shape
    g = kbd.shape[0]
    nc = kbd.shape[1] // NSA_REP
    nsel = imp_mat_t.shape[0]
    assert nc % min(nc, CMP_ROWS) == 0
    in_specs = [
        pl.BlockSpec((NSA_GW, tq), lambda gi, i: (gi, q0 + i)),
        pl.BlockSpec((None, NSA_REP * nc, NSA_GW), lambda gi, i: (gi, 0, 0)),
        pl.BlockSpec((None, NSA_GW, NSA_REP * nc), lambda gi, i: (gi, 0, 0)),
        pl.BlockSpec((nsel, nc), lambda gi, i: (0, 0)),
    ]
    in_specs += [pl.BlockSpec(memory_space=pl.ANY)] * 2
    args = [q_t, kbd, v_t, imp_mat_t, *prev]
    aliases = {len(args) - 2: 0, len(args) - 1: 1}
    return pl.pallas_call(
        functools.partial(_cmp_topk_kernel, tq=tq, nc=nc, nsel=nsel, topk=topk, q0=q0),
        grid=(g, nq),
        in_specs=in_specs,
        out_specs=[
            pl.BlockSpec((tq, NSA_GW), lambda gi, i: (q0 + i, gi)),
            pl.BlockSpec((None, nsel_all, tq), lambda gi, i: (gi, 0, q0 + i)),
        ],
        out_shape=[
            jax.ShapeDtypeStruct((t, qw), F32),
            jax.ShapeDtypeStruct((g, nsel_all, t), F32),
        ],
        input_output_aliases=aliases,
        scratch_shapes=[
            pltpu.VMEM((NSA_REP * nc, tq), F32),
            pltpu.VMEM((NSA_REP * nc, tq), BF16),
            pltpu.VMEM((nc, tq), F32),
        ],
        compiler_params=_cparams(("parallel", "parallel")),
        name="nsa_cmp_topk",
    )(*args)


def _flash_kernel(q_ref, k_ref, v_ref, sel_ref, o_ref,
                  m_ref, l_ref, alpha_ref, acc_ref, s_ref, p_ref, bias_ref, kbd_ref, vt_ref, qs_ref, *, tq, kt):
    qi = pl.program_id(1)
    dh = NSA_HEAD_DIM
    n_sub = tq // FLASH_SUB
    n_blk = kt // SEL_BLOCK

    m_ref[...] = jnp.full_like(m_ref, NEG_INIT)
    l_ref[...] = jnp.zeros_like(l_ref)
    acc_ref[...] = jnp.zeros_like(acc_ref)
    qs_ref[...] = (q_ref[...] * ((dh ** -0.5) * LOG2_E)).astype(BF16)
    vt_ref[dh:, :] = jnp.ones((FLASH_SUM_ROWS, kt), BF16)

    def key_tile(ki, positional, first_sub=0):
        k4 = k_ref[ki]
        kseg = _div_pow2(lax.broadcasted_iota(jnp.int32, k4.shape, 1), dh)
        for r in range(NSA_REP):
            kbd_ref[r * kt:(r + 1) * kt, :] = jnp.where(kseg == r, k4, jnp.zeros_like(k4))
        vt_ref[:dh, :] = v_ref[ki]

        def scores(sub):
            c0 = sub * FLASH_SUB
            cols = slice(c0, c0 + FLASH_SUB)
            s_ref[:, cols] = jnp.dot(kbd_ref[...], qs_ref[:, cols],
                                     preferred_element_type=F32)
            if not positional:
                return
            for jb in range(n_blk):
                rows = slice(jb * SEL_BLOCK, (jb + 1) * SEL_BLOCK)
                picked = sel_ref[pl.ds(ki * n_blk + jb, 1), cols] > 0.5
                tpos = qi * tq + c0 + lax.broadcasted_iota(jnp.int32, (SEL_BLOCK, FLASH_SUB), 1)
                kpos = ki * kt + jb * SEL_BLOCK + lax.broadcasted_iota(jnp.int32, (SEL_BLOCK, FLASH_SUB), 0)
                bias_ref[rows, cols] = jnp.where(picked & (kpos <= tpos), 0.0, -jnp.inf)

        scores(first_sub)
        for sub in range(first_sub, n_sub):
            c0 = sub * FLASH_SUB
            cols = slice(c0, c0 + FLASH_SUB)
            if sub + 1 < n_sub:
                scores(sub + 1)
            if not positional:
                picked_sub = [sel_ref[pl.ds(ki * n_blk + jb, 1), cols] > 0.5 for jb in range(n_blk)]
            for ch in range(FLASH_SUB // LANES):
                lanes = slice(c0 + ch * LANES, c0 + (ch + 1) * LANES)
                if positional:
                    bias = bias_ref[:, lanes]
                else:
                    picked = [pk[:, ch * LANES:(ch + 1) * LANES] for pk in picked_sub]
                for r in range(NSA_REP):
                    if positional:
                        x = s_ref[r * kt:(r + 1) * kt, lanes] + bias
                    else:
                        x = jnp.concatenate(
                            [jnp.where(picked[jb], s_ref[r * kt + jb * SEL_BLOCK:r * kt + (jb + 1) * SEL_BLOCK, lanes],
                                       -jnp.inf) for jb in range(n_blk)], axis=0)
                    m_prev = m_ref[r:r + 1, lanes]
                    m_new = jnp.maximum(m_prev, jnp.max(x, axis=0, keepdims=True))
                    m_ref[r:r + 1, lanes] = m_new
                    alpha_ref[r:r + 1, lanes] = jnp.exp2(m_prev - m_new)
                    p_ref[r * kt:(r + 1) * kt, lanes] = jnp.exp2(x - m_new).astype(BF16)
            for r in range(NSA_REP):
                pv = jnp.dot(vt_ref[...], p_ref[r * kt:(r + 1) * kt, cols],
                             preferred_element_type=F32)
                hd = slice(r * dh, (r + 1) * dh)
                alpha = alpha_ref[r:r + 1, cols]
                acc_ref[hd, cols] = acc_ref[hd, cols] * alpha + pv[:dh]
                l_ref[r:r + 1, cols] = l_ref[r:r + 1, cols] * alpha + pv[dh:dh + 1]

    def before_diagonal(ki, carry):
        key_tile(ki, False)
        return carry

    diag = qi * (tq // kt)
    lax.fori_loop(0, diag, before_diagonal, 0)
    for d in range(tq // kt):
        def on_diagonal(ki, carry, first_sub=d * kt // FLASH_SUB):
            key_tile(ki, True, first_sub=first_sub)
            return carry
        lax.fori_loop(diag + d, diag + d + 1, on_diagonal, 0)

    for ch in range(tq // LANES):
        lanes = slice(ch * LANES, (ch + 1) * LANES)
        out_t = jnp.concatenate(
            [acc_ref[r * dh:(r + 1) * dh, lanes] / l_ref[r:r + 1, lanes] for r in range(NSA_REP)], axis=0)
        o_ref[lanes, :] = out_t.T


def flash_branch(q_t, k4, v_t, sel_t, tq, kt):
    qw, t = q_t.shape
    g, nk = k4.shape[:2]
    dh = v_t.shape[2]
    nsel = sel_t.shape[1]
    assert t % tq == 0 and tq % kt == 0 and kt % SEL_BLOCK == 0 and tq % FLASH_SUB == 0 and kt % FLASH_SUB == 0
    return pl.pallas_call(
        functools.partial(_flash_kernel, tq=tq, kt=kt),
        grid=(g, t // tq),
        in_specs=[
            pl.BlockSpec((NSA_GW, tq), lambda gi, i: (gi, i)),
            pl.BlockSpec((None, nk, kt, NSA_GW), lambda gi, i: (gi, 0, 0, 0), pipeline_mode=pl.Buffered(1)),
            pl.BlockSpec((None, nk, dh, kt), lambda gi, i: (gi, 0, 0, 0), pipeline_mode=pl.Buffered(1)),
            pl.BlockSpec((None, nsel, tq), lambda gi, i: (gi, 0, i)),
        ],
        out_specs=pl.BlockSpec((tq, NSA_GW), lambda gi, i: (i, gi)),
        scratch_shapes=[
            pltpu.VMEM((8, tq), F32),
            pltpu.VMEM((8, tq), F32),
            pltpu.VMEM((8, tq), F32),
            pltpu.VMEM((NSA_GW, tq), F32),
            pltpu.VMEM((NSA_REP * kt, tq), F32),
            pltpu.VMEM((NSA_REP * kt, tq), BF16),
            pltpu.VMEM((kt, tq), F32),
            pltpu.VMEM((NSA_REP * kt, NSA_GW), BF16),
            pltpu.VMEM((dh + FLASH_SUM_ROWS, kt), BF16),
            pltpu.VMEM((NSA_GW, tq), BF16),
        ],
        out_shape=jax.ShapeDtypeStruct((t, qw), F32),
        compiler_params=_cparams(("parallel", "arbitrary")),
        name="nsa_selected",
    )(q_t, k4, v_t, sel_t)


def _window_kernel(q_ref, k_ref, v_ref, o_ref, s_ref, p_ref, bias_ref, kbd_ref, vt_ref, *, tq, kt):
    dh = NSA_HEAD_DIM
    nw = WINDOW // kt + 1
    span = nw * kt
    n_tiles = tq // kt
    qs = (q_ref[...] * ((dh ** -0.5) * LOG2_E)).astype(BF16)

    def stage(h):
        i = pl.program_id(1) * n_tiles + h
        for w in range(nw):
            tile = i - (nw - 1) + w
            k4 = k_ref[jnp.maximum(tile, 0)]
            kseg = _div_pow2(lax.broadcasted_iota(jnp.int32, k4.shape, 1), dh)
            for r in range(NSA_REP):
                at = r * span + w * kt
                kbd_ref[h, at:at + kt, :] = jnp.where(kseg == r, k4, jnp.zeros_like(k4))
            vt_ref[h, :dh, w * kt:(w + 1) * kt] = v_ref[jnp.maximum(tile, 0)]
            tpos = i * kt + lax.broadcasted_iota(jnp.int32, (kt, kt), 1)
            kpos = tile * kt + lax.broadcasted_iota(jnp.int32, (kt, kt), 0)
            allowed = (kpos >= 0) & (kpos <= tpos) & (kpos > tpos - WINDOW)
            bias_ref[h, w * kt:(w + 1) * kt, :] = jnp.where(allowed, 0.0, -jnp.inf)
        vt_ref[h, dh:, :] = jnp.ones((FLASH_SUM_ROWS, span), BF16)

    def scores(h):
        s_ref[h] = jnp.dot(kbd_ref[h], qs[:, h * kt:(h + 1) * kt], preferred_element_type=F32)

    def softmax(h):
        for ch in range(kt // LANES):
            lanes = slice(ch * LANES, (ch + 1) * LANES)
            for r in range(NSA_REP):
                m = jnp.full((1, LANES), -jnp.inf, F32)
                for w in range(nw):
                    rows = slice(r * span + w * kt, r * span + (w + 1) * kt)
                    m = jnp.maximum(m, jnp.max(s_ref[h, rows, lanes] + bias_ref[h, w * kt:(w + 1) * kt, lanes],
                                               axis=0, keepdims=True))
                m = jnp.where(m > -jnp.inf, m, 0.0)
                for w in range(nw):
                    rows = slice(r * span + w * kt, r * span + (w + 1) * kt)
                    x = s_ref[h, rows, lanes] + bias_ref[h, w * kt:(w + 1) * kt, lanes]
                    p_ref[h, rows, lanes] = jnp.exp2(x - m).astype(BF16)

    def weighted(h):
        return [jnp.dot(vt_ref[h], p_ref[h, r * span:(r + 1) * span, :], preferred_element_type=F32)
                for r in range(NSA_REP)]

    def store(h, pv):
        for ch in range(kt // LANES):
            lanes = slice(ch * LANES, (ch + 1) * LANES)
            out_t = jnp.concatenate(
                [pv[r][:dh, lanes] / jnp.maximum(pv[r][dh:dh + 1, lanes], 1e-30) for r in range(NSA_REP)], axis=0)
            o_ref[h * kt + ch * LANES:h * kt + (ch + 1) * LANES, :] = out_t.T

    for h in range(n_tiles):
        stage(h)
        scores(h)
    pvs = []
    for h in range(n_tiles):
        softmax(h)
        pvs.append(weighted(h))
    for h in range(n_tiles):
        store(h, pvs[h])


def window_branch(q_t, k4, v_t, tq):
    qw, t = q_t.shape
    g, nk, kt, _ = k4.shape
    dh = v_t.shape[2]
    assert tq % kt == 0 and WINDOW % kt == 0 and t % tq == 0
    n_tiles = tq // kt
    span = (WINDOW // kt + 1) * kt
    return pl.pallas_call(
        functools.partial(_window_kernel, tq=tq, kt=kt),
        grid=(g, t // tq),
        in_specs=[
            pl.BlockSpec((NSA_GW, tq), lambda gi, i: (gi, i)),
            pl.BlockSpec((None, nk, kt, NSA_GW), lambda gi, i: (gi, 0, 0, 0)),
            pl.BlockSpec((None, nk, dh, kt), lambda gi, i: (gi, 0, 0, 0)),
        ],
        out_specs=pl.BlockSpec((tq, NSA_GW), lambda gi, i: (i, gi)),
        scratch_shapes=[
            pltpu.VMEM((n_tiles, NSA_REP * span, kt), F32),
            pltpu.VMEM((n_tiles, NSA_REP * span, kt), BF16),
            pltpu.VMEM((n_tiles, span, kt), F32),
            pltpu.VMEM((n_tiles, NSA_REP * span, NSA_GW), BF16),
            pltpu.VMEM((n_tiles, dh + FLASH_SUM_ROWS, span), BF16),
        ],
        out_shape=jax.ShapeDtypeStruct((t, qw), F32),
        compiler_params=_cparams(("parallel", "parallel")),
        name="nsa_window",
    )(q_t, k4, v_t)


def _nsa_out_kernel(oc_ref, os_ref, ow_ref, gl_ref, eg_ref, w_ref, gpost_ref, h_ref, out_ref):
    pieces = _split3(_sigmoid(gl_ref[...]))
    mixed = jnp.zeros(oc_ref.shape, F32)
    for b, br_ref in enumerate((oc_ref, os_ref, ow_ref)):
        gfull = jnp.zeros(oc_ref.shape, F32)
        for piece in pieces:
            gfull = gfull + jnp.dot(piece, eg_ref[b], preferred_element_type=F32)
        mixed = mixed + gfull * br_ref[...]
    mix = jnp.dot(mixed.astype(BF16), w_ref[...], preferred_element_type=F32)
    out_ref[...] = h_ref[...] + _rms(mix, gpost_ref[...])


def nsa_out(oc, osel, ow, gate_logits, expand, w_o, gpost, h, tm=512):
    t, qw = oc.shape
    d = h.shape[1]
    assert t % tm == 0
    row = lambda w: pl.BlockSpec((tm, w), lambda i: (i, 0))
    return pl.pallas_call(
        _nsa_out_kernel,
        grid=(t // tm,),
        in_specs=[
            row(qw), row(qw), row(qw), row(LANES),
            pl.BlockSpec((3, LANES, qw), lambda i: (0, 0, 0)),
            pl.BlockSpec((qw, d), lambda i: (0, 0)),
            pl.BlockSpec((1, d), lambda i: (0, 0)),
            row(d),
        ],
        out_specs=row(d),
        out_shape=jax.ShapeDtypeStruct((t, d), F32),
        compiler_params=_cparams(("parallel",)),
        name="nsa_out",
    )(oc, osel, ow, gate_logits, expand, w_o, gpost.reshape(1, d), h)


def _pad_cols(w, n):
    return jnp.pad(w, ((0, 0), (0, n - w.shape[1])))


def _importance_matrix(nc, nsel):
    r = SEL_BLOCK // CMP_STRIDE
    c = CMP_BLOCK // CMP_STRIDE
    mat = np.zeros((nc, nsel), np.float32)
    for kblk in range(nsel):
        for m in range(r):
            for n in range(c):
                j = r * kblk + m - n
                if 0 <= j < nc - 1:
                    mat[j, kblk] += 1.0
    return jnp.asarray(mat, BF16)


def _gate_expand():
    e = np.zeros((3, LANES, NSA_GROUPS * NSA_GW), np.float32)
    for head in range(NSA_GROUPS * NSA_REP):
        for b in range(3):
            e[b, head * 3 + b, head * NSA_HEAD_DIM:(head + 1) * NSA_HEAD_DIM] = 1.0
    return jnp.asarray(e, BF16)


def _block_diag_kv(a, b):
    g, nc, dh = a.shape
    eye = jnp.eye(NSA_REP, dtype=bool)
    a_t = jnp.swapaxes(a, 1, 2)
    a_bd = jnp.where(eye[None, :, None, :, None], a_t[:, None, :, None, :], 0.0)
    b_bd = jnp.where(eye[None, :, None, :, None], b[:, None, :, None, :], 0.0)
    return (a_bd.reshape(g, NSA_REP * dh, NSA_REP * nc).astype(BF16),
            b_bd.reshape(g, NSA_REP * nc, NSA_REP * dh).astype(BF16))


def kernel(x, p, mix_pre_norm, mix_post_norm, ffn_pre_norm, ffn_post_norm, gdn_w_in, gdn_conv_w, gdn_a_log,
           gdn_dt_bias, gdn_o_norm, gdn_w_out, kv_norm, kv_w, cmp_pos, cmp_w1, cmp_w2, nsa_w_qg, nsa_w_o,
           ffn_w_in, ffn_w_out, ple_w_in, ple_w_gate):
    depth = p.shape[0]
    n_a = gdn_w_in.shape[0]
    t = x.shape[1]
    h = x[0]
    conv_w_cols = gdn_conv_w.shape[2]
    vw = GDN_V_HEADS * GDN_HEAD_DIM
    main_w = conv_w_cols + vw

    def channel_and_ple(h, i):
        return ffn_ple(h, ffn_pre_norm[i], ffn_w_in[i].astype(BF16), ffn_w_out[i].astype(BF16),
                       ffn_post_norm[i], p[i, 0], ple_w_in[i].astype(BF16), ple_w_gate[i].astype(BF16))

    for i in range(n_a):
        w_in = gdn_w_in[i]
        w_beta = w_in[:, main_w:main_w + GDN_V_HEADS]
        w_a = w_in[:, main_w + GDN_V_HEADS:]
        w_small = _pad_cols(jnp.concatenate([w_beta, w_a, w_a], axis=1), LANES).astype(BF16)
        proj = norm_matmul(h, mix_pre_norm[i], w_in.astype(BF16), n=main_w)
        ba = norm_matmul(h, mix_pre_norm[i], w_small)
        pad_vec = lambda v: jnp.pad(v, (GDN_V_HEADS, LANES - 2 * GDN_V_HEADS))
        alog_vec = (pad_vec(gdn_a_log[i]) + jnp.pad(gdn_a_log[i], (2 * GDN_V_HEADS, LANES - 3 * GDN_V_HEADS)))
        dtb_vec = (pad_vec(gdn_dt_bias[i]) + jnp.pad(gdn_dt_bias[i], (2 * GDN_V_HEADS, LANES - 3 * GDN_V_HEADS)))
        slab = gdn_gates(ba, alog_vec.reshape(1, LANES), dtb_vec.reshape(1, LANES))
        qkv_hm = gdn_conv(proj, gdn_conv_w[i])
        kt_hm = jnp.swapaxes(qkv_hm[GDN_QK_HEADS:2 * GDN_QK_HEADS], 1, 2)
        o = gdn_scan(qkv_hm, kt_hm, slab, slab.T)
        onorm_tiled = jnp.tile(gdn_o_norm[i], GDN_V_HEADS).reshape(1, vw)
        h = gdn_out(o, proj, onorm_tiled, gdn_w_out[i].astype(BF16), mix_post_norm[i], h)
        h = channel_and_ple(h, i)

    g = NSA_GROUPS
    dh = NSA_HEAD_DIM
    kv = norm_matmul(h, kv_norm, kv_w.astype(BF16), tn=768)
    kv6 = jnp.transpose(kv.reshape(t, 6, g, dh), (1, 2, 0, 3))
    nc = t // CMP_STRIDE
    nsel = t // SEL_BLOCK
    x2 = kv6[0:2].reshape(2, g, nc, CMP_STRIDE * dh)
    pos2 = cmp_pos.reshape(2, 2, CMP_STRIDE * dh)
    cmp_out = compress(x2, pos2, cmp_w1.astype(BF16), cmp_w2.astype(BF16))
    imp_mat_t = _importance_matrix(nc, nsel).T
    nq_r = t // (CMP_TQ * CMP_RANGES)
    assert t % (CMP_TQ * CMP_RANGES) == 0 and nc % CMP_RANGES == 0 and nsel % (8 * CMP_RANGES) == 0
    cmp_ranges = []
    for rg in range(CMP_RANGES):
        nc_r, nsel_r = (rg + 1) * nc // CMP_RANGES, (rg + 1) * nsel // CMP_RANGES
        vbd_t, kbd = _block_diag_kv(cmp_out[1][:, :nc_r], cmp_out[0][:, :nc_r])
        cmp_ranges.append((kbd, vbd_t, imp_mat_t[:nsel_r, :nc_r]))
    kt = FLASH_KT
    tiles = lambda a: a.reshape(g, t // kt, kt, dh)
    rep_k = lambda a: jnp.tile(tiles(a), (1, 1, 1, NSA_REP)).astype(BF16)
    rep_t = lambda a: jnp.swapaxes(tiles(a), 2, 3).astype(BF16)
    k_slc, v_slc_t = rep_k(kv6[2]), rep_t(kv6[3])
    k_win, v_win_t = rep_k(kv6[4]), rep_t(kv6[5])
    expand = _gate_expand()

    for i in range(n_a, depth):
        j = i - n_a
        qw = g * NSA_GW
        w_qg = nsa_w_qg[j]
        q_t = norm_matmul_t(h, mix_pre_norm[i], w_qg[:, :qw].T.astype(BF16))
        gate_logits = norm_matmul(h, mix_pre_norm[i], _pad_cols(w_qg[:, qw:], LANES).astype(BF16))
        cmp_res = (jnp.zeros((t, qw), F32), jnp.zeros((g, nsel, t), F32))
        for rg, (kbd, vbd_t, imp_r) in enumerate(cmp_ranges):
            cmp_res = cmp_topk(q_t, kbd, vbd_t, imp_r, nsel, min(SEL_TOPK, nsel), rg * nq_r, nq_r, CMP_TQ,
                               prev=cmp_res)
        o_c, sel_t = cmp_res
        o_s = flash_branch(q_t, k_slc, v_slc_t, sel_t, tq=min(FLASH_TQ, t), kt=kt)
        o_w = window_branch(q_t, k_win, v_win_t, tq=WINDOW_TQ)
        h = nsa_out(o_c, o_s, o_w, gate_logits, expand, nsa_w_o[j].astype(BF16), mix_post_norm[i], h)
        h = channel_and_ple(h, i)
    return h[None]
```

```python
import functools
import math

import numpy as np
import jax
import jax.numpy as jnp
from jax import lax
from jax.experimental import pallas as pl
from jax.experimental.pallas import tpu as pltpu

F32 = jnp.float32
BF16 = jnp.bfloat16

NORM_EPS = 1e-6
L2_EPS = 1e-6
GDN_QK_HEADS = 8
GDN_V_HEADS = 16
GDN_HEAD_DIM = 128
GDN_CONV = 4
GDN_CHUNK = 64
GDN_GROUP = 256
GDN_SCAN_HEADS = 8
GDN_CONV_HEADS = 4
NSA_GROUPS = 4
NSA_REP = 4
NSA_HEAD_DIM = 64
NSA_GW = NSA_REP * NSA_HEAD_DIM
CMP_BLOCK = 32
CMP_STRIDE = 16
SEL_BLOCK = 64
SEL_TOPK = 16
WINDOW = 512
FORCED_SCORE = 1e4
LANES = 128
NEG_INIT = -(2.0 ** 100)
LOG2_E = 1.4426950408889634
FLASH_TQ = 2048
FLASH_KT = 256
WINDOW_TQ = 512
FLASH_SUB = 256
FLASH_SUM_ROWS = 16
CMP_ROWS = 128
CMP_TQ = 512
CMP_SUB = 256
CMP_RANGES = 8

V7X_VMEM_BYTES = 64 * 1024 * 1024
VMEM_LIMIT = V7X_VMEM_BYTES * 7 // 8
FFN_VMEM_LIMIT = V7X_VMEM_BYTES * 15 // 16


def _cparams(sem, vmem_limit=VMEM_LIMIT):
    return pltpu.CompilerParams(dimension_semantics=sem, vmem_limit_bytes=vmem_limit)


def _rms(x, gain):
    return x * lax.rsqrt(jnp.mean(x * x, axis=-1, keepdims=True) + NORM_EPS) * gain


def _mm(a, b):
    return jnp.dot(a.astype(BF16), b.astype(BF16), preferred_element_type=F32)


def _sigmoid(x):
    return 1.0 / (1.0 + jnp.exp(-x))


def _div_pow2(x, d):
    shift = d.bit_length() - 1
    assert d == 1 << shift
    return jnp.right_shift(x, shift)


def _split3(x):
    a = x.astype(BF16)
    r = x - a.astype(F32)
    b = r.astype(BF16)
    c = (r - b.astype(F32)).astype(BF16)
    return a, b, c


def _norm_matmul_kernel(x_ref, g_ref, w_ref, o_ref, xn_ref):
    @pl.when(pl.program_id(1) == 0)
    def _():
        xn_ref[...] = _rms(x_ref[...], g_ref[...]).astype(BF16)

    o_ref[...] = jnp.dot(xn_ref[...], w_ref[...], preferred_element_type=F32)


def _norm_matmul_t_kernel(x_ref, g_ref, wt_ref, o_ref, xn_ref):
    @pl.when(pl.program_id(1) == 0)
    def _():
        xn_ref[...] = _rms(x_ref[...], g_ref[...]).astype(BF16)

    o_ref[...] = lax.dot_general(wt_ref[...], xn_ref[...], (((1,), (1,)), ((), ())), preferred_element_type=F32)


def norm_matmul_t(h, gain, w_t, tm=2048, tn=1024):
    t, d = h.shape
    n = w_t.shape[0]
    tn = min(tn, n)
    assert t % tm == 0 and n % tn == 0
    return pl.pallas_call(
        _norm_matmul_t_kernel,
        grid=(t // tm, n // tn),
        in_specs=[
            pl.BlockSpec((tm, d), lambda i, j: (i, 0)),
            pl.BlockSpec((1, d), lambda i, j: (0, 0)),
            pl.BlockSpec((tn, d), lambda i, j: (j, 0)),
        ],
        out_specs=pl.BlockSpec((tn, tm), lambda i, j: (j, i)),
        out_shape=jax.ShapeDtypeStruct((n, t), F32),
        scratch_shapes=[pltpu.VMEM((tm, d), BF16)],
        compiler_params=_cparams(("parallel", "arbitrary")),
        name="norm_matmul_t",
    )(h, gain.reshape(1, d), w_t)


def norm_matmul(h, gain, w, tm=2048, tn=1024, n=None):
    t, d = h.shape
    n = w.shape[1] if n is None else n
    tn = min(tn, n)
    assert t % tm == 0 and n % tn == 0
    return pl.pallas_call(
        _norm_matmul_kernel,
        grid=(t // tm, n // tn),
        in_specs=[
            pl.BlockSpec((tm, d), lambda i, j: (i, 0)),
            pl.BlockSpec((1, d), lambda i, j: (0, 0)),
            pl.BlockSpec((d, tn), lambda i, j: (0, j)),
        ],
        out_specs=pl.BlockSpec((tm, tn), lambda i, j: (i, j)),
        out_shape=jax.ShapeDtypeStruct((t, n), F32),
        scratch_shapes=[pltpu.VMEM((tm, d), BF16)],
        compiler_params=_cparams(("parallel", "arbitrary")),
        name="norm_matmul",
    )(h, gain.reshape(1, d), w)


def _ffn_ple_kernel(h_ref, gpre_ref, wg_ref, wu_ref, wo_ref, gpost_ref, p_ref, wple_ref, wgt_ref,
                    o_ref, xn_ref, acc_ref, *, nf):
    f = pl.program_id(1)

    @pl.when(f == 0)
    def _():
        xn_ref[...] = _rms(h_ref[...], gpre_ref[...]).astype(BF16)
        acc_ref[...] = jnp.zeros_like(acc_ref)

    xn = xn_ref[...]
    tf = wg_ref.shape[1]
    half = (tf // LANES // 2) * LANES
    for a, b in ((0, half), (half, tf)):
        gate = jnp.dot(xn, wg_ref[:, a:b], preferred_element_type=F32)
        up = jnp.dot(xn, wu_ref[:, a:b], preferred_element_type=F32)
        act = gate * _sigmoid(gate) * up
        acc_ref[...] += jnp.dot(act.astype(BF16), wo_ref[a:b, :], preferred_element_type=F32)

    @pl.when(f == nf - 1)
    def _():
        h2 = h_ref[...] + _rms(acc_ref[...], gpost_ref[...])
        emb = jnp.dot(p_ref[...].astype(BF16), wple_ref[...], preferred_element_type=F32)
        gt = _sigmoid(jnp.dot(h2.astype(BF16), wgt_ref[...], preferred_element_type=F32))
        o_ref[...] = h2 + emb * gt


def ffn_ple(h, gpre, w_in, w_out, gpost, p, w_ple, w_plegate, tm=1024, tf=1408):
    t, d = h.shape
    fh = w_out.shape[0]
    pd = p.shape[1]
    assert t % tm == 0 and fh % tf == 0 and w_in.shape[1] == 2 * fh
    nf = fh // tf
    return pl.pallas_call(
        functools.partial(_ffn_ple_kernel, nf=nf),
        grid=(t // tm, nf),
        in_specs=[
            pl.BlockSpec((tm, d), lambda i, f: (i, 0)),
            pl.BlockSpec((1, d), lambda i, f: (0, 0)),
            pl.BlockSpec((d, tf), lambda i, f: (0, f)),
            pl.BlockSpec((d, tf), lambda i, f: (0, nf + f)),
            pl.BlockSpec((tf, d), lambda i, f: (f, 0)),
            pl.BlockSpec((1, d), lambda i, f: (0, 0)),
            pl.BlockSpec((tm, pd), lambda i, f: (i, 0)),
            pl.BlockSpec((pd, d), lambda i, f: (0, 0), pipeline_mode=pl.Buffered(1)),
            pl.BlockSpec((d, d), lambda i, f: (0, 0), pipeline_mode=pl.Buffered(1)),
        ],
        out_specs=pl.BlockSpec((tm, d), lambda i, f: (i, 0)),
        out_shape=jax.ShapeDtypeStruct((t, d), F32),
        scratch_shapes=[pltpu.VMEM((tm, d), BF16), pltpu.VMEM((tm, d), F32)],
        compiler_params=_cparams(("parallel", "arbitrary"), FFN_VMEM_LIMIT),
        name="ffn_ple",
    )(h, gpre.reshape(1, d), w_in, w_in, w_out, gpost.reshape(1, d), p, w_ple, w_plegate)


def _gdn_conv_kernel(x_ref, halo_ref, w_ref, o_ref, *, tm):
    c = pl.program_id(0)
    i = pl.program_id(1)
    x = x_ref[...]
    halo = jnp.where(i > 0, halo_ref[...], 0.0)
    ext = jnp.concatenate([halo, x], axis=0)
    w = w_ref[...]
    y = x * w[GDN_CONV - 1:GDN_CONV, :]
    for k in range(1, GDN_CONV):
        shifted = pltpu.roll(ext, k, axis=0)[8:8 + tm]
        y = y + shifted * w[GDN_CONV - 1 - k:GDN_CONV - k, :]
    y = y * _sigmoid(y)
    for hd in range(GDN_CONV_HEADS):
        head = c * GDN_CONV_HEADS + hd
        seg = y[:, hd * LANES:(hd + 1) * LANES]
        normed = seg * lax.rsqrt(jnp.sum(seg * seg, axis=-1, keepdims=True) + L2_EPS)
        q_scale = jnp.where(head < GDN_QK_HEADS, GDN_HEAD_DIM ** -0.5, 1.0)
        o_ref[hd] = jnp.where(head < 2 * GDN_QK_HEADS, normed * q_scale, seg)


def gdn_conv(proj, conv_w, tm=1024):
    t = proj.shape[0]
    n_tiles = conv_w.shape[1] // LANES
    cw = GDN_CONV_HEADS
    assert t % tm == 0 and n_tiles % cw == 0
    return pl.pallas_call(
        functools.partial(_gdn_conv_kernel, tm=tm),
        grid=(n_tiles // cw, t // tm),
        in_specs=[
            pl.BlockSpec((tm, cw * LANES), lambda c, i: (i, c)),
            pl.BlockSpec((8, cw * LANES), lambda c, i: (jnp.maximum(i * (tm // 8) - 1, 0), c)),
            pl.BlockSpec((GDN_CONV, cw * LANES), lambda c, i: (0, c)),
        ],
        out_specs=pl.BlockSpec((cw, tm, LANES), lambda c, i: (c, i, 0)),
        out_shape=jax.ShapeDtypeStruct((n_tiles, t, LANES), F32),
        compiler_params=_cparams(("parallel", "parallel")),
        name="gdn_conv",
    )(proj, proj, conv_w)


def _gdn_gate_kernel(x_ref, alog_ref, dtb_ref, lc_ref, lf_ref, o_ref):
    x = x_ref[...]
    lane = lax.broadcasted_iota(jnp.int32, x.shape, 1)
    beta = _sigmoid(x)
    z = x + dtb_ref[...]
    softplus = jnp.maximum(z, 0.0) + jnp.log(1.0 + jnp.exp(-jnp.abs(z)))
    g = -jnp.exp(alog_ref[...]) * softplus
    gcum = jnp.zeros_like(x)
    gtot = jnp.zeros_like(x)
    for piece in _split3(g):
        gcum = gcum + jnp.dot(lc_ref[...], piece, preferred_element_type=F32)
        gtot = gtot + jnp.dot(lf_ref[...], piece, preferred_element_type=F32)
    o_ref[...] = jnp.where(lane < GDN_V_HEADS, beta, jnp.where(lane < 2 * GDN_V_HEADS, gcum, gtot))


def gdn_gates(ba, alog_vec, dtb_vec):
    t = ba.shape[0]
    tm = GDN_GROUP
    r = np.arange(tm)
    same = (r[:, None] // GDN_CHUNK) == (r[None, :] // GDN_CHUNK)
    lc = jnp.asarray(same & (r[None, :] <= r[:, None]), BF16)
    lf = jnp.asarray(same, BF16)
    return pl.pallas_call(
        _gdn_gate_kernel,
        grid=(t // tm,),
        in_specs=[
            pl.BlockSpec((tm, LANES), lambda i: (i, 0)),
            pl.BlockSpec((1, LANES), lambda i: (0, 0)),
            pl.BlockSpec((1, LANES), lambda i: (0, 0)),
            pl.BlockSpec((tm, tm), lambda i: (0, 0)),
            pl.BlockSpec((tm, tm), lambda i: (0, 0)),
        ],
        out_specs=pl.BlockSpec((tm, LANES), lambda i: (i, 0)),
        out_shape=jax.ShapeDtypeStruct((t, LANES), F32),
        compiler_params=_cparams(("parallel",)),
        name="gdn_gates",
    )(ba, alog_vec, dtb_vec, lc, lf)


def _gdn_scan_kernel(q_ref, k_ref, kt_ref, v_ref, slab_ref, slabt_ref, o_ref, s_ref):
    n = pl.program_id(1)

    @pl.when(n == 0)
    def _():
        s_ref[...] = jnp.zeros_like(s_ref)

    L = GDN_GROUP
    C = GDN_CHUNK
    D = GDN_HEAD_DIM
    slab = slab_ref[...]
    slabt = slabt_ref[...]
    row = lax.broadcasted_iota(jnp.int32, (L, L), 0)
    col = lax.broadcasted_iota(jnp.int32, (L, L), 1)
    same = _div_pow2(row, C) == _div_pow2(col, C)
    causal = same & (col <= row)
    strict = same & (col < row)
    eye = (row == col).astype(F32)
    lane = lax.broadcasted_iota(jnp.int32, (L, LANES), 1)
    sub = lax.broadcasted_iota(jnp.int32, (LANES, L), 0)

    def column(idx):
        return jnp.sum(jnp.where(lane == idx, slab, 0.0), axis=1, keepdims=True)

    def rowvec(idx):
        return jnp.sum(jnp.where(sub == idx, slabt, 0.0), axis=0, keepdims=True)

    colk = lax.broadcasted_iota(jnp.int32, (D, L), 1)
    slots = range(2 * GDN_SCAN_HEADS)

    q = [q_ref[hq] for hq in range(GDN_SCAN_HEADS)]
    k = [k_ref[hq] for hq in range(GDN_SCAN_HEADS)]
    kt = [kt_ref[hq] for hq in range(GDN_SCAN_HEADS)]
    kk = [_mm(k[hq], kt[hq]) for hq in range(GDN_SCAN_HEADS)]
    qk = [_mm(q[hq], kt[hq]) for hq in range(GDN_SCAN_HEADS)]
    beta_c, gc_c, gt_c, decay, bp, inv, kdt = [], [], [], [], [], [], []
    for slot in slots:
        hq = slot // 2
        hv = 2 * (pl.program_id(0) * GDN_SCAN_HEADS + hq) + slot % 2
        beta_c.append(column(hv))
        gc_c.append(column(GDN_V_HEADS + hv))
        gt_c.append(column(2 * GDN_V_HEADS + hv))
        gc_r = rowvec(GDN_V_HEADS + hv)
        gt_r = rowvec(2 * GDN_V_HEADS + hv)
        decay.append(jnp.where(causal, jnp.exp(jnp.where(causal, gc_c[slot] - gc_r, 0.0)), 0.0))
        bp.append(jnp.where(strict, -(kk[hq] * beta_c[slot]) * decay[slot], 0.0))
        inv.append(eye + bp[slot])
        kdt.append(kt[hq] * jnp.exp(gt_r - gc_r))
    for _ in range(5):
        bp = [_mm(bp[slot], bp[slot]) for slot in slots]
        inv = [inv[slot] + _mm(inv[slot], bp[slot]) for slot in slots]
    u, w, qkm, q_dec = [], [], [], []
    for slot in slots:
        hq = slot // 2
        egc = jnp.exp(gc_c[slot])
        rhs = jnp.concatenate([v_ref[slot] * beta_c[slot], k[hq] * (beta_c[slot] * egc)], axis=1)
        sol = _mm(inv[slot], rhs)
        u.append(sol[:, :D])
        w.append(sol[:, D:])
        qkm.append(jnp.where(causal, qk[hq] * decay[slot], 0.0))
        q_dec.append(q[hq] * egc)
    state = [s_ref[slot] for slot in slots]
    v_done = [[] for _ in slots]
    for c in range(L // C):
        lo, hi = c * C, (c + 1) * C
        r = [_mm(jnp.concatenate([w[slot][lo:hi], q_dec[slot][lo:hi]], axis=0), state[slot]) for slot in slots]
        for slot in slots:
            v_done[slot].append(u[slot][lo:hi] - r[slot][:C])
            v_all = jnp.concatenate(v_done[slot] + [jnp.zeros((L - hi, D), F32)] * (hi < L), axis=0)
            kdt_c = jnp.where((colk >= lo) & (colk < hi), kdt[slot], 0.0)
            both = _mm(jnp.concatenate([qkm[slot][lo:hi, :], kdt_c], axis=0), v_all)
            o_ref[lo:hi, slot * D:(slot + 1) * D] = r[slot][C:] + both[:C]
            state[slot] = state[slot] * jnp.exp(gt_c[slot][lo:lo + 1, :]) + both[C:]
    for slot in slots:
        s_ref[slot] = state[slot]


def gdn_scan(qkv_hm, kt_hm, slab, slabt):
    t = qkv_hm.shape[1]
    L = GDN_GROUP
    D = GDN_HEAD_DIM
    hq = GDN_SCAN_HEADS
    assert t % L == 0 and GDN_QK_HEADS % hq == 0
    q_blocks = GDN_QK_HEADS // hq
    return pl.pallas_call(
        _gdn_scan_kernel,
        grid=(q_blocks, t // L),
        in_specs=[
            pl.BlockSpec((hq, L, D), lambda j, n: (j, n, 0)),
            pl.BlockSpec((hq, L, D), lambda j, n: (q_blocks + j, n, 0)),
            pl.BlockSpec((hq, D, L), lambda j, n: (j, 0, n)),
            pl.BlockSpec((2 * hq, L, D), lambda j, n: (q_blocks + j, n, 0)),
            pl.BlockSpec((L, LANES), lambda j, n: (n, 0)),
            pl.BlockSpec((LANES, L), lambda j, n: (0, n)),
        ],
        out_specs=pl.BlockSpec((L, 2 * hq * D), lambda j, n: (n, j)),
        out_shape=jax.ShapeDtypeStruct((t, GDN_V_HEADS * D), F32),
        scratch_shapes=[pltpu.VMEM((2 * hq, D, D), F32)],
        compiler_params=_cparams(("parallel", "arbitrary")),
        name="gdn_scan",
    )(qkv_hm, qkv_hm, kt_hm, qkv_hm, slab, slabt)


def _gdn_out_kernel(o_ref, z_ref, onorm_ref, w_ref, gpost_ref, h_ref, out_ref):
    o = o_ref[...]
    z = z_ref[...]
    parts = []
    for hd in range(GDN_V_HEADS):
        seg = o[:, hd * GDN_HEAD_DIM:(hd + 1) * GDN_HEAD_DIM]
        parts.append(seg * lax.rsqrt(jnp.mean(seg * seg, axis=-1, keepdims=True) + NORM_EPS))
    gated = jnp.concatenate(parts, axis=1) * onorm_ref[...] * (z * _sigmoid(z))
    mix = jnp.dot(gated.astype(BF16), w_ref[...], preferred_element_type=F32)
    out_ref[...] = h_ref[...] + _rms(mix, gpost_ref[...])


def gdn_out(o, proj, onorm_tiled, w_out, gpost, h, tm=512):
    t, vw = o.shape
    d = h.shape[1]
    z_blk = (proj.shape[1] - vw) // vw
    assert proj.shape[1] % vw == 0 and t % tm == 0
    return pl.pallas_call(
        _gdn_out_kernel,
        grid=(t // tm,),
        in_specs=[
            pl.BlockSpec((tm, vw), lambda i: (i, 0)),
            pl.BlockSpec((tm, vw), lambda i: (i, z_blk)),
            pl.BlockSpec((1, vw), lambda i: (0, 0)),
            pl.BlockSpec((vw, d), lambda i: (0, 0)),
            pl.BlockSpec((1, d), lambda i: (0, 0)),
            pl.BlockSpec((tm, d), lambda i: (i, 0)),
        ],
        out_specs=pl.BlockSpec((tm, d), lambda i: (i, 0)),
        out_shape=jax.ShapeDtypeStruct((t, d), F32),
        compiler_params=_cparams(("parallel",)),
        name="gdn_out",
    )(o, proj, onorm_tiled, w_out, gpost.reshape(1, d), h)


def _compress_kernel(x_ref, pos_ref, w1_ref, w2_ref, o_ref):
    x = x_ref[...]
    pos = pos_ref[...]
    nc, half = x.shape
    w1 = w1_ref[...]
    first = _mm(x + pos[0:1, :], w1[:half])
    second = _mm(x + pos[1:2, :], w1[half:])
    hid = first + pltpu.roll(second, nc - 1, axis=0)
    hid = hid * _sigmoid(hid)
    out = jnp.dot(hid.astype(BF16), w2_ref[...], preferred_element_type=F32)
    rowi = lax.broadcasted_iota(jnp.int32, out.shape, 0)
    o_ref[...] = jnp.where(rowi < nc - 1, out, 0.0)


def compress(x2, pos2, w1, w2):
    _, g, nc, wdt = x2.shape
    hid = w1.shape[2]
    dh = w2.shape[2]
    return pl.pallas_call(
        _compress_kernel,
        grid=(2, g),
        in_specs=[
            pl.BlockSpec((None, None, nc, wdt), lambda b, gi: (b, gi, 0, 0)),
            pl.BlockSpec((None, 2, wdt), lambda b, gi: (b, 0, 0)),
            pl.BlockSpec((None, 2 * wdt, hid), lambda b, gi: (b, 0, 0)),
            pl.BlockSpec((None, hid, dh), lambda b, gi: (b, 0, 0)),
        ],
        out_specs=pl.BlockSpec((None, None, nc, dh), lambda b, gi: (b, gi, 0, 0)),
        out_shape=jax.ShapeDtypeStruct((2, g, nc, dh), F32),
        compiler_params=_cparams(("parallel", "parallel")),
        name="nsa_compress",
    )(x2, pos2, w1, w2)


def _cmp_topk_kernel(q_ref, kbd_ref, vt_ref, m_ref, *refs, tq, nc, nsel, topk, q0):
    oc_ref, sel_ref, s_ref, p_ref, psum_ref = refs[-5:]
    i = q0 + pl.program_id(1)
    dh = NSA_HEAD_DIM
    rows_per = math.gcd(nc, CMP_ROWS)
    q = (q_ref[...] * ((dh ** -0.5) * LOG2_E)).astype(BF16)
    subs = [slice(c0, c0 + CMP_SUB) for c0 in range(0, tq, CMP_SUB)]

    def scores(cols):
        s_ref[:, cols] = jnp.dot(kbd_ref[...], q[:, cols], preferred_element_type=F32)

    def softmax(cols):
        for c0 in range(cols.start, cols.stop, LANES):
            lanes = slice(c0, c0 + LANES)
            tpos = i * tq + c0 + lax.broadcasted_iota(jnp.int32, (rows_per, LANES), 1)
            cblk0 = lax.broadcasted_iota(jnp.int32, (rows_per, LANES), 0)
            masks = [None if CMP_STRIDE * ((c + 1) * rows_per - 1) + CMP_BLOCK - 1 <= q0 * tq
                     else (CMP_STRIDE * (cblk0 + c * rows_per) + CMP_BLOCK - 1) <= tpos
                     for c in range(nc // rows_per)]
            visible = lambda rows, mask: s_ref[rows, lanes] if mask is None else jnp.where(mask, s_ref[rows, lanes],
                                                                                            -jnp.inf)
            for r in range(NSA_REP):
                pieces = [slice(r * nc + c * rows_per, r * nc + (c + 1) * rows_per) for c in range(nc // rows_per)]
                m = jnp.full((1, LANES), -jnp.inf, F32)
                for rows, mask in zip(pieces, masks):
                    m = jnp.maximum(m, jnp.max(visible(rows, mask), axis=0, keepdims=True))
                m = jnp.where(m > -jnp.inf, m, 0.0)
                total = jnp.zeros((1, LANES), F32)
                for rows, mask in zip(pieces, masks):
                    e = jnp.exp2(visible(rows, mask) - m)
                    s_ref[rows, lanes] = e
                    total = total + jnp.sum(e, axis=0, keepdims=True)
                inv = 1.0 / jnp.maximum(total, 1e-30)
                for c, rows in enumerate(pieces):
                    p = s_ref[rows, lanes] * inv
                    p_ref[rows, lanes] = p.astype(BF16)
                    prow = slice(c * rows_per, (c + 1) * rows_per)
                    psum_ref[prow, lanes] = p if r == 0 else psum_ref[prow, lanes] + p

    def outputs(cols):
        oc_t = jnp.dot(vt_ref[...], p_ref[:, cols], preferred_element_type=F32)
        imp = jnp.zeros((nsel, CMP_SUB), F32)
        for piece in _split3(psum_ref[:, cols]):
            imp = imp + jnp.dot(m_ref[...], piece, preferred_element_type=F32)
        return oc_t, imp

    def select(cols, oc_t, imp):
        for c0 in range(0, CMP_SUB, LANES):
            local = slice(c0, c0 + LANES)
            lanes = slice(cols.start + c0, cols.start + c0 + LANES)
            oc_ref[lanes, :] = oc_t[:, local].T
            t1 = i * tq + cols.start + c0 + lax.broadcasted_iota(jnp.int32, (nsel, LANES), 1)
            blk = lax.broadcasted_iota(jnp.int32, (nsel, LANES), 0)
            cur = _div_pow2(t1, SEL_BLOCK)
            forced = (blk == 0) | (blk == cur) | (blk == cur - 1)
            valid = blk * SEL_BLOCK <= t1
            score = jnp.where(valid, jnp.where(forced, FORCED_SCORE, imp[:, local]), -jnp.inf)
            blkf = blk.astype(F32)
            work = score
            for _ in range(topk):
                mx = jnp.max(work, axis=0, keepdims=True)
                first = jnp.min(jnp.where(work == mx, blkf, float(nsel)), axis=0, keepdims=True)
                work = jnp.where(blkf == first, -jnp.inf, work)
            sel_ref[:nsel, lanes] = jnp.where((score > -jnp.inf) & (work == -jnp.inf), 1.0, 0.0).astype(sel_ref.dtype)
            if sel_ref.shape[0] > nsel:
                sel_ref[nsel:, lanes] = jnp.zeros((sel_ref.shape[0] - nsel, LANES), sel_ref.dtype)

    for cols in subs:
        scores(cols)
    results = []
    for cols in subs:
        softmax(cols)
        results.append(outputs(cols))
    for cols, (oc_t, imp) in zip(subs, results):
        select(cols, oc_t, imp)


def cmp_topk(q_t, kbd, v_t, imp_mat_t, nsel_all, topk, q0, nq, tq, prev):
    qw, t = q_t.shape
    g = kbd.shape[0]
    nc = kbd.shape[1] // NSA_REP
    nsel = imp_mat_t.shape[0]
    assert math.gcd(nc, CMP_ROWS) % 8 == 0
    in_specs = [
        pl.BlockSpec((NSA_GW, tq), lambda gi, i: (gi, q0 + i)),
        pl.BlockSpec((None, NSA_REP * nc, NSA_GW), lambda gi, i: (gi, 0, 0)),
        pl.BlockSpec((None, NSA_GW, NSA_REP * nc), lambda gi, i: (gi, 0, 0)),
        pl.BlockSpec((nsel, nc), lambda gi, i: (0, 0)),
    ]
    in_specs += [pl.BlockSpec(memory_space=pl.ANY)] * 2
    args = [q_t, kbd, v_t, imp_mat_t, *prev]
    aliases = {len(args) - 2: 0, len(args) - 1: 1}
    return pl.pallas_call(
        functools.partial(_cmp_topk_kernel, tq=tq, nc=nc, nsel=nsel, topk=topk, q0=q0),
        grid=(g, nq),
        in_specs=in_specs,
        out_specs=[
            pl.BlockSpec((tq, NSA_GW), lambda gi, i: (q0 + i, gi)),
            pl.BlockSpec((None, nsel_all, tq), lambda gi, i: (gi, 0, q0 + i)),
        ],
        out_shape=[
            jax.ShapeDtypeStruct((t, qw), F32),
            jax.ShapeDtypeStruct((g, nsel_all, t), F32),
        ],
        input_output_aliases=aliases,
        scratch_shapes=[
            pltpu.VMEM((NSA_REP * nc, tq), F32),
            pltpu.VMEM((NSA_REP * nc, tq), BF16),
            pltpu.VMEM((nc, tq), F32),
        ],
        compiler_params=_cparams(("parallel", "parallel")),
        name="nsa_cmp_topk",
    )(*args)


def _flash_kernel(q_ref, k_ref, v_ref, sel_ref, o_ref,
                  m_ref, l_ref, alpha_ref, acc_ref, s_ref, p_ref, bias_ref, kbd_ref, vt_ref, qs_ref, *, tq, kt):
    qi = pl.program_id(1)
    dh = NSA_HEAD_DIM
    n_sub = tq // FLASH_SUB
    n_blk = kt // SEL_BLOCK

    m_ref[...] = jnp.full_like(m_ref, NEG_INIT)
    l_ref[...] = jnp.zeros_like(l_ref)
    acc_ref[...] = jnp.zeros_like(acc_ref)
    qs_ref[...] = (q_ref[...] * ((dh ** -0.5) * LOG2_E)).astype(BF16)
    vt_ref[dh:, :] = jnp.ones((FLASH_SUM_ROWS, kt), BF16)

    def key_tile(ki, positional, first_sub=0):
        k4 = k_ref[ki]
        kseg = _div_pow2(lax.broadcasted_iota(jnp.int32, k4.shape, 1), dh)
        for r in range(NSA_REP):
            kbd_ref[r * kt:(r + 1) * kt, :] = jnp.where(kseg == r, k4, jnp.zeros_like(k4))
        vt_ref[:dh, :] = v_ref[ki]

        def scores(sub):
            c0 = sub * FLASH_SUB
            cols = slice(c0, c0 + FLASH_SUB)
            s_ref[:, cols] = jnp.dot(kbd_ref[...], qs_ref[:, cols],
                                     preferred_element_type=F32)
            if not positional:
                return
            for jb in range(n_blk):
                rows = slice(jb * SEL_BLOCK, (jb + 1) * SEL_BLOCK)
                picked = sel_ref[pl.ds(ki * n_blk + jb, 1), cols] > 0.5
                tpos = qi * tq + c0 + lax.broadcasted_iota(jnp.int32, (SEL_BLOCK, FLASH_SUB), 1)
                kpos = ki * kt + jb * SEL_BLOCK + lax.broadcasted_iota(jnp.int32, (SEL_BLOCK, FLASH_SUB), 0)
                bias_ref[rows, cols] = jnp.where(picked & (kpos <= tpos), 0.0, -jnp.inf)

        scores(first_sub)
        for sub in range(first_sub, n_sub):
            c0 = sub * FLASH_SUB
            cols = slice(c0, c0 + FLASH_SUB)
            if sub + 1 < n_sub:
                scores(sub + 1)
            if not positional:
                picked_sub = [sel_ref[pl.ds(ki * n_blk + jb, 1), cols] > 0.5 for jb in range(n_blk)]
            for ch in range(FLASH_SUB // LANES):
                lanes = slice(c0 + ch * LANES, c0 + (ch + 1) * LANES)
                if positional:
                    bias = bias_ref[:, lanes]
                else:
                    picked = [pk[:, ch * LANES:(ch + 1) * LANES] for pk in picked_sub]
                for r in range(NSA_REP):
                    if positional:
                        x = s_ref[r * kt:(r + 1) * kt, lanes] + bias
                    else:
                        x = jnp.concatenate(
                            [jnp.where(picked[jb], s_ref[r * kt + jb * SEL_BLOCK:r * kt + (jb + 1) * SEL_BLOCK, lanes],
                                       -jnp.inf) for jb in range(n_blk)], axis=0)
                    m_prev = m_ref[r:r + 1, lanes]
                    m_new = jnp.maximum(m_prev, jnp.max(x, axis=0, keepdims=True))
                    m_ref[r:r + 1, lanes] = m_new
                    alpha_ref[r:r + 1, lanes] = jnp.exp2(m_prev - m_new)
                    p_ref[r * kt:(r + 1) * kt, lanes] = jnp.exp2(x - m_new).astype(BF16)
            for r in range(NSA_REP):
                pv = jnp.dot(vt_ref[...], p_ref[r * kt:(r + 1) * kt, cols],
                             preferred_element_type=F32)
                hd = slice(r * dh, (r + 1) * dh)
                alpha = alpha_ref[r:r + 1, cols]
                acc_ref[hd, cols] = acc_ref[hd, cols] * alpha + pv[:dh]
                l_ref[r:r + 1, cols] = l_ref[r:r + 1, cols] * alpha + pv[dh:dh + 1]

    def before_diagonal(ki, carry):
        key_tile(ki, False)
        return carry

    diag = qi * (tq // kt)
    lax.fori_loop(0, diag, before_diagonal, 0)
    for d in range(tq // kt):
        def on_diagonal(ki, carry, first_sub=d * kt // FLASH_SUB):
            key_tile(ki, True, first_sub=first_sub)
            return carry
        lax.fori_loop(diag + d, diag + d + 1, on_diagonal, 0)

    for ch in range(tq // LANES):
        lanes = slice(ch * LANES, (ch + 1) * LANES)
        out_t = jnp.concatenate(
            [acc_ref[r * dh:(r + 1) * dh, lanes] / l_ref[r:r + 1, lanes] for r in range(NSA_REP)], axis=0)
        o_ref[lanes, :] = out_t.T


def flash_branch(q_t, k4, v_t, sel_t, tq, kt):
    qw, t = q_t.shape
    g, nk = k4.shape[:2]
    dh = v_t.shape[2]
    nsel = sel_t.shape[1]
    assert t % tq == 0 and tq % kt == 0 and kt % SEL_BLOCK == 0 and tq % FLASH_SUB == 0 and kt % FLASH_SUB == 0
    return pl.pallas_call(
        functools.partial(_flash_kernel, tq=tq, kt=kt),
        grid=(g, t // tq),
        in_specs=[
            pl.BlockSpec((NSA_GW, tq), lambda gi, i: (gi, i)),
            pl.BlockSpec((None, nk, kt, NSA_GW), lambda gi, i: (gi, 0, 0, 0), pipeline_mode=pl.Buffered(1)),
            pl.BlockSpec((None, nk, dh, kt), lambda gi, i: (gi, 0, 0, 0), pipeline_mode=pl.Buffered(1)),
            pl.BlockSpec((None, nsel, tq), lambda gi, i: (gi, 0, i)),
        ],
        out_specs=pl.BlockSpec((tq, NSA_GW), lambda gi, i: (i, gi)),
        scratch_shapes=[
            pltpu.VMEM((8, tq), F32),
            pltpu.VMEM((8, tq), F32),
            pltpu.VMEM((8, tq), F32),
            pltpu.VMEM((NSA_GW, tq), F32),
            pltpu.VMEM((NSA_REP * kt, tq), F32),
            pltpu.VMEM((NSA_REP * kt, tq), BF16),
            pltpu.VMEM((kt, tq), F32),
            pltpu.VMEM((NSA_REP * kt, NSA_GW), BF16),
            pltpu.VMEM((dh + FLASH_SUM_ROWS, kt), BF16),
            pltpu.VMEM((NSA_GW, tq), BF16),
        ],
        out_shape=jax.ShapeDtypeStruct((t, qw), F32),
        compiler_params=_cparams(("parallel", "arbitrary")),
        name="nsa_selected",
    )(q_t, k4, v_t, sel_t)


def _window_kernel(q_ref, k_ref, v_ref, o_ref, s_ref, p_ref, bias_ref, kbd_ref, vt_ref, *, tq, kt):
    dh = NSA_HEAD_DIM
    nw = WINDOW // kt + 1
    span = nw * kt
    n_tiles = tq // kt
    qs = (q_ref[...] * ((dh ** -0.5) * LOG2_E)).astype(BF16)

    def stage(h):
        i = pl.program_id(1) * n_tiles + h
        for w in range(nw):
            tile = i - (nw - 1) + w
            k4 = k_ref[jnp.maximum(tile, 0)]
            kseg = _div_pow2(lax.broadcasted_iota(jnp.int32, k4.shape, 1), dh)
            for r in range(NSA_REP):
                at = r * span + w * kt
                kbd_ref[h, at:at + kt, :] = jnp.where(kseg == r, k4, jnp.zeros_like(k4))
            vt_ref[h, :dh, w * kt:(w + 1) * kt] = v_ref[jnp.maximum(tile, 0)]
            tpos = i * kt + lax.broadcasted_iota(jnp.int32, (kt, kt), 1)
            kpos = tile * kt + lax.broadcasted_iota(jnp.int32, (kt, kt), 0)
            allowed = (kpos >= 0) & (kpos <= tpos) & (kpos > tpos - WINDOW)
            bias_ref[h, w * kt:(w + 1) * kt, :] = jnp.where(allowed, 0.0, -jnp.inf)
        vt_ref[h, dh:, :] = jnp.ones((FLASH_SUM_ROWS, span), BF16)

    def scores(h):
        s_ref[h] = jnp.dot(kbd_ref[h], qs[:, h * kt:(h + 1) * kt], preferred_element_type=F32)

    def softmax(h):
        for ch in range(kt // LANES):
            lanes = slice(ch * LANES, (ch + 1) * LANES)
            for r in range(NSA_REP):
                m = jnp.full((1, LANES), -jnp.inf, F32)
                for w in range(nw):
                    rows = slice(r * span + w * kt, r * span + (w + 1) * kt)
                    m = jnp.maximum(m, jnp.max(s_ref[h, rows, lanes] + bias_ref[h, w * kt:(w + 1) * kt, lanes],
                                               axis=0, keepdims=True))
                m = jnp.where(m > -jnp.inf, m, 0.0)
                for w in range(nw):
                    rows = slice(r * span + w * kt, r * span + (w + 1) * kt)
                    x = s_ref[h, rows, lanes] + bias_ref[h, w * kt:(w + 1) * kt, lanes]
                    p_ref[h, rows, lanes] = jnp.exp2(x - m).astype(BF16)

    def weighted(h):
        return [jnp.dot(vt_ref[h], p_ref[h, r * span:(r + 1) * span, :], preferred_element_type=F32)
                for r in range(NSA_REP)]

    def store(h, pv):
        for ch in range(kt // LANES):
            lanes = slice(ch * LANES, (ch + 1) * LANES)
            out_t = jnp.concatenate(
                [pv[r][:dh, lanes] / jnp.maximum(pv[r][dh:dh + 1, lanes], 1e-30) for r in range(NSA_REP)], axis=0)
            o_ref[h * kt + ch * LANES:h * kt + (ch + 1) * LANES, :] = out_t.T

    for h in range(n_tiles):
        stage(h)
        scores(h)
    pvs = []
    for h in range(n_tiles):
        softmax(h)
        pvs.append(weighted(h))
    for h in range(n_tiles):
        store(h, pvs[h])


def window_branch(q_t, k4, v_t, tq):
    qw, t = q_t.shape
    g, nk, kt, _ = k4.shape
    dh = v_t.shape[2]
    assert tq % kt == 0 and WINDOW % kt == 0 and t % tq == 0
    n_tiles = tq // kt
    span = (WINDOW // kt + 1) * kt
    return pl.pallas_call(
        functools.partial(_window_kernel, tq=tq, kt=kt),
        grid=(g, t // tq),
        in_specs=[
            pl.BlockSpec((NSA_GW, tq), lambda gi, i: (gi, i)),
            pl.BlockSpec((None, nk, kt, NSA_GW), lambda gi, i: (gi, 0, 0, 0)),
            pl.BlockSpec((None, nk, dh, kt), lambda gi, i: (gi, 0, 0, 0)),
        ],
        out_specs=pl.BlockSpec((tq, NSA_GW), lambda gi, i: (i, gi)),
        scratch_shapes=[
            pltpu.VMEM((n_tiles, NSA_REP * span, kt), F32),
            pltpu.VMEM((n_tiles, NSA_REP * span, kt), BF16),
            pltpu.VMEM((n_tiles, span, kt), F32),
            pltpu.VMEM((n_tiles, NSA_REP * span, NSA_GW), BF16),
            pltpu.VMEM((n_tiles, dh + FLASH_SUM_ROWS, span), BF16),
        ],
        out_shape=jax.ShapeDtypeStruct((t, qw), F32),
        compiler_params=_cparams(("parallel", "parallel")),
        name="nsa_window",
    )(q_t, k4, v_t)


def _nsa_out_kernel(oc_ref, os_ref, ow_ref, gl_ref, eg_ref, w_ref, gpost_ref, h_ref, out_ref):
    pieces = _split3(_sigmoid(gl_ref[...]))
    mixed = jnp.zeros(oc_ref.shape, F32)
    for b, br_ref in enumerate((oc_ref, os_ref, ow_ref)):
        gfull = jnp.zeros(oc_ref.shape, F32)
        for piece in pieces:
            gfull = gfull + jnp.dot(piece, eg_ref[b], preferred_element_type=F32)
        mixed = mixed + gfull * br_ref[...]
    mix = jnp.dot(mixed.astype(BF16), w_ref[...], preferred_element_type=F32)
    out_ref[...] = h_ref[...] + _rms(mix, gpost_ref[...])


def nsa_out(oc, osel, ow, gate_logits, expand, w_o, gpost, h, tm=512):
    t, qw = oc.shape
    d = h.shape[1]
    assert t % tm == 0
    row = lambda w: pl.BlockSpec((tm, w), lambda i: (i, 0))
    return pl.pallas_call(
        _nsa_out_kernel,
        grid=(t // tm,),
        in_specs=[
            row(qw), row(qw), row(qw), row(LANES),
            pl.BlockSpec((3, LANES, qw), lambda i: (0, 0, 0)),
            pl.BlockSpec((qw, d), lambda i: (0, 0)),
            pl.BlockSpec((1, d), lambda i: (0, 0)),
            row(d),
        ],
        out_specs=row(d),
        out_shape=jax.ShapeDtypeStruct((t, d), F32),
        compiler_params=_cparams(("parallel",)),
        name="nsa_out",
    )(oc, osel, ow, gate_logits, expand, w_o, gpost.reshape(1, d), h)


def _pad_cols(w, n):
    return jnp.pad(w, ((0, 0), (0, n - w.shape[1])))


def _importance_matrix(nc, nsel):
    r = SEL_BLOCK // CMP_STRIDE
    c = CMP_BLOCK // CMP_STRIDE
    mat = np.zeros((nc, nsel), np.float32)
    for kblk in range(nsel):
        for m in range(r):
            for n in range(c):
                j = r * kblk + m - n
                if 0 <= j < nc - 1:
                    mat[j, kblk] += 1.0
    return jnp.asarray(mat, BF16)


def _gate_expand():
    e = np.zeros((3, LANES, NSA_GROUPS * NSA_GW), np.float32)
    for head in range(NSA_GROUPS * NSA_REP):
        for b in range(3):
            e[b, head * 3 + b, head * NSA_HEAD_DIM:(head + 1) * NSA_HEAD_DIM] = 1.0
    return jnp.asarray(e, BF16)


def _block_diag_kv(a, b):
    g, nc, dh = a.shape
    eye = jnp.eye(NSA_REP, dtype=bool)
    a_t = jnp.swapaxes(a, 1, 2)
    a_bd = jnp.where(eye[None, :, None, :, None], a_t[:, None, :, None, :], 0.0)
    b_bd = jnp.where(eye[None, :, None, :, None], b[:, None, :, None, :], 0.0)
    return (a_bd.reshape(g, NSA_REP * dh, NSA_REP * nc).astype(BF16),
            b_bd.reshape(g, NSA_REP * nc, NSA_REP * dh).astype(BF16))


def kernel(x, p, mix_pre_norm, mix_post_norm, ffn_pre_norm, ffn_post_norm, gdn_w_in, gdn_conv_w, gdn_a_log,
           gdn_dt_bias, gdn_o_norm, gdn_w_out, kv_norm, kv_w, cmp_pos, cmp_w1, cmp_w2, nsa_w_qg, nsa_w_o,
           ffn_w_in, ffn_w_out, ple_w_in, ple_w_gate):
    depth = p.shape[0]
    n_a = gdn_w_in.shape[0]
    t = x.shape[1]
    h = x[0]
    conv_w_cols = gdn_conv_w.shape[2]
    vw = GDN_V_HEADS * GDN_HEAD_DIM
    main_w = conv_w_cols + vw

    def channel_and_ple(h, i):
        return ffn_ple(h, ffn_pre_norm[i], ffn_w_in[i].astype(BF16), ffn_w_out[i].astype(BF16),
                       ffn_post_norm[i], p[i, 0], ple_w_in[i].astype(BF16), ple_w_gate[i].astype(BF16))

    for i in range(n_a):
        w_in = gdn_w_in[i]
        w_beta = w_in[:, main_w:main_w + GDN_V_HEADS]
        w_a = w_in[:, main_w + GDN_V_HEADS:]
        w_small = _pad_cols(jnp.concatenate([w_beta, w_a, w_a], axis=1), LANES).astype(BF16)
        proj = norm_matmul(h, mix_pre_norm[i], w_in.astype(BF16), n=main_w)
        ba = norm_matmul(h, mix_pre_norm[i], w_small)
        pad_vec = lambda v: jnp.pad(v, (GDN_V_HEADS, LANES - 2 * GDN_V_HEADS))
        alog_vec = (pad_vec(gdn_a_log[i]) + jnp.pad(gdn_a_log[i], (2 * GDN_V_HEADS, LANES - 3 * GDN_V_HEADS)))
        dtb_vec = (pad_vec(gdn_dt_bias[i]) + jnp.pad(gdn_dt_bias[i], (2 * GDN_V_HEADS, LANES - 3 * GDN_V_HEADS)))
        slab = gdn_gates(ba, alog_vec.reshape(1, LANES), dtb_vec.reshape(1, LANES))
        qkv_hm = gdn_conv(proj, gdn_conv_w[i])
        kt_hm = jnp.swapaxes(qkv_hm[GDN_QK_HEADS:2 * GDN_QK_HEADS], 1, 2)
        o = gdn_scan(qkv_hm, kt_hm, slab, slab.T)
        onorm_tiled = jnp.tile(gdn_o_norm[i], GDN_V_HEADS).reshape(1, vw)
        h = gdn_out(o, proj, onorm_tiled, gdn_w_out[i].astype(BF16), mix_post_norm[i], h)
        h = channel_and_ple(h, i)

    g = NSA_GROUPS
    dh = NSA_HEAD_DIM
    kv = norm_matmul(h, kv_norm, kv_w.astype(BF16), tn=768)
    kv6 = jnp.transpose(kv.reshape(t, 6, g, dh), (1, 2, 0, 3))
    nc = t // CMP_STRIDE
    nsel = t // SEL_BLOCK
    x2 = kv6[0:2].reshape(2, g, nc, CMP_STRIDE * dh)
    pos2 = cmp_pos.reshape(2, 2, CMP_STRIDE * dh)
    cmp_out = compress(x2, pos2, cmp_w1.astype(BF16), cmp_w2.astype(BF16))
    imp_mat_t = _importance_matrix(nc, nsel).T
    nq_r = t // (CMP_TQ * CMP_RANGES)
    assert t % (CMP_TQ * CMP_RANGES) == 0 and nc % CMP_RANGES == 0 and nsel % (8 * CMP_RANGES) == 0
    cmp_ranges = []
    for rg in range(CMP_RANGES):
        nc_r, nsel_r = (rg + 1) * nc // CMP_RANGES, (rg + 1) * nsel // CMP_RANGES
        vbd_t, kbd = _block_diag_kv(cmp_out[1][:, :nc_r], cmp_out[0][:, :nc_r])
        cmp_ranges.append((kbd, vbd_t, imp_mat_t[:nsel_r, :nc_r]))
    kt = FLASH_KT
    tiles = lambda a: a.reshape(g, t // kt, kt, dh)
    rep_k = lambda a: jnp.tile(tiles(a), (1, 1, 1, NSA_REP)).astype(BF16)
    rep_t = lambda a: jnp.swapaxes(tiles(a), 2, 3).astype(BF16)
    k_slc, v_slc_t = rep_k(kv6[2]), rep_t(kv6[3])
    k_win, v_win_t = rep_k(kv6[4]), rep_t(kv6[5])
    expand = _gate_expand()

    for i in range(n_a, depth):
        j = i - n_a
        qw = g * NSA_GW
        w_qg = nsa_w_qg[j]
        q_t = norm_matmul_t(h, mix_pre_norm[i], w_qg[:, :qw].T.astype(BF16))
        gate_logits = norm_matmul(h, mix_pre_norm[i], _pad_cols(w_qg[:, qw:], LANES).astype(BF16))
        cmp_res = (jnp.zeros((t, qw), F32), jnp.zeros((g, nsel, t), F32))
        for rg, (kbd, vbd_t, imp_r) in enumerate(cmp_ranges):
            cmp_res = cmp_topk(q_t, kbd, vbd_t, imp_r, nsel, min(SEL_TOPK, nsel), rg * nq_r, nq_r, CMP_TQ,
                               prev=cmp_res)
        o_c, sel_t = cmp_res
        o_s = flash_branch(q_t, k_slc, v_slc_t, sel_t, tq=min(FLASH_TQ, t), kt=kt)
        o_w = window_branch(q_t, k_win, v_win_t, tq=WINDOW_TQ)
        h = nsa_out(o_c, o_s, o_w, gate_logits, expand, nsa_w_o[j].astype(BF16), mix_post_norm[i], h)
        h = channel_and_ple(h, i)
    return h[None]
```

```python
import functools

import numpy as np
import jax
import jax.numpy as jnp
from jax import lax
from jax.experimental import pallas as pl
from jax.experimental.pallas import tpu as pltpu

F32 = jnp.float32
BF16 = jnp.bfloat16

NORM_EPS = 1e-6
L2_EPS = 1e-6
GDN_QK_HEADS = 8
GDN_V_HEADS = 16
GDN_HEAD_DIM = 128
GDN_CONV = 4
GDN_CHUNK = 64
GDN_GROUP = 256
GDN_SCAN_HEADS = 8
GDN_CONV_HEADS = 4
NSA_GROUPS = 4
NSA_REP = 4
NSA_HEAD_DIM = 64
NSA_GW = NSA_REP * NSA_HEAD_DIM
CMP_BLOCK = 32
CMP_STRIDE = 16
SEL_BLOCK = 64
SEL_TOPK = 16
WINDOW = 512
FORCED_SCORE = 1e4
LANES = 128
NEG_INIT = -(2.0 ** 100)
LOG2_E = 1.4426950408889634
FLASH_TQ = 2048
FLASH_KT = 256
WINDOW_TQ = 512
FLASH_SUB = 256
FLASH_SUM_ROWS = 16
CMP_ROWS = 256
CMP_TQ = 512
CMP_SUB = 256
CMP_RANGES = 4

V7X_VMEM_BYTES = 64 * 1024 * 1024
VMEM_LIMIT = V7X_VMEM_BYTES * 7 // 8
FFN_VMEM_LIMIT = V7X_VMEM_BYTES * 15 // 16


def _cparams(sem, vmem_limit=VMEM_LIMIT):
    return pltpu.CompilerParams(dimension_semantics=sem, vmem_limit_bytes=vmem_limit)


def _rms(x, gain):
    return x * lax.rsqrt(jnp.mean(x * x, axis=-1, keepdims=True) + NORM_EPS) * gain


def _mm(a, b):
    return jnp.dot(a.astype(BF16), b.astype(BF16), preferred_element_type=F32)


def _sigmoid(x):
    return 1.0 / (1.0 + jnp.exp2(x * (-LOG2_E)))


def _div_pow2(x, d):
    shift = d.bit_length() - 1
    assert d == 1 << shift
    return jnp.right_shift(x, shift)


def _split3(x):
    a = x.astype(BF16)
    r = x - a.astype(F32)
    b = r.astype(BF16)
    c = (r - b.astype(F32)).astype(BF16)
    return a, b, c


def _norm_matmul_kernel(x_ref, g_ref, w_ref, o_ref, xn_ref):
    @pl.when(pl.program_id(1) == 0)
    def _():
        xn_ref[...] = _rms(x_ref[...], g_ref[...]).astype(BF16)

    o_ref[...] = jnp.dot(xn_ref[...], w_ref[...], preferred_element_type=F32)


def _norm_matmul_t_kernel(x_ref, g_ref, wt_ref, o_ref, xn_ref):
    @pl.when(pl.program_id(1) == 0)
    def _():
        xn_ref[...] = _rms(x_ref[...], g_ref[...]).astype(BF16)

    o_ref[...] = lax.dot_general(wt_ref[...], xn_ref[...], (((1,), (1,)), ((), ())), preferred_element_type=F32)


def norm_matmul_t(h, gain, w_t, tm=2048, tn=1024):
    t, d = h.shape
    n = w_t.shape[0]
    tn = min(tn, n)
    assert t % tm == 0 and n % tn == 0
    return pl.pallas_call(
        _norm_matmul_t_kernel,
        grid=(t // tm, n // tn),
        in_specs=[
            pl.BlockSpec((tm, d), lambda i, j: (i, 0)),
            pl.BlockSpec((1, d), lambda i, j: (0, 0)),
            pl.BlockSpec((tn, d), lambda i, j: (j, 0)),
        ],
        out_specs=pl.BlockSpec((tn, tm), lambda i, j: (j, i)),
        out_shape=jax.ShapeDtypeStruct((n, t), F32),
        scratch_shapes=[pltpu.VMEM((tm, d), BF16)],
        compiler_params=_cparams(("parallel", "arbitrary")),
        name="norm_matmul_t",
    )(h, gain.reshape(1, d), w_t)


def norm_matmul(h, gain, w, tm=2048, tn=1024, n=None):
    t, d = h.shape
    n = w.shape[1] if n is None else n
    tn = min(tn, n)
    assert t % tm == 0 and n % tn == 0
    return pl.pallas_call(
        _norm_matmul_kernel,
        grid=(t // tm, n // tn),
        in_specs=[
            pl.BlockSpec((tm, d), lambda i, j: (i, 0)),
            pl.BlockSpec((1, d), lambda i, j: (0, 0)),
            pl.BlockSpec((d, tn), lambda i, j: (0, j)),
        ],
        out_specs=pl.BlockSpec((tm, tn), lambda i, j: (i, j)),
        out_shape=jax.ShapeDtypeStruct((t, n), F32),
        scratch_shapes=[pltpu.VMEM((tm, d), BF16)],
        compiler_params=_cparams(("parallel", "arbitrary")),
        name="norm_matmul",
    )(h, gain.reshape(1, d), w)


def _ffn_ple_kernel(h_ref, gpre_ref, wg_ref, wu_ref, wo_ref, gpost_ref, p_ref, wple_ref, wgt_ref,
                    o_ref, xn_ref, acc_ref, *, nf):
    f = pl.program_id(1)

    @pl.when(f == 0)
    def _():
        xn_ref[...] = _rms(h_ref[...], gpre_ref[...]).astype(BF16)
        acc_ref[...] = jnp.zeros_like(acc_ref)

    xn = xn_ref[...]
    tf = wg_ref.shape[1]
    half = (tf // LANES // 2) * LANES
    for a, b in ((0, half), (half, tf)):
        gate = jnp.dot(xn, wg_ref[:, a:b], preferred_element_type=F32)
        up = jnp.dot(xn, wu_ref[:, a:b], preferred_element_type=F32)
        act = gate * _sigmoid(gate) * up
        acc_ref[...] += jnp.dot(act.astype(BF16), wo_ref[a:b, :], preferred_element_type=F32)

    @pl.when(f == nf - 1)
    def _():
        h2 = h_ref[...] + _rms(acc_ref[...], gpost_ref[...])
        emb = jnp.dot(p_ref[...].astype(BF16), wple_ref[...], preferred_element_type=F32)
        gt = _sigmoid(jnp.dot(h2.astype(BF16), wgt_ref[...], preferred_element_type=F32))
        o_ref[...] = h2 + emb * gt


def ffn_ple(h, gpre, w_in, w_out, gpost, p, w_ple, w_plegate, tm=1024, tf=1408):
    t, d = h.shape
    fh = w_out.shape[0]
    pd = p.shape[1]
    assert t % tm == 0 and fh % tf == 0 and w_in.shape[1] == 2 * fh
    nf = fh // tf
    return pl.pallas_call(
        functools.partial(_ffn_ple_kernel, nf=nf),
        grid=(t // tm, nf),
        in_specs=[
            pl.BlockSpec((tm, d), lambda i, f: (i, 0)),
            pl.BlockSpec((1, d), lambda i, f: (0, 0)),
            pl.BlockSpec((d, tf), lambda i, f: (0, f)),
            pl.BlockSpec((d, tf), lambda i, f: (0, nf + f)),
            pl.BlockSpec((tf, d), lambda i, f: (f, 0)),
            pl.BlockSpec((1, d), lambda i, f: (0, 0)),
            pl.BlockSpec((tm, pd), lambda i, f: (i, 0)),
            pl.BlockSpec((pd, d), lambda i, f: (0, 0), pipeline_mode=pl.Buffered(1)),
            pl.BlockSpec((d, d), lambda i, f: (0, 0), pipeline_mode=pl.Buffered(1)),
        ],
        out_specs=pl.BlockSpec((tm, d), lambda i, f: (i, 0)),
        out_shape=jax.ShapeDtypeStruct((t, d), F32),
        scratch_shapes=[pltpu.VMEM((tm, d), BF16), pltpu.VMEM((tm, d), F32)],
        compiler_params=_cparams(("parallel", "arbitrary"), FFN_VMEM_LIMIT),
        name="ffn_ple",
    )(h, gpre.reshape(1, d), w_in, w_in, w_out, gpost.reshape(1, d), p, w_ple, w_plegate)


def _gdn_conv_kernel(x_ref, halo_ref, w_ref, o_ref, *, tm):
    c = pl.program_id(0)
    i = pl.program_id(1)
    x = x_ref[...]
    halo = jnp.where(i > 0, halo_ref[...], 0.0)
    ext = jnp.concatenate([halo, x], axis=0)
    w = w_ref[...]
    y = x * w[GDN_CONV - 1:GDN_CONV, :]
    for k in range(1, GDN_CONV):
        shifted = pltpu.roll(ext, k, axis=0)[8:8 + tm]
        y = y + shifted * w[GDN_CONV - 1 - k:GDN_CONV - k, :]
    y = y * _sigmoid(y)
    for hd in range(GDN_CONV_HEADS):
        head = c * GDN_CONV_HEADS + hd
        seg = y[:, hd * LANES:(hd + 1) * LANES]
        normed = seg * lax.rsqrt(jnp.sum(seg * seg, axis=-1, keepdims=True) + L2_EPS)
        q_scale = jnp.where(head < GDN_QK_HEADS, GDN_HEAD_DIM ** -0.5, 1.0)
        o_ref[hd] = jnp.where(head < 2 * GDN_QK_HEADS, normed * q_scale, seg)


def gdn_conv(proj, conv_w, tm=2048):
    t = proj.shape[0]
    n_tiles = conv_w.shape[1] // LANES
    cw = GDN_CONV_HEADS
    assert t % tm == 0 and n_tiles % cw == 0
    return pl.pallas_call(
        functools.partial(_gdn_conv_kernel, tm=tm),
        grid=(n_tiles // cw, t // tm),
        in_specs=[
            pl.BlockSpec((tm, cw * LANES), lambda c, i: (i, c)),
            pl.BlockSpec((8, cw * LANES), lambda c, i: (jnp.maximum(i * (tm // 8) - 1, 0), c)),
            pl.BlockSpec((GDN_CONV, cw * LANES), lambda c, i: (0, c)),
        ],
        out_specs=pl.BlockSpec((cw, tm, LANES), lambda c, i: (c, i, 0)),
        out_shape=jax.ShapeDtypeStruct((n_tiles, t, LANES), F32),
        compiler_params=_cparams(("parallel", "parallel")),
        name="gdn_conv",
    )(proj, proj, conv_w)


def _gdn_gate_kernel(x_ref, alog_ref, dtb_ref, lc_ref, lf_ref, o_ref):
    x = x_ref[...]
    lane = lax.broadcasted_iota(jnp.int32, x.shape, 1)
    beta = _sigmoid(x)
    z = x + dtb_ref[...]
    softplus = jnp.maximum(z, 0.0) + jnp.log(1.0 + jnp.exp(-jnp.abs(z)))
    g = -jnp.exp(alog_ref[...]) * softplus
    gcum = jnp.zeros_like(x)
    gtot = jnp.zeros_like(x)
    for piece in _split3(g):
        gcum = gcum + jnp.dot(lc_ref[...], piece, preferred_element_type=F32)
        gtot = gtot + jnp.dot(lf_ref[...], piece, preferred_element_type=F32)
    o_ref[...] = jnp.where(lane < GDN_V_HEADS, beta, jnp.where(lane < 2 * GDN_V_HEADS, gcum, gtot))


def gdn_gates(ba, alog_vec, dtb_vec):
    t = ba.shape[0]
    tm = GDN_GROUP
    r = np.arange(tm)
    same = (r[:, None] // GDN_CHUNK) == (r[None, :] // GDN_CHUNK)
    lc = jnp.asarray(same & (r[None, :] <= r[:, None]), BF16)
    lf = jnp.asarray(same, BF16)
    return pl.pallas_call(
        _gdn_gate_kernel,
        grid=(t // tm,),
        in_specs=[
            pl.BlockSpec((tm, LANES), lambda i: (i, 0)),
            pl.BlockSpec((1, LANES), lambda i: (0, 0)),
            pl.BlockSpec((1, LANES), lambda i: (0, 0)),
            pl.BlockSpec((tm, tm), lambda i: (0, 0)),
            pl.BlockSpec((tm, tm), lambda i: (0, 0)),
        ],
        out_specs=pl.BlockSpec((tm, LANES), lambda i: (i, 0)),
        out_shape=jax.ShapeDtypeStruct((t, LANES), F32),
        compiler_params=_cparams(("parallel",)),
        name="gdn_gates",
    )(ba, alog_vec, dtb_vec, lc, lf)


def _gdn_scan_kernel(q_ref, k_ref, kt_ref, v_ref, slab_ref, slabt_ref, o_ref, s_ref):
    n = pl.program_id(1)

    @pl.when(n == 0)
    def _():
        s_ref[...] = jnp.zeros_like(s_ref)

    L = GDN_GROUP
    C = GDN_CHUNK
    D = GDN_HEAD_DIM
    slab = slab_ref[...]
    slabt = slabt_ref[...]
    row = lax.broadcasted_iota(jnp.int32, (L, L), 0)
    col = lax.broadcasted_iota(jnp.int32, (L, L), 1)
    same = _div_pow2(row, C) == _div_pow2(col, C)
    causal = same & (col <= row)
    strict = same & (col < row)
    eye = (row == col).astype(F32)
    lane = lax.broadcasted_iota(jnp.int32, (L, LANES), 1)
    sub = lax.broadcasted_iota(jnp.int32, (LANES, L), 0)

    def column(idx):
        return jnp.sum(jnp.where(lane == idx, slab, 0.0), axis=1, keepdims=True)

    def rowvec(idx):
        return jnp.sum(jnp.where(sub == idx, slabt, 0.0), axis=0, keepdims=True)

    colk = lax.broadcasted_iota(jnp.int32, (D, L), 1)
    slots = range(2 * GDN_SCAN_HEADS)

    q = [q_ref[hq] for hq in range(GDN_SCAN_HEADS)]
    k = [k_ref[hq] for hq in range(GDN_SCAN_HEADS)]
    kt = [kt_ref[hq] for hq in range(GDN_SCAN_HEADS)]
    kk = [_mm(k[hq], kt[hq]) for hq in range(GDN_SCAN_HEADS)]
    qk = [_mm(q[hq], kt[hq]) for hq in range(GDN_SCAN_HEADS)]
    beta_c, gc_c, gt_c, decay, bp, inv, kdt = [], [], [], [], [], [], []
    for slot in slots:
        hq = slot // 2
        hv = 2 * (pl.program_id(0) * GDN_SCAN_HEADS + hq) + slot % 2
        beta_c.append(column(hv))
        gc_c.append(column(GDN_V_HEADS + hv))
        gt_c.append(column(2 * GDN_V_HEADS + hv))
        gc_r = rowvec(GDN_V_HEADS + hv)
        gt_r = rowvec(2 * GDN_V_HEADS + hv)
        decay.append(jnp.where(causal, jnp.exp(jnp.where(causal, gc_c[slot] - gc_r, 0.0)), 0.0))
        bp.append(jnp.where(strict, -(kk[hq] * beta_c[slot]) * decay[slot], 0.0))
        inv.append(eye + bp[slot])
        kdt.append(kt[hq] * jnp.exp(gt_r - gc_r))
    for _ in range(5):
        bp = [_mm(bp[slot], bp[slot]) for slot in slots]
        inv = [inv[slot] + _mm(inv[slot], bp[slot]) for slot in slots]
    u, w, qkm, q_dec = [], [], [], []
    for slot in slots:
        hq = slot // 2
        egc = jnp.exp(gc_c[slot])
        rhs = jnp.concatenate([v_ref[slot] * beta_c[slot], k[hq] * (beta_c[slot] * egc)], axis=1)
        sol = _mm(inv[slot], rhs)
        u.append(sol[:, :D])
        w.append(sol[:, D:])
        qkm.append(jnp.where(causal, qk[hq] * decay[slot], 0.0))
        q_dec.append(q[hq] * egc)
    state = [s_ref[slot] for slot in slots]
    v_done = [[] for _ in slots]
    for c in range(L // C):
        lo, hi = c * C, (c + 1) * C
        r = [_mm(jnp.concatenate([w[slot][lo:hi], q_dec[slot][lo:hi]], axis=0), state[slot]) for slot in slots]
        for slot in slots:
            v_done[slot].append(u[slot][lo:hi] - r[slot][:C])
            v_all = jnp.concatenate(v_done[slot] + [jnp.zeros((L - hi, D), F32)] * (hi < L), axis=0)
            kdt_c = jnp.where((colk >= lo) & (colk < hi), kdt[slot], 0.0)
            both = _mm(jnp.concatenate([qkm[slot][lo:hi, :], kdt_c], axis=0), v_all)
            o_ref[lo:hi, slot * D:(slot + 1) * D] = r[slot][C:] + both[:C]
            state[slot] = state[slot] * jnp.exp(gt_c[slot][lo:lo + 1, :]) + both[C:]
    for slot in slots:
        s_ref[slot] = state[slot]


def gdn_scan(qkv_hm, kt_hm, slab, slabt):
    t = qkv_hm.shape[1]
    L = GDN_GROUP
    D = GDN_HEAD_DIM
    hq = GDN_SCAN_HEADS
    assert t % L == 0 and GDN_QK_HEADS % hq == 0
    q_blocks = GDN_QK_HEADS // hq
    return pl.pallas_call(
        _gdn_scan_kernel,
        grid=(q_blocks, t // L),
        in_specs=[
            pl.BlockSpec((hq, L, D), lambda j, n: (j, n, 0)),
            pl.BlockSpec((hq, L, D), lambda j, n: (q_blocks + j, n, 0)),
            pl.BlockSpec((hq, D, L), lambda j, n: (j, 0, n)),
            pl.BlockSpec((2 * hq, L, D), lambda j, n: (q_blocks + j, n, 0)),
            pl.BlockSpec((L, LANES), lambda j, n: (n, 0)),
            pl.BlockSpec((LANES, L), lambda j, n: (0, n)),
        ],
        out_specs=pl.BlockSpec((L, 2 * hq * D), lambda j, n: (n, j)),
        out_shape=jax.ShapeDtypeStruct((t, GDN_V_HEADS * D), F32),
        scratch_shapes=[pltpu.VMEM((2 * hq, D, D), F32)],
        compiler_params=_cparams(("parallel", "arbitrary")),
        name="gdn_scan",
    )(qkv_hm, qkv_hm, kt_hm, qkv_hm, slab, slabt)


def _gdn_out_kernel(o_ref, z_ref, onorm_ref, w_ref, gpost_ref, h_ref, out_ref):
    o = o_ref[...]
    z = z_ref[...]
    parts = []
    for hd in range(GDN_V_HEADS):
        seg = o[:, hd * GDN_HEAD_DIM:(hd + 1) * GDN_HEAD_DIM]
        parts.append(seg * lax.rsqrt(jnp.mean(seg * seg, axis=-1, keepdims=True) + NORM_EPS))
    gated = jnp.concatenate(parts, axis=1) * onorm_ref[...] * (z * _sigmoid(z))
    mix = jnp.dot(gated.astype(BF16), w_ref[...], preferred_element_type=F32)
    out_ref[...] = h_ref[...] + _rms(mix, gpost_ref[...])


def gdn_out(o, proj, onorm_tiled, w_out, gpost, h, tm=512):
    t, vw = o.shape
    d = h.shape[1]
    z_blk = (proj.shape[1] - vw) // vw
    assert proj.shape[1] % vw == 0 and t % tm == 0
    return pl.pallas_call(
        _gdn_out_kernel,
        grid=(t // tm,),
        in_specs=[
            pl.BlockSpec((tm, vw), lambda i: (i, 0)),
            pl.BlockSpec((tm, vw), lambda i: (i, z_blk)),
            pl.BlockSpec((1, vw), lambda i: (0, 0)),
            pl.BlockSpec((vw, d), lambda i: (0, 0)),
            pl.BlockSpec((1, d), lambda i: (0, 0)),
            pl.BlockSpec((tm, d), lambda i: (i, 0)),
        ],
        out_specs=pl.BlockSpec((tm, d), lambda i: (i, 0)),
        out_shape=jax.ShapeDtypeStruct((t, d), F32),
        compiler_params=_cparams(("parallel",)),
        name="gdn_out",
    )(o, proj, onorm_tiled, w_out, gpost.reshape(1, d), h)


def _compress_kernel(x_ref, pos_ref, w1_ref, w2_ref, o_ref):
    x = x_ref[...]
    pos = pos_ref[...]
    nc, half = x.shape
    w1 = w1_ref[...]
    first = _mm(x + pos[0:1, :], w1[:half])
    second = _mm(x + pos[1:2, :], w1[half:])
    hid = first + pltpu.roll(second, nc - 1, axis=0)
    hid = hid * _sigmoid(hid)
    out = jnp.dot(hid.astype(BF16), w2_ref[...], preferred_element_type=F32)
    rowi = lax.broadcasted_iota(jnp.int32, out.shape, 0)
    o_ref[...] = jnp.where(rowi < nc - 1, out, 0.0)


def compress(x2, pos2, w1, w2):
    _, g, nc, wdt = x2.shape
    hid = w1.shape[2]
    dh = w2.shape[2]
    return pl.pallas_call(
        _compress_kernel,
        grid=(2, g),
        in_specs=[
            pl.BlockSpec((None, None, nc, wdt), lambda b, gi: (b, gi, 0, 0)),
            pl.BlockSpec((None, 2, wdt), lambda b, gi: (b, 0, 0)),
            pl.BlockSpec((None, 2 * wdt, hid), lambda b, gi: (b, 0, 0)),
            pl.BlockSpec((None, hid, dh), lambda b, gi: (b, 0, 0)),
        ],
        out_specs=pl.BlockSpec((None, None, nc, dh), lambda b, gi: (b, gi, 0, 0)),
        out_shape=jax.ShapeDtypeStruct((2, g, nc, dh), F32),
        compiler_params=_cparams(("parallel", "parallel")),
        name="nsa_compress",
    )(x2, pos2, w1, w2)


def _cmp_topk_kernel(q_ref, kbd_ref, vt_ref, m_ref, *refs, tq, nc, nsel, topk, q0):
    oc_ref, sel_ref, s_ref, p_ref, psum_ref = refs[-5:]
    i = q0 + pl.program_id(1)
    dh = NSA_HEAD_DIM
    rows_per = min(nc, CMP_ROWS)
    q = (q_ref[...] * ((dh ** -0.5) * LOG2_E)).astype(BF16)
    subs = [slice(c0, c0 + CMP_SUB) for c0 in range(0, tq, CMP_SUB)]

    def scores(cols):
        s_ref[:, cols] = jnp.dot(kbd_ref[...], q[:, cols], preferred_element_type=F32)

    def softmax(cols):
        for c0 in range(cols.start, cols.stop, LANES):
            lanes = slice(c0, c0 + LANES)
            tpos = i * tq + c0 + lax.broadcasted_iota(jnp.int32, (rows_per, LANES), 1)
            cblk0 = lax.broadcasted_iota(jnp.int32, (rows_per, LANES), 0)
            masks = [None if CMP_STRIDE * ((c + 1) * rows_per - 1) + CMP_BLOCK - 1 <= q0 * tq
                     else (CMP_STRIDE * (cblk0 + c * rows_per) + CMP_BLOCK - 1) <= tpos
                     for c in range(nc // rows_per)]
            visible = lambda rows, mask: s_ref[rows, lanes] if mask is None else jnp.where(mask, s_ref[rows, lanes],
                                                                                            -jnp.inf)
            for r in range(NSA_REP):
                pieces = [slice(r * nc + c * rows_per, r * nc + (c + 1) * rows_per) for c in range(nc // rows_per)]
                m = jnp.full((1, LANES), -jnp.inf, F32)
                for rows, mask in zip(pieces, masks):
                    m = jnp.maximum(m, jnp.max(visible(rows, mask), axis=0, keepdims=True))
                m = jnp.where(m > -jnp.inf, m, 0.0)
                total = jnp.zeros((1, LANES), F32)
                for rows, mask in zip(pieces, masks):
                    e = jnp.exp2(visible(rows, mask) - m)
                    s_ref[rows, lanes] = e
                    total = total + jnp.sum(e, axis=0, keepdims=True)
                inv = 1.0 / jnp.maximum(total, 1e-30)
                for c, rows in enumerate(pieces):
                    p = s_ref[rows, lanes] * inv
                    p_ref[rows, lanes] = p.astype(BF16)
                    prow = slice(c * rows_per, (c + 1) * rows_per)
                    psum_ref[prow, lanes] = p if r == 0 else psum_ref[prow, lanes] + p

    def outputs(cols):
        oc_t = jnp.dot(vt_ref[...], p_ref[:, cols], preferred_element_type=F32)
        imp = jnp.zeros((nsel, CMP_SUB), F32)
        for piece in _split3(psum_ref[:, cols]):
            imp = imp + jnp.dot(m_ref[...], piece, preferred_element_type=F32)
        return oc_t, imp

    def select(cols, oc_t, imp):
        for c0 in range(0, CMP_SUB, LANES):
            local = slice(c0, c0 + LANES)
            lanes = slice(cols.start + c0, cols.start + c0 + LANES)
            oc_ref[lanes, :] = oc_t[:, local].T
            t1 = i * tq + cols.start + c0 + lax.broadcasted_iota(jnp.int32, (nsel, LANES), 1)
            blk = lax.broadcasted_iota(jnp.int32, (nsel, LANES), 0)
            cur = _div_pow2(t1, SEL_BLOCK)
            forced = (blk == 0) | (blk == cur) | (blk == cur - 1)
            valid = blk * SEL_BLOCK <= t1
            score = jnp.where(valid, jnp.where(forced, FORCED_SCORE, imp[:, local]), -jnp.inf)
            blkf = blk.astype(F32)
            work = score
            for _ in range(topk):
                mx = jnp.max(work, axis=0, keepdims=True)
                first = jnp.min(jnp.where(work == mx, blkf, float(nsel)), axis=0, keepdims=True)
                work = jnp.where(blkf == first, -jnp.inf, work)
            sel_ref[:nsel, lanes] = jnp.where((score > -jnp.inf) & (work == -jnp.inf), 1.0, 0.0).astype(sel_ref.dtype)
            if sel_ref.shape[0] > nsel:
                sel_ref[nsel:, lanes] = jnp.zeros((sel_ref.shape[0] - nsel, LANES), sel_ref.dtype)

    for cols in subs:
        scores(cols)
    results = []
    for cols in subs:
        softmax(cols)
        results.append(outputs(cols))
    for cols, (oc_t, imp) in zip(subs, results):
        select(cols, oc_t, imp)


def cmp_topk(q_t, kbd, v_t, imp_mat_t, nsel_all, topk, q0, nq, tq, prev):
    qw, t = q_t.shape
    g = kbd.shape[0]
    nc = kbd.shape[1] // NSA_REP
    nsel = imp_mat_t.shape[0]
    assert nc % min(nc, CMP_ROWS) == 0
    in_specs = [
        pl.BlockSpec((NSA_GW, tq), lambda gi, i: (gi, q0 + i)),
        pl.BlockSpec((None, NSA_REP * nc, NSA_GW), lambda gi, i: (gi, 0, 0)),
        pl.BlockSpec((None, NSA_GW, NSA_REP * nc), lambda gi, i: (gi, 0, 0)),
        pl.BlockSpec((nsel, nc), lambda gi, i: (0, 0)),
    ]
    in_specs += [pl.BlockSpec(memory_space=pl.ANY)] * 2
    args = [q_t, kbd, v_t, imp_mat_t, *prev]
    aliases = {len(args) - 2: 0, len(args) - 1: 1}
    return pl.pallas_call(
        functools.partial(_cmp_topk_kernel, tq=tq, nc=nc, nsel=nsel, topk=topk, q0=q0),
        grid=(g, nq),
        in_specs=in_specs,
        out_specs=[
            pl.BlockSpec((tq, NSA_GW), lambda gi, i: (q0 + i, gi)),
            pl.BlockSpec((None, nsel_all, tq), lambda gi, i: (gi, 0, q0 + i)),
        ],
        out_shape=[
            jax.ShapeDtypeStruct((t, qw), F32),
            jax.ShapeDtypeStruct((g, nsel_all, t), F32),
        ],
        input_output_aliases=aliases,
        scratch_shapes=[
            pltpu.VMEM((NSA_REP * nc, tq), F32),
            pltpu.VMEM((NSA_REP * nc, tq), BF16),
            pltpu.VMEM((nc, tq), F32),
        ],
        compiler_params=_cparams(("parallel", "parallel")),
        name="nsa_cmp_topk",
    )(*args)


def _flash_kernel(q_ref, k_ref, v_ref, sel_ref, o_ref,
                  m_ref, l_ref, alpha_ref, acc_ref, s_ref, p_ref, bias_ref, kbd_ref, vt_ref, qs_ref, *, tq, kt):
    qi = pl.program_id(1)
    dh = NSA_HEAD_DIM
    n_sub = tq // FLASH_SUB
    n_blk = kt // SEL_BLOCK

    m_ref[...] = jnp.full_like(m_ref, NEG_INIT)
    l_ref[...] = jnp.zeros_like(l_ref)
    acc_ref[...] = jnp.zeros_like(acc_ref)
    qs_ref[...] = (q_ref[...] * ((dh ** -0.5) * LOG2_E)).astype(BF16)
    vt_ref[dh:, :] = jnp.ones((FLASH_SUM_ROWS, kt), BF16)

    def key_tile(ki, positional, first_sub=0):
        k4 = k_ref[ki]
        kseg = _div_pow2(lax.broadcasted_iota(jnp.int32, k4.shape, 1), dh)
        for r in range(NSA_REP):
            kbd_ref[r * kt:(r + 1) * kt, :] = jnp.where(kseg == r, k4, jnp.zeros_like(k4))
        vt_ref[:dh, :] = v_ref[ki]

        def scores(sub):
            c0 = sub * FLASH_SUB
            cols = slice(c0, c0 + FLASH_SUB)
            s_ref[:, cols] = jnp.dot(kbd_ref[...], qs_ref[:, cols],
                                     preferred_element_type=F32)
            if not positional:
                return
            for jb in range(n_blk):
                rows = slice(jb * SEL_BLOCK, (jb + 1) * SEL_BLOCK)
                picked = sel_ref[pl.ds(ki * n_blk + jb, 1), cols] > 0.5
                tpos = qi * tq + c0 + lax.broadcasted_iota(jnp.int32, (SEL_BLOCK, FLASH_SUB), 1)
                kpos = ki * kt + jb * SEL_BLOCK + lax.broadcasted_iota(jnp.int32, (SEL_BLOCK, FLASH_SUB), 0)
                bias_ref[rows, cols] = jnp.where(picked & (kpos <= tpos), 0.0, -jnp.inf)

        scores(first_sub)
        for sub in range(first_sub, n_sub):
            c0 = sub * FLASH_SUB
            cols = slice(c0, c0 + FLASH_SUB)
            if sub + 1 < n_sub:
                scores(sub + 1)
            if not positional:
                picked_sub = [sel_ref[pl.ds(ki * n_blk + jb, 1), cols] > 0.5 for jb in range(n_blk)]
            for ch in range(FLASH_SUB // LANES):
                lanes = slice(c0 + ch * LANES, c0 + (ch + 1) * LANES)
                if positional:
                    bias = bias_ref[:, lanes]
                else:
                    picked = [pk[:, ch * LANES:(ch + 1) * LANES] for pk in picked_sub]
                for r in range(NSA_REP):
                    if positional:
                        x = s_ref[r * kt:(r + 1) * kt, lanes] + bias
                    else:
                        x = jnp.concatenate(
                            [jnp.where(picked[jb], s_ref[r * kt + jb * SEL_BLOCK:r * kt + (jb + 1) * SEL_BLOCK, lanes],
                                       -jnp.inf) for jb in range(n_blk)], axis=0)
                    m_prev = m_ref[r:r + 1, lanes]
                    m_new = jnp.maximum(m_prev, jnp.max(x, axis=0, keepdims=True))
                    m_ref[r:r + 1, lanes] = m_new
                    alpha_ref[r:r + 1, lanes] = jnp.exp2(m_prev - m_new)
                    p_ref[r * kt:(r + 1) * kt, lanes] = jnp.exp2(x - m_new).astype(BF16)
            for r in range(NSA_REP):
                pv = jnp.dot(vt_ref[...], p_ref[r * kt:(r + 1) * kt, cols],
                             preferred_element_type=F32)
                hd = slice(r * dh, (r + 1) * dh)
                alpha = alpha_ref[r:r + 1, cols]
                acc_ref[hd, cols] = acc_ref[hd, cols] * alpha + pv[:dh]
                l_ref[r:r + 1, cols] = l_ref[r:r + 1, cols] * alpha + pv[dh:dh + 1]

    def before_diagonal(ki, carry):
        key_tile(ki, False)
        return carry

    diag = qi * (tq // kt)
    lax.fori_loop(0, diag, before_diagonal, 0)
    for d in range(tq // kt):
        def on_diagonal(ki, carry, first_sub=d * kt // FLASH_SUB):
            key_tile(ki, True, first_sub=first_sub)
            return carry
        lax.fori_loop(diag + d, diag + d + 1, on_diagonal, 0)

    for ch in range(tq // LANES):
        lanes = slice(ch * LANES, (ch + 1) * LANES)
        out_t = jnp.concatenate(
            [acc_ref[r * dh:(r + 1) * dh, lanes] / l_ref[r:r + 1, lanes] for r in range(NSA_REP)], axis=0)
        o_ref[lanes, :] = out_t.T


def flash_branch(q_t, k4, v_t, sel_t, tq, kt):
    qw, t = q_t.shape
    g, nk = k4.shape[:2]
    dh = v_t.shape[2]
    nsel = sel_t.shape[1]
    assert t % tq == 0 and tq % kt == 0 and kt % SEL_BLOCK == 0 and tq % FLASH_SUB == 0 and kt % FLASH_SUB == 0
    return pl.pallas_call(
        functools.partial(_flash_kernel, tq=tq, kt=kt),
        grid=(g, t // tq),
        in_specs=[
            pl.BlockSpec((NSA_GW, tq), lambda gi, i: (gi, i)),
            pl.BlockSpec((None, nk, kt, NSA_GW), lambda gi, i: (gi, 0, 0, 0), pipeline_mode=pl.Buffered(1)),
            pl.BlockSpec((None, nk, dh, kt), lambda gi, i: (gi, 0, 0, 0), pipeline_mode=pl.Buffered(1)),
            pl.BlockSpec((None, nsel, tq), lambda gi, i: (gi, 0, i)),
        ],
        out_specs=pl.BlockSpec((tq, NSA_GW), lambda gi, i: (i, gi)),
        scratch_shapes=[
            pltpu.VMEM((8, tq), F32),
            pltpu.VMEM((8, tq), F32),
            pltpu.VMEM((8, tq), F32),
            pltpu.VMEM((NSA_GW, tq), F32),
            pltpu.VMEM((NSA_REP * kt, tq), F32),
            pltpu.VMEM((NSA_REP * kt, tq), BF16),
            pltpu.VMEM((kt, tq), F32),
            pltpu.VMEM((NSA_REP * kt, NSA_GW), BF16),
            pltpu.VMEM((dh + FLASH_SUM_ROWS, kt), BF16),
            pltpu.VMEM((NSA_GW, tq), BF16),
        ],
        out_shape=jax.ShapeDtypeStruct((t, qw), F32),
        compiler_params=_cparams(("parallel", "arbitrary")),
        name="nsa_selected",
    )(q_t, k4, v_t, sel_t)


def _window_kernel(q_ref, k_ref, v_ref, o_ref, s_ref, p_ref, bias_ref, kbd_ref, vt_ref, *, tq, kt):
    dh = NSA_HEAD_DIM
    nw = WINDOW // kt + 1
    span = nw * kt
    n_tiles = tq // kt
    qs = (q_ref[...] * ((dh ** -0.5) * LOG2_E)).astype(BF16)

    def stage(h):
        i = pl.program_id(1) * n_tiles + h
        for w in range(nw):
            tile = i - (nw - 1) + w
            k4 = k_ref[jnp.maximum(tile, 0)]
            kseg = _div_pow2(lax.broadcasted_iota(jnp.int32, k4.shape, 1), dh)
            for r in range(NSA_REP):
                at = r * span + w * kt
                kbd_ref[h, at:at + kt, :] = jnp.where(kseg == r, k4, jnp.zeros_like(k4))
            vt_ref[h, :dh, w * kt:(w + 1) * kt] = v_ref[jnp.maximum(tile, 0)]
            tpos = i * kt + lax.broadcasted_iota(jnp.int32, (kt, kt), 1)
            kpos = tile * kt + lax.broadcasted_iota(jnp.int32, (kt, kt), 0)
            allowed = (kpos >= 0) & (kpos <= tpos) & (kpos > tpos - WINDOW)
            bias_ref[h, w * kt:(w + 1) * kt, :] = jnp.where(allowed, 0.0, -jnp.inf)
        vt_ref[h, dh:, :] = jnp.ones((FLASH_SUM_ROWS, span), BF16)

    def scores(h):
        s_ref[h] = jnp.dot(kbd_ref[h], qs[:, h * kt:(h + 1) * kt], preferred_element_type=F32)

    def softmax(h):
        for ch in range(kt // LANES):
            lanes = slice(ch * LANES, (ch + 1) * LANES)
            for r in range(NSA_REP):
                m = jnp.full((1, LANES), -jnp.inf, F32)
                for w in range(nw):
                    rows = slice(r * span + w * kt, r * span + (w + 1) * kt)
                    m = jnp.maximum(m, jnp.max(s_ref[h, rows, lanes] + bias_ref[h, w * kt:(w + 1) * kt, lanes],
                                               axis=0, keepdims=True))
                m = jnp.where(m > -jnp.inf, m, 0.0)
                for w in range(nw):
                    rows = slice(r * span + w * kt, r * span + (w + 1) * kt)
                    x = s_ref[h, rows, lanes] + bias_ref[h, w * kt:(w + 1) * kt, lanes]
                    p_ref[h, rows, lanes] = jnp.exp2(x - m).astype(BF16)

    def weighted(h):
        return [jnp.dot(vt_ref[h], p_ref[h, r * span:(r + 1) * span, :], preferred_element_type=F32)
                for r in range(NSA_REP)]

    def store(h, pv):
        for ch in range(kt // LANES):
            lanes = slice(ch * LANES, (ch + 1) * LANES)
            out_t = jnp.concatenate(
                [pv[r][:dh, lanes] / jnp.maximum(pv[r][dh:dh + 1, lanes], 1e-30) for r in range(NSA_REP)], axis=0)
            o_ref[h * kt + ch * LANES:h * kt + (ch + 1) * LANES, :] = out_t.T

    for h in range(n_tiles):
        stage(h)
        scores(h)
    pvs = []
    for h in range(n_tiles):
        softmax(h)
        pvs.append(weighted(h))
    for h in range(n_tiles):
        store(h, pvs[h])


def window_branch(q_t, k4, v_t, tq):
    qw, t = q_t.shape
    g, nk, kt, _ = k4.shape
    dh = v_t.shape[2]
    assert tq % kt == 0 and WINDOW % kt == 0 and t % tq == 0
    n_tiles = tq // kt
    span = (WINDOW // kt + 1) * kt
    return pl.pallas_call(
        functools.partial(_window_kernel, tq=tq, kt=kt),
        grid=(g, t // tq),
        in_specs=[
            pl.BlockSpec((NSA_GW, tq), lambda gi, i: (gi, i)),
            pl.BlockSpec((None, nk, kt, NSA_GW), lambda gi, i: (gi, 0, 0, 0)),
            pl.BlockSpec((None, nk, dh, kt), lambda gi, i: (gi, 0, 0, 0)),
        ],
        out_specs=pl.BlockSpec((tq, NSA_GW), lambda gi, i: (i, gi)),
        scratch_shapes=[
            pltpu.VMEM((n_tiles, NSA_REP * span, kt), F32),
            pltpu.VMEM((n_tiles, NSA_REP * span, kt), BF16),
            pltpu.VMEM((n_tiles, span, kt), F32),
            pltpu.VMEM((n_tiles, NSA_REP * span, NSA_GW), BF16),
            pltpu.VMEM((n_tiles, dh + FLASH_SUM_ROWS, span), BF16),
        ],
        out_shape=jax.ShapeDtypeStruct((t, qw), F32),
        compiler_params=_cparams(("parallel", "parallel")),
        name="nsa_window",
    )(q_t, k4, v_t)


def _nsa_out_kernel(oc_ref, os_ref, ow_ref, gl_ref, eg_ref, w_ref, gpost_ref, h_ref, out_ref):
    pieces = _split3(_sigmoid(gl_ref[...]))
    mixed = jnp.zeros(oc_ref.shape, F32)
    for b, br_ref in enumerate((oc_ref, os_ref, ow_ref)):
        gfull = jnp.zeros(oc_ref.shape, F32)
        for piece in pieces:
            gfull = gfull + jnp.dot(piece, eg_ref[b], preferred_element_type=F32)
        mixed = mixed + gfull * br_ref[...]
    mix = jnp.dot(mixed.astype(BF16), w_ref[...], preferred_element_type=F32)
    out_ref[...] = h_ref[...] + _rms(mix, gpost_ref[...])


def nsa_out(oc, osel, ow, gate_logits, expand, w_o, gpost, h, tm=512):
    t, qw = oc.shape
    d = h.shape[1]
    assert t % tm == 0
    row = lambda w: pl.BlockSpec((tm, w), lambda i: (i, 0))
    return pl.pallas_call(
        _nsa_out_kernel,
        grid=(t // tm,),
        in_specs=[
            row(qw), row(qw), row(qw), row(LANES),
            pl.BlockSpec((3, LANES, qw), lambda i: (0, 0, 0)),
            pl.BlockSpec((qw, d), lambda i: (0, 0)),
            pl.BlockSpec((1, d), lambda i: (0, 0)),
            row(d),
        ],
        out_specs=row(d),
        out_shape=jax.ShapeDtypeStruct((t, d), F32),
        compiler_params=_cparams(("parallel",)),
        name="nsa_out",
    )(oc, osel, ow, gate_logits, expand, w_o, gpost.reshape(1, d), h)


def _pad_cols(w, n):
    return jnp.pad(w, ((0, 0), (0, n - w.shape[1])))


def _importance_matrix(nc, nsel):
    r = SEL_BLOCK // CMP_STRIDE
    c = CMP_BLOCK // CMP_STRIDE
    mat = np.zeros((nc, nsel), np.float32)
    for kblk in range(nsel):
        for m in range(r):
            for n in range(c):
                j = r * kblk + m - n
                if 0 <= j < nc - 1:
                    mat[j, kblk] += 1.0
    return jnp.asarray(mat, BF16)


def _gate_expand():
    e = np.zeros((3, LANES, NSA_GROUPS * NSA_GW), np.float32)
    for head in range(NSA_GROUPS * NSA_REP):
        for b in range(3):
            e[b, head * 3 + b, head * NSA_HEAD_DIM:(head + 1) * NSA_HEAD_DIM] = 1.0
    return jnp.asarray(e, BF16)


def _block_diag_kv(a, b):
    g, nc, dh = a.shape
    eye = jnp.eye(NSA_REP, dtype=bool)
    a_t = jnp.swapaxes(a, 1, 2)
    a_bd = jnp.where(eye[None, :, None, :, None], a_t[:, None, :, None, :], 0.0)
    b_bd = jnp.where(eye[None, :, None, :, None], b[:, None, :, None, :], 0.0)
    return (a_bd.reshape(g, NSA_REP * dh, NSA_REP * nc).astype(BF16),
            b_bd.reshape(g, NSA_REP * nc, NSA_REP * dh).astype(BF16))


def kernel(x, p, mix_pre_norm, mix_post_norm, ffn_pre_norm, ffn_post_norm, gdn_w_in, gdn_conv_w, gdn_a_log,
           gdn_dt_bias, gdn_o_norm, gdn_w_out, kv_norm, kv_w, cmp_pos, cmp_w1, cmp_w2, nsa_w_qg, nsa_w_o,
           ffn_w_in, ffn_w_out, ple_w_in, ple_w_gate):
    depth = p.shape[0]
    n_a = gdn_w_in.shape[0]
    t = x.shape[1]
    h = x[0]
    conv_w_cols = gdn_conv_w.shape[2]
    vw = GDN_V_HEADS * GDN_HEAD_DIM
    main_w = conv_w_cols + vw

    def channel_and_ple(h, i):
        return ffn_ple(h, ffn_pre_norm[i], ffn_w_in[i].astype(BF16), ffn_w_out[i].astype(BF16),
                       ffn_post_norm[i], p[i, 0], ple_w_in[i].astype(BF16), ple_w_gate[i].astype(BF16))

    for i in range(n_a):
        w_in = gdn_w_in[i]
        w_beta = w_in[:, main_w:main_w + GDN_V_HEADS]
        w_a = w_in[:, main_w + GDN_V_HEADS:]
        w_small = _pad_cols(jnp.concatenate([w_beta, w_a, w_a], axis=1), LANES).astype(BF16)
        proj = norm_matmul(h, mix_pre_norm[i], w_in.astype(BF16), n=main_w)
        ba = norm_matmul(h, mix_pre_norm[i], w_small)
        pad_vec = lambda v: jnp.pad(v, (GDN_V_HEADS, LANES - 2 * GDN_V_HEADS))
        alog_vec = (pad_vec(gdn_a_log[i]) + jnp.pad(gdn_a_log[i], (2 * GDN_V_HEADS, LANES - 3 * GDN_V_HEADS)))
        dtb_vec = (pad_vec(gdn_dt_bias[i]) + jnp.pad(gdn_dt_bias[i], (2 * GDN_V_HEADS, LANES - 3 * GDN_V_HEADS)))
        slab = gdn_gates(ba, alog_vec.reshape(1, LANES), dtb_vec.reshape(1, LANES))
        qkv_hm = gdn_conv(proj, gdn_conv_w[i])
        kt_hm = jnp.swapaxes(qkv_hm[GDN_QK_HEADS:2 * GDN_QK_HEADS], 1, 2)
        o = gdn_scan(qkv_hm, kt_hm, slab, slab.T)
        onorm_tiled = jnp.tile(gdn_o_norm[i], GDN_V_HEADS).reshape(1, vw)
        h = gdn_out(o, proj, onorm_tiled, gdn_w_out[i].astype(BF16), mix_post_norm[i], h)
        h = channel_and_ple(h, i)

    g = NSA_GROUPS
    dh = NSA_HEAD_DIM
    kv = norm_matmul(h, kv_norm, kv_w.astype(BF16), tn=768)
    kv6 = jnp.transpose(kv.reshape(t, 6, g, dh), (1, 2, 0, 3))
    nc = t // CMP_STRIDE
    nsel = t // SEL_BLOCK
    x2 = kv6[0:2].reshape(2, g, nc, CMP_STRIDE * dh)
    pos2 = cmp_pos.reshape(2, 2, CMP_STRIDE * dh)
    cmp_out = compress(x2, pos2, cmp_w1.astype(BF16), cmp_w2.astype(BF16))
    imp_mat_t = _importance_matrix(nc, nsel).T
    nq_r = t // (CMP_TQ * CMP_RANGES)
    assert t % (CMP_TQ * CMP_RANGES) == 0 and nc % CMP_RANGES == 0 and nsel % (8 * CMP_RANGES) == 0
    cmp_ranges = []
    for rg in range(CMP_RANGES):
        nc_r, nsel_r = (rg + 1) * nc // CMP_RANGES, (rg + 1) * nsel // CMP_RANGES
        vbd_t, kbd = _block_diag_kv(cmp_out[1][:, :nc_r], cmp_out[0][:, :nc_r])
        cmp_ranges.append((kbd, vbd_t, imp_mat_t[:nsel_r, :nc_r]))
    kt = FLASH_KT
    tiles = lambda a: a.reshape(g, t // kt, kt, dh)
    rep_k = lambda a: jnp.tile(tiles(a), (1, 1, 1, NSA_REP)).astype(BF16)
    rep_t = lambda a: jnp.swapaxes(tiles(a), 2, 3).astype(BF16)
    k_slc, v_slc_t = rep_k(kv6[2]), rep_t(kv6[3])
    k_win, v_win_t = rep_k(kv6[4]), rep_t(kv6[5])
    expand = _gate_expand()

    for i in range(n_a, depth):
        j = i - n_a
        qw = g * NSA_GW
        w_qg = nsa_w_qg[j]
        q_t = norm_matmul_t(h, mix_pre_norm[i], w_qg[:, :qw].T.astype(BF16))
        gate_logits = norm_matmul(h, mix_pre_norm[i], _pad_cols(w_qg[:, qw:], LANES).astype(BF16))
        cmp_res = (jnp.zeros((t, qw), F32), jnp.zeros((g, nsel, t), F32))
        for rg, (kbd, vbd_t, imp_r) in enumerate(cmp_ranges):
            cmp_res = cmp_topk(q_t, kbd, vbd_t, imp_r, nsel, min(SEL_TOPK, nsel), rg * nq_r, nq_r, CMP_TQ,
                               prev=cmp_res)
        o_c, sel_t = cmp_res
        o_s = flash_branch(q_t, k_slc, v_slc_t, sel_t, tq=min(FLASH_TQ, t), kt=kt)
        o_w = window_branch(q_t, k_win, v_win_t, tq=WINDOW_TQ)
        h = nsa_out(o_c, o_s, o_w, gate_logits, expand, nsa_w_o[j].astype(BF16), mix_post_norm[i], h)
        h = channel_and_ple(h, i)
    return h[None]
```
